```python
import jax, jax.numpy as jnp
from jax import lax
import numpy as np

D_MODEL = 1024
BATCH = 2
SEQ = 8192
DEPTH = 1

N_HEADS = 8
HEAD_DK = 128
HEAD_DV = 128
KEY_DIM = N_HEADS * HEAD_DK
VAL_DIM = N_HEADS * HEAD_DV
QKV_CONV = 4
CHUNK = 64
SC_DIM = D_MODEL
SC_CONV = 3
N_GROUPS = 4
EXPERTS_PER_GROUP = 8
N_EXPERTS = N_GROUPS * EXPERTS_PER_GROUP
TOP_K = 2
D_EXPERT = 512
MOE_BLOCK = 128
N_MOD = 6
EPS = 1e-6
IN_WIDTHS = (2 * KEY_DIM + VAL_DIM, VAL_DIM, N_HEADS, N_HEADS, SC_DIM, SC_DIM, SC_DIM, D_MODEL, D_MODEL)
IN_TOTAL = 2 * KEY_DIM + 2 * VAL_DIM + 2 * N_HEADS + 3 * SC_DIM + 2 * D_MODEL

kernel_name = "hybrid_gdn_shortconv_hmoe_adaln"


def rmsnorm(x, g):
    xf = x.astype(jnp.float32)
    y = xf * lax.rsqrt(jnp.mean(xf * xf, axis=-1, keepdims=True) + EPS)
    return (y * g.astype(jnp.float32)).astype(x.dtype)


def l2norm(x):
    return x * lax.rsqrt(jnp.sum(x * x, axis=-1, keepdims=True) + EPS)


def causal_depthwise_conv(x, w):
    k_w, ch = w.shape
    return lax.conv_general_dilated(
        x, w[:, None, :].astype(x.dtype), window_strides=(1,), padding=((k_w - 1, 0),),
        dimension_numbers=("NWC", "WIO", "NWC"), feature_group_count=ch)


def split_cols(y):
    idx = np.cumsum(np.array(IN_WIDTHS))[:-1].tolist()
    return jnp.split(y, idx, axis=-1)


def gated_delta_rule_chunked(q, k, v, g, beta):
    bsz, t_len, h, dk = q.shape
    dv = v.shape[-1]
    n = t_len // CHUNK
    q = l2norm(q) * (dk ** -0.5)
    k = l2norm(k)

    def to_chunks(t):
        return t.reshape(bsz, n, CHUNK, h, -1).transpose(0, 3, 1, 2, 4)

    q, k, v = to_chunks(q), to_chunks(k), to_chunks(v)
    g = jnp.cumsum(g.reshape(bsz, n, CHUNK, h).transpose(0, 3, 1, 2), axis=-1)
    beta = beta.reshape(bsz, n, CHUNK, h).transpose(0, 3, 1, 2)[..., None]
    causal = jnp.tril(jnp.ones((CHUNK, CHUNK), bool))
    strict = jnp.tril(jnp.ones((CHUNK, CHUNK), bool), k=-1)
    diff = g[..., :, None] - g[..., None, :]
    decay = jnp.where(causal, jnp.exp(jnp.where(causal, diff, 0.0)), 0.0)
    k_beta = k * beta
    a = jnp.where(strict, jnp.einsum('bhnik,bhnjk->bhnij', k_beta, k) * decay, 0.0)
    eye = jnp.eye(CHUNK, dtype=a.dtype)
    t_inv = lax.linalg.triangular_solve(eye + a, jnp.broadcast_to(eye, a.shape),
                                        left_side=True, lower=True, unit_diagonal=True)
    u = jnp.einsum('bhnij,bhnjv->bhniv', t_inv, v * beta)
    w = jnp.einsum('bhnij,bhnjk->bhnik', t_inv, k_beta * jnp.exp(g)[..., None])
    intra = jnp.einsum('bhnik,bhnjk->bhnij', q, k) * decay
    q_dec = q * jnp.exp(g)[..., None]
    g_last = g[..., -1]
    k_dec = k * jnp.exp(g_last[..., None] - g)[..., None]

    def step(state, inp):
        u_n, w_n, q_n, intra_n, k_n, dec_n = inp
        v_new = u_n - jnp.einsum('bhck,bhkv->bhcv', w_n, state)
        o_n = jnp.einsum('bhck,bhkv->bhcv', q_n, state) + jnp.einsum('bhcs,bhsv->bhcv', intra_n, v_new)
        state = state * dec_n[..., None, None] + jnp.einsum('bhck,bhcv->bhkv', k_n, v_new)
        return state, o_n

    xs = tuple(jnp.moveaxis(t, 2, 0) for t in (u, w, q_dec, intra, k_dec, jnp.exp(g_last)))
    s0 = jnp.zeros((bsz, h, dk, dv), jnp.float32)
    _, o = lax.scan(step, s0, xs)
    return o.transpose(1, 0, 3, 2, 4).reshape(bsz, t_len, h, dv)


def hybrid_mixer(h, w_in, conv_qkv_w, a_log, dt_bias, onorm_g, w_proj_a, conv_sc_w, w_proj_b, w_out):
    bsz, t_len, _ = h.shape
    qkv, z, b_lg, a_lg, s_b, s_c, s_x, g_a, g_b = split_cols(h @ w_in)
    qkv = jax.nn.silu(causal_depthwise_conv(qkv, conv_qkv_w)).astype(jnp.float32)
    q, k, v = jnp.split(qkv, [KEY_DIM, 2 * KEY_DIM], axis=-1)
    q = q.reshape(bsz, t_len, N_HEADS, HEAD_DK)
    k = k.reshape(bsz, t_len, N_HEADS, HEAD_DK)
    v = v.reshape(bsz, t_len, N_HEADS, HEAD_DV)
    beta = jax.nn.sigmoid(b_lg.astype(jnp.float32))
    g = -jnp.exp(a_log.astype(jnp.float32)) * jax.nn.softplus(a_lg.astype(jnp.float32) + dt_bias.astype(jnp.float32))
    o = gated_delta_rule_chunked(q, k, v, g, beta)
    o = o * lax.rsqrt(jnp.mean(o * o, axis=-1, keepdims=True) + EPS) * onorm_g.astype(jnp.float32)
    o = o * jax.nn.silu(z.astype(jnp.float32).reshape(bsz, t_len, N_HEADS, HEAD_DV))
    y_a = o.reshape(bsz, t_len, VAL_DIM).astype(h.dtype) @ w_proj_a
    y_b = (s_b * causal_depthwise_conv(s_c * s_x, conv_sc_w)) @ w_proj_b
    merged = jax.nn.sigmoid(g_a) * y_a + jax.nn.sigmoid(g_b) * y_b
    return merged @ w_out


def hierarchical_moe(h, w_group, b_group, w_expert, b_expert, w1, w3, w2):
    bsz, t_len, d = h.shape
    tok = h.reshape(-1, d)
    m = tok.shape[0]
    g_logits = (tok @ w_group).astype(jnp.float32) + b_group.astype(jnp.float32)
    g_sel = jnp.argmax(g_logits, axis=-1)
    p_group = jnp.take_along_axis(jax.nn.softmax(g_logits, axis=-1), g_sel[:, None], axis=-1)
    e_logits = ((tok @ w_expert).astype(jnp.float32) + b_expert.astype(jnp.float32)).reshape(m, N_GROUPS, EXPERTS_PER_GROUP)
    e_logits = jnp.take_along_axis(e_logits, g_sel[:, None, None], axis=1)[:, 0]
    top_v, top_i = lax.top_k(e_logits, TOP_K)
    wts = p_group * jax.nn.softmax(top_v, axis=-1)
    flat_e = (g_sel[:, None] * EXPERTS_PER_GROUP + top_i).reshape(-1)
    n_rows = m * TOP_K
    order = jnp.argsort(flat_e)
    sorted_e = flat_e[order]
    tok_idx = order // TOP_K
    wts_sorted = wts.reshape(-1)[order]
    sizes = jnp.bincount(flat_e, length=N_EXPERTS)
    padded = ((sizes + MOE_BLOCK - 1) // MOE_BLOCK) * MOE_BLOCK
    pad_end = jnp.cumsum(padded)
    pad_start = pad_end - padded
    start = jnp.cumsum(sizes) - sizes
    dest = pad_start[sorted_e] + (jnp.arange(n_rows) - start[sorted_e])
    n_blocks = -(-n_rows // MOE_BLOCK) + N_EXPERTS
    cap = n_blocks * MOE_BLOCK
    buf_tok = jnp.full((cap,), m, jnp.int32).at[dest].set(tok_idx.astype(jnp.int32))
    buf_w = jnp.zeros((cap,), tok.dtype).at[dest].set(wts_sorted.astype(tok.dtype))
    block_expert = jnp.clip(jnp.searchsorted(pad_end, jnp.arange(n_blocks) * MOE_BLOCK, side='right'), 0, N_EXPERTS - 1)
    tok_pad = jnp.concatenate([tok, jnp.zeros((1, d), tok.dtype)], axis=0)
    xs = tok_pad[buf_tok].reshape(n_blocks, MOE_BLOCK, d)

    def expert_block(args):
        xb, e = args
        hid = jax.nn.silu(xb @ w1[e]) * (xb @ w3[e])
        return hid @ w2[e]

    ys = lax.map(expert_block, (xs, block_expert)).reshape(cap, d)
    out = jax.ops.segment_sum(ys * buf_w[:, None], buf_tok, num_segments=m + 1)[:m]
    return out.reshape(bsz, t_len, d)


def setup_inputs(seed: int = 0) -> dict:
    key = jax.random.key(seed)
    ks = jax.random.split(key, 32)
    f32 = jnp.float32
    L, D = DEPTH, D_MODEL

    def nrm(k, shape, fan_in, gain=1.0):
        return gain * (fan_in ** -0.5) * jax.random.normal(k, shape, f32)

    def gain_init(k, shape):
        return 1.0 + 0.02 * jax.random.normal(k, shape, f32)

    dt = jnp.exp(jax.random.uniform(ks[8], (L, N_HEADS), f32, float(np.log(1e-3)), float(np.log(1e-1))))
    return {
        "x": jax.random.normal(ks[0], (BATCH, SEQ, D), f32),
        "c": jax.random.normal(ks[1], (BATCH, D), f32),
        "w_ada": nrm(ks[2], (L, D, N_MOD * D), D, 0.5),
        "b_ada": 0.02 * jax.random.normal(ks[3], (L, N_MOD * D), f32),
        "norm1_g": gain_init(ks[4], (L, D)),
        "w_in": nrm(ks[5], (L, D, IN_TOTAL), D),
        "conv_qkv_w": nrm(ks[6], (L, QKV_CONV, 2 * KEY_DIM + VAL_DIM), QKV_CONV),
        "a_log": jnp.log(jax.random.uniform(ks[7], (L, N_HEADS), f32, 1.0, 16.0)),
        "dt_bias": dt + jnp.log(-jnp.expm1(-dt)),
        "onorm_g": gain_init(ks[9], (L, HEAD_DV)),
        "w_proj_a": nrm(ks[10], (L, VAL_DIM, D), VAL_DIM),
        "conv_sc_w": nrm(ks[11], (L, SC_CONV, SC_DIM), SC_CONV),
        "w_proj_b": nrm(ks[12], (L, SC_DIM, D), SC_DIM),
        "w_out": nrm(ks[13], (L, D, D), D),
        "norm2_g": gain_init(ks[14], (L, D)),
        "w_group": nrm(ks[15], (L, D, N_GROUPS), D),
        "b_group": 0.01 * jax.random.normal(ks[16], (L, N_GROUPS), f32),
        "w_expert": nrm(ks[17], (L, D, N_EXPERTS), D),
        "b_expert": 0.01 * jax.random.normal(ks[18], (L, N_EXPERTS), f32),
        "w1": nrm(ks[19], (L, N_EXPERTS, D, D_EXPERT), D),
        "w3": nrm(ks[20], (L, N_EXPERTS, D, D_EXPERT), D),
        "w2": nrm(ks[21], (L, N_EXPERTS, D_EXPERT, D), D_EXPERT),
        "normf_g": gain_init(ks[22], (D,)),
    }


def reference(x, c, w_ada, b_ada, norm1_g, w_in, conv_qkv_w, a_log, dt_bias, onorm_g, w_proj_a,
              conv_sc_w, w_proj_b, w_out, norm2_g, w_group, b_group, w_expert, b_expert, w1, w3, w2, normf_g):
    c_act = jax.nn.silu(c)
    for l in range(DEPTH):
        mod = (c_act @ w_ada[l] + b_ada[l])[:, None, :]
        sh1, sc1, gt1, sh2, sc2, gt2 = jnp.split(mod, N_MOD, axis=-1)
        h = rmsnorm(x, norm1_g[l]) * (1.0 + sc1) + sh1
        x = x + gt1 * hybrid_mixer(h, w_in[l], conv_qkv_w[l], a_log[l], dt_bias[l], onorm_g[l],
                                   w_proj_a[l], conv_sc_w[l], w_proj_b[l], w_out[l])
        h = rmsnorm(x, norm2_g[l]) * (1.0 + sc2) + sh2
        x = x + gt2 * hierarchical_moe(h, w_group[l], b_group[l], w_expert[l], b_expert[l], w1[l], w3[l], w2[l])
    return rmsnorm(x, normf_g)
```

```python
import functools

import jax
import jax.numpy as jnp
from jax import lax
from jax.experimental import pallas as pl
from jax.experimental.pallas import tpu as pltpu

F32 = jnp.float32
BF16 = jnp.bfloat16
HIGHEST = lax.Precision.HIGHEST

N_HEADS = 8
HEAD_D = 128
CHUNK = 64
QKV_CONV = 4
SC_CONV = 3
N_GROUPS = 4
EXPERTS_PER_GROUP = 8
N_EXPERTS = N_GROUPS * EXPERTS_PER_GROUP
EPS = 1e-6

LANES = 128
SUBLANES = 8
BF16_ROWS = 16
N_PROJ_COLS = 9
EXPERT_BLOCK = 256
INPROJ_ROWS = 1024
GDN_ROWS = 512
POST_ROWS = 256
MOVE_ROWS = 256
MIB = 1024 * 1024


def _sigmoid(x):
    return 1.0 / (1.0 + jnp.exp(-x))


def _silu(x):
    return x * _sigmoid(x)


def _softplus(x):
    return jnp.maximum(x, 0.0) + jnp.log(1.0 + jnp.exp(-jnp.abs(x)))


def _dot(a, b):
    return jnp.dot(a, b, preferred_element_type=F32)


def _dot_nt(a, b):
    return lax.dot_general(a, b, (((1,), (1,)), ((), ())), preferred_element_type=F32)


def _dot_hi(a, b):
    return jnp.dot(a, b, preferred_element_type=F32, precision=HIGHEST)


def _params(semantics, vmem_mib):
    return pltpu.CompilerParams(dimension_semantics=semantics, vmem_limit_bytes=vmem_mib * MIB)


def _ada_kernel(c_ref, w_ref, b_ref, o_ref):
    o_ref[...] = _dot_hi(_silu(c_ref[...]), w_ref[...]) + b_ref[...]


def _ada(c_pad, w_ada, b_ada):
    d = c_pad.shape[1]
    n = w_ada.shape[1]
    return pl.pallas_call(
        _ada_kernel,
        grid=(n // d,),
        in_specs=[
            pl.BlockSpec((SUBLANES, d), lambda j: (0, 0)),
            pl.BlockSpec((d, d), lambda j: (0, j)),
            pl.BlockSpec((1, d), lambda j: (0, j)),
        ],
        out_specs=pl.BlockSpec((SUBLANES, d), lambda j: (0, j)),
        out_shape=jax.ShapeDtypeStruct((SUBLANES, n), F32),
        compiler_params=_params(("arbitrary",), 24),
        name="ada",
    )(c_pad, w_ada, b_ada)


def _inproj_kernel(x_ref, g_ref, sc_ref, sh_ref, w_ref, wba_ref, o_ref, ba_ref, h_ref):
    @pl.when(pl.program_id(1) == 0)
    def _():
        x = x_ref[...]
        y = x * lax.rsqrt(jnp.mean(x * x, axis=-1, keepdims=True) + EPS)
        h = (y * g_ref[...]) * (1.0 + sc_ref[...]) + sh_ref[...]
        hb = h.astype(BF16)
        h_ref[...] = hb
        ba_ref[...] = _dot(hb, wba_ref[...])

    o_ref[...] = _dot(h_ref[...], w_ref[...]).astype(BF16)


def _inproj(x2, norm_g, sc, sh, w_main, w_ba, seq, tm):
    m, d = x2.shape
    n = w_main.shape[1]
    tn = d
    per_batch = seq // tm
    return pl.pallas_call(
        _inproj_kernel,
        grid=(m // tm, n // tn),
        in_specs=[
            pl.BlockSpec((tm, d), lambda i, j: (i, 0)),
            pl.BlockSpec((1, d), lambda i, j: (0, 0)),
            pl.BlockSpec((None, 1, d), lambda i, j: (i // per_batch, 0, 0)),
            pl.BlockSpec((None, 1, d), lambda i, j: (i // per_batch, 0, 0)),
            pl.BlockSpec((d, tn), lambda i, j: (0, j)),
            pl.BlockSpec((d, LANES), lambda i, j: (0, 0)),
        ],
        out_specs=[
            pl.BlockSpec((tm, tn), lambda i, j: (i, j)),
            pl.BlockSpec((tm, LANES), lambda i, j: (i, 0)),
        ],
        out_shape=[
            jax.ShapeDtypeStruct((m, n), BF16),
            jax.ShapeDtypeStruct((m, LANES), F32),
        ],
        scratch_shapes=[pltpu.VMEM((tm, d), BF16)],
        compiler_params=_params(("arbitrary", "arbitrary"), 40),
        name="inproj",
    )(x2, norm_g, sc, sh, w_main, w_ba)


def _unit_lower_inverse(a):
    row = lax.broadcasted_iota(jnp.int32, a.shape, 0)
    col = lax.broadcasted_iota(jnp.int32, a.shape, 1)
    eye = jnp.where(row == col, 1.0, 0.0).astype(F32)
    p = eye - a
    xp = a
    n = 2
    while n < CHUNK:
        xp = _dot_hi(xp, xp)
        p = p + _dot_hi(p, xp)
        n *= 2
    return p


def _causal_conv_silu(win, cw, k_w):
    acc = win * cw[k_w - 1:k_w, :]
    for s in range(1, k_w):
        acc = acc + pltpu.roll(win, s, 0) * cw[k_w - 1 - s:k_w - s, :]
    return _silu(acc[SUBLANES:, :])


def _gdn_kernel(q_ref, k_ref, v_ref, z_ref, ba_ref, cw_ref, hp_ref, o_ref, s_ref, tail_ref):
    tb = q_ref.shape[0]
    kd = N_HEADS * HEAD_D

    @pl.when(pl.program_id(1) == 0)
    def _():
        s_ref[...] = jnp.zeros(s_ref.shape, F32)
        tail_ref[...] = jnp.zeros(tail_ref.shape, F32)

    row = lax.broadcasted_iota(jnp.int32, (CHUNK, CHUNK), 0)
    col = lax.broadcasted_iota(jnp.int32, (CHUNK, CHUNK), 1)
    causal = row >= col
    strict = row > col
    tril = jnp.where(causal, 1.0, 0.0).astype(F32)
    a_log = hp_ref[0:1, :]
    dt_bias = hp_ref[1:2, :]
    onorm_g = hp_ref[2:3, :]
    zeros_half = jnp.zeros((CHUNK, HEAD_D), F32)

    def chunk_body(c, carry):
        base = pl.multiple_of(c * CHUNK, CHUNK)
        ba = ba_ref[pl.ds(base, CHUNK), :]
        beta_all = _sigmoid(ba)
        g_all = -jnp.exp(a_log) * _softplus(ba + dt_bias)
        gcs = _dot_hi(tril, g_all)
        gcs_t = jnp.concatenate([gcs, gcs], axis=0).T
        for h in range(N_HEADS):
            lo, hi = h * HEAD_D, (h + 1) * HEAD_D

            def window(ref, off):
                cur = ref[pl.ds(base, CHUNK), lo:hi].astype(F32)
                return jnp.concatenate([tail_ref[:, off + lo:off + hi], cur], axis=0)

            q = _causal_conv_silu(window(q_ref, 0), cw_ref[:, lo:hi], QKV_CONV)
            k = _causal_conv_silu(window(k_ref, kd), cw_ref[:, kd + lo:kd + hi], QKV_CONV)
            v = _causal_conv_silu(window(v_ref, 2 * kd), cw_ref[:, 2 * kd + lo:2 * kd + hi], QKV_CONV)
            qn = q * lax.rsqrt(jnp.sum(q * q, axis=-1, keepdims=True) + EPS) * (HEAD_D ** -0.5)
            kn = k * lax.rsqrt(jnp.sum(k * k, axis=-1, keepdims=True) + EPS)
            beta = beta_all[:, h:h + 1]
            gc = gcs[:, N_HEADS + h:N_HEADS + h + 1]
            gr = gcs_t[N_HEADS + h:N_HEADS + h + 1, 0:CHUNK]
            gl = gcs[CHUNK - 1:CHUNK, N_HEADS + h:N_HEADS + h + 1]
            decay = jnp.where(causal, jnp.exp(jnp.where(causal, gc - gr, 0.0)), 0.0)
            kb = kn * beta
            e_gc = jnp.exp(gc)
            kq = _dot_nt(jnp.concatenate([kb, qn], axis=0).astype(BF16), kn.astype(BF16))
            a = jnp.where(strict, kq[:CHUNK] * decay, 0.0)
            intra = kq[CHUNK:] * decay
            t_inv = _unit_lower_inverse(a)
            uw = _dot(t_inv.astype(BF16),
                      jnp.concatenate([v * beta, kb * e_gc], axis=1).astype(BF16))
            u = uw[:, :HEAD_D]
            w = uw[:, HEAD_D:]
            q_dec = qn * e_gc
            k_dec = kn * jnp.exp(gl - gc)
            state = s_ref[h]
            ws = _dot(jnp.concatenate([w, q_dec], axis=0).astype(BF16), state.astype(BF16))
            v_new = u - ws[:CHUNK]
            v_new_b = v_new.astype(BF16)
            o = ws[CHUNK:] + _dot(intra.astype(BF16), v_new_b)
            k_dec_t = jnp.concatenate([k_dec, zeros_half], axis=0).T
            upd = _dot(k_dec_t.astype(BF16),
                       jnp.concatenate([v_new, zeros_half], axis=0).astype(BF16))
            s_ref[h] = state * jnp.exp(gl) + upd
            on = o * lax.rsqrt(jnp.mean(o * o, axis=-1, keepdims=True) + EPS) * onorm_g
            z = z_ref[pl.ds(base, CHUNK), lo:hi].astype(F32)
            o_ref[pl.ds(base, CHUNK), lo:hi] = (on * _silu(z)).astype(BF16)

        last = pl.multiple_of(base + CHUNK - BF16_ROWS, BF16_ROWS)
        for j, ref in enumerate((q_ref, k_ref, v_ref)):
            rows = ref[pl.ds(last, BF16_ROWS), :].astype(F32)
            tail_ref[:, j * kd:(j + 1) * kd] = rows[BF16_ROWS - SUBLANES:, :]
        return carry

    lax.fori_loop(0, tb // CHUNK, chunk_body, 0)


def _gdn(proj, ba, conv_w, head_params, batch, seq, tb):
    m = proj.shape[0]
    kd = N_HEADS * HEAD_D
    nt = seq // tb

    def col(j):
        return pl.BlockSpec((tb, kd), lambda b, t: (b * nt + t, j))

    return pl.pallas_call(
        _gdn_kernel,
        grid=(batch, nt),
        in_specs=[
            col(0), col(1), col(2), col(3),
            pl.BlockSpec((tb, LANES), lambda b, t: (b * nt + t, 0)),
            pl.BlockSpec((QKV_CONV, 3 * kd), lambda b, t: (0, 0)),
            pl.BlockSpec((SUBLANES, LANES), lambda b, t: (0, 0)),
        ],
        out_specs=pl.BlockSpec((tb, kd), lambda b, t: (b * nt + t, 0)),
        out_shape=jax.ShapeDtypeStruct((m, kd), BF16),
        scratch_shapes=[
            pltpu.VMEM((N_HEADS, HEAD_D, HEAD_D), F32),
            pltpu.VMEM((SUBLANES, 3 * kd), F32),
        ],
        compiler_params=_params(("arbitrary", "arbitrary"), 32),
        name="gdn",
    )(proj, proj, proj, proj, ba, conv_w, head_params)


def _post_kernel(x_ref, og_ref, sb_ref, sc_ref, sx_ref, ga_ref, gb_ref, mod_ref, n2g_ref, cw_ref,
                 wpa_ref, wpb_ref, wout_ref, wr_ref, br_ref,
                 x1_ref, h2_ref, route_ref, cnt_ref, win_ref, run_ref, *, per_batch):
    tm = x_ref.shape[0]
    i = pl.program_id(0)

    @pl.when(i == 0)
    def _():
        run_ref[...] = jnp.zeros(run_ref.shape, F32)

    @pl.when(i % per_batch == 0)
    def _():
        win_ref[0:SUBLANES, :] = jnp.zeros((SUBLANES, win_ref.shape[1]), F32)

    win_ref[SUBLANES:, :] = sc_ref[...].astype(F32) * sx_ref[...].astype(F32)
    conv = win_ref[pl.ds(SUBLANES, tm), :] * cw_ref[SC_CONV - 1:SC_CONV, :]
    for s in range(1, SC_CONV):
        conv = conv + win_ref[pl.ds(SUBLANES - s, tm), :] * cw_ref[SC_CONV - 1 - s:SC_CONV - s, :]
    win_ref[0:SUBLANES, :] = win_ref[pl.ds(tm, SUBLANES), :]
    y_b = _dot((sb_ref[...].astype(F32) * conv).astype(BF16), wpb_ref[...])
    y_a = _dot(og_ref[...], wpa_ref[...])
    merged = _sigmoid(ga_ref[...].astype(F32)) * y_a + _sigmoid(gb_ref[...].astype(F32)) * y_b
    mix = _dot(merged.astype(BF16), wout_ref[...])
    x1 = x_ref[...] + mod_ref[0:1, :] * mix
    x1_ref[...] = x1

    y = x1 * lax.rsqrt(jnp.mean(x1 * x1, axis=-1, keepdims=True) + EPS)
    h2 = (y * n2g_ref[...]) * (1.0 + mod_ref[1:2, :]) + mod_ref[2:3, :]
    h2_ref[...] = h2

    lg = _dot_hi(h2, wr_ref[...]) + br_ref[...]
    lane = lax.broadcasted_iota(jnp.int32, lg.shape, 1).astype(F32)
    neg = jnp.float32(-jnp.inf)
    big = jnp.float32(2 * LANES)

    def first_max(mask):
        vmax = jnp.max(jnp.where(mask, lg, neg), axis=-1, keepdims=True)
        idx = jnp.min(jnp.where(mask & (lg == vmax), lane, big), axis=-1, keepdims=True)
        return vmax, idx

    gmask = lane < N_GROUPS
    g_max, g_sel = first_max(gmask)
    p_group = 1.0 / jnp.sum(jnp.where(gmask, jnp.exp(lg - g_max), 0.0), axis=-1, keepdims=True)
    e_lo = N_GROUPS + EXPERTS_PER_GROUP * g_sel
    emask = (lane >= e_lo) & (lane < e_lo + EXPERTS_PER_GROUP)
    v1, i1 = first_max(emask)
    v2, i2 = first_max(emask & (lane != i1))
    ex = jnp.exp(v2 - v1)
    w1 = p_group * (1.0 / (1.0 + ex))
    w2 = p_group * (ex / (1.0 + ex))
    e1 = i1 - N_GROUPS
    e2 = i2 - N_GROUPS

    onehot = jnp.where((lane == e1) | (lane == e2), 1.0, 0.0).astype(F32)
    row = lax.broadcasted_iota(jnp.int32, (tm, tm), 0)
    col = lax.broadcasted_iota(jnp.int32, (tm, tm), 1)
    before = jnp.where(row > col, 1.0, 0.0).astype(BF16)
    seen = _dot(before, onehot.astype(BF16)) + run_ref[0:1, :]
    r1 = jnp.sum(jnp.where(lane == e1, seen, 0.0), axis=-1, keepdims=True)
    r2 = jnp.sum(jnp.where(lane == e2, seen, 0.0), axis=-1, keepdims=True)
    run_ref[0:1, :] = run_ref[0:1, :] + jnp.sum(onehot, axis=0, keepdims=True)
    cnt_ref[...] = jnp.broadcast_to(run_ref[0:1, :], cnt_ref.shape)

    out = jnp.where(lane == 0, e1, 0.0)
    out = jnp.where(lane == 1, e2, out)
    out = jnp.where(lane == 2, w1, out)
    out = jnp.where(lane == 3, w2, out)
    out = jnp.where(lane == 4, r1, out)
    out = jnp.where(lane == 5, r2, out)
    route_ref[...] = out


def _post(x2, og, proj, mod, n2g, conv_w, wpa, wpb, wout, w_route, b_route, seq, tm):
    m, d = x2.shape
    per_batch = seq // tm

    def rows(j):
        return pl.BlockSpec((tm, d), lambda i: (i, j))

    def whole(shape):
        return pl.BlockSpec(shape, lambda i: tuple(0 for _ in shape))

    return pl.pallas_call(
        functools.partial(_post_kernel, per_batch=per_batch),
        grid=(m // tm,),
        in_specs=[
            rows(0), rows(0), rows(4), rows(5), rows(6), rows(7), rows(8),
            pl.BlockSpec((None, SUBLANES, d), lambda i: (i // per_batch, 0, 0)),
            whole((1, d)), whole((SC_CONV, d)),
            whole((d, d)), whole((d, d)), whole((d, d)),
            whole((d, LANES)), whole((1, LANES)),
        ],
        out_specs=[
            rows(0), rows(0),
            pl.BlockSpec((tm, LANES), lambda i: (i, 0)),
            pl.BlockSpec((SUBLANES, LANES), lambda i: (0, 0)),
        ],
        out_shape=[
            jax.ShapeDtypeStruct((m, d), F32),
            jax.ShapeDtypeStruct((m, d), F32),
            jax.ShapeDtypeStruct((m, LANES), F32),
            jax.ShapeDtypeStruct((SUBLANES, LANES), F32),
        ],
        scratch_shapes=[
            pltpu.VMEM((tm + SUBLANES, d), F32),
            pltpu.VMEM((SUBLANES, LANES), F32),
        ],
        compiler_params=_params(("arbitrary",), 48),
        name="post",
    )(x2, og, proj, proj, proj, proj, proj, mod, n2g, conv_w, wpa, wpb, wout, w_route, b_route)


def _row_copy(src_ref, src_row, dst_ref, dst_row, sem):
    return pltpu.make_async_copy(src_ref.at[pl.ds(src_row, 1), :], dst_ref.at[pl.ds(dst_row, 1), :], sem)


def _dispatch_kernel(d1_ref, d2_ref, h2_ref, xs_in_ref, xs_ref, sem):
    del xs_in_ref
    tm = h2_ref.shape[0]
    t0 = pl.program_id(0) * tm

    def start(r, carry):
        _row_copy(h2_ref, r, xs_ref, d1_ref[t0 + r], sem.at[0]).start()
        _row_copy(h2_ref, r, xs_ref, d2_ref[t0 + r], sem.at[1]).start()
        return carry

    def wait(r, carry):
        _row_copy(h2_ref, r, xs_ref, d1_ref[t0 + r], sem.at[0]).wait()
        _row_copy(h2_ref, r, xs_ref, d2_ref[t0 + r], sem.at[1]).wait()
        return carry

    lax.fori_loop(0, tm, start, 0)
    lax.fori_loop(0, tm, wait, 0)


def _dispatch(d1, d2, h2, xs_zero, tm):
    m, d = h2.shape
    return pl.pallas_call(
        _dispatch_kernel,
        grid_spec=pltpu.PrefetchScalarGridSpec(
            num_scalar_prefetch=2,
            grid=(m // tm,),
            in_specs=[
                pl.BlockSpec((tm, d), lambda i, d1, d2: (i, 0)),
                pl.BlockSpec(memory_space=pl.ANY),
            ],
            out_specs=pl.BlockSpec(memory_space=pl.ANY),
            scratch_shapes=[pltpu.SemaphoreType.DMA((2,))],
        ),
        out_shape=jax.ShapeDtypeStruct(xs_zero.shape, F32),
        input_output_aliases={3: 0},
        compiler_params=_params(("arbitrary",), 24),
        name="dispatch",
    )(d1, d2, h2, xs_zero)


def _expert_kernel(be_ref, act_ref, x_ref, w1_ref, w3_ref, w2_ref, y_ref):
    b = pl.program_id(0)

    @pl.when(act_ref[b] > 0)
    def _():
        xb = x_ref[...].astype(BF16)
        hid = _silu(_dot(xb, w1_ref[...])) * _dot(xb, w3_ref[...])
        y_ref[...] = _dot(hid.astype(BF16), w2_ref[...])

    @pl.when(act_ref[b] == 0)
    def _():
        y_ref[...] = jnp.zeros(y_ref.shape, F32)


def _experts(block_expert, block_active, xs, w1, w3, w2):
    cap, d = xs.shape
    de = w1.shape[2]
    bm = EXPERT_BLOCK
    return pl.pallas_call(
        _expert_kernel,
        grid_spec=pltpu.PrefetchScalarGridSpec(
            num_scalar_prefetch=2,
            grid=(cap // bm,),
            in_specs=[
                pl.BlockSpec((bm, d), lambda b, be, act: (b, 0)),
                pl.BlockSpec((None, d, de), lambda b, be, act: (be[b], 0, 0)),
                pl.BlockSpec((None, d, de), lambda b, be, act: (be[b], 0, 0)),
                pl.BlockSpec((None, de, d), lambda b, be, act: (be[b], 0, 0)),
            ],
            out_specs=pl.BlockSpec((bm, d), lambda b, be, act: (b, 0)),
        ),
        out_shape=jax.ShapeDtypeStruct((cap, d), F32),
        compiler_params=_params(("arbitrary",), 32),
        name="experts",
    )(block_expert, block_active, xs, w1, w3, w2)


def _final_kernel(d1_ref, d2_ref, x1_ref, route_ref, gt_ref, nfg_ref, ys_ref, o_ref, buf_ref, sem):
    tm = x1_ref.shape[0]
    t0 = pl.program_id(0) * tm

    def start(r, carry):
        _row_copy(ys_ref, d1_ref[t0 + r], buf_ref.at[0], r, sem.at[0]).start()
        _row_copy(ys_ref, d2_ref[t0 + r], buf_ref.at[1], r, sem.at[1]).start()
        return carry

    def wait(r, carry):
        _row_copy(ys_ref, d1_ref[t0 + r], buf_ref.at[0], r, sem.at[0]).wait()
        _row_copy(ys_ref, d2_ref[t0 + r], buf_ref.at[1], r, sem.at[1]).wait()
        return carry

    lax.fori_loop(0, tm, start, 0)
    lax.fori_loop(0, tm, wait, 0)
    route = route_ref[...]
    moe = buf_ref[0] * route[:, 2:3] + buf_ref[1] * route[:, 3:4]
    x2 = x1_ref[...] + gt_ref[...] * moe
    y = x2 * lax.rsqrt(jnp.mean(x2 * x2, axis=-1, keepdims=True) + EPS)
    o_ref[...] = y * nfg_ref[...]


def _final(d1, d2, x1, route, gt2, nfg, ys, seq, tm):
    m, d = x1.shape
    per_batch = seq // tm
    return pl.pallas_call(
        _final_kernel,
        grid_spec=pltpu.PrefetchScalarGridSpec(
            num_scalar_prefetch=2,
            grid=(m // tm,),
            in_specs=[
                pl.BlockSpec((tm, d), lambda i, d1, d2: (i, 0)),
                pl.BlockSpec((tm, LANES), lambda i, d1, d2: (i, 0)),
                pl.BlockSpec((None, 1, d), lambda i, d1, d2: (i // per_batch, 0, 0)),
                pl.BlockSpec((1, d), lambda i, d1, d2: (0, 0)),
                pl.BlockSpec(memory_space=pl.ANY),
            ],
            out_specs=pl.BlockSpec((tm, d), lambda i, d1, d2: (i, 0)),
            scratch_shapes=[pltpu.VMEM((2, tm, d), F32), pltpu.SemaphoreType.DMA((2,))],
        ),
        out_shape=jax.ShapeDtypeStruct((m, d), F32),
        compiler_params=_params(("arbitrary",), 32),
        name="final",
    )(d1, d2, x1, route, gt2, nfg, ys)


def _tile(n, pref):
    t = min(n, pref)
    assert n % t == 0
    return t


def kernel(x, c, w_ada, b_ada, norm1_g, w_in, conv_qkv_w, a_log, dt_bias, onorm_g, w_proj_a,
           conv_sc_w, w_proj_b, w_out, norm2_g, w_group, b_group, w_expert, b_expert, w1, w3, w2,
           normf_g):
    batch, seq, d = x.shape
    depth = w_ada.shape[0]
    m = batch * seq
    kd = N_HEADS * HEAD_D
    assert d == kd and seq % CHUNK == 0 and batch <= SUBLANES
    c_pad = jnp.zeros((SUBLANES, d), F32).at[:batch].set(c)
    x2 = x.reshape(m, d)

    for l in range(depth):
        mod = _ada(c_pad, w_ada[l], b_ada[l][None, :])[:batch]
        sh1, sc1, gt1, sh2, sc2, gt2 = [mod[:, None, j * d:(j + 1) * d] for j in range(6)]

        w = w_in[l]
        o_ba = 3 * kd + kd
        w_main = jnp.concatenate([w[:, :o_ba], w[:, o_ba + 2 * N_HEADS:]], axis=1).astype(BF16)
        w_ba = jnp.zeros((d, LANES), F32).at[:, :2 * N_HEADS].set(w[:, o_ba:o_ba + 2 * N_HEADS]).astype(BF16)
        proj, ba = _inproj(x2, norm1_g[l][None, :], sc1, sh1, w_main, w_ba, seq, _tile(seq, INPROJ_ROWS))

        head_params = jnp.zeros((SUBLANES, LANES), F32)
        head_params = head_params.at[0, N_HEADS:2 * N_HEADS].set(a_log[l])
        head_params = head_params.at[1, N_HEADS:2 * N_HEADS].set(dt_bias[l])
        head_params = head_params.at[2, :].set(onorm_g[l])
        og = _gdn(proj, ba, conv_qkv_w[l], head_params, batch, seq, _tile(seq, GDN_ROWS))

        mod_post = jnp.zeros((batch, SUBLANES, d), F32)
        mod_post = mod_post.at[:, 0:1].set(gt1).at[:, 1:2].set(sc2).at[:, 2:3].set(sh2)
        w_route = jnp.zeros((d, LANES), F32)
        w_route = w_route.at[:, :N_GROUPS].set(w_group[l]).at[:, N_GROUPS:N_GROUPS + N_EXPERTS].set(w_expert[l])
        b_route = jnp.zeros((1, LANES), F32)
        b_route = b_route.at[0, :N_GROUPS].set(b_group[l]).at[0, N_GROUPS:N_GROUPS + N_EXPERTS].set(b_expert[l])
        x1, h2, route, counts = _post(
            x2, og, proj, mod_post, norm2_g[l][None, :], conv_sc_w[l],
            w_proj_a[l].astype(BF16), w_proj_b[l].astype(BF16), w_out[l].astype(BF16),
            w_route, b_route, seq, _tile(seq, POST_ROWS))

        bm = EXPERT_BLOCK
        n_blocks = (2 * m) // bm + N_EXPERTS
        sizes = counts[0, :N_EXPERTS].astype(jnp.int32)
        padded = ((sizes + bm - 1) // bm) * bm
        pad_end = jnp.cumsum(padded)
        pad_start = pad_end - padded
        e1 = route[:, 0].astype(jnp.int32)
        e2 = route[:, 1].astype(jnp.int32)
        d1 = pad_start[e1] + route[:, 4].astype(jnp.int32)
        d2 = pad_start[e2] + route[:, 5].astype(jnp.int32)
        block_row = jnp.arange(n_blocks, dtype=jnp.int32) * bm
        block_expert = jnp.clip(jnp.searchsorted(pad_end, block_row, side="right"),
                                0, N_EXPERTS - 1).astype(jnp.int32)
        block_active = (block_row < pad_end[-1]).astype(jnp.int32)

        xs = _dispatch(d1, d2, h2, jnp.zeros((n_blocks * bm, d), F32), _tile(seq, MOVE_ROWS))
        ys = _experts(block_expert, block_active, xs, w1[l].astype(BF16), w3[l].astype(BF16),
                      w2[l].astype(BF16))
        nfg = normf_g[None, :] if l == depth - 1 else jnp.ones((1, d), F32)
        x2 = _final(d1, d2, x1, route, gt2, nfg, ys, seq, _tile(seq, MOVE_ROWS))
        assert depth == 1
    return x2.reshape(batch, seq, d)
```

```python
import functools

import jax
import jax.numpy as jnp
from jax import lax
from jax.experimental import pallas as pl
from jax.experimental.pallas import tpu as pltpu

F32 = jnp.float32
BF16 = jnp.bfloat16
HIGHEST = lax.Precision.HIGHEST

N_HEADS = 8
HEAD_D = 128
CHUNK = 64
QKV_CONV = 4
SC_CONV = 3
N_GROUPS = 4
EXPERTS_PER_GROUP = 8
N_EXPERTS = N_GROUPS * EXPERTS_PER_GROUP
EPS = 1e-6

LANES = 128
SUBLANES = 8
BF16_ROWS = 16
N_PROJ_COLS = 9
EXPERT_BLOCK = 256
INPROJ_ROWS = 1024
GDN_ROWS = 256
POST_ROWS = 256
MOVE_ROWS = 256
MIB = 1024 * 1024


def _sigmoid(x):
    return 1.0 / (1.0 + jnp.exp(-x))


def _silu(x):
    return x * _sigmoid(x)


def _softplus(x):
    return jnp.maximum(x, 0.0) + jnp.log(1.0 + jnp.exp(-jnp.abs(x)))


def _dot(a, b):
    return jnp.dot(a, b, preferred_element_type=F32)


def _dot_nt(a, b):
    return lax.dot_general(a, b, (((1,), (1,)), ((), ())), preferred_element_type=F32)


def _dot_hi(a, b):
    return jnp.dot(a, b, preferred_element_type=F32, precision=HIGHEST)


def _params(semantics, vmem_mib):
    return pltpu.CompilerParams(dimension_semantics=semantics, vmem_limit_bytes=vmem_mib * MIB)


def _ada_kernel(c_ref, w_ref, b_ref, o_ref):
    o_ref[...] = _dot_hi(_silu(c_ref[...]), w_ref[...]) + b_ref[...]


def _ada(c_pad, w_ada, b_ada):
    d = c_pad.shape[1]
    n = w_ada.shape[1]
    return pl.pallas_call(
        _ada_kernel,
        grid=(n // d,),
        in_specs=[
            pl.BlockSpec((SUBLANES, d), lambda j: (0, 0)),
            pl.BlockSpec((d, d), lambda j: (0, j)),
            pl.BlockSpec((1, d), lambda j: (0, j)),
        ],
        out_specs=pl.BlockSpec((SUBLANES, d), lambda j: (0, j)),
        out_shape=jax.ShapeDtypeStruct((SUBLANES, n), F32),
        compiler_params=_params(("arbitrary",), 24),
        name="ada",
    )(c_pad, w_ada, b_ada)


def _inproj_kernel(x_ref, g_ref, sc_ref, sh_ref, w_ref, wba_ref, o_ref, ba_ref, h_ref):
    @pl.when(pl.program_id(1) == 0)
    def _():
        x = x_ref[...]
        y = x * lax.rsqrt(jnp.mean(x * x, axis=-1, keepdims=True) + EPS)
        h = (y * g_ref[...]) * (1.0 + sc_ref[...]) + sh_ref[...]
        hb = h.astype(BF16)
        h_ref[...] = hb
        ba_ref[...] = _dot(hb, wba_ref[...])

    o_ref[...] = _dot(h_ref[...], w_ref[...]).astype(BF16)


def _inproj(x2, norm_g, sc, sh, w_main, w_ba, seq, tm):
    m, d = x2.shape
    n = w_main.shape[1]
    tn = d
    per_batch = seq // tm
    return pl.pallas_call(
        _inproj_kernel,
        grid=(m // tm, n // tn),
        in_specs=[
            pl.BlockSpec((tm, d), lambda i, j: (i, 0)),
            pl.BlockSpec((1, d), lambda i, j: (0, 0)),
            pl.BlockSpec((None, 1, d), lambda i, j: (i // per_batch, 0, 0)),
            pl.BlockSpec((None, 1, d), lambda i, j: (i // per_batch, 0, 0)),
            pl.BlockSpec((d, tn), lambda i, j: (0, j)),
            pl.BlockSpec((d, LANES), lambda i, j: (0, 0)),
        ],
        out_specs=[
            pl.BlockSpec((tm, tn), lambda i, j: (i, j)),
            pl.BlockSpec((tm, LANES), lambda i, j: (i, 0)),
        ],
        out_shape=[
            jax.ShapeDtypeStruct((m, n), BF16),
            jax.ShapeDtypeStruct((m, LANES), F32),
        ],
        scratch_shapes=[pltpu.VMEM((tm, d), BF16)],
        compiler_params=_params(("arbitrary", "arbitrary"), 40),
        name="inproj",
    )(x2, norm_g, sc, sh, w_main, w_ba)


def _bmm(a, b):
    return jnp.einsum("hmk,hkn->hmn", a.astype(BF16), b.astype(BF16), preferred_element_type=F32)


def _bmm_nt(a, b):
    return jnp.einsum("hmk,hnk->hmn", a.astype(BF16), b.astype(BF16), preferred_element_type=F32)


def _unit_lower_inverse(a):
    row = lax.broadcasted_iota(jnp.int32, a.shape[1:], 0)
    col = lax.broadcasted_iota(jnp.int32, a.shape[1:], 1)
    eye = jnp.where(row == col, 1.0, 0.0).astype(F32)
    p = eye - a
    xp = a
    n = 2
    while n < CHUNK:
        xp = _bmm(xp, xp)
        p = p + _bmm(p, xp)
        n *= 2
    return p


def _causal_conv_silu(win, cw, k_w):
    acc = win * cw[k_w - 1:k_w, :]
    for s in range(1, k_w):
        acc = acc + pltpu.roll(win, s, 0) * cw[k_w - 1 - s:k_w - s, :]
    return _silu(acc[SUBLANES:, :])


def _gdn_kernel(q_ref, k_ref, v_ref, z_ref, ba_ref, cw_ref, hp_ref, o_ref,
                s_ref, tail_ref, wq_ref, u_ref, ik_ref, dec_ref):
    nb, tb = q_ref.shape[0], q_ref.shape[1]
    kd = N_HEADS * HEAD_D
    nbh = nb * N_HEADS

    @pl.when(pl.program_id(0) == 0)
    def _():
        s_ref[...] = jnp.zeros(s_ref.shape, F32)
        tail_ref[...] = jnp.zeros(tail_ref.shape, F32)

    row = lax.broadcasted_iota(jnp.int32, (CHUNK, CHUNK), 0)
    col = lax.broadcasted_iota(jnp.int32, (CHUNK, CHUNK), 1)
    causal = row >= col
    strict = row > col
    tril = jnp.where(causal, 1.0, 0.0).astype(F32)
    a_log = hp_ref[0:1, :]
    dt_bias = hp_ref[1:2, :]
    onorm_g = hp_ref[2:3, :]
    zeros_half = jnp.zeros((CHUNK, HEAD_D), F32)

    def precompute(c, carry):
        base = pl.multiple_of(c * CHUNK, CHUNK)
        qs, ks, vs, betas, gcs, grs, gls = [], [], [], [], [], [], []
        for b in range(nb):
            ba = ba_ref[b, pl.ds(base, CHUNK), :]
            beta_all = _sigmoid(ba)
            g_all = -jnp.exp(a_log) * _softplus(ba + dt_bias)
            gcum = _dot_hi(tril, g_all)
            gcum_t = jnp.concatenate([gcum, gcum], axis=0).T
            for h in range(N_HEADS):
                lo, hi = h * HEAD_D, (h + 1) * HEAD_D

                def conv(ref, off):
                    cur = ref[b, pl.ds(base, CHUNK), lo:hi].astype(F32)
                    win = jnp.concatenate([tail_ref[b, :, off + lo:off + hi], cur], axis=0)
                    return _causal_conv_silu(win, cw_ref[:, off + lo:off + hi], QKV_CONV)

                qs.append(conv(q_ref, 0))
                ks.append(conv(k_ref, kd))
                vs.append(conv(v_ref, 2 * kd))
                betas.append(beta_all[:, h:h + 1])
                gcs.append(gcum[:, N_HEADS + h:N_HEADS + h + 1])
                grs.append(gcum_t[N_HEADS + h:N_HEADS + h + 1, 0:CHUNK])
                gls.append(gcum[CHUNK - 1:CHUNK, N_HEADS + h:N_HEADS + h + 1])
        q, k, v = jnp.stack(qs), jnp.stack(ks), jnp.stack(vs)
        beta, gc, gr, gl = jnp.stack(betas), jnp.stack(gcs), jnp.stack(grs), jnp.stack(gls)
        qn = q * lax.rsqrt(jnp.sum(q * q, axis=-1, keepdims=True) + EPS) * (HEAD_D ** -0.5)
        kn = k * lax.rsqrt(jnp.sum(k * k, axis=-1, keepdims=True) + EPS)
        decay = jnp.where(causal, jnp.exp(jnp.where(causal, gc - gr, 0.0)), 0.0)
        kb = kn * beta
        e_gc = jnp.exp(gc)
        kq = _bmm_nt(jnp.concatenate([kb, qn], axis=1), kn)
        a = jnp.where(strict, kq[:, :CHUNK] * decay, 0.0)
        intra = kq[:, CHUNK:] * decay
        uw = _bmm(_unit_lower_inverse(a), jnp.concatenate([v * beta, kb * e_gc], axis=2))
        u_ref[c] = uw[:, :, :HEAD_D]
        wq_ref[c] = jnp.concatenate([uw[:, :, HEAD_D:], qn * e_gc], axis=1).astype(BF16)
        k_dec = kn * jnp.exp(gl - gc)
        k_dec_t = jnp.stack([jnp.concatenate([k_dec[i], zeros_half], axis=0).T[:, :CHUNK]
                             for i in range(nbh)])
        ik_ref[c] = jnp.concatenate([intra, k_dec_t], axis=1).astype(BF16)
        dec_ref[c] = jnp.broadcast_to(jnp.exp(gl), (nbh, 1, HEAD_D))

        last = pl.multiple_of(base + CHUNK - BF16_ROWS, BF16_ROWS)
        for b in range(nb):
            for j, ref in enumerate((q_ref, k_ref, v_ref)):
                rows = ref[b, pl.ds(last, BF16_ROWS), :].astype(F32)
                tail_ref[b, :, j * kd:(j + 1) * kd] = rows[BF16_ROWS - SUBLANES:, :]
        return carry

    def recur(c, carry):
        base = pl.multiple_of(c * CHUNK, CHUNK)
        state = s_ref[...]
        ws = _bmm(wq_ref[c], state)
        v_new = u_ref[c] - ws[:, :CHUNK]
        r = _bmm(ik_ref[c], v_new)
        o = ws[:, CHUNK:] + r[:, :CHUNK]
        s_ref[...] = state * dec_ref[c] + r[:, CHUNK:]
        on = o * lax.rsqrt(jnp.mean(o * o, axis=-1, keepdims=True) + EPS) * onorm_g
        for b in range(nb):
            for h in range(N_HEADS):
                lo, hi = h * HEAD_D, (h + 1) * HEAD_D
                z = z_ref[b, pl.ds(base, CHUNK), lo:hi].astype(F32)
                o_ref[b, pl.ds(base, CHUNK), lo:hi] = (on[b * N_HEADS + h] * _silu(z)).astype(BF16)
        return carry

    lax.fori_loop(0, tb // CHUNK, precompute, 0)
    lax.fori_loop(0, tb // CHUNK, recur, 0)


def _gdn(proj, ba, conv_w, head_params, batch, seq, tb):
    kd = N_HEADS * HEAD_D
    nc = tb // CHUNK
    nbh = batch * N_HEADS
    proj3 = proj.reshape(batch, seq, proj.shape[1])
    ba3 = ba.reshape(batch, seq, LANES)

    def col(j):
        return pl.BlockSpec((batch, tb, kd), lambda t: (0, t, j))

    out = pl.pallas_call(
        _gdn_kernel,
        grid=(seq // tb,),
        in_specs=[
            col(0), col(1), col(2), col(3),
            pl.BlockSpec((batch, tb, LANES), lambda t: (0, t, 0)),
            pl.BlockSpec((QKV_CONV, 3 * kd), lambda t: (0, 0)),
            pl.BlockSpec((SUBLANES, LANES), lambda t: (0, 0)),
        ],
        out_specs=pl.BlockSpec((batch, tb, kd), lambda t: (0, t, 0)),
        out_shape=jax.ShapeDtypeStruct((batch, seq, kd), BF16),
        scratch_shapes=[
            pltpu.VMEM((nbh, HEAD_D, HEAD_D), F32),
            pltpu.VMEM((batch, SUBLANES, 3 * kd), F32),
            pltpu.VMEM((nc, nbh, 2 * CHUNK, HEAD_D), BF16),
            pltpu.VMEM((nc, nbh, CHUNK, HEAD_D), F32),
            pltpu.VMEM((nc, nbh, CHUNK + HEAD_D, CHUNK), BF16),
            pltpu.VMEM((nc, nbh, 1, HEAD_D), F32),
        ],
        compiler_params=_params(("arbitrary",), 48),
        name="gdn",
    )(proj3, proj3, proj3, proj3, ba3, conv_w, head_params)
    return out.reshape(batch * seq, kd)


def _post_kernel(x_ref, og_ref, sb_ref, sc_ref, sx_ref, ga_ref, gb_ref, mod_ref, n2g_ref, cw_ref,
                 wpa_ref, wpb_ref, wout_ref, wr_ref, br_ref,
                 x1_ref, h2_ref, route_ref, cnt_ref, win_ref, run_ref, *, per_batch):
    tm = x_ref.shape[0]
    i = pl.program_id(0)

    @pl.when(i == 0)
    def _():
        run_ref[...] = jnp.zeros(run_ref.shape, F32)

    @pl.when(i % per_batch == 0)
    def _():
        win_ref[0:SUBLANES, :] = jnp.zeros((SUBLANES, win_ref.shape[1]), F32)

    win_ref[SUBLANES:, :] = sc_ref[...].astype(F32) * sx_ref[...].astype(F32)
    conv = win_ref[pl.ds(SUBLANES, tm), :] * cw_ref[SC_CONV - 1:SC_CONV, :]
    for s in range(1, SC_CONV):
        conv = conv + win_ref[pl.ds(SUBLANES - s, tm), :] * cw_ref[SC_CONV - 1 - s:SC_CONV - s, :]
    win_ref[0:SUBLANES, :] = win_ref[pl.ds(tm, SUBLANES), :]
    y_b = _dot((sb_ref[...].astype(F32) * conv).astype(BF16), wpb_ref[...])
    y_a = _dot(og_ref[...], wpa_ref[...])
    merged = _sigmoid(ga_ref[...].astype(F32)) * y_a + _sigmoid(gb_ref[...].astype(F32)) * y_b
    mix = _dot(merged.astype(BF16), wout_ref[...])
    x1 = x_ref[...] + mod_ref[0:1, :] * mix
    x1_ref[...] = x1

    y = x1 * lax.rsqrt(jnp.mean(x1 * x1, axis=-1, keepdims=True) + EPS)
    h2 = (y * n2g_ref[...]) * (1.0 + mod_ref[1:2, :]) + mod_ref[2:3, :]
    h2_ref[...] = h2

    lg = _dot_hi(h2, wr_ref[...]) + br_ref[...]
    lane = lax.broadcasted_iota(jnp.int32, lg.shape, 1).astype(F32)
    neg = jnp.float32(-jnp.inf)
    big = jnp.float32(2 * LANES)

    def first_max(mask):
        vmax = jnp.max(jnp.where(mask, lg, neg), axis=-1, keepdims=True)
        idx = jnp.min(jnp.where(mask & (lg == vmax), lane, big), axis=-1, keepdims=True)
        return vmax, idx

    gmask = lane < N_GROUPS
    g_max, g_sel = first_max(gmask)
    p_group = 1.0 / jnp.sum(jnp.where(gmask, jnp.exp(lg - g_max), 0.0), axis=-1, keepdims=True)
    e_lo = N_GROUPS + EXPERTS_PER_GROUP * g_sel
    emask = (lane >= e_lo) & (lane < e_lo + EXPERTS_PER_GROUP)
    v1, i1 = first_max(emask)
    v2, i2 = first_max(emask & (lane != i1))
    ex = jnp.exp(v2 - v1)
    w1 = p_group * (1.0 / (1.0 + ex))
    w2 = p_group * (ex / (1.0 + ex))
    e1 = i1 - N_GROUPS
    e2 = i2 - N_GROUPS

    onehot = jnp.where((lane == e1) | (lane == e2), 1.0, 0.0).astype(F32)
    row = lax.broadcasted_iota(jnp.int32, (tm, tm), 0)
    col = lax.broadcasted_iota(jnp.int32, (tm, tm), 1)
    before = jnp.where(row > col, 1.0, 0.0).astype(BF16)
    seen = _dot(before, onehot.astype(BF16)) + run_ref[0:1, :]
    r1 = jnp.sum(jnp.where(lane == e1, seen, 0.0), axis=-1, keepdims=True)
    r2 = jnp.sum(jnp.where(lane == e2, seen, 0.0), axis=-1, keepdims=True)
    run_ref[0:1, :] = run_ref[0:1, :] + jnp.sum(onehot, axis=0, keepdims=True)
    cnt_ref[...] = jnp.broadcast_to(run_ref[0:1, :], cnt_ref.shape)

    out = jnp.where(lane == 0, e1, 0.0)
    out = jnp.where(lane == 1, e2, out)
    out = jnp.where(lane == 2, w1, out)
    out = jnp.where(lane == 3, w2, out)
    out = jnp.where(lane == 4, r1, out)
    out = jnp.where(lane == 5, r2, out)
    route_ref[...] = out


def _post(x2, og, proj, mod, n2g, conv_w, wpa, wpb, wout, w_route, b_route, seq, tm):
    m, d = x2.shape
    per_batch = seq // tm

    def rows(j):
        return pl.BlockSpec((tm, d), lambda i: (i, j))

    def whole(shape):
        return pl.BlockSpec(shape, lambda i: tuple(0 for _ in shape))

    return pl.pallas_call(
        functools.partial(_post_kernel, per_batch=per_batch),
        grid=(m // tm,),
        in_specs=[
            rows(0), rows(0), rows(4), rows(5), rows(6), rows(7), rows(8),
            pl.BlockSpec((None, SUBLANES, d), lambda i: (i // per_batch, 0, 0)),
            whole((1, d)), whole((SC_CONV, d)),
            whole((d, d)), whole((d, d)), whole((d, d)),
            whole((d, LANES)), whole((1, LANES)),
        ],
        out_specs=[
            rows(0), rows(0),
            pl.BlockSpec((tm, LANES), lambda i: (i, 0)),
            pl.BlockSpec((SUBLANES, LANES), lambda i: (0, 0)),
        ],
        out_shape=[
            jax.ShapeDtypeStruct((m, d), F32),
            jax.ShapeDtypeStruct((m, d), F32),
            jax.ShapeDtypeStruct((m, LANES), F32),
            jax.ShapeDtypeStruct((SUBLANES, LANES), F32),
        ],
        scratch_shapes=[
            pltpu.VMEM((tm + SUBLANES, d), F32),
            pltpu.VMEM((SUBLANES, LANES), F32),
        ],
        compiler_params=_params(("arbitrary",), 48),
        name="post",
    )(x2, og, proj, proj, proj, proj, proj, mod, n2g, conv_w, wpa, wpb, wout, w_route, b_route)


def _row_copy(src_ref, src_row, dst_ref, dst_row, sem):
    return pltpu.make_async_copy(src_ref.at[pl.ds(src_row, 1), :], dst_ref.at[pl.ds(dst_row, 1), :], sem)


def _dispatch_kernel(d1_ref, d2_ref, h2_ref, xs_in_ref, xs_ref, sem):
    del xs_in_ref
    tm = h2_ref.shape[0]
    t0 = pl.program_id(0) * tm

    def start(r, carry):
        _row_copy(h2_ref, r, xs_ref, d1_ref[t0 + r], sem.at[0]).start()
        _row_copy(h2_ref, r, xs_ref, d2_ref[t0 + r], sem.at[1]).start()
        return carry

    def wait(r, carry):
        _row_copy(h2_ref, r, xs_ref, d1_ref[t0 + r], sem.at[0]).wait()
        _row_copy(h2_ref, r, xs_ref, d2_ref[t0 + r], sem.at[1]).wait()
        return carry

    lax.fori_loop(0, tm, start, 0)
    lax.fori_loop(0, tm, wait, 0)


def _dispatch(d1, d2, h2, xs_zero, tm):
    m, d = h2.shape
    return pl.pallas_call(
        _dispatch_kernel,
        grid_spec=pltpu.PrefetchScalarGridSpec(
            num_scalar_prefetch=2,
            grid=(m // tm,),
            in_specs=[
                pl.BlockSpec((tm, d), lambda i, d1, d2: (i, 0)),
                pl.BlockSpec(memory_space=pl.ANY),
            ],
            out_specs=pl.BlockSpec(memory_space=pl.ANY),
            scratch_shapes=[pltpu.SemaphoreType.DMA((2,))],
        ),
        out_shape=jax.ShapeDtypeStruct(xs_zero.shape, F32),
        input_output_aliases={3: 0},
        compiler_params=_params(("arbitrary",), 24),
        name="dispatch",
    )(d1, d2, h2, xs_zero)


def _expert_kernel(be_ref, act_ref, x_ref, w1_ref, w3_ref, w2_ref, y_ref):
    b = pl.program_id(0)

    @pl.when(act_ref[b] > 0)
    def _():
        xb = x_ref[...].astype(BF16)
        hid = _silu(_dot(xb, w1_ref[...])) * _dot(xb, w3_ref[...])
        y_ref[...] = _dot(hid.astype(BF16), w2_ref[...])

    @pl.when(act_ref[b] == 0)
    def _():
        y_ref[...] = jnp.zeros(y_ref.shape, F32)


def _experts(block_expert, block_active, xs, w1, w3, w2):
    cap, d = xs.shape
    de = w1.shape[2]
    bm = EXPERT_BLOCK
    return pl.pallas_call(
        _expert_kernel,
        grid_spec=pltpu.PrefetchScalarGridSpec(
            num_scalar_prefetch=2,
            grid=(cap // bm,),
            in_specs=[
                pl.BlockSpec((bm, d), lambda b, be, act: (b, 0)),
                pl.BlockSpec((None, d, de), lambda b, be, act: (be[b], 0, 0)),
                pl.BlockSpec((None, d, de), lambda b, be, act: (be[b], 0, 0)),
                pl.BlockSpec((None, de, d), lambda b, be, act: (be[b], 0, 0)),
            ],
            out_specs=pl.BlockSpec((bm, d), lambda b, be, act: (b, 0)),
        ),
        out_shape=jax.ShapeDtypeStruct((cap, d), F32),
        compiler_params=_params(("arbitrary",), 32),
        name="experts",
    )(block_expert, block_active, xs, w1, w3, w2)


def _final_kernel(d1_ref, d2_ref, x1_ref, route_ref, gt_ref, nfg_ref, ys_ref, o_ref, buf_ref, sem):
    tm = x1_ref.shape[0]
    t0 = pl.program_id(0) * tm

    def start(r, carry):
        _row_copy(ys_ref, d1_ref[t0 + r], buf_ref.at[0], r, sem.at[0]).start()
        _row_copy(ys_ref, d2_ref[t0 + r], buf_ref.at[1], r, sem.at[1]).start()
        return carry

    def wait(r, carry):
        _row_copy(ys_ref, d1_ref[t0 + r], buf_ref.at[0], r, sem.at[0]).wait()
        _row_copy(ys_ref, d2_ref[t0 + r], buf_ref.at[1], r, sem.at[1]).wait()
        return carry

    lax.fori_loop(0, tm, start, 0)
    lax.fori_loop(0, tm, wait, 0)
    route = route_ref[...]
    moe = buf_ref[0] * route[:, 2:3] + buf_ref[1] * route[:, 3:4]
    x2 = x1_ref[...] + gt_ref[...] * moe
    y = x2 * lax.rsqrt(jnp.mean(x2 * x2, axis=-1, keepdims=True) + EPS)
    o_ref[...] = y * nfg_ref[...]


def _final(d1, d2, x1, route, gt2, nfg, ys, seq, tm):
    m, d = x1.shape
    per_batch = seq // tm
    return pl.pallas_call(
        _final_kernel,
        grid_spec=pltpu.PrefetchScalarGridSpec(
            num_scalar_prefetch=2,
            grid=(m // tm,),
            in_specs=[
                pl.BlockSpec((tm, d), lambda i, d1, d2: (i, 0)),
                pl.BlockSpec((tm, LANES), lambda i, d1, d2: (i, 0)),
                pl.BlockSpec((None, 1, d), lambda i, d1, d2: (i // per_batch, 0, 0)),
                pl.BlockSpec((1, d), lambda i, d1, d2: (0, 0)),
                pl.BlockSpec(memory_space=pl.ANY),
            ],
            out_specs=pl.BlockSpec((tm, d), lambda i, d1, d2: (i, 0)),
            scratch_shapes=[pltpu.VMEM((2, tm, d), F32), pltpu.SemaphoreType.DMA((2,))],
        ),
        out_shape=jax.ShapeDtypeStruct((m, d), F32),
        compiler_params=_params(("arbitrary",), 32),
        name="final",
    )(d1, d2, x1, route, gt2, nfg, ys)


def _tile(n, pref):
    t = min(n, pref)
    assert n % t == 0
    return t


def kernel(x, c, w_ada, b_ada, norm1_g, w_in, conv_qkv_w, a_log, dt_bias, onorm_g, w_proj_a,
           conv_sc_w, w_proj_b, w_out, norm2_g, w_group, b_group, w_expert, b_expert, w1, w3, w2,
           normf_g):
    batch, seq, d = x.shape
    depth = w_ada.shape[0]
    m = batch * seq
    kd = N_HEADS * HEAD_D
    assert d == kd and seq % CHUNK == 0 and batch <= SUBLANES
    c_pad = jnp.zeros((SUBLANES, d), F32).at[:batch].set(c)
    x2 = x.reshape(m, d)

    for l in range(depth):
        mod = _ada(c_pad, w_ada[l], b_ada[l][None, :])[:batch]
        sh1, sc1, gt1, sh2, sc2, gt2 = [mod[:, None, j * d:(j + 1) * d] for j in range(6)]

        w = w_in[l]
        o_ba = 3 * kd + kd
        w_main = jnp.concatenate([w[:, :o_ba], w[:, o_ba + 2 * N_HEADS:]], axis=1).astype(BF16)
        w_ba = jnp.zeros((d, LANES), F32).at[:, :2 * N_HEADS].set(w[:, o_ba:o_ba + 2 * N_HEADS]).astype(BF16)
        proj, ba = _inproj(x2, norm1_g[l][None, :], sc1, sh1, w_main, w_ba, seq, _tile(seq, INPROJ_ROWS))

        head_params = jnp.zeros((SUBLANES, LANES), F32)
        head_params = head_params.at[0, N_HEADS:2 * N_HEADS].set(a_log[l])
        head_params = head_params.at[1, N_HEADS:2 * N_HEADS].set(dt_bias[l])
        head_params = head_params.at[2, :].set(onorm_g[l])
        og = _gdn(proj, ba, conv_qkv_w[l], head_params, batch, seq, _tile(seq, GDN_ROWS))

        mod_post = jnp.zeros((batch, SUBLANES, d), F32)
        mod_post = mod_post.at[:, 0:1].set(gt1).at[:, 1:2].set(sc2).at[:, 2:3].set(sh2)
        w_route = jnp.zeros((d, LANES), F32)
        w_route = w_route.at[:, :N_GROUPS].set(w_group[l]).at[:, N_GROUPS:N_GROUPS + N_EXPERTS].set(w_expert[l])
        b_route = jnp.zeros((1, LANES), F32)
        b_route = b_route.at[0, :N_GROUPS].set(b_group[l]).at[0, N_GROUPS:N_GROUPS + N_EXPERTS].set(b_expert[l])
        x1, h2, route, counts = _post(
            x2, og, proj, mod_post, norm2_g[l][None, :], conv_sc_w[l],
            w_proj_a[l].astype(BF16), w_proj_b[l].astype(BF16), w_out[l].astype(BF16),
            w_route, b_route, seq, _tile(seq, POST_ROWS))

        bm = EXPERT_BLOCK
        n_blocks = (2 * m) // bm + N_EXPERTS
        sizes = counts[0, :N_EXPERTS].astype(jnp.int32)
        padded = ((sizes + bm - 1) // bm) * bm
        pad_end = jnp.cumsum(padded)
        pad_start = pad_end - padded
        e1 = route[:, 0].astype(jnp.int32)
        e2 = route[:, 1].astype(jnp.int32)
        d1 = pad_start[e1] + route[:, 4].astype(jnp.int32)
        d2 = pad_start[e2] + route[:, 5].astype(jnp.int32)
        block_row = jnp.arange(n_blocks, dtype=jnp.int32) * bm
        block_expert = jnp.minimum(jnp.sum(block_row[:, None] >= pad_end[None, :], axis=1),
                                   N_EXPERTS - 1).astype(jnp.int32)
        block_active = (block_row < pad_end[-1]).astype(jnp.int32)

        xs = _dispatch(d1, d2, h2, jnp.zeros((n_blocks * bm, d), F32), _tile(seq, MOVE_ROWS))
        ys = _experts(block_expert, block_active, xs, w1[l].astype(BF16), w3[l].astype(BF16),
                      w2[l].astype(BF16))
        nfg = normf_g[None, :] if l == depth - 1 else jnp.ones((1, d), F32)
        x2 = _final(d1, d2, x1, route, gt2, nfg, ys, seq, _tile(seq, MOVE_ROWS))
        assert depth == 1
    return x2.reshape(batch, seq, d)
```

```python
import functools

import jax
import jax.numpy as jnp
from jax import lax
from jax.experimental import pallas as pl
from jax.experimental.pallas import tpu as pltpu

F32 = jnp.float32
BF16 = jnp.bfloat16
HIGHEST = lax.Precision.HIGHEST

N_HEADS = 8
HEAD_D = 128
CHUNK = 64
QKV_CONV = 4
SC_CONV = 3
N_GROUPS = 4
EXPERTS_PER_GROUP = 8
N_EXPERTS = N_GROUPS * EXPERTS_PER_GROUP
EPS = 1e-6

LANES = 128
SUBLANES = 8
BF16_ROWS = 16
N_PROJ_COLS = 9
EXPERT_BLOCK = 256
INPROJ_ROWS = 1024
GDN_ROWS = 256
POST_ROWS = 256
MOVE_ROWS = 256
DMA_UNROLL = 8
MIB = 1024 * 1024


def _sigmoid(x):
    return 1.0 / (1.0 + jnp.exp(-x))


def _silu(x):
    return x * _sigmoid(x)


def _softplus(x):
    return jnp.maximum(x, 0.0) + jnp.log(1.0 + jnp.exp(-jnp.abs(x)))


def _dot(a, b):
    return jnp.dot(a, b, preferred_element_type=F32)


def _dot_nt(a, b):
    return lax.dot_general(a, b, (((1,), (1,)), ((), ())), preferred_element_type=F32)


def _dot_hi(a, b):
    return jnp.dot(a, b, preferred_element_type=F32, precision=HIGHEST)


def _params(semantics, vmem_mib):
    return pltpu.CompilerParams(dimension_semantics=semantics, vmem_limit_bytes=vmem_mib * MIB)


def _ada_kernel(c_ref, w_ref, b_ref, o_ref):
    o_ref[...] = _dot_hi(_silu(c_ref[...]), w_ref[...]) + b_ref[...]


def _ada(c_pad, w_ada, b_ada):
    d = c_pad.shape[1]
    n = w_ada.shape[1]
    return pl.pallas_call(
        _ada_kernel,
        grid=(n // d,),
        in_specs=[
            pl.BlockSpec((SUBLANES, d), lambda j: (0, 0)),
            pl.BlockSpec((d, d), lambda j: (0, j)),
            pl.BlockSpec((1, d), lambda j: (0, j)),
        ],
        out_specs=pl.BlockSpec((SUBLANES, d), lambda j: (0, j)),
        out_shape=jax.ShapeDtypeStruct((SUBLANES, n), F32),
        compiler_params=_params(("arbitrary",), 24),
        name="ada",
    )(c_pad, w_ada, b_ada)


def _inproj_kernel(x_ref, g_ref, sc_ref, sh_ref, w_ref, wba_ref, o_ref, ba_ref, h_ref):
    @pl.when(pl.program_id(1) == 0)
    def _():
        x = x_ref[...]
        y = x * lax.rsqrt(jnp.mean(x * x, axis=-1, keepdims=True) + EPS)
        h = (y * g_ref[...]) * (1.0 + sc_ref[...]) + sh_ref[...]
        hb = h.astype(BF16)
        h_ref[...] = hb
        ba_ref[...] = _dot(hb, wba_ref[...])

    o_ref[...] = _dot(h_ref[...], w_ref[...]).astype(BF16)


def _inproj(x2, norm_g, sc, sh, w_main, w_ba, seq, tm):
    m, d = x2.shape
    n = w_main.shape[1]
    tn = d
    per_batch = seq // tm
    return pl.pallas_call(
        _inproj_kernel,
        grid=(m // tm, n // tn),
        in_specs=[
            pl.BlockSpec((tm, d), lambda i, j: (i, 0)),
            pl.BlockSpec((1, d), lambda i, j: (0, 0)),
            pl.BlockSpec((None, 1, d), lambda i, j: (i // per_batch, 0, 0)),
            pl.BlockSpec((None, 1, d), lambda i, j: (i // per_batch, 0, 0)),
            pl.BlockSpec((d, tn), lambda i, j: (0, j)),
            pl.BlockSpec((d, LANES), lambda i, j: (0, 0)),
        ],
        out_specs=[
            pl.BlockSpec((tm, tn), lambda i, j: (i, j)),
            pl.BlockSpec((tm, LANES), lambda i, j: (i, 0)),
        ],
        out_shape=[
            jax.ShapeDtypeStruct((m, n), BF16),
            jax.ShapeDtypeStruct((m, LANES), F32),
        ],
        scratch_shapes=[pltpu.VMEM((tm, d), BF16)],
        compiler_params=_params(("arbitrary", "arbitrary"), 40),
        name="inproj",
    )(x2, norm_g, sc, sh, w_main, w_ba)


def _bmm(a, b):
    return jnp.einsum("hmk,hkn->hmn", a.astype(BF16), b.astype(BF16), preferred_element_type=F32)


def _bmm_nt(a, b):
    return jnp.einsum("hmk,hnk->hmn", a.astype(BF16), b.astype(BF16), preferred_element_type=F32)


def _unit_lower_inverse(a):
    row = lax.broadcasted_iota(jnp.int32, a.shape[1:], 0)
    col = lax.broadcasted_iota(jnp.int32, a.shape[1:], 1)
    eye = jnp.where(row == col, 1.0, 0.0).astype(F32)
    p = eye - a
    xp = a
    n = 2
    while n < CHUNK:
        xp = _bmm(xp, xp)
        p = p + _bmm(p, xp)
        n *= 2
    return p


def _causal_conv_silu(win, cw, k_w):
    acc = win * cw[k_w - 1:k_w, :]
    for s in range(1, k_w):
        acc = acc + pltpu.roll(win, s, 0) * cw[k_w - 1 - s:k_w - s, :]
    return _silu(acc[SUBLANES:, :])


def _gdn_kernel(q_ref, k_ref, v_ref, z_ref, ba_ref, cw_ref, hp_ref, o_ref,
                s_ref, tail_ref, wq_ref, u_ref, ik_ref, dec_ref):
    nb, tb = q_ref.shape[0], q_ref.shape[1]
    kd = N_HEADS * HEAD_D
    nbh = nb * N_HEADS

    @pl.when(pl.program_id(0) == 0)
    def _():
        s_ref[...] = jnp.zeros(s_ref.shape, F32)
        tail_ref[...] = jnp.zeros(tail_ref.shape, F32)

    row = lax.broadcasted_iota(jnp.int32, (CHUNK, CHUNK), 0)
    col = lax.broadcasted_iota(jnp.int32, (CHUNK, CHUNK), 1)
    causal = row >= col
    strict = row > col
    tril = jnp.where(causal, 1.0, 0.0).astype(F32)
    a_log = hp_ref[0:1, :]
    dt_bias = hp_ref[1:2, :]
    onorm_g = hp_ref[2:3, :]
    zeros_half = jnp.zeros((CHUNK, HEAD_D), F32)

    def precompute(c, carry):
        base = pl.multiple_of(c * CHUNK, CHUNK)
        qs, ks, vs, betas, gcs, grs, gls = [], [], [], [], [], [], []
        for b in range(nb):
            ba = ba_ref[b, pl.ds(base, CHUNK), :]
            beta_all = _sigmoid(ba)
            g_all = -jnp.exp(a_log) * _softplus(ba + dt_bias)
            gcum = _dot_hi(tril, g_all)
            gcum_t = jnp.concatenate([gcum, gcum], axis=0).T
            for h in range(N_HEADS):
                lo, hi = h * HEAD_D, (h + 1) * HEAD_D

                def conv(ref, off):
                    cur = ref[b, pl.ds(base, CHUNK), lo:hi].astype(F32)
                    win = jnp.concatenate([tail_ref[b, :, off + lo:off + hi], cur], axis=0)
                    return _causal_conv_silu(win, cw_ref[:, off + lo:off + hi], QKV_CONV)

                qs.append(conv(q_ref, 0))
                ks.append(conv(k_ref, kd))
                vs.append(conv(v_ref, 2 * kd))
                betas.append(beta_all[:, h:h + 1])
                gcs.append(gcum[:, N_HEADS + h:N_HEADS + h + 1])
                grs.append(gcum_t[N_HEADS + h:N_HEADS + h + 1, 0:CHUNK])
                gls.append(gcum[CHUNK - 1:CHUNK, N_HEADS + h:N_HEADS + h + 1])
        q, k, v = jnp.stack(qs), jnp.stack(ks), jnp.stack(vs)
        beta, gc, gr, gl = jnp.stack(betas), jnp.stack(gcs), jnp.stack(grs), jnp.stack(gls)
        qn = q * lax.rsqrt(jnp.sum(q * q, axis=-1, keepdims=True) + EPS) * (HEAD_D ** -0.5)
        kn = k * lax.rsqrt(jnp.sum(k * k, axis=-1, keepdims=True) + EPS)
        decay = jnp.where(causal, jnp.exp(jnp.where(causal, gc - gr, 0.0)), 0.0)
        kb = kn * beta
        e_gc = jnp.exp(gc)
        kq = _bmm_nt(jnp.concatenate([kb, qn], axis=1), kn)
        a = jnp.where(strict, kq[:, :CHUNK] * decay, 0.0)
        intra = kq[:, CHUNK:] * decay
        uw = _bmm(_unit_lower_inverse(a), jnp.concatenate([v * beta, kb * e_gc], axis=2))
        u_ref[c] = uw[:, :, :HEAD_D]
        wq_ref[c] = jnp.concatenate([uw[:, :, HEAD_D:], qn * e_gc], axis=1).astype(BF16)
        k_dec = kn * jnp.exp(gl - gc)
        k_dec_t = jnp.stack([jnp.concatenate([k_dec[i], zeros_half], axis=0).T[:, :CHUNK]
                             for i in range(nbh)])
        ik_ref[c] = jnp.concatenate([intra, k_dec_t], axis=1).astype(BF16)
        dec_ref[c] = jnp.broadcast_to(jnp.exp(gl), (nbh, 1, HEAD_D))

        last = pl.multiple_of(base + CHUNK - BF16_ROWS, BF16_ROWS)
        for b in range(nb):
            for j, ref in enumerate((q_ref, k_ref, v_ref)):
                rows = ref[b, pl.ds(last, BF16_ROWS), :].astype(F32)
                tail_ref[b, :, j * kd:(j + 1) * kd] = rows[BF16_ROWS - SUBLANES:, :]
        return carry

    def recur(c, carry):
        base = pl.multiple_of(c * CHUNK, CHUNK)
        state = s_ref[...]
        ws = _bmm(wq_ref[c], state)
        v_new = u_ref[c] - ws[:, :CHUNK]
        r = _bmm(ik_ref[c], v_new)
        o = ws[:, CHUNK:] + r[:, :CHUNK]
        s_ref[...] = state * dec_ref[c] + r[:, CHUNK:]
        on = o * lax.rsqrt(jnp.mean(o * o, axis=-1, keepdims=True) + EPS) * onorm_g
        for b in range(nb):
            for h in range(N_HEADS):
                lo, hi = h * HEAD_D, (h + 1) * HEAD_D
                z = z_ref[b, pl.ds(base, CHUNK), lo:hi].astype(F32)
                o_ref[b, pl.ds(base, CHUNK), lo:hi] = (on[b * N_HEADS + h] * _silu(z)).astype(BF16)
        return carry

    lax.fori_loop(0, tb // CHUNK, precompute, 0)
    lax.fori_loop(0, tb // CHUNK, recur, 0)


def _gdn(proj, ba, conv_w, head_params, batch, seq, tb):
    kd = N_HEADS * HEAD_D
    nc = tb // CHUNK
    nbh = batch * N_HEADS
    proj3 = proj.reshape(batch, seq, proj.shape[1])
    ba3 = ba.reshape(batch, seq, LANES)

    def col(j):
        return pl.BlockSpec((batch, tb, kd), lambda t: (0, t, j))

    out = pl.pallas_call(
        _gdn_kernel,
        grid=(seq // tb,),
        in_specs=[
            col(0), col(1), col(2), col(3),
            pl.BlockSpec((batch, tb, LANES), lambda t: (0, t, 0)),
            pl.BlockSpec((QKV_CONV, 3 * kd), lambda t: (0, 0)),
            pl.BlockSpec((SUBLANES, LANES), lambda t: (0, 0)),
        ],
        out_specs=pl.BlockSpec((batch, tb, kd), lambda t: (0, t, 0)),
        out_shape=jax.ShapeDtypeStruct((batch, seq, kd), BF16),
        scratch_shapes=[
            pltpu.VMEM((nbh, HEAD_D, HEAD_D), F32),
            pltpu.VMEM((batch, SUBLANES, 3 * kd), F32),
            pltpu.VMEM((nc, nbh, 2 * CHUNK, HEAD_D), BF16),
            pltpu.VMEM((nc, nbh, CHUNK, HEAD_D), F32),
            pltpu.VMEM((nc, nbh, CHUNK + HEAD_D, CHUNK), BF16),
            pltpu.VMEM((nc, nbh, 1, HEAD_D), F32),
        ],
        compiler_params=_params(("arbitrary",), 48),
        name="gdn",
    )(proj3, proj3, proj3, proj3, ba3, conv_w, head_params)
    return out.reshape(batch * seq, kd)


def _post_kernel(x_ref, og_ref, sb_ref, sc_ref, sx_ref, ga_ref, gb_ref, mod_ref, n2g_ref, cw_ref,
                 wpa_ref, wpb_ref, wout_ref, wr_ref, br_ref,
                 x1_ref, h2_ref, route_ref, cnt_ref, win_ref, run_ref, *, per_batch):
    tm = x_ref.shape[0]
    i = pl.program_id(0)

    @pl.when(i == 0)
    def _():
        run_ref[...] = jnp.zeros(run_ref.shape, F32)

    @pl.when(i % per_batch == 0)
    def _():
        win_ref[0:SUBLANES, :] = jnp.zeros((SUBLANES, win_ref.shape[1]), F32)

    win_ref[SUBLANES:, :] = sc_ref[...].astype(F32) * sx_ref[...].astype(F32)
    conv = win_ref[pl.ds(SUBLANES, tm), :] * cw_ref[SC_CONV - 1:SC_CONV, :]
    for s in range(1, SC_CONV):
        conv = conv + win_ref[pl.ds(SUBLANES - s, tm), :] * cw_ref[SC_CONV - 1 - s:SC_CONV - s, :]
    win_ref[0:SUBLANES, :] = win_ref[pl.ds(tm, SUBLANES), :]
    y_b = _dot((sb_ref[...].astype(F32) * conv).astype(BF16), wpb_ref[...])
    y_a = _dot(og_ref[...], wpa_ref[...])
    merged = _sigmoid(ga_ref[...].astype(F32)) * y_a + _sigmoid(gb_ref[...].astype(F32)) * y_b
    mix = _dot(merged.astype(BF16), wout_ref[...])
    x1 = x_ref[...] + mod_ref[0:1, :] * mix
    x1_ref[...] = x1

    y = x1 * lax.rsqrt(jnp.mean(x1 * x1, axis=-1, keepdims=True) + EPS)
    h2 = (y * n2g_ref[...]) * (1.0 + mod_ref[1:2, :]) + mod_ref[2:3, :]
    h2_ref[...] = h2

    lg = _dot_hi(h2, wr_ref[...]) + br_ref[...]
    lane = lax.broadcasted_iota(jnp.int32, lg.shape, 1).astype(F32)
    neg = jnp.float32(-jnp.inf)
    big = jnp.float32(2 * LANES)

    def first_max(mask):
        vmax = jnp.max(jnp.where(mask, lg, neg), axis=-1, keepdims=True)
        idx = jnp.min(jnp.where(mask & (lg == vmax), lane, big), axis=-1, keepdims=True)
        return vmax, idx

    gmask = lane < N_GROUPS
    g_max, g_sel = first_max(gmask)
    p_group = 1.0 / jnp.sum(jnp.where(gmask, jnp.exp(lg - g_max), 0.0), axis=-1, keepdims=True)
    e_lo = N_GROUPS + EXPERTS_PER_GROUP * g_sel
    emask = (lane >= e_lo) & (lane < e_lo + EXPERTS_PER_GROUP)
    v1, i1 = first_max(emask)
    v2, i2 = first_max(emask & (lane != i1))
    ex = jnp.exp(v2 - v1)
    w1 = p_group * (1.0 / (1.0 + ex))
    w2 = p_group * (ex / (1.0 + ex))
    e1 = i1 - N_GROUPS
    e2 = i2 - N_GROUPS

    onehot = jnp.where((lane == e1) | (lane == e2), 1.0, 0.0).astype(F32)
    row = lax.broadcasted_iota(jnp.int32, (tm, tm), 0)
    col = lax.broadcasted_iota(jnp.int32, (tm, tm), 1)
    before = jnp.where(row > col, 1.0, 0.0).astype(BF16)
    seen = _dot(before, onehot.astype(BF16)) + run_ref[0:1, :]
    r1 = jnp.sum(jnp.where(lane == e1, seen, 0.0), axis=-1, keepdims=True)
    r2 = jnp.sum(jnp.where(lane == e2, seen, 0.0), axis=-1, keepdims=True)
    run_ref[0:1, :] = run_ref[0:1, :] + jnp.sum(onehot, axis=0, keepdims=True)
    cnt_ref[...] = jnp.broadcast_to(run_ref[0:1, :], cnt_ref.shape)

    out = jnp.where(lane == 0, e1, 0.0)
    out = jnp.where(lane == 1, e2, out)
    out = jnp.where(lane == 2, w1, out)
    out = jnp.where(lane == 3, w2, out)
    out = jnp.where(lane == 4, r1, out)
    out = jnp.where(lane == 5, r2, out)
    route_ref[...] = out


def _post(x2, og, proj, mod, n2g, conv_w, wpa, wpb, wout, w_route, b_route, seq, tm):
    m, d = x2.shape
    per_batch = seq // tm

    def rows(j):
        return pl.BlockSpec((tm, d), lambda i: (i, j))

    def whole(shape):
        return pl.BlockSpec(shape, lambda i: tuple(0 for _ in shape))

    return pl.pallas_call(
        functools.partial(_post_kernel, per_batch=per_batch),
        grid=(m // tm,),
        in_specs=[
            rows(0), rows(0), rows(4), rows(5), rows(6), rows(7), rows(8),
            pl.BlockSpec((None, SUBLANES, d), lambda i: (i // per_batch, 0, 0)),
            whole((1, d)), whole((SC_CONV, d)),
            whole((d, d)), whole((d, d)), whole((d, d)),
            whole((d, LANES)), whole((1, LANES)),
        ],
        out_specs=[
            rows(0), rows(0),
            pl.BlockSpec((tm, LANES), lambda i: (i, 0)),
            pl.BlockSpec((SUBLANES, LANES), lambda i: (0, 0)),
        ],
        out_shape=[
            jax.ShapeDtypeStruct((m, d), F32),
            jax.ShapeDtypeStruct((m, d), F32),
            jax.ShapeDtypeStruct((m, LANES), F32),
            jax.ShapeDtypeStruct((SUBLANES, LANES), F32),
        ],
        scratch_shapes=[
            pltpu.VMEM((tm + SUBLANES, d), F32),
            pltpu.VMEM((SUBLANES, LANES), F32),
        ],
        compiler_params=_params(("arbitrary",), 48),
        name="post",
    )(x2, og, proj, proj, proj, proj, proj, mod, n2g, conv_w, wpa, wpb, wout, w_route, b_route)


def _plan_kernel(route_ref, cnt_ref, d_ref):
    bm = EXPERT_BLOCK
    sizes = cnt_ref[...]
    padded = jnp.floor((sizes + (bm - 1.0)) * (1.0 / bm)) * bm
    lane_i = lax.broadcasted_iota(jnp.int32, sizes.shape, 1)
    incl = padded
    s = 1
    while s < LANES:
        incl = incl + jnp.where(lane_i >= s, pltpu.roll(incl, s, 1), 0.0)
        s *= 2
    start = (incl - padded)[0:1, :]
    r = route_ref[...]
    lane = lax.broadcasted_iota(jnp.int32, r.shape, 1).astype(F32)
    d1 = jnp.sum(jnp.where(lane == r[:, 0:1], start, 0.0), axis=-1, keepdims=True) + r[:, 4:5]
    d2 = jnp.sum(jnp.where(lane == r[:, 1:2], start, 0.0), axis=-1, keepdims=True) + r[:, 5:6]
    out = jnp.where(lane == 0.0, d1, jnp.where(lane == 1.0, d2, 0.0))
    d_ref[...] = out.T[0:SUBLANES, :].astype(jnp.int32)


def _plan(route, counts, tm):
    m = route.shape[0]
    return pl.pallas_call(
        _plan_kernel,
        grid=(m // tm,),
        in_specs=[
            pl.BlockSpec((tm, LANES), lambda i: (i, 0)),
            pl.BlockSpec((SUBLANES, LANES), lambda i: (0, 0)),
        ],
        out_specs=pl.BlockSpec((SUBLANES, tm), lambda i: (0, i)),
        out_shape=jax.ShapeDtypeStruct((SUBLANES, m), jnp.int32),
        compiler_params=_params(("arbitrary",), 16),
        name="plan",
    )(route, counts)


def _row_copy(src_ref, src_row, dst_ref, dst_row, sem):
    return pltpu.make_async_copy(src_ref.at[pl.ds(src_row, 1), :], dst_ref.at[pl.ds(dst_row, 1), :], sem)


def _rows_copy(src_ref, dst_ref, dst_row, n, sem):
    return pltpu.make_async_copy(src_ref, dst_ref.at[pl.ds(dst_row, n), :], sem)


def _dispatch_kernel(d1_ref, d2_ref, tail_ref, h2_ref, xs_ref, zero_ref, sem):
    tm = h2_ref.shape[0]
    bm = zero_ref.shape[0]
    t0 = pl.program_id(0) * tm

    @pl.when(pl.program_id(0) == 0)
    def _():
        zero_ref[...] = jnp.zeros(zero_ref.shape, F32)
        n_blocks = xs_ref.shape[0] // bm

        def zero_block(row):
            return _rows_copy(zero_ref, xs_ref, pl.multiple_of(row, bm), bm, sem.at[0])

        for e in range(N_EXPERTS):
            @pl.when(tail_ref[e] >= 0)
            def _():
                zero_block(tail_ref[e]).start()
        lax.fori_loop(tail_ref[N_EXPERTS], n_blocks, lambda j, c: (zero_block(j * bm).start(), c)[1], 0)
        for e in range(N_EXPERTS):
            @pl.when(tail_ref[e] >= 0)
            def _():
                zero_block(tail_ref[e]).wait()
        lax.fori_loop(tail_ref[N_EXPERTS], n_blocks, lambda j, c: (zero_block(j * bm).wait(), c)[1], 0)

    def start(r, carry):
        _row_copy(h2_ref, r, xs_ref, d1_ref[t0 + r], sem.at[0]).start()
        _row_copy(h2_ref, r, xs_ref, d2_ref[t0 + r], sem.at[1]).start()
        return carry

    lax.fori_loop(0, tm, start, 0, unroll=DMA_UNROLL)
    _rows_copy(h2_ref, xs_ref, 0, tm, sem.at[0]).wait()
    _rows_copy(h2_ref, xs_ref, 0, tm, sem.at[1]).wait()


def _dispatch(d1, d2, tail_start, h2, cap, tm):
    m, d = h2.shape
    return pl.pallas_call(
        _dispatch_kernel,
        grid_spec=pltpu.PrefetchScalarGridSpec(
            num_scalar_prefetch=3,
            grid=(m // tm,),
            in_specs=[pl.BlockSpec((tm, d), lambda i, d1, d2, tl: (i, 0))],
            out_specs=pl.BlockSpec(memory_space=pl.ANY),
            scratch_shapes=[pltpu.VMEM((EXPERT_BLOCK, d), F32), pltpu.SemaphoreType.DMA((2,))],
        ),
        out_shape=jax.ShapeDtypeStruct((cap, d), F32),
        compiler_params=_params(("arbitrary",), 24),
        name="dispatch",
    )(d1, d2, tail_start, h2)


def _expert_kernel(be_ref, act_ref, x_ref, w1_ref, w3_ref, w2_ref, y_ref, w1b_ref, w3b_ref, w2b_ref):
    b = pl.program_id(0)

    @pl.when(act_ref[b] > 0)
    def _():
        @pl.when((b == 0) | (be_ref[b] != be_ref[jnp.maximum(b - 1, 0)]))
        def _():
            w1b_ref[...] = w1_ref[...].astype(BF16)
            w3b_ref[...] = w3_ref[...].astype(BF16)
            w2b_ref[...] = w2_ref[...].astype(BF16)

        xb = x_ref[...].astype(BF16)
        hid = _silu(_dot(xb, w1b_ref[...])) * _dot(xb, w3b_ref[...])
        y_ref[...] = _dot(hid.astype(BF16), w2b_ref[...])

    @pl.when(act_ref[b] == 0)
    def _():
        y_ref[...] = jnp.zeros(y_ref.shape, F32)


def _experts(block_expert, block_active, xs, w1, w3, w2):
    cap, d = xs.shape
    de = w1.shape[2]
    bm = EXPERT_BLOCK

    def x_block(b, be, act):
        return (jnp.minimum(b, jnp.maximum(act[cap // bm], 1) - 1), 0)

    return pl.pallas_call(
        _expert_kernel,
        grid_spec=pltpu.PrefetchScalarGridSpec(
            num_scalar_prefetch=2,
            grid=(cap // bm,),
            in_specs=[
                pl.BlockSpec((bm, d), x_block),
                pl.BlockSpec((None, d, de), lambda b, be, act: (be[b], 0, 0)),
                pl.BlockSpec((None, d, de), lambda b, be, act: (be[b], 0, 0)),
                pl.BlockSpec((None, de, d), lambda b, be, act: (be[b], 0, 0)),
            ],
            out_specs=pl.BlockSpec((bm, d), lambda b, be, act: (b, 0)),
            scratch_shapes=[pltpu.VMEM((d, de), BF16), pltpu.VMEM((d, de), BF16),
                            pltpu.VMEM((de, d), BF16)],
        ),
        out_shape=jax.ShapeDtypeStruct((cap, d), F32),
        compiler_params=_params(("arbitrary",), 40),
        name="experts",
    )(block_expert, block_active, xs, w1, w3, w2)


def _final_kernel(d1_ref, d2_ref, x1_ref, route_ref, gt_ref, nfg_ref, ys_ref, o_ref, buf_ref, sem):
    tm = x1_ref.shape[0]
    t0 = pl.program_id(0) * tm

    def start(r, carry):
        _row_copy(ys_ref, d1_ref[t0 + r], buf_ref.at[0], r, sem.at[0]).start()
        _row_copy(ys_ref, d2_ref[t0 + r], buf_ref.at[1], r, sem.at[1]).start()
        return carry

    lax.fori_loop(0, tm, start, 0, unroll=DMA_UNROLL)
    for k in range(2):
        pltpu.make_async_copy(ys_ref.at[pl.ds(0, tm), :], buf_ref.at[k], sem.at[k]).wait()
    route = route_ref[...]
    moe = buf_ref[0] * route[:, 2:3] + buf_ref[1] * route[:, 3:4]
    x2 = x1_ref[...] + gt_ref[...] * moe
    y = x2 * lax.rsqrt(jnp.mean(x2 * x2, axis=-1, keepdims=True) + EPS)
    o_ref[...] = y * nfg_ref[...]


def _final(d1, d2, x1, route, gt2, nfg, ys, seq, tm):
    m, d = x1.shape
    per_batch = seq // tm
    return pl.pallas_call(
        _final_kernel,
        grid_spec=pltpu.PrefetchScalarGridSpec(
            num_scalar_prefetch=2,
            grid=(m // tm,),
            in_specs=[
                pl.BlockSpec((tm, d), lambda i, d1, d2: (i, 0)),
                pl.BlockSpec((tm, LANES), lambda i, d1, d2: (i, 0)),
                pl.BlockSpec((None, 1, d), lambda i, d1, d2: (i // per_batch, 0, 0)),
                pl.BlockSpec((1, d), lambda i, d1, d2: (0, 0)),
                pl.BlockSpec(memory_space=pl.ANY),
            ],
            out_specs=pl.BlockSpec((tm, d), lambda i, d1, d2: (i, 0)),
            scratch_shapes=[pltpu.VMEM((2, tm, d), F32), pltpu.SemaphoreType.DMA((2,))],
        ),
        out_shape=jax.ShapeDtypeStruct((m, d), F32),
        compiler_params=_params(("arbitrary",), 32),
        name="final",
    )(d1, d2, x1, route, gt2, nfg, ys)


def _tile(n, pref):
    t = min(n, pref)
    assert n % t == 0
    return t


def kernel(x, c, w_ada, b_ada, norm1_g, w_in, conv_qkv_w, a_log, dt_bias, onorm_g, w_proj_a,
           conv_sc_w, w_proj_b, w_out, norm2_g, w_group, b_group, w_expert, b_expert, w1, w3, w2,
           normf_g):
    batch, seq, d = x.shape
    depth = w_ada.shape[0]
    m = batch * seq
    kd = N_HEADS * HEAD_D
    assert d == kd and seq % CHUNK == 0 and batch <= SUBLANES
    c_pad = jnp.zeros((SUBLANES, d), F32).at[:batch].set(c)
    x2 = x.reshape(m, d)

    for l in range(depth):
        mod = _ada(c_pad, w_ada[l], b_ada[l][None, :])[:batch]
        sh1, sc1, gt1, sh2, sc2, gt2 = [mod[:, None, j * d:(j + 1) * d] for j in range(6)]

        w = w_in[l]
        o_ba = 3 * kd + kd
        w_main = jnp.concatenate([w[:, :o_ba], w[:, o_ba + 2 * N_HEADS:]], axis=1).astype(BF16)
        w_ba = jnp.zeros((d, LANES), F32).at[:, :2 * N_HEADS].set(w[:, o_ba:o_ba + 2 * N_HEADS]).astype(BF16)
        proj, ba = _inproj(x2, norm1_g[l][None, :], sc1, sh1, w_main, w_ba, seq, _tile(seq, INPROJ_ROWS))

        head_params = jnp.zeros((SUBLANES, LANES), F32)
        head_params = head_params.at[0, N_HEADS:2 * N_HEADS].set(a_log[l])
        head_params = head_params.at[1, N_HEADS:2 * N_HEADS].set(dt_bias[l])
        head_params = head_params.at[2, :].set(onorm_g[l])
        og = _gdn(proj, ba, conv_qkv_w[l], head_params, batch, seq, _tile(seq, GDN_ROWS))

        mod_post = jnp.zeros((batch, SUBLANES, d), F32)
        mod_post = mod_post.at[:, 0:1].set(gt1).at[:, 1:2].set(sc2).at[:, 2:3].set(sh2)
        w_route = jnp.zeros((d, LANES), F32)
        w_route = w_route.at[:, :N_GROUPS].set(w_group[l]).at[:, N_GROUPS:N_GROUPS + N_EXPERTS].set(w_expert[l])
        b_route = jnp.zeros((1, LANES), F32)
        b_route = b_route.at[0, :N_GROUPS].set(b_group[l]).at[0, N_GROUPS:N_GROUPS + N_EXPERTS].set(b_expert[l])
        x1, h2, route, counts = _post(
            x2, og, proj, mod_post, norm2_g[l][None, :], conv_sc_w[l],
            w_proj_a[l].astype(BF16), w_proj_b[l].astype(BF16), w_out[l].astype(BF16),
            w_route, b_route, seq, _tile(seq, POST_ROWS))

        bm = EXPERT_BLOCK
        n_blocks = (2 * m) // bm + N_EXPERTS
        sizes = counts[0, :N_EXPERTS].astype(jnp.int32)
        padded = ((sizes + bm - 1) // bm) * bm
        pad_end = jnp.cumsum(padded)
        tail_start = jnp.concatenate([jnp.where(padded > 0, pad_end - bm, -1),
                                      pad_end[-1:] // bm]).astype(jnp.int32)
        block_row = jnp.arange(n_blocks, dtype=jnp.int32) * bm
        block_expert = jnp.minimum(jnp.sum(block_row[:, None] >= pad_end[None, :], axis=1),
                                   N_EXPERTS - 1).astype(jnp.int32)
        block_active = jnp.concatenate([(block_row < pad_end[-1]).astype(jnp.int32),
                                        (pad_end[-1:] // bm).astype(jnp.int32)])
        dest = _plan(route, counts, _tile(seq, INPROJ_ROWS))
        d1, d2 = dest[0], dest[1]

        xs = _dispatch(d1, d2, tail_start, h2, n_blocks * bm, _tile(seq, MOVE_ROWS))
        ys = _experts(block_expert, block_active, xs, w1[l], w3[l], w2[l])
        nfg = normf_g[None, :] if l == depth - 1 else jnp.ones((1, d), F32)
        x2 = _final(d1, d2, x1, route, gt2, nfg, ys, seq, _tile(seq, MOVE_ROWS))
        assert depth == 1
    return x2.reshape(batch, seq, d)
```

```python
import functools

import jax
import jax.numpy as jnp
from jax import lax
from jax.experimental import pallas as pl
from jax.experimental.pallas import tpu as pltpu

F32 = jnp.float32
BF16 = jnp.bfloat16
HIGHEST = lax.Precision.HIGHEST

N_HEADS = 8
HEAD_D = 128
CHUNK = 64
QKV_CONV = 4
SC_CONV = 3
N_GROUPS = 4
EXPERTS_PER_GROUP = 8
N_EXPERTS = N_GROUPS * EXPERTS_PER_GROUP
EPS = 1e-6

LANES = 128
SUBLANES = 8
BF16_ROWS = 16
N_PROJ_COLS = 9
Q_COL, K_COL, V_COL, Z_COL, SB_COL, SC_COL, SX_COL, GA_COL, GB_COL = range(N_PROJ_COLS)
N_ACT_COLS = 7
EXPERT_BLOCK = 256
INPROJ_ROWS = 1024
GDN_ROWS = 256
POST_ROWS = 256
MOVE_ROWS = 256
DMA_UNROLL = 8
MIB = 1024 * 1024


def _sigmoid(x):
    return 1.0 / (1.0 + jnp.exp(-x))


def _silu(x):
    return x * _sigmoid(x)


def _softplus(x):
    return jnp.maximum(x, 0.0) + jnp.log(1.0 + jnp.exp(-jnp.abs(x)))


def _dot(a, b):
    return jnp.dot(a, b, preferred_element_type=F32)


def _dot_nt(a, b):
    return lax.dot_general(a, b, (((1,), (1,)), ((), ())), preferred_element_type=F32)


def _dot_hi(a, b):
    return jnp.dot(a, b, preferred_element_type=F32, precision=HIGHEST)


def _params(semantics, vmem_mib):
    return pltpu.CompilerParams(dimension_semantics=semantics, vmem_limit_bytes=vmem_mib * MIB)


def _ada_kernel(c_ref, w_ref, b_ref, o_ref):
    o_ref[...] = _dot_hi(_silu(c_ref[...]), w_ref[...]) + b_ref[...]


def _ada(c_pad, w_ada, b_ada):
    d = c_pad.shape[1]
    n = w_ada.shape[1]
    return pl.pallas_call(
        _ada_kernel,
        grid=(n // d,),
        in_specs=[
            pl.BlockSpec((SUBLANES, d), lambda j: (0, 0)),
            pl.BlockSpec((d, d), lambda j: (0, j)),
            pl.BlockSpec((1, d), lambda j: (0, j)),
        ],
        out_specs=pl.BlockSpec((SUBLANES, d), lambda j: (0, j)),
        out_shape=jax.ShapeDtypeStruct((SUBLANES, n), F32),
        compiler_params=_params(("arbitrary",), 24),
        name="ada",
    )(c_pad, w_ada, b_ada)


def _conv_rows(x, halo, cw, k_w):
    def taps(a):
        out = a * cw[k_w - 1:k_w, :]
        for s in range(1, k_w):
            out = out + pltpu.roll(a, s, 0) * cw[k_w - 1 - s:k_w - s, :]
        return out

    top = taps(jnp.concatenate([halo, x[:SUBLANES]], axis=0))[SUBLANES:]
    return jnp.concatenate([top, taps(x)[SUBLANES:]], axis=0)


def _l2norm_heads(y, scale):
    parts = []
    for h in range(N_HEADS):
        p = y[:, h * HEAD_D:(h + 1) * HEAD_D]
        parts.append(p * (lax.rsqrt(jnp.sum(p * p, axis=-1, keepdims=True) + EPS) * scale))
    return jnp.concatenate(parts, axis=1)


def _inproj_kernel(x_ref, g_ref, sc_ref, sh_ref, w_ref, wba_ref, cwq_ref, cws_ref, o_ref, ba_ref,
                   h_ref, sb_ref, scs_ref, halo_ref, *, per_batch):
    tm = x_ref.shape[0]
    i = pl.program_id(0)
    j = pl.program_id(1)

    @pl.when(j == 0)
    def _():
        x = x_ref[...]
        y = x * lax.rsqrt(jnp.mean(x * x, axis=-1, keepdims=True) + EPS)
        h = (y * g_ref[...]) * (1.0 + sc_ref[...]) + sh_ref[...]
        hb = h.astype(BF16)
        h_ref[...] = hb
        ba_ref[...] = _dot(hb, wba_ref[...])

    @pl.when((j == 0) & (i % per_batch == 0))
    def _():
        halo_ref[...] = jnp.zeros(halo_ref.shape, F32)

    def project():
        return _dot(h_ref[...], w_ref[...])

    def conv_silu(slot):
        acc = project()
        y = _silu(_conv_rows(acc, halo_ref[slot], cwq_ref[...], QKV_CONV))
        halo_ref[slot] = acc[tm - SUBLANES:]
        return y

    @pl.when(j == Q_COL)
    def _():
        o_ref[...] = _l2norm_heads(conv_silu(0), HEAD_D ** -0.5).astype(BF16)

    @pl.when(j == K_COL)
    def _():
        o_ref[...] = _l2norm_heads(conv_silu(1), 1.0).astype(BF16)

    @pl.when(j == V_COL)
    def _():
        o_ref[...] = conv_silu(2).astype(BF16)

    @pl.when(j == Z_COL)
    def _():
        o_ref[...] = _silu(project()).astype(BF16)

    @pl.when(j == SB_COL)
    def _():
        sb_ref[...] = project().astype(BF16)

    @pl.when(j == SC_COL)
    def _():
        scs_ref[...] = project().astype(BF16)

    @pl.when(j == SX_COL)
    def _():
        p = scs_ref[...].astype(F32) * project()
        conv = _conv_rows(p, halo_ref[3], cws_ref[...], SC_CONV)
        halo_ref[3] = p[tm - SUBLANES:]
        o_ref[...] = (sb_ref[...].astype(F32) * conv).astype(BF16)

    @pl.when(j >= GA_COL)
    def _():
        o_ref[...] = _sigmoid(project()).astype(BF16)


def _inproj(x2, norm_g, sc, sh, w_main, w_ba, conv_qkv, conv_sc, seq, tm):
    m, d = x2.shape
    n = w_main.shape[1]
    assert n == N_PROJ_COLS * d
    per_batch = seq // tm

    def out_col(j):
        return j - jnp.clip(j - SB_COL, 0, SX_COL - SB_COL)

    return pl.pallas_call(
        functools.partial(_inproj_kernel, per_batch=per_batch),
        grid=(m // tm, N_PROJ_COLS),
        in_specs=[
            pl.BlockSpec((tm, d), lambda i, j: (i, 0)),
            pl.BlockSpec((1, d), lambda i, j: (0, 0)),
            pl.BlockSpec((None, 1, d), lambda i, j: (i // per_batch, 0, 0)),
            pl.BlockSpec((None, 1, d), lambda i, j: (i // per_batch, 0, 0)),
            pl.BlockSpec((d, d), lambda i, j: (0, j)),
            pl.BlockSpec((d, LANES), lambda i, j: (0, 0)),
            pl.BlockSpec((QKV_CONV, d), lambda i, j: (0, jnp.minimum(j, V_COL))),
            pl.BlockSpec((SC_CONV, d), lambda i, j: (0, 0)),
        ],
        out_specs=[
            pl.BlockSpec((tm, d), lambda i, j: (i, out_col(j))),
            pl.BlockSpec((tm, LANES), lambda i, j: (i, 0)),
        ],
        out_shape=[
            jax.ShapeDtypeStruct((m, N_ACT_COLS * d), BF16),
            jax.ShapeDtypeStruct((m, LANES), F32),
        ],
        scratch_shapes=[
            pltpu.VMEM((tm, d), BF16),
            pltpu.VMEM((tm, d), BF16),
            pltpu.VMEM((tm, d), BF16),
            pltpu.VMEM((QKV_CONV, SUBLANES, d), F32),
        ],
        compiler_params=_params(("arbitrary", "arbitrary"), 48),
        name="inproj",
    )(x2, norm_g, sc, sh, w_main, w_ba, conv_qkv, conv_sc)


def _bmm(a, b):
    return jnp.einsum("hmk,hkn->hmn", a.astype(BF16), b.astype(BF16), preferred_element_type=F32)


def _bmm_nt(a, b):
    return jnp.einsum("hmk,hnk->hmn", a.astype(BF16), b.astype(BF16), preferred_element_type=F32)


def _unit_lower_inverse(a):
    row = lax.broadcasted_iota(jnp.int32, a.shape[1:], 0)
    col = lax.broadcasted_iota(jnp.int32, a.shape[1:], 1)
    eye = jnp.where(row == col, 1.0, 0.0).astype(F32)
    p = eye - a
    xp = a
    n = 2
    while n < CHUNK:
        xp = _bmm(xp, xp)
        p = p + _bmm(p, xp)
        n *= 2
    return p


def _gdn_kernel(q_ref, k_ref, v_ref, z_ref, ba_ref, hp_ref, o_ref,
                s_ref, wq_ref, u_ref, ik_ref, dec_ref):
    nb, tb = q_ref.shape[0], q_ref.shape[1]
    nbh = nb * N_HEADS

    @pl.when(pl.program_id(0) == 0)
    def _():
        s_ref[...] = jnp.zeros(s_ref.shape, F32)

    row = lax.broadcasted_iota(jnp.int32, (CHUNK, CHUNK), 0)
    col = lax.broadcasted_iota(jnp.int32, (CHUNK, CHUNK), 1)
    causal = row >= col
    strict = row > col
    tril = jnp.where(causal, 1.0, 0.0).astype(F32)
    a_log = hp_ref[0:1, :]
    dt_bias = hp_ref[1:2, :]
    onorm_g = hp_ref[2:3, :]
    zeros_half = jnp.zeros((CHUNK, HEAD_D), F32)

    def precompute(c, carry):
        base = pl.multiple_of(c * CHUNK, CHUNK)
        qs, ks, vs, betas, gcs, grs, gls = [], [], [], [], [], [], []
        for b in range(nb):
            ba = ba_ref[b, pl.ds(base, CHUNK), :]
            beta_all = _sigmoid(ba)
            g_all = -jnp.exp(a_log) * _softplus(ba + dt_bias)
            gcum = _dot_hi(tril, g_all)
            gcum_t = jnp.concatenate([gcum, gcum], axis=0).T
            for h in range(N_HEADS):
                lo, hi = h * HEAD_D, (h + 1) * HEAD_D
                qs.append(q_ref[b, pl.ds(base, CHUNK), lo:hi].astype(F32))
                ks.append(k_ref[b, pl.ds(base, CHUNK), lo:hi].astype(F32))
                vs.append(v_ref[b, pl.ds(base, CHUNK), lo:hi].astype(F32))
                betas.append(beta_all[:, h:h + 1])
                gcs.append(gcum[:, N_HEADS + h:N_HEADS + h + 1])
                grs.append(gcum_t[N_HEADS + h:N_HEADS + h + 1, 0:CHUNK])
                gls.append(gcum[CHUNK - 1:CHUNK, N_HEADS + h:N_HEADS + h + 1])
        qn, kn, v = jnp.stack(qs), jnp.stack(ks), jnp.stack(vs)
        beta, gc, gr, gl = jnp.stack(betas), jnp.stack(gcs), jnp.stack(grs), jnp.stack(gls)
        decay = jnp.where(causal, jnp.exp(jnp.where(causal, gc - gr, 0.0)), 0.0)
        kb = kn * beta
        e_gc = jnp.exp(gc)
        kq = _bmm_nt(jnp.concatenate([kb, qn], axis=1), kn)
        a = jnp.where(strict, kq[:, :CHUNK] * decay, 0.0)
        intra = kq[:, CHUNK:] * decay
        uw = _bmm(_unit_lower_inverse(a), jnp.concatenate([v * beta, kb * e_gc], axis=2))
        u_ref[c] = uw[:, :, :HEAD_D]
        wq_ref[c] = jnp.concatenate([uw[:, :, HEAD_D:], qn * e_gc], axis=1).astype(BF16)
        k_dec = kn * jnp.exp(gl - gc)
        k_dec_t = jnp.stack([jnp.concatenate([k_dec[i], zeros_half], axis=0).T[:, :CHUNK]
                             for i in range(nbh)])
        ik_ref[c] = jnp.concatenate([intra, k_dec_t], axis=1).astype(BF16)
        dec_ref[c] = jnp.broadcast_to(jnp.exp(gl), (nbh, 1, HEAD_D))
        return carry

    def recur(c, carry):
        base = pl.multiple_of(c * CHUNK, CHUNK)
        state = s_ref[...]
        ws = _bmm(wq_ref[c], state)
        v_new = u_ref[c] - ws[:, :CHUNK]
        r = _bmm(ik_ref[c], v_new)
        o = ws[:, CHUNK:] + r[:, :CHUNK]
        s_ref[...] = state * dec_ref[c] + r[:, CHUNK:]
        on = o * lax.rsqrt(jnp.mean(o * o, axis=-1, keepdims=True) + EPS) * onorm_g
        for b in range(nb):
            for h in range(N_HEADS):
                lo, hi = h * HEAD_D, (h + 1) * HEAD_D
                gate = z_ref[b, pl.ds(base, CHUNK), lo:hi].astype(F32)
                o_ref[b, pl.ds(base, CHUNK), lo:hi] = (on[b * N_HEADS + h] * gate).astype(BF16)
        return carry

    lax.fori_loop(0, tb // CHUNK, precompute, 0)
    lax.fori_loop(0, tb // CHUNK, recur, 0)


def _gdn(proj, ba, head_params, batch, seq, tb):
    kd = N_HEADS * HEAD_D
    nc = tb // CHUNK
    nbh = batch * N_HEADS
    proj3 = proj.reshape(batch, seq, proj.shape[1])
    ba3 = ba.reshape(batch, seq, LANES)

    def col(j):
        return pl.BlockSpec((batch, tb, kd), lambda t: (0, t, j))

    out = pl.pallas_call(
        _gdn_kernel,
        grid=(seq // tb,),
        in_specs=[
            col(0), col(1), col(2), col(3),
            pl.BlockSpec((batch, tb, LANES), lambda t: (0, t, 0)),
            pl.BlockSpec((SUBLANES, LANES), lambda t: (0, 0)),
        ],
        out_specs=pl.BlockSpec((batch, tb, kd), lambda t: (0, t, 0)),
        out_shape=jax.ShapeDtypeStruct((batch, seq, kd), BF16),
        scratch_shapes=[
            pltpu.VMEM((nbh, HEAD_D, HEAD_D), F32),
            pltpu.VMEM((nc, nbh, 2 * CHUNK, HEAD_D), BF16),
            pltpu.VMEM((nc, nbh, CHUNK, HEAD_D), F32),
            pltpu.VMEM((nc, nbh, CHUNK + HEAD_D, CHUNK), BF16),
            pltpu.VMEM((nc, nbh, 1, HEAD_D), F32),
        ],
        compiler_params=_params(("arbitrary",), 48),
        name="gdn",
    )(proj3, proj3, proj3, proj3, ba3, head_params)
    return out.reshape(batch * seq, kd)


def _post_kernel(x_ref, og_ref, cb_ref, ga_ref, gb_ref, mod_ref, n2g_ref,
                 wpa_ref, wpb_ref, wout_ref, wr_ref, br_ref,
                 x1_ref, h2_ref, route_ref, cnt_ref, run_ref):
    tm = x_ref.shape[0]
    i = pl.program_id(0)

    @pl.when(i == 0)
    def _():
        run_ref[...] = jnp.zeros(run_ref.shape, F32)

    y_b = _dot(cb_ref[...], wpb_ref[...])
    y_a = _dot(og_ref[...], wpa_ref[...])
    merged = ga_ref[...].astype(F32) * y_a + gb_ref[...].astype(F32) * y_b
    mix = _dot(merged.astype(BF16), wout_ref[...])
    x1 = x_ref[...] + mod_ref[0:1, :] * mix
    x1_ref[...] = x1

    y = x1 * lax.rsqrt(jnp.mean(x1 * x1, axis=-1, keepdims=True) + EPS)
    h2 = (y * n2g_ref[...]) * (1.0 + mod_ref[1:2, :]) + mod_ref[2:3, :]
    h2_ref[...] = h2

    lg = _dot_hi(h2, wr_ref[...]) + br_ref[...]
    lane = lax.broadcasted_iota(jnp.int32, lg.shape, 1).astype(F32)
    neg = jnp.float32(-jnp.inf)
    big = jnp.float32(2 * LANES)

    def first_max(mask):
        vmax = jnp.max(jnp.where(mask, lg, neg), axis=-1, keepdims=True)
        idx = jnp.min(jnp.where(mask & (lg == vmax), lane, big), axis=-1, keepdims=True)
        return vmax, idx

    gmask = lane < N_GROUPS
    g_max, g_sel = first_max(gmask)
    p_group = 1.0 / jnp.sum(jnp.where(gmask, jnp.exp(lg - g_max), 0.0), axis=-1, keepdims=True)
    e_lo = N_GROUPS + EXPERTS_PER_GROUP * g_sel
    emask = (lane >= e_lo) & (lane < e_lo + EXPERTS_PER_GROUP)
    v1, i1 = first_max(emask)
    v2, i2 = first_max(emask & (lane != i1))
    ex = jnp.exp(v2 - v1)
    w1 = p_group * (1.0 / (1.0 + ex))
    w2 = p_group * (ex / (1.0 + ex))
    e1 = i1 - N_GROUPS
    e2 = i2 - N_GROUPS

    onehot = jnp.where((lane == e1) | (lane == e2), 1.0, 0.0).astype(F32)
    row = lax.broadcasted_iota(jnp.int32, (tm, tm), 0)
    col = lax.broadcasted_iota(jnp.int32, (tm, tm), 1)
    before = jnp.where(row > col, 1.0, 0.0).astype(BF16)
    seen = _dot(before, onehot.astype(BF16)) + run_ref[0:1, :]
    r1 = jnp.sum(jnp.where(lane == e1, seen, 0.0), axis=-1, keepdims=True)
    r2 = jnp.sum(jnp.where(lane == e2, seen, 0.0), axis=-1, keepdims=True)
    run_ref[0:1, :] = run_ref[0:1, :] + jnp.sum(onehot, axis=0, keepdims=True)
    cnt_ref[...] = jnp.broadcast_to(run_ref[0:1, :], cnt_ref.shape)

    out = jnp.where(lane == 0, e1, 0.0)
    out = jnp.where(lane == 1, e2, out)
    out = jnp.where(lane == 2, w1, out)
    out = jnp.where(lane == 3, w2, out)
    out = jnp.where(lane == 4, r1, out)
    out = jnp.where(lane == 5, r2, out)
    route_ref[...] = out


def _post(x2, og, act, mod, n2g, wpa, wpb, wout, w_route, b_route, seq, tm):
    m, d = x2.shape
    per_batch = seq // tm

    def rows(j):
        return pl.BlockSpec((tm, d), lambda i: (i, j))

    def whole(shape):
        return pl.BlockSpec(shape, lambda i: tuple(0 for _ in shape))

    return pl.pallas_call(
        _post_kernel,
        grid=(m // tm,),
        in_specs=[
            rows(0), rows(0), rows(4), rows(5), rows(6),
            pl.BlockSpec((None, SUBLANES, d), lambda i: (i // per_batch, 0, 0)),
            whole((1, d)),
            whole((d, d)), whole((d, d)), whole((d, d)),
            whole((d, LANES)), whole((1, LANES)),
        ],
        out_specs=[
            rows(0), rows(0),
            pl.BlockSpec((tm, LANES), lambda i: (i, 0)),
            pl.BlockSpec((SUBLANES, LANES), lambda i: (0, 0)),
        ],
        out_shape=[
            jax.ShapeDtypeStruct((m, d), F32),
            jax.ShapeDtypeStruct((m, d), F32),
            jax.ShapeDtypeStruct((m, LANES), F32),
            jax.ShapeDtypeStruct((SUBLANES, LANES), F32),
        ],
        scratch_shapes=[pltpu.VMEM((SUBLANES, LANES), F32)],
        compiler_params=_params(("arbitrary",), 48),
        name="post",
    )(x2, og, act, act, act, mod, n2g, wpa, wpb, wout, w_route, b_route)


def _plan_kernel(route_ref, cnt_ref, d_ref):
    bm = EXPERT_BLOCK
    sizes = cnt_ref[...]
    padded = jnp.floor((sizes + (bm - 1.0)) * (1.0 / bm)) * bm
    lane_i = lax.broadcasted_iota(jnp.int32, sizes.shape, 1)
    incl = padded
    s = 1
    while s < LANES:
        incl = incl + jnp.where(lane_i >= s, pltpu.roll(incl, s, 1), 0.0)
        s *= 2
    start = (incl - padded)[0:1, :]
    r = route_ref[...]
    lane = lax.broadcasted_iota(jnp.int32, r.shape, 1).astype(F32)
    d1 = jnp.sum(jnp.where(lane == r[:, 0:1], start, 0.0), axis=-1, keepdims=True) + r[:, 4:5]
    d2 = jnp.sum(jnp.where(lane == r[:, 1:2], start, 0.0), axis=-1, keepdims=True) + r[:, 5:6]
    out = jnp.where(lane == 0.0, d1, jnp.where(lane == 1.0, d2, 0.0))
    d_ref[...] = out.T[0:SUBLANES, :].astype(jnp.int32)


def _plan(route, counts, tm):
    m = route.shape[0]
    return pl.pallas_call(
        _plan_kernel,
        grid=(m // tm,),
        in_specs=[
            pl.BlockSpec((tm, LANES), lambda i: (i, 0)),
            pl.BlockSpec((SUBLANES, LANES), lambda i: (0, 0)),
        ],
        out_specs=pl.BlockSpec((SUBLANES, tm), lambda i: (0, i)),
        out_shape=jax.ShapeDtypeStruct((SUBLANES, m), jnp.int32),
        compiler_params=_params(("arbitrary",), 16),
        name="plan",
    )(route, counts)


def _row_copy(src_ref, src_row, dst_ref, dst_row, sem):
    return pltpu.make_async_copy(src_ref.at[pl.ds(src_row, 1), :], dst_ref.at[pl.ds(dst_row, 1), :], sem)


def _rows_copy(src_ref, dst_ref, dst_row, n, sem):
    return pltpu.make_async_copy(src_ref, dst_ref.at[pl.ds(dst_row, n), :], sem)


def _dispatch_kernel(d1_ref, d2_ref, tail_ref, h2_ref, xs_ref, zero_ref, sem):
    tm = h2_ref.shape[0]
    bm = zero_ref.shape[0]
    t0 = pl.program_id(0) * tm

    @pl.when(pl.program_id(0) == 0)
    def _():
        zero_ref[...] = jnp.zeros(zero_ref.shape, F32)
        n_blocks = xs_ref.shape[0] // bm

        def zero_block(row):
            return _rows_copy(zero_ref, xs_ref, pl.multiple_of(row, bm), bm, sem.at[0])

        for e in range(N_EXPERTS):
            @pl.when(tail_ref[e] >= 0)
            def _():
                zero_block(tail_ref[e]).start()
        lax.fori_loop(tail_ref[N_EXPERTS], n_blocks, lambda j, c: (zero_block(j * bm).start(), c)[1], 0)
        for e in range(N_EXPERTS):
            @pl.when(tail_ref[e] >= 0)
            def _():
                zero_block(tail_ref[e]).wait()
        lax.fori_loop(tail_ref[N_EXPERTS], n_blocks, lambda j, c: (zero_block(j * bm).wait(), c)[1], 0)

    def start(r, carry):
        _row_copy(h2_ref, r, xs_ref, d1_ref[t0 + r], sem.at[0]).start()
        _row_copy(h2_ref, r, xs_ref, d2_ref[t0 + r], sem.at[1]).start()
        return carry

    lax.fori_loop(0, tm, start, 0, unroll=DMA_UNROLL)
    _rows_copy(h2_ref, xs_ref, 0, tm, sem.at[0]).wait()
    _rows_copy(h2_ref, xs_ref, 0, tm, sem.at[1]).wait()


def _dispatch(d1, d2, tail_start, h2, cap, tm):
    m, d = h2.shape
    return pl.pallas_call(
        _dispatch_kernel,
        grid_spec=pltpu.PrefetchScalarGridSpec(
            num_scalar_prefetch=3,
            grid=(m // tm,),
            in_specs=[pl.BlockSpec((tm, d), lambda i, d1, d2, tl: (i, 0))],
            out_specs=pl.BlockSpec(memory_space=pl.ANY),
            scratch_shapes=[pltpu.VMEM((EXPERT_BLOCK, d), F32), pltpu.SemaphoreType.DMA((2,))],
        ),
        out_shape=jax.ShapeDtypeStruct((cap, d), F32),
        compiler_params=_params(("arbitrary",), 24),
        name="dispatch",
    )(d1, d2, tail_start, h2)


def _expert_kernel(be_ref, act_ref, x_ref, w1_ref, w3_ref, w2_ref, y_ref, w1b_ref, w3b_ref, w2b_ref):
    b = pl.program_id(0)

    @pl.when(act_ref[b] > 0)
    def _():
        @pl.when((b == 0) | (be_ref[b] != be_ref[jnp.maximum(b - 1, 0)]))
        def _():
            w1b_ref[...] = w1_ref[...].astype(BF16)
            w3b_ref[...] = w3_ref[...].astype(BF16)
            w2b_ref[...] = w2_ref[...].astype(BF16)

        xb = x_ref[...].astype(BF16)
        hid = _silu(_dot(xb, w1b_ref[...])) * _dot(xb, w3b_ref[...])
        y_ref[...] = _dot(hid.astype(BF16), w2b_ref[...])

    @pl.when(act_ref[b] == 0)
    def _():
        y_ref[...] = jnp.zeros(y_ref.shape, F32)


def _experts(block_expert, block_active, xs, w1, w3, w2):
    cap, d = xs.shape
    de = w1.shape[2]
    bm = EXPERT_BLOCK

    def x_block(b, be, act):
        return (jnp.minimum(b, jnp.maximum(act[cap // bm], 1) - 1), 0)

    return pl.pallas_call(
        _expert_kernel,
        grid_spec=pltpu.PrefetchScalarGridSpec(
            num_scalar_prefetch=2,
            grid=(cap // bm,),
            in_specs=[
                pl.BlockSpec((bm, d), x_block),
                pl.BlockSpec((None, d, de), lambda b, be, act: (be[b], 0, 0)),
                pl.BlockSpec((None, d, de), lambda b, be, act: (be[b], 0, 0)),
                pl.BlockSpec((None, de, d), lambda b, be, act: (be[b], 0, 0)),
            ],
            out_specs=pl.BlockSpec((bm, d), lambda b, be, act: (b, 0)),
            scratch_shapes=[pltpu.VMEM((d, de), BF16), pltpu.VMEM((d, de), BF16),
                            pltpu.VMEM((de, d), BF16)],
        ),
        out_shape=jax.ShapeDtypeStruct((cap, d), F32),
        compiler_params=_params(("arbitrary",), 40),
        name="experts",
    )(block_expert, block_active, xs, w1, w3, w2)


def _final_kernel(d1_ref, d2_ref, x1_ref, route_ref, gt_ref, nfg_ref, ys_ref, o_ref, buf_ref, sem):
    tm = x1_ref.shape[0]
    t0 = pl.program_id(0) * tm

    def start(r, carry):
        _row_copy(ys_ref, d1_ref[t0 + r], buf_ref.at[0], r, sem.at[0]).start()
        _row_copy(ys_ref, d2_ref[t0 + r], buf_ref.at[1], r, sem.at[1]).start()
        return carry

    lax.fori_loop(0, tm, start, 0, unroll=DMA_UNROLL)
    for k in range(2):
        pltpu.make_async_copy(ys_ref.at[pl.ds(0, tm), :], buf_ref.at[k], sem.at[k]).wait()
    route = route_ref[...]
    moe = buf_ref[0] * route[:, 2:3] + buf_ref[1] * route[:, 3:4]
    x2 = x1_ref[...] + gt_ref[...] * moe
    y = x2 * lax.rsqrt(jnp.mean(x2 * x2, axis=-1, keepdims=True) + EPS)
    o_ref[...] = y * nfg_ref[...]


def _final(d1, d2, x1, route, gt2, nfg, ys, seq, tm):
    m, d = x1.shape
    per_batch = seq // tm
    return pl.pallas_call(
        _final_kernel,
        grid_spec=pltpu.PrefetchScalarGridSpec(
            num_scalar_prefetch=2,
            grid=(m // tm,),
            in_specs=[
                pl.BlockSpec((tm, d), lambda i, d1, d2: (i, 0)),
                pl.BlockSpec((tm, LANES), lambda i, d1, d2: (i, 0)),
                pl.BlockSpec((None, 1, d), lambda i, d1, d2: (i // per_batch, 0, 0)),
                pl.BlockSpec((1, d), lambda i, d1, d2: (0, 0)),
                pl.BlockSpec(memory_space=pl.ANY),
            ],
            out_specs=pl.BlockSpec((tm, d), lambda i, d1, d2: (i, 0)),
            scratch_shapes=[pltpu.VMEM((2, tm, d), F32), pltpu.SemaphoreType.DMA((2,))],
        ),
        out_shape=jax.ShapeDtypeStruct((m, d), F32),
        compiler_params=_params(("arbitrary",), 32),
        name="final",
    )(d1, d2, x1, route, gt2, nfg, ys)


def _tile(n, pref):
    t = min(n, pref)
    assert n % t == 0
    return t


def kernel(x, c, w_ada, b_ada, norm1_g, w_in, conv_qkv_w, a_log, dt_bias, onorm_g, w_proj_a,
           conv_sc_w, w_proj_b, w_out, norm2_g, w_group, b_group, w_expert, b_expert, w1, w3, w2,
           normf_g):
    batch, seq, d = x.shape
    depth = w_ada.shape[0]
    m = batch * seq
    kd = N_HEADS * HEAD_D
    assert d == kd and seq % CHUNK == 0 and batch <= SUBLANES
    c_pad = jnp.zeros((SUBLANES, d), F32).at[:batch].set(c)
    x2 = x.reshape(m, d)

    for l in range(depth):
        mod = _ada(c_pad, w_ada[l], b_ada[l][None, :])[:batch]
        sh1, sc1, gt1, sh2, sc2, gt2 = [mod[:, None, j * d:(j + 1) * d] for j in range(6)]

        w = w_in[l]
        o_ba = 3 * kd + kd
        w_main = jnp.concatenate([w[:, :o_ba], w[:, o_ba + 2 * N_HEADS:]], axis=1).astype(BF16)
        w_ba = jnp.zeros((d, LANES), F32).at[:, :2 * N_HEADS].set(w[:, o_ba:o_ba + 2 * N_HEADS]).astype(BF16)
        act, ba = _inproj(x2, norm1_g[l][None, :], sc1, sh1, w_main, w_ba, conv_qkv_w[l], conv_sc_w[l],
                          seq, _tile(seq, INPROJ_ROWS))

        head_params = jnp.zeros((SUBLANES, LANES), F32)
        head_params = head_params.at[0, N_HEADS:2 * N_HEADS].set(a_log[l])
        head_params = head_params.at[1, N_HEADS:2 * N_HEADS].set(dt_bias[l])
        head_params = head_params.at[2, :].set(onorm_g[l])
        og = _gdn(act, ba, head_params, batch, seq, _tile(seq, GDN_ROWS))

        mod_post = jnp.zeros((batch, SUBLANES, d), F32)
        mod_post = mod_post.at[:, 0:1].set(gt1).at[:, 1:2].set(sc2).at[:, 2:3].set(sh2)
        w_route = jnp.zeros((d, LANES), F32)
        w_route = w_route.at[:, :N_GROUPS].set(w_group[l]).at[:, N_GROUPS:N_GROUPS + N_EXPERTS].set(w_expert[l])
        b_route = jnp.zeros((1, LANES), F32)
        b_route = b_route.at[0, :N_GROUPS].set(b_group[l]).at[0, N_GROUPS:N_GROUPS + N_EXPERTS].set(b_expert[l])
        x1, h2, route, counts = _post(
            x2, og, act, mod_post, norm2_g[l][None, :],
            w_proj_a[l].astype(BF16), w_proj_b[l].astype(BF16), w_out[l].astype(BF16),
            w_route, b_route, seq, _tile(seq, POST_ROWS))

        bm = EXPERT_BLOCK
        n_blocks = (2 * m) // bm + N_EXPERTS
        sizes = counts[0, :N_EXPERTS].astype(jnp.int32)
        padded = ((sizes + bm - 1) // bm) * bm
        pad_end = jnp.cumsum(padded)
        tail_start = jnp.concatenate([jnp.where(padded > 0, pad_end - bm, -1),
                                      pad_end[-1:] // bm]).astype(jnp.int32)
        block_row = jnp.arange(n_blocks, dtype=jnp.int32) * bm
        block_expert = jnp.minimum(jnp.sum(block_row[:, None] >= pad_end[None, :], axis=1),
                                   N_EXPERTS - 1).astype(jnp.int32)
        block_active = jnp.concatenate([(block_row < pad_end[-1]).astype(jnp.int32),
                                        (pad_end[-1:] // bm).astype(jnp.int32)])
        dest = _plan(route, counts, _tile(seq, INPROJ_ROWS))
        d1, d2 = dest[0], dest[1]

        xs = _dispatch(d1, d2, tail_start, h2, n_blocks * bm, _tile(seq, MOVE_ROWS))
        ys = _experts(block_expert, block_active, xs, w1[l], w3[l], w2[l])
        nfg = normf_g[None, :] if l == depth - 1 else jnp.ones((1, d), F32)
        x2 = _final(d1, d2, x1, route, gt2, nfg, ys, seq, _tile(seq, MOVE_ROWS))
        assert depth == 1
    return x2.reshape(batch, seq, d)
```

```python
import functools

import jax
import jax.numpy as jnp
from jax import lax
from jax.experimental import pallas as pl
from jax.experimental.pallas import tpu as pltpu

F32 = jnp.float32
BF16 = jnp.bfloat16
HIGHEST = lax.Precision.HIGHEST

N_HEADS = 8
HEAD_D = 128
CHUNK = 64
QKV_CONV = 4
SC_CONV = 3
N_GROUPS = 4
EXPERTS_PER_GROUP = 8
N_EXPERTS = N_GROUPS * EXPERTS_PER_GROUP
EPS = 1e-6

LANES = 128
SUBLANES = 8
BF16_ROWS = 16
N_PROJ_COLS = 9
EXPERT_BLOCK = 512
INPROJ_ROWS = 2048
GDN_ROWS = 256
POST_ROWS = 512
MOVE_ROWS = 256
DMA_UNROLL = 8
MIB = 1024 * 1024


def _sigmoid(x):
    return 1.0 / (1.0 + jnp.exp(-x))


def _silu(x):
    return x * _sigmoid(x)


def _softplus(x):
    return jnp.maximum(x, 0.0) + jnp.log(1.0 + jnp.exp(-jnp.abs(x)))


def _dot(a, b):
    return jnp.dot(a, b, preferred_element_type=F32)


def _dot_nt(a, b):
    return lax.dot_general(a, b, (((1,), (1,)), ((), ())), preferred_element_type=F32)


def _dot_hi(a, b):
    return jnp.dot(a, b, preferred_element_type=F32, precision=HIGHEST)


def _dot_split(a, b_hi, b_lo):
    a_hi = a.astype(BF16)
    a_lo = (a - a_hi.astype(F32)).astype(BF16)
    return _dot(a_hi, b_hi) + (_dot(a_lo, b_hi) + _dot(a_hi, b_lo))


def _params(semantics, vmem_mib):
    return pltpu.CompilerParams(dimension_semantics=semantics, vmem_limit_bytes=vmem_mib * MIB)


def _ada_kernel(c_ref, w_ref, b_ref, o_ref):
    o_ref[...] = _dot_hi(_silu(c_ref[...]), w_ref[...]) + b_ref[...]


def _ada(c_pad, w_ada, b_ada):
    d = c_pad.shape[1]
    n = w_ada.shape[1]
    return pl.pallas_call(
        _ada_kernel,
        grid=(n // d,),
        in_specs=[
            pl.BlockSpec((SUBLANES, d), lambda j: (0, 0)),
            pl.BlockSpec((d, d), lambda j: (0, j)),
            pl.BlockSpec((1, d), lambda j: (0, j)),
        ],
        out_specs=pl.BlockSpec((SUBLANES, d), lambda j: (0, j)),
        out_shape=jax.ShapeDtypeStruct((SUBLANES, n), F32),
        compiler_params=_params(("arbitrary",), 24),
        name="ada",
    )(c_pad, w_ada, b_ada)


def _inproj_kernel(x_ref, g_ref, sc_ref, sh_ref, w_ref, wba_ref, o_ref, ba_ref, h_ref):
    @pl.when(pl.program_id(1) == 0)
    def _():
        x = x_ref[...]
        y = x * lax.rsqrt(jnp.mean(x * x, axis=-1, keepdims=True) + EPS)
        h = (y * g_ref[...]) * (1.0 + sc_ref[...]) + sh_ref[...]
        hb = h.astype(BF16)
        h_ref[...] = hb
        ba_ref[...] = _dot(hb, wba_ref[...])

    o_ref[...] = _dot(h_ref[...], w_ref[...]).astype(BF16)


def _inproj(x2, norm_g, sc, sh, w_main, w_ba, seq, tm):
    m, d = x2.shape
    n = w_main.shape[1]
    tn = d
    per_batch = seq // tm
    return pl.pallas_call(
        _inproj_kernel,
        grid=(m // tm, n // tn),
        in_specs=[
            pl.BlockSpec((tm, d), lambda i, j: (i, 0)),
            pl.BlockSpec((1, d), lambda i, j: (0, 0)),
            pl.BlockSpec((None, 1, d), lambda i, j: (i // per_batch, 0, 0)),
            pl.BlockSpec((None, 1, d), lambda i, j: (i // per_batch, 0, 0)),
            pl.BlockSpec((d, tn), lambda i, j: (0, j)),
            pl.BlockSpec((d, LANES), lambda i, j: (0, 0)),
        ],
        out_specs=[
            pl.BlockSpec((tm, tn), lambda i, j: (i, j)),
            pl.BlockSpec((tm, LANES), lambda i, j: (i, 0)),
        ],
        out_shape=[
            jax.ShapeDtypeStruct((m, n), BF16),
            jax.ShapeDtypeStruct((m, LANES), F32),
        ],
        scratch_shapes=[pltpu.VMEM((tm, d), BF16)],
        compiler_params=_params(("arbitrary", "arbitrary"), 56),
        name="inproj",
    )(x2, norm_g, sc, sh, w_main, w_ba)


def _bmm(a, b):
    return jnp.einsum("hmk,hkn->hmn", a.astype(BF16), b.astype(BF16), preferred_element_type=F32)


def _bmm_nt(a, b):
    return jnp.einsum("hmk,hnk->hmn", a.astype(BF16), b.astype(BF16), preferred_element_type=F32)


def _unit_lower_inverse(a):
    row = lax.broadcasted_iota(jnp.int32, a.shape[1:], 0)
    col = lax.broadcasted_iota(jnp.int32, a.shape[1:], 1)
    eye = jnp.where(row == col, 1.0, 0.0).astype(F32)
    p = eye - a
    xp = a
    n = 2
    while n < CHUNK:
        xp = _bmm(xp, xp)
        p = p + _bmm(p, xp)
        n *= 2
    return p


def _causal_conv_silu(win, cw, k_w):
    acc = win * cw[k_w - 1:k_w, :]
    for s in range(1, k_w):
        acc = acc + pltpu.roll(win, s, 0) * cw[k_w - 1 - s:k_w - s, :]
    return _silu(acc[SUBLANES:, :])


def _gdn_kernel(q_ref, k_ref, v_ref, z_ref, ba_ref, cw_ref, hp_ref, o_ref,
                s_ref, tail_ref, wq_ref, u_ref, ik_ref, dec_ref):
    nb, tb = q_ref.shape[0], q_ref.shape[1]
    kd = N_HEADS * HEAD_D
    nbh = nb * N_HEADS

    @pl.when(pl.program_id(0) == 0)
    def _():
        s_ref[...] = jnp.zeros(s_ref.shape, F32)
        tail_ref[...] = jnp.zeros(tail_ref.shape, F32)

    row = lax.broadcasted_iota(jnp.int32, (CHUNK, CHUNK), 0)
    col = lax.broadcasted_iota(jnp.int32, (CHUNK, CHUNK), 1)
    causal = row >= col
    strict = row > col
    tril = jnp.where(causal, 1.0, 0.0).astype(F32)
    a_log = hp_ref[0:1, :]
    dt_bias = hp_ref[1:2, :]
    onorm_g = hp_ref[2:3, :]
    zeros_half = jnp.zeros((CHUNK, HEAD_D), F32)

    def precompute(c, carry):
        base = pl.multiple_of(c * CHUNK, CHUNK)
        qs, ks, vs, betas, gcs, grs, gls = [], [], [], [], [], [], []
        for b in range(nb):
            ba = ba_ref[b, pl.ds(base, CHUNK), :]
            beta_all = _sigmoid(ba)
            g_all = -jnp.exp(a_log) * _softplus(ba + dt_bias)
            gcum = _dot_hi(tril, g_all)
            gcum_t = jnp.concatenate([gcum, gcum], axis=0).T
            for h in range(N_HEADS):
                lo, hi = h * HEAD_D, (h + 1) * HEAD_D

                def conv(ref, off):
                    cur = ref[b, pl.ds(base, CHUNK), lo:hi].astype(F32)
                    win = jnp.concatenate([tail_ref[b, :, off + lo:off + hi], cur], axis=0)
                    return _causal_conv_silu(win, cw_ref[:, off + lo:off + hi], QKV_CONV)

                qs.append(conv(q_ref, 0))
                ks.append(conv(k_ref, kd))
                vs.append(conv(v_ref, 2 * kd))
                betas.append(beta_all[:, h:h + 1])
                gcs.append(gcum[:, N_HEADS + h:N_HEADS + h + 1])
                grs.append(gcum_t[N_HEADS + h:N_HEADS + h + 1, 0:CHUNK])
                gls.append(gcum[CHUNK - 1:CHUNK, N_HEADS + h:N_HEADS + h + 1])
        q, k, v = jnp.stack(qs), jnp.stack(ks), jnp.stack(vs)
        beta, gc, gr, gl = jnp.stack(betas), jnp.stack(gcs), jnp.stack(grs), jnp.stack(gls)
        qn = q * lax.rsqrt(jnp.sum(q * q, axis=-1, keepdims=True) + EPS) * (HEAD_D ** -0.5)
        kn = k * lax.rsqrt(jnp.sum(k * k, axis=-1, keepdims=True) + EPS)
        decay = jnp.where(causal, jnp.exp(jnp.where(causal, gc - gr, 0.0)), 0.0)
        kb = kn * beta
        e_gc = jnp.exp(gc)
        kq = _bmm_nt(jnp.concatenate([kb, qn], axis=1), kn)
        a = jnp.where(strict, kq[:, :CHUNK] * decay, 0.0)
        intra = kq[:, CHUNK:] * decay
        uw = _bmm(_unit_lower_inverse(a), jnp.concatenate([v * beta, kb * e_gc], axis=2))
        u_ref[c] = uw[:, :, :HEAD_D]
        wq_ref[c] = jnp.concatenate([uw[:, :, HEAD_D:], qn * e_gc], axis=1).astype(BF16)
        k_dec = kn * jnp.exp(gl - gc)
        k_dec_t = jnp.stack([jnp.concatenate([k_dec[i], zeros_half], axis=0).T[:, :CHUNK]
                             for i in range(nbh)])
        ik_ref[c] = jnp.concatenate([intra, k_dec_t], axis=1).astype(BF16)
        dec_ref[c] = jnp.broadcast_to(jnp.exp(gl), (nbh, 1, HEAD_D))

        last = pl.multiple_of(base + CHUNK - BF16_ROWS, BF16_ROWS)
        for b in range(nb):
            for j, ref in enumerate((q_ref, k_ref, v_ref)):
                rows = ref[b, pl.ds(last, BF16_ROWS), :].astype(F32)
                tail_ref[b, :, j * kd:(j + 1) * kd] = rows[BF16_ROWS - SUBLANES:, :]
        return carry

    def recur(c, carry):
        base = pl.multiple_of(c * CHUNK, CHUNK)
        state = s_ref[...]
        ws = _bmm(wq_ref[c], state)
        v_new = u_ref[c] - ws[:, :CHUNK]
        r = _bmm(ik_ref[c], v_new)
        o = ws[:, CHUNK:] + r[:, :CHUNK]
        s_ref[...] = state * dec_ref[c] + r[:, CHUNK:]
        on = o * lax.rsqrt(jnp.mean(o * o, axis=-1, keepdims=True) + EPS) * onorm_g
        for b in range(nb):
            for h in range(N_HEADS):
                lo, hi = h * HEAD_D, (h + 1) * HEAD_D
                z = z_ref[b, pl.ds(base, CHUNK), lo:hi].astype(F32)
                o_ref[b, pl.ds(base, CHUNK), lo:hi] = (on[b * N_HEADS + h] * _silu(z)).astype(BF16)
        return carry

    lax.fori_loop(0, tb // CHUNK, precompute, 0)
    lax.fori_loop(0, tb // CHUNK, recur, 0)


def _gdn(proj, ba, conv_w, head_params, batch, seq, tb):
    kd = N_HEADS * HEAD_D
    nc = tb // CHUNK
    nbh = batch * N_HEADS
    proj3 = proj.reshape(batch, seq, proj.shape[1])
    ba3 = ba.reshape(batch, seq, LANES)

    def col(j):
        return pl.BlockSpec((batch, tb, kd), lambda t: (0, t, j))

    out = pl.pallas_call(
        _gdn_kernel,
        grid=(seq // tb,),
        in_specs=[
            col(0), col(1), col(2), col(3),
            pl.BlockSpec((batch, tb, LANES), lambda t: (0, t, 0)),
            pl.BlockSpec((QKV_CONV, 3 * kd), lambda t: (0, 0)),
            pl.BlockSpec((SUBLANES, LANES), lambda t: (0, 0)),
        ],
        out_specs=pl.BlockSpec((batch, tb, kd), lambda t: (0, t, 0)),
        out_shape=jax.ShapeDtypeStruct((batch, seq, kd), BF16),
        scratch_shapes=[
            pltpu.VMEM((nbh, HEAD_D, HEAD_D), F32),
            pltpu.VMEM((batch, SUBLANES, 3 * kd), F32),
            pltpu.VMEM((nc, nbh, 2 * CHUNK, HEAD_D), BF16),
            pltpu.VMEM((nc, nbh, CHUNK, HEAD_D), F32),
            pltpu.VMEM((nc, nbh, CHUNK + HEAD_D, CHUNK), BF16),
            pltpu.VMEM((nc, nbh, 1, HEAD_D), F32),
        ],
        compiler_params=_params(("arbitrary",), 48),
        name="gdn",
    )(proj3, proj3, proj3, proj3, ba3, conv_w, head_params)
    return out.reshape(batch * seq, kd)


def _post_kernel(x_ref, og_ref, sb_ref, sc_ref, sx_ref, ga_ref, gb_ref, mod_ref, n2g_ref, cw_ref,
                 wpa_ref, wpb_ref, wout_ref, wrh_ref, wrl_ref, br_ref,
                 x1_ref, h2_ref, route_ref, cnt_ref, win_ref, run_ref, *, per_batch):
    tm = x_ref.shape[0]
    i = pl.program_id(0)

    @pl.when(i == 0)
    def _():
        run_ref[...] = jnp.zeros(run_ref.shape, F32)

    @pl.when(i % per_batch == 0)
    def _():
        win_ref[0:SUBLANES, :] = jnp.zeros((SUBLANES, win_ref.shape[1]), F32)

    win_ref[SUBLANES:, :] = sc_ref[...].astype(F32) * sx_ref[...].astype(F32)
    conv = win_ref[pl.ds(SUBLANES, tm), :] * cw_ref[SC_CONV - 1:SC_CONV, :]
    for s in range(1, SC_CONV):
        conv = conv + win_ref[pl.ds(SUBLANES - s, tm), :] * cw_ref[SC_CONV - 1 - s:SC_CONV - s, :]
    win_ref[0:SUBLANES, :] = win_ref[pl.ds(tm, SUBLANES), :]
    y_b = _dot((sb_ref[...].astype(F32) * conv).astype(BF16), wpb_ref[...])
    y_a = _dot(og_ref[...], wpa_ref[...])
    merged = _sigmoid(ga_ref[...].astype(F32)) * y_a + _sigmoid(gb_ref[...].astype(F32)) * y_b
    mix = _dot(merged.astype(BF16), wout_ref[...])
    x1 = x_ref[...] + mod_ref[0:1, :] * mix
    x1_ref[...] = x1

    y = x1 * lax.rsqrt(jnp.mean(x1 * x1, axis=-1, keepdims=True) + EPS)
    h2 = (y * n2g_ref[...]) * (1.0 + mod_ref[1:2, :]) + mod_ref[2:3, :]
    h2_ref[...] = h2

    lg = _dot_split(h2, wrh_ref[...], wrl_ref[...]) + br_ref[...]
    lane = lax.broadcasted_iota(jnp.int32, lg.shape, 1).astype(F32)
    neg = jnp.float32(-jnp.inf)
    big = jnp.float32(2 * LANES)

    def first_max(mask):
        vmax = jnp.max(jnp.where(mask, lg, neg), axis=-1, keepdims=True)
        idx = jnp.min(jnp.where(mask & (lg == vmax), lane, big), axis=-1, keepdims=True)
        return vmax, idx

    gmask = lane < N_GROUPS
    g_max, g_sel = first_max(gmask)
    p_group = 1.0 / jnp.sum(jnp.where(gmask, jnp.exp(lg - g_max), 0.0), axis=-1, keepdims=True)
    e_lo = N_GROUPS + EXPERTS_PER_GROUP * g_sel
    emask = (lane >= e_lo) & (lane < e_lo + EXPERTS_PER_GROUP)
    v1, i1 = first_max(emask)
    v2, i2 = first_max(emask & (lane != i1))
    ex = jnp.exp(v2 - v1)
    w1 = p_group * (1.0 / (1.0 + ex))
    w2 = p_group * (ex / (1.0 + ex))
    e1 = i1 - N_GROUPS
    e2 = i2 - N_GROUPS

    onehot = jnp.where((lane == e1) | (lane == e2), 1.0, 0.0).astype(F32)
    row = lax.broadcasted_iota(jnp.int32, (tm, tm), 0)
    col = lax.broadcasted_iota(jnp.int32, (tm, tm), 1)
    before = jnp.where(row > col, 1.0, 0.0).astype(BF16)
    seen = _dot(before, onehot.astype(BF16)) + run_ref[0:1, :]
    r1 = jnp.sum(jnp.where(lane == e1, seen, 0.0), axis=-1, keepdims=True)
    r2 = jnp.sum(jnp.where(lane == e2, seen, 0.0), axis=-1, keepdims=True)
    run_ref[0:1, :] = run_ref[0:1, :] + jnp.sum(onehot, axis=0, keepdims=True)
    cnt_ref[...] = jnp.broadcast_to(run_ref[0:1, :], cnt_ref.shape)

    out = jnp.where(lane == 0, e1, 0.0)
    out = jnp.where(lane == 1, e2, out)
    out = jnp.where(lane == 2, w1, out)
    out = jnp.where(lane == 3, w2, out)
    out = jnp.where(lane == 4, r1, out)
    out = jnp.where(lane == 5, r2, out)
    route_ref[...] = out


def _post(x2, og, proj, mod, n2g, conv_w, wpa, wpb, wout, w_route_hi, w_route_lo, b_route, seq, tm):
    m, d = x2.shape
    per_batch = seq // tm

    def rows(j):
        return pl.BlockSpec((tm, d), lambda i: (i, j))

    def whole(shape):
        return pl.BlockSpec(shape, lambda i: tuple(0 for _ in shape))

    return pl.pallas_call(
        functools.partial(_post_kernel, per_batch=per_batch),
        grid=(m // tm,),
        in_specs=[
            rows(0), rows(0), rows(4), rows(5), rows(6), rows(7), rows(8),
            pl.BlockSpec((None, SUBLANES, d), lambda i: (i // per_batch, 0, 0)),
            whole((1, d)), whole((SC_CONV, d)),
            whole((d, d)), whole((d, d)), whole((d, d)),
            whole((d, LANES)), whole((d, LANES)), whole((1, LANES)),
        ],
        out_specs=[
            rows(0), rows(0),
            pl.BlockSpec((tm, LANES), lambda i: (i, 0)),
            pl.BlockSpec((SUBLANES, LANES), lambda i: (0, 0)),
        ],
        out_shape=[
            jax.ShapeDtypeStruct((m, d), F32),
            jax.ShapeDtypeStruct((m, d), F32),
            jax.ShapeDtypeStruct((m, LANES), F32),
            jax.ShapeDtypeStruct((SUBLANES, LANES), F32),
        ],
        scratch_shapes=[
            pltpu.VMEM((tm + SUBLANES, d), F32),
            pltpu.VMEM((SUBLANES, LANES), F32),
        ],
        compiler_params=_params(("arbitrary",), 56),
        name="post",
    )(x2, og, proj, proj, proj, proj, proj, mod, n2g, conv_w, wpa, wpb, wout, w_route_hi, w_route_lo,
      b_route)


def _plan_kernel(route_ref, cnt_ref, d_ref):
    bm = EXPERT_BLOCK
    sizes = cnt_ref[...]
    padded = jnp.floor((sizes + (bm - 1.0)) * (1.0 / bm)) * bm
    lane_i = lax.broadcasted_iota(jnp.int32, sizes.shape, 1)
    incl = padded
    s = 1
    while s < LANES:
        incl = incl + jnp.where(lane_i >= s, pltpu.roll(incl, s, 1), 0.0)
        s *= 2
    start = (incl - padded)[0:1, :]
    r = route_ref[...]
    lane = lax.broadcasted_iota(jnp.int32, r.shape, 1).astype(F32)
    d1 = jnp.sum(jnp.where(lane == r[:, 0:1], start, 0.0), axis=-1, keepdims=True) + r[:, 4:5]
    d2 = jnp.sum(jnp.where(lane == r[:, 1:2], start, 0.0), axis=-1, keepdims=True) + r[:, 5:6]
    out = jnp.where(lane == 0.0, d1, jnp.where(lane == 1.0, d2, 0.0))
    d_ref[...] = out.T[0:SUBLANES, :].astype(jnp.int32)


def _plan(route, counts, tm):
    m = route.shape[0]
    return pl.pallas_call(
        _plan_kernel,
        grid=(m // tm,),
        in_specs=[
            pl.BlockSpec((tm, LANES), lambda i: (i, 0)),
            pl.BlockSpec((SUBLANES, LANES), lambda i: (0, 0)),
        ],
        out_specs=pl.BlockSpec((SUBLANES, tm), lambda i: (0, i)),
        out_shape=jax.ShapeDtypeStruct((SUBLANES, m), jnp.int32),
        compiler_params=_params(("arbitrary",), 16),
        name="plan",
    )(route, counts)


def _row_copy(src_ref, src_row, dst_ref, dst_row, sem):
    return pltpu.make_async_copy(src_ref.at[pl.ds(src_row, 1), :], dst_ref.at[pl.ds(dst_row, 1), :], sem)


def _rows_copy(src_ref, dst_ref, dst_row, n, sem):
    return pltpu.make_async_copy(src_ref, dst_ref.at[pl.ds(dst_row, n), :], sem)


def _dispatch_kernel(d1_ref, d2_ref, tail_ref, h2_ref, xs_ref, zero_ref, sem):
    tm = h2_ref.shape[0]
    bm = zero_ref.shape[0]
    t0 = pl.program_id(0) * tm

    @pl.when(pl.program_id(0) == 0)
    def _():
        zero_ref[...] = jnp.zeros(zero_ref.shape, F32)
        n_blocks = xs_ref.shape[0] // bm

        def zero_block(row):
            return _rows_copy(zero_ref, xs_ref, pl.multiple_of(row, bm), bm, sem.at[0])

        for e in range(N_EXPERTS):
            @pl.when(tail_ref[e] >= 0)
            def _():
                zero_block(tail_ref[e]).start()
        lax.fori_loop(tail_ref[N_EXPERTS], n_blocks, lambda j, c: (zero_block(j * bm).start(), c)[1], 0)
        for e in range(N_EXPERTS):
            @pl.when(tail_ref[e] >= 0)
            def _():
                zero_block(tail_ref[e]).wait()
        lax.fori_loop(tail_ref[N_EXPERTS], n_blocks, lambda j, c: (zero_block(j * bm).wait(), c)[1], 0)

    def start(r, carry):
        _row_copy(h2_ref, r, xs_ref, d1_ref[t0 + r], sem.at[0]).start()
        _row_copy(h2_ref, r, xs_ref, d2_ref[t0 + r], sem.at[1]).start()
        return carry

    lax.fori_loop(0, tm, start, 0, unroll=DMA_UNROLL)
    _rows_copy(h2_ref, xs_ref, 0, tm, sem.at[0]).wait()
    _rows_copy(h2_ref, xs_ref, 0, tm, sem.at[1]).wait()


def _dispatch(d1, d2, tail_start, h2, cap, tm):
    m, d = h2.shape
    return pl.pallas_call(
        _dispatch_kernel,
        grid_spec=pltpu.PrefetchScalarGridSpec(
            num_scalar_prefetch=3,
            grid=(m // tm,),
            in_specs=[pl.BlockSpec((tm, d), lambda i, d1, d2, tl: (i, 0))],
            out_specs=pl.BlockSpec(memory_space=pl.ANY),
            scratch_shapes=[pltpu.VMEM((EXPERT_BLOCK, d), F32), pltpu.SemaphoreType.DMA((2,))],
        ),
        out_shape=jax.ShapeDtypeStruct((cap, d), F32),
        compiler_params=_params(("arbitrary",), 24),
        name="dispatch",
    )(d1, d2, tail_start, h2)


def _expert_kernel(be_ref, act_ref, x_ref, w1_ref, w3_ref, w2_ref, y_ref, w1b_ref, w3b_ref, w2b_ref):
    b = pl.program_id(0)

    @pl.when(act_ref[b] > 0)
    def _():
        @pl.when((b == 0) | (be_ref[b] != be_ref[jnp.maximum(b - 1, 0)]))
        def _():
            w1b_ref[...] = w1_ref[...].astype(BF16)
            w3b_ref[...] = w3_ref[...].astype(BF16)
            w2b_ref[...] = w2_ref[...].astype(BF16)

        xb = x_ref[...].astype(BF16)
        hid = _silu(_dot(xb, w1b_ref[...])) * _dot(xb, w3b_ref[...])
        y_ref[...] = _dot(hid.astype(BF16), w2b_ref[...])

    @pl.when(act_ref[b] == 0)
    def _():
        y_ref[...] = jnp.zeros(y_ref.shape, F32)


def _experts(block_expert, block_active, xs, w1, w3, w2):
    cap, d = xs.shape
    de = w1.shape[2]
    bm = EXPERT_BLOCK

    def x_block(b, be, act):
        return (jnp.minimum(b, jnp.maximum(act[cap // bm], 1) - 1), 0)

    return pl.pallas_call(
        _expert_kernel,
        grid_spec=pltpu.PrefetchScalarGridSpec(
            num_scalar_prefetch=2,
            grid=(cap // bm,),
            in_specs=[
                pl.BlockSpec((bm, d), x_block),
                pl.BlockSpec((None, d, de), lambda b, be, act: (be[b], 0, 0)),
                pl.BlockSpec((None, d, de), lambda b, be, act: (be[b], 0, 0)),
                pl.BlockSpec((None, de, d), lambda b, be, act: (be[b], 0, 0)),
            ],
            out_specs=pl.BlockSpec((bm, d), lambda b, be, act: (b, 0)),
            scratch_shapes=[pltpu.VMEM((d, de), BF16), pltpu.VMEM((d, de), BF16),
                            pltpu.VMEM((de, d), BF16)],
        ),
        out_shape=jax.ShapeDtypeStruct((cap, d), F32),
        compiler_params=_params(("arbitrary",), 40),
        name="experts",
    )(block_expert, block_active, xs, w1, w3, w2)


def _final_kernel(d1_ref, d2_ref, x1_ref, route_ref, gt_ref, nfg_ref, ys_ref, o_ref, buf_ref, sem):
    tm = x1_ref.shape[0]
    t0 = pl.program_id(0) * tm

    def start(r, carry):
        _row_copy(ys_ref, d1_ref[t0 + r], buf_ref.at[0], r, sem.at[0]).start()
        _row_copy(ys_ref, d2_ref[t0 + r], buf_ref.at[1], r, sem.at[1]).start()
        return carry

    lax.fori_loop(0, tm, start, 0, unroll=DMA_UNROLL)
    for k in range(2):
        pltpu.make_async_copy(ys_ref.at[pl.ds(0, tm), :], buf_ref.at[k], sem.at[k]).wait()
    route = route_ref[...]
    moe = buf_ref[0] * route[:, 2:3] + buf_ref[1] * route[:, 3:4]
    x2 = x1_ref[...] + gt_ref[...] * moe
    y = x2 * lax.rsqrt(jnp.mean(x2 * x2, axis=-1, keepdims=True) + EPS)
    o_ref[...] = y * nfg_ref[...]


def _final(d1, d2, x1, route, gt2, nfg, ys, seq, tm):
    m, d = x1.shape
    per_batch = seq // tm
    return pl.pallas_call(
        _final_kernel,
        grid_spec=pltpu.PrefetchScalarGridSpec(
            num_scalar_prefetch=2,
            grid=(m // tm,),
            in_specs=[
                pl.BlockSpec((tm, d), lambda i, d1, d2: (i, 0)),
                pl.BlockSpec((tm, LANES), lambda i, d1, d2: (i, 0)),
                pl.BlockSpec((None, 1, d), lambda i, d1, d2: (i // per_batch, 0, 0)),
                pl.BlockSpec((1, d), lambda i, d1, d2: (0, 0)),
                pl.BlockSpec(memory_space=pl.ANY),
            ],
            out_specs=pl.BlockSpec((tm, d), lambda i, d1, d2: (i, 0)),
            scratch_shapes=[pltpu.VMEM((2, tm, d), F32), pltpu.SemaphoreType.DMA((2,))],
        ),
        out_shape=jax.ShapeDtypeStruct((m, d), F32),
        compiler_params=_params(("arbitrary",), 32),
        name="final",
    )(d1, d2, x1, route, gt2, nfg, ys)


def _tile(n, pref):
    t = min(n, pref)
    assert n % t == 0
    return t


def kernel(x, c, w_ada, b_ada, norm1_g, w_in, conv_qkv_w, a_log, dt_bias, onorm_g, w_proj_a,
           conv_sc_w, w_proj_b, w_out, norm2_g, w_group, b_group, w_expert, b_expert, w1, w3, w2,
           normf_g):
    batch, seq, d = x.shape
    depth = w_ada.shape[0]
    m = batch * seq
    kd = N_HEADS * HEAD_D
    assert d == kd and seq % CHUNK == 0 and batch <= SUBLANES
    c_pad = jnp.zeros((SUBLANES, d), F32).at[:batch].set(c)
    x2 = x.reshape(m, d)

    for l in range(depth):
        mod = _ada(c_pad, w_ada[l], b_ada[l][None, :])[:batch]
        sh1, sc1, gt1, sh2, sc2, gt2 = [mod[:, None, j * d:(j + 1) * d] for j in range(6)]

        w = w_in[l]
        o_ba = 3 * kd + kd
        w_main = jnp.concatenate([w[:, :o_ba], w[:, o_ba + 2 * N_HEADS:]], axis=1).astype(BF16)
        w_ba = jnp.zeros((d, LANES), F32).at[:, :2 * N_HEADS].set(w[:, o_ba:o_ba + 2 * N_HEADS]).astype(BF16)
        proj, ba = _inproj(x2, norm1_g[l][None, :], sc1, sh1, w_main, w_ba, seq, _tile(seq, INPROJ_ROWS))

        head_params = jnp.zeros((SUBLANES, LANES), F32)
        head_params = head_params.at[0, N_HEADS:2 * N_HEADS].set(a_log[l])
        head_params = head_params.at[1, N_HEADS:2 * N_HEADS].set(dt_bias[l])
        head_params = head_params.at[2, :].set(onorm_g[l])
        og = _gdn(proj, ba, conv_qkv_w[l], head_params, batch, seq, _tile(seq, GDN_ROWS))

        mod_post = jnp.zeros((batch, SUBLANES, d), F32)
        mod_post = mod_post.at[:, 0:1].set(gt1).at[:, 1:2].set(sc2).at[:, 2:3].set(sh2)
        w_route = jnp.zeros((d, LANES), F32)
        w_route = w_route.at[:, :N_GROUPS].set(w_group[l]).at[:, N_GROUPS:N_GROUPS + N_EXPERTS].set(w_expert[l])
        w_route_hi = w_route.astype(BF16)
        w_route_lo = (w_route - w_route_hi.astype(F32)).astype(BF16)
        b_route = jnp.zeros((1, LANES), F32)
        b_route = b_route.at[0, :N_GROUPS].set(b_group[l]).at[0, N_GROUPS:N_GROUPS + N_EXPERTS].set(b_expert[l])
        x1, h2, route, counts = _post(
            x2, og, proj, mod_post, norm2_g[l][None, :], conv_sc_w[l],
            w_proj_a[l].astype(BF16), w_proj_b[l].astype(BF16), w_out[l].astype(BF16),
            w_route_hi, w_route_lo, b_route, seq, _tile(seq, POST_ROWS))

        bm = EXPERT_BLOCK
        n_blocks = (2 * m) // bm + N_EXPERTS
        sizes = counts[0, :N_EXPERTS].astype(jnp.int32)
        padded = ((sizes + bm - 1) // bm) * bm
        pad_end = jnp.cumsum(padded)
        tail_start = jnp.concatenate([jnp.where(padded > 0, pad_end - bm, -1),
                                      pad_end[-1:] // bm]).astype(jnp.int32)
        block_row = jnp.arange(n_blocks, dtype=jnp.int32) * bm
        block_expert = jnp.minimum(jnp.sum(block_row[:, None] >= pad_end[None, :], axis=1),
                                   N_EXPERTS - 1).astype(jnp.int32)
        block_active = jnp.concatenate([(block_row < pad_end[-1]).astype(jnp.int32),
                                        (pad_end[-1:] // bm).astype(jnp.int32)])
        dest = _plan(route, counts, _tile(seq, INPROJ_ROWS))
        d1, d2 = dest[0], dest[1]

        xs = _dispatch(d1, d2, tail_start, h2, n_blocks * bm, _tile(seq, MOVE_ROWS))
        ys = _experts(block_expert, block_active, xs, w1[l], w3[l], w2[l])
        nfg = normf_g[None, :] if l == depth - 1 else jnp.ones((1, d), F32)
        x2 = _final(d1, d2, x1, route, gt2, nfg, ys, seq, _tile(seq, MOVE_ROWS))
        assert depth == 1
    return x2.reshape(batch, seq, d)
```

```python
import functools

import jax
import jax.numpy as jnp
from jax import lax
from jax.experimental import pallas as pl
from jax.experimental.pallas import tpu as pltpu

F32 = jnp.float32
BF16 = jnp.bfloat16
HIGHEST = lax.Precision.HIGHEST

N_HEADS = 8
HEAD_D = 128
CHUNK = 64
QKV_CONV = 4
SC_CONV = 3
N_GROUPS = 4
EXPERTS_PER_GROUP = 8
N_EXPERTS = N_GROUPS * EXPERTS_PER_GROUP
EPS = 1e-6

LANES = 128
SUBLANES = 8
BF16_ROWS = 16
N_PROJ_COLS = 9
EXPERT_BLOCK = 512
INPROJ_ROWS = 2048
GDN_ROWS = 256
POST_ROWS = 512
MOVE_ROWS = 256
DMA_UNROLL = 8
MIB = 1024 * 1024


def _sigmoid(x):
    return 1.0 / (1.0 + jnp.exp(-x))


def _silu(x):
    half = 0.5 * x
    return half + half * jnp.tanh(half)


def _softplus(x):
    return jnp.maximum(x, 0.0) + jnp.log(1.0 + jnp.exp(-jnp.abs(x)))


def _dot(a, b):
    return jnp.dot(a, b, preferred_element_type=F32)


def _dot_nt(a, b):
    return lax.dot_general(a, b, (((1,), (1,)), ((), ())), preferred_element_type=F32)


def _dot_hi(a, b):
    return jnp.dot(a, b, preferred_element_type=F32, precision=HIGHEST)


def _dot_split(a, b_hi, b_lo):
    a_hi = a.astype(BF16)
    a_lo = (a - a_hi.astype(F32)).astype(BF16)
    return _dot(a_hi, b_hi) + (_dot(a_lo, b_hi) + _dot(a_hi, b_lo))


def _params(semantics, vmem_mib):
    return pltpu.CompilerParams(dimension_semantics=semantics, vmem_limit_bytes=vmem_mib * MIB)


def _ada_kernel(c_ref, w_ref, b_ref, o_ref):
    o_ref[...] = _dot_hi(_silu(c_ref[...]), w_ref[...]) + b_ref[...]


def _ada(c_pad, w_ada, b_ada):
    d = c_pad.shape[1]
    n = w_ada.shape[1]
    return pl.pallas_call(
        _ada_kernel,
        grid=(n // d,),
        in_specs=[
            pl.BlockSpec((SUBLANES, d), lambda j: (0, 0)),
            pl.BlockSpec((d, d), lambda j: (0, j)),
            pl.BlockSpec((1, d), lambda j: (0, j)),
        ],
        out_specs=pl.BlockSpec((SUBLANES, d), lambda j: (0, j)),
        out_shape=jax.ShapeDtypeStruct((SUBLANES, n), F32),
        compiler_params=_params(("arbitrary",), 24),
        name="ada",
    )(c_pad, w_ada, b_ada)


def _inproj_kernel(x_ref, g_ref, sc_ref, sh_ref, w_ref, wba_ref, o_ref, ba_ref, h_ref):
    @pl.when(pl.program_id(1) == 0)
    def _():
        x = x_ref[...]
        y = x * lax.rsqrt(jnp.mean(x * x, axis=-1, keepdims=True) + EPS)
        h = (y * g_ref[...]) * (1.0 + sc_ref[...]) + sh_ref[...]
        hb = h.astype(BF16)
        h_ref[...] = hb
        ba_ref[...] = _dot(hb, wba_ref[...])

    o_ref[...] = _dot(h_ref[...], w_ref[...]).astype(BF16)


def _inproj(x2, norm_g, sc, sh, w_main, w_ba, seq, tm):
    m, d = x2.shape
    n = w_main.shape[1]
    tn = d
    per_batch = seq // tm
    return pl.pallas_call(
        _inproj_kernel,
        grid=(m // tm, n // tn),
        in_specs=[
            pl.BlockSpec((tm, d), lambda i, j: (i, 0)),
            pl.BlockSpec((1, d), lambda i, j: (0, 0)),
            pl.BlockSpec((None, 1, d), lambda i, j: (i // per_batch, 0, 0)),
            pl.BlockSpec((None, 1, d), lambda i, j: (i // per_batch, 0, 0)),
            pl.BlockSpec((d, tn), lambda i, j: (0, j)),
            pl.BlockSpec((d, LANES), lambda i, j: (0, 0)),
        ],
        out_specs=[
            pl.BlockSpec((tm, tn), lambda i, j: (i, j)),
            pl.BlockSpec((tm, LANES), lambda i, j: (i, 0)),
        ],
        out_shape=[
            jax.ShapeDtypeStruct((m, n), BF16),
            jax.ShapeDtypeStruct((m, LANES), F32),
        ],
        scratch_shapes=[pltpu.VMEM((tm, d), BF16)],
        compiler_params=_params(("arbitrary", "arbitrary"), 56),
        name="inproj",
    )(x2, norm_g, sc, sh, w_main, w_ba)


def _bmm(a, b):
    return jnp.einsum("hmk,hkn->hmn", a.astype(BF16), b.astype(BF16), preferred_element_type=F32)


def _bmm_nt(a, b):
    return jnp.einsum("hmk,hnk->hmn", a.astype(BF16), b.astype(BF16), preferred_element_type=F32)


def _unit_lower_inverse(a):
    row = lax.broadcasted_iota(jnp.int32, a.shape[1:], 0)
    col = lax.broadcasted_iota(jnp.int32, a.shape[1:], 1)
    eye = jnp.where(row == col, 1.0, 0.0).astype(F32)
    p = eye - a
    xp = a
    n = 2
    while n < CHUNK:
        xp = _bmm(xp, xp)
        p = p + _bmm(p, xp)
        n *= 2
    return p


def _lane_sums(x):
    h, rows, width = x.shape
    ones = jnp.ones((width, width), BF16)
    return _dot(x.reshape(h * rows, width).astype(BF16), ones).reshape(h, rows, width)


def _causal_conv_silu(win, cw, k_w):
    assert k_w == 4
    tiled = (win.shape[0] // SUBLANES, SUBLANES, win.shape[1])

    def pair(x, x1, j):
        return (x.reshape(tiled) * cw[j][None] + x1.reshape(tiled) * cw[j - 1][None]).reshape(win.shape)

    win1 = pltpu.roll(win, 1, 0)
    acc = pair(win, win1, 3) + pltpu.roll(pair(win, win1, 1), 2, 0)
    return _silu(acc[SUBLANES:, :])


def _gdn_kernel(q_ref, k_ref, v_ref, z_ref, ba_ref, cw_ref, hp_ref, o_ref,
                s_ref, tail_ref, wq_ref, u_ref, ik_ref, dec_ref):
    nb, tb = q_ref.shape[0], q_ref.shape[1]
    kd = N_HEADS * HEAD_D
    nbh = nb * N_HEADS

    @pl.when(pl.program_id(0) == 0)
    def _():
        s_ref[...] = jnp.zeros(s_ref.shape, F32)
        tail_ref[...] = jnp.zeros(tail_ref.shape, F32)

    row = lax.broadcasted_iota(jnp.int32, (CHUNK, CHUNK), 0)
    col = lax.broadcasted_iota(jnp.int32, (CHUNK, CHUNK), 1)
    causal = row >= col
    strict = row > col
    tril = jnp.where(causal, 1.0, 0.0).astype(F32)
    a_log = hp_ref[0:1, :]
    dt_bias = hp_ref[1:2, :]
    onorm_g = hp_ref[2:3, :]
    zeros_half = jnp.zeros((CHUNK, HEAD_D), F32)

    def precompute(c, carry):
        base = pl.multiple_of(c * CHUNK, CHUNK)
        qs, ks, vs, betas, gcs, grs, gls = [], [], [], [], [], [], []
        for b in range(nb):
            ba = ba_ref[b, pl.ds(base, CHUNK), :]
            beta_all = _sigmoid(ba)
            g_all = -jnp.exp(a_log) * _softplus(ba + dt_bias)
            gcum = _dot_hi(tril, g_all)
            gcum_t = jnp.concatenate([gcum, gcum], axis=0).T
            for h in range(N_HEADS):
                lo, hi = h * HEAD_D, (h + 1) * HEAD_D

                def conv(ref, off):
                    cur = ref[b, pl.ds(base, CHUNK), lo:hi].astype(F32)
                    win = jnp.concatenate([tail_ref[b, :, off + lo:off + hi], cur], axis=0)
                    return _causal_conv_silu(win, cw_ref[:, :, off + lo:off + hi], QKV_CONV)

                qs.append(conv(q_ref, 0))
                ks.append(conv(k_ref, kd))
                vs.append(conv(v_ref, 2 * kd))
                betas.append(beta_all[:, h:h + 1])
                gcs.append(gcum[:, N_HEADS + h:N_HEADS + h + 1])
                grs.append(gcum_t[N_HEADS + h:N_HEADS + h + 1, 0:CHUNK])
                gls.append(gcum[CHUNK - 1:CHUNK, N_HEADS + h:N_HEADS + h + 1])
        q, k, v = jnp.stack(qs), jnp.stack(ks), jnp.stack(vs)
        beta, gc, gr, gl = jnp.stack(betas), jnp.stack(gcs), jnp.stack(grs), jnp.stack(gls)
        qn = q * (lax.rsqrt(_lane_sums(q * q) + EPS) * (HEAD_D ** -0.5))
        kn = k * lax.rsqrt(_lane_sums(k * k) + EPS)
        decay = jnp.where(causal, jnp.exp(jnp.where(causal, gc - gr, 0.0)), 0.0)
        kb = kn * beta
        e_gc = jnp.exp(gc)
        kq = _bmm_nt(jnp.concatenate([kb, qn], axis=1), kn)
        a = jnp.where(strict, kq[:, :CHUNK] * decay, 0.0)
        intra = kq[:, CHUNK:] * decay
        uw = _bmm(_unit_lower_inverse(a), jnp.concatenate([v * beta, kb * e_gc], axis=2))
        u_ref[c] = uw[:, :, :HEAD_D]
        wq_ref[c] = jnp.concatenate([uw[:, :, HEAD_D:], qn * e_gc], axis=1).astype(BF16)
        k_dec = kn * jnp.exp(gl - gc)
        k_dec_t = jnp.stack([jnp.concatenate([k_dec[i], zeros_half], axis=0).T[:, :CHUNK]
                             for i in range(nbh)])
        ik_ref[c] = jnp.concatenate([intra, k_dec_t], axis=1).astype(BF16)
        dec_ref[c] = jnp.broadcast_to(jnp.exp(gl), (nbh, 1, HEAD_D))

        last = pl.multiple_of(base + CHUNK - BF16_ROWS, BF16_ROWS)
        for b in range(nb):
            for j, ref in enumerate((q_ref, k_ref, v_ref)):
                rows = ref[b, pl.ds(last, BF16_ROWS), :].astype(F32)
                tail_ref[b, :, j * kd:(j + 1) * kd] = rows[BF16_ROWS - SUBLANES:, :]
        return carry

    def recur(c, carry):
        base = pl.multiple_of(c * CHUNK, CHUNK)
        state = s_ref[...]
        ws = _bmm(wq_ref[c], state)
        v_new = u_ref[c] - ws[:, :CHUNK]
        r = _bmm(ik_ref[c], v_new)
        o = ws[:, CHUNK:] + r[:, :CHUNK]
        s_ref[...] = state * dec_ref[c] + r[:, CHUNK:]
        on = o * lax.rsqrt(jnp.mean(o * o, axis=-1, keepdims=True) + EPS) * onorm_g
        for b in range(nb):
            for h in range(N_HEADS):
                lo, hi = h * HEAD_D, (h + 1) * HEAD_D
                z = z_ref[b, pl.ds(base, CHUNK), lo:hi].astype(F32)
                o_ref[b, pl.ds(base, CHUNK), lo:hi] = (on[b * N_HEADS + h] * _silu(z)).astype(BF16)
        return carry

    lax.fori_loop(0, tb // CHUNK, precompute, 0)
    lax.fori_loop(0, tb // CHUNK, recur, 0)


def _gdn(proj, ba, conv_w, head_params, batch, seq, tb):
    kd = N_HEADS * HEAD_D
    nc = tb // CHUNK
    nbh = batch * N_HEADS
    proj3 = proj.reshape(batch, seq, proj.shape[1])
    ba3 = ba.reshape(batch, seq, LANES)

    def col(j):
        return pl.BlockSpec((batch, tb, kd), lambda t: (0, t, j))

    out = pl.pallas_call(
        _gdn_kernel,
        grid=(seq // tb,),
        in_specs=[
            col(0), col(1), col(2), col(3),
            pl.BlockSpec((batch, tb, LANES), lambda t: (0, t, 0)),
            pl.BlockSpec((QKV_CONV, SUBLANES, 3 * kd), lambda t: (0, 0, 0)),
            pl.BlockSpec((SUBLANES, LANES), lambda t: (0, 0)),
        ],
        out_specs=pl.BlockSpec((batch, tb, kd), lambda t: (0, t, 0)),
        out_shape=jax.ShapeDtypeStruct((batch, seq, kd), BF16),
        scratch_shapes=[
            pltpu.VMEM((nbh, HEAD_D, HEAD_D), F32),
            pltpu.VMEM((batch, SUBLANES, 3 * kd), F32),
            pltpu.VMEM((nc, nbh, 2 * CHUNK, HEAD_D), BF16),
            pltpu.VMEM((nc, nbh, CHUNK, HEAD_D), F32),
            pltpu.VMEM((nc, nbh, CHUNK + HEAD_D, CHUNK), BF16),
            pltpu.VMEM((nc, nbh, 1, HEAD_D), F32),
        ],
        compiler_params=_params(("arbitrary",), 48),
        name="gdn",
    )(proj3, proj3, proj3, proj3, ba3, conv_w, head_params)
    return out.reshape(batch * seq, kd)


def _post_kernel(x_ref, og_ref, sb_ref, sc_ref, sx_ref, ga_ref, gb_ref, mod_ref, n2g_ref, cw_ref,
                 wpa_ref, wpb_ref, wout_ref, wrh_ref, wrl_ref, br_ref,
                 x1_ref, h2_ref, route_ref, cnt_ref, win_ref, run_ref, *, per_batch):
    tm = x_ref.shape[0]
    i = pl.program_id(0)

    @pl.when(i == 0)
    def _():
        run_ref[...] = jnp.zeros(run_ref.shape, F32)

    @pl.when(i % per_batch == 0)
    def _():
        win_ref[0:SUBLANES, :] = jnp.zeros((SUBLANES, win_ref.shape[1]), F32)

    win_ref[SUBLANES:, :] = sc_ref[...].astype(F32) * sx_ref[...].astype(F32)
    conv = win_ref[pl.ds(SUBLANES, tm), :] * cw_ref[SC_CONV - 1:SC_CONV, :]
    for s in range(1, SC_CONV):
        conv = conv + win_ref[pl.ds(SUBLANES - s, tm), :] * cw_ref[SC_CONV - 1 - s:SC_CONV - s, :]
    win_ref[0:SUBLANES, :] = win_ref[pl.ds(tm, SUBLANES), :]
    y_b = _dot((sb_ref[...].astype(F32) * conv).astype(BF16), wpb_ref[...])
    y_a = _dot(og_ref[...], wpa_ref[...])
    merged = _sigmoid(ga_ref[...].astype(F32)) * y_a + _sigmoid(gb_ref[...].astype(F32)) * y_b
    mix = _dot(merged.astype(BF16), wout_ref[...])
    x1 = x_ref[...] + mod_ref[0:1, :] * mix
    x1_ref[...] = x1

    y = x1 * lax.rsqrt(jnp.mean(x1 * x1, axis=-1, keepdims=True) + EPS)
    h2 = (y * n2g_ref[...]) * (1.0 + mod_ref[1:2, :]) + mod_ref[2:3, :]
    h2_ref[...] = h2

    lg = _dot_split(h2, wrh_ref[...], wrl_ref[...]) + br_ref[...]
    lane = lax.broadcasted_iota(jnp.int32, lg.shape, 1).astype(F32)
    neg = jnp.float32(-jnp.inf)
    big = jnp.float32(2 * LANES)

    def first_max(mask):
        vmax = jnp.max(jnp.where(mask, lg, neg), axis=-1, keepdims=True)
        idx = jnp.min(jnp.where(mask & (lg == vmax), lane, big), axis=-1, keepdims=True)
        return vmax, idx

    gmask = lane < N_GROUPS
    g_max, g_sel = first_max(gmask)
    p_group = 1.0 / jnp.sum(jnp.where(gmask, jnp.exp(lg - g_max), 0.0), axis=-1, keepdims=True)
    e_lo = N_GROUPS + EXPERTS_PER_GROUP * g_sel
    emask = (lane >= e_lo) & (lane < e_lo + EXPERTS_PER_GROUP)
    v1, i1 = first_max(emask)
    v2, i2 = first_max(emask & (lane != i1))
    ex = jnp.exp(v2 - v1)
    w1 = p_group * (1.0 / (1.0 + ex))
    w2 = p_group * (ex / (1.0 + ex))
    e1 = i1 - N_GROUPS
    e2 = i2 - N_GROUPS

    onehot = jnp.where((lane == e1) | (lane == e2), 1.0, 0.0).astype(F32)
    row = lax.broadcasted_iota(jnp.int32, (tm, tm), 0)
    col = lax.broadcasted_iota(jnp.int32, (tm, tm), 1)
    before = jnp.where(row > col, 1.0, 0.0).astype(BF16)
    seen = _dot(before, onehot.astype(BF16)) + run_ref[0:1, :]
    r1 = jnp.sum(jnp.where(lane == e1, seen, 0.0), axis=-1, keepdims=True)
    r2 = jnp.sum(jnp.where(lane == e2, seen, 0.0), axis=-1, keepdims=True)
    run_ref[0:1, :] = run_ref[0:1, :] + jnp.sum(onehot, axis=0, keepdims=True)
    cnt_ref[...] = jnp.broadcast_to(run_ref[0:1, :], cnt_ref.shape)

    out = jnp.where(lane == 0, e1, 0.0)
    out = jnp.where(lane == 1, e2, out)
    out = jnp.where(lane == 2, w1, out)
    out = jnp.where(lane == 3, w2, out)
    out = jnp.where(lane == 4, r1, out)
    out = jnp.where(lane == 5, r2, out)
    route_ref[...] = out


def _post(x2, og, proj, mod, n2g, conv_w, wpa, wpb, wout, w_route_hi, w_route_lo, b_route, seq, tm):
    m, d = x2.shape
    per_batch = seq // tm

    def rows(j):
        return pl.BlockSpec((tm, d), lambda i: (i, j))

    def whole(shape):
        return pl.BlockSpec(shape, lambda i: tuple(0 for _ in shape))

    return pl.pallas_call(
        functools.partial(_post_kernel, per_batch=per_batch),
        grid=(m // tm,),
        in_specs=[
            rows(0), rows(0), rows(4), rows(5), rows(6), rows(7), rows(8),
            pl.BlockSpec((None, SUBLANES, d), lambda i: (i // per_batch, 0, 0)),
            whole((1, d)), whole((SC_CONV, d)),
            whole((d, d)), whole((d, d)), whole((d, d)),
            whole((d, LANES)), whole((d, LANES)), whole((1, LANES)),
        ],
        out_specs=[
            rows(0), rows(0),
            pl.BlockSpec((tm, LANES), lambda i: (i, 0)),
            pl.BlockSpec((SUBLANES, LANES), lambda i: (0, 0)),
        ],
        out_shape=[
            jax.ShapeDtypeStruct((m, d), F32),
            jax.ShapeDtypeStruct((m, d), F32),
            jax.ShapeDtypeStruct((m, LANES), F32),
            jax.ShapeDtypeStruct((SUBLANES, LANES), F32),
        ],
        scratch_shapes=[
            pltpu.VMEM((tm + SUBLANES, d), F32),
            pltpu.VMEM((SUBLANES, LANES), F32),
        ],
        compiler_params=_params(("arbitrary",), 56),
        name="post",
    )(x2, og, proj, proj, proj, proj, proj, mod, n2g, conv_w, wpa, wpb, wout, w_route_hi, w_route_lo,
      b_route)


def _plan_kernel(route_ref, cnt_ref, d_ref):
    bm = EXPERT_BLOCK
    sizes = cnt_ref[...]
    padded = jnp.floor((sizes + (bm - 1.0)) * (1.0 / bm)) * bm
    lane_i = lax.broadcasted_iota(jnp.int32, sizes.shape, 1)
    incl = padded
    s = 1
    while s < LANES:
        incl = incl + jnp.where(lane_i >= s, pltpu.roll(incl, s, 1), 0.0)
        s *= 2
    start = (incl - padded)[0:1, :]
    r = route_ref[...]
    lane = lax.broadcasted_iota(jnp.int32, r.shape, 1).astype(F32)
    d1 = jnp.sum(jnp.where(lane == r[:, 0:1], start, 0.0), axis=-1, keepdims=True) + r[:, 4:5]
    d2 = jnp.sum(jnp.where(lane == r[:, 1:2], start, 0.0), axis=-1, keepdims=True) + r[:, 5:6]
    out = jnp.where(lane == 0.0, d1, jnp.where(lane == 1.0, d2, 0.0))
    d_ref[...] = out.T[0:SUBLANES, :].astype(jnp.int32)


def _plan(route, counts, tm):
    m = route.shape[0]
    return pl.pallas_call(
        _plan_kernel,
        grid=(m // tm,),
        in_specs=[
            pl.BlockSpec((tm, LANES), lambda i: (i, 0)),
            pl.BlockSpec((SUBLANES, LANES), lambda i: (0, 0)),
        ],
        out_specs=pl.BlockSpec((SUBLANES, tm), lambda i: (0, i)),
        out_shape=jax.ShapeDtypeStruct((SUBLANES, m), jnp.int32),
        compiler_params=_params(("arbitrary",), 16),
        name="plan",
    )(route, counts)


def _row_copy(src_ref, src_row, dst_ref, dst_row, sem):
    return pltpu.make_async_copy(src_ref.at[pl.ds(src_row, 1), :], dst_ref.at[pl.ds(dst_row, 1), :], sem)


def _rows_copy(src_ref, dst_ref, dst_row, n, sem):
    return pltpu.make_async_copy(src_ref, dst_ref.at[pl.ds(dst_row, n), :], sem)


def _dispatch_kernel(d1_ref, d2_ref, tail_ref, h2_ref, xs_ref, zero_ref, sem):
    tm = h2_ref.shape[0]
    bm = zero_ref.shape[0]
    t0 = pl.program_id(0) * tm

    @pl.when(pl.program_id(0) == 0)
    def _():
        zero_ref[...] = jnp.zeros(zero_ref.shape, F32)
        n_blocks = xs_ref.shape[0] // bm

        def zero_block(row):
            return _rows_copy(zero_ref, xs_ref, pl.multiple_of(row, bm), bm, sem.at[0])

        for e in range(N_EXPERTS):
            @pl.when(tail_ref[e] >= 0)
            def _():
                zero_block(tail_ref[e]).start()
        lax.fori_loop(tail_ref[N_EXPERTS], n_blocks, lambda j, c: (zero_block(j * bm).start(), c)[1], 0)
        for e in range(N_EXPERTS):
            @pl.when(tail_ref[e] >= 0)
            def _():
                zero_block(tail_ref[e]).wait()
        lax.fori_loop(tail_ref[N_EXPERTS], n_blocks, lambda j, c: (zero_block(j * bm).wait(), c)[1], 0)

    def start(r, carry):
        _row_copy(h2_ref, r, xs_ref, d1_ref[t0 + r], sem.at[0]).start()
        _row_copy(h2_ref, r, xs_ref, d2_ref[t0 + r], sem.at[1]).start()
        return carry

    lax.fori_loop(0, tm, start, 0, unroll=DMA_UNROLL)
    _rows_copy(h2_ref, xs_ref, 0, tm, sem.at[0]).wait()
    _rows_copy(h2_ref, xs_ref, 0, tm, sem.at[1]).wait()


def _dispatch(d1, d2, tail_start, h2, cap, tm):
    m, d = h2.shape
    return pl.pallas_call(
        _dispatch_kernel,
        grid_spec=pltpu.PrefetchScalarGridSpec(
            num_scalar_prefetch=3,
            grid=(m // tm,),
            in_specs=[pl.BlockSpec((tm, d), lambda i, d1, d2, tl: (i, 0))],
            out_specs=pl.BlockSpec(memory_space=pl.ANY),
            scratch_shapes=[pltpu.VMEM((EXPERT_BLOCK, d), F32), pltpu.SemaphoreType.DMA((2,))],
        ),
        out_shape=jax.ShapeDtypeStruct((cap, d), F32),
        compiler_params=_params(("arbitrary",), 24),
        name="dispatch",
    )(d1, d2, tail_start, h2)


def _expert_kernel(be_ref, act_ref, x_ref, w1_ref, w3_ref, w2_ref, y_ref, w1b_ref, w3b_ref, w2b_ref):
    b = pl.program_id(0)

    @pl.when(act_ref[b] > 0)
    def _():
        @pl.when((b == 0) | (be_ref[b] != be_ref[jnp.maximum(b - 1, 0)]))
        def _():
            w1b_ref[...] = w1_ref[...].astype(BF16)
            w3b_ref[...] = w3_ref[...].astype(BF16)
            w2b_ref[...] = w2_ref[...].astype(BF16)

        xb = x_ref[...].astype(BF16)
        hid = _silu(_dot(xb, w1b_ref[...])) * _dot(xb, w3b_ref[...])
        y_ref[...] = _dot(hid.astype(BF16), w2b_ref[...])

    @pl.when(act_ref[b] == 0)
    def _():
        y_ref[...] = jnp.zeros(y_ref.shape, F32)


def _experts(block_expert, block_active, xs, w1, w3, w2):
    cap, d = xs.shape
    de = w1.shape[2]
    bm = EXPERT_BLOCK

    def x_block(b, be, act):
        return (jnp.minimum(b, jnp.maximum(act[cap // bm], 1) - 1), 0)

    return pl.pallas_call(
        _expert_kernel,
        grid_spec=pltpu.PrefetchScalarGridSpec(
            num_scalar_prefetch=2,
            grid=(cap // bm,),
            in_specs=[
                pl.BlockSpec((bm, d), x_block),
                pl.BlockSpec((None, d, de), lambda b, be, act: (be[b], 0, 0)),
                pl.BlockSpec((None, d, de), lambda b, be, act: (be[b], 0, 0)),
                pl.BlockSpec((None, de, d), lambda b, be, act: (be[b], 0, 0)),
            ],
            out_specs=pl.BlockSpec((bm, d), lambda b, be, act: (b, 0)),
            scratch_shapes=[pltpu.VMEM((d, de), BF16), pltpu.VMEM((d, de), BF16),
                            pltpu.VMEM((de, d), BF16)],
        ),
        out_shape=jax.ShapeDtypeStruct((cap, d), F32),
        compiler_params=_params(("arbitrary",), 40),
        name="experts",
    )(block_expert, block_active, xs, w1, w3, w2)


def _final_kernel(d1_ref, d2_ref, x1_ref, route_ref, gt_ref, nfg_ref, ys_ref, o_ref, buf_ref, sem):
    tm = x1_ref.shape[0]
    t0 = pl.program_id(0) * tm

    def start(r, carry):
        _row_copy(ys_ref, d1_ref[t0 + r], buf_ref.at[0], r, sem.at[0]).start()
        _row_copy(ys_ref, d2_ref[t0 + r], buf_ref.at[1], r, sem.at[1]).start()
        return carry

    lax.fori_loop(0, tm, start, 0, unroll=DMA_UNROLL)
    for k in range(2):
        pltpu.make_async_copy(ys_ref.at[pl.ds(0, tm), :], buf_ref.at[k], sem.at[k]).wait()
    route = route_ref[...]
    moe = buf_ref[0] * route[:, 2:3] + buf_ref[1] * route[:, 3:4]
    x2 = x1_ref[...] + gt_ref[...] * moe
    y = x2 * lax.rsqrt(jnp.mean(x2 * x2, axis=-1, keepdims=True) + EPS)
    o_ref[...] = y * nfg_ref[...]


def _final(d1, d2, x1, route, gt2, nfg, ys, seq, tm):
    m, d = x1.shape
    per_batch = seq // tm
    return pl.pallas_call(
        _final_kernel,
        grid_spec=pltpu.PrefetchScalarGridSpec(
            num_scalar_prefetch=2,
            grid=(m // tm,),
            in_specs=[
                pl.BlockSpec((tm, d), lambda i, d1, d2: (i, 0)),
                pl.BlockSpec((tm, LANES), lambda i, d1, d2: (i, 0)),
                pl.BlockSpec((None, 1, d), lambda i, d1, d2: (i // per_batch, 0, 0)),
                pl.BlockSpec((1, d), lambda i, d1, d2: (0, 0)),
                pl.BlockSpec(memory_space=pl.ANY),
            ],
            out_specs=pl.BlockSpec((tm, d), lambda i, d1, d2: (i, 0)),
            scratch_shapes=[pltpu.VMEM((2, tm, d), F32), pltpu.SemaphoreType.DMA((2,))],
        ),
        out_shape=jax.ShapeDtypeStruct((m, d), F32),
        compiler_params=_params(("arbitrary",), 32),
        name="final",
    )(d1, d2, x1, route, gt2, nfg, ys)


def _tile(n, pref):
    t = min(n, pref)
    assert n % t == 0
    return t


def kernel(x, c, w_ada, b_ada, norm1_g, w_in, conv_qkv_w, a_log, dt_bias, onorm_g, w_proj_a,
           conv_sc_w, w_proj_b, w_out, norm2_g, w_group, b_group, w_expert, b_expert, w1, w3, w2,
           normf_g):
    batch, seq, d = x.shape
    depth = w_ada.shape[0]
    m = batch * seq
    kd = N_HEADS * HEAD_D
    assert d == kd and seq % CHUNK == 0 and batch <= SUBLANES
    c_pad = jnp.zeros((SUBLANES, d), F32).at[:batch].set(c)
    x2 = x.reshape(m, d)

    for l in range(depth):
        mod = _ada(c_pad, w_ada[l], b_ada[l][None, :])[:batch]
        sh1, sc1, gt1, sh2, sc2, gt2 = [mod[:, None, j * d:(j + 1) * d] for j in range(6)]

        w = w_in[l]
        o_ba = 3 * kd + kd
        w_main = jnp.concatenate([w[:, :o_ba], w[:, o_ba + 2 * N_HEADS:]], axis=1).astype(BF16)
        w_ba = jnp.zeros((d, LANES), F32).at[:, :2 * N_HEADS].set(w[:, o_ba:o_ba + 2 * N_HEADS]).astype(BF16)
        proj, ba = _inproj(x2, norm1_g[l][None, :], sc1, sh1, w_main, w_ba, seq, _tile(seq, INPROJ_ROWS))

        head_params = jnp.zeros((SUBLANES, LANES), F32)
        head_params = head_params.at[0, N_HEADS:2 * N_HEADS].set(a_log[l])
        head_params = head_params.at[1, N_HEADS:2 * N_HEADS].set(dt_bias[l])
        head_params = head_params.at[2, :].set(onorm_g[l])
        conv_taps = jnp.broadcast_to(conv_qkv_w[l][:, None, :], (QKV_CONV, SUBLANES, 3 * kd))
        og = _gdn(proj, ba, conv_taps, head_params, batch, seq, _tile(seq, GDN_ROWS))

        mod_post = jnp.zeros((batch, SUBLANES, d), F32)
        mod_post = mod_post.at[:, 0:1].set(gt1).at[:, 1:2].set(sc2).at[:, 2:3].set(sh2)
        w_route = jnp.zeros((d, LANES), F32)
        w_route = w_route.at[:, :N_GROUPS].set(w_group[l]).at[:, N_GROUPS:N_GROUPS + N_EXPERTS].set(w_expert[l])
        w_route_hi = w_route.astype(BF16)
        w_route_lo = (w_route - w_route_hi.astype(F32)).astype(BF16)
        b_route = jnp.zeros((1, LANES), F32)
        b_route = b_route.at[0, :N_GROUPS].set(b_group[l]).at[0, N_GROUPS:N_GROUPS + N_EXPERTS].set(b_expert[l])
        x1, h2, route, counts = _post(
            x2, og, proj, mod_post, norm2_g[l][None, :], conv_sc_w[l],
            w_proj_a[l].astype(BF16), w_proj_b[l].astype(BF16), w_out[l].astype(BF16),
            w_route_hi, w_route_lo, b_route, seq, _tile(seq, POST_ROWS))

        bm = EXPERT_BLOCK
        n_blocks = (2 * m) // bm + N_EXPERTS
        sizes = counts[0, :N_EXPERTS].astype(jnp.int32)
        padded = ((sizes + bm - 1) // bm) * bm
        pad_end = jnp.cumsum(padded)
        tail_start = jnp.concatenate([jnp.where(padded > 0, pad_end - bm, -1),
                                      pad_end[-1:] // bm]).astype(jnp.int32)
        block_row = jnp.arange(n_blocks, dtype=jnp.int32) * bm
        block_expert = jnp.minimum(jnp.sum(block_row[:, None] >= pad_end[None, :], axis=1),
                                   N_EXPERTS - 1).astype(jnp.int32)
        block_active = jnp.concatenate([(block_row < pad_end[-1]).astype(jnp.int32),
                                        (pad_end[-1:] // bm).astype(jnp.int32)])
        dest = _plan(route, counts, _tile(seq, INPROJ_ROWS))
        d1, d2 = dest[0], dest[1]

        xs = _dispatch(d1, d2, tail_start, h2, n_blocks * bm, _tile(seq, MOVE_ROWS))
        ys = _experts(block_expert, block_active, xs, w1[l], w3[l], w2[l])
        nfg = normf_g[None, :] if l == depth - 1 else jnp.ones((1, d), F32)
        x2 = _final(d1, d2, x1, route, gt2, nfg, ys, seq, _tile(seq, MOVE_ROWS))
        assert depth == 1
    return x2.reshape(batch, seq, d)
```

```python
import functools

import jax
import jax.numpy as jnp
from jax import lax
from jax.experimental import pallas as pl
from jax.experimental.pallas import tpu as pltpu

F32 = jnp.float32
BF16 = jnp.bfloat16
HIGHEST = lax.Precision.HIGHEST

N_HEADS = 8
HEAD_D = 128
CHUNK = 64
QKV_CONV = 4
SC_CONV = 3
N_GROUPS = 4
EXPERTS_PER_GROUP = 8
N_EXPERTS = N_GROUPS * EXPERTS_PER_GROUP
EPS = 1e-6

LANES = 128
SUBLANES = 8
BF16_ROWS = 16
N_PROJ_COLS = 9
EXPERT_BLOCK = 512
INPROJ_ROWS = 2048
GDN_ROWS = 256
POST_ROWS = 512
MOVE_ROWS = 256
DMA_UNROLL = 8
MIB = 1024 * 1024


def _sigmoid(x):
    return 1.0 / (1.0 + jnp.exp(-x))


def _silu(x):
    half = 0.5 * x
    return half + half * jnp.tanh(half)


def _softplus(x):
    return jnp.maximum(x, 0.0) + jnp.log(1.0 + jnp.exp(-jnp.abs(x)))


def _dot(a, b):
    return jnp.dot(a, b, preferred_element_type=F32)


def _dot_nt(a, b):
    return lax.dot_general(a, b, (((1,), (1,)), ((), ())), preferred_element_type=F32)


def _dot_hi(a, b):
    return jnp.dot(a, b, preferred_element_type=F32, precision=HIGHEST)


def _dot_split(a, b_hi, b_lo):
    a_hi = a.astype(BF16)
    a_lo = (a - a_hi.astype(F32)).astype(BF16)
    return _dot(a_hi, b_hi) + (_dot(a_lo, b_hi) + _dot(a_hi, b_lo))


def _read_rows(ref, rows):
    return jnp.concatenate([ref[pl.ds(j, rows, stride=SUBLANES), :] for j in range(SUBLANES)], axis=1)


def _write_rows(ref, value):
    rows = value.shape[0]
    for j in range(SUBLANES):
        ref[pl.ds(j, rows, stride=SUBLANES), :] = value[:, j * LANES:(j + 1) * LANES]


def _row_tile(ref, row):
    if not isinstance(row, int):
        row = pl.multiple_of(row * SUBLANES, SUBLANES)
    else:
        row = row * SUBLANES
    return ref.at[pl.ds(row, SUBLANES), :]


def _params(semantics, vmem_mib):
    return pltpu.CompilerParams(dimension_semantics=semantics, vmem_limit_bytes=vmem_mib * MIB)


def _ada_kernel(c_ref, w_ref, b_ref, o_ref):
    o_ref[...] = _dot_hi(_silu(c_ref[...]), w_ref[...]) + b_ref[...]


def _ada(c_pad, w_ada, b_ada):
    d = c_pad.shape[1]
    n = w_ada.shape[1]
    return pl.pallas_call(
        _ada_kernel,
        grid=(n // d,),
        in_specs=[
            pl.BlockSpec((SUBLANES, d), lambda j: (0, 0)),
            pl.BlockSpec((d, d), lambda j: (0, j)),
            pl.BlockSpec((1, d), lambda j: (0, j)),
        ],
        out_specs=pl.BlockSpec((SUBLANES, d), lambda j: (0, j)),
        out_shape=jax.ShapeDtypeStruct((SUBLANES, n), F32),
        compiler_params=_params(("arbitrary",), 24),
        name="ada",
    )(c_pad, w_ada, b_ada)


def _inproj_kernel(x_ref, g_ref, sc_ref, sh_ref, w_ref, wba_ref, o_ref, ba_ref, h_ref):
    @pl.when(pl.program_id(1) == 0)
    def _():
        x = x_ref[...]
        y = x * lax.rsqrt(jnp.mean(x * x, axis=-1, keepdims=True) + EPS)
        h = (y * g_ref[...]) * (1.0 + sc_ref[...]) + sh_ref[...]
        hb = h.astype(BF16)
        h_ref[...] = hb
        ba_ref[...] = _dot(hb, wba_ref[...])

    o_ref[...] = _dot(h_ref[...], w_ref[...]).astype(BF16)


def _inproj(x2, norm_g, sc, sh, w_main, w_ba, seq, tm):
    m, d = x2.shape
    n = w_main.shape[1]
    tn = d
    per_batch = seq // tm
    return pl.pallas_call(
        _inproj_kernel,
        grid=(m // tm, n // tn),
        in_specs=[
            pl.BlockSpec((tm, d), lambda i, j: (i, 0)),
            pl.BlockSpec((1, d), lambda i, j: (0, 0)),
            pl.BlockSpec((None, 1, d), lambda i, j: (i // per_batch, 0, 0)),
            pl.BlockSpec((None, 1, d), lambda i, j: (i // per_batch, 0, 0)),
            pl.BlockSpec((d, tn), lambda i, j: (0, j)),
            pl.BlockSpec((d, LANES), lambda i, j: (0, 0)),
        ],
        out_specs=[
            pl.BlockSpec((tm, tn), lambda i, j: (i, j)),
            pl.BlockSpec((tm, LANES), lambda i, j: (i, 0)),
        ],
        out_shape=[
            jax.ShapeDtypeStruct((m, n), BF16),
            jax.ShapeDtypeStruct((m, LANES), F32),
        ],
        scratch_shapes=[pltpu.VMEM((tm, d), BF16)],
        compiler_params=_params(("arbitrary", "arbitrary"), 56),
        name="inproj",
    )(x2, norm_g, sc, sh, w_main, w_ba)


def _bmm(a, b):
    return jnp.einsum("hmk,hkn->hmn", a.astype(BF16), b.astype(BF16), preferred_element_type=F32)


def _bmm_nt(a, b):
    return jnp.einsum("hmk,hnk->hmn", a.astype(BF16), b.astype(BF16), preferred_element_type=F32)


def _unit_lower_inverse(a):
    row = lax.broadcasted_iota(jnp.int32, a.shape[1:], 0)
    col = lax.broadcasted_iota(jnp.int32, a.shape[1:], 1)
    eye = jnp.where(row == col, 1.0, 0.0).astype(F32)
    p = eye - a
    xp = a
    n = 2
    while n < CHUNK:
        xp = _bmm(xp, xp)
        p = p + _bmm(p, xp)
        n *= 2
    return p


def _lane_sums(x):
    h, rows, width = x.shape
    ones = jnp.ones((width, width), BF16)
    return _dot(x.reshape(h * rows, width).astype(BF16), ones).reshape(h, rows, width)


def _causal_conv_silu(win, cw, k_w):
    assert k_w == 4
    tiled = (win.shape[0] // SUBLANES, SUBLANES, win.shape[1])

    def pair(x, x1, j):
        return (x.reshape(tiled) * cw[j][None] + x1.reshape(tiled) * cw[j - 1][None]).reshape(win.shape)

    win1 = pltpu.roll(win, 1, 0)
    acc = pair(win, win1, 3) + pltpu.roll(pair(win, win1, 1), 2, 0)
    return _silu(acc[SUBLANES:, :])


def _gdn_kernel(q_ref, k_ref, v_ref, z_ref, ba_ref, cw_ref, hp_ref, o_ref,
                s_ref, tail_ref, wq_ref, u_ref, ik_ref, dec_ref):
    nb, tb = q_ref.shape[0], q_ref.shape[1]
    kd = N_HEADS * HEAD_D
    nbh = nb * N_HEADS

    @pl.when(pl.program_id(0) == 0)
    def _():
        s_ref[...] = jnp.zeros(s_ref.shape, F32)
        tail_ref[...] = jnp.zeros(tail_ref.shape, F32)

    row = lax.broadcasted_iota(jnp.int32, (CHUNK, CHUNK), 0)
    col = lax.broadcasted_iota(jnp.int32, (CHUNK, CHUNK), 1)
    causal = row >= col
    strict = row > col
    tril = jnp.where(causal, 1.0, 0.0).astype(F32)
    a_log = hp_ref[0:1, :]
    dt_bias = hp_ref[1:2, :]
    onorm_g = hp_ref[2:3, :]
    zeros_half = jnp.zeros((CHUNK, HEAD_D), F32)

    def precompute(c, carry):
        base = pl.multiple_of(c * CHUNK, CHUNK)
        qs, ks, vs, betas, gcs, grs, gls = [], [], [], [], [], [], []
        for b in range(nb):
            ba = ba_ref[b, pl.ds(base, CHUNK), :]
            beta_all = _sigmoid(ba)
            g_all = -jnp.exp(a_log) * _softplus(ba + dt_bias)
            gcum = _dot_hi(tril, g_all)
            gcum_t = jnp.concatenate([gcum, gcum], axis=0).T
            for h in range(N_HEADS):
                lo, hi = h * HEAD_D, (h + 1) * HEAD_D

                def conv(ref, off):
                    cur = ref[b, pl.ds(base, CHUNK), lo:hi].astype(F32)
                    win = jnp.concatenate([tail_ref[b, :, off + lo:off + hi], cur], axis=0)
                    return _causal_conv_silu(win, cw_ref[:, :, off + lo:off + hi], QKV_CONV)

                qs.append(conv(q_ref, 0))
                ks.append(conv(k_ref, kd))
                vs.append(conv(v_ref, 2 * kd))
                betas.append(beta_all[:, h:h + 1])
                gcs.append(gcum[:, N_HEADS + h:N_HEADS + h + 1])
                grs.append(gcum_t[N_HEADS + h:N_HEADS + h + 1, 0:CHUNK])
                gls.append(gcum[CHUNK - 1:CHUNK, N_HEADS + h:N_HEADS + h + 1])
        q, k, v = jnp.stack(qs), jnp.stack(ks), jnp.stack(vs)
        beta, gc, gr, gl = jnp.stack(betas), jnp.stack(gcs), jnp.stack(grs), jnp.stack(gls)
        qn = q * (lax.rsqrt(_lane_sums(q * q) + EPS) * (HEAD_D ** -0.5))
        kn = k * lax.rsqrt(_lane_sums(k * k) + EPS)
        decay = jnp.where(causal, jnp.exp(jnp.where(causal, gc - gr, 0.0)), 0.0)
        kb = kn * beta
        e_gc = jnp.exp(gc)
        kq = _bmm_nt(jnp.concatenate([kb, qn], axis=1), kn)
        a = jnp.where(strict, kq[:, :CHUNK] * decay, 0.0)
        intra = kq[:, CHUNK:] * decay
        uw = _bmm(_unit_lower_inverse(a), jnp.concatenate([v * beta, kb * e_gc], axis=2))
        u_ref[c] = uw[:, :, :HEAD_D]
        wq_ref[c] = jnp.concatenate([uw[:, :, HEAD_D:], qn * e_gc], axis=1).astype(BF16)
        k_dec = kn * jnp.exp(gl - gc)
        k_dec_t = jnp.stack([jnp.concatenate([k_dec[i], zeros_half], axis=0).T[:, :CHUNK]
                             for i in range(nbh)])
        ik_ref[c] = jnp.concatenate([intra, k_dec_t], axis=1).astype(BF16)
        dec_ref[c] = jnp.broadcast_to(jnp.exp(gl), (nbh, 1, HEAD_D))

        last = pl.multiple_of(base + CHUNK - BF16_ROWS, BF16_ROWS)
        for b in range(nb):
            for j, ref in enumerate((q_ref, k_ref, v_ref)):
                rows = ref[b, pl.ds(last, BF16_ROWS), :].astype(F32)
                tail_ref[b, :, j * kd:(j + 1) * kd] = rows[BF16_ROWS - SUBLANES:, :]
        return carry

    def recur(c, carry):
        base = pl.multiple_of(c * CHUNK, CHUNK)
        state = s_ref[...]
        ws = _bmm(wq_ref[c], state)
        v_new = u_ref[c] - ws[:, :CHUNK]
        r = _bmm(ik_ref[c], v_new)
        o = ws[:, CHUNK:] + r[:, :CHUNK]
        s_ref[...] = state * dec_ref[c] + r[:, CHUNK:]
        on = o * lax.rsqrt(jnp.mean(o * o, axis=-1, keepdims=True) + EPS) * onorm_g
        for b in range(nb):
            for h in range(N_HEADS):
                lo, hi = h * HEAD_D, (h + 1) * HEAD_D
                z = z_ref[b, pl.ds(base, CHUNK), lo:hi].astype(F32)
                o_ref[b, pl.ds(base, CHUNK), lo:hi] = (on[b * N_HEADS + h] * _silu(z)).astype(BF16)
        return carry

    lax.fori_loop(0, tb // CHUNK, precompute, 0)
    lax.fori_loop(0, tb // CHUNK, recur, 0)


def _gdn(proj, ba, conv_w, head_params, batch, seq, tb):
    kd = N_HEADS * HEAD_D
    nc = tb // CHUNK
    nbh = batch * N_HEADS
    proj3 = proj.reshape(batch, seq, proj.shape[1])
    ba3 = ba.reshape(batch, seq, LANES)

    def col(j):
        return pl.BlockSpec((batch, tb, kd), lambda t: (0, t, j))

    out = pl.pallas_call(
        _gdn_kernel,
        grid=(seq // tb,),
        in_specs=[
            col(0), col(1), col(2), col(3),
            pl.BlockSpec((batch, tb, LANES), lambda t: (0, t, 0)),
            pl.BlockSpec((QKV_CONV, SUBLANES, 3 * kd), lambda t: (0, 0, 0)),
            pl.BlockSpec((SUBLANES, LANES), lambda t: (0, 0)),
        ],
        out_specs=pl.BlockSpec((batch, tb, kd), lambda t: (0, t, 0)),
        out_shape=jax.ShapeDtypeStruct((batch, seq, kd), BF16),
        scratch_shapes=[
            pltpu.VMEM((nbh, HEAD_D, HEAD_D), F32),
            pltpu.VMEM((batch, SUBLANES, 3 * kd), F32),
            pltpu.VMEM((nc, nbh, 2 * CHUNK, HEAD_D), BF16),
            pltpu.VMEM((nc, nbh, CHUNK, HEAD_D), F32),
            pltpu.VMEM((nc, nbh, CHUNK + HEAD_D, CHUNK), BF16),
            pltpu.VMEM((nc, nbh, 1, HEAD_D), F32),
        ],
        compiler_params=_params(("arbitrary",), 48),
        name="gdn",
    )(proj3, proj3, proj3, proj3, ba3, conv_w, head_params)
    return out.reshape(batch * seq, kd)


def _post_kernel(x_ref, og_ref, sb_ref, sc_ref, sx_ref, ga_ref, gb_ref, mod_ref, n2g_ref, cw_ref,
                 wpa_ref, wpb_ref, wout_ref, wrh_ref, wrl_ref, br_ref,
                 x1_ref, h2_ref, route_ref, cnt_ref, win_ref, run_ref, *, per_batch):
    tm = x_ref.shape[0]
    i = pl.program_id(0)

    @pl.when(i == 0)
    def _():
        run_ref[...] = jnp.zeros(run_ref.shape, F32)

    @pl.when(i % per_batch == 0)
    def _():
        win_ref[0:SUBLANES, :] = jnp.zeros((SUBLANES, win_ref.shape[1]), F32)

    win_ref[SUBLANES:, :] = sc_ref[...].astype(F32) * sx_ref[...].astype(F32)
    conv = win_ref[pl.ds(SUBLANES, tm), :] * cw_ref[SC_CONV - 1:SC_CONV, :]
    for s in range(1, SC_CONV):
        conv = conv + win_ref[pl.ds(SUBLANES - s, tm), :] * cw_ref[SC_CONV - 1 - s:SC_CONV - s, :]
    win_ref[0:SUBLANES, :] = win_ref[pl.ds(tm, SUBLANES), :]
    y_b = _dot((sb_ref[...].astype(F32) * conv).astype(BF16), wpb_ref[...])
    y_a = _dot(og_ref[...], wpa_ref[...])
    merged = _sigmoid(ga_ref[...].astype(F32)) * y_a + _sigmoid(gb_ref[...].astype(F32)) * y_b
    mix = _dot(merged.astype(BF16), wout_ref[...])
    x1 = x_ref[...] + mod_ref[0:1, :] * mix
    x1_ref[...] = x1

    y = x1 * lax.rsqrt(jnp.mean(x1 * x1, axis=-1, keepdims=True) + EPS)
    h2 = (y * n2g_ref[...]) * (1.0 + mod_ref[1:2, :]) + mod_ref[2:3, :]
    _write_rows(h2_ref, h2)

    lg = _dot_split(h2, wrh_ref[...], wrl_ref[...]) + br_ref[...]
    lane = lax.broadcasted_iota(jnp.int32, lg.shape, 1).astype(F32)
    neg = jnp.float32(-jnp.inf)
    big = jnp.float32(2 * LANES)

    def first_max(mask):
        vmax = jnp.max(jnp.where(mask, lg, neg), axis=-1, keepdims=True)
        idx = jnp.min(jnp.where(mask & (lg == vmax), lane, big), axis=-1, keepdims=True)
        return vmax, idx

    gmask = lane < N_GROUPS
    g_max, g_sel = first_max(gmask)
    p_group = 1.0 / jnp.sum(jnp.where(gmask, jnp.exp(lg - g_max), 0.0), axis=-1, keepdims=True)
    e_lo = N_GROUPS + EXPERTS_PER_GROUP * g_sel
    emask = (lane >= e_lo) & (lane < e_lo + EXPERTS_PER_GROUP)
    v1, i1 = first_max(emask)
    v2, i2 = first_max(emask & (lane != i1))
    ex = jnp.exp(v2 - v1)
    w1 = p_group * (1.0 / (1.0 + ex))
    w2 = p_group * (ex / (1.0 + ex))
    e1 = i1 - N_GROUPS
    e2 = i2 - N_GROUPS

    onehot = jnp.where((lane == e1) | (lane == e2), 1.0, 0.0).astype(F32)
    row = lax.broadcasted_iota(jnp.int32, (tm, tm), 0)
    col = lax.broadcasted_iota(jnp.int32, (tm, tm), 1)
    before = jnp.where(row > col, 1.0, 0.0).astype(BF16)
    seen = _dot(before, onehot.astype(BF16)) + run_ref[0:1, :]
    r1 = jnp.sum(jnp.where(lane == e1, seen, 0.0), axis=-1, keepdims=True)
    r2 = jnp.sum(jnp.where(lane == e2, seen, 0.0), axis=-1, keepdims=True)
    run_ref[0:1, :] = run_ref[0:1, :] + jnp.sum(onehot, axis=0, keepdims=True)
    cnt_ref[...] = jnp.broadcast_to(run_ref[0:1, :], cnt_ref.shape)

    out = jnp.where(lane == 0, e1, 0.0)
    out = jnp.where(lane == 1, e2, out)
    out = jnp.where(lane == 2, w1, out)
    out = jnp.where(lane == 3, w2, out)
    out = jnp.where(lane == 4, r1, out)
    out = jnp.where(lane == 5, r2, out)
    route_ref[...] = out


def _post(x2, og, proj, mod, n2g, conv_w, wpa, wpb, wout, w_route_hi, w_route_lo, b_route, seq, tm):
    m, d = x2.shape
    assert d == SUBLANES * LANES
    per_batch = seq // tm

    def rows(j):
        return pl.BlockSpec((tm, d), lambda i: (i, j))

    def whole(shape):
        return pl.BlockSpec(shape, lambda i: tuple(0 for _ in shape))

    return pl.pallas_call(
        functools.partial(_post_kernel, per_batch=per_batch),
        grid=(m // tm,),
        in_specs=[
            rows(0), rows(0), rows(4), rows(5), rows(6), rows(7), rows(8),
            pl.BlockSpec((None, SUBLANES, d), lambda i: (i // per_batch, 0, 0)),
            whole((1, d)), whole((SC_CONV, d)),
            whole((d, d)), whole((d, d)), whole((d, d)),
            whole((d, LANES)), whole((d, LANES)), whole((1, LANES)),
        ],
        out_specs=[
            rows(0), pl.BlockSpec((tm * SUBLANES, LANES), lambda i: (i, 0)),
            pl.BlockSpec((tm, LANES), lambda i: (i, 0)),
            pl.BlockSpec((SUBLANES, LANES), lambda i: (0, 0)),
        ],
        out_shape=[
            jax.ShapeDtypeStruct((m, d), F32),
            jax.ShapeDtypeStruct((m * SUBLANES, LANES), F32),
            jax.ShapeDtypeStruct((m, LANES), F32),
            jax.ShapeDtypeStruct((SUBLANES, LANES), F32),
        ],
        scratch_shapes=[
            pltpu.VMEM((tm + SUBLANES, d), F32),
            pltpu.VMEM((SUBLANES, LANES), F32),
        ],
        compiler_params=_params(("arbitrary",), 56),
        name="post",
    )(x2, og, proj, proj, proj, proj, proj, mod, n2g, conv_w, wpa, wpb, wout, w_route_hi, w_route_lo,
      b_route)


def _plan_kernel(route_ref, cnt_ref, d_ref):
    bm = EXPERT_BLOCK
    sizes = cnt_ref[...]
    padded = jnp.floor((sizes + (bm - 1.0)) * (1.0 / bm)) * bm
    lane_i = lax.broadcasted_iota(jnp.int32, sizes.shape, 1)
    incl = padded
    s = 1
    while s < LANES:
        incl = incl + jnp.where(lane_i >= s, pltpu.roll(incl, s, 1), 0.0)
        s *= 2
    start = (incl - padded)[0:1, :]
    r = route_ref[...]
    lane = lax.broadcasted_iota(jnp.int32, r.shape, 1).astype(F32)
    d1 = jnp.sum(jnp.where(lane == r[:, 0:1], start, 0.0), axis=-1, keepdims=True) + r[:, 4:5]
    d2 = jnp.sum(jnp.where(lane == r[:, 1:2], start, 0.0), axis=-1, keepdims=True) + r[:, 5:6]
    out = jnp.where(lane == 0.0, d1, jnp.where(lane == 1.0, d2, 0.0))
    d_ref[...] = out.T[0:SUBLANES, :].astype(jnp.int32)


def _plan(route, counts, tm):
    m = route.shape[0]
    return pl.pallas_call(
        _plan_kernel,
        grid=(m // tm,),
        in_specs=[
            pl.BlockSpec((tm, LANES), lambda i: (i, 0)),
            pl.BlockSpec((SUBLANES, LANES), lambda i: (0, 0)),
        ],
        out_specs=pl.BlockSpec((SUBLANES, tm), lambda i: (0, i)),
        out_shape=jax.ShapeDtypeStruct((SUBLANES, m), jnp.int32),
        compiler_params=_params(("arbitrary",), 16),
        name="plan",
    )(route, counts)


def _row_copy(src_ref, src_row, dst_ref, dst_row, sem):
    return pltpu.make_async_copy(_row_tile(src_ref, src_row), _row_tile(dst_ref, dst_row), sem)


def _rows_copy(src_ref, dst_ref, dst_row, n, sem):
    return pltpu.make_async_copy(src_ref, dst_ref.at[pl.ds(dst_row * SUBLANES, n * SUBLANES), :], sem)


def _dispatch_kernel(d1_ref, d2_ref, tail_ref, h2_ref, xs_ref, zero_ref, sem):
    tm = h2_ref.shape[0] // SUBLANES
    bm = zero_ref.shape[0] // SUBLANES
    t0 = pl.program_id(0) * tm

    @pl.when(pl.program_id(0) == 0)
    def _():
        zero_ref[...] = jnp.zeros(zero_ref.shape, F32)
        n_blocks = xs_ref.shape[0] // (bm * SUBLANES)

        def zero_block(row):
            return _rows_copy(zero_ref, xs_ref, pl.multiple_of(row, bm), bm, sem.at[0])

        for e in range(N_EXPERTS):
            @pl.when(tail_ref[e] >= 0)
            def _():
                zero_block(tail_ref[e]).start()
        lax.fori_loop(tail_ref[N_EXPERTS], n_blocks, lambda j, c: (zero_block(j * bm).start(), c)[1], 0)
        for e in range(N_EXPERTS):
            @pl.when(tail_ref[e] >= 0)
            def _():
                zero_block(tail_ref[e]).wait()
        lax.fori_loop(tail_ref[N_EXPERTS], n_blocks, lambda j, c: (zero_block(j * bm).wait(), c)[1], 0)

    def start(r, carry):
        _row_copy(h2_ref, r, xs_ref, d1_ref[t0 + r], sem.at[0]).start()
        _row_copy(h2_ref, r, xs_ref, d2_ref[t0 + r], sem.at[1]).start()
        return carry

    lax.fori_loop(0, tm, start, 0, unroll=DMA_UNROLL)
    _rows_copy(h2_ref, xs_ref, 0, tm, sem.at[0]).wait()
    _rows_copy(h2_ref, xs_ref, 0, tm, sem.at[1]).wait()


def _dispatch(d1, d2, tail_start, h2, cap, tm):
    m = h2.shape[0] // SUBLANES
    return pl.pallas_call(
        _dispatch_kernel,
        grid_spec=pltpu.PrefetchScalarGridSpec(
            num_scalar_prefetch=3,
            grid=(m // tm,),
            in_specs=[pl.BlockSpec((tm * SUBLANES, LANES), lambda i, d1, d2, tl: (i, 0))],
            out_specs=pl.BlockSpec(memory_space=pl.ANY),
            scratch_shapes=[pltpu.VMEM((EXPERT_BLOCK * SUBLANES, LANES), F32),
                            pltpu.SemaphoreType.DMA((2,))],
        ),
        out_shape=jax.ShapeDtypeStruct((cap * SUBLANES, LANES), F32),
        compiler_params=_params(("arbitrary",), 24),
        name="dispatch",
    )(d1, d2, tail_start, h2)


def _expert_kernel(be_ref, act_ref, x_ref, w1_ref, w3_ref, w2_ref, y_ref, w1b_ref, w3b_ref, w2b_ref):
    b = pl.program_id(0)

    @pl.when(act_ref[b] > 0)
    def _():
        @pl.when((b == 0) | (be_ref[b] != be_ref[jnp.maximum(b - 1, 0)]))
        def _():
            w1b_ref[...] = w1_ref[...].astype(BF16)
            w3b_ref[...] = w3_ref[...].astype(BF16)
            w2b_ref[...] = w2_ref[...].astype(BF16)

        xb = _read_rows(x_ref, x_ref.shape[0] // SUBLANES).astype(BF16)
        hid = _silu(_dot(xb, w1b_ref[...])) * _dot(xb, w3b_ref[...])
        _write_rows(y_ref, _dot(hid.astype(BF16), w2b_ref[...]))

    @pl.when(act_ref[b] == 0)
    def _():
        y_ref[...] = jnp.zeros(y_ref.shape, F32)


def _experts(block_expert, block_active, xs, w1, w3, w2):
    cap = xs.shape[0] // SUBLANES
    d, de = w1.shape[1], w1.shape[2]
    bm = EXPERT_BLOCK

    def x_block(b, be, act):
        return (jnp.minimum(b, jnp.maximum(act[cap // bm], 1) - 1), 0)

    return pl.pallas_call(
        _expert_kernel,
        grid_spec=pltpu.PrefetchScalarGridSpec(
            num_scalar_prefetch=2,
            grid=(cap // bm,),
            in_specs=[
                pl.BlockSpec((bm * SUBLANES, LANES), x_block),
                pl.BlockSpec((None, d, de), lambda b, be, act: (be[b], 0, 0)),
                pl.BlockSpec((None, d, de), lambda b, be, act: (be[b], 0, 0)),
                pl.BlockSpec((None, de, d), lambda b, be, act: (be[b], 0, 0)),
            ],
            out_specs=pl.BlockSpec((bm * SUBLANES, LANES), lambda b, be, act: (b, 0)),
            scratch_shapes=[pltpu.VMEM((d, de), BF16), pltpu.VMEM((d, de), BF16),
                            pltpu.VMEM((de, d), BF16)],
        ),
        out_shape=jax.ShapeDtypeStruct((cap * SUBLANES, LANES), F32),
        compiler_params=_params(("arbitrary",), 40),
        name="experts",
    )(block_expert, block_active, xs, w1, w3, w2)


def _final_kernel(d1_ref, d2_ref, x1_ref, route_ref, gt_ref, nfg_ref, ys_ref, o_ref, buf_ref, sem):
    tm = x1_ref.shape[0]
    t0 = pl.program_id(0) * tm

    def start(r, carry):
        _row_copy(ys_ref, d1_ref[t0 + r], buf_ref.at[0], r, sem.at[0]).start()
        _row_copy(ys_ref, d2_ref[t0 + r], buf_ref.at[1], r, sem.at[1]).start()
        return carry

    lax.fori_loop(0, tm, start, 0, unroll=DMA_UNROLL)
    for k in range(2):
        pltpu.make_async_copy(ys_ref.at[pl.ds(0, tm * SUBLANES), :], buf_ref.at[k], sem.at[k]).wait()
    route = route_ref[...]
    moe = _read_rows(buf_ref.at[0], tm) * route[:, 2:3] + _read_rows(buf_ref.at[1], tm) * route[:, 3:4]
    x2 = x1_ref[...] + gt_ref[...] * moe
    y = x2 * lax.rsqrt(jnp.mean(x2 * x2, axis=-1, keepdims=True) + EPS)
    o_ref[...] = y * nfg_ref[...]


def _final(d1, d2, x1, route, gt2, nfg, ys, seq, tm):
    m, d = x1.shape
    per_batch = seq // tm
    return pl.pallas_call(
        _final_kernel,
        grid_spec=pltpu.PrefetchScalarGridSpec(
            num_scalar_prefetch=2,
            grid=(m // tm,),
            in_specs=[
                pl.BlockSpec((tm, d), lambda i, d1, d2: (i, 0)),
                pl.BlockSpec((tm, LANES), lambda i, d1, d2: (i, 0)),
                pl.BlockSpec((None, 1, d), lambda i, d1, d2: (i // per_batch, 0, 0)),
                pl.BlockSpec((1, d), lambda i, d1, d2: (0, 0)),
                pl.BlockSpec(memory_space=pl.ANY),
            ],
            out_specs=pl.BlockSpec((tm, d), lambda i, d1, d2: (i, 0)),
            scratch_shapes=[pltpu.VMEM((2, tm * SUBLANES, LANES), F32), pltpu.SemaphoreType.DMA((2,))],
        ),
        out_shape=jax.ShapeDtypeStruct((m, d), F32),
        compiler_params=_params(("arbitrary",), 32),
        name="final",
    )(d1, d2, x1, route, gt2, nfg, ys)


def _tile(n, pref):
    t = min(n, pref)
    assert n % t == 0
    return t


def kernel(x, c, w_ada, b_ada, norm1_g, w_in, conv_qkv_w, a_log, dt_bias, onorm_g, w_proj_a,
           conv_sc_w, w_proj_b, w_out, norm2_g, w_group, b_group, w_expert, b_expert, w1, w3, w2,
           normf_g):
    batch, seq, d = x.shape
    depth = w_ada.shape[0]
    m = batch * seq
    kd = N_HEADS * HEAD_D
    assert d == kd and seq % CHUNK == 0 and batch <= SUBLANES
    c_pad = jnp.zeros((SUBLANES, d), F32).at[:batch].set(c)
    x2 = x.reshape(m, d)

    for l in range(depth):
        mod = _ada(c_pad, w_ada[l], b_ada[l][None, :])[:batch]
        sh1, sc1, gt1, sh2, sc2, gt2 = [mod[:, None, j * d:(j + 1) * d] for j in range(6)]

        w = w_in[l]
        o_ba = 3 * kd + kd
        w_main = jnp.concatenate([w[:, :o_ba], w[:, o_ba + 2 * N_HEADS:]], axis=1).astype(BF16)
        w_ba = jnp.zeros((d, LANES), F32).at[:, :2 * N_HEADS].set(w[:, o_ba:o_ba + 2 * N_HEADS]).astype(BF16)
        proj, ba = _inproj(x2, norm1_g[l][None, :], sc1, sh1, w_main, w_ba, seq, _tile(seq, INPROJ_ROWS))

        head_params = jnp.zeros((SUBLANES, LANES), F32)
        head_params = head_params.at[0, N_HEADS:2 * N_HEADS].set(a_log[l])
        head_params = head_params.at[1, N_HEADS:2 * N_HEADS].set(dt_bias[l])
        head_params = head_params.at[2, :].set(onorm_g[l])
        conv_taps = jnp.broadcast_to(conv_qkv_w[l][:, None, :], (QKV_CONV, SUBLANES, 3 * kd))
        og = _gdn(proj, ba, conv_taps, head_params, batch, seq, _tile(seq, GDN_ROWS))

        mod_post = jnp.zeros((batch, SUBLANES, d), F32)
        mod_post = mod_post.at[:, 0:1].set(gt1).at[:, 1:2].set(sc2).at[:, 2:3].set(sh2)
        w_route = jnp.zeros((d, LANES), F32)
        w_route = w_route.at[:, :N_GROUPS].set(w_group[l]).at[:, N_GROUPS:N_GROUPS + N_EXPERTS].set(w_expert[l])
        w_route_hi = w_route.astype(BF16)
        w_route_lo = (w_route - w_route_hi.astype(F32)).astype(BF16)
        b_route = jnp.zeros((1, LANES), F32)
        b_route = b_route.at[0, :N_GROUPS].set(b_group[l]).at[0, N_GROUPS:N_GROUPS + N_EXPERTS].set(b_expert[l])
        x1, h2, route, counts = _post(
            x2, og, proj, mod_post, norm2_g[l][None, :], conv_sc_w[l],
            w_proj_a[l].astype(BF16), w_proj_b[l].astype(BF16), w_out[l].astype(BF16),
            w_route_hi, w_route_lo, b_route, seq, _tile(seq, POST_ROWS))

        bm = EXPERT_BLOCK
        n_blocks = (2 * m) // bm + N_EXPERTS
        sizes = counts[0, :N_EXPERTS].astype(jnp.int32)
        padded = ((sizes + bm - 1) // bm) * bm
        pad_end = jnp.cumsum(padded)
        tail_start = jnp.concatenate([jnp.where(padded > 0, pad_end - bm, -1),
                                      pad_end[-1:] // bm]).astype(jnp.int32)
        block_row = jnp.arange(n_blocks, dtype=jnp.int32) * bm
        block_expert = jnp.minimum(jnp.sum(block_row[:, None] >= pad_end[None, :], axis=1),
                                   N_EXPERTS - 1).astype(jnp.int32)
        block_active = jnp.concatenate([(block_row < pad_end[-1]).astype(jnp.int32),
                                        (pad_end[-1:] // bm).astype(jnp.int32)])
        dest = _plan(route, counts, _tile(seq, INPROJ_ROWS))
        d1, d2 = dest[0], dest[1]

        xs = _dispatch(d1, d2, tail_start, h2, n_blocks * bm, _tile(seq, MOVE_ROWS))
        ys = _experts(block_expert, block_active, xs, w1[l], w3[l], w2[l])
        nfg = normf_g[None, :] if l == depth - 1 else jnp.ones((1, d), F32)
        x2 = _final(d1, d2, x1, route, gt2, nfg, ys, seq, _tile(seq, MOVE_ROWS))
        assert depth == 1
    return x2.reshape(batch, seq, d)
```

```python
import functools

import jax
import jax.numpy as jnp
from jax import lax
from jax.experimental import pallas as pl
from jax.experimental.pallas import tpu as pltpu

F32 = jnp.float32
BF16 = jnp.bfloat16
HIGHEST = lax.Precision.HIGHEST

N_HEADS = 8
HEAD_D = 128
CHUNK = 64
QKV_CONV = 4
SC_CONV = 3
N_GROUPS = 4
EXPERTS_PER_GROUP = 8
N_EXPERTS = N_GROUPS * EXPERTS_PER_GROUP
EPS = 1e-6

LANES = 128
SUBLANES = 8
BF16_ROWS = 16
N_PROJ_COLS = 9
EXPERT_BLOCK = 512
INPROJ_ROWS = 2048
GDN_ROWS = 256
POST_ROWS = 512
MOVE_ROWS = 256
DMA_UNROLL = 8
MIB = 1024 * 1024


def _sigmoid(x):
    return 1.0 / (1.0 + jnp.exp(-x))


def _silu(x):
    half = 0.5 * x
    return half + half * jnp.tanh(half)


def _softplus(x):
    return jnp.maximum(x, 0.0) + jnp.log(1.0 + jnp.exp(-jnp.abs(x)))


def _dot(a, b):
    return jnp.dot(a, b, preferred_element_type=F32)


def _dot_nt(a, b):
    return lax.dot_general(a, b, (((1,), (1,)), ((), ())), preferred_element_type=F32)


def _dot_hi(a, b):
    return jnp.dot(a, b, preferred_element_type=F32, precision=HIGHEST)


def _dot_split(a, b_hi, b_lo):
    a_hi = a.astype(BF16)
    a_lo = (a - a_hi.astype(F32)).astype(BF16)
    return _dot(a_hi, b_hi) + (_dot(a_lo, b_hi) + _dot(a_hi, b_lo))


def _read_rows(ref, rows):
    return jnp.concatenate([ref[pl.ds(j, rows, stride=SUBLANES), :] for j in range(SUBLANES)], axis=1)


def _write_rows(ref, value):
    rows = value.shape[0]
    for j in range(SUBLANES):
        ref[pl.ds(j, rows, stride=SUBLANES), :] = value[:, j * LANES:(j + 1) * LANES]


def _row_tile(ref, row):
    if not isinstance(row, int):
        row = pl.multiple_of(row * SUBLANES, SUBLANES)
    else:
        row = row * SUBLANES
    return ref.at[pl.ds(row, SUBLANES), :]


def _params(semantics, vmem_mib):
    return pltpu.CompilerParams(dimension_semantics=semantics, vmem_limit_bytes=vmem_mib * MIB)


def _ada_kernel(c_ref, w_ref, b_ref, o_ref):
    o_ref[...] = _dot_hi(_silu(c_ref[...]), w_ref[...]) + b_ref[...]


def _ada(c_pad, w_ada, b_ada):
    d = c_pad.shape[1]
    n = w_ada.shape[1]
    return pl.pallas_call(
        _ada_kernel,
        grid=(n // d,),
        in_specs=[
            pl.BlockSpec((SUBLANES, d), lambda j: (0, 0)),
            pl.BlockSpec((d, d), lambda j: (0, j)),
            pl.BlockSpec((1, d), lambda j: (0, j)),
        ],
        out_specs=pl.BlockSpec((SUBLANES, d), lambda j: (0, j)),
        out_shape=jax.ShapeDtypeStruct((SUBLANES, n), F32),
        compiler_params=_params(("arbitrary",), 24),
        name="ada",
    )(c_pad, w_ada, b_ada)


def _inproj_kernel(x_ref, g_ref, sc_ref, sh_ref, w_ref, wba_ref, o_ref, ba_ref, h_ref):
    @pl.when(pl.program_id(1) == 0)
    def _():
        x = x_ref[...]
        y = x * lax.rsqrt(jnp.mean(x * x, axis=-1, keepdims=True) + EPS)
        h = (y * g_ref[...]) * (1.0 + sc_ref[...]) + sh_ref[...]
        hb = h.astype(BF16)
        h_ref[...] = hb
        ba_ref[...] = _dot(hb, wba_ref[...])

    o_ref[...] = _dot(h_ref[...], w_ref[...]).astype(BF16)


def _inproj(x2, norm_g, sc, sh, w_main, w_ba, seq, tm):
    m, d = x2.shape
    n = w_main.shape[1]
    tn = d
    per_batch = seq // tm
    return pl.pallas_call(
        _inproj_kernel,
        grid=(m // tm, n // tn),
        in_specs=[
            pl.BlockSpec((tm, d), lambda i, j: (i, 0)),
            pl.BlockSpec((1, d), lambda i, j: (0, 0)),
            pl.BlockSpec((None, 1, d), lambda i, j: (i // per_batch, 0, 0)),
            pl.BlockSpec((None, 1, d), lambda i, j: (i // per_batch, 0, 0)),
            pl.BlockSpec((d, tn), lambda i, j: (0, j)),
            pl.BlockSpec((d, LANES), lambda i, j: (0, 0)),
        ],
        out_specs=[
            pl.BlockSpec((tm, tn), lambda i, j: (i, j)),
            pl.BlockSpec((tm, LANES), lambda i, j: (i, 0)),
        ],
        out_shape=[
            jax.ShapeDtypeStruct((m, n), BF16),
            jax.ShapeDtypeStruct((m, LANES), F32),
        ],
        scratch_shapes=[pltpu.VMEM((tm, d), BF16)],
        compiler_params=_params(("arbitrary", "arbitrary"), 56),
        name="inproj",
    )(x2, norm_g, sc, sh, w_main, w_ba)


def _bmm(a, b):
    return jnp.einsum("hmk,hkn->hmn", a.astype(BF16), b.astype(BF16), preferred_element_type=F32)


def _bmm_nt(a, b):
    return jnp.einsum("hmk,hnk->hmn", a.astype(BF16), b.astype(BF16), preferred_element_type=F32)


def _unit_lower_inverse(a):
    row = lax.broadcasted_iota(jnp.int32, a.shape[1:], 0)
    col = lax.broadcasted_iota(jnp.int32, a.shape[1:], 1)
    eye = jnp.where(row == col, 1.0, 0.0).astype(F32)
    p = eye - a
    xp = a
    n = 2
    while n < CHUNK:
        xp = _bmm(xp, xp)
        p = p + _bmm(p, xp)
        n *= 2
    return p


def _lane_sums(x):
    h, rows, width = x.shape
    ones = jnp.ones((width, width), BF16)
    return _dot(x.reshape(h * rows, width).astype(BF16), ones).reshape(h, rows, width)


def _causal_conv_silu(win, cw, k_w):
    assert k_w == 4
    tiled = (win.shape[0] // SUBLANES, SUBLANES, win.shape[1])

    def pair(x, x1, j):
        return (x.reshape(tiled) * cw[j][None] + x1.reshape(tiled) * cw[j - 1][None]).reshape(win.shape)

    win1 = pltpu.roll(win, 1, 0)
    acc = pair(win, win1, 3) + pltpu.roll(pair(win, win1, 1), 2, 0)
    return _silu(acc[SUBLANES:, :])


def _gdn_kernel(q_ref, k_ref, v_ref, z_ref, ba_ref, cw_ref, hp_ref, o_ref,
                s_ref, tail_ref, wq_ref, u_ref, ik_ref, dec_ref):
    nb, tb = q_ref.shape[0], q_ref.shape[1]
    kd = N_HEADS * HEAD_D
    nbh = nb * N_HEADS

    @pl.when(pl.program_id(0) == 0)
    def _():
        s_ref[...] = jnp.zeros(s_ref.shape, F32)
        tail_ref[...] = jnp.zeros(tail_ref.shape, F32)

    row = lax.broadcasted_iota(jnp.int32, (CHUNK, CHUNK), 0)
    col = lax.broadcasted_iota(jnp.int32, (CHUNK, CHUNK), 1)
    causal = row >= col
    strict = row > col
    tril = jnp.where(causal, 1.0, 0.0).astype(F32)
    a_log = hp_ref[0:1, :]
    dt_bias = hp_ref[1:2, :]
    onorm_g = hp_ref[2:3, :]
    zeros_half = jnp.zeros((CHUNK, HEAD_D), F32)

    def precompute(c, carry):
        base = pl.multiple_of(c * CHUNK, CHUNK)
        qs, ks, vs, betas, gcs, grs, gls = [], [], [], [], [], [], []
        for b in range(nb):
            ba = ba_ref[b, pl.ds(base, CHUNK), :]
            beta_all = _sigmoid(ba)
            g_all = -jnp.exp(a_log) * _softplus(ba + dt_bias)
            gcum = _dot_hi(tril, g_all)
            gcum_t = jnp.concatenate([gcum, gcum], axis=0).T
            for h in range(N_HEADS):
                lo, hi = h * HEAD_D, (h + 1) * HEAD_D

                def conv(ref, off):
                    cur = ref[b, pl.ds(base, CHUNK), lo:hi].astype(F32)
                    win = jnp.concatenate([tail_ref[b, :, off + lo:off + hi], cur], axis=0)
                    return _causal_conv_silu(win, cw_ref[:, :, off + lo:off + hi], QKV_CONV)

                qs.append(conv(q_ref, 0))
                ks.append(conv(k_ref, kd))
                vs.append(conv(v_ref, 2 * kd))
                betas.append(beta_all[:, h:h + 1])
                gcs.append(gcum[:, N_HEADS + h:N_HEADS + h + 1])
                grs.append(gcum_t[N_HEADS + h:N_HEADS + h + 1, 0:CHUNK])
                gls.append(gcum[CHUNK - 1:CHUNK, N_HEADS + h:N_HEADS + h + 1])
        q, k, v = jnp.stack(qs), jnp.stack(ks), jnp.stack(vs)
        beta, gc, gr, gl = jnp.stack(betas), jnp.stack(gcs), jnp.stack(grs), jnp.stack(gls)
        qn = q * (lax.rsqrt(_lane_sums(q * q) + EPS) * (HEAD_D ** -0.5))
        kn = k * lax.rsqrt(_lane_sums(k * k) + EPS)
        decay = jnp.where(causal, jnp.exp(jnp.where(causal, gc - gr, 0.0)), 0.0)
        kb = kn * beta
        e_gc = jnp.exp(gc)
        kq = _bmm_nt(jnp.concatenate([kb, qn], axis=1), kn)
        a = jnp.where(strict, kq[:, :CHUNK] * decay, 0.0)
        intra = kq[:, CHUNK:] * decay
        uw = _bmm(_unit_lower_inverse(a), jnp.concatenate([v * beta, kb * e_gc], axis=2))
        u_ref[c] = uw[:, :, :HEAD_D]
        wq_ref[c] = jnp.concatenate([uw[:, :, HEAD_D:], qn * e_gc], axis=1).astype(BF16)
        k_dec = kn * jnp.exp(gl - gc)
        k_dec_t = jnp.stack([jnp.concatenate([k_dec[i], zeros_half], axis=0).T[:, :CHUNK]
                             for i in range(nbh)])
        ik_ref[c] = jnp.concatenate([intra, k_dec_t], axis=1).astype(BF16)
        dec_ref[c] = jnp.broadcast_to(jnp.exp(gl), (nbh, 1, HEAD_D))

        last = pl.multiple_of(base + CHUNK - BF16_ROWS, BF16_ROWS)
        for b in range(nb):
            for j, ref in enumerate((q_ref, k_ref, v_ref)):
                rows = ref[b, pl.ds(last, BF16_ROWS), :].astype(F32)
                tail_ref[b, :, j * kd:(j + 1) * kd] = rows[BF16_ROWS - SUBLANES:, :]
        return carry

    def recur(c, carry):
        base = pl.multiple_of(c * CHUNK, CHUNK)
        state = s_ref[...]
        ws = _bmm(wq_ref[c], state)
        v_new = u_ref[c] - ws[:, :CHUNK]
        r = _bmm(ik_ref[c], v_new)
        o = ws[:, CHUNK:] + r[:, :CHUNK]
        s_ref[...] = state * dec_ref[c] + r[:, CHUNK:]
        on = o * lax.rsqrt(jnp.mean(o * o, axis=-1, keepdims=True) + EPS) * onorm_g
        for b in range(nb):
            for h in range(N_HEADS):
                lo, hi = h * HEAD_D, (h + 1) * HEAD_D
                z = z_ref[b, pl.ds(base, CHUNK), lo:hi].astype(F32)
                o_ref[b, pl.ds(base, CHUNK), lo:hi] = (on[b * N_HEADS + h] * _silu(z)).astype(BF16)
        return carry

    lax.fori_loop(0, tb // CHUNK, precompute, 0)
    lax.fori_loop(0, tb // CHUNK, recur, 0)


def _gdn(proj, ba, conv_w, head_params, batch, seq, tb):
    kd = N_HEADS * HEAD_D
    nc = tb // CHUNK
    nbh = batch * N_HEADS
    proj3 = proj.reshape(batch, seq, proj.shape[1])
    ba3 = ba.reshape(batch, seq, LANES)

    def col(j):
        return pl.BlockSpec((batch, tb, kd), lambda t: (0, t, j))

    out = pl.pallas_call(
        _gdn_kernel,
        grid=(seq // tb,),
        in_specs=[
            col(0), col(1), col(2), col(3),
            pl.BlockSpec((batch, tb, LANES), lambda t: (0, t, 0)),
            pl.BlockSpec((QKV_CONV, SUBLANES, 3 * kd), lambda t: (0, 0, 0)),
            pl.BlockSpec((SUBLANES, LANES), lambda t: (0, 0)),
        ],
        out_specs=pl.BlockSpec((batch, tb, kd), lambda t: (0, t, 0)),
        out_shape=jax.ShapeDtypeStruct((batch, seq, kd), BF16),
        scratch_shapes=[
            pltpu.VMEM((nbh, HEAD_D, HEAD_D), F32),
            pltpu.VMEM((batch, SUBLANES, 3 * kd), F32),
            pltpu.VMEM((nc, nbh, 2 * CHUNK, HEAD_D), BF16),
            pltpu.VMEM((nc, nbh, CHUNK, HEAD_D), F32),
            pltpu.VMEM((nc, nbh, CHUNK + HEAD_D, CHUNK), BF16),
            pltpu.VMEM((nc, nbh, 1, HEAD_D), F32),
        ],
        compiler_params=_params(("arbitrary",), 48),
        name="gdn",
    )(proj3, proj3, proj3, proj3, ba3, conv_w, head_params)
    return out.reshape(batch * seq, kd)


def _post_kernel(x_ref, og_ref, sb_ref, sc_ref, sx_ref, ga_ref, gb_ref, mod_ref, n2g_ref, cw_ref,
                 wpa_ref, wpb_ref, wout_ref, wrh_ref, wrl_ref, br_ref,
                 x1_ref, h2_ref, route_ref, cnt_ref, win_ref, run_ref, *, per_batch):
    tm = x_ref.shape[0]
    i = pl.program_id(0)

    @pl.when(i == 0)
    def _():
        run_ref[...] = jnp.zeros(run_ref.shape, F32)

    @pl.when(i % per_batch == 0)
    def _():
        win_ref[0:SUBLANES, :] = jnp.zeros((SUBLANES, win_ref.shape[1]), F32)

    win_ref[SUBLANES:, :] = sc_ref[...].astype(F32) * sx_ref[...].astype(F32)
    conv = win_ref[pl.ds(SUBLANES, tm), :] * cw_ref[SC_CONV - 1:SC_CONV, :]
    for s in range(1, SC_CONV):
        conv = conv + win_ref[pl.ds(SUBLANES - s, tm), :] * cw_ref[SC_CONV - 1 - s:SC_CONV - s, :]
    win_ref[0:SUBLANES, :] = win_ref[pl.ds(tm, SUBLANES), :]
    y_b = _dot((sb_ref[...].astype(F32) * conv).astype(BF16), wpb_ref[...])
    y_a = _dot(og_ref[...], wpa_ref[...])
    merged = _sigmoid(ga_ref[...].astype(F32)) * y_a + _sigmoid(gb_ref[...].astype(F32)) * y_b
    mix = _dot(merged.astype(BF16), wout_ref[...])
    x1 = x_ref[...] + mod_ref[0:1, :] * mix
    x1_ref[...] = x1

    y = x1 * lax.rsqrt(jnp.mean(x1 * x1, axis=-1, keepdims=True) + EPS)
    h2 = (y * n2g_ref[...]) * (1.0 + mod_ref[1:2, :]) + mod_ref[2:3, :]
    _write_rows(h2_ref, h2)

    lg = _dot_split(h2, wrh_ref[...], wrl_ref[...]) + br_ref[...]
    lane = lax.broadcasted_iota(jnp.int32, lg.shape, 1).astype(F32)
    neg = jnp.float32(-jnp.inf)
    big = jnp.float32(2 * LANES)

    def first_max(mask):
        vmax = jnp.max(jnp.where(mask, lg, neg), axis=-1, keepdims=True)
        idx = jnp.min(jnp.where(mask & (lg == vmax), lane, big), axis=-1, keepdims=True)
        return vmax, idx

    gmask = lane < N_GROUPS
    g_max, g_sel = first_max(gmask)
    p_group = 1.0 / jnp.sum(jnp.where(gmask, jnp.exp(lg - g_max), 0.0), axis=-1, keepdims=True)
    e_lo = N_GROUPS + EXPERTS_PER_GROUP * g_sel
    emask = (lane >= e_lo) & (lane < e_lo + EXPERTS_PER_GROUP)
    v1, i1 = first_max(emask)
    v2, i2 = first_max(emask & (lane != i1))
    ex = jnp.exp(v2 - v1)
    w1 = p_group * (1.0 / (1.0 + ex))
    w2 = p_group * (ex / (1.0 + ex))
    e1 = i1 - N_GROUPS
    e2 = i2 - N_GROUPS

    onehot = jnp.where((lane == e1) | (lane == e2), 1.0, 0.0).astype(F32)
    row = lax.broadcasted_iota(jnp.int32, (tm, tm), 0)
    col = lax.broadcasted_iota(jnp.int32, (tm, tm), 1)
    before = jnp.where(row > col, 1.0, 0.0).astype(BF16)
    seen = _dot(before, onehot.astype(BF16)) + run_ref[0:1, :]
    r1 = jnp.sum(jnp.where(lane == e1, seen, 0.0), axis=-1, keepdims=True)
    r2 = jnp.sum(jnp.where(lane == e2, seen, 0.0), axis=-1, keepdims=True)
    run_ref[0:1, :] = run_ref[0:1, :] + jnp.sum(onehot, axis=0, keepdims=True)
    cnt_ref[...] = jnp.broadcast_to(run_ref[0:1, :], cnt_ref.shape)

    out = jnp.where(lane == 0, e1, 0.0)
    out = jnp.where(lane == 1, e2, out)
    out = jnp.where(lane == 2, w1, out)
    out = jnp.where(lane == 3, w2, out)
    out = jnp.where(lane == 4, r1, out)
    out = jnp.where(lane == 5, r2, out)
    route_ref[...] = out


def _post(x2, og, proj, mod, n2g, conv_w, wpa, wpb, wout, w_route_hi, w_route_lo, b_route, seq, tm):
    m, d = x2.shape
    assert d == SUBLANES * LANES
    per_batch = seq // tm

    def rows(j):
        return pl.BlockSpec((tm, d), lambda i: (i, j))

    def whole(shape):
        return pl.BlockSpec(shape, lambda i: tuple(0 for _ in shape))

    return pl.pallas_call(
        functools.partial(_post_kernel, per_batch=per_batch),
        grid=(m // tm,),
        in_specs=[
            rows(0), rows(0), rows(4), rows(5), rows(6), rows(7), rows(8),
            pl.BlockSpec((None, SUBLANES, d), lambda i: (i // per_batch, 0, 0)),
            whole((1, d)), whole((SC_CONV, d)),
            whole((d, d)), whole((d, d)), whole((d, d)),
            whole((d, LANES)), whole((d, LANES)), whole((1, LANES)),
        ],
        out_specs=[
            rows(0), pl.BlockSpec((tm * SUBLANES, LANES), lambda i: (i, 0)),
            pl.BlockSpec((tm, LANES), lambda i: (i, 0)),
            pl.BlockSpec((SUBLANES, LANES), lambda i: (0, 0)),
        ],
        out_shape=[
            jax.ShapeDtypeStruct((m, d), F32),
            jax.ShapeDtypeStruct((m * SUBLANES, LANES), F32),
            jax.ShapeDtypeStruct((m, LANES), F32),
            jax.ShapeDtypeStruct((SUBLANES, LANES), F32),
        ],
        scratch_shapes=[
            pltpu.VMEM((tm + SUBLANES, d), F32),
            pltpu.VMEM((SUBLANES, LANES), F32),
        ],
        compiler_params=_params(("arbitrary",), 56),
        name="post",
    )(x2, og, proj, proj, proj, proj, proj, mod, n2g, conv_w, wpa, wpb, wout, w_route_hi, w_route_lo,
      b_route)


def _plan_kernel(route_ref, cnt_ref, d_ref):
    bm = EXPERT_BLOCK
    sizes = cnt_ref[...]
    padded = jnp.floor((sizes + (bm - 1.0)) * (1.0 / bm)) * bm
    lane_i = lax.broadcasted_iota(jnp.int32, sizes.shape, 1)
    incl = padded
    s = 1
    while s < LANES:
        incl = incl + jnp.where(lane_i >= s, pltpu.roll(incl, s, 1), 0.0)
        s *= 2
    start = (incl - padded)[0:1, :]
    r = route_ref[...]
    lane = lax.broadcasted_iota(jnp.int32, r.shape, 1).astype(F32)
    d1 = jnp.sum(jnp.where(lane == r[:, 0:1], start, 0.0), axis=-1, keepdims=True) + r[:, 4:5]
    d2 = jnp.sum(jnp.where(lane == r[:, 1:2], start, 0.0), axis=-1, keepdims=True) + r[:, 5:6]
    out = jnp.where(lane == 0.0, d1, jnp.where(lane == 1.0, d2, 0.0))
    d_ref[...] = out.T[0:SUBLANES, :].astype(jnp.int32)


def _plan(route, counts, tm):
    m = route.shape[0]
    return pl.pallas_call(
        _plan_kernel,
        grid=(m // tm,),
        in_specs=[
            pl.BlockSpec((tm, LANES), lambda i: (i, 0)),
            pl.BlockSpec((SUBLANES, LANES), lambda i: (0, 0)),
        ],
        out_specs=pl.BlockSpec((SUBLANES, tm), lambda i: (0, i)),
        out_shape=jax.ShapeDtypeStruct((SUBLANES, m), jnp.int32),
        compiler_params=_params(("arbitrary",), 16),
        name="plan",
    )(route, counts)


def _row_copy(src_ref, src_row, dst_ref, dst_row, sem):
    return pltpu.make_async_copy(_row_tile(src_ref, src_row), _row_tile(dst_ref, dst_row), sem)


def _rows_copy(src_ref, dst_ref, dst_row, n, sem):
    return pltpu.make_async_copy(src_ref, dst_ref.at[pl.ds(dst_row * SUBLANES, n * SUBLANES), :], sem)


def _dispatch_kernel(d1_ref, d2_ref, tail_ref, h2_ref, xs_ref, zero_ref, sem):
    tm = h2_ref.shape[0] // SUBLANES
    bm = zero_ref.shape[0] // SUBLANES
    t0 = pl.program_id(0) * tm

    @pl.when(pl.program_id(0) == 0)
    def _():
        zero_ref[...] = jnp.zeros(zero_ref.shape, F32)
        n_blocks = xs_ref.shape[0] // (bm * SUBLANES)

        def zero_block(row):
            return _rows_copy(zero_ref, xs_ref, pl.multiple_of(row, bm), bm, sem.at[0])

        for e in range(N_EXPERTS):
            @pl.when(tail_ref[e] >= 0)
            def _():
                zero_block(tail_ref[e]).start()
        lax.fori_loop(tail_ref[N_EXPERTS], n_blocks, lambda j, c: (zero_block(j * bm).start(), c)[1], 0)
        for e in range(N_EXPERTS):
            @pl.when(tail_ref[e] >= 0)
            def _():
                zero_block(tail_ref[e]).wait()
        lax.fori_loop(tail_ref[N_EXPERTS], n_blocks, lambda j, c: (zero_block(j * bm).wait(), c)[1], 0)

    def start(r, carry):
        _row_copy(h2_ref, r, xs_ref, d1_ref[t0 + r], sem.at[0]).start(priority=0)
        _row_copy(h2_ref, r, xs_ref, d2_ref[t0 + r], sem.at[1]).start(priority=1)
        return carry

    lax.fori_loop(0, tm, start, 0, unroll=DMA_UNROLL)
    _rows_copy(h2_ref, xs_ref, 0, tm, sem.at[0]).wait()
    _rows_copy(h2_ref, xs_ref, 0, tm, sem.at[1]).wait()


def _dispatch(d1, d2, tail_start, h2, cap, tm):
    m = h2.shape[0] // SUBLANES
    return pl.pallas_call(
        _dispatch_kernel,
        grid_spec=pltpu.PrefetchScalarGridSpec(
            num_scalar_prefetch=3,
            grid=(m // tm,),
            in_specs=[pl.BlockSpec((tm * SUBLANES, LANES), lambda i, d1, d2, tl: (i, 0))],
            out_specs=pl.BlockSpec(memory_space=pl.ANY),
            scratch_shapes=[pltpu.VMEM((EXPERT_BLOCK * SUBLANES, LANES), F32),
                            pltpu.SemaphoreType.DMA((2,))],
        ),
        out_shape=jax.ShapeDtypeStruct((cap * SUBLANES, LANES), F32),
        compiler_params=_params(("arbitrary",), 24),
        name="dispatch",
    )(d1, d2, tail_start, h2)


def _expert_kernel(be_ref, act_ref, x_ref, w1_ref, w3_ref, w2_ref, y_ref, w1b_ref, w3b_ref, w2b_ref):
    b = pl.program_id(0)

    @pl.when(act_ref[b] > 0)
    def _():
        @pl.when((b == 0) | (be_ref[b] != be_ref[jnp.maximum(b - 1, 0)]))
        def _():
            w1b_ref[...] = w1_ref[...].astype(BF16)
            w3b_ref[...] = w3_ref[...].astype(BF16)
            w2b_ref[...] = w2_ref[...].astype(BF16)

        xb = _read_rows(x_ref, x_ref.shape[0] // SUBLANES).astype(BF16)
        hid = _silu(_dot(xb, w1b_ref[...])) * _dot(xb, w3b_ref[...])
        _write_rows(y_ref, _dot(hid.astype(BF16), w2b_ref[...]))

    @pl.when(act_ref[b] == 0)
    def _():
        y_ref[...] = jnp.zeros(y_ref.shape, F32)


def _experts(block_expert, block_active, xs, w1, w3, w2):
    cap = xs.shape[0] // SUBLANES
    d, de = w1.shape[1], w1.shape[2]
    bm = EXPERT_BLOCK

    def x_block(b, be, act):
        return (jnp.minimum(b, jnp.maximum(act[cap // bm], 1) - 1), 0)

    return pl.pallas_call(
        _expert_kernel,
        grid_spec=pltpu.PrefetchScalarGridSpec(
            num_scalar_prefetch=2,
            grid=(cap // bm,),
            in_specs=[
                pl.BlockSpec((bm * SUBLANES, LANES), x_block),
                pl.BlockSpec((None, d, de), lambda b, be, act: (be[b], 0, 0)),
                pl.BlockSpec((None, d, de), lambda b, be, act: (be[b], 0, 0)),
                pl.BlockSpec((None, de, d), lambda b, be, act: (be[b], 0, 0)),
            ],
            out_specs=pl.BlockSpec((bm * SUBLANES, LANES), lambda b, be, act: (b, 0)),
            scratch_shapes=[pltpu.VMEM((d, de), BF16), pltpu.VMEM((d, de), BF16),
                            pltpu.VMEM((de, d), BF16)],
        ),
        out_shape=jax.ShapeDtypeStruct((cap * SUBLANES, LANES), F32),
        compiler_params=_params(("arbitrary",), 40),
        name="experts",
    )(block_expert, block_active, xs, w1, w3, w2)


def _final_kernel(d1_ref, d2_ref, x1_ref, route_ref, gt_ref, nfg_ref, ys_ref, o_ref, buf_ref, sem):
    tm = x1_ref.shape[0]
    t0 = pl.program_id(0) * tm

    def start(r, carry):
        _row_copy(ys_ref, d1_ref[t0 + r], buf_ref.at[0], r, sem.at[0]).start(priority=0)
        _row_copy(ys_ref, d2_ref[t0 + r], buf_ref.at[1], r, sem.at[1]).start(priority=1)
        return carry

    lax.fori_loop(0, tm, start, 0, unroll=DMA_UNROLL)
    for k in range(2):
        pltpu.make_async_copy(ys_ref.at[pl.ds(0, tm * SUBLANES), :], buf_ref.at[k], sem.at[k]).wait()
    route = route_ref[...]
    moe = _read_rows(buf_ref.at[0], tm) * route[:, 2:3] + _read_rows(buf_ref.at[1], tm) * route[:, 3:4]
    x2 = x1_ref[...] + gt_ref[...] * moe
    y = x2 * lax.rsqrt(jnp.mean(x2 * x2, axis=-1, keepdims=True) + EPS)
    o_ref[...] = y * nfg_ref[...]


def _final(d1, d2, x1, route, gt2, nfg, ys, seq, tm):
    m, d = x1.shape
    per_batch = seq // tm
    return pl.pallas_call(
        _final_kernel,
        grid_spec=pltpu.PrefetchScalarGridSpec(
            num_scalar_prefetch=2,
            grid=(m // tm,),
            in_specs=[
                pl.BlockSpec((tm, d), lambda i, d1, d2: (i, 0)),
                pl.BlockSpec((tm, LANES), lambda i, d1, d2: (i, 0)),
                pl.BlockSpec((None, 1, d), lambda i, d1, d2: (i // per_batch, 0, 0)),
                pl.BlockSpec((1, d), lambda i, d1, d2: (0, 0)),
                pl.BlockSpec(memory_space=pl.ANY),
            ],
            out_specs=pl.BlockSpec((tm, d), lambda i, d1, d2: (i, 0)),
            scratch_shapes=[pltpu.VMEM((2, tm * SUBLANES, LANES), F32), pltpu.SemaphoreType.DMA((2,))],
        ),
        out_shape=jax.ShapeDtypeStruct((m, d), F32),
        compiler_params=_params(("arbitrary",), 32),
        name="final",
    )(d1, d2, x1, route, gt2, nfg, ys)


def _tile(n, pref):
    t = min(n, pref)
    assert n % t == 0
    return t


def kernel(x, c, w_ada, b_ada, norm1_g, w_in, conv_qkv_w, a_log, dt_bias, onorm_g, w_proj_a,
           conv_sc_w, w_proj_b, w_out, norm2_g, w_group, b_group, w_expert, b_expert, w1, w3, w2,
           normf_g):
    batch, seq, d = x.shape
    depth = w_ada.shape[0]
    m = batch * seq
    kd = N_HEADS * HEAD_D
    assert d == kd and seq % CHUNK == 0 and batch <= SUBLANES
    c_pad = jnp.zeros((SUBLANES, d), F32).at[:batch].set(c)
    x2 = x.reshape(m, d)

    for l in range(depth):
        mod = _ada(c_pad, w_ada[l], b_ada[l][None, :])[:batch]
        sh1, sc1, gt1, sh2, sc2, gt2 = [mod[:, None, j * d:(j + 1) * d] for j in range(6)]

        w = w_in[l]
        o_ba = 3 * kd + kd
        w_main = jnp.concatenate([w[:, :o_ba], w[:, o_ba + 2 * N_HEADS:]], axis=1).astype(BF16)
        w_ba = jnp.zeros((d, LANES), F32).at[:, :2 * N_HEADS].set(w[:, o_ba:o_ba + 2 * N_HEADS]).astype(BF16)
        proj, ba = _inproj(x2, norm1_g[l][None, :], sc1, sh1, w_main, w_ba, seq, _tile(seq, INPROJ_ROWS))

        head_params = jnp.zeros((SUBLANES, LANES), F32)
        head_params = head_params.at[0, N_HEADS:2 * N_HEADS].set(a_log[l])
        head_params = head_params.at[1, N_HEADS:2 * N_HEADS].set(dt_bias[l])
        head_params = head_params.at[2, :].set(onorm_g[l])
        conv_taps = jnp.broadcast_to(conv_qkv_w[l][:, None, :], (QKV_CONV, SUBLANES, 3 * kd))
        og = _gdn(proj, ba, conv_taps, head_params, batch, seq, _tile(seq, GDN_ROWS))

        mod_post = jnp.zeros((batch, SUBLANES, d), F32)
        mod_post = mod_post.at[:, 0:1].set(gt1).at[:, 1:2].set(sc2).at[:, 2:3].set(sh2)
        w_route = jnp.zeros((d, LANES), F32)
        w_route = w_route.at[:, :N_GROUPS].set(w_group[l]).at[:, N_GROUPS:N_GROUPS + N_EXPERTS].set(w_expert[l])
        w_route_hi = w_route.astype(BF16)
        w_route_lo = (w_route - w_route_hi.astype(F32)).astype(BF16)
        b_route = jnp.zeros((1, LANES), F32)
        b_route = b_route.at[0, :N_GROUPS].set(b_group[l]).at[0, N_GROUPS:N_GROUPS + N_EXPERTS].set(b_expert[l])
        x1, h2, route, counts = _post(
            x2, og, proj, mod_post, norm2_g[l][None, :], conv_sc_w[l],
            w_proj_a[l].astype(BF16), w_proj_b[l].astype(BF16), w_out[l].astype(BF16),
            w_route_hi, w_route_lo, b_route, seq, _tile(seq, POST_ROWS))

        bm = EXPERT_BLOCK
        n_blocks = (2 * m) // bm + N_EXPERTS
        sizes = counts[0, :N_EXPERTS].astype(jnp.int32)
        padded = ((sizes + bm - 1) // bm) * bm
        pad_end = jnp.cumsum(padded)
        tail_start = jnp.concatenate([jnp.where(padded > 0, pad_end - bm, -1),
                                      pad_end[-1:] // bm]).astype(jnp.int32)
        block_row = jnp.arange(n_blocks, dtype=jnp.int32) * bm
        block_expert = jnp.minimum(jnp.sum(block_row[:, None] >= pad_end[None, :], axis=1),
                                   N_EXPERTS - 1).astype(jnp.int32)
        block_active = jnp.concatenate([(block_row < pad_end[-1]).astype(jnp.int32),
                                        (pad_end[-1:] // bm).astype(jnp.int32)])
        dest = _plan(route, counts, _tile(seq, INPROJ_ROWS))
        d1, d2 = dest[0], dest[1]

        xs = _dispatch(d1, d2, tail_start, h2, n_blocks * bm, _tile(seq, MOVE_ROWS))
        ys = _experts(block_expert, block_active, xs, w1[l], w3[l], w2[l])
        nfg = normf_g[None, :] if l == depth - 1 else jnp.ones((1, d), F32)
        x2 = _final(d1, d2, x1, route, gt2, nfg, ys, seq, _tile(seq, MOVE_ROWS))
        assert depth == 1
    return x2.reshape(batch, seq, d)
```

```python
import functools

import jax
import jax.numpy as jnp
from jax import lax
from jax.experimental import pallas as pl
from jax.experimental.pallas import tpu as pltpu

F32 = jnp.float32
BF16 = jnp.bfloat16
HIGHEST = lax.Precision.HIGHEST

N_HEADS = 8
HEAD_D = 128
CHUNK = 64
QKV_CONV = 4
SC_CONV = 3
N_GROUPS = 4
EXPERTS_PER_GROUP = 8
N_EXPERTS = N_GROUPS * EXPERTS_PER_GROUP
EPS = 1e-6

LANES = 128
SUBLANES = 8
BF16_ROWS = 16
N_PROJ_COLS = 9
EXPERT_BLOCK = 512
INPROJ_ROWS = 2048
GDN_ROWS = 256
POST_ROWS = 512
MOVE_ROWS = 512
DMA_UNROLL = 8
MIB = 1024 * 1024


def _sigmoid(x):
    return 1.0 / (1.0 + jnp.exp(-x))


def _silu(x):
    half = 0.5 * x
    return half + half * jnp.tanh(half)


def _softplus(x):
    return jnp.maximum(x, 0.0) + jnp.log(1.0 + jnp.exp(-jnp.abs(x)))


def _dot(a, b):
    return jnp.dot(a, b, preferred_element_type=F32)


def _dot_nt(a, b):
    return lax.dot_general(a, b, (((1,), (1,)), ((), ())), preferred_element_type=F32)


def _dot_hi(a, b):
    return jnp.dot(a, b, preferred_element_type=F32, precision=HIGHEST)


def _dot_split(a, b_hi, b_lo):
    a_hi = a.astype(BF16)
    a_lo = (a - a_hi.astype(F32)).astype(BF16)
    return _dot(a_hi, b_hi) + (_dot(a_lo, b_hi) + _dot(a_hi, b_lo))


def _read_rows(ref, rows):
    return jnp.concatenate([ref[pl.ds(j, rows, stride=SUBLANES), :] for j in range(SUBLANES)], axis=1)


def _write_rows(ref, value):
    rows = value.shape[0]
    for j in range(SUBLANES):
        ref[pl.ds(j, rows, stride=SUBLANES), :] = value[:, j * LANES:(j + 1) * LANES]


def _row_tile(ref, row):
    if not isinstance(row, int):
        row = pl.multiple_of(row * SUBLANES, SUBLANES)
    else:
        row = row * SUBLANES
    return ref.at[pl.ds(row, SUBLANES), :]


def _params(semantics, vmem_mib):
    return pltpu.CompilerParams(dimension_semantics=semantics, vmem_limit_bytes=vmem_mib * MIB)


def _ada_kernel(c_ref, w_ref, b_ref, o_ref):
    o_ref[...] = _dot_hi(_silu(c_ref[...]), w_ref[...]) + b_ref[...]


def _ada(c_pad, w_ada, b_ada):
    d = c_pad.shape[1]
    n = w_ada.shape[1]
    return pl.pallas_call(
        _ada_kernel,
        grid=(n // d,),
        in_specs=[
            pl.BlockSpec((SUBLANES, d), lambda j: (0, 0)),
            pl.BlockSpec((d, d), lambda j: (0, j)),
            pl.BlockSpec((1, d), lambda j: (0, j)),
        ],
        out_specs=pl.BlockSpec((SUBLANES, d), lambda j: (0, j)),
        out_shape=jax.ShapeDtypeStruct((SUBLANES, n), F32),
        compiler_params=_params(("arbitrary",), 24),
        name="ada",
    )(c_pad, w_ada, b_ada)


def _inproj_kernel(x_ref, g_ref, sc_ref, sh_ref, w_ref, wba_ref, o_ref, ba_ref, h_ref):
    @pl.when(pl.program_id(1) == 0)
    def _():
        x = x_ref[...]
        y = x * lax.rsqrt(jnp.mean(x * x, axis=-1, keepdims=True) + EPS)
        h = (y * g_ref[...]) * (1.0 + sc_ref[...]) + sh_ref[...]
        hb = h.astype(BF16)
        h_ref[...] = hb
        ba_ref[...] = _dot(hb, wba_ref[...])

    o_ref[...] = _dot(h_ref[...], w_ref[...]).astype(BF16)


def _inproj(x2, norm_g, sc, sh, w_main, w_ba, seq, tm):
    m, d = x2.shape
    n = w_main.shape[1]
    tn = d
    per_batch = seq // tm
    return pl.pallas_call(
        _inproj_kernel,
        grid=(m // tm, n // tn),
        in_specs=[
            pl.BlockSpec((tm, d), lambda i, j: (i, 0)),
            pl.BlockSpec((1, d), lambda i, j: (0, 0)),
            pl.BlockSpec((None, 1, d), lambda i, j: (i // per_batch, 0, 0)),
            pl.BlockSpec((None, 1, d), lambda i, j: (i // per_batch, 0, 0)),
            pl.BlockSpec((d, tn), lambda i, j: (0, j)),
            pl.BlockSpec((d, LANES), lambda i, j: (0, 0)),
        ],
        out_specs=[
            pl.BlockSpec((tm, tn), lambda i, j: (i, j)),
            pl.BlockSpec((tm, LANES), lambda i, j: (i, 0)),
        ],
        out_shape=[
            jax.ShapeDtypeStruct((m, n), BF16),
            jax.ShapeDtypeStruct((m, LANES), F32),
        ],
        scratch_shapes=[pltpu.VMEM((tm, d), BF16)],
        compiler_params=_params(("arbitrary", "arbitrary"), 56),
        name="inproj",
    )(x2, norm_g, sc, sh, w_main, w_ba)


def _bmm(a, b):
    return jnp.einsum("hmk,hkn->hmn", a.astype(BF16), b.astype(BF16), preferred_element_type=F32)


def _bmm_nt(a, b):
    return jnp.einsum("hmk,hnk->hmn", a.astype(BF16), b.astype(BF16), preferred_element_type=F32)


def _unit_lower_inverse(a):
    row = lax.broadcasted_iota(jnp.int32, a.shape[1:], 0)
    col = lax.broadcasted_iota(jnp.int32, a.shape[1:], 1)
    eye = jnp.where(row == col, 1.0, 0.0).astype(F32)
    p = eye - a
    xp = a
    n = 2
    while n < CHUNK:
        xp = _bmm(xp, xp)
        p = p + _bmm(p, xp)
        n *= 2
    return p


def _lane_sums(x):
    h, rows, width = x.shape
    ones = jnp.ones((width, width), BF16)
    return _dot(x.reshape(h * rows, width).astype(BF16), ones).reshape(h, rows, width)


def _causal_conv_silu(win, cw, k_w):
    assert k_w == 4
    tiled = (win.shape[0] // SUBLANES, SUBLANES, win.shape[1])

    def pair(x, x1, j):
        return (x.reshape(tiled) * cw[j][None] + x1.reshape(tiled) * cw[j - 1][None]).reshape(win.shape)

    win1 = pltpu.roll(win, 1, 0)
    acc = pair(win, win1, 3) + pltpu.roll(pair(win, win1, 1), 2, 0)
    return _silu(acc[SUBLANES:, :])


def _gdn_kernel(q_ref, k_ref, v_ref, z_ref, ba_ref, cw_ref, hp_ref, o_ref,
                s_ref, tail_ref, wq_ref, u_ref, ik_ref, dec_ref):
    nb, tb = q_ref.shape[0], q_ref.shape[1]
    kd = N_HEADS * HEAD_D
    nbh = nb * N_HEADS

    @pl.when(pl.program_id(0) == 0)
    def _():
        s_ref[...] = jnp.zeros(s_ref.shape, F32)
        tail_ref[...] = jnp.zeros(tail_ref.shape, F32)

    row = lax.broadcasted_iota(jnp.int32, (CHUNK, CHUNK), 0)
    col = lax.broadcasted_iota(jnp.int32, (CHUNK, CHUNK), 1)
    causal = row >= col
    strict = row > col
    tril = jnp.where(causal, 1.0, 0.0).astype(F32)
    a_log = hp_ref[0:1, :]
    dt_bias = hp_ref[1:2, :]
    onorm_g = hp_ref[2:3, :]
    zeros_half = jnp.zeros((CHUNK, HEAD_D), F32)

    def precompute(c, carry):
        base = pl.multiple_of(c * CHUNK, CHUNK)
        qs, ks, vs, betas, gcs, grs, gls = [], [], [], [], [], [], []
        for b in range(nb):
            ba = ba_ref[b, pl.ds(base, CHUNK), :]
            beta_all = _sigmoid(ba)
            g_all = -jnp.exp(a_log) * _softplus(ba + dt_bias)
            gcum = _dot_hi(tril, g_all)
            gcum_t = jnp.concatenate([gcum, gcum], axis=0).T
            for h in range(N_HEADS):
                lo, hi = h * HEAD_D, (h + 1) * HEAD_D

                def conv(ref, off):
                    cur = ref[b, pl.ds(base, CHUNK), lo:hi].astype(F32)
                    win = jnp.concatenate([tail_ref[b, :, off + lo:off + hi], cur], axis=0)
                    return _causal_conv_silu(win, cw_ref[:, :, off + lo:off + hi], QKV_CONV)

                qs.append(conv(q_ref, 0))
                ks.append(conv(k_ref, kd))
                vs.append(conv(v_ref, 2 * kd))
                betas.append(beta_all[:, h:h + 1])
                gcs.append(gcum[:, N_HEADS + h:N_HEADS + h + 1])
                grs.append(gcum_t[N_HEADS + h:N_HEADS + h + 1, 0:CHUNK])
                gls.append(gcum[CHUNK - 1:CHUNK, N_HEADS + h:N_HEADS + h + 1])
        q, k, v = jnp.stack(qs), jnp.stack(ks), jnp.stack(vs)
        beta, gc, gr, gl = jnp.stack(betas), jnp.stack(gcs), jnp.stack(grs), jnp.stack(gls)
        qn = q * (lax.rsqrt(_lane_sums(q * q) + EPS) * (HEAD_D ** -0.5))
        kn = k * lax.rsqrt(_lane_sums(k * k) + EPS)
        decay = jnp.where(causal, jnp.exp(jnp.where(causal, gc - gr, 0.0)), 0.0)
        kb = kn * beta
        e_gc = jnp.exp(gc)
        kq = _bmm_nt(jnp.concatenate([kb, qn], axis=1), kn)
        a = jnp.where(strict, kq[:, :CHUNK] * decay, 0.0)
        intra = kq[:, CHUNK:] * decay
        uw = _bmm(_unit_lower_inverse(a), jnp.concatenate([v * beta, kb * e_gc], axis=2))
        u_ref[c] = uw[:, :, :HEAD_D]
        wq_ref[c] = jnp.concatenate([uw[:, :, HEAD_D:], qn * e_gc], axis=1).astype(BF16)
        k_dec = kn * jnp.exp(gl - gc)
        k_dec_t = jnp.stack([jnp.concatenate([k_dec[i], zeros_half], axis=0).T[:, :CHUNK]
                             for i in range(nbh)])
        ik_ref[c] = jnp.concatenate([intra, k_dec_t], axis=1).astype(BF16)
        dec_ref[c] = jnp.broadcast_to(jnp.exp(gl), (nbh, 1, HEAD_D))

        last = pl.multiple_of(base + CHUNK - BF16_ROWS, BF16_ROWS)
        for b in range(nb):
            for j, ref in enumerate((q_ref, k_ref, v_ref)):
                rows = ref[b, pl.ds(last, BF16_ROWS), :].astype(F32)
                tail_ref[b, :, j * kd:(j + 1) * kd] = rows[BF16_ROWS - SUBLANES:, :]
        return carry

    def recur(c, carry):
        base = pl.multiple_of(c * CHUNK, CHUNK)
        state = s_ref[...]
        ws = _bmm(wq_ref[c], state)
        v_new = u_ref[c] - ws[:, :CHUNK]
        r = _bmm(ik_ref[c], v_new)
        o = ws[:, CHUNK:] + r[:, :CHUNK]
        s_ref[...] = state * dec_ref[c] + r[:, CHUNK:]
        on = o * lax.rsqrt(jnp.mean(o * o, axis=-1, keepdims=True) + EPS) * onorm_g
        for b in range(nb):
            for h in range(N_HEADS):
                lo, hi = h * HEAD_D, (h + 1) * HEAD_D
                z = z_ref[b, pl.ds(base, CHUNK), lo:hi].astype(F32)
                o_ref[b, pl.ds(base, CHUNK), lo:hi] = (on[b * N_HEADS + h] * _silu(z)).astype(BF16)
        return carry

    lax.fori_loop(0, tb // CHUNK, precompute, 0)
    lax.fori_loop(0, tb // CHUNK, recur, 0)


def _gdn(proj, ba, conv_w, head_params, batch, seq, tb):
    kd = N_HEADS * HEAD_D
    nc = tb // CHUNK
    nbh = batch * N_HEADS
    proj3 = proj.reshape(batch, seq, proj.shape[1])
    ba3 = ba.reshape(batch, seq, LANES)

    def col(j):
        return pl.BlockSpec((batch, tb, kd), lambda t: (0, t, j))

    out = pl.pallas_call(
        _gdn_kernel,
        grid=(seq // tb,),
        in_specs=[
            col(0), col(1), col(2), col(3),
            pl.BlockSpec((batch, tb, LANES), lambda t: (0, t, 0)),
            pl.BlockSpec((QKV_CONV, SUBLANES, 3 * kd), lambda t: (0, 0, 0)),
            pl.BlockSpec((SUBLANES, LANES), lambda t: (0, 0)),
        ],
        out_specs=pl.BlockSpec((batch, tb, kd), lambda t: (0, t, 0)),
        out_shape=jax.ShapeDtypeStruct((batch, seq, kd), BF16),
        scratch_shapes=[
            pltpu.VMEM((nbh, HEAD_D, HEAD_D), F32),
            pltpu.VMEM((batch, SUBLANES, 3 * kd), F32),
            pltpu.VMEM((nc, nbh, 2 * CHUNK, HEAD_D), BF16),
            pltpu.VMEM((nc, nbh, CHUNK, HEAD_D), F32),
            pltpu.VMEM((nc, nbh, CHUNK + HEAD_D, CHUNK), BF16),
            pltpu.VMEM((nc, nbh, 1, HEAD_D), F32),
        ],
        compiler_params=_params(("arbitrary",), 48),
        name="gdn",
    )(proj3, proj3, proj3, proj3, ba3, conv_w, head_params)
    return out.reshape(batch * seq, kd)


def _post_kernel(x_ref, og_ref, sb_ref, sc_ref, sx_ref, ga_ref, gb_ref, mod_ref, n2g_ref, cw_ref,
                 wpa_ref, wpb_ref, wout_ref, wrh_ref, wrl_ref, br_ref,
                 x1_ref, h2_ref, route_ref, cnt_ref, win_ref, run_ref, *, per_batch):
    tm = x_ref.shape[0]
    i = pl.program_id(0)

    @pl.when(i == 0)
    def _():
        run_ref[...] = jnp.zeros(run_ref.shape, F32)

    @pl.when(i % per_batch == 0)
    def _():
        win_ref[0:SUBLANES, :] = jnp.zeros((SUBLANES, win_ref.shape[1]), F32)

    win_ref[SUBLANES:, :] = sc_ref[...].astype(F32) * sx_ref[...].astype(F32)
    conv = win_ref[pl.ds(SUBLANES, tm), :] * cw_ref[SC_CONV - 1:SC_CONV, :]
    for s in range(1, SC_CONV):
        conv = conv + win_ref[pl.ds(SUBLANES - s, tm), :] * cw_ref[SC_CONV - 1 - s:SC_CONV - s, :]
    win_ref[0:SUBLANES, :] = win_ref[pl.ds(tm, SUBLANES), :]
    y_b = _dot((sb_ref[...].astype(F32) * conv).astype(BF16), wpb_ref[...])
    y_a = _dot(og_ref[...], wpa_ref[...])
    merged = _sigmoid(ga_ref[...].astype(F32)) * y_a + _sigmoid(gb_ref[...].astype(F32)) * y_b
    mix = _dot(merged.astype(BF16), wout_ref[...])
    x1 = x_ref[...] + mod_ref[0:1, :] * mix
    x1_ref[...] = x1

    y = x1 * lax.rsqrt(jnp.mean(x1 * x1, axis=-1, keepdims=True) + EPS)
    h2 = (y * n2g_ref[...]) * (1.0 + mod_ref[1:2, :]) + mod_ref[2:3, :]
    _write_rows(h2_ref, h2)

    lg = _dot_split(h2, wrh_ref[...], wrl_ref[...]) + br_ref[...]
    lane = lax.broadcasted_iota(jnp.int32, lg.shape, 1).astype(F32)
    neg = jnp.float32(-jnp.inf)
    big = jnp.float32(2 * LANES)

    def first_max(mask):
        vmax = jnp.max(jnp.where(mask, lg, neg), axis=-1, keepdims=True)
        idx = jnp.min(jnp.where(mask & (lg == vmax), lane, big), axis=-1, keepdims=True)
        return vmax, idx

    gmask = lane < N_GROUPS
    g_max, g_sel = first_max(gmask)
    p_group = 1.0 / jnp.sum(jnp.where(gmask, jnp.exp(lg - g_max), 0.0), axis=-1, keepdims=True)
    e_lo = N_GROUPS + EXPERTS_PER_GROUP * g_sel
    emask = (lane >= e_lo) & (lane < e_lo + EXPERTS_PER_GROUP)
    v1, i1 = first_max(emask)
    v2, i2 = first_max(emask & (lane != i1))
    ex = jnp.exp(v2 - v1)
    w1 = p_group * (1.0 / (1.0 + ex))
    w2 = p_group * (ex / (1.0 + ex))
    e1 = i1 - N_GROUPS
    e2 = i2 - N_GROUPS

    onehot = jnp.where((lane == e1) | (lane == e2), 1.0, 0.0).astype(F32)
    row = lax.broadcasted_iota(jnp.int32, (tm, tm), 0)
    col = lax.broadcasted_iota(jnp.int32, (tm, tm), 1)
    before = jnp.where(row > col, 1.0, 0.0).astype(BF16)
    seen = _dot(before, onehot.astype(BF16)) + run_ref[0:1, :]
    r1 = jnp.sum(jnp.where(lane == e1, seen, 0.0), axis=-1, keepdims=True)
    r2 = jnp.sum(jnp.where(lane == e2, seen, 0.0), axis=-1, keepdims=True)
    run_ref[0:1, :] = run_ref[0:1, :] + jnp.sum(onehot, axis=0, keepdims=True)
    cnt_ref[...] = jnp.broadcast_to(run_ref[0:1, :], cnt_ref.shape)

    out = jnp.where(lane == 0, e1, 0.0)
    out = jnp.where(lane == 1, e2, out)
    out = jnp.where(lane == 2, w1, out)
    out = jnp.where(lane == 3, w2, out)
    out = jnp.where(lane == 4, r1, out)
    out = jnp.where(lane == 5, r2, out)
    route_ref[...] = out


def _post(x2, og, proj, mod, n2g, conv_w, wpa, wpb, wout, w_route_hi, w_route_lo, b_route, seq, tm):
    m, d = x2.shape
    assert d == SUBLANES * LANES
    per_batch = seq // tm

    def rows(j):
        return pl.BlockSpec((tm, d), lambda i: (i, j))

    def whole(shape):
        return pl.BlockSpec(shape, lambda i: tuple(0 for _ in shape))

    return pl.pallas_call(
        functools.partial(_post_kernel, per_batch=per_batch),
        grid=(m // tm,),
        in_specs=[
            rows(0), rows(0), rows(4), rows(5), rows(6), rows(7), rows(8),
            pl.BlockSpec((None, SUBLANES, d), lambda i: (i // per_batch, 0, 0)),
            whole((1, d)), whole((SC_CONV, d)),
            whole((d, d)), whole((d, d)), whole((d, d)),
            whole((d, LANES)), whole((d, LANES)), whole((1, LANES)),
        ],
        out_specs=[
            rows(0), pl.BlockSpec((tm * SUBLANES, LANES), lambda i: (i, 0)),
            pl.BlockSpec((tm, LANES), lambda i: (i, 0)),
            pl.BlockSpec((SUBLANES, LANES), lambda i: (0, 0)),
        ],
        out_shape=[
            jax.ShapeDtypeStruct((m, d), F32),
            jax.ShapeDtypeStruct((m * SUBLANES, LANES), F32),
            jax.ShapeDtypeStruct((m, LANES), F32),
            jax.ShapeDtypeStruct((SUBLANES, LANES), F32),
        ],
        scratch_shapes=[
            pltpu.VMEM((tm + SUBLANES, d), F32),
            pltpu.VMEM((SUBLANES, LANES), F32),
        ],
        compiler_params=_params(("arbitrary",), 56),
        name="post",
    )(x2, og, proj, proj, proj, proj, proj, mod, n2g, conv_w, wpa, wpb, wout, w_route_hi, w_route_lo,
      b_route)


def _plan_kernel(route_ref, cnt_ref, d_ref):
    bm = EXPERT_BLOCK
    sizes = cnt_ref[...]
    padded = jnp.floor((sizes + (bm - 1.0)) * (1.0 / bm)) * bm
    lane_i = lax.broadcasted_iota(jnp.int32, sizes.shape, 1)
    incl = padded
    s = 1
    while s < LANES:
        incl = incl + jnp.where(lane_i >= s, pltpu.roll(incl, s, 1), 0.0)
        s *= 2
    start = (incl - padded)[0:1, :]
    r = route_ref[...]
    lane = lax.broadcasted_iota(jnp.int32, r.shape, 1).astype(F32)
    d1 = jnp.sum(jnp.where(lane == r[:, 0:1], start, 0.0), axis=-1, keepdims=True) + r[:, 4:5]
    d2 = jnp.sum(jnp.where(lane == r[:, 1:2], start, 0.0), axis=-1, keepdims=True) + r[:, 5:6]
    out = jnp.where(lane == 0.0, d1, jnp.where(lane == 1.0, d2, 0.0))
    d_ref[...] = out.T[0:SUBLANES, :].astype(jnp.int32)


def _plan(route, counts, tm):
    m = route.shape[0]
    return pl.pallas_call(
        _plan_kernel,
        grid=(m // tm,),
        in_specs=[
            pl.BlockSpec((tm, LANES), lambda i: (i, 0)),
            pl.BlockSpec((SUBLANES, LANES), lambda i: (0, 0)),
        ],
        out_specs=pl.BlockSpec((SUBLANES, tm), lambda i: (0, i)),
        out_shape=jax.ShapeDtypeStruct((SUBLANES, m), jnp.int32),
        compiler_params=_params(("arbitrary",), 16),
        name="plan",
    )(route, counts)


def _row_copy(src_ref, src_row, dst_ref, dst_row, sem):
    return pltpu.make_async_copy(_row_tile(src_ref, src_row), _row_tile(dst_ref, dst_row), sem)


def _rows_copy(src_ref, dst_ref, dst_row, n, sem):
    return pltpu.make_async_copy(src_ref, dst_ref.at[pl.ds(dst_row * SUBLANES, n * SUBLANES), :], sem)


def _dispatch_kernel(d1_ref, d2_ref, tail_ref, h2_ref, xs_ref, zero_ref, sem):
    tm = h2_ref.shape[0] // SUBLANES
    bm = zero_ref.shape[0] // SUBLANES
    t0 = pl.program_id(0) * tm

    @pl.when(pl.program_id(0) == 0)
    def _():
        zero_ref[...] = jnp.zeros(zero_ref.shape, F32)
        n_blocks = xs_ref.shape[0] // (bm * SUBLANES)

        def zero_block(row):
            return _rows_copy(zero_ref, xs_ref, pl.multiple_of(row, bm), bm, sem.at[0])

        for e in range(N_EXPERTS):
            @pl.when(tail_ref[e] >= 0)
            def _():
                zero_block(tail_ref[e]).start()
        lax.fori_loop(tail_ref[N_EXPERTS], n_blocks, lambda j, c: (zero_block(j * bm).start(), c)[1], 0)
        for e in range(N_EXPERTS):
            @pl.when(tail_ref[e] >= 0)
            def _():
                zero_block(tail_ref[e]).wait()
        lax.fori_loop(tail_ref[N_EXPERTS], n_blocks, lambda j, c: (zero_block(j * bm).wait(), c)[1], 0)

    def start(r, carry):
        _row_copy(h2_ref, r, xs_ref, d1_ref[t0 + r], sem.at[0]).start(priority=0)
        _row_copy(h2_ref, r, xs_ref, d2_ref[t0 + r], sem.at[1]).start(priority=1)
        return carry

    lax.fori_loop(0, tm, start, 0, unroll=DMA_UNROLL)
    _rows_copy(h2_ref, xs_ref, 0, tm, sem.at[0]).wait()
    _rows_copy(h2_ref, xs_ref, 0, tm, sem.at[1]).wait()


def _dispatch(d1, d2, tail_start, h2, cap, tm):
    m = h2.shape[0] // SUBLANES
    return pl.pallas_call(
        _dispatch_kernel,
        grid_spec=pltpu.PrefetchScalarGridSpec(
            num_scalar_prefetch=3,
            grid=(m // tm,),
            in_specs=[pl.BlockSpec((tm * SUBLANES, LANES), lambda i, d1, d2, tl: (i, 0))],
            out_specs=pl.BlockSpec(memory_space=pl.ANY),
            scratch_shapes=[pltpu.VMEM((EXPERT_BLOCK * SUBLANES, LANES), F32),
                            pltpu.SemaphoreType.DMA((2,))],
        ),
        out_shape=jax.ShapeDtypeStruct((cap * SUBLANES, LANES), F32),
        compiler_params=_params(("arbitrary",), 24),
        name="dispatch",
    )(d1, d2, tail_start, h2)


def _expert_kernel(be_ref, act_ref, x_ref, w1_ref, w3_ref, w2_ref, y_ref, w1b_ref, w3b_ref, w2b_ref):
    b = pl.program_id(0)

    @pl.when(act_ref[b] > 0)
    def _():
        @pl.when((b == 0) | (be_ref[b] != be_ref[jnp.maximum(b - 1, 0)]))
        def _():
            w1b_ref[...] = w1_ref[...].astype(BF16)
            w3b_ref[...] = w3_ref[...].astype(BF16)
            w2b_ref[...] = w2_ref[...].astype(BF16)

        xb = _read_rows(x_ref, x_ref.shape[0] // SUBLANES).astype(BF16)
        hid = _silu(_dot(xb, w1b_ref[...])) * _dot(xb, w3b_ref[...])
        _write_rows(y_ref, _dot(hid.astype(BF16), w2b_ref[...]))

    @pl.when(act_ref[b] == 0)
    def _():
        y_ref[...] = jnp.zeros(y_ref.shape, F32)


def _experts(block_expert, block_active, xs, w1, w3, w2):
    cap = xs.shape[0] // SUBLANES
    d, de = w1.shape[1], w1.shape[2]
    bm = EXPERT_BLOCK

    def x_block(b, be, act):
        return (jnp.minimum(b, jnp.maximum(act[cap // bm], 1) - 1), 0)

    return pl.pallas_call(
        _expert_kernel,
        grid_spec=pltpu.PrefetchScalarGridSpec(
            num_scalar_prefetch=2,
            grid=(cap // bm,),
            in_specs=[
                pl.BlockSpec((bm * SUBLANES, LANES), x_block),
                pl.BlockSpec((None, d, de), lambda b, be, act: (be[b], 0, 0)),
                pl.BlockSpec((None, d, de), lambda b, be, act: (be[b], 0, 0)),
                pl.BlockSpec((None, de, d), lambda b, be, act: (be[b], 0, 0)),
            ],
            out_specs=pl.BlockSpec((bm * SUBLANES, LANES), lambda b, be, act: (b, 0)),
            scratch_shapes=[pltpu.VMEM((d, de), BF16), pltpu.VMEM((d, de), BF16),
                            pltpu.VMEM((de, d), BF16)],
        ),
        out_shape=jax.ShapeDtypeStruct((cap * SUBLANES, LANES), F32),
        compiler_params=_params(("arbitrary",), 40),
        name="experts",
    )(block_expert, block_active, xs, w1, w3, w2)


def _final_kernel(d1_ref, d2_ref, x1_ref, route_ref, gt_ref, nfg_ref, ys_ref, o_ref, buf_ref, sem):
    tm = x1_ref.shape[0]
    i = pl.program_id(0)

    def gather(tile, s):
        t0 = tile * tm

        def start(r, carry):
            _row_copy(ys_ref, d1_ref[t0 + r], buf_ref.at[s, 0], r, sem.at[s, 0]).start(priority=0)
            _row_copy(ys_ref, d2_ref[t0 + r], buf_ref.at[s, 1], r, sem.at[s, 1]).start(priority=1)
            return carry

        lax.fori_loop(0, tm, start, 0, unroll=DMA_UNROLL)

    def combine(cur, nxt):
        @pl.when(i + 1 < pl.num_programs(0))
        def _():
            gather(i + 1, nxt)

        for k in range(2):
            pltpu.make_async_copy(ys_ref.at[pl.ds(0, tm * SUBLANES), :], buf_ref.at[cur, k],
                                  sem.at[cur, k]).wait()
        route = route_ref[...]
        moe = (_read_rows(buf_ref.at[cur, 0], tm) * route[:, 2:3]
               + _read_rows(buf_ref.at[cur, 1], tm) * route[:, 3:4])
        x2 = x1_ref[...] + gt_ref[...] * moe
        y = x2 * lax.rsqrt(jnp.mean(x2 * x2, axis=-1, keepdims=True) + EPS)
        o_ref[...] = y * nfg_ref[...]

    @pl.when(i == 0)
    def _():
        gather(0, 0)

    @pl.when(i % 2 == 0)
    def _():
        combine(0, 1)

    @pl.when(i % 2 == 1)
    def _():
        combine(1, 0)


def _final(d1, d2, x1, route, gt2, nfg, ys, seq, tm):
    m, d = x1.shape
    per_batch = seq // tm
    return pl.pallas_call(
        _final_kernel,
        grid_spec=pltpu.PrefetchScalarGridSpec(
            num_scalar_prefetch=2,
            grid=(m // tm,),
            in_specs=[
                pl.BlockSpec((tm, d), lambda i, d1, d2: (i, 0)),
                pl.BlockSpec((tm, LANES), lambda i, d1, d2: (i, 0)),
                pl.BlockSpec((None, 1, d), lambda i, d1, d2: (i // per_batch, 0, 0)),
                pl.BlockSpec((1, d), lambda i, d1, d2: (0, 0)),
                pl.BlockSpec(memory_space=pl.ANY),
            ],
            out_specs=pl.BlockSpec((tm, d), lambda i, d1, d2: (i, 0)),
            scratch_shapes=[pltpu.VMEM((2, 2, tm * SUBLANES, LANES), F32),
                            pltpu.SemaphoreType.DMA((2, 2))],
        ),
        out_shape=jax.ShapeDtypeStruct((m, d), F32),
        compiler_params=_params(("arbitrary",), 32),
        name="final",
    )(d1, d2, x1, route, gt2, nfg, ys)


def _tile(n, pref):
    t = min(n, pref)
    assert n % t == 0
    return t


def kernel(x, c, w_ada, b_ada, norm1_g, w_in, conv_qkv_w, a_log, dt_bias, onorm_g, w_proj_a,
           conv_sc_w, w_proj_b, w_out, norm2_g, w_group, b_group, w_expert, b_expert, w1, w3, w2,
           normf_g):
    batch, seq, d = x.shape
    depth = w_ada.shape[0]
    m = batch * seq
    kd = N_HEADS * HEAD_D
    assert d == kd and seq % CHUNK == 0 and batch <= SUBLANES
    c_pad = jnp.zeros((SUBLANES, d), F32).at[:batch].set(c)
    x2 = x.reshape(m, d)

    for l in range(depth):
        mod = _ada(c_pad, w_ada[l], b_ada[l][None, :])[:batch]
        sh1, sc1, gt1, sh2, sc2, gt2 = [mod[:, None, j * d:(j + 1) * d] for j in range(6)]

        w = w_in[l]
        o_ba = 3 * kd + kd
        w_main = jnp.concatenate([w[:, :o_ba], w[:, o_ba + 2 * N_HEADS:]], axis=1).astype(BF16)
        w_ba = jnp.zeros((d, LANES), F32).at[:, :2 * N_HEADS].set(w[:, o_ba:o_ba + 2 * N_HEADS]).astype(BF16)
        proj, ba = _inproj(x2, norm1_g[l][None, :], sc1, sh1, w_main, w_ba, seq, _tile(seq, INPROJ_ROWS))

        head_params = jnp.zeros((SUBLANES, LANES), F32)
        head_params = head_params.at[0, N_HEADS:2 * N_HEADS].set(a_log[l])
        head_params = head_params.at[1, N_HEADS:2 * N_HEADS].set(dt_bias[l])
        head_params = head_params.at[2, :].set(onorm_g[l])
        conv_taps = jnp.broadcast_to(conv_qkv_w[l][:, None, :], (QKV_CONV, SUBLANES, 3 * kd))
        og = _gdn(proj, ba, conv_taps, head_params, batch, seq, _tile(seq, GDN_ROWS))

        mod_post = jnp.zeros((batch, SUBLANES, d), F32)
        mod_post = mod_post.at[:, 0:1].set(gt1).at[:, 1:2].set(sc2).at[:, 2:3].set(sh2)
        w_route = jnp.zeros((d, LANES), F32)
        w_route = w_route.at[:, :N_GROUPS].set(w_group[l]).at[:, N_GROUPS:N_GROUPS + N_EXPERTS].set(w_expert[l])
        w_route_hi = w_route.astype(BF16)
        w_route_lo = (w_route - w_route_hi.astype(F32)).astype(BF16)
        b_route = jnp.zeros((1, LANES), F32)
        b_route = b_route.at[0, :N_GROUPS].set(b_group[l]).at[0, N_GROUPS:N_GROUPS + N_EXPERTS].set(b_expert[l])
        x1, h2, route, counts = _post(
            x2, og, proj, mod_post, norm2_g[l][None, :], conv_sc_w[l],
            w_proj_a[l].astype(BF16), w_proj_b[l].astype(BF16), w_out[l].astype(BF16),
            w_route_hi, w_route_lo, b_route, seq, _tile(seq, POST_ROWS))

        bm = EXPERT_BLOCK
        n_blocks = (2 * m) // bm + N_EXPERTS
        sizes = counts[0, :N_EXPERTS].astype(jnp.int32)
        padded = ((sizes + bm - 1) // bm) * bm
        pad_end = jnp.cumsum(padded)
        tail_start = jnp.concatenate([jnp.where(padded > 0, pad_end - bm, -1),
                                      pad_end[-1:] // bm]).astype(jnp.int32)
        block_row = jnp.arange(n_blocks, dtype=jnp.int32) * bm
        block_expert = jnp.minimum(jnp.sum(block_row[:, None] >= pad_end[None, :], axis=1),
                                   N_EXPERTS - 1).astype(jnp.int32)
        block_active = jnp.concatenate([(block_row < pad_end[-1]).astype(jnp.int32),
                                        (pad_end[-1:] // bm).astype(jnp.int32)])
        dest = _plan(route, counts, _tile(seq, INPROJ_ROWS))
        d1, d2 = dest[0], dest[1]

        xs = _dispatch(d1, d2, tail_start, h2, n_blocks * bm, _tile(seq, MOVE_ROWS))
        ys = _experts(block_expert, block_active, xs, w1[l], w3[l], w2[l])
        nfg = normf_g[None, :] if l == depth - 1 else jnp.ones((1, d), F32)
        x2 = _final(d1, d2, x1, route, gt2, nfg, ys, seq, _tile(seq, MOVE_ROWS))
        assert depth == 1
    return x2.reshape(batch, seq, d)
```

```python
import functools

import jax
import jax.numpy as jnp
from jax import lax
from jax.experimental import pallas as pl
from jax.experimental.pallas import tpu as pltpu

F32 = jnp.float32
BF16 = jnp.bfloat16
HIGHEST = lax.Precision.HIGHEST

N_HEADS = 8
HEAD_D = 128
CHUNK = 64
QKV_CONV = 4
SC_CONV = 3
N_GROUPS = 4
EXPERTS_PER_GROUP = 8
N_EXPERTS = N_GROUPS * EXPERTS_PER_GROUP
EPS = 1e-6

LANES = 128
SUBLANES = 8
BF16_ROWS = 16
N_PROJ_COLS = 9
EXPERT_BLOCK = 512
INPROJ_ROWS = 2048
GDN_ROWS = 256
PRE_CHUNKS = 2
POST_ROWS = 512
MOVE_ROWS = 512
DMA_UNROLL = 8
MIB = 1024 * 1024


def _sigmoid(x):
    return 0.5 + 0.5 * jnp.tanh(0.5 * x)


def _silu(x):
    half = 0.5 * x
    return half + half * jnp.tanh(half)


def _softplus(x):
    return jnp.maximum(x, 0.0) + jnp.log(1.0 + jnp.exp(-jnp.abs(x)))


def _dot(a, b):
    return jnp.dot(a, b, preferred_element_type=F32)


def _dot_nt(a, b):
    return lax.dot_general(a, b, (((1,), (1,)), ((), ())), preferred_element_type=F32)


def _dot_hi(a, b):
    return jnp.dot(a, b, preferred_element_type=F32, precision=HIGHEST)


def _dot_split(a, b_hi, b_lo):
    a_hi = a.astype(BF16)
    a_lo = (a - a_hi.astype(F32)).astype(BF16)
    return _dot(a_hi, b_hi) + (_dot(a_lo, b_hi) + _dot(a_hi, b_lo))


def _read_rows(ref, rows):
    return jnp.concatenate([ref[pl.ds(j, rows, stride=SUBLANES), :] for j in range(SUBLANES)], axis=1)


def _write_rows(ref, value):
    rows = value.shape[0]
    for j in range(SUBLANES):
        ref[pl.ds(j, rows, stride=SUBLANES), :] = value[:, j * LANES:(j + 1) * LANES]


def _row_tile(ref, row):
    if not isinstance(row, int):
        row = pl.multiple_of(row * SUBLANES, SUBLANES)
    else:
        row = row * SUBLANES
    return ref.at[pl.ds(row, SUBLANES), :]


def _params(semantics, vmem_mib):
    return pltpu.CompilerParams(dimension_semantics=semantics, vmem_limit_bytes=vmem_mib * MIB)


def _ada_kernel(c_ref, w_ref, b_ref, o_ref):
    o_ref[...] = _dot_hi(_silu(c_ref[...]), w_ref[...]) + b_ref[...]


def _ada(c_pad, w_ada, b_ada):
    d = c_pad.shape[1]
    n = w_ada.shape[1]
    return pl.pallas_call(
        _ada_kernel,
        grid=(n // d,),
        in_specs=[
            pl.BlockSpec((SUBLANES, d), lambda j: (0, 0)),
            pl.BlockSpec((d, d), lambda j: (0, j)),
            pl.BlockSpec((1, d), lambda j: (0, j)),
        ],
        out_specs=pl.BlockSpec((SUBLANES, d), lambda j: (0, j)),
        out_shape=jax.ShapeDtypeStruct((SUBLANES, n), F32),
        compiler_params=_params(("arbitrary",), 24),
        name="ada",
    )(c_pad, w_ada, b_ada)


def _inproj_kernel(x_ref, g_ref, sc_ref, sh_ref, wa_ref, wb_ref, wba_ref, o_ref, ba_ref, h_ref, *, na):
    j = pl.program_id(1)

    @pl.when(j == 0)
    def _():
        x = x_ref[...]
        y = x * lax.rsqrt(jnp.mean(x * x, axis=-1, keepdims=True) + EPS)
        h = (y * g_ref[...]) * (1.0 + sc_ref[...]) + sh_ref[...]
        hb = h.astype(BF16)
        h_ref[...] = hb
        ba_ref[...] = _dot(hb, wba_ref[...])

    @pl.when(j < na)
    def _():
        o_ref[...] = _dot(h_ref[...], wa_ref[...]).astype(BF16)

    @pl.when(j >= na)
    def _():
        o_ref[...] = _dot(h_ref[...], wb_ref[...]).astype(BF16)


def _inproj(x2, norm_g, sc, sh, w_a, w_b, w_ba, seq, tm):
    m, d = x2.shape
    tn = d
    na, nb = w_a.shape[1] // tn, w_b.shape[1] // tn
    n = (na + nb) * tn
    per_batch = seq // tm
    return pl.pallas_call(
        functools.partial(_inproj_kernel, na=na),
        grid=(m // tm, n // tn),
        in_specs=[
            pl.BlockSpec((tm, d), lambda i, j: (i, 0)),
            pl.BlockSpec((1, d), lambda i, j: (0, 0)),
            pl.BlockSpec((None, 1, d), lambda i, j: (i // per_batch, 0, 0)),
            pl.BlockSpec((None, 1, d), lambda i, j: (i // per_batch, 0, 0)),
            pl.BlockSpec((d, tn), lambda i, j: (0, jnp.minimum(j, na - 1))),
            pl.BlockSpec((d, tn), lambda i, j: (0, jnp.maximum(j - na, 0))),
            pl.BlockSpec((d, LANES), lambda i, j: (0, 0)),
        ],
        out_specs=[
            pl.BlockSpec((tm, tn), lambda i, j: (i, j)),
            pl.BlockSpec((tm, LANES), lambda i, j: (i, 0)),
        ],
        out_shape=[
            jax.ShapeDtypeStruct((m, n), BF16),
            jax.ShapeDtypeStruct((m, LANES), F32),
        ],
        scratch_shapes=[pltpu.VMEM((tm, d), BF16)],
        compiler_params=_params(("arbitrary", "arbitrary"), 56),
        name="inproj",
    )(x2, norm_g, sc, sh, w_a, w_b, w_ba)


def _bmm(a, b):
    return jnp.einsum("hmk,hkn->hmn", a.astype(BF16), b.astype(BF16), preferred_element_type=F32)


def _bmm_nt(a, b):
    return jnp.einsum("hmk,hnk->hmn", a.astype(BF16), b.astype(BF16), preferred_element_type=F32)


def _unit_lower_inverse(a):
    row = lax.broadcasted_iota(jnp.int32, a.shape[1:], 0)
    col = lax.broadcasted_iota(jnp.int32, a.shape[1:], 1)
    apart = row ^ col
    eye = jnp.where(row == col, 1.0, 0.0).astype(F32)
    t = jnp.where(apart < 2, eye - a, 0.0)
    s = 2
    while s < CHUNK:
        coupling = jnp.where((apart >= s) & (apart < 2 * s), a, 0.0)
        t = t - _bmm(t, _bmm(coupling, t))
        s *= 2
    return t


def _lane_sums(x):
    h, rows, width = x.shape
    ones = jnp.ones((width, width), BF16)
    return _dot(x.reshape(h * rows, width).astype(BF16), ones).reshape(h, rows, width)


def _causal_conv_silu(win, cw, k_w):
    assert k_w == 4
    tiled = (win.shape[0] // SUBLANES, SUBLANES, win.shape[1])

    def pair(x, x1, j):
        return (x.reshape(tiled) * cw[j][None] + x1.reshape(tiled) * cw[j - 1][None]).reshape(win.shape)

    win1 = pltpu.roll(win, 1, 0)
    acc = pair(win, win1, 3) + pltpu.roll(pair(win, win1, 1), 2, 0)
    return _silu(acc[SUBLANES:, :])


def _gdn_kernel(q_ref, k_ref, v_ref, z_ref, ba_ref, cw_ref, hp_ref, o_ref,
                s_ref, tail_ref, wq_ref, u_ref, ik_ref, dec_ref):
    nb, tb = q_ref.shape[0], q_ref.shape[1]
    kd = N_HEADS * HEAD_D
    nbh = nb * N_HEADS

    @pl.when(pl.program_id(0) == 0)
    def _():
        s_ref[...] = jnp.zeros(s_ref.shape, F32)
        tail_ref[...] = jnp.zeros(tail_ref.shape, F32)

    row = lax.broadcasted_iota(jnp.int32, (CHUNK, CHUNK), 0)
    col = lax.broadcasted_iota(jnp.int32, (CHUNK, CHUNK), 1)
    causal = row >= col
    strict = row > col
    tril = jnp.where(causal, 1.0, 0.0).astype(F32)
    a_log = hp_ref[0:1, :]
    dt_bias = hp_ref[1:2, :]
    onorm_g = hp_ref[2:3, :]
    zeros_half = jnp.zeros((CHUNK, HEAD_D), F32)

    def precompute(cp, carry):
        qs, ks, vs, betas, gcs, grs, gls = [], [], [], [], [], [], []
        for sub in range(PRE_CHUNKS):
            base = pl.multiple_of((cp * PRE_CHUNKS + sub) * CHUNK, CHUNK)
            prev = pl.multiple_of(base - BF16_ROWS, BF16_ROWS)
            for b in range(nb):
                ba = ba_ref[b, pl.ds(base, CHUNK), :]
                beta_all = _sigmoid(ba)
                g_all = -jnp.exp(a_log) * _softplus(ba + dt_bias)
                gcum = _dot_hi(tril, g_all)
                gcum_t = jnp.concatenate([gcum, gcum], axis=0).T
                for h in range(N_HEADS):
                    lo, hi = h * HEAD_D, (h + 1) * HEAD_D

                    def conv(ref, off):
                        cur = ref[b, pl.ds(base, CHUNK), lo:hi].astype(F32)
                        if sub == 0:
                            before = tail_ref[b, :, off + lo:off + hi]
                        else:
                            before = ref[b, pl.ds(prev, BF16_ROWS), lo:hi].astype(F32)[SUBLANES:]
                        win = jnp.concatenate([before, cur], axis=0)
                        return _causal_conv_silu(win, cw_ref[:, :, off + lo:off + hi], QKV_CONV)

                    qs.append(conv(q_ref, 0))
                    ks.append(conv(k_ref, kd))
                    vs.append(conv(v_ref, 2 * kd))
                    betas.append(beta_all[:, h:h + 1])
                    gcs.append(gcum[:, N_HEADS + h:N_HEADS + h + 1])
                    grs.append(gcum_t[N_HEADS + h:N_HEADS + h + 1, 0:CHUNK])
                    gls.append(gcum[CHUNK - 1:CHUNK, N_HEADS + h:N_HEADS + h + 1])
        q, k, v = jnp.stack(qs), jnp.stack(ks), jnp.stack(vs)
        beta, gc, gr, gl = jnp.stack(betas), jnp.stack(gcs), jnp.stack(grs), jnp.stack(gls)
        qn = q * (lax.rsqrt(_lane_sums(q * q) + EPS) * (HEAD_D ** -0.5))
        kn = k * lax.rsqrt(_lane_sums(k * k) + EPS)
        decay = jnp.where(causal, jnp.exp(jnp.where(causal, gc - gr, 0.0)), 0.0)
        kb = kn * beta
        e_gc = jnp.exp(gc)
        kq = _bmm_nt(jnp.concatenate([kb, qn], axis=1), kn)
        a = jnp.where(strict, kq[:, :CHUNK] * decay, 0.0)
        intra = kq[:, CHUNK:] * decay
        uw = _bmm(_unit_lower_inverse(a), jnp.concatenate([v * beta, kb * e_gc], axis=2))
        wq = jnp.concatenate([uw[:, :, HEAD_D:], qn * e_gc], axis=1).astype(BF16)
        k_dec = kn * jnp.exp(gl - gc)
        k_dec_t = jnp.stack([jnp.concatenate([k_dec[i], zeros_half], axis=0).T[:, :CHUNK]
                             for i in range(PRE_CHUNKS * nbh)])
        ik = jnp.concatenate([intra, k_dec_t], axis=1).astype(BF16)
        dec = jnp.broadcast_to(jnp.exp(gl), (PRE_CHUNKS * nbh, 1, HEAD_D))
        for sub in range(PRE_CHUNKS):
            c = cp * PRE_CHUNKS + sub
            rows = slice(sub * nbh, (sub + 1) * nbh)
            u_ref[c] = uw[rows, :, :HEAD_D]
            wq_ref[c] = wq[rows]
            ik_ref[c] = ik[rows]
            dec_ref[c] = dec[rows]

        last = pl.multiple_of((cp + 1) * PRE_CHUNKS * CHUNK - BF16_ROWS, BF16_ROWS)
        for b in range(nb):
            for j, ref in enumerate((q_ref, k_ref, v_ref)):
                rows = ref[b, pl.ds(last, BF16_ROWS), :].astype(F32)
                tail_ref[b, :, j * kd:(j + 1) * kd] = rows[BF16_ROWS - SUBLANES:, :]
        return carry

    def recur(c, carry):
        base = pl.multiple_of(c * CHUNK, CHUNK)
        state = s_ref[...]
        ws = _bmm(wq_ref[c], state)
        v_new = u_ref[c] - ws[:, :CHUNK]
        r = _bmm(ik_ref[c], v_new)
        o = ws[:, CHUNK:] + r[:, :CHUNK]
        s_ref[...] = state * dec_ref[c] + r[:, CHUNK:]
        on = o * lax.rsqrt(jnp.mean(o * o, axis=-1, keepdims=True) + EPS) * onorm_g
        for b in range(nb):
            for h in range(N_HEADS):
                lo, hi = h * HEAD_D, (h + 1) * HEAD_D
                z = z_ref[b, pl.ds(base, CHUNK), lo:hi].astype(F32)
                o_ref[b, pl.ds(base, CHUNK), lo:hi] = (on[b * N_HEADS + h] * _silu(z)).astype(BF16)
        return carry

    lax.fori_loop(0, tb // (PRE_CHUNKS * CHUNK), precompute, 0)
    lax.fori_loop(0, tb // CHUNK, recur, 0)


def _gdn(proj, ba, conv_w, head_params, batch, seq, tb):
    kd = N_HEADS * HEAD_D
    nc = tb // CHUNK
    nbh = batch * N_HEADS
    proj3 = proj.reshape(batch, seq, proj.shape[1])
    ba3 = ba.reshape(batch, seq, LANES)

    def col(j):
        return pl.BlockSpec((batch, tb, kd), lambda t: (0, t, j))

    out = pl.pallas_call(
        _gdn_kernel,
        grid=(seq // tb,),
        in_specs=[
            col(0), col(1), col(2), col(3),
            pl.BlockSpec((batch, tb, LANES), lambda t: (0, t, 0)),
            pl.BlockSpec((QKV_CONV, SUBLANES, 3 * kd), lambda t: (0, 0, 0)),
            pl.BlockSpec((SUBLANES, LANES), lambda t: (0, 0)),
        ],
        out_specs=pl.BlockSpec((batch, tb, kd), lambda t: (0, t, 0)),
        out_shape=jax.ShapeDtypeStruct((batch, seq, kd), BF16),
        scratch_shapes=[
            pltpu.VMEM((nbh, HEAD_D, HEAD_D), F32),
            pltpu.VMEM((batch, SUBLANES, 3 * kd), F32),
            pltpu.VMEM((nc, nbh, 2 * CHUNK, HEAD_D), BF16),
            pltpu.VMEM((nc, nbh, CHUNK, HEAD_D), F32),
            pltpu.VMEM((nc, nbh, CHUNK + HEAD_D, CHUNK), BF16),
            pltpu.VMEM((nc, nbh, 1, HEAD_D), F32),
        ],
        compiler_params=_params(("arbitrary",), 48),
        name="gdn",
    )(proj3, proj3, proj3, proj3, ba3, conv_w, head_params)
    return out.reshape(batch * seq, kd)


def _post_kernel(x_ref, og_ref, sb_ref, sc_ref, sx_ref, ga_ref, gb_ref, mod_ref, n2g_ref, cw_ref,
                 wpa_ref, wpb_ref, wout_ref, wrh_ref, wrl_ref, br_ref,
                 x1_ref, h2_ref, route_ref, cnt_ref, win_ref, run_ref, *, per_batch):
    tm = x_ref.shape[0]
    i = pl.program_id(0)

    @pl.when(i == 0)
    def _():
        run_ref[...] = jnp.zeros(run_ref.shape, F32)

    @pl.when(i % per_batch == 0)
    def _():
        win_ref[...] = jnp.zeros(win_ref.shape, F32)

    assert SC_CONV == 3
    prod = sc_ref[...].astype(F32) * sx_ref[...].astype(F32)
    win = jnp.concatenate([win_ref[...], prod], axis=0)
    win_ref[...] = prod[tm - SUBLANES:, :]
    tiled = (win.shape[0] // SUBLANES, SUBLANES, win.shape[1])
    win1 = pltpu.roll(win, 1, 0)
    conv = (win.reshape(tiled) * cw_ref[2][None] + win1.reshape(tiled) * cw_ref[1][None]
            + pltpu.roll(win1, 1, 0).reshape(tiled) * cw_ref[0][None]).reshape(win.shape)[SUBLANES:, :]
    y_b = _dot((sb_ref[...].astype(F32) * conv).astype(BF16), wpb_ref[...])
    y_a = _dot(og_ref[...], wpa_ref[...])
    merged = _sigmoid(ga_ref[...].astype(F32)) * y_a + _sigmoid(gb_ref[...].astype(F32)) * y_b
    mix = _dot(merged.astype(BF16), wout_ref[...])
    x1 = x_ref[...] + mod_ref[0:1, :] * mix
    x1_ref[...] = x1

    y = x1 * lax.rsqrt(jnp.mean(x1 * x1, axis=-1, keepdims=True) + EPS)
    h2 = (y * n2g_ref[...]) * (1.0 + mod_ref[1:2, :]) + mod_ref[2:3, :]
    _write_rows(h2_ref, h2)

    lg = _dot_split(h2, wrh_ref[...], wrl_ref[...]) + br_ref[...]
    lane = lax.broadcasted_iota(jnp.int32, lg.shape, 1).astype(F32)
    neg = jnp.float32(-jnp.inf)
    big = jnp.float32(2 * LANES)

    def first_max(mask):
        vmax = jnp.max(jnp.where(mask, lg, neg), axis=-1, keepdims=True)
        idx = jnp.min(jnp.where(mask & (lg == vmax), lane, big), axis=-1, keepdims=True)
        return vmax, idx

    gmask = lane < N_GROUPS
    g_max, g_sel = first_max(gmask)
    p_group = 1.0 / jnp.sum(jnp.where(gmask, jnp.exp(lg - g_max), 0.0), axis=-1, keepdims=True)
    e_lo = N_GROUPS + EXPERTS_PER_GROUP * g_sel
    emask = (lane >= e_lo) & (lane < e_lo + EXPERTS_PER_GROUP)
    v1, i1 = first_max(emask)
    v2, i2 = first_max(emask & (lane != i1))
    ex = jnp.exp(v2 - v1)
    w1 = p_group * (1.0 / (1.0 + ex))
    w2 = p_group * (ex / (1.0 + ex))
    e1 = i1 - N_GROUPS
    e2 = i2 - N_GROUPS

    onehot = jnp.where((lane == e1) | (lane == e2), 1.0, 0.0).astype(F32)
    row = lax.broadcasted_iota(jnp.int32, (tm, tm), 0)
    col = lax.broadcasted_iota(jnp.int32, (tm, tm), 1)
    before = jnp.where(row > col, 1.0, 0.0).astype(BF16)
    seen = _dot(before, onehot.astype(BF16)) + run_ref[0:1, :]
    r1 = jnp.sum(jnp.where(lane == e1, seen, 0.0), axis=-1, keepdims=True)
    r2 = jnp.sum(jnp.where(lane == e2, seen, 0.0), axis=-1, keepdims=True)
    run_ref[0:1, :] = run_ref[0:1, :] + jnp.sum(onehot, axis=0, keepdims=True)
    cnt_ref[...] = jnp.broadcast_to(run_ref[0:1, :], cnt_ref.shape)

    out = jnp.where(lane == 0, e1, 0.0)
    out = jnp.where(lane == 1, e2, out)
    out = jnp.where(lane == 2, w1, out)
    out = jnp.where(lane == 3, w2, out)
    out = jnp.where(lane == 4, r1, out)
    out = jnp.where(lane == 5, r2, out)
    route_ref[...] = out


def _post(x2, og, proj, mod, n2g, conv_w, wpa, wpb, wout, w_route_hi, w_route_lo, b_route, seq, tm):
    m, d = x2.shape
    assert d == SUBLANES * LANES
    per_batch = seq // tm

    def rows(j):
        return pl.BlockSpec((tm, d), lambda i: (i, j))

    def whole(shape):
        return pl.BlockSpec(shape, lambda i: tuple(0 for _ in shape))

    return pl.pallas_call(
        functools.partial(_post_kernel, per_batch=per_batch),
        grid=(m // tm,),
        in_specs=[
            rows(0), rows(0), rows(4), rows(5), rows(6), rows(7), rows(8),
            pl.BlockSpec((None, SUBLANES, d), lambda i: (i // per_batch, 0, 0)),
            whole((1, d)), whole((SC_CONV, SUBLANES, d)),
            whole((d, d)), whole((d, d)), whole((d, d)),
            whole((d, LANES)), whole((d, LANES)), whole((1, LANES)),
        ],
        out_specs=[
            rows(0), pl.BlockSpec((tm * SUBLANES, LANES), lambda i: (i, 0)),
            pl.BlockSpec((tm, LANES), lambda i: (i, 0)),
            pl.BlockSpec((SUBLANES, LANES), lambda i: (0, 0)),
        ],
        out_shape=[
            jax.ShapeDtypeStruct((m, d), F32),
            jax.ShapeDtypeStruct((m * SUBLANES, LANES), F32),
            jax.ShapeDtypeStruct((m, LANES), F32),
            jax.ShapeDtypeStruct((SUBLANES, LANES), F32),
        ],
        scratch_shapes=[
            pltpu.VMEM((SUBLANES, d), F32),
            pltpu.VMEM((SUBLANES, LANES), F32),
        ],
        compiler_params=_params(("arbitrary",), 56),
        name="post",
    )(x2, og, proj, proj, proj, proj, proj, mod, n2g, conv_w, wpa, wpb, wout, w_route_hi, w_route_lo,
      b_route)


def _plan_kernel(route_ref, cnt_ref, d_ref):
    bm = EXPERT_BLOCK
    sizes = cnt_ref[...]
    padded = jnp.floor((sizes + (bm - 1.0)) * (1.0 / bm)) * bm
    lane_i = lax.broadcasted_iota(jnp.int32, sizes.shape, 1)
    incl = padded
    s = 1
    while s < LANES:
        incl = incl + jnp.where(lane_i >= s, pltpu.roll(incl, s, 1), 0.0)
        s *= 2
    start = (incl - padded)[0:1, :]
    r = route_ref[...]
    lane = lax.broadcasted_iota(jnp.int32, r.shape, 1).astype(F32)
    d1 = jnp.sum(jnp.where(lane == r[:, 0:1], start, 0.0), axis=-1, keepdims=True) + r[:, 4:5]
    d2 = jnp.sum(jnp.where(lane == r[:, 1:2], start, 0.0), axis=-1, keepdims=True) + r[:, 5:6]
    out = jnp.where(lane == 0.0, d1, jnp.where(lane == 1.0, d2, 0.0))
    d_ref[...] = out.T[0:SUBLANES, :].astype(jnp.int32)


def _plan(route, counts, tm):
    m = route.shape[0]
    return pl.pallas_call(
        _plan_kernel,
        grid=(m // tm,),
        in_specs=[
            pl.BlockSpec((tm, LANES), lambda i: (i, 0)),
            pl.BlockSpec((SUBLANES, LANES), lambda i: (0, 0)),
        ],
        out_specs=pl.BlockSpec((SUBLANES, tm), lambda i: (0, i)),
        out_shape=jax.ShapeDtypeStruct((SUBLANES, m), jnp.int32),
        compiler_params=_params(("arbitrary",), 16),
        name="plan",
    )(route, counts)


def _row_copy(src_ref, src_row, dst_ref, dst_row, sem):
    return pltpu.make_async_copy(_row_tile(src_ref, src_row), _row_tile(dst_ref, dst_row), sem)


def _rows_copy(src_ref, dst_ref, dst_row, n, sem):
    return pltpu.make_async_copy(src_ref, dst_ref.at[pl.ds(dst_row * SUBLANES, n * SUBLANES), :], sem)


def _dispatch_kernel(d1_ref, d2_ref, tail_ref, h2_ref, xs_ref, zero_ref, sem):
    tm = h2_ref.shape[0] // SUBLANES
    bm = zero_ref.shape[0] // SUBLANES
    t0 = pl.program_id(0) * tm

    @pl.when(pl.program_id(0) == 0)
    def _():
        zero_ref[...] = jnp.zeros(zero_ref.shape, F32)
        n_blocks = xs_ref.shape[0] // (bm * SUBLANES)

        def zero_block(row):
            return _rows_copy(zero_ref, xs_ref, pl.multiple_of(row, bm), bm, sem.at[0])

        for e in range(N_EXPERTS):
            @pl.when(tail_ref[e] >= 0)
            def _():
                zero_block(tail_ref[e]).start()
        lax.fori_loop(tail_ref[N_EXPERTS], n_blocks, lambda j, c: (zero_block(j * bm).start(), c)[1], 0)
        for e in range(N_EXPERTS):
            @pl.when(tail_ref[e] >= 0)
            def _():
                zero_block(tail_ref[e]).wait()
        lax.fori_loop(tail_ref[N_EXPERTS], n_blocks, lambda j, c: (zero_block(j * bm).wait(), c)[1], 0)

    def start(r, carry):
        _row_copy(h2_ref, r, xs_ref, d1_ref[t0 + r], sem.at[0]).start(priority=0)
        _row_copy(h2_ref, r, xs_ref, d2_ref[t0 + r], sem.at[1]).start(priority=1)
        return carry

    lax.fori_loop(0, tm, start, 0, unroll=DMA_UNROLL)
    _rows_copy(h2_ref, xs_ref, 0, tm, sem.at[0]).wait()
    _rows_copy(h2_ref, xs_ref, 0, tm, sem.at[1]).wait()


def _dispatch(d1, d2, tail_start, h2, cap, tm):
    m = h2.shape[0] // SUBLANES
    return pl.pallas_call(
        _dispatch_kernel,
        grid_spec=pltpu.PrefetchScalarGridSpec(
            num_scalar_prefetch=3,
            grid=(m // tm,),
            in_specs=[pl.BlockSpec((tm * SUBLANES, LANES), lambda i, d1, d2, tl: (i, 0))],
            out_specs=pl.BlockSpec(memory_space=pl.ANY),
            scratch_shapes=[pltpu.VMEM((EXPERT_BLOCK * SUBLANES, LANES), F32),
                            pltpu.SemaphoreType.DMA((2,))],
        ),
        out_shape=jax.ShapeDtypeStruct((cap * SUBLANES, LANES), F32),
        compiler_params=_params(("arbitrary",), 24),
        name="dispatch",
    )(d1, d2, tail_start, h2)


def _expert_kernel(be_ref, act_ref, x_ref, w1_ref, w3_ref, w2_ref, y_ref, w1b_ref, w3b_ref, w2b_ref):
    b = pl.program_id(0)

    @pl.when(act_ref[b] > 0)
    def _():
        @pl.when((b == 0) | (be_ref[b] != be_ref[jnp.maximum(b - 1, 0)]))
        def _():
            w1b_ref[...] = w1_ref[...].astype(BF16)
            w3b_ref[...] = w3_ref[...].astype(BF16)
            w2b_ref[...] = w2_ref[...].astype(BF16)

        xb = _read_rows(x_ref, x_ref.shape[0] // SUBLANES).astype(BF16)
        hid = _silu(_dot(xb, w1b_ref[...])) * _dot(xb, w3b_ref[...])
        _write_rows(y_ref, _dot(hid.astype(BF16), w2b_ref[...]))

    @pl.when(act_ref[b] == 0)
    def _():
        y_ref[...] = jnp.zeros(y_ref.shape, F32)


def _experts(block_expert, block_active, xs, w1, w3, w2):
    cap = xs.shape[0] // SUBLANES
    d, de = w1.shape[1], w1.shape[2]
    bm = EXPERT_BLOCK

    def x_block(b, be, act):
        return (jnp.minimum(b, jnp.maximum(act[cap // bm], 1) - 1), 0)

    return pl.pallas_call(
        _expert_kernel,
        grid_spec=pltpu.PrefetchScalarGridSpec(
            num_scalar_prefetch=2,
            grid=(cap // bm,),
            in_specs=[
                pl.BlockSpec((bm * SUBLANES, LANES), x_block),
                pl.BlockSpec((None, d, de), lambda b, be, act: (be[b], 0, 0)),
                pl.BlockSpec((None, d, de), lambda b, be, act: (be[b], 0, 0)),
                pl.BlockSpec((None, de, d), lambda b, be, act: (be[b], 0, 0)),
            ],
            out_specs=pl.BlockSpec((bm * SUBLANES, LANES), lambda b, be, act: (b, 0)),
            scratch_shapes=[pltpu.VMEM((d, de), BF16), pltpu.VMEM((d, de), BF16),
                            pltpu.VMEM((de, d), BF16)],
        ),
        out_shape=jax.ShapeDtypeStruct((cap * SUBLANES, LANES), F32),
        compiler_params=_params(("arbitrary",), 40),
        name="experts",
    )(block_expert, block_active, xs, w1, w3, w2)


def _final_kernel(d1_ref, d2_ref, x1_ref, route_ref, gt_ref, nfg_ref, ys_ref, o_ref, buf_ref, sem):
    tm = x1_ref.shape[0]
    i = pl.program_id(0)

    def gather(tile, s):
        t0 = tile * tm

        def start(r, carry):
            _row_copy(ys_ref, d1_ref[t0 + r], buf_ref.at[s, 0], r, sem.at[s, 0]).start(priority=0)
            _row_copy(ys_ref, d2_ref[t0 + r], buf_ref.at[s, 1], r, sem.at[s, 1]).start(priority=1)
            return carry

        lax.fori_loop(0, tm, start, 0, unroll=DMA_UNROLL)

    def combine(cur, nxt):
        @pl.when(i + 1 < pl.num_programs(0))
        def _():
            gather(i + 1, nxt)

        for k in range(2):
            pltpu.make_async_copy(ys_ref.at[pl.ds(0, tm * SUBLANES), :], buf_ref.at[cur, k],
                                  sem.at[cur, k]).wait()
        route = route_ref[...]
        moe = (_read_rows(buf_ref.at[cur, 0], tm) * route[:, 2:3]
               + _read_rows(buf_ref.at[cur, 1], tm) * route[:, 3:4])
        x2 = x1_ref[...] + gt_ref[...] * moe
        y = x2 * lax.rsqrt(jnp.mean(x2 * x2, axis=-1, keepdims=True) + EPS)
        o_ref[...] = y * nfg_ref[...]

    @pl.when(i == 0)
    def _():
        gather(0, 0)

    @pl.when(i % 2 == 0)
    def _():
        combine(0, 1)

    @pl.when(i % 2 == 1)
    def _():
        combine(1, 0)


def _final(d1, d2, x1, route, gt2, nfg, ys, seq, tm):
    m, d = x1.shape
    per_batch = seq // tm
    return pl.pallas_call(
        _final_kernel,
        grid_spec=pltpu.PrefetchScalarGridSpec(
            num_scalar_prefetch=2,
            grid=(m // tm,),
            in_specs=[
                pl.BlockSpec((tm, d), lambda i, d1, d2: (i, 0)),
                pl.BlockSpec((tm, LANES), lambda i, d1, d2: (i, 0)),
                pl.BlockSpec((None, 1, d), lambda i, d1, d2: (i // per_batch, 0, 0)),
                pl.BlockSpec((1, d), lambda i, d1, d2: (0, 0)),
                pl.BlockSpec(memory_space=pl.ANY),
            ],
            out_specs=pl.BlockSpec((tm, d), lambda i, d1, d2: (i, 0)),
            scratch_shapes=[pltpu.VMEM((2, 2, tm * SUBLANES, LANES), F32),
                            pltpu.SemaphoreType.DMA((2, 2))],
        ),
        out_shape=jax.ShapeDtypeStruct((m, d), F32),
        compiler_params=_params(("arbitrary",), 32),
        name="final",
    )(d1, d2, x1, route, gt2, nfg, ys)


def _tile(n, pref):
    t = min(n, pref)
    assert n % t == 0
    return t


def kernel(x, c, w_ada, b_ada, norm1_g, w_in, conv_qkv_w, a_log, dt_bias, onorm_g, w_proj_a,
           conv_sc_w, w_proj_b, w_out, norm2_g, w_group, b_group, w_expert, b_expert, w1, w3, w2,
           normf_g):
    batch, seq, d = x.shape
    depth = w_ada.shape[0]
    m = batch * seq
    kd = N_HEADS * HEAD_D
    assert d == kd and seq % CHUNK == 0 and batch <= SUBLANES
    c_pad = jnp.zeros((SUBLANES, d), F32).at[:batch].set(c)
    x2 = x.reshape(m, d)

    for l in range(depth):
        mod = _ada(c_pad, w_ada[l], b_ada[l][None, :])[:batch]
        sh1, sc1, gt1, sh2, sc2, gt2 = [mod[:, None, j * d:(j + 1) * d] for j in range(6)]

        w = w_in[l]
        o_ba = 3 * kd + kd
        w_a = w[:, :o_ba].astype(BF16)
        w_b = w[:, o_ba + 2 * N_HEADS:].astype(BF16)
        w_ba = jnp.zeros((d, LANES), F32).at[:, :2 * N_HEADS].set(w[:, o_ba:o_ba + 2 * N_HEADS]).astype(BF16)
        proj, ba = _inproj(x2, norm1_g[l][None, :], sc1, sh1, w_a, w_b, w_ba, seq, _tile(seq, INPROJ_ROWS))

        head_params = jnp.zeros((SUBLANES, LANES), F32)
        head_params = head_params.at[0, N_HEADS:2 * N_HEADS].set(a_log[l])
        head_params = head_params.at[1, N_HEADS:2 * N_HEADS].set(dt_bias[l])
        head_params = head_params.at[2, :].set(onorm_g[l])
        conv_taps = jnp.broadcast_to(conv_qkv_w[l][:, None, :], (QKV_CONV, SUBLANES, 3 * kd))
        og = _gdn(proj, ba, conv_taps, head_params, batch, seq, _tile(seq, GDN_ROWS))

        mod_post = jnp.zeros((batch, SUBLANES, d), F32)
        mod_post = mod_post.at[:, 0:1].set(gt1).at[:, 1:2].set(sc2).at[:, 2:3].set(sh2)
        w_route = jnp.zeros((d, LANES), F32)
        w_route = w_route.at[:, :N_GROUPS].set(w_group[l]).at[:, N_GROUPS:N_GROUPS + N_EXPERTS].set(w_expert[l])
        w_route_hi = w_route.astype(BF16)
        w_route_lo = (w_route - w_route_hi.astype(F32)).astype(BF16)
        b_route = jnp.zeros((1, LANES), F32)
        b_route = b_route.at[0, :N_GROUPS].set(b_group[l]).at[0, N_GROUPS:N_GROUPS + N_EXPERTS].set(b_expert[l])
        x1, h2, route, counts = _post(
            x2, og, proj, mod_post, norm2_g[l][None, :],
            jnp.broadcast_to(conv_sc_w[l][:, None, :], (SC_CONV, SUBLANES, d)),
            w_proj_a[l].astype(BF16), w_proj_b[l].astype(BF16), w_out[l].astype(BF16),
            w_route_hi, w_route_lo, b_route, seq, _tile(seq, POST_ROWS))

        bm = EXPERT_BLOCK
        n_blocks = (2 * m) // bm + N_EXPERTS
        sizes = counts[0, :N_EXPERTS].astype(jnp.int32)
        padded = ((sizes + bm - 1) // bm) * bm
        pad_end = jnp.cumsum(padded)
        tail_start = jnp.concatenate([jnp.where(padded > 0, pad_end - bm, -1),
                                      pad_end[-1:] // bm]).astype(jnp.int32)
        block_row = jnp.arange(n_blocks, dtype=jnp.int32) * bm
        block_expert = jnp.minimum(jnp.sum(block_row[:, None] >= pad_end[None, :], axis=1),
                                   N_EXPERTS - 1).astype(jnp.int32)
        block_active = jnp.concatenate([(block_row < pad_end[-1]).astype(jnp.int32),
                                        (pad_end[-1:] // bm).astype(jnp.int32)])
        dest = _plan(route, counts, _tile(seq, INPROJ_ROWS))
        d1, d2 = dest[0], dest[1]

        xs = _dispatch(d1, d2, tail_start, h2, n_blocks * bm, _tile(seq, MOVE_ROWS))
        ys = _experts(block_expert, block_active, xs, w1[l], w3[l], w2[l])
        nfg = normf_g[None, :] if l == depth - 1 else jnp.ones((1, d), F32)
        x2 = _final(d1, d2, x1, route, gt2, nfg, ys, seq, _tile(seq, MOVE_ROWS))
        assert depth == 1
    return x2.reshape(batch, seq, d)
```

```python
import functools

import jax
import jax.numpy as jnp
from jax import lax
from jax.experimental import pallas as pl
from jax.experimental.pallas import tpu as pltpu

F32 = jnp.float32
BF16 = jnp.bfloat16
HIGHEST = lax.Precision.HIGHEST

N_HEADS = 8
HEAD_D = 128
CHUNK = 64
QKV_CONV = 4
SC_CONV = 3
N_GROUPS = 4
EXPERTS_PER_GROUP = 8
N_EXPERTS = N_GROUPS * EXPERTS_PER_GROUP
EPS = 1e-6

LANES = 128
SUBLANES = 8
BF16_ROWS = 16
N_PROJ_COLS = 9
V_COL = 2
EXPERT_BLOCK = 512
INPROJ_ROWS = 2048
GDN_ROWS = 256
PRE_CHUNKS = 2
POST_ROWS = 512
MOVE_ROWS = 512
DMA_UNROLL = 8
MIB = 1024 * 1024


def _sigmoid(x):
    return 0.5 + 0.5 * jnp.tanh(0.5 * x)


def _silu(x):
    half = 0.5 * x
    return half + half * jnp.tanh(half)


def _softplus(x):
    return jnp.maximum(x, 0.0) + jnp.log(1.0 + jnp.exp(-jnp.abs(x)))


def _dot(a, b):
    return jnp.dot(a, b, preferred_element_type=F32)


def _dot_nt(a, b):
    return lax.dot_general(a, b, (((1,), (1,)), ((), ())), preferred_element_type=F32)


def _dot_hi(a, b):
    return jnp.dot(a, b, preferred_element_type=F32, precision=HIGHEST)


def _dot_split(a, b_hi, b_lo):
    a_hi = a.astype(BF16)
    a_lo = (a - a_hi.astype(F32)).astype(BF16)
    return _dot(a_hi, b_hi) + (_dot(a_lo, b_hi) + _dot(a_hi, b_lo))


def _read_rows(ref, rows):
    return jnp.concatenate([ref[pl.ds(j, rows, stride=SUBLANES), :] for j in range(SUBLANES)], axis=1)


def _write_rows(ref, value):
    rows = value.shape[0]
    for j in range(SUBLANES):
        ref[pl.ds(j, rows, stride=SUBLANES), :] = value[:, j * LANES:(j + 1) * LANES]


def _row_tile(ref, row):
    if not isinstance(row, int):
        row = pl.multiple_of(row * SUBLANES, SUBLANES)
    else:
        row = row * SUBLANES
    return ref.at[pl.ds(row, SUBLANES), :]


def _params(semantics, vmem_mib):
    return pltpu.CompilerParams(dimension_semantics=semantics, vmem_limit_bytes=vmem_mib * MIB)


def _ada_kernel(c_ref, w_ref, b_ref, o_ref):
    o_ref[...] = _dot_hi(_silu(c_ref[...]), w_ref[...]) + b_ref[...]


def _ada(c_pad, w_ada, b_ada):
    d = c_pad.shape[1]
    n = w_ada.shape[1]
    return pl.pallas_call(
        _ada_kernel,
        grid=(n // d,),
        in_specs=[
            pl.BlockSpec((SUBLANES, d), lambda j: (0, 0)),
            pl.BlockSpec((d, d), lambda j: (0, j)),
            pl.BlockSpec((1, d), lambda j: (0, j)),
        ],
        out_specs=pl.BlockSpec((SUBLANES, d), lambda j: (0, j)),
        out_shape=jax.ShapeDtypeStruct((SUBLANES, n), F32),
        compiler_params=_params(("arbitrary",), 24),
        name="ada",
    )(c_pad, w_ada, b_ada)


def _conv4_silu_rows(x, halo, cw):
    def taps(a):
        tiled = (a.shape[0] // SUBLANES, SUBLANES, a.shape[1])
        a1 = pltpu.roll(a, 1, 0)

        def pair(j):
            return (a.reshape(tiled) * cw[j][None] + a1.reshape(tiled) * cw[j - 1][None]).reshape(a.shape)

        return pair(3) + pltpu.roll(pair(1), 2, 0)

    top = taps(jnp.concatenate([halo, x[:SUBLANES]], axis=0))[SUBLANES:]
    return _silu(jnp.concatenate([top, taps(x)[SUBLANES:]], axis=0))


def _inproj_kernel(x_ref, g_ref, sc_ref, sh_ref, wa_ref, wb_ref, wba_ref, cw_ref, o_ref, ba_ref,
                   h_ref, halo_ref, *, na, per_batch):
    tm = x_ref.shape[0]
    i = pl.program_id(0)
    j = pl.program_id(1)

    @pl.when(j == 0)
    def _():
        x = x_ref[...]
        y = x * lax.rsqrt(jnp.mean(x * x, axis=-1, keepdims=True) + EPS)
        h = (y * g_ref[...]) * (1.0 + sc_ref[...]) + sh_ref[...]
        hb = h.astype(BF16)
        h_ref[...] = hb
        ba_ref[...] = _dot(hb, wba_ref[...])

    @pl.when((j == 0) & (i % per_batch == 0))
    def _():
        halo_ref[...] = jnp.zeros(halo_ref.shape, F32)

    for col in range(V_COL + 1):
        @pl.when(j == col)
        def _():
            acc = _dot(h_ref[...], wa_ref[...])
            o_ref[...] = _conv4_silu_rows(acc, halo_ref[col], cw_ref[...]).astype(BF16)
            halo_ref[col] = acc[tm - SUBLANES:, :]

    @pl.when((j > V_COL) & (j < na))
    def _():
        o_ref[...] = _dot(h_ref[...], wa_ref[...]).astype(BF16)

    @pl.when(j >= na)
    def _():
        o_ref[...] = _dot(h_ref[...], wb_ref[...]).astype(BF16)


def _inproj(x2, norm_g, sc, sh, w_a, w_b, w_ba, conv_taps, seq, tm):
    m, d = x2.shape
    tn = d
    na, nb = w_a.shape[1] // tn, w_b.shape[1] // tn
    n = (na + nb) * tn
    per_batch = seq // tm
    return pl.pallas_call(
        functools.partial(_inproj_kernel, na=na, per_batch=per_batch),
        grid=(m // tm, n // tn),
        in_specs=[
            pl.BlockSpec((tm, d), lambda i, j: (i, 0)),
            pl.BlockSpec((1, d), lambda i, j: (0, 0)),
            pl.BlockSpec((None, 1, d), lambda i, j: (i // per_batch, 0, 0)),
            pl.BlockSpec((None, 1, d), lambda i, j: (i // per_batch, 0, 0)),
            pl.BlockSpec((d, tn), lambda i, j: (0, jnp.minimum(j, na - 1))),
            pl.BlockSpec((d, tn), lambda i, j: (0, jnp.maximum(j - na, 0))),
            pl.BlockSpec((d, LANES), lambda i, j: (0, 0)),
            pl.BlockSpec((QKV_CONV, SUBLANES, tn), lambda i, j: (0, 0, jnp.minimum(j, V_COL))),
        ],
        out_specs=[
            pl.BlockSpec((tm, tn), lambda i, j: (i, j)),
            pl.BlockSpec((tm, LANES), lambda i, j: (i, 0)),
        ],
        out_shape=[
            jax.ShapeDtypeStruct((m, n), BF16),
            jax.ShapeDtypeStruct((m, LANES), F32),
        ],
        scratch_shapes=[pltpu.VMEM((tm, d), BF16), pltpu.VMEM((V_COL + 1, SUBLANES, d), F32)],
        compiler_params=_params(("arbitrary", "arbitrary"), 56),
        name="inproj",
    )(x2, norm_g, sc, sh, w_a, w_b, w_ba, conv_taps)


def _bmm(a, b):
    return jnp.einsum("hmk,hkn->hmn", a.astype(BF16), b.astype(BF16), preferred_element_type=F32)


def _bmm_nt(a, b):
    return jnp.einsum("hmk,hnk->hmn", a.astype(BF16), b.astype(BF16), preferred_element_type=F32)


def _unit_lower_inverse(a):
    row = lax.broadcasted_iota(jnp.int32, a.shape[1:], 0)
    col = lax.broadcasted_iota(jnp.int32, a.shape[1:], 1)
    apart = row ^ col
    eye = jnp.where(row == col, 1.0, 0.0).astype(F32)
    t = jnp.where(apart < 2, eye - a, 0.0)
    s = 2
    while s < CHUNK:
        coupling = jnp.where((apart >= s) & (apart < 2 * s), a, 0.0)
        t = t - _bmm(t, _bmm(coupling, t))
        s *= 2
    return t


def _lane_sums(x):
    h, rows, width = x.shape
    ones = jnp.ones((width, width), BF16)
    return _dot(x.reshape(h * rows, width).astype(BF16), ones).reshape(h, rows, width)


def _gdn_kernel(q_ref, k_ref, v_ref, z_ref, ba_ref, hp_ref, o_ref,
                s_ref, wq_ref, u_ref, ik_ref, dec_ref):
    nb, tb = q_ref.shape[0], q_ref.shape[1]
    nbh = nb * N_HEADS

    @pl.when(pl.program_id(0) == 0)
    def _():
        s_ref[...] = jnp.zeros(s_ref.shape, F32)

    row = lax.broadcasted_iota(jnp.int32, (CHUNK, CHUNK), 0)
    col = lax.broadcasted_iota(jnp.int32, (CHUNK, CHUNK), 1)
    causal = row >= col
    strict = row > col
    tril = jnp.where(causal, 1.0, 0.0).astype(F32)
    a_log = hp_ref[0:1, :]
    dt_bias = hp_ref[1:2, :]
    onorm_g = hp_ref[2:3, :]
    zeros_half = jnp.zeros((CHUNK, HEAD_D), F32)

    def precompute(cp, carry):
        qs, ks, vs, betas, gcs, grs, gls = [], [], [], [], [], [], []
        for sub in range(PRE_CHUNKS):
            base = pl.multiple_of((cp * PRE_CHUNKS + sub) * CHUNK, CHUNK)
            for b in range(nb):
                ba = ba_ref[b, pl.ds(base, CHUNK), :]
                beta_all = _sigmoid(ba)
                g_all = -jnp.exp(a_log) * _softplus(ba + dt_bias)
                gcum = _dot_hi(tril, g_all)
                gcum_t = jnp.concatenate([gcum, gcum], axis=0).T
                for h in range(N_HEADS):
                    lo, hi = h * HEAD_D, (h + 1) * HEAD_D
                    qs.append(q_ref[b, pl.ds(base, CHUNK), lo:hi].astype(F32))
                    ks.append(k_ref[b, pl.ds(base, CHUNK), lo:hi].astype(F32))
                    vs.append(v_ref[b, pl.ds(base, CHUNK), lo:hi].astype(F32))
                    betas.append(beta_all[:, h:h + 1])
                    gcs.append(gcum[:, N_HEADS + h:N_HEADS + h + 1])
                    grs.append(gcum_t[N_HEADS + h:N_HEADS + h + 1, 0:CHUNK])
                    gls.append(gcum[CHUNK - 1:CHUNK, N_HEADS + h:N_HEADS + h + 1])
        q, k, v = jnp.stack(qs), jnp.stack(ks), jnp.stack(vs)
        beta, gc, gr, gl = jnp.stack(betas), jnp.stack(gcs), jnp.stack(grs), jnp.stack(gls)
        qn = q * (lax.rsqrt(_lane_sums(q * q) + EPS) * (HEAD_D ** -0.5))
        kn = k * lax.rsqrt(_lane_sums(k * k) + EPS)
        decay = jnp.where(causal, jnp.exp(jnp.where(causal, gc - gr, 0.0)), 0.0)
        kb = kn * beta
        e_gc = jnp.exp(gc)
        kq = _bmm_nt(jnp.concatenate([kb, qn], axis=1), kn)
        a = jnp.where(strict, kq[:, :CHUNK] * decay, 0.0)
        intra = kq[:, CHUNK:] * decay
        uw = _bmm(_unit_lower_inverse(a), jnp.concatenate([v * beta, kb * e_gc], axis=2))
        wq = jnp.concatenate([uw[:, :, HEAD_D:], qn * e_gc], axis=1).astype(BF16)
        k_dec = kn * jnp.exp(gl - gc)
        k_dec_t = jnp.stack([jnp.concatenate([k_dec[i], zeros_half], axis=0).T[:, :CHUNK]
                             for i in range(PRE_CHUNKS * nbh)])
        ik = jnp.concatenate([intra, k_dec_t], axis=1).astype(BF16)
        dec = jnp.broadcast_to(jnp.exp(gl), (PRE_CHUNKS * nbh, 1, HEAD_D))
        for sub in range(PRE_CHUNKS):
            c = cp * PRE_CHUNKS + sub
            rows = slice(sub * nbh, (sub + 1) * nbh)
            u_ref[c] = uw[rows, :, :HEAD_D]
            wq_ref[c] = wq[rows]
            ik_ref[c] = ik[rows]
            dec_ref[c] = dec[rows]
        return carry

    def recur(c, carry):
        base = pl.multiple_of(c * CHUNK, CHUNK)
        state = s_ref[...]
        ws = _bmm(wq_ref[c], state)
        v_new = u_ref[c] - ws[:, :CHUNK]
        r = _bmm(ik_ref[c], v_new)
        o = ws[:, CHUNK:] + r[:, :CHUNK]
        s_ref[...] = state * dec_ref[c] + r[:, CHUNK:]
        on = o * lax.rsqrt(jnp.mean(o * o, axis=-1, keepdims=True) + EPS) * onorm_g
        for b in range(nb):
            for h in range(N_HEADS):
                lo, hi = h * HEAD_D, (h + 1) * HEAD_D
                z = z_ref[b, pl.ds(base, CHUNK), lo:hi].astype(F32)
                o_ref[b, pl.ds(base, CHUNK), lo:hi] = (on[b * N_HEADS + h] * _silu(z)).astype(BF16)
        return carry

    lax.fori_loop(0, tb // (PRE_CHUNKS * CHUNK), precompute, 0)
    lax.fori_loop(0, tb // CHUNK, recur, 0)


def _gdn(proj, ba, head_params, batch, seq, tb):
    kd = N_HEADS * HEAD_D
    nc = tb // CHUNK
    nbh = batch * N_HEADS
    proj3 = proj.reshape(batch, seq, proj.shape[1])
    ba3 = ba.reshape(batch, seq, LANES)

    def col(j):
        return pl.BlockSpec((batch, tb, kd), lambda t: (0, t, j))

    out = pl.pallas_call(
        _gdn_kernel,
        grid=(seq // tb,),
        in_specs=[
            col(0), col(1), col(2), col(3),
            pl.BlockSpec((batch, tb, LANES), lambda t: (0, t, 0)),
            pl.BlockSpec((SUBLANES, LANES), lambda t: (0, 0)),
        ],
        out_specs=pl.BlockSpec((batch, tb, kd), lambda t: (0, t, 0)),
        out_shape=jax.ShapeDtypeStruct((batch, seq, kd), BF16),
        scratch_shapes=[
            pltpu.VMEM((nbh, HEAD_D, HEAD_D), F32),
            pltpu.VMEM((nc, nbh, 2 * CHUNK, HEAD_D), BF16),
            pltpu.VMEM((nc, nbh, CHUNK, HEAD_D), F32),
            pltpu.VMEM((nc, nbh, CHUNK + HEAD_D, CHUNK), BF16),
            pltpu.VMEM((nc, nbh, 1, HEAD_D), F32),
        ],
        compiler_params=_params(("arbitrary",), 48),
        name="gdn",
    )(proj3, proj3, proj3, proj3, ba3, head_params)
    return out.reshape(batch * seq, kd)


def _post_kernel(x_ref, og_ref, sb_ref, sc_ref, sx_ref, ga_ref, gb_ref, mod_ref, n2g_ref, cw_ref,
                 wpa_ref, wpb_ref, wout_ref, wrh_ref, wrl_ref, br_ref,
                 x1_ref, h2_ref, route_ref, cnt_ref, win_ref, run_ref, *, per_batch):
    tm = x_ref.shape[0]
    i = pl.program_id(0)

    @pl.when(i == 0)
    def _():
        run_ref[...] = jnp.zeros(run_ref.shape, F32)

    @pl.when(i % per_batch == 0)
    def _():
        win_ref[...] = jnp.zeros(win_ref.shape, F32)

    assert SC_CONV == 3
    prod = sc_ref[...].astype(F32) * sx_ref[...].astype(F32)
    win = jnp.concatenate([win_ref[...], prod], axis=0)
    win_ref[...] = prod[tm - SUBLANES:, :]
    tiled = (win.shape[0] // SUBLANES, SUBLANES, win.shape[1])
    win1 = pltpu.roll(win, 1, 0)
    conv = (win.reshape(tiled) * cw_ref[2][None] + win1.reshape(tiled) * cw_ref[1][None]
            + pltpu.roll(win1, 1, 0).reshape(tiled) * cw_ref[0][None]).reshape(win.shape)[SUBLANES:, :]
    y_b = _dot((sb_ref[...].astype(F32) * conv).astype(BF16), wpb_ref[...])
    y_a = _dot(og_ref[...], wpa_ref[...])
    merged = _sigmoid(ga_ref[...].astype(F32)) * y_a + _sigmoid(gb_ref[...].astype(F32)) * y_b
    mix = _dot(merged.astype(BF16), wout_ref[...])
    x1 = x_ref[...] + mod_ref[0:1, :] * mix
    x1_ref[...] = x1

    y = x1 * lax.rsqrt(jnp.mean(x1 * x1, axis=-1, keepdims=True) + EPS)
    h2 = (y * n2g_ref[...]) * (1.0 + mod_ref[1:2, :]) + mod_ref[2:3, :]
    _write_rows(h2_ref, h2)

    lg = _dot_split(h2, wrh_ref[...], wrl_ref[...]) + br_ref[...]
    lane = lax.broadcasted_iota(jnp.int32, lg.shape, 1).astype(F32)
    neg = jnp.float32(-jnp.inf)
    big = jnp.float32(2 * LANES)

    def first_max(mask):
        vmax = jnp.max(jnp.where(mask, lg, neg), axis=-1, keepdims=True)
        idx = jnp.min(jnp.where(mask & (lg == vmax), lane, big), axis=-1, keepdims=True)
        return vmax, idx

    gmask = lane < N_GROUPS
    g_max, g_sel = first_max(gmask)
    p_group = 1.0 / jnp.sum(jnp.where(gmask, jnp.exp(lg - g_max), 0.0), axis=-1, keepdims=True)
    e_lo = N_GROUPS + EXPERTS_PER_GROUP * g_sel
    emask = (lane >= e_lo) & (lane < e_lo + EXPERTS_PER_GROUP)
    v1, i1 = first_max(emask)
    v2, i2 = first_max(emask & (lane != i1))
    ex = jnp.exp(v2 - v1)
    w1 = p_group * (1.0 / (1.0 + ex))
    w2 = p_group * (ex / (1.0 + ex))
    e1 = i1 - N_GROUPS
    e2 = i2 - N_GROUPS

    onehot = jnp.where((lane == e1) | (lane == e2), 1.0, 0.0).astype(F32)
    row = lax.broadcasted_iota(jnp.int32, (tm, tm), 0)
    col = lax.broadcasted_iota(jnp.int32, (tm, tm), 1)
    before = jnp.where(row > col, 1.0, 0.0).astype(BF16)
    seen = _dot(before, onehot.astype(BF16)) + run_ref[0:1, :]
    r1 = jnp.sum(jnp.where(lane == e1, seen, 0.0), axis=-1, keepdims=True)
    r2 = jnp.sum(jnp.where(lane == e2, seen, 0.0), axis=-1, keepdims=True)
    run_ref[0:1, :] = run_ref[0:1, :] + jnp.sum(onehot, axis=0, keepdims=True)
    cnt_ref[...] = jnp.broadcast_to(run_ref[0:1, :], cnt_ref.shape)

    out = jnp.where(lane == 0, e1, 0.0)
    out = jnp.where(lane == 1, e2, out)
    out = jnp.where(lane == 2, w1, out)
    out = jnp.where(lane == 3, w2, out)
    out = jnp.where(lane == 4, r1, out)
    out = jnp.where(lane == 5, r2, out)
    route_ref[...] = out


def _post(x2, og, proj, mod, n2g, conv_w, wpa, wpb, wout, w_route_hi, w_route_lo, b_route, seq, tm):
    m, d = x2.shape
    assert d == SUBLANES * LANES
    per_batch = seq // tm

    def rows(j):
        return pl.BlockSpec((tm, d), lambda i: (i, j))

    def whole(shape):
        return pl.BlockSpec(shape, lambda i: tuple(0 for _ in shape))

    return pl.pallas_call(
        functools.partial(_post_kernel, per_batch=per_batch),
        grid=(m // tm,),
        in_specs=[
            rows(0), rows(0), rows(4), rows(5), rows(6), rows(7), rows(8),
            pl.BlockSpec((None, SUBLANES, d), lambda i: (i // per_batch, 0, 0)),
            whole((1, d)), whole((SC_CONV, SUBLANES, d)),
            whole((d, d)), whole((d, d)), whole((d, d)),
            whole((d, LANES)), whole((d, LANES)), whole((1, LANES)),
        ],
        out_specs=[
            rows(0), pl.BlockSpec((tm * SUBLANES, LANES), lambda i: (i, 0)),
            pl.BlockSpec((tm, LANES), lambda i: (i, 0)),
            pl.BlockSpec((SUBLANES, LANES), lambda i: (0, 0)),
        ],
        out_shape=[
            jax.ShapeDtypeStruct((m, d), F32),
            jax.ShapeDtypeStruct((m * SUBLANES, LANES), F32),
            jax.ShapeDtypeStruct((m, LANES), F32),
            jax.ShapeDtypeStruct((SUBLANES, LANES), F32),
        ],
        scratch_shapes=[
            pltpu.VMEM((SUBLANES, d), F32),
            pltpu.VMEM((SUBLANES, LANES), F32),
        ],
        compiler_params=_params(("arbitrary",), 56),
        name="post",
    )(x2, og, proj, proj, proj, proj, proj, mod, n2g, conv_w, wpa, wpb, wout, w_route_hi, w_route_lo,
      b_route)


def _plan_kernel(route_ref, cnt_ref, d_ref):
    bm = EXPERT_BLOCK
    sizes = cnt_ref[...]
    padded = jnp.floor((sizes + (bm - 1.0)) * (1.0 / bm)) * bm
    lane_i = lax.broadcasted_iota(jnp.int32, sizes.shape, 1)
    incl = padded
    s = 1
    while s < LANES:
        incl = incl + jnp.where(lane_i >= s, pltpu.roll(incl, s, 1), 0.0)
        s *= 2
    start = (incl - padded)[0:1, :]
    r = route_ref[...]
    lane = lax.broadcasted_iota(jnp.int32, r.shape, 1).astype(F32)
    d1 = jnp.sum(jnp.where(lane == r[:, 0:1], start, 0.0), axis=-1, keepdims=True) + r[:, 4:5]
    d2 = jnp.sum(jnp.where(lane == r[:, 1:2], start, 0.0), axis=-1, keepdims=True) + r[:, 5:6]
    out = jnp.where(lane == 0.0, d1, jnp.where(lane == 1.0, d2, 0.0))
    d_ref[...] = out.T[0:SUBLANES, :].astype(jnp.int32)


def _plan(route, counts, tm):
    m = route.shape[0]
    return pl.pallas_call(
        _plan_kernel,
        grid=(m // tm,),
        in_specs=[
            pl.BlockSpec((tm, LANES), lambda i: (i, 0)),
            pl.BlockSpec((SUBLANES, LANES), lambda i: (0, 0)),
        ],
        out_specs=pl.BlockSpec((SUBLANES, tm), lambda i: (0, i)),
        out_shape=jax.ShapeDtypeStruct((SUBLANES, m), jnp.int32),
        compiler_params=_params(("arbitrary",), 16),
        name="plan",
    )(route, counts)


def _row_copy(src_ref, src_row, dst_ref, dst_row, sem):
    return pltpu.make_async_copy(_row_tile(src_ref, src_row), _row_tile(dst_ref, dst_row), sem)


def _rows_copy(src_ref, dst_ref, dst_row, n, sem):
    return pltpu.make_async_copy(src_ref, dst_ref.at[pl.ds(dst_row * SUBLANES, n * SUBLANES), :], sem)


def _dispatch_kernel(d1_ref, d2_ref, tail_ref, h2_ref, xs_ref, zero_ref, sem):
    tm = h2_ref.shape[0] // SUBLANES
    bm = zero_ref.shape[0] // SUBLANES
    t0 = pl.program_id(0) * tm

    @pl.when(pl.program_id(0) == 0)
    def _():
        zero_ref[...] = jnp.zeros(zero_ref.shape, F32)
        n_blocks = xs_ref.shape[0] // (bm * SUBLANES)

        def zero_block(row):
            return _rows_copy(zero_ref, xs_ref, pl.multiple_of(row, bm), bm, sem.at[0])

        for e in range(N_EXPERTS):
            @pl.when(tail_ref[e] >= 0)
            def _():
                zero_block(tail_ref[e]).start()
        lax.fori_loop(tail_ref[N_EXPERTS], n_blocks, lambda j, c: (zero_block(j * bm).start(), c)[1], 0)
        for e in range(N_EXPERTS):
            @pl.when(tail_ref[e] >= 0)
            def _():
                zero_block(tail_ref[e]).wait()
        lax.fori_loop(tail_ref[N_EXPERTS], n_blocks, lambda j, c: (zero_block(j * bm).wait(), c)[1], 0)

    def start(r, carry):
        _row_copy(h2_ref, r, xs_ref, d1_ref[t0 + r], sem.at[0]).start(priority=0)
        _row_copy(h2_ref, r, xs_ref, d2_ref[t0 + r], sem.at[1]).start(priority=1)
        return carry

    lax.fori_loop(0, tm, start, 0, unroll=DMA_UNROLL)
    _rows_copy(h2_ref, xs_ref, 0, tm, sem.at[0]).wait()
    _rows_copy(h2_ref, xs_ref, 0, tm, sem.at[1]).wait()


def _dispatch(d1, d2, tail_start, h2, cap, tm):
    m = h2.shape[0] // SUBLANES
    return pl.pallas_call(
        _dispatch_kernel,
        grid_spec=pltpu.PrefetchScalarGridSpec(
            num_scalar_prefetch=3,
            grid=(m // tm,),
            in_specs=[pl.BlockSpec((tm * SUBLANES, LANES), lambda i, d1, d2, tl: (i, 0))],
            out_specs=pl.BlockSpec(memory_space=pl.ANY),
            scratch_shapes=[pltpu.VMEM((EXPERT_BLOCK * SUBLANES, LANES), F32),
                            pltpu.SemaphoreType.DMA((2,))],
        ),
        out_shape=jax.ShapeDtypeStruct((cap * SUBLANES, LANES), F32),
        compiler_params=_params(("arbitrary",), 24),
        name="dispatch",
    )(d1, d2, tail_start, h2)


def _expert_kernel(be_ref, act_ref, x_ref, w1_ref, w3_ref, w2_ref, y_ref, w1b_ref, w3b_ref, w2b_ref):
    b = pl.program_id(0)

    @pl.when(act_ref[b] > 0)
    def _():
        @pl.when((b == 0) | (be_ref[b] != be_ref[jnp.maximum(b - 1, 0)]))
        def _():
            w1b_ref[...] = w1_ref[...].astype(BF16)
            w3b_ref[...] = w3_ref[...].astype(BF16)
            w2b_ref[...] = w2_ref[...].astype(BF16)

        xb = _read_rows(x_ref, x_ref.shape[0] // SUBLANES).astype(BF16)
        hid = _silu(_dot(xb, w1b_ref[...])) * _dot(xb, w3b_ref[...])
        _write_rows(y_ref, _dot(hid.astype(BF16), w2b_ref[...]))

    @pl.when(act_ref[b] == 0)
    def _():
        y_ref[...] = jnp.zeros(y_ref.shape, F32)


def _experts(block_expert, block_active, xs, w1, w3, w2):
    cap = xs.shape[0] // SUBLANES
    d, de = w1.shape[1], w1.shape[2]
    bm = EXPERT_BLOCK

    def x_block(b, be, act):
        return (jnp.minimum(b, jnp.maximum(act[cap // bm], 1) - 1), 0)

    return pl.pallas_call(
        _expert_kernel,
        grid_spec=pltpu.PrefetchScalarGridSpec(
            num_scalar_prefetch=2,
            grid=(cap // bm,),
            in_specs=[
                pl.BlockSpec((bm * SUBLANES, LANES), x_block),
                pl.BlockSpec((None, d, de), lambda b, be, act: (be[b], 0, 0)),
                pl.BlockSpec((None, d, de), lambda b, be, act: (be[b], 0, 0)),
                pl.BlockSpec((None, de, d), lambda b, be, act: (be[b], 0, 0)),
            ],
            out_specs=pl.BlockSpec((bm * SUBLANES, LANES), lambda b, be, act: (b, 0)),
            scratch_shapes=[pltpu.VMEM((d, de), BF16), pltpu.VMEM((d, de), BF16),
                            pltpu.VMEM((de, d), BF16)],
        ),
        out_shape=jax.ShapeDtypeStruct((cap * SUBLANES, LANES), F32),
        compiler_params=_params(("arbitrary",), 40),
        name="experts",
    )(block_expert, block_active, xs, w1, w3, w2)


def _final_kernel(d1_ref, d2_ref, x1_ref, route_ref, gt_ref, nfg_ref, ys_ref, o_ref, buf_ref, sem):
    tm = x1_ref.shape[0]
    i = pl.program_id(0)

    def gather(tile, s):
        t0 = tile * tm

        def start(r, carry):
            _row_copy(ys_ref, d1_ref[t0 + r], buf_ref.at[s, 0], r, sem.at[s, 0]).start(priority=0)
            _row_copy(ys_ref, d2_ref[t0 + r], buf_ref.at[s, 1], r, sem.at[s, 1]).start(priority=1)
            return carry

        lax.fori_loop(0, tm, start, 0, unroll=DMA_UNROLL)

    def combine(cur, nxt):
        @pl.when(i + 1 < pl.num_programs(0))
        def _():
            gather(i + 1, nxt)

        for k in range(2):
            pltpu.make_async_copy(ys_ref.at[pl.ds(0, tm * SUBLANES), :], buf_ref.at[cur, k],
                                  sem.at[cur, k]).wait()
        route = route_ref[...]
        moe = (_read_rows(buf_ref.at[cur, 0], tm) * route[:, 2:3]
               + _read_rows(buf_ref.at[cur, 1], tm) * route[:, 3:4])
        x2 = x1_ref[...] + gt_ref[...] * moe
        y = x2 * lax.rsqrt(jnp.mean(x2 * x2, axis=-1, keepdims=True) + EPS)
        o_ref[...] = y * nfg_ref[...]

    @pl.when(i == 0)
    def _():
        gather(0, 0)

    @pl.when(i % 2 == 0)
    def _():
        combine(0, 1)

    @pl.when(i % 2 == 1)
    def _():
        combine(1, 0)


def _final(d1, d2, x1, route, gt2, nfg, ys, seq, tm):
    m, d = x1.shape
    per_batch = seq // tm
    return pl.pallas_call(
        _final_kernel,
        grid_spec=pltpu.PrefetchScalarGridSpec(
            num_scalar_prefetch=2,
            grid=(m // tm,),
            in_specs=[
                pl.BlockSpec((tm, d), lambda i, d1, d2: (i, 0)),
                pl.BlockSpec((tm, LANES), lambda i, d1, d2: (i, 0)),
                pl.BlockSpec((None, 1, d), lambda i, d1, d2: (i // per_batch, 0, 0)),
                pl.BlockSpec((1, d), lambda i, d1, d2: (0, 0)),
                pl.BlockSpec(memory_space=pl.ANY),
            ],
            out_specs=pl.BlockSpec((tm, d), lambda i, d1, d2: (i, 0)),
            scratch_shapes=[pltpu.VMEM((2, 2, tm * SUBLANES, LANES), F32),
                            pltpu.SemaphoreType.DMA((2, 2))],
        ),
        out_shape=jax.ShapeDtypeStruct((m, d), F32),
        compiler_params=_params(("arbitrary",), 32),
        name="final",
    )(d1, d2, x1, route, gt2, nfg, ys)


def _tile(n, pref):
    t = min(n, pref)
    assert n % t == 0
    return t


def kernel(x, c, w_ada, b_ada, norm1_g, w_in, conv_qkv_w, a_log, dt_bias, onorm_g, w_proj_a,
           conv_sc_w, w_proj_b, w_out, norm2_g, w_group, b_group, w_expert, b_expert, w1, w3, w2,
           normf_g):
    batch, seq, d = x.shape
    depth = w_ada.shape[0]
    m = batch * seq
    kd = N_HEADS * HEAD_D
    assert d == kd and seq % CHUNK == 0 and batch <= SUBLANES
    c_pad = jnp.zeros((SUBLANES, d), F32).at[:batch].set(c)
    x2 = x.reshape(m, d)

    for l in range(depth):
        mod = _ada(c_pad, w_ada[l], b_ada[l][None, :])[:batch]
        sh1, sc1, gt1, sh2, sc2, gt2 = [mod[:, None, j * d:(j + 1) * d] for j in range(6)]

        w = w_in[l]
        o_ba = 3 * kd + kd
        w_a = w[:, :o_ba].astype(BF16)
        w_b = w[:, o_ba + 2 * N_HEADS:].astype(BF16)
        w_ba = jnp.zeros((d, LANES), F32).at[:, :2 * N_HEADS].set(w[:, o_ba:o_ba + 2 * N_HEADS]).astype(BF16)
        conv_taps = jnp.broadcast_to(conv_qkv_w[l][:, None, :], (QKV_CONV, SUBLANES, 3 * kd))
        proj, ba = _inproj(x2, norm1_g[l][None, :], sc1, sh1, w_a, w_b, w_ba, conv_taps, seq,
                           _tile(seq, INPROJ_ROWS))

        head_params = jnp.zeros((SUBLANES, LANES), F32)
        head_params = head_params.at[0, N_HEADS:2 * N_HEADS].set(a_log[l])
        head_params = head_params.at[1, N_HEADS:2 * N_HEADS].set(dt_bias[l])
        head_params = head_params.at[2, :].set(onorm_g[l])
        og = _gdn(proj, ba, head_params, batch, seq, _tile(seq, GDN_ROWS))

        mod_post = jnp.zeros((batch, SUBLANES, d), F32)
        mod_post = mod_post.at[:, 0:1].set(gt1).at[:, 1:2].set(sc2).at[:, 2:3].set(sh2)
        w_route = jnp.zeros((d, LANES), F32)
        w_route = w_route.at[:, :N_GROUPS].set(w_group[l]).at[:, N_GROUPS:N_GROUPS + N_EXPERTS].set(w_expert[l])
        w_route_hi = w_route.astype(BF16)
        w_route_lo = (w_route - w_route_hi.astype(F32)).astype(BF16)
        b_route = jnp.zeros((1, LANES), F32)
        b_route = b_route.at[0, :N_GROUPS].set(b_group[l]).at[0, N_GROUPS:N_GROUPS + N_EXPERTS].set(b_expert[l])
        x1, h2, route, counts = _post(
            x2, og, proj, mod_post, norm2_g[l][None, :],
            jnp.broadcast_to(conv_sc_w[l][:, None, :], (SC_CONV, SUBLANES, d)),
            w_proj_a[l].astype(BF16), w_proj_b[l].astype(BF16), w_out[l].astype(BF16),
            w_route_hi, w_route_lo, b_route, seq, _tile(seq, POST_ROWS))

        bm = EXPERT_BLOCK
        n_blocks = (2 * m) // bm + N_EXPERTS
        sizes = counts[0, :N_EXPERTS].astype(jnp.int32)
        padded = ((sizes + bm - 1) // bm) * bm
        pad_end = jnp.cumsum(padded)
        tail_start = jnp.concatenate([jnp.where(padded > 0, pad_end - bm, -1),
                                      pad_end[-1:] // bm]).astype(jnp.int32)
        block_row = jnp.arange(n_blocks, dtype=jnp.int32) * bm
        block_expert = jnp.minimum(jnp.sum(block_row[:, None] >= pad_end[None, :], axis=1),
                                   N_EXPERTS - 1).astype(jnp.int32)
        block_active = jnp.concatenate([(block_row < pad_end[-1]).astype(jnp.int32),
                                        (pad_end[-1:] // bm).astype(jnp.int32)])
        dest = _plan(route, counts, _tile(seq, INPROJ_ROWS))
        d1, d2 = dest[0], dest[1]

        xs = _dispatch(d1, d2, tail_start, h2, n_blocks * bm, _tile(seq, MOVE_ROWS))
        ys = _experts(block_expert, block_active, xs, w1[l], w3[l], w2[l])
        nfg = normf_g[None, :] if l == depth - 1 else jnp.ones((1, d), F32)
        x2 = _final(d1, d2, x1, route, gt2, nfg, ys, seq, _tile(seq, MOVE_ROWS))
        assert depth == 1
    return x2.reshape(batch, seq, d)
```

```python
import functools

import jax
import jax.numpy as jnp
from jax import lax
from jax.experimental import pallas as pl
from jax.experimental.pallas import tpu as pltpu

F32 = jnp.float32
BF16 = jnp.bfloat16
HIGHEST = lax.Precision.HIGHEST

N_HEADS = 8
HEAD_D = 128
CHUNK = 64
QKV_CONV = 4
SC_CONV = 3
N_GROUPS = 4
EXPERTS_PER_GROUP = 8
N_EXPERTS = N_GROUPS * EXPERTS_PER_GROUP
EPS = 1e-6

LANES = 128
SUBLANES = 8
BF16_ROWS = 16
EXPERT_BLOCK = 512
INPROJ_ROWS = 2048
GDN_ROWS = 256
PRE_CHUNKS = 2
POST_ROWS = 512
MOVE_ROWS = 512
DMA_UNROLL = 8
MIB = 1024 * 1024
VMEM_MIB = {"ada": 24, "inproj": 56, "gdn": 48, "post": 56, "plan": 16, "dispatch": 24, "experts": 40,
            "final": 32}


def _sigmoid(x):
    return 0.5 + 0.5 * jnp.tanh(0.5 * x)


def _silu(x):
    half = 0.5 * x
    return half + half * jnp.tanh(half)


def _softplus(x):
    return jnp.maximum(x, 0.0) + jnp.log(1.0 + jnp.exp(-jnp.abs(x)))


def _dot(a, b):
    return jnp.dot(a, b, preferred_element_type=F32)


def _dot_hi(a, b):
    return jnp.dot(a, b, preferred_element_type=F32, precision=HIGHEST)


def _dot_split(a, b_hi, b_lo):
    a_hi = a.astype(BF16)
    a_lo = (a - a_hi.astype(F32)).astype(BF16)
    return _dot(a_hi, b_hi) + (_dot(a_lo, b_hi) + _dot(a_hi, b_lo))


def _read_rows(ref, rows):
    return jnp.concatenate([ref[pl.ds(j, rows, stride=SUBLANES), :] for j in range(SUBLANES)], axis=1)


def _write_rows(ref, value):
    rows = value.shape[0]
    for j in range(SUBLANES):
        ref[pl.ds(j, rows, stride=SUBLANES), :] = value[:, j * LANES:(j + 1) * LANES]


def _row_tile(ref, row):
    if not isinstance(row, int):
        row = pl.multiple_of(row * SUBLANES, SUBLANES)
    else:
        row = row * SUBLANES
    return ref.at[pl.ds(row, SUBLANES), :]


def _params(call, n_grid_axes=1):
    return pltpu.CompilerParams(dimension_semantics=("arbitrary",) * n_grid_axes,
                                vmem_limit_bytes=VMEM_MIB[call] * MIB)


def _ada_kernel(c_ref, w_ref, b_ref, o_ref):
    o_ref[...] = _dot_hi(_silu(c_ref[...]), w_ref[...]) + b_ref[...]


def _ada(c_pad, w_ada, b_ada):
    d = c_pad.shape[1]
    n = w_ada.shape[1]
    return pl.pallas_call(
        _ada_kernel,
        grid=(n // d,),
        in_specs=[
            pl.BlockSpec((SUBLANES, d), lambda j: (0, 0)),
            pl.BlockSpec((d, d), lambda j: (0, j)),
            pl.BlockSpec((1, d), lambda j: (0, j)),
        ],
        out_specs=pl.BlockSpec((SUBLANES, d), lambda j: (0, j)),
        out_shape=jax.ShapeDtypeStruct((SUBLANES, n), F32),
        compiler_params=_params("ada"),
        name="ada",
    )(c_pad, w_ada, b_ada)


def _inproj_kernel(x_ref, g_ref, sc_ref, sh_ref, wa_ref, wb_ref, wba_ref, o_ref, ba_ref, h_ref, *, na):
    j = pl.program_id(1)

    @pl.when(j == 0)
    def _():
        x = x_ref[...]
        y = x * lax.rsqrt(jnp.mean(x * x, axis=-1, keepdims=True) + EPS)
        h = (y * g_ref[...]) * (1.0 + sc_ref[...]) + sh_ref[...]
        hb = h.astype(BF16)
        h_ref[...] = hb
        ba_ref[...] = _dot(hb, wba_ref[...])

    @pl.when(j < na)
    def _():
        o_ref[...] = _dot(h_ref[...], wa_ref[...]).astype(BF16)

    @pl.when(j >= na)
    def _():
        o_ref[...] = _dot(h_ref[...], wb_ref[...]).astype(BF16)


def _inproj(x2, norm_g, sc, sh, w_a, w_b, w_ba, seq, tm):
    m, d = x2.shape
    tn = d
    na, nb = w_a.shape[1] // tn, w_b.shape[1] // tn
    n = (na + nb) * tn
    per_batch = seq // tm
    return pl.pallas_call(
        functools.partial(_inproj_kernel, na=na),
        grid=(m // tm, n // tn),
        in_specs=[
            pl.BlockSpec((tm, d), lambda i, j: (i, 0)),
            pl.BlockSpec((1, d), lambda i, j: (0, 0)),
            pl.BlockSpec((None, 1, d), lambda i, j: (i // per_batch, 0, 0)),
            pl.BlockSpec((None, 1, d), lambda i, j: (i // per_batch, 0, 0)),
            pl.BlockSpec((d, tn), lambda i, j: (0, jnp.minimum(j, na - 1))),
            pl.BlockSpec((d, tn), lambda i, j: (0, jnp.maximum(j - na, 0))),
            pl.BlockSpec((d, LANES), lambda i, j: (0, 0)),
        ],
        out_specs=[
            pl.BlockSpec((tm, tn), lambda i, j: (i, j)),
            pl.BlockSpec((tm, LANES), lambda i, j: (i, 0)),
        ],
        out_shape=[
            jax.ShapeDtypeStruct((m, n), BF16),
            jax.ShapeDtypeStruct((m, LANES), F32),
        ],
        scratch_shapes=[pltpu.VMEM((tm, d), BF16)],
        compiler_params=_params("inproj", 2),
        name="inproj",
    )(x2, norm_g, sc, sh, w_a, w_b, w_ba)


def _bmm(a, b):
    return jnp.einsum("hmk,hkn->hmn", a.astype(BF16), b.astype(BF16), preferred_element_type=F32)


def _bmm_nt(a, b):
    return jnp.einsum("hmk,hnk->hmn", a.astype(BF16), b.astype(BF16), preferred_element_type=F32)


def _unit_lower_inverse(a):
    row = lax.broadcasted_iota(jnp.int32, a.shape[1:], 0)
    col = lax.broadcasted_iota(jnp.int32, a.shape[1:], 1)
    apart = row ^ col
    eye = jnp.where(row == col, 1.0, 0.0).astype(F32)
    t = jnp.where(apart < 2, eye - a, 0.0)
    s = 2
    while s < CHUNK:
        coupling = jnp.where((apart >= s) & (apart < 2 * s), a, 0.0)
        t = t - _bmm(t, _bmm(coupling, t))
        s *= 2
    return t


def _lane_sums(x):
    h, rows, width = x.shape
    ones = jnp.ones((width, width), BF16)
    return _dot(x.reshape(h * rows, width).astype(BF16), ones).reshape(h, rows, width)


def _causal_conv_silu(win, cw, k_w):
    assert k_w == 4
    tiled = (win.shape[0] // SUBLANES, SUBLANES, win.shape[1])

    def pair(x, x1, j):
        return (x.reshape(tiled) * cw[j][None] + x1.reshape(tiled) * cw[j - 1][None]).reshape(win.shape)

    win1 = pltpu.roll(win, 1, 0)
    acc = pair(win, win1, 3) + pltpu.roll(pair(win, win1, 1), 2, 0)
    return _silu(acc[SUBLANES:, :])


def _gdn_kernel(q_ref, k_ref, v_ref, z_ref, ba_ref, cw_ref, hp_ref, o_ref,
                s_ref, tail_ref, wq_ref, u_ref, ik_ref, dec_ref):
    nb, tb = q_ref.shape[0], q_ref.shape[1]
    kd = N_HEADS * HEAD_D
    nbh = nb * N_HEADS

    @pl.when(pl.program_id(0) == 0)
    def _():
        s_ref[...] = jnp.zeros(s_ref.shape, F32)
        tail_ref[...] = jnp.zeros(tail_ref.shape, F32)

    row = lax.broadcasted_iota(jnp.int32, (CHUNK, CHUNK), 0)
    col = lax.broadcasted_iota(jnp.int32, (CHUNK, CHUNK), 1)
    causal = row >= col
    strict = row > col
    tril = jnp.where(causal, 1.0, 0.0).astype(F32)
    a_log = hp_ref[0:1, :]
    dt_bias = hp_ref[1:2, :]
    onorm_g = hp_ref[2:3, :]
    zeros_half = jnp.zeros((CHUNK, HEAD_D), F32)

    def precompute(cp, carry):
        qs, ks, vs, betas, gcs, grs, gls = [], [], [], [], [], [], []
        for sub in range(PRE_CHUNKS):
            base = pl.multiple_of((cp * PRE_CHUNKS + sub) * CHUNK, CHUNK)
            prev = pl.multiple_of(base - BF16_ROWS, BF16_ROWS)
            for b in range(nb):
                ba = ba_ref[b, pl.ds(base, CHUNK), :]
                beta_all = _sigmoid(ba)
                g_all = -jnp.exp(a_log) * _softplus(ba + dt_bias)
                gcum = _dot_hi(tril, g_all)
                gcum_t = jnp.concatenate([gcum, gcum], axis=0).T
                for h in range(N_HEADS):
                    lo, hi = h * HEAD_D, (h + 1) * HEAD_D

                    def conv(ref, off):
                        cur = ref[b, pl.ds(base, CHUNK), lo:hi].astype(F32)
                        if sub == 0:
                            before = tail_ref[b, :, off + lo:off + hi]
                        else:
                            before = ref[b, pl.ds(prev, BF16_ROWS), lo:hi].astype(F32)[SUBLANES:]
                        win = jnp.concatenate([before, cur], axis=0)
                        return _causal_conv_silu(win, cw_ref[:, :, off + lo:off + hi], QKV_CONV)

                    qs.append(conv(q_ref, 0))
                    ks.append(conv(k_ref, kd))
                    vs.append(conv(v_ref, 2 * kd))
                    betas.append(beta_all[:, h:h + 1])
                    gcs.append(gcum[:, N_HEADS + h:N_HEADS + h + 1])
                    grs.append(gcum_t[N_HEADS + h:N_HEADS + h + 1, 0:CHUNK])
                    gls.append(gcum[CHUNK - 1:CHUNK, N_HEADS + h:N_HEADS + h + 1])
        q, k, v = jnp.stack(qs), jnp.stack(ks), jnp.stack(vs)
        beta, gc, gr, gl = jnp.stack(betas), jnp.stack(gcs), jnp.stack(grs), jnp.stack(gls)
        qn = q * (lax.rsqrt(_lane_sums(q * q) + EPS) * (HEAD_D ** -0.5))
        kn = k * lax.rsqrt(_lane_sums(k * k) + EPS)
        decay = jnp.where(causal, jnp.exp(jnp.where(causal, gc - gr, 0.0)), 0.0)
        kb = kn * beta
        e_gc = jnp.exp(gc)
        kq = _bmm_nt(jnp.concatenate([kb, qn], axis=1), kn)
        a = jnp.where(strict, kq[:, :CHUNK] * decay, 0.0)
        intra = kq[:, CHUNK:] * decay
        uw = _bmm(_unit_lower_inverse(a), jnp.concatenate([v * beta, kb * e_gc], axis=2))
        wq = jnp.concatenate([uw[:, :, HEAD_D:], qn * e_gc], axis=1).astype(BF16)
        k_dec = kn * jnp.exp(gl - gc)
        k_dec_t = jnp.stack([jnp.concatenate([k_dec[i], zeros_half], axis=0).T[:, :CHUNK]
                             for i in range(PRE_CHUNKS * nbh)])
        ik = jnp.concatenate([intra, k_dec_t], axis=1).astype(BF16)
        dec = jnp.broadcast_to(jnp.exp(gl), (PRE_CHUNKS * nbh, 1, HEAD_D))
        for sub in range(PRE_CHUNKS):
            c = cp * PRE_CHUNKS + sub
            rows = slice(sub * nbh, (sub + 1) * nbh)
            u_ref[c] = uw[rows, :, :HEAD_D]
            wq_ref[c] = wq[rows]
            ik_ref[c] = ik[rows]
            dec_ref[c] = dec[rows]

        last = pl.multiple_of((cp + 1) * PRE_CHUNKS * CHUNK - BF16_ROWS, BF16_ROWS)
        for b in range(nb):
            for j, ref in enumerate((q_ref, k_ref, v_ref)):
                rows = ref[b, pl.ds(last, BF16_ROWS), :].astype(F32)
                tail_ref[b, :, j * kd:(j + 1) * kd] = rows[BF16_ROWS - SUBLANES:, :]
        return carry

    def recur(c, carry):
        base = pl.multiple_of(c * CHUNK, CHUNK)
        state = s_ref[...]
        ws = _bmm(wq_ref[c], state)
        v_new = u_ref[c] - ws[:, :CHUNK]
        r = _bmm(ik_ref[c], v_new)
        o = ws[:, CHUNK:] + r[:, :CHUNK]
        s_ref[...] = state * dec_ref[c] + r[:, CHUNK:]
        on = o * lax.rsqrt(jnp.mean(o * o, axis=-1, keepdims=True) + EPS) * onorm_g
        for b in range(nb):
            for h in range(N_HEADS):
                lo, hi = h * HEAD_D, (h + 1) * HEAD_D
                z = z_ref[b, pl.ds(base, CHUNK), lo:hi].astype(F32)
                o_ref[b, pl.ds(base, CHUNK), lo:hi] = (on[b * N_HEADS + h] * _silu(z)).astype(BF16)
        return carry

    lax.fori_loop(0, tb // (PRE_CHUNKS * CHUNK), precompute, 0)
    lax.fori_loop(0, tb // CHUNK, recur, 0)


def _gdn(proj, ba, conv_w, head_params, batch, seq, tb):
    kd = N_HEADS * HEAD_D
    nc = tb // CHUNK
    nbh = batch * N_HEADS
    proj3 = proj.reshape(batch, seq, proj.shape[1])
    ba3 = ba.reshape(batch, seq, LANES)

    def col(j):
        return pl.BlockSpec((batch, tb, kd), lambda t: (0, t, j))

    out = pl.pallas_call(
        _gdn_kernel,
        grid=(seq // tb,),
        in_specs=[
            col(0), col(1), col(2), col(3),
            pl.BlockSpec((batch, tb, LANES), lambda t: (0, t, 0)),
            pl.BlockSpec((QKV_CONV, SUBLANES, 3 * kd), lambda t: (0, 0, 0)),
            pl.BlockSpec((SUBLANES, LANES), lambda t: (0, 0)),
        ],
        out_specs=pl.BlockSpec((batch, tb, kd), lambda t: (0, t, 0)),
        out_shape=jax.ShapeDtypeStruct((batch, seq, kd), BF16),
        scratch_shapes=[
            pltpu.VMEM((nbh, HEAD_D, HEAD_D), F32),
            pltpu.VMEM((batch, SUBLANES, 3 * kd), F32),
            pltpu.VMEM((nc, nbh, 2 * CHUNK, HEAD_D), BF16),
            pltpu.VMEM((nc, nbh, CHUNK, HEAD_D), F32),
            pltpu.VMEM((nc, nbh, CHUNK + HEAD_D, CHUNK), BF16),
            pltpu.VMEM((nc, nbh, 1, HEAD_D), F32),
        ],
        compiler_params=_params("gdn"),
        name="gdn",
    )(proj3, proj3, proj3, proj3, ba3, conv_w, head_params)
    return out.reshape(batch * seq, kd)


def _post_kernel(x_ref, og_ref, sb_ref, sc_ref, sx_ref, ga_ref, gb_ref, mod_ref, n2g_ref, cw_ref,
                 wpa_ref, wpb_ref, wout_ref, wrh_ref, wrl_ref, br_ref,
                 x1_ref, h2_ref, route_ref, cnt_ref, win_ref, run_ref, *, per_batch):
    tm = x_ref.shape[0]
    i = pl.program_id(0)

    @pl.when(i == 0)
    def _():
        run_ref[...] = jnp.zeros(run_ref.shape, F32)

    @pl.when(i % per_batch == 0)
    def _():
        win_ref[...] = jnp.zeros(win_ref.shape, F32)

    assert SC_CONV == 3
    prod = sc_ref[...].astype(F32) * sx_ref[...].astype(F32)
    win = jnp.concatenate([win_ref[...], prod], axis=0)
    win_ref[...] = prod[tm - SUBLANES:, :]
    tiled = (win.shape[0] // SUBLANES, SUBLANES, win.shape[1])
    win1 = pltpu.roll(win, 1, 0)
    conv = (win.reshape(tiled) * cw_ref[2][None] + win1.reshape(tiled) * cw_ref[1][None]
            + pltpu.roll(win1, 1, 0).reshape(tiled) * cw_ref[0][None]).reshape(win.shape)[SUBLANES:, :]
    y_b = _dot((sb_ref[...].astype(F32) * conv).astype(BF16), wpb_ref[...])
    y_a = _dot(og_ref[...], wpa_ref[...])
    merged = _sigmoid(ga_ref[...].astype(F32)) * y_a + _sigmoid(gb_ref[...].astype(F32)) * y_b
    mix = _dot(merged.astype(BF16), wout_ref[...])
    x1 = x_ref[...] + mod_ref[0:1, :] * mix
    x1_ref[...] = x1

    y = x1 * lax.rsqrt(jnp.mean(x1 * x1, axis=-1, keepdims=True) + EPS)
    h2 = (y * n2g_ref[...]) * (1.0 + mod_ref[1:2, :]) + mod_ref[2:3, :]
    _write_rows(h2_ref, h2)

    lg = _dot_split(h2, wrh_ref[...], wrl_ref[...]) + br_ref[...]
    lane = lax.broadcasted_iota(jnp.int32, lg.shape, 1).astype(F32)
    neg = jnp.float32(-jnp.inf)
    big = jnp.float32(2 * LANES)

    def first_max(mask):
        vmax = jnp.max(jnp.where(mask, lg, neg), axis=-1, keepdims=True)
        idx = jnp.min(jnp.where(mask & (lg == vmax), lane, big), axis=-1, keepdims=True)
        return vmax, idx

    gmask = lane < N_GROUPS
    g_max, g_sel = first_max(gmask)
    p_group = 1.0 / jnp.sum(jnp.where(gmask, jnp.exp(lg - g_max), 0.0), axis=-1, keepdims=True)
    e_lo = N_GROUPS + EXPERTS_PER_GROUP * g_sel
    emask = (lane >= e_lo) & (lane < e_lo + EXPERTS_PER_GROUP)
    v1, i1 = first_max(emask)
    v2, i2 = first_max(emask & (lane != i1))
    ex = jnp.exp(v2 - v1)
    w1 = p_group * (1.0 / (1.0 + ex))
    w2 = p_group * (ex / (1.0 + ex))
    e1 = i1 - N_GROUPS
    e2 = i2 - N_GROUPS

    onehot = jnp.where((lane == e1) | (lane == e2), 1.0, 0.0).astype(F32)
    row = lax.broadcasted_iota(jnp.int32, (tm, tm), 0)
    col = lax.broadcasted_iota(jnp.int32, (tm, tm), 1)
    before = jnp.where(row > col, 1.0, 0.0).astype(BF16)
    seen = _dot(before, onehot.astype(BF16)) + run_ref[0:1, :]
    r1 = jnp.sum(jnp.where(lane == e1, seen, 0.0), axis=-1, keepdims=True)
    r2 = jnp.sum(jnp.where(lane == e2, seen, 0.0), axis=-1, keepdims=True)
    run_ref[0:1, :] = run_ref[0:1, :] + jnp.sum(onehot, axis=0, keepdims=True)
    cnt_ref[...] = jnp.broadcast_to(run_ref[0:1, :], cnt_ref.shape)

    out = jnp.where(lane == 0, e1, 0.0)
    out = jnp.where(lane == 1, e2, out)
    out = jnp.where(lane == 2, w1, out)
    out = jnp.where(lane == 3, w2, out)
    out = jnp.where(lane == 4, r1, out)
    out = jnp.where(lane == 5, r2, out)
    route_ref[...] = out


def _post(x2, og, proj, mod, n2g, conv_w, wpa, wpb, wout, w_route_hi, w_route_lo, b_route, seq, tm):
    m, d = x2.shape
    assert d == SUBLANES * LANES
    per_batch = seq // tm

    def rows(j):
        return pl.BlockSpec((tm, d), lambda i: (i, j))

    def whole(shape):
        return pl.BlockSpec(shape, lambda i: tuple(0 for _ in shape))

    return pl.pallas_call(
        functools.partial(_post_kernel, per_batch=per_batch),
        grid=(m // tm,),
        in_specs=[
            rows(0), rows(0), rows(4), rows(5), rows(6), rows(7), rows(8),
            pl.BlockSpec((None, SUBLANES, d), lambda i: (i // per_batch, 0, 0)),
            whole((1, d)), whole((SC_CONV, SUBLANES, d)),
            whole((d, d)), whole((d, d)), whole((d, d)),
            whole((d, LANES)), whole((d, LANES)), whole((1, LANES)),
        ],
        out_specs=[
            rows(0), pl.BlockSpec((tm * SUBLANES, LANES), lambda i: (i, 0)),
            pl.BlockSpec((tm, LANES), lambda i: (i, 0)),
            pl.BlockSpec((SUBLANES, LANES), lambda i: (0, 0)),
        ],
        out_shape=[
            jax.ShapeDtypeStruct((m, d), F32),
            jax.ShapeDtypeStruct((m * SUBLANES, LANES), F32),
            jax.ShapeDtypeStruct((m, LANES), F32),
            jax.ShapeDtypeStruct((SUBLANES, LANES), F32),
        ],
        scratch_shapes=[
            pltpu.VMEM((SUBLANES, d), F32),
            pltpu.VMEM((SUBLANES, LANES), F32),
        ],
        compiler_params=_params("post"),
        name="post",
    )(x2, og, proj, proj, proj, proj, proj, mod, n2g, conv_w, wpa, wpb, wout, w_route_hi, w_route_lo,
      b_route)


def _plan_kernel(route_ref, cnt_ref, d_ref):
    bm = EXPERT_BLOCK
    sizes = cnt_ref[...]
    padded = jnp.floor((sizes + (bm - 1.0)) * (1.0 / bm)) * bm
    lane_i = lax.broadcasted_iota(jnp.int32, sizes.shape, 1)
    incl = padded
    s = 1
    while s < LANES:
        incl = incl + jnp.where(lane_i >= s, pltpu.roll(incl, s, 1), 0.0)
        s *= 2
    start = (incl - padded)[0:1, :]
    r = route_ref[...]
    lane = lax.broadcasted_iota(jnp.int32, r.shape, 1).astype(F32)
    d1 = jnp.sum(jnp.where(lane == r[:, 0:1], start, 0.0), axis=-1, keepdims=True) + r[:, 4:5]
    d2 = jnp.sum(jnp.where(lane == r[:, 1:2], start, 0.0), axis=-1, keepdims=True) + r[:, 5:6]
    out = jnp.where(lane == 0.0, d1, jnp.where(lane == 1.0, d2, 0.0))
    d_ref[...] = out.T[0:SUBLANES, :].astype(jnp.int32)


def _plan(route, counts, tm):
    m = route.shape[0]
    return pl.pallas_call(
        _plan_kernel,
        grid=(m // tm,),
        in_specs=[
            pl.BlockSpec((tm, LANES), lambda i: (i, 0)),
            pl.BlockSpec((SUBLANES, LANES), lambda i: (0, 0)),
        ],
        out_specs=pl.BlockSpec((SUBLANES, tm), lambda i: (0, i)),
        out_shape=jax.ShapeDtypeStruct((SUBLANES, m), jnp.int32),
        compiler_params=_params("plan"),
        name="plan",
    )(route, counts)


def _row_copy(src_ref, src_row, dst_ref, dst_row, sem):
    return pltpu.make_async_copy(_row_tile(src_ref, src_row), _row_tile(dst_ref, dst_row), sem)


def _rows_copy(src_ref, dst_ref, dst_row, n, sem):
    return pltpu.make_async_copy(src_ref, dst_ref.at[pl.ds(dst_row * SUBLANES, n * SUBLANES), :], sem)


def _dispatch_kernel(d1_ref, d2_ref, tail_ref, h2_ref, xs_ref, zero_ref, sem):
    tm = h2_ref.shape[0] // SUBLANES
    bm = zero_ref.shape[0] // SUBLANES
    t0 = pl.program_id(0) * tm

    @pl.when(pl.program_id(0) == 0)
    def _():
        zero_ref[...] = jnp.zeros(zero_ref.shape, F32)
        n_blocks = xs_ref.shape[0] // (bm * SUBLANES)

        def zero_block(row):
            return _rows_copy(zero_ref, xs_ref, pl.multiple_of(row, bm), bm, sem.at[0])

        for e in range(N_EXPERTS):
            @pl.when(tail_ref[e] >= 0)
            def _():
                zero_block(tail_ref[e]).start()
        lax.fori_loop(tail_ref[N_EXPERTS], n_blocks, lambda j, c: (zero_block(j * bm).start(), c)[1], 0)
        for e in range(N_EXPERTS):
            @pl.when(tail_ref[e] >= 0)
            def _():
                zero_block(tail_ref[e]).wait()
        lax.fori_loop(tail_ref[N_EXPERTS], n_blocks, lambda j, c: (zero_block(j * bm).wait(), c)[1], 0)

    def start(r, carry):
        _row_copy(h2_ref, r, xs_ref, d1_ref[t0 + r], sem.at[0]).start(priority=0)
        _row_copy(h2_ref, r, xs_ref, d2_ref[t0 + r], sem.at[1]).start(priority=1)
        return carry

    lax.fori_loop(0, tm, start, 0, unroll=DMA_UNROLL)
    _rows_copy(h2_ref, xs_ref, 0, tm, sem.at[0]).wait()
    _rows_copy(h2_ref, xs_ref, 0, tm, sem.at[1]).wait()


def _dispatch(d1, d2, tail_start, h2, cap, tm):
    m = h2.shape[0] // SUBLANES
    return pl.pallas_call(
        _dispatch_kernel,
        grid_spec=pltpu.PrefetchScalarGridSpec(
            num_scalar_prefetch=3,
            grid=(m // tm,),
            in_specs=[pl.BlockSpec((tm * SUBLANES, LANES), lambda i, d1, d2, tl: (i, 0))],
            out_specs=pl.BlockSpec(memory_space=pl.ANY),
            scratch_shapes=[pltpu.VMEM((EXPERT_BLOCK * SUBLANES, LANES), F32),
                            pltpu.SemaphoreType.DMA((2,))],
        ),
        out_shape=jax.ShapeDtypeStruct((cap * SUBLANES, LANES), F32),
        compiler_params=_params("dispatch"),
        name="dispatch",
    )(d1, d2, tail_start, h2)


def _expert_kernel(be_ref, act_ref, x_ref, w1_ref, w3_ref, w2_ref, y_ref, w1b_ref, w3b_ref, w2b_ref):
    b = pl.program_id(0)

    @pl.when(act_ref[b] > 0)
    def _():
        @pl.when((b == 0) | (be_ref[b] != be_ref[jnp.maximum(b - 1, 0)]))
        def _():
            w1b_ref[...] = w1_ref[...].astype(BF16)
            w3b_ref[...] = w3_ref[...].astype(BF16)
            w2b_ref[...] = w2_ref[...].astype(BF16)

        xb = _read_rows(x_ref, x_ref.shape[0] // SUBLANES).astype(BF16)
        hid = _silu(_dot(xb, w1b_ref[...])) * _dot(xb, w3b_ref[...])
        _write_rows(y_ref, _dot(hid.astype(BF16), w2b_ref[...]))

    @pl.when(act_ref[b] == 0)
    def _():
        y_ref[...] = jnp.zeros(y_ref.shape, F32)


def _experts(block_expert, block_active, xs, w1, w3, w2):
    cap = xs.shape[0] // SUBLANES
    d, de = w1.shape[1], w1.shape[2]
    bm = EXPERT_BLOCK

    def x_block(b, be, act):
        return (jnp.minimum(b, jnp.maximum(act[cap // bm], 1) - 1), 0)

    return pl.pallas_call(
        _expert_kernel,
        grid_spec=pltpu.PrefetchScalarGridSpec(
            num_scalar_prefetch=2,
            grid=(cap // bm,),
            in_specs=[
                pl.BlockSpec((bm * SUBLANES, LANES), x_block),
                pl.BlockSpec((None, d, de), lambda b, be, act: (be[b], 0, 0)),
                pl.BlockSpec((None, d, de), lambda b, be, act: (be[b], 0, 0)),
                pl.BlockSpec((None, de, d), lambda b, be, act: (be[b], 0, 0)),
            ],
            out_specs=pl.BlockSpec((bm * SUBLANES, LANES), lambda b, be, act: (b, 0)),
            scratch_shapes=[pltpu.VMEM((d, de), BF16), pltpu.VMEM((d, de), BF16),
                            pltpu.VMEM((de, d), BF16)],
        ),
        out_shape=jax.ShapeDtypeStruct((cap * SUBLANES, LANES), F32),
        compiler_params=_params("experts"),
        name="experts",
    )(block_expert, block_active, xs, w1, w3, w2)


def _final_kernel(d1_ref, d2_ref, x1_ref, route_ref, gt_ref, nfg_ref, ys_ref, o_ref, buf_ref, sem):
    tm = x1_ref.shape[0]
    i = pl.program_id(0)

    def gather(tile, s):
        t0 = tile * tm

        def start(r, carry):
            _row_copy(ys_ref, d1_ref[t0 + r], buf_ref.at[s, 0], r, sem.at[s, 0]).start(priority=0)
            _row_copy(ys_ref, d2_ref[t0 + r], buf_ref.at[s, 1], r, sem.at[s, 1]).start(priority=1)
            return carry

        lax.fori_loop(0, tm, start, 0, unroll=DMA_UNROLL)

    def wait_rows(s):
        for k in range(2):
            pltpu.make_async_copy(ys_ref.at[pl.ds(0, tm * SUBLANES), :], buf_ref.at[s, k],
                                  sem.at[s, k]).wait()

    def combine(cur, nxt):
        wait_rows(cur)
        last = pl.num_programs(0) - 1
        t0 = jnp.minimum(i + 1, last) * tm
        for r in range(tm):
            _row_copy(ys_ref, d1_ref[t0 + r], buf_ref.at[nxt, 0], r, sem.at[nxt, 0]).start(priority=0)
            _row_copy(ys_ref, d2_ref[t0 + r], buf_ref.at[nxt, 1], r, sem.at[nxt, 1]).start(priority=1)
        route = route_ref[...]
        moe = (_read_rows(buf_ref.at[cur, 0], tm) * route[:, 2:3]
               + _read_rows(buf_ref.at[cur, 1], tm) * route[:, 3:4])
        x2 = x1_ref[...] + gt_ref[...] * moe
        y = x2 * lax.rsqrt(jnp.mean(x2 * x2, axis=-1, keepdims=True) + EPS)
        o_ref[...] = y * nfg_ref[...]

        @pl.when(i == last)
        def _():
            wait_rows(nxt)

    @pl.when(i == 0)
    def _():
        gather(0, 0)

    @pl.when(i % 2 == 0)
    def _():
        combine(0, 1)

    @pl.when(i % 2 == 1)
    def _():
        combine(1, 0)


def _final(d1, d2, x1, route, gt2, nfg, ys, seq, tm):
    m, d = x1.shape
    per_batch = seq // tm
    return pl.pallas_call(
        _final_kernel,
        grid_spec=pltpu.PrefetchScalarGridSpec(
            num_scalar_prefetch=2,
            grid=(m // tm,),
            in_specs=[
                pl.BlockSpec((tm, d), lambda i, d1, d2: (i, 0)),
                pl.BlockSpec((tm, LANES), lambda i, d1, d2: (i, 0)),
                pl.BlockSpec((None, 1, d), lambda i, d1, d2: (i // per_batch, 0, 0)),
                pl.BlockSpec((1, d), lambda i, d1, d2: (0, 0)),
                pl.BlockSpec(memory_space=pl.ANY),
            ],
            out_specs=pl.BlockSpec((tm, d), lambda i, d1, d2: (i, 0)),
            scratch_shapes=[pltpu.VMEM((2, 2, tm * SUBLANES, LANES), F32),
                            pltpu.SemaphoreType.DMA((2, 2))],
        ),
        out_shape=jax.ShapeDtypeStruct((m, d), F32),
        compiler_params=_params("final"),
        name="final",
    )(d1, d2, x1, route, gt2, nfg, ys)


def _tile(n, pref):
    t = min(n, pref)
    assert n % t == 0
    return t


def kernel(x, c, w_ada, b_ada, norm1_g, w_in, conv_qkv_w, a_log, dt_bias, onorm_g, w_proj_a,
           conv_sc_w, w_proj_b, w_out, norm2_g, w_group, b_group, w_expert, b_expert, w1, w3, w2,
           normf_g):
    batch, seq, d = x.shape
    depth = w_ada.shape[0]
    m = batch * seq
    kd = N_HEADS * HEAD_D
    assert d == kd and seq % CHUNK == 0 and batch <= SUBLANES
    assert depth == 1, "the last stage applies the final rmsnorm: one layer only"
    c_pad = jnp.zeros((SUBLANES, d), F32).at[:batch].set(c)
    x2 = x.reshape(m, d)

    for l in range(depth):
        mod = _ada(c_pad, w_ada[l], b_ada[l][None, :])[:batch]
        sh1, sc1, gt1, sh2, sc2, gt2 = [mod[:, None, j * d:(j + 1) * d] for j in range(6)]

        w = w_in[l]
        o_ba = 3 * kd + kd
        w_a = w[:, :o_ba].astype(BF16)
        w_b = w[:, o_ba + 2 * N_HEADS:].astype(BF16)
        w_ba = jnp.zeros((d, LANES), F32).at[:, :2 * N_HEADS].set(w[:, o_ba:o_ba + 2 * N_HEADS]).astype(BF16)
        proj, ba = _inproj(x2, norm1_g[l][None, :], sc1, sh1, w_a, w_b, w_ba, seq, _tile(seq, INPROJ_ROWS))

        head_params = jnp.zeros((SUBLANES, LANES), F32)
        head_params = head_params.at[0, N_HEADS:2 * N_HEADS].set(a_log[l])
        head_params = head_params.at[1, N_HEADS:2 * N_HEADS].set(dt_bias[l])
        head_params = head_params.at[2, :].set(onorm_g[l])
        conv_taps = jnp.broadcast_to(conv_qkv_w[l][:, None, :], (QKV_CONV, SUBLANES, 3 * kd))
        og = _gdn(proj, ba, conv_taps, head_params, batch, seq, _tile(seq, GDN_ROWS))

        mod_post = jnp.zeros((batch, SUBLANES, d), F32)
        mod_post = mod_post.at[:, 0:1].set(gt1).at[:, 1:2].set(sc2).at[:, 2:3].set(sh2)
        w_route = jnp.zeros((d, LANES), F32)
        w_route = w_route.at[:, :N_GROUPS].set(w_group[l]).at[:, N_GROUPS:N_GROUPS + N_EXPERTS].set(w_expert[l])
        w_route_hi = w_route.astype(BF16)
        w_route_lo = (w_route - w_route_hi.astype(F32)).astype(BF16)
        b_route = jnp.zeros((1, LANES), F32)
        b_route = b_route.at[0, :N_GROUPS].set(b_group[l]).at[0, N_GROUPS:N_GROUPS + N_EXPERTS].set(b_expert[l])
        x1, h2, route, counts = _post(
            x2, og, proj, mod_post, norm2_g[l][None, :],
            jnp.broadcast_to(conv_sc_w[l][:, None, :], (SC_CONV, SUBLANES, d)),
            w_proj_a[l].astype(BF16), w_proj_b[l].astype(BF16), w_out[l].astype(BF16),
            w_route_hi, w_route_lo, b_route, seq, _tile(seq, POST_ROWS))

        bm = EXPERT_BLOCK
        n_blocks = (2 * m) // bm + N_EXPERTS
        sizes = counts[0, :N_EXPERTS].astype(jnp.int32)
        padded = ((sizes + bm - 1) // bm) * bm
        pad_end = jnp.cumsum(padded)
        tail_start = jnp.concatenate([jnp.where(padded > 0, pad_end - bm, -1),
                                      pad_end[-1:] // bm]).astype(jnp.int32)
        block_row = jnp.arange(n_blocks, dtype=jnp.int32) * bm
        block_expert = jnp.minimum(jnp.sum(block_row[:, None] >= pad_end[None, :], axis=1),
                                   N_EXPERTS - 1).astype(jnp.int32)
        block_active = jnp.concatenate([(block_row < pad_end[-1]).astype(jnp.int32),
                                        (pad_end[-1:] // bm).astype(jnp.int32)])
        dest = _plan(route, counts, _tile(seq, INPROJ_ROWS))
        d1, d2 = dest[0], dest[1]

        xs = _dispatch(d1, d2, tail_start, h2, n_blocks * bm, _tile(seq, MOVE_ROWS))
        ys = _experts(block_expert, block_active, xs, w1[l], w3[l], w2[l])
        nfg = normf_g[None, :]
        x2 = _final(d1, d2, x1, route, gt2, nfg, ys, seq, _tile(seq, MOVE_ROWS))
    return x2.reshape(batch, seq, d)
```

```python
import functools

import jax
import jax.numpy as jnp
from jax import lax
from jax.experimental import pallas as pl
from jax.experimental.pallas import tpu as pltpu

F32 = jnp.float32
BF16 = jnp.bfloat16
HIGHEST = lax.Precision.HIGHEST

N_HEADS = 8
HEAD_D = 128
CHUNK = 64
QKV_CONV = 4
SC_CONV = 3
N_GROUPS = 4
EXPERTS_PER_GROUP = 8
N_EXPERTS = N_GROUPS * EXPERTS_PER_GROUP
EPS = 1e-6

LANES = 128
SUBLANES = 8
BF16_ROWS = 16
EXPERT_BLOCK = 512
INPROJ_ROWS = 2048
GDN_ROWS = 256
PRE_CHUNKS = 2
POST_ROWS = 512
MOVE_ROWS = 512
DMA_UNROLL = 8
MIB = 1024 * 1024
VMEM_MIB = {"ada": 24, "inproj": 56, "gdn": 48, "post": 56, "plan": 16, "invert": 16, "experts": 48,
            "final": 40}


def _sigmoid(x):
    return 0.5 + 0.5 * jnp.tanh(0.5 * x)


def _silu(x):
    half = 0.5 * x
    return half + half * jnp.tanh(half)


def _softplus(x):
    return jnp.maximum(x, 0.0) + jnp.log(1.0 + jnp.exp(-jnp.abs(x)))


def _dot(a, b):
    return jnp.dot(a, b, preferred_element_type=F32)


def _dot_hi(a, b):
    return jnp.dot(a, b, preferred_element_type=F32, precision=HIGHEST)


def _dot_split(a, b_hi, b_lo):
    a_hi = a.astype(BF16)
    a_lo = (a - a_hi.astype(F32)).astype(BF16)
    return _dot(a_hi, b_hi) + (_dot(a_lo, b_hi) + _dot(a_hi, b_lo))


def _read_rows(ref, rows):
    return jnp.concatenate([ref[pl.ds(j, rows, stride=SUBLANES), :] for j in range(SUBLANES)], axis=1)


def _write_rows(ref, value):
    rows = value.shape[0]
    for j in range(SUBLANES):
        ref[pl.ds(j, rows, stride=SUBLANES), :] = value[:, j * LANES:(j + 1) * LANES]


def _row_tile(ref, row):
    if not isinstance(row, int):
        row = pl.multiple_of(row * SUBLANES, SUBLANES)
    else:
        row = row * SUBLANES
    return ref.at[pl.ds(row, SUBLANES), :]


def _params(call, n_grid_axes=1):
    return pltpu.CompilerParams(dimension_semantics=("arbitrary",) * n_grid_axes,
                                vmem_limit_bytes=VMEM_MIB[call] * MIB)


def _ada_kernel(c_ref, w_ref, b_ref, o_ref):
    o_ref[...] = _dot_hi(_silu(c_ref[...]), w_ref[...]) + b_ref[...]


def _ada(c_pad, w_ada, b_ada):
    d = c_pad.shape[1]
    n = w_ada.shape[1]
    return pl.pallas_call(
        _ada_kernel,
        grid=(n // d,),
        in_specs=[
            pl.BlockSpec((SUBLANES, d), lambda j: (0, 0)),
            pl.BlockSpec((d, d), lambda j: (0, j)),
            pl.BlockSpec((1, d), lambda j: (0, j)),
        ],
        out_specs=pl.BlockSpec((SUBLANES, d), lambda j: (0, j)),
        out_shape=jax.ShapeDtypeStruct((SUBLANES, n), F32),
        compiler_params=_params("ada"),
        name="ada",
    )(c_pad, w_ada, b_ada)


def _inproj_kernel(x_ref, g_ref, sc_ref, sh_ref, wa_ref, wb_ref, wba_ref, o_ref, ba_ref, h_ref, *, na):
    j = pl.program_id(1)

    @pl.when(j == 0)
    def _():
        x = x_ref[...]
        y = x * lax.rsqrt(jnp.mean(x * x, axis=-1, keepdims=True) + EPS)
        h = (y * g_ref[...]) * (1.0 + sc_ref[...]) + sh_ref[...]
        hb = h.astype(BF16)
        h_ref[...] = hb
        ba_ref[...] = _dot(hb, wba_ref[...])

    @pl.when(j < na)
    def _():
        o_ref[...] = _dot(h_ref[...], wa_ref[...]).astype(BF16)

    @pl.when(j >= na)
    def _():
        o_ref[...] = _dot(h_ref[...], wb_ref[...]).astype(BF16)


def _inproj(x2, norm_g, sc, sh, w_a, w_b, w_ba, seq, tm):
    m, d = x2.shape
    tn = d
    na, nb = w_a.shape[1] // tn, w_b.shape[1] // tn
    n = (na + nb) * tn
    per_batch = seq // tm
    return pl.pallas_call(
        functools.partial(_inproj_kernel, na=na),
        grid=(m // tm, n // tn),
        in_specs=[
            pl.BlockSpec((tm, d), lambda i, j: (i, 0)),
            pl.BlockSpec((1, d), lambda i, j: (0, 0)),
            pl.BlockSpec((None, 1, d), lambda i, j: (i // per_batch, 0, 0)),
            pl.BlockSpec((None, 1, d), lambda i, j: (i // per_batch, 0, 0)),
            pl.BlockSpec((d, tn), lambda i, j: (0, jnp.minimum(j, na - 1))),
            pl.BlockSpec((d, tn), lambda i, j: (0, jnp.maximum(j - na, 0))),
            pl.BlockSpec((d, LANES), lambda i, j: (0, 0)),
        ],
        out_specs=[
            pl.BlockSpec((tm, tn), lambda i, j: (i, j)),
            pl.BlockSpec((tm, LANES), lambda i, j: (i, 0)),
        ],
        out_shape=[
            jax.ShapeDtypeStruct((m, n), BF16),
            jax.ShapeDtypeStruct((m, LANES), F32),
        ],
        scratch_shapes=[pltpu.VMEM((tm, d), BF16)],
        compiler_params=_params("inproj", 2),
        name="inproj",
    )(x2, norm_g, sc, sh, w_a, w_b, w_ba)


def _bmm(a, b):
    return jnp.einsum("hmk,hkn->hmn", a.astype(BF16), b.astype(BF16), preferred_element_type=F32)


def _bmm_nt(a, b):
    return jnp.einsum("hmk,hnk->hmn", a.astype(BF16), b.astype(BF16), preferred_element_type=F32)


def _unit_lower_inverse(a):
    row = lax.broadcasted_iota(jnp.int32, a.shape[1:], 0)
    col = lax.broadcasted_iota(jnp.int32, a.shape[1:], 1)
    apart = row ^ col
    eye = jnp.where(row == col, 1.0, 0.0).astype(F32)
    t = jnp.where(apart < 2, eye - a, 0.0)
    s = 2
    while s < CHUNK:
        coupling = jnp.where((apart >= s) & (apart < 2 * s), a, 0.0)
        t = t - _bmm(t, _bmm(coupling, t))
        s *= 2
    return t


def _lane_sums(x):
    h, rows, width = x.shape
    ones = jnp.ones((width, width), BF16)
    return _dot(x.reshape(h * rows, width).astype(BF16), ones).reshape(h, rows, width)


def _causal_conv_silu(win, cw, k_w):
    assert k_w == 4
    tiled = (win.shape[0] // SUBLANES, SUBLANES, win.shape[1])

    def pair(x, x1, j):
        return (x.reshape(tiled) * cw[j][None] + x1.reshape(tiled) * cw[j - 1][None]).reshape(win.shape)

    win1 = pltpu.roll(win, 1, 0)
    acc = pair(win, win1, 3) + pltpu.roll(pair(win, win1, 1), 2, 0)
    return _silu(acc[SUBLANES:, :])


def _gdn_kernel(q_ref, k_ref, v_ref, z_ref, ba_ref, cw_ref, hp_ref, o_ref,
                s_ref, tail_ref, wq_ref, u_ref, ik_ref, dec_ref):
    nb, tb = q_ref.shape[0], q_ref.shape[1]
    kd = N_HEADS * HEAD_D
    nbh = nb * N_HEADS

    @pl.when(pl.program_id(0) == 0)
    def _():
        s_ref[...] = jnp.zeros(s_ref.shape, F32)
        tail_ref[...] = jnp.zeros(tail_ref.shape, F32)

    row = lax.broadcasted_iota(jnp.int32, (CHUNK, CHUNK), 0)
    col = lax.broadcasted_iota(jnp.int32, (CHUNK, CHUNK), 1)
    causal = row >= col
    strict = row > col
    tril = jnp.where(causal, 1.0, 0.0).astype(F32)
    a_log = hp_ref[0:1, :]
    dt_bias = hp_ref[1:2, :]
    onorm_g = hp_ref[2:3, :]
    zeros_half = jnp.zeros((CHUNK, HEAD_D), F32)

    def precompute(cp, carry):
        qs, ks, vs, betas, gcs, grs, gls = [], [], [], [], [], [], []
        for sub in range(PRE_CHUNKS):
            base = pl.multiple_of((cp * PRE_CHUNKS + sub) * CHUNK, CHUNK)
            prev = pl.multiple_of(base - BF16_ROWS, BF16_ROWS)
            for b in range(nb):
                ba = ba_ref[b, pl.ds(base, CHUNK), :]
                beta_all = _sigmoid(ba)
                g_all = -jnp.exp(a_log) * _softplus(ba + dt_bias)
                gcum = _dot_hi(tril, g_all)
                gcum_t = jnp.concatenate([gcum, gcum], axis=0).T
                for h in range(N_HEADS):
                    lo, hi = h * HEAD_D, (h + 1) * HEAD_D

                    def conv(ref, off):
                        cur = ref[b, pl.ds(base, CHUNK), lo:hi].astype(F32)
                        if sub == 0:
                            before = tail_ref[b, :, off + lo:off + hi]
                        else:
                            before = ref[b, pl.ds(prev, BF16_ROWS), lo:hi].astype(F32)[SUBLANES:]
                        win = jnp.concatenate([before, cur], axis=0)
                        return _causal_conv_silu(win, cw_ref[:, :, off + lo:off + hi], QKV_CONV)

                    qs.append(conv(q_ref, 0))
                    ks.append(conv(k_ref, kd))
                    vs.append(conv(v_ref, 2 * kd))
                    betas.append(beta_all[:, h:h + 1])
                    gcs.append(gcum[:, N_HEADS + h:N_HEADS + h + 1])
                    grs.append(gcum_t[N_HEADS + h:N_HEADS + h + 1, 0:CHUNK])
                    gls.append(gcum[CHUNK - 1:CHUNK, N_HEADS + h:N_HEADS + h + 1])
        q, k, v = jnp.stack(qs), jnp.stack(ks), jnp.stack(vs)
        beta, gc, gr, gl = jnp.stack(betas), jnp.stack(gcs), jnp.stack(grs), jnp.stack(gls)
        qn = q * (lax.rsqrt(_lane_sums(q * q) + EPS) * (HEAD_D ** -0.5))
        kn = k * lax.rsqrt(_lane_sums(k * k) + EPS)
        decay = jnp.where(causal, jnp.exp(jnp.where(causal, gc - gr, 0.0)), 0.0)
        kb = kn * beta
        e_gc = jnp.exp(gc)
        kq = _bmm_nt(jnp.concatenate([kb, qn], axis=1), kn)
        a = jnp.where(strict, kq[:, :CHUNK] * decay, 0.0)
        intra = kq[:, CHUNK:] * decay
        uw = _bmm(_unit_lower_inverse(a), jnp.concatenate([v * beta, kb * e_gc], axis=2))
        wq = jnp.concatenate([uw[:, :, HEAD_D:], qn * e_gc], axis=1).astype(BF16)
        k_dec = kn * jnp.exp(gl - gc)
        k_dec_t = jnp.stack([jnp.concatenate([k_dec[i], zeros_half], axis=0).T[:, :CHUNK]
                             for i in range(PRE_CHUNKS * nbh)])
        ik = jnp.concatenate([intra, k_dec_t], axis=1).astype(BF16)
        dec = jnp.broadcast_to(jnp.exp(gl), (PRE_CHUNKS * nbh, 1, HEAD_D))
        for sub in range(PRE_CHUNKS):
            c = cp * PRE_CHUNKS + sub
            rows = slice(sub * nbh, (sub + 1) * nbh)
            u_ref[c] = uw[rows, :, :HEAD_D]
            wq_ref[c] = wq[rows]
            ik_ref[c] = ik[rows]
            dec_ref[c] = dec[rows]

        last = pl.multiple_of((cp + 1) * PRE_CHUNKS * CHUNK - BF16_ROWS, BF16_ROWS)
        for b in range(nb):
            for j, ref in enumerate((q_ref, k_ref, v_ref)):
                rows = ref[b, pl.ds(last, BF16_ROWS), :].astype(F32)
                tail_ref[b, :, j * kd:(j + 1) * kd] = rows[BF16_ROWS - SUBLANES:, :]
        return carry

    def recur(c, carry):
        base = pl.multiple_of(c * CHUNK, CHUNK)
        state = s_ref[...]
        ws = _bmm(wq_ref[c], state)
        v_new = u_ref[c] - ws[:, :CHUNK]
        r = _bmm(ik_ref[c], v_new)
        o = ws[:, CHUNK:] + r[:, :CHUNK]
        s_ref[...] = state * dec_ref[c] + r[:, CHUNK:]
        on = o * lax.rsqrt(jnp.mean(o * o, axis=-1, keepdims=True) + EPS) * onorm_g
        for b in range(nb):
            for h in range(N_HEADS):
                lo, hi = h * HEAD_D, (h + 1) * HEAD_D
                z = z_ref[b, pl.ds(base, CHUNK), lo:hi].astype(F32)
                o_ref[b, pl.ds(base, CHUNK), lo:hi] = (on[b * N_HEADS + h] * _silu(z)).astype(BF16)
        return carry

    lax.fori_loop(0, tb // (PRE_CHUNKS * CHUNK), precompute, 0)
    lax.fori_loop(0, tb // CHUNK, recur, 0)


def _gdn(proj, ba, conv_w, head_params, batch, seq, tb):
    kd = N_HEADS * HEAD_D
    nc = tb // CHUNK
    nbh = batch * N_HEADS
    proj3 = proj.reshape(batch, seq, proj.shape[1])
    ba3 = ba.reshape(batch, seq, LANES)

    def col(j):
        return pl.BlockSpec((batch, tb, kd), lambda t: (0, t, j))

    out = pl.pallas_call(
        _gdn_kernel,
        grid=(seq // tb,),
        in_specs=[
            col(0), col(1), col(2), col(3),
            pl.BlockSpec((batch, tb, LANES), lambda t: (0, t, 0)),
            pl.BlockSpec((QKV_CONV, SUBLANES, 3 * kd), lambda t: (0, 0, 0)),
            pl.BlockSpec((SUBLANES, LANES), lambda t: (0, 0)),
        ],
        out_specs=pl.BlockSpec((batch, tb, kd), lambda t: (0, t, 0)),
        out_shape=jax.ShapeDtypeStruct((batch, seq, kd), BF16),
        scratch_shapes=[
            pltpu.VMEM((nbh, HEAD_D, HEAD_D), F32),
            pltpu.VMEM((batch, SUBLANES, 3 * kd), F32),
            pltpu.VMEM((nc, nbh, 2 * CHUNK, HEAD_D), BF16),
            pltpu.VMEM((nc, nbh, CHUNK, HEAD_D), F32),
            pltpu.VMEM((nc, nbh, CHUNK + HEAD_D, CHUNK), BF16),
            pltpu.VMEM((nc, nbh, 1, HEAD_D), F32),
        ],
        compiler_params=_params("gdn"),
        name="gdn",
    )(proj3, proj3, proj3, proj3, ba3, conv_w, head_params)
    return out.reshape(batch * seq, kd)


def _post_kernel(x_ref, og_ref, sb_ref, sc_ref, sx_ref, ga_ref, gb_ref, mod_ref, n2g_ref, cw_ref,
                 wpa_ref, wpb_ref, wout_ref, wrh_ref, wrl_ref, br_ref,
                 x1_ref, h2_ref, route_ref, cnt_ref, win_ref, run_ref, *, per_batch):
    tm = x_ref.shape[0]
    i = pl.program_id(0)

    @pl.when(i == 0)
    def _():
        run_ref[...] = jnp.zeros(run_ref.shape, F32)

    @pl.when(i % per_batch == 0)
    def _():
        win_ref[...] = jnp.zeros(win_ref.shape, F32)

    assert SC_CONV == 3
    prod = sc_ref[...].astype(F32) * sx_ref[...].astype(F32)
    win = jnp.concatenate([win_ref[...], prod], axis=0)
    win_ref[...] = prod[tm - SUBLANES:, :]
    tiled = (win.shape[0] // SUBLANES, SUBLANES, win.shape[1])
    win1 = pltpu.roll(win, 1, 0)
    conv = (win.reshape(tiled) * cw_ref[2][None] + win1.reshape(tiled) * cw_ref[1][None]
            + pltpu.roll(win1, 1, 0).reshape(tiled) * cw_ref[0][None]).reshape(win.shape)[SUBLANES:, :]
    y_b = _dot((sb_ref[...].astype(F32) * conv).astype(BF16), wpb_ref[...])
    y_a = _dot(og_ref[...], wpa_ref[...])
    merged = _sigmoid(ga_ref[...].astype(F32)) * y_a + _sigmoid(gb_ref[...].astype(F32)) * y_b
    mix = _dot(merged.astype(BF16), wout_ref[...])
    x1 = x_ref[...] + mod_ref[0:1, :] * mix
    x1_ref[...] = x1

    y = x1 * lax.rsqrt(jnp.mean(x1 * x1, axis=-1, keepdims=True) + EPS)
    h2 = (y * n2g_ref[...]) * (1.0 + mod_ref[1:2, :]) + mod_ref[2:3, :]
    _write_rows(h2_ref, h2)

    lg = _dot_split(h2, wrh_ref[...], wrl_ref[...]) + br_ref[...]
    lane = lax.broadcasted_iota(jnp.int32, lg.shape, 1).astype(F32)
    neg = jnp.float32(-jnp.inf)
    big = jnp.float32(2 * LANES)

    def first_max(mask):
        vmax = jnp.max(jnp.where(mask, lg, neg), axis=-1, keepdims=True)
        idx = jnp.min(jnp.where(mask & (lg == vmax), lane, big), axis=-1, keepdims=True)
        return vmax, idx

    gmask = lane < N_GROUPS
    g_max, g_sel = first_max(gmask)
    p_group = 1.0 / jnp.sum(jnp.where(gmask, jnp.exp(lg - g_max), 0.0), axis=-1, keepdims=True)
    e_lo = N_GROUPS + EXPERTS_PER_GROUP * g_sel
    emask = (lane >= e_lo) & (lane < e_lo + EXPERTS_PER_GROUP)
    v1, i1 = first_max(emask)
    v2, i2 = first_max(emask & (lane != i1))
    ex = jnp.exp(v2 - v1)
    w1 = p_group * (1.0 / (1.0 + ex))
    w2 = p_group * (ex / (1.0 + ex))
    e1 = i1 - N_GROUPS
    e2 = i2 - N_GROUPS

    onehot = jnp.where((lane == e1) | (lane == e2), 1.0, 0.0).astype(F32)
    row = lax.broadcasted_iota(jnp.int32, (tm, tm), 0)
    col = lax.broadcasted_iota(jnp.int32, (tm, tm), 1)
    before = jnp.where(row > col, 1.0, 0.0).astype(BF16)
    seen = _dot(before, onehot.astype(BF16)) + run_ref[0:1, :]
    r1 = jnp.sum(jnp.where(lane == e1, seen, 0.0), axis=-1, keepdims=True)
    r2 = jnp.sum(jnp.where(lane == e2, seen, 0.0), axis=-1, keepdims=True)
    run_ref[0:1, :] = run_ref[0:1, :] + jnp.sum(onehot, axis=0, keepdims=True)
    cnt_ref[...] = jnp.broadcast_to(run_ref[0:1, :], cnt_ref.shape)

    out = jnp.where(lane == 0, e1, 0.0)
    out = jnp.where(lane == 1, e2, out)
    out = jnp.where(lane == 2, w1, out)
    out = jnp.where(lane == 3, w2, out)
    out = jnp.where(lane == 4, r1, out)
    out = jnp.where(lane == 5, r2, out)
    route_ref[...] = out


def _post(x2, og, proj, mod, n2g, conv_w, wpa, wpb, wout, w_route_hi, w_route_lo, b_route, seq, tm):
    m, d = x2.shape
    assert d == SUBLANES * LANES
    per_batch = seq // tm

    def rows(j):
        return pl.BlockSpec((tm, d), lambda i: (i, j))

    def whole(shape):
        return pl.BlockSpec(shape, lambda i: tuple(0 for _ in shape))

    return pl.pallas_call(
        functools.partial(_post_kernel, per_batch=per_batch),
        grid=(m // tm,),
        in_specs=[
            rows(0), rows(0), rows(4), rows(5), rows(6), rows(7), rows(8),
            pl.BlockSpec((None, SUBLANES, d), lambda i: (i // per_batch, 0, 0)),
            whole((1, d)), whole((SC_CONV, SUBLANES, d)),
            whole((d, d)), whole((d, d)), whole((d, d)),
            whole((d, LANES)), whole((d, LANES)), whole((1, LANES)),
        ],
        out_specs=[
            rows(0), pl.BlockSpec((tm * SUBLANES, LANES), lambda i: (i, 0)),
            pl.BlockSpec((tm, LANES), lambda i: (i, 0)),
            pl.BlockSpec((SUBLANES, LANES), lambda i: (0, 0)),
        ],
        out_shape=[
            jax.ShapeDtypeStruct((m, d), F32),
            jax.ShapeDtypeStruct((m * SUBLANES, LANES), F32),
            jax.ShapeDtypeStruct((m, LANES), F32),
            jax.ShapeDtypeStruct((SUBLANES, LANES), F32),
        ],
        scratch_shapes=[
            pltpu.VMEM((SUBLANES, d), F32),
            pltpu.VMEM((SUBLANES, LANES), F32),
        ],
        compiler_params=_params("post"),
        name="post",
    )(x2, og, proj, proj, proj, proj, proj, mod, n2g, conv_w, wpa, wpb, wout, w_route_hi, w_route_lo,
      b_route)


def _plan_kernel(route_ref, cnt_ref, d_ref):
    bm = EXPERT_BLOCK
    sizes = cnt_ref[...]
    padded = jnp.floor((sizes + (bm - 1.0)) * (1.0 / bm)) * bm
    lane_i = lax.broadcasted_iota(jnp.int32, sizes.shape, 1)
    incl = padded
    s = 1
    while s < LANES:
        incl = incl + jnp.where(lane_i >= s, pltpu.roll(incl, s, 1), 0.0)
        s *= 2
    start = (incl - padded)[0:1, :]
    r = route_ref[...]
    lane = lax.broadcasted_iota(jnp.int32, r.shape, 1).astype(F32)
    d1 = jnp.sum(jnp.where(lane == r[:, 0:1], start, 0.0), axis=-1, keepdims=True) + r[:, 4:5]
    d2 = jnp.sum(jnp.where(lane == r[:, 1:2], start, 0.0), axis=-1, keepdims=True) + r[:, 5:6]
    out = jnp.where(lane == 0.0, d1, jnp.where(lane == 1.0, d2, 0.0))
    d_ref[...] = out.T[0:SUBLANES, :].astype(jnp.int32)


def _plan(route, counts, tm):
    m = route.shape[0]
    return pl.pallas_call(
        _plan_kernel,
        grid=(m // tm,),
        in_specs=[
            pl.BlockSpec((tm, LANES), lambda i: (i, 0)),
            pl.BlockSpec((SUBLANES, LANES), lambda i: (0, 0)),
        ],
        out_specs=pl.BlockSpec((SUBLANES, tm), lambda i: (0, i)),
        out_shape=jax.ShapeDtypeStruct((SUBLANES, m), jnp.int32),
        compiler_params=_params("plan"),
        name="plan",
    )(route, counts)


def _row_copy(src_ref, src_row, dst_ref, dst_row, sem):
    return pltpu.make_async_copy(_row_tile(src_ref, src_row), _row_tile(dst_ref, dst_row), sem)


def _invert_kernel(lo_ref, hi_ref, d1_ref, d2_ref, inv_ref, *, m):
    tm = d1_ref.shape[0]
    bm = EXPERT_BLOCK
    i = pl.program_id(0)

    @pl.when(i == 0)
    def _():
        def fill(lo, hi):
            def group(g, carry):
                for j in range(DMA_UNROLL):
                    r = jnp.minimum(lo + g * DMA_UNROLL + j, inv_ref.shape[0] - 1)
                    inv_ref[r] = 2 * m + (r & (bm - 1))
                return carry

            lax.fori_loop(0, (hi - lo + DMA_UNROLL - 1) // DMA_UNROLL, group, 0)

        for e in range(N_EXPERTS):
            fill(lo_ref[e], hi_ref[e])
        fill(hi_ref[N_EXPERTS - 1], inv_ref.shape[0])

    def body(r, carry):
        t = i * tm + r
        inv_ref[d1_ref[r]] = t
        inv_ref[d2_ref[r]] = m + t
        return carry

    lax.fori_loop(0, tm, body, 0, unroll=DMA_UNROLL)


def _invert(pad_lo, pad_hi, d1, d2, cap, tm):
    m = d1.shape[0]
    assert EXPERT_BLOCK & (EXPERT_BLOCK - 1) == 0
    return pl.pallas_call(
        functools.partial(_invert_kernel, m=m),
        grid_spec=pltpu.PrefetchScalarGridSpec(
            num_scalar_prefetch=2,
            grid=(m // tm,),
            in_specs=[pl.BlockSpec((tm,), lambda i, lo, hi: (i,), memory_space=pltpu.SMEM),
                      pl.BlockSpec((tm,), lambda i, lo, hi: (i,), memory_space=pltpu.SMEM)],
            out_specs=pl.BlockSpec(memory_space=pltpu.SMEM),
        ),
        out_shape=jax.ShapeDtypeStruct((cap,), jnp.int32),
        compiler_params=_params("invert"),
        name="invert",
    )(pad_lo, pad_hi, d1, d2)


def _expert_kernel(be_ref, act_ref, inv_prv_ref, inv_cur_ref, inv_nxt_ref, h2_ref,
                   w1_ref, w3_ref, w2_ref, y_ref,
                   w1b_ref, w3b_ref, w2b_ref, x0_ref, x1_ref, y0_ref, y1_ref, gsem, ssem, *, m):
    b = pl.program_id(0)
    bm = x0_ref.shape[0] // SUBLANES
    n_active = act_ref[be_ref.shape[0]]

    def gather(inv_ref, r, x_ref, sem):
        return _row_copy(h2_ref, inv_ref[0, r] & (m - 1), x_ref, r, sem)

    def scatter(inv_ref, r, yb_ref, sem):
        return _row_copy(yb_ref, r, y_ref, inv_ref[0, r], sem)

    def wait_gathers(x_ref, sem):
        pltpu.make_async_copy(h2_ref.at[pl.ds(0, bm * SUBLANES), :], x_ref, sem).wait()

    def wait_scatters(yb_ref, sem):
        pltpu.make_async_copy(yb_ref, y_ref.at[pl.ds(0, bm * SUBLANES), :], sem).wait()

    def step(x_cur, x_nxt, y_cur, y_prv, g_cur, g_nxt, s_cur, s_prv, first):
        if first:
            @pl.when(b == 0)
            def _():
                y_cur[...] = jnp.zeros(y_cur.shape, F32)
                spare = pltpu.make_async_copy(
                    y_cur, y_ref.at[pl.ds(2 * m * SUBLANES, bm * SUBLANES), :], s_cur)
                spare.start()
                spare.wait()
                for r in range(bm):
                    gather(inv_cur_ref, r, x_cur, g_cur).start(priority=0)

        wait_gathers(x_cur, g_cur)

        @pl.when(b >= 2)
        def _():
            wait_scatters(y_cur, s_cur)

        @pl.when((b == 0) | (be_ref[b] != be_ref[jnp.maximum(b - 1, 0)]))
        def _():
            w1b_ref[...] = w1_ref[...].astype(BF16)
            w3b_ref[...] = w3_ref[...].astype(BF16)
            w2b_ref[...] = w2_ref[...].astype(BF16)

        has_next = b + 1 < n_active
        has_prev = b >= 1
        for r in range(bm):
            @pl.when(has_next)
            def _():
                gather(inv_nxt_ref, r, x_nxt, g_nxt).start(priority=0)

            @pl.when(has_prev)
            def _():
                scatter(inv_prv_ref, r, y_prv, s_prv).start(priority=1)

        xb = _read_rows(x_cur, bm).astype(BF16)
        hid = _silu(_dot(xb, w1b_ref[...])) * _dot(xb, w3b_ref[...])
        _write_rows(y_cur, _dot(hid.astype(BF16), w2b_ref[...]))

        @pl.when(b + 1 == n_active)
        def _():
            for r in range(bm):
                scatter(inv_cur_ref, r, y_cur, s_cur).start(priority=1)
            wait_scatters(y_cur, s_cur)

            @pl.when(has_prev)
            def _():
                wait_scatters(y_prv, s_prv)

    @pl.when((b < n_active) & (b % 2 == 0))
    def _():
        step(x0_ref, x1_ref, y0_ref, y1_ref, gsem.at[0], gsem.at[1], ssem.at[0], ssem.at[1], True)

    @pl.when((b < n_active) & (b % 2 == 1))
    def _():
        step(x1_ref, x0_ref, y1_ref, y0_ref, gsem.at[1], gsem.at[0], ssem.at[1], ssem.at[0], False)


def _experts(block_expert, block_active, inv, h2, w1, w3, w2):
    m = h2.shape[0] // SUBLANES
    assert m & (m - 1) == 0
    d, de = w1.shape[1], w1.shape[2]
    bm = EXPERT_BLOCK
    n_blocks = block_expert.shape[0]
    inv3 = inv.reshape(n_blocks, 1, bm)

    def rows_of(offset):
        return pl.BlockSpec((None, 1, bm),
                            lambda b, be, act: (jnp.clip(b + offset, 0, n_blocks - 1), 0, 0),
                            memory_space=pltpu.SMEM)

    tile = pltpu.VMEM((bm * SUBLANES, LANES), F32)
    return pl.pallas_call(
        functools.partial(_expert_kernel, m=m),
        grid_spec=pltpu.PrefetchScalarGridSpec(
            num_scalar_prefetch=2,
            grid=(n_blocks,),
            in_specs=[
                rows_of(-1), rows_of(0), rows_of(1),
                pl.BlockSpec(memory_space=pl.ANY),
                pl.BlockSpec((None, d, de), lambda b, be, act: (be[b], 0, 0)),
                pl.BlockSpec((None, d, de), lambda b, be, act: (be[b], 0, 0)),
                pl.BlockSpec((None, de, d), lambda b, be, act: (be[b], 0, 0)),
            ],
            out_specs=pl.BlockSpec(memory_space=pl.ANY),
            scratch_shapes=[
                pltpu.VMEM((d, de), BF16), pltpu.VMEM((d, de), BF16), pltpu.VMEM((de, d), BF16),
                tile, tile,
                tile, tile,
                pltpu.SemaphoreType.DMA((2,)), pltpu.SemaphoreType.DMA((2,)),
            ],
        ),
        out_shape=jax.ShapeDtypeStruct(((2 * m + bm) * SUBLANES, LANES), F32),
        compiler_params=_params("experts"),
        name="experts",
    )(block_expert, block_active, inv3, inv3, inv3, h2, w1, w3, w2)


def _final_kernel(x1_ref, ya_ref, yb_ref, route_ref, gt_ref, nfg_ref, o_ref):
    tm = x1_ref.shape[0]
    route = route_ref[...]
    moe = _read_rows(ya_ref, tm) * route[:, 2:3] + _read_rows(yb_ref, tm) * route[:, 3:4]
    x2 = x1_ref[...] + gt_ref[...] * moe
    y = x2 * lax.rsqrt(jnp.mean(x2 * x2, axis=-1, keepdims=True) + EPS)
    o_ref[...] = y * nfg_ref[...]


def _final(x1, route, gt2, nfg, ys, seq, tm):
    m, d = x1.shape
    per_batch = seq // tm
    return pl.pallas_call(
        _final_kernel,
        grid=(m // tm,),
        in_specs=[
            pl.BlockSpec((tm, d), lambda i: (i, 0)),
            pl.BlockSpec((tm * SUBLANES, LANES), lambda i: (i, 0)),
            pl.BlockSpec((tm * SUBLANES, LANES), lambda i: (i + m // tm, 0)),
            pl.BlockSpec((tm, LANES), lambda i: (i, 0)),
            pl.BlockSpec((None, 1, d), lambda i: (i // per_batch, 0, 0)),
            pl.BlockSpec((1, d), lambda i: (0, 0)),
        ],
        out_specs=pl.BlockSpec((tm, d), lambda i: (i, 0)),
        out_shape=jax.ShapeDtypeStruct((m, d), F32),
        compiler_params=_params("final"),
        name="final",
    )(x1, ys, ys, route, gt2, nfg)


def _tile(n, pref):
    t = min(n, pref)
    assert n % t == 0
    return t


def kernel(x, c, w_ada, b_ada, norm1_g, w_in, conv_qkv_w, a_log, dt_bias, onorm_g, w_proj_a,
           conv_sc_w, w_proj_b, w_out, norm2_g, w_group, b_group, w_expert, b_expert, w1, w3, w2,
           normf_g):
    batch, seq, d = x.shape
    depth = w_ada.shape[0]
    m = batch * seq
    kd = N_HEADS * HEAD_D
    assert d == kd and seq % CHUNK == 0 and batch <= SUBLANES
    assert depth == 1, "the last stage applies the final rmsnorm: one layer only"
    c_pad = jnp.zeros((SUBLANES, d), F32).at[:batch].set(c)
    x2 = x.reshape(m, d)

    for l in range(depth):
        mod = _ada(c_pad, w_ada[l], b_ada[l][None, :])[:batch]
        sh1, sc1, gt1, sh2, sc2, gt2 = [mod[:, None, j * d:(j + 1) * d] for j in range(6)]

        w = w_in[l]
        o_ba = 3 * kd + kd
        w_a = w[:, :o_ba].astype(BF16)
        w_b = w[:, o_ba + 2 * N_HEADS:].astype(BF16)
        w_ba = jnp.zeros((d, LANES), F32).at[:, :2 * N_HEADS].set(w[:, o_ba:o_ba + 2 * N_HEADS]).astype(BF16)
        proj, ba = _inproj(x2, norm1_g[l][None, :], sc1, sh1, w_a, w_b, w_ba, seq, _tile(seq, INPROJ_ROWS))

        head_params = jnp.zeros((SUBLANES, LANES), F32)
        head_params = head_params.at[0, N_HEADS:2 * N_HEADS].set(a_log[l])
        head_params = head_params.at[1, N_HEADS:2 * N_HEADS].set(dt_bias[l])
        head_params = head_params.at[2, :].set(onorm_g[l])
        conv_taps = jnp.broadcast_to(conv_qkv_w[l][:, None, :], (QKV_CONV, SUBLANES, 3 * kd))
        og = _gdn(proj, ba, conv_taps, head_params, batch, seq, _tile(seq, GDN_ROWS))

        mod_post = jnp.zeros((batch, SUBLANES, d), F32)
        mod_post = mod_post.at[:, 0:1].set(gt1).at[:, 1:2].set(sc2).at[:, 2:3].set(sh2)
        w_route = jnp.zeros((d, LANES), F32)
        w_route = w_route.at[:, :N_GROUPS].set(w_group[l]).at[:, N_GROUPS:N_GROUPS + N_EXPERTS].set(w_expert[l])
        w_route_hi = w_route.astype(BF16)
        w_route_lo = (w_route - w_route_hi.astype(F32)).astype(BF16)
        b_route = jnp.zeros((1, LANES), F32)
        b_route = b_route.at[0, :N_GROUPS].set(b_group[l]).at[0, N_GROUPS:N_GROUPS + N_EXPERTS].set(b_expert[l])
        x1, h2, route, counts = _post(
            x2, og, proj, mod_post, norm2_g[l][None, :],
            jnp.broadcast_to(conv_sc_w[l][:, None, :], (SC_CONV, SUBLANES, d)),
            w_proj_a[l].astype(BF16), w_proj_b[l].astype(BF16), w_out[l].astype(BF16),
            w_route_hi, w_route_lo, b_route, seq, _tile(seq, POST_ROWS))

        bm = EXPERT_BLOCK
        n_blocks = (2 * m) // bm + N_EXPERTS
        sizes = counts[0, :N_EXPERTS].astype(jnp.int32)
        padded = ((sizes + bm - 1) // bm) * bm
        pad_end = jnp.cumsum(padded)
        block_row = jnp.arange(n_blocks, dtype=jnp.int32) * bm
        block_expert = jnp.minimum(jnp.sum(block_row[:, None] >= pad_end[None, :], axis=1),
                                   N_EXPERTS - 1).astype(jnp.int32)
        block_active = jnp.concatenate([(block_row < pad_end[-1]).astype(jnp.int32),
                                        (pad_end[-1:] // bm).astype(jnp.int32)])
        dest = _plan(route, counts, _tile(seq, INPROJ_ROWS))
        pad_lo = (pad_end - padded + sizes).astype(jnp.int32)
        inv = _invert(pad_lo, pad_end.astype(jnp.int32), dest[0], dest[1], n_blocks * bm,
                      _tile(seq, INPROJ_ROWS))
        ys = _experts(block_expert, block_active, inv, h2, w1[l], w3[l], w2[l])
        nfg = normf_g[None, :]
        x2 = _final(x1, route, gt2, nfg, ys, seq, _tile(seq, MOVE_ROWS))
    return x2.reshape(batch, seq, d)
```

```python
import functools

import jax
import jax.numpy as jnp
from jax import lax
from jax.experimental import pallas as pl
from jax.experimental.pallas import tpu as pltpu

F32 = jnp.float32
BF16 = jnp.bfloat16
HIGHEST = lax.Precision.HIGHEST

N_HEADS = 8
HEAD_D = 128
CHUNK = 64
QKV_CONV = 4
SC_CONV = 3
N_GROUPS = 4
EXPERTS_PER_GROUP = 8
N_EXPERTS = N_GROUPS * EXPERTS_PER_GROUP
EPS = 1e-6

LANES = 128
SUBLANES = 8
BF16_ROWS = 16
EXPERT_BLOCK = 512
INPROJ_ROWS = 2048
GDN_ROWS = 512
PRE_CHUNKS = 2
POST_ROWS = 512
MOVE_ROWS = 1024
DMA_UNROLL = 8
MIB = 1024 * 1024
VMEM_MIB = {"ada": 24, "inproj": 56, "gdn": 56, "post": 56, "plan": 16, "dispatch": 24, "experts": 40,
            "final": 48}


def _sigmoid(x):
    return 0.5 + 0.5 * jnp.tanh(0.5 * x)


def _silu(x):
    half = 0.5 * x
    return half + half * jnp.tanh(half)


def _softplus(x):
    return jnp.maximum(x, 0.0) + jnp.log(1.0 + jnp.exp(-jnp.abs(x)))


def _dot(a, b):
    return jnp.dot(a, b, preferred_element_type=F32)


def _dot_hi(a, b):
    return jnp.dot(a, b, preferred_element_type=F32, precision=HIGHEST)


def _dot_split(a, b_hi, b_lo):
    a_hi = a.astype(BF16)
    a_lo = (a - a_hi.astype(F32)).astype(BF16)
    return _dot(a_hi, b_hi) + (_dot(a_lo, b_hi) + _dot(a_hi, b_lo))


def _read_rows(ref, rows):
    return jnp.concatenate([ref[pl.ds(j, rows, stride=SUBLANES), :] for j in range(SUBLANES)], axis=1)


def _write_rows(ref, value):
    rows = value.shape[0]
    for j in range(SUBLANES):
        ref[pl.ds(j, rows, stride=SUBLANES), :] = value[:, j * LANES:(j + 1) * LANES]


def _row_tile(ref, row):
    if not isinstance(row, int):
        row = pl.multiple_of(row * SUBLANES, SUBLANES)
    else:
        row = row * SUBLANES
    return ref.at[pl.ds(row, SUBLANES), :]


def _params(call, n_grid_axes=1):
    return pltpu.CompilerParams(dimension_semantics=("arbitrary",) * n_grid_axes,
                                vmem_limit_bytes=VMEM_MIB[call] * MIB)


def _ada_kernel(c_ref, w_ref, b_ref, o_ref):
    o_ref[...] = _dot_hi(_silu(c_ref[...]), w_ref[...]) + b_ref[...]


def _ada(c_pad, w_ada, b_ada):
    d = c_pad.shape[1]
    n = w_ada.shape[1]
    return pl.pallas_call(
        _ada_kernel,
        grid=(n // d,),
        in_specs=[
            pl.BlockSpec((SUBLANES, d), lambda j: (0, 0)),
            pl.BlockSpec((d, d), lambda j: (0, j)),
            pl.BlockSpec((1, d), lambda j: (0, j)),
        ],
        out_specs=pl.BlockSpec((SUBLANES, d), lambda j: (0, j)),
        out_shape=jax.ShapeDtypeStruct((SUBLANES, n), F32),
        compiler_params=_params("ada"),
        name="ada",
    )(c_pad, w_ada, b_ada)


def _inproj_kernel(x_ref, g_ref, sc_ref, sh_ref, wa_ref, wb_ref, wba_ref, o_ref, ba_ref, h_ref, *, na):
    j = pl.program_id(1)

    @pl.when(j == 0)
    def _():
        x = x_ref[...]
        y = x * lax.rsqrt(jnp.mean(x * x, axis=-1, keepdims=True) + EPS)
        h = (y * g_ref[...]) * (1.0 + sc_ref[...]) + sh_ref[...]
        hb = h.astype(BF16)
        h_ref[...] = hb
        ba_ref[...] = _dot(hb, wba_ref[...])

    @pl.when(j < na)
    def _():
        o_ref[...] = _dot(h_ref[...], wa_ref[...]).astype(BF16)

    @pl.when(j >= na)
    def _():
        o_ref[...] = _dot(h_ref[...], wb_ref[...]).astype(BF16)


def _inproj(x2, norm_g, sc, sh, w_a, w_b, w_ba, seq, tm):
    m, d = x2.shape
    tn = d
    na, nb = w_a.shape[1] // tn, w_b.shape[1] // tn
    n = (na + nb) * tn
    per_batch = seq // tm
    return pl.pallas_call(
        functools.partial(_inproj_kernel, na=na),
        grid=(m // tm, n // tn),
        in_specs=[
            pl.BlockSpec((tm, d), lambda i, j: (i, 0)),
            pl.BlockSpec((1, d), lambda i, j: (0, 0)),
            pl.BlockSpec((None, 1, d), lambda i, j: (i // per_batch, 0, 0)),
            pl.BlockSpec((None, 1, d), lambda i, j: (i // per_batch, 0, 0)),
            pl.BlockSpec((d, tn), lambda i, j: (0, jnp.minimum(j, na - 1))),
            pl.BlockSpec((d, tn), lambda i, j: (0, jnp.maximum(j - na, 0))),
            pl.BlockSpec((d, LANES), lambda i, j: (0, 0)),
        ],
        out_specs=[
            pl.BlockSpec((tm, tn), lambda i, j: (i, j)),
            pl.BlockSpec((tm, LANES), lambda i, j: (i, 0)),
        ],
        out_shape=[
            jax.ShapeDtypeStruct((m, n), BF16),
            jax.ShapeDtypeStruct((m, LANES), F32),
        ],
        scratch_shapes=[pltpu.VMEM((tm, d), BF16)],
        compiler_params=_params("inproj", 2),
        name="inproj",
    )(x2, norm_g, sc, sh, w_a, w_b, w_ba)


def _bmm(a, b):
    return jnp.einsum("hmk,hkn->hmn", a.astype(BF16), b.astype(BF16), preferred_element_type=F32)


def _bmm_nt(a, b):
    return jnp.einsum("hmk,hnk->hmn", a.astype(BF16), b.astype(BF16), preferred_element_type=F32)


def _unit_lower_inverse(a):
    row = lax.broadcasted_iota(jnp.int32, a.shape[1:], 0)
    col = lax.broadcasted_iota(jnp.int32, a.shape[1:], 1)
    apart = row ^ col
    eye = jnp.where(row == col, 1.0, 0.0).astype(F32)
    t = jnp.where(apart < 2, eye - a, 0.0)
    s = 2
    while s < CHUNK:
        coupling = jnp.where((apart >= s) & (apart < 2 * s), a, 0.0)
        t = t - _bmm(t, _bmm(coupling, t))
        s *= 2
    return t


def _lane_sums(x):
    h, rows, width = x.shape
    ones = jnp.ones((width, width), BF16)
    return _dot(x.reshape(h * rows, width).astype(BF16), ones).reshape(h, rows, width)


def _causal_conv_silu(win, cw, k_w):
    assert k_w == 4
    tiled = (win.shape[0] // SUBLANES, SUBLANES, win.shape[1])

    def pair(x, x1, j):
        return (x.reshape(tiled) * cw[j][None] + x1.reshape(tiled) * cw[j - 1][None]).reshape(win.shape)

    win1 = pltpu.roll(win, 1, 0)
    acc = pair(win, win1, 3) + pltpu.roll(pair(win, win1, 1), 2, 0)
    return _silu(acc[SUBLANES:, :])


def _gdn_kernel(q_ref, k_ref, v_ref, z_ref, ba_ref, cw_ref, hp_ref, o_ref,
                s_ref, tail_ref, wq_ref, u_ref, ik_ref, dec_ref):
    nb, tb = q_ref.shape[0], q_ref.shape[1]
    kd = N_HEADS * HEAD_D
    nbh = nb * N_HEADS

    @pl.when(pl.program_id(0) == 0)
    def _():
        s_ref[...] = jnp.zeros(s_ref.shape, F32)
        tail_ref[...] = jnp.zeros(tail_ref.shape, F32)

    row = lax.broadcasted_iota(jnp.int32, (CHUNK, CHUNK), 0)
    col = lax.broadcasted_iota(jnp.int32, (CHUNK, CHUNK), 1)
    causal = row >= col
    strict = row > col
    tril = jnp.where(causal, 1.0, 0.0).astype(F32)
    a_log = hp_ref[0:1, :]
    dt_bias = hp_ref[1:2, :]
    onorm_g = hp_ref[2:3, :]
    zeros_half = jnp.zeros((CHUNK, HEAD_D), F32)

    def precompute(cp, carry):
        qs, ks, vs, betas, gcs, grs, gls = [], [], [], [], [], [], []
        for sub in range(PRE_CHUNKS):
            base = pl.multiple_of((cp * PRE_CHUNKS + sub) * CHUNK, CHUNK)
            prev = pl.multiple_of(base - BF16_ROWS, BF16_ROWS)
            for b in range(nb):
                ba = ba_ref[b, pl.ds(base, CHUNK), :]
                beta_all = _sigmoid(ba)
                g_all = -jnp.exp(a_log) * _softplus(ba + dt_bias)
                gcum = _dot_hi(tril, g_all)
                gcum_t = jnp.concatenate([gcum, gcum], axis=0).T
                for h in range(N_HEADS):
                    lo, hi = h * HEAD_D, (h + 1) * HEAD_D

                    def conv(ref, off):
                        cur = ref[b, pl.ds(base, CHUNK), lo:hi].astype(F32)
                        if sub == 0:
                            before = tail_ref[b, :, off + lo:off + hi]
                        else:
                            before = ref[b, pl.ds(prev, BF16_ROWS), lo:hi].astype(F32)[SUBLANES:]
                        win = jnp.concatenate([before, cur], axis=0)
                        return _causal_conv_silu(win, cw_ref[:, :, off + lo:off + hi], QKV_CONV)

                    qs.append(conv(q_ref, 0))
                    ks.append(conv(k_ref, kd))
                    vs.append(conv(v_ref, 2 * kd))
                    betas.append(beta_all[:, h:h + 1])
                    gcs.append(gcum[:, N_HEADS + h:N_HEADS + h + 1])
                    grs.append(gcum_t[N_HEADS + h:N_HEADS + h + 1, 0:CHUNK])
                    gls.append(gcum[CHUNK - 1:CHUNK, N_HEADS + h:N_HEADS + h + 1])
        q, k, v = jnp.stack(qs), jnp.stack(ks), jnp.stack(vs)
        beta, gc, gr, gl = jnp.stack(betas), jnp.stack(gcs), jnp.stack(grs), jnp.stack(gls)
        qn = q * (lax.rsqrt(_lane_sums(q * q) + EPS) * (HEAD_D ** -0.5))
        kn = k * lax.rsqrt(_lane_sums(k * k) + EPS)
        decay = jnp.where(causal, jnp.exp(jnp.where(causal, gc - gr, 0.0)), 0.0)
        kb = kn * beta
        e_gc = jnp.exp(gc)
        kq = _bmm_nt(jnp.concatenate([kb, qn], axis=1), kn)
        a = jnp.where(strict, kq[:, :CHUNK] * decay, 0.0)
        intra = kq[:, CHUNK:] * decay
        uw = _bmm(_unit_lower_inverse(a), jnp.concatenate([v * beta, kb * e_gc], axis=2))
        wq = jnp.concatenate([uw[:, :, HEAD_D:], qn * e_gc], axis=1).astype(BF16)
        k_dec = kn * jnp.exp(gl - gc)
        k_dec_t = jnp.stack([jnp.concatenate([k_dec[i], zeros_half], axis=0).T[:, :CHUNK]
                             for i in range(PRE_CHUNKS * nbh)])
        ik = jnp.concatenate([intra, k_dec_t], axis=1).astype(BF16)
        dec = jnp.broadcast_to(jnp.exp(gl), (PRE_CHUNKS * nbh, 1, HEAD_D))
        for sub in range(PRE_CHUNKS):
            c = cp * PRE_CHUNKS + sub
            rows = slice(sub * nbh, (sub + 1) * nbh)
            u_ref[c] = uw[rows, :, :HEAD_D]
            wq_ref[c] = wq[rows]
            ik_ref[c] = ik[rows]
            dec_ref[c] = dec[rows]

        last = pl.multiple_of((cp + 1) * PRE_CHUNKS * CHUNK - BF16_ROWS, BF16_ROWS)
        for b in range(nb):
            for j, ref in enumerate((q_ref, k_ref, v_ref)):
                rows = ref[b, pl.ds(last, BF16_ROWS), :].astype(F32)
                tail_ref[b, :, j * kd:(j + 1) * kd] = rows[BF16_ROWS - SUBLANES:, :]
        return carry

    def recur(c, carry):
        base = pl.multiple_of(c * CHUNK, CHUNK)
        state = s_ref[...]
        ws = _bmm(wq_ref[c], state)
        v_new = u_ref[c] - ws[:, :CHUNK]
        r = _bmm(ik_ref[c], v_new)
        o = ws[:, CHUNK:] + r[:, :CHUNK]
        s_ref[...] = state * dec_ref[c] + r[:, CHUNK:]
        on = o * lax.rsqrt(jnp.mean(o * o, axis=-1, keepdims=True) + EPS) * onorm_g
        for b in range(nb):
            for h in range(N_HEADS):
                lo, hi = h * HEAD_D, (h + 1) * HEAD_D
                z = z_ref[b, pl.ds(base, CHUNK), lo:hi].astype(F32)
                o_ref[b, pl.ds(base, CHUNK), lo:hi] = (on[b * N_HEADS + h] * _silu(z)).astype(BF16)
        return carry

    lax.fori_loop(0, tb // (PRE_CHUNKS * CHUNK), precompute, 0)
    lax.fori_loop(0, tb // CHUNK, recur, 0)


def _gdn(proj, ba, conv_w, head_params, batch, seq, tb):
    kd = N_HEADS * HEAD_D
    nc = tb // CHUNK
    nbh = batch * N_HEADS
    proj3 = proj.reshape(batch, seq, proj.shape[1])
    ba3 = ba.reshape(batch, seq, LANES)

    def col(j):
        return pl.BlockSpec((batch, tb, kd), lambda t: (0, t, j))

    out = pl.pallas_call(
        _gdn_kernel,
        grid=(seq // tb,),
        in_specs=[
            col(0), col(1), col(2), col(3),
            pl.BlockSpec((batch, tb, LANES), lambda t: (0, t, 0)),
            pl.BlockSpec((QKV_CONV, SUBLANES, 3 * kd), lambda t: (0, 0, 0)),
            pl.BlockSpec((SUBLANES, LANES), lambda t: (0, 0)),
        ],
        out_specs=pl.BlockSpec((batch, tb, kd), lambda t: (0, t, 0)),
        out_shape=jax.ShapeDtypeStruct((batch, seq, kd), BF16),
        scratch_shapes=[
            pltpu.VMEM((nbh, HEAD_D, HEAD_D), F32),
            pltpu.VMEM((batch, SUBLANES, 3 * kd), F32),
            pltpu.VMEM((nc, nbh, 2 * CHUNK, HEAD_D), BF16),
            pltpu.VMEM((nc, nbh, CHUNK, HEAD_D), F32),
            pltpu.VMEM((nc, nbh, CHUNK + HEAD_D, CHUNK), BF16),
            pltpu.VMEM((nc, nbh, 1, HEAD_D), F32),
        ],
        compiler_params=_params("gdn"),
        name="gdn",
    )(proj3, proj3, proj3, proj3, ba3, conv_w, head_params)
    return out.reshape(batch * seq, kd)


def _post_kernel(x_ref, og_ref, sb_ref, sc_ref, sx_ref, ga_ref, gb_ref, mod_ref, n2g_ref, cw_ref,
                 wpa_ref, wpb_ref, wout_ref, wrh_ref, wrl_ref, br_ref,
                 x1_ref, h2_ref, route_ref, cnt_ref, win_ref, run_ref, *, per_batch):
    tm = x_ref.shape[0]
    i = pl.program_id(0)

    @pl.when(i == 0)
    def _():
        run_ref[...] = jnp.zeros(run_ref.shape, F32)

    @pl.when(i % per_batch == 0)
    def _():
        win_ref[...] = jnp.zeros(win_ref.shape, F32)

    assert SC_CONV == 3
    prod = sc_ref[...].astype(F32) * sx_ref[...].astype(F32)
    win = jnp.concatenate([win_ref[...], prod], axis=0)
    win_ref[...] = prod[tm - SUBLANES:, :]
    tiled = (win.shape[0] // SUBLANES, SUBLANES, win.shape[1])
    win1 = pltpu.roll(win, 1, 0)
    conv = (win.reshape(tiled) * cw_ref[2][None] + win1.reshape(tiled) * cw_ref[1][None]
            + pltpu.roll(win1, 1, 0).reshape(tiled) * cw_ref[0][None]).reshape(win.shape)[SUBLANES:, :]
    y_b = _dot((sb_ref[...].astype(F32) * conv).astype(BF16), wpb_ref[...])
    y_a = _dot(og_ref[...], wpa_ref[...])
    merged = _sigmoid(ga_ref[...].astype(F32)) * y_a + _sigmoid(gb_ref[...].astype(F32)) * y_b
    mix = _dot(merged.astype(BF16), wout_ref[...])
    x1 = x_ref[...] + mod_ref[0:1, :] * mix
    x1_ref[...] = x1

    y = x1 * lax.rsqrt(jnp.mean(x1 * x1, axis=-1, keepdims=True) + EPS)
    h2 = (y * n2g_ref[...]) * (1.0 + mod_ref[1:2, :]) + mod_ref[2:3, :]
    _write_rows(h2_ref, h2)

    lg = _dot_split(h2, wrh_ref[...], wrl_ref[...]) + br_ref[...]
    lane = lax.broadcasted_iota(jnp.int32, lg.shape, 1).astype(F32)
    neg = jnp.float32(-jnp.inf)
    big = jnp.float32(2 * LANES)

    def first_max(mask):
        vmax = jnp.max(jnp.where(mask, lg, neg), axis=-1, keepdims=True)
        idx = jnp.min(jnp.where(mask & (lg == vmax), lane, big), axis=-1, keepdims=True)
        return vmax, idx

    gmask = lane < N_GROUPS
    g_max, g_sel = first_max(gmask)
    p_group = 1.0 / jnp.sum(jnp.where(gmask, jnp.exp(lg - g_max), 0.0), axis=-1, keepdims=True)
    e_lo = N_GROUPS + EXPERTS_PER_GROUP * g_sel
    emask = (lane >= e_lo) & (lane < e_lo + EXPERTS_PER_GROUP)
    v1, i1 = first_max(emask)
    v2, i2 = first_max(emask & (lane != i1))
    ex = jnp.exp(v2 - v1)
    w1 = p_group * (1.0 / (1.0 + ex))
    w2 = p_group * (ex / (1.0 + ex))
    e1 = i1 - N_GROUPS
    e2 = i2 - N_GROUPS

    onehot = jnp.where((lane == e1) | (lane == e2), 1.0, 0.0).astype(F32)
    row = lax.broadcasted_iota(jnp.int32, (tm, tm), 0)
    col = lax.broadcasted_iota(jnp.int32, (tm, tm), 1)
    before = jnp.where(row > col, 1.0, 0.0).astype(BF16)
    seen = _dot(before, onehot.astype(BF16)) + run_ref[0:1, :]
    r1 = jnp.sum(jnp.where(lane == e1, seen, 0.0), axis=-1, keepdims=True)
    r2 = jnp.sum(jnp.where(lane == e2, seen, 0.0), axis=-1, keepdims=True)
    run_ref[0:1, :] = run_ref[0:1, :] + jnp.sum(onehot, axis=0, keepdims=True)
    cnt_ref[...] = jnp.broadcast_to(run_ref[0:1, :], cnt_ref.shape)

    out = jnp.where(lane == 0, e1, 0.0)
    out = jnp.where(lane == 1, e2, out)
    out = jnp.where(lane == 2, w1, out)
    out = jnp.where(lane == 3, w2, out)
    out = jnp.where(lane == 4, r1, out)
    out = jnp.where(lane == 5, r2, out)
    route_ref[...] = out


def _post(x2, og, proj, mod, n2g, conv_w, wpa, wpb, wout, w_route_hi, w_route_lo, b_route, seq, tm):
    m, d = x2.shape
    assert d == SUBLANES * LANES
    per_batch = seq // tm

    def rows(j):
        return pl.BlockSpec((tm, d), lambda i: (i, j))

    def whole(shape):
        return pl.BlockSpec(shape, lambda i: tuple(0 for _ in shape))

    return pl.pallas_call(
        functools.partial(_post_kernel, per_batch=per_batch),
        grid=(m // tm,),
        in_specs=[
            rows(0), rows(0), rows(4), rows(5), rows(6), rows(7), rows(8),
            pl.BlockSpec((None, SUBLANES, d), lambda i: (i // per_batch, 0, 0)),
            whole((1, d)), whole((SC_CONV, SUBLANES, d)),
            whole((d, d)), whole((d, d)), whole((d, d)),
            whole((d, LANES)), whole((d, LANES)), whole((1, LANES)),
        ],
        out_specs=[
            rows(0), pl.BlockSpec((tm * SUBLANES, LANES), lambda i: (i, 0)),
            pl.BlockSpec((tm, LANES), lambda i: (i, 0)),
            pl.BlockSpec((SUBLANES, LANES), lambda i: (0, 0)),
        ],
        out_shape=[
            jax.ShapeDtypeStruct((m, d), F32),
            jax.ShapeDtypeStruct((m * SUBLANES, LANES), F32),
            jax.ShapeDtypeStruct((m, LANES), F32),
            jax.ShapeDtypeStruct((SUBLANES, LANES), F32),
        ],
        scratch_shapes=[
            pltpu.VMEM((SUBLANES, d), F32),
            pltpu.VMEM((SUBLANES, LANES), F32),
        ],
        compiler_params=_params("post"),
        name="post",
    )(x2, og, proj, proj, proj, proj, proj, mod, n2g, conv_w, wpa, wpb, wout, w_route_hi, w_route_lo,
      b_route)


def _plan_kernel(route_ref, cnt_ref, d_ref):
    bm = EXPERT_BLOCK
    sizes = cnt_ref[...]
    padded = jnp.floor((sizes + (bm - 1.0)) * (1.0 / bm)) * bm
    lane_i = lax.broadcasted_iota(jnp.int32, sizes.shape, 1)
    incl = padded
    s = 1
    while s < LANES:
        incl = incl + jnp.where(lane_i >= s, pltpu.roll(incl, s, 1), 0.0)
        s *= 2
    start = (incl - padded)[0:1, :]
    r = route_ref[...]
    lane = lax.broadcasted_iota(jnp.int32, r.shape, 1).astype(F32)
    d1 = jnp.sum(jnp.where(lane == r[:, 0:1], start, 0.0), axis=-1, keepdims=True) + r[:, 4:5]
    d2 = jnp.sum(jnp.where(lane == r[:, 1:2], start, 0.0), axis=-1, keepdims=True) + r[:, 5:6]
    out = jnp.where(lane == 0.0, d1, jnp.where(lane == 1.0, d2, 0.0))
    d_ref[...] = out.T[0:SUBLANES, :].astype(jnp.int32)


def _plan(route, counts, tm):
    m = route.shape[0]
    return pl.pallas_call(
        _plan_kernel,
        grid=(m // tm,),
        in_specs=[
            pl.BlockSpec((tm, LANES), lambda i: (i, 0)),
            pl.BlockSpec((SUBLANES, LANES), lambda i: (0, 0)),
        ],
        out_specs=pl.BlockSpec((SUBLANES, tm), lambda i: (0, i)),
        out_shape=jax.ShapeDtypeStruct((SUBLANES, m), jnp.int32),
        compiler_params=_params("plan"),
        name="plan",
    )(route, counts)


def _row_copy(src_ref, src_row, dst_ref, dst_row, sem):
    return pltpu.make_async_copy(_row_tile(src_ref, src_row), _row_tile(dst_ref, dst_row), sem)


def _rows_copy(src_ref, dst_ref, dst_row, n, sem):
    return pltpu.make_async_copy(src_ref, dst_ref.at[pl.ds(dst_row * SUBLANES, n * SUBLANES), :], sem)


def _dispatch_kernel(d1_ref, d2_ref, tail_ref, h2_ref, xs_ref, zero_ref, sem):
    tm = h2_ref.shape[0] // SUBLANES
    bm = zero_ref.shape[0] // SUBLANES
    t0 = pl.program_id(0) * tm

    @pl.when(pl.program_id(0) == 0)
    def _():
        zero_ref[...] = jnp.zeros(zero_ref.shape, F32)
        n_blocks = xs_ref.shape[0] // (bm * SUBLANES)

        def zero_block(row):
            return _rows_copy(zero_ref, xs_ref, pl.multiple_of(row, bm), bm, sem.at[0])

        for e in range(N_EXPERTS):
            @pl.when(tail_ref[e] >= 0)
            def _():
                zero_block(tail_ref[e]).start()
        lax.fori_loop(tail_ref[N_EXPERTS], n_blocks, lambda j, c: (zero_block(j * bm).start(), c)[1], 0)
        for e in range(N_EXPERTS):
            @pl.when(tail_ref[e] >= 0)
            def _():
                zero_block(tail_ref[e]).wait()
        lax.fori_loop(tail_ref[N_EXPERTS], n_blocks, lambda j, c: (zero_block(j * bm).wait(), c)[1], 0)

    def start(r, carry):
        _row_copy(h2_ref, r, xs_ref, d1_ref[t0 + r], sem.at[0]).start(priority=0)
        _row_copy(h2_ref, r, xs_ref, d2_ref[t0 + r], sem.at[1]).start(priority=1)
        return carry

    lax.fori_loop(0, tm, start, 0, unroll=DMA_UNROLL)
    _rows_copy(h2_ref, xs_ref, 0, tm, sem.at[0]).wait()
    _rows_copy(h2_ref, xs_ref, 0, tm, sem.at[1]).wait()


def _dispatch(d1, d2, tail_start, h2, cap, tm):
    m = h2.shape[0] // SUBLANES
    return pl.pallas_call(
        _dispatch_kernel,
        grid_spec=pltpu.PrefetchScalarGridSpec(
            num_scalar_prefetch=3,
            grid=(m // tm,),
            in_specs=[pl.BlockSpec((tm * SUBLANES, LANES), lambda i, d1, d2, tl: (i, 0))],
            out_specs=pl.BlockSpec(memory_space=pl.ANY),
            scratch_shapes=[pltpu.VMEM((EXPERT_BLOCK * SUBLANES, LANES), F32),
                            pltpu.SemaphoreType.DMA((2,))],
        ),
        out_shape=jax.ShapeDtypeStruct((cap * SUBLANES, LANES), F32),
        compiler_params=_params("dispatch"),
        name="dispatch",
    )(d1, d2, tail_start, h2)


def _expert_kernel(be_ref, act_ref, x_ref, w1_ref, w3_ref, w2_ref, y_ref, w1b_ref, w3b_ref, w2b_ref):
    b = pl.program_id(0)

    @pl.when(act_ref[b] > 0)
    def _():
        @pl.when((b == 0) | (be_ref[b] != be_ref[jnp.maximum(b - 1, 0)]))
        def _():
            w1b_ref[...] = w1_ref[...].astype(BF16)
            w3b_ref[...] = w3_ref[...].astype(BF16)
            w2b_ref[...] = w2_ref[...].astype(BF16)

        xb = _read_rows(x_ref, x_ref.shape[0] // SUBLANES).astype(BF16)
        hid = _silu(_dot(xb, w1b_ref[...])) * _dot(xb, w3b_ref[...])
        _write_rows(y_ref, _dot(hid.astype(BF16), w2b_ref[...]))

    @pl.when(act_ref[b] == 0)
    def _():
        y_ref[...] = jnp.zeros(y_ref.shape, F32)


def _experts(block_expert, block_active, xs, w1, w3, w2):
    cap = xs.shape[0] // SUBLANES
    d, de = w1.shape[1], w1.shape[2]
    bm = EXPERT_BLOCK

    def x_block(b, be, act):
        return (jnp.minimum(b, jnp.maximum(act[cap // bm], 1) - 1), 0)

    return pl.pallas_call(
        _expert_kernel,
        grid_spec=pltpu.PrefetchScalarGridSpec(
            num_scalar_prefetch=2,
            grid=(cap // bm,),
            in_specs=[
                pl.BlockSpec((bm * SUBLANES, LANES), x_block),
                pl.BlockSpec((None, d, de), lambda b, be, act: (be[b], 0, 0)),
                pl.BlockSpec((None, d, de), lambda b, be, act: (be[b], 0, 0)),
                pl.BlockSpec((None, de, d), lambda b, be, act: (be[b], 0, 0)),
            ],
            out_specs=pl.BlockSpec((bm * SUBLANES, LANES), lambda b, be, act: (b, 0)),
            scratch_shapes=[pltpu.VMEM((d, de), BF16), pltpu.VMEM((d, de), BF16),
                            pltpu.VMEM((de, d), BF16)],
        ),
        out_shape=jax.ShapeDtypeStruct((cap * SUBLANES, LANES), F32),
        compiler_params=_params("experts"),
        name="experts",
    )(block_expert, block_active, xs, w1, w3, w2)


def _final_kernel(d1_ref, d2_ref, x1_ref, route_ref, gt_ref, nfg_ref, ys_ref, o_ref, buf_ref, sem):
    tm = x1_ref.shape[0]
    i = pl.program_id(0)

    def gather(tile, s):
        t0 = tile * tm

        def start(r, carry):
            _row_copy(ys_ref, d1_ref[t0 + r], buf_ref.at[s, 0], r, sem.at[s, 0]).start(priority=0)
            _row_copy(ys_ref, d2_ref[t0 + r], buf_ref.at[s, 1], r, sem.at[s, 1]).start(priority=1)
            return carry

        lax.fori_loop(0, tm, start, 0, unroll=DMA_UNROLL)

    def wait_rows(s):
        for k in range(2):
            pltpu.make_async_copy(ys_ref.at[pl.ds(0, tm * SUBLANES), :], buf_ref.at[s, k],
                                  sem.at[s, k]).wait()

    def combine(cur, nxt):
        wait_rows(cur)
        last = pl.num_programs(0) - 1
        t0 = jnp.minimum(i + 1, last) * tm
        for r in range(tm):
            _row_copy(ys_ref, d1_ref[t0 + r], buf_ref.at[nxt, 0], r, sem.at[nxt, 0]).start(priority=0)
            _row_copy(ys_ref, d2_ref[t0 + r], buf_ref.at[nxt, 1], r, sem.at[nxt, 1]).start(priority=1)
        route = route_ref[...]
        moe = (_read_rows(buf_ref.at[cur, 0], tm) * route[:, 2:3]
               + _read_rows(buf_ref.at[cur, 1], tm) * route[:, 3:4])
        x2 = x1_ref[...] + gt_ref[...] * moe
        y = x2 * lax.rsqrt(jnp.mean(x2 * x2, axis=-1, keepdims=True) + EPS)
        o_ref[...] = y * nfg_ref[...]

        @pl.when(i == last)
        def _():
            wait_rows(nxt)

    @pl.when(i == 0)
    def _():
        gather(0, 0)

    @pl.when(i % 2 == 0)
    def _():
        combine(0, 1)

    @pl.when(i % 2 == 1)
    def _():
        combine(1, 0)


def _final(d1, d2, x1, route, gt2, nfg, ys, seq, tm):
    m, d = x1.shape
    per_batch = seq // tm
    return pl.pallas_call(
        _final_kernel,
        grid_spec=pltpu.PrefetchScalarGridSpec(
            num_scalar_prefetch=2,
            grid=(m // tm,),
            in_specs=[
                pl.BlockSpec((tm, d), lambda i, d1, d2: (i, 0)),
                pl.BlockSpec((tm, LANES), lambda i, d1, d2: (i, 0)),
                pl.BlockSpec((None, 1, d), lambda i, d1, d2: (i // per_batch, 0, 0)),
                pl.BlockSpec((1, d), lambda i, d1, d2: (0, 0)),
                pl.BlockSpec(memory_space=pl.ANY),
            ],
            out_specs=pl.BlockSpec((tm, d), lambda i, d1, d2: (i, 0)),
            scratch_shapes=[pltpu.VMEM((2, 2, tm * SUBLANES, LANES), F32),
                            pltpu.SemaphoreType.DMA((2, 2))],
        ),
        out_shape=jax.ShapeDtypeStruct((m, d), F32),
        compiler_params=_params("final"),
        name="final",
    )(d1, d2, x1, route, gt2, nfg, ys)


def _tile(n, pref):
    t = min(n, pref)
    assert n % t == 0
    return t


def kernel(x, c, w_ada, b_ada, norm1_g, w_in, conv_qkv_w, a_log, dt_bias, onorm_g, w_proj_a,
           conv_sc_w, w_proj_b, w_out, norm2_g, w_group, b_group, w_expert, b_expert, w1, w3, w2,
           normf_g):
    batch, seq, d = x.shape
    depth = w_ada.shape[0]
    m = batch * seq
    kd = N_HEADS * HEAD_D
    assert d == kd and seq % CHUNK == 0 and batch <= SUBLANES
    assert depth == 1, "the last stage applies the final rmsnorm: one layer only"
    c_pad = jnp.zeros((SUBLANES, d), F32).at[:batch].set(c)
    x2 = x.reshape(m, d)

    for l in range(depth):
        mod = _ada(c_pad, w_ada[l], b_ada[l][None, :])[:batch]
        sh1, sc1, gt1, sh2, sc2, gt2 = [mod[:, None, j * d:(j + 1) * d] for j in range(6)]

        w = w_in[l]
        o_ba = 3 * kd + kd
        w_a = w[:, :o_ba].astype(BF16)
        w_b = w[:, o_ba + 2 * N_HEADS:].astype(BF16)
        w_ba = jnp.zeros((d, LANES), F32).at[:, :2 * N_HEADS].set(w[:, o_ba:o_ba + 2 * N_HEADS]).astype(BF16)
        proj, ba = _inproj(x2, norm1_g[l][None, :], sc1, sh1, w_a, w_b, w_ba, seq, _tile(seq, INPROJ_ROWS))

        head_params = jnp.zeros((SUBLANES, LANES), F32)
        head_params = head_params.at[0, N_HEADS:2 * N_HEADS].set(a_log[l])
        head_params = head_params.at[1, N_HEADS:2 * N_HEADS].set(dt_bias[l])
        head_params = head_params.at[2, :].set(onorm_g[l])
        conv_taps = jnp.broadcast_to(conv_qkv_w[l][:, None, :], (QKV_CONV, SUBLANES, 3 * kd))
        og = _gdn(proj, ba, conv_taps, head_params, batch, seq, _tile(seq, GDN_ROWS))

        mod_post = jnp.zeros((batch, SUBLANES, d), F32)
        mod_post = mod_post.at[:, 0:1].set(gt1).at[:, 1:2].set(sc2).at[:, 2:3].set(sh2)
        w_route = jnp.zeros((d, LANES), F32)
        w_route = w_route.at[:, :N_GROUPS].set(w_group[l]).at[:, N_GROUPS:N_GROUPS + N_EXPERTS].set(w_expert[l])
        w_route_hi = w_route.astype(BF16)
        w_route_lo = (w_route - w_route_hi.astype(F32)).astype(BF16)
        b_route = jnp.zeros((1, LANES), F32)
        b_route = b_route.at[0, :N_GROUPS].set(b_group[l]).at[0, N_GROUPS:N_GROUPS + N_EXPERTS].set(b_expert[l])
        x1, h2, route, counts = _post(
            x2, og, proj, mod_post, norm2_g[l][None, :],
            jnp.broadcast_to(conv_sc_w[l][:, None, :], (SC_CONV, SUBLANES, d)),
            w_proj_a[l].astype(BF16), w_proj_b[l].astype(BF16), w_out[l].astype(BF16),
            w_route_hi, w_route_lo, b_route, seq, _tile(seq, POST_ROWS))

        bm = EXPERT_BLOCK
        n_blocks = (2 * m) // bm + N_EXPERTS
        sizes = counts[0, :N_EXPERTS].astype(jnp.int32)
        padded = ((sizes + bm - 1) // bm) * bm
        pad_end = jnp.cumsum(padded)
        tail_start = jnp.concatenate([jnp.where(padded > 0, pad_end - bm, -1),
                                      pad_end[-1:] // bm]).astype(jnp.int32)
        block_row = jnp.arange(n_blocks, dtype=jnp.int32) * bm
        block_expert = jnp.minimum(jnp.sum(block_row[:, None] >= pad_end[None, :], axis=1),
                                   N_EXPERTS - 1).astype(jnp.int32)
        block_active = jnp.concatenate([(block_row < pad_end[-1]).astype(jnp.int32),
                                        (pad_end[-1:] // bm).astype(jnp.int32)])
        dest = _plan(route, counts, _tile(seq, INPROJ_ROWS))
        d1, d2 = dest[0], dest[1]

        xs = _dispatch(d1, d2, tail_start, h2, n_blocks * bm, _tile(seq, MOVE_ROWS))
        ys = _experts(block_expert, block_active, xs, w1[l], w3[l], w2[l])
        nfg = normf_g[None, :]
        x2 = _final(d1, d2, x1, route, gt2, nfg, ys, seq, _tile(seq, MOVE_ROWS))
    return x2.reshape(batch, seq, d)
```

```python
import functools

import jax
import jax.numpy as jnp
from jax import lax
from jax.experimental import pallas as pl
from jax.experimental.pallas import tpu as pltpu

F32 = jnp.float32
BF16 = jnp.bfloat16
HIGHEST = lax.Precision.HIGHEST

N_HEADS = 8
HEAD_D = 128
CHUNK = 64
QKV_CONV = 4
SC_CONV = 3
N_GROUPS = 4
EXPERTS_PER_GROUP = 8
N_EXPERTS = N_GROUPS * EXPERTS_PER_GROUP
EPS = 1e-6

LANES = 128
SUBLANES = 8
BF16_ROWS = 16
EXPERT_BLOCK = 512
INPROJ_ROWS = 2048
GDN_ROWS = 256
PRE_CHUNKS = 2
POST_ROWS = 512
DISPATCH_ROWS = 1024
FINAL_ROWS = 512
DMA_UNROLL = 8
MIB = 1024 * 1024
VMEM_MIB = {"ada": 24, "inproj": 56, "gdn": 48, "post": 56, "plan": 16, "dispatch": 24, "experts": 40,
            "final": 32}


def _sigmoid(x):
    return 0.5 + 0.5 * jnp.tanh(0.5 * x)


def _silu(x):
    half = 0.5 * x
    return half + half * jnp.tanh(half)


def _softplus(x):
    return jnp.maximum(x, 0.0) + jnp.log(1.0 + jnp.exp(-jnp.abs(x)))


def _dot(a, b):
    return jnp.dot(a, b, preferred_element_type=F32)


def _dot_hi(a, b):
    return jnp.dot(a, b, preferred_element_type=F32, precision=HIGHEST)


def _dot_split(a, b_hi, b_lo):
    a_hi = a.astype(BF16)
    a_lo = (a - a_hi.astype(F32)).astype(BF16)
    return _dot(a_hi, b_hi) + (_dot(a_lo, b_hi) + _dot(a_hi, b_lo))


def _read_rows(ref, rows):
    return jnp.concatenate([ref[pl.ds(j, rows, stride=SUBLANES), :] for j in range(SUBLANES)], axis=1)


def _write_rows(ref, value):
    rows = value.shape[0]
    for j in range(SUBLANES):
        ref[pl.ds(j, rows, stride=SUBLANES), :] = value[:, j * LANES:(j + 1) * LANES]


def _row_tile(ref, row):
    if not isinstance(row, int):
        row = pl.multiple_of(row * SUBLANES, SUBLANES)
    else:
        row = row * SUBLANES
    return ref.at[pl.ds(row, SUBLANES), :]


def _params(call, n_grid_axes=1):
    return pltpu.CompilerParams(dimension_semantics=("arbitrary",) * n_grid_axes,
                                vmem_limit_bytes=VMEM_MIB[call] * MIB)


def _ada_kernel(c_ref, w_ref, b_ref, o_ref):
    o_ref[...] = _dot_hi(_silu(c_ref[...]), w_ref[...]) + b_ref[...]


def _ada(c_pad, w_ada, b_ada):
    d = c_pad.shape[1]
    n = w_ada.shape[1]
    return pl.pallas_call(
        _ada_kernel,
        grid=(n // d,),
        in_specs=[
            pl.BlockSpec((SUBLANES, d), lambda j: (0, 0)),
            pl.BlockSpec((d, d), lambda j: (0, j)),
            pl.BlockSpec((1, d), lambda j: (0, j)),
        ],
        out_specs=pl.BlockSpec((SUBLANES, d), lambda j: (0, j)),
        out_shape=jax.ShapeDtypeStruct((SUBLANES, n), F32),
        compiler_params=_params("ada"),
        name="ada",
    )(c_pad, w_ada, b_ada)


def _inproj_kernel(x_ref, g_ref, sc_ref, sh_ref, wa_ref, wb_ref, wba_ref, o_ref, ba_ref, h_ref, wa_bf_ref,
                   *, na):
    i, j = pl.program_id(0), pl.program_id(1)

    @pl.when(jnp.logical_and(i == 0, j < na))
    def _():
        wa_bf_ref[j] = wa_ref[...].astype(BF16)

    @pl.when(j == 0)
    def _():
        x = x_ref[...]
        y = x * lax.rsqrt(jnp.mean(x * x, axis=-1, keepdims=True) + EPS)
        h = (y * g_ref[...]) * (1.0 + sc_ref[...]) + sh_ref[...]
        hb = h.astype(BF16)
        h_ref[...] = hb
        ba_ref[...] = _dot(hb, wba_ref[...])

    @pl.when(j < na)
    def _():
        o_ref[...] = _dot(h_ref[...], wa_bf_ref[j]).astype(BF16)

    @pl.when(j >= na)
    def _():
        o_ref[...] = _dot(h_ref[...], wb_ref[...]).astype(BF16)


def _inproj(x2, norm_g, sc, sh, w, na, w_b, w_ba, seq, tm):
    m, d = x2.shape
    tn = d
    nb = w_b.shape[1] // tn
    n = (na + nb) * tn
    per_batch = seq // tm
    return pl.pallas_call(
        functools.partial(_inproj_kernel, na=na),
        grid=(m // tm, n // tn),
        in_specs=[
            pl.BlockSpec((tm, d), lambda i, j: (i, 0)),
            pl.BlockSpec((1, d), lambda i, j: (0, 0)),
            pl.BlockSpec((None, 1, d), lambda i, j: (i // per_batch, 0, 0)),
            pl.BlockSpec((None, 1, d), lambda i, j: (i // per_batch, 0, 0)),
            pl.BlockSpec((d, tn), lambda i, j: (0, jnp.where(i == 0, jnp.minimum(j, na - 1), na - 1)),
                         pipeline_mode=pl.Buffered(1)),
            pl.BlockSpec((d, tn), lambda i, j: (0, jnp.maximum(j - na, 0))),
            pl.BlockSpec((d, LANES), lambda i, j: (0, 0)),
        ],
        out_specs=[
            pl.BlockSpec((tm, tn), lambda i, j: (i, j)),
            pl.BlockSpec((tm, LANES), lambda i, j: (i, 0)),
        ],
        out_shape=[
            jax.ShapeDtypeStruct((m, n), BF16),
            jax.ShapeDtypeStruct((m, LANES), F32),
        ],
        scratch_shapes=[pltpu.VMEM((tm, d), BF16), pltpu.VMEM((na, d, tn), BF16)],
        compiler_params=_params("inproj", 2),
        name="inproj",
    )(x2, norm_g, sc, sh, w, w_b, w_ba)


def _bmm(a, b):
    return jnp.einsum("hmk,hkn->hmn", a.astype(BF16), b.astype(BF16), preferred_element_type=F32)


def _bmm_nt(a, b):
    return jnp.einsum("hmk,hnk->hmn", a.astype(BF16), b.astype(BF16), preferred_element_type=F32)


def _unit_lower_inverse(a):
    row = lax.broadcasted_iota(jnp.int32, a.shape[1:], 0)
    col = lax.broadcasted_iota(jnp.int32, a.shape[1:], 1)
    apart = row ^ col
    eye = jnp.where(row == col, 1.0, 0.0).astype(F32)
    t = jnp.where(apart < 2, eye - a, 0.0)
    s = 2
    while s < CHUNK:
        coupling = jnp.where((apart >= s) & (apart < 2 * s), a, 0.0)
        t = t - _bmm(t, _bmm(coupling, t))
        s *= 2
    return t


def _lane_sums(x):
    h, rows, width = x.shape
    ones = jnp.ones((width, width), BF16)
    return _dot(x.reshape(h * rows, width).astype(BF16), ones).reshape(h, rows, width)


def _causal_conv_silu(win, cw, k_w):
    assert k_w == 4
    tiled = (win.shape[0] // SUBLANES, SUBLANES, win.shape[1])

    def pair(x, x1, j):
        return (x.reshape(tiled) * cw[j][None] + x1.reshape(tiled) * cw[j - 1][None]).reshape(win.shape)

    win1 = pltpu.roll(win, 1, 0)
    acc = pair(win, win1, 3) + pltpu.roll(pair(win, win1, 1), 2, 0)
    return _silu(acc[SUBLANES:, :])


def _gdn_kernel(q_ref, k_ref, v_ref, z_ref, ba_ref, cw_ref, hp_ref, o_ref,
                s_ref, tail_ref, wq_ref, u_ref, ik_ref, dec_ref):
    nb, tb = q_ref.shape[0], q_ref.shape[1]
    kd = N_HEADS * HEAD_D
    nbh = nb * N_HEADS

    @pl.when(pl.program_id(0) == 0)
    def _():
        s_ref[...] = jnp.zeros(s_ref.shape, F32)
        tail_ref[...] = jnp.zeros(tail_ref.shape, F32)

    row = lax.broadcasted_iota(jnp.int32, (CHUNK, CHUNK), 0)
    col = lax.broadcasted_iota(jnp.int32, (CHUNK, CHUNK), 1)
    causal = row >= col
    strict = row > col
    tril = jnp.where(causal, 1.0, 0.0).astype(F32)
    a_log = hp_ref[0:1, :]
    dt_bias = hp_ref[1:2, :]
    onorm_g = hp_ref[2:3, :]
    zeros_half = jnp.zeros((CHUNK, HEAD_D), F32)

    def precompute(cp, carry):
        qs, ks, vs, betas, gcs, grs, gls = [], [], [], [], [], [], []
        for sub in range(PRE_CHUNKS):
            base = pl.multiple_of((cp * PRE_CHUNKS + sub) * CHUNK, CHUNK)
            prev = pl.multiple_of(base - BF16_ROWS, BF16_ROWS)
            for b in range(nb):
                ba = ba_ref[b, pl.ds(base, CHUNK), :]
                beta_all = _sigmoid(ba)
                g_all = -jnp.exp(a_log) * _softplus(ba + dt_bias)
                gcum = _dot_hi(tril, g_all)
                gcum_t = jnp.concatenate([gcum, gcum], axis=0).T
                for h in range(N_HEADS):
                    lo, hi = h * HEAD_D, (h + 1) * HEAD_D

                    def conv(ref, off):
                        cur = ref[b, pl.ds(base, CHUNK), lo:hi].astype(F32)
                        if sub == 0:
                            before = tail_ref[b, :, off + lo:off + hi]
                        else:
                            before = ref[b, pl.ds(prev, BF16_ROWS), lo:hi].astype(F32)[SUBLANES:]
                        win = jnp.concatenate([before, cur], axis=0)
                        return _causal_conv_silu(win, cw_ref[:, :, off + lo:off + hi], QKV_CONV)

                    qs.append(conv(q_ref, 0))
                    ks.append(conv(k_ref, kd))
                    vs.append(conv(v_ref, 2 * kd))
                    betas.append(beta_all[:, h:h + 1])
                    gcs.append(gcum[:, N_HEADS + h:N_HEADS + h + 1])
                    grs.append(gcum_t[N_HEADS + h:N_HEADS + h + 1, 0:CHUNK])
                    gls.append(gcum[CHUNK - 1:CHUNK, N_HEADS + h:N_HEADS + h + 1])
        q, k, v = jnp.stack(qs), jnp.stack(ks), jnp.stack(vs)
        beta, gc, gr, gl = jnp.stack(betas), jnp.stack(gcs), jnp.stack(grs), jnp.stack(gls)
        qn = q * (lax.rsqrt(_lane_sums(q * q) + EPS) * (HEAD_D ** -0.5))
        kn = k * lax.rsqrt(_lane_sums(k * k) + EPS)
        decay = jnp.where(causal, jnp.exp(jnp.where(causal, gc - gr, 0.0)), 0.0)
        kb = kn * beta
        e_gc = jnp.exp(gc)
        kq = _bmm_nt(jnp.concatenate([kb, qn], axis=1), kn)
        a = jnp.where(strict, kq[:, :CHUNK] * decay, 0.0)
        intra = kq[:, CHUNK:] * decay
        uw = _bmm(_unit_lower_inverse(a), jnp.concatenate([v * beta, kb * e_gc], axis=2))
        wq = jnp.concatenate([uw[:, :, HEAD_D:], qn * e_gc], axis=1).astype(BF16)
        k_dec = kn * jnp.exp(gl - gc)
        k_dec_t = jnp.stack([jnp.concatenate([k_dec[i], zeros_half], axis=0).T[:, :CHUNK]
                             for i in range(PRE_CHUNKS * nbh)])
        ik = jnp.concatenate([intra, k_dec_t], axis=1).astype(BF16)
        dec = jnp.broadcast_to(jnp.exp(gl), (PRE_CHUNKS * nbh, 1, HEAD_D))
        for sub in range(PRE_CHUNKS):
            c = cp * PRE_CHUNKS + sub
            rows = slice(sub * nbh, (sub + 1) * nbh)
            u_ref[c] = uw[rows, :, :HEAD_D]
            wq_ref[c] = wq[rows]
            ik_ref[c] = ik[rows]
            dec_ref[c] = dec[rows]

        last = pl.multiple_of((cp + 1) * PRE_CHUNKS * CHUNK - BF16_ROWS, BF16_ROWS)
        for b in range(nb):
            for j, ref in enumerate((q_ref, k_ref, v_ref)):
                rows = ref[b, pl.ds(last, BF16_ROWS), :].astype(F32)
                tail_ref[b, :, j * kd:(j + 1) * kd] = rows[BF16_ROWS - SUBLANES:, :]
        return carry

    def recur(c, carry):
        base = pl.multiple_of(c * CHUNK, CHUNK)
        state = s_ref[...]
        ws = _bmm(wq_ref[c], state)
        v_new = u_ref[c] - ws[:, :CHUNK]
        r = _bmm(ik_ref[c], v_new)
        o = ws[:, CHUNK:] + r[:, :CHUNK]
        s_ref[...] = state * dec_ref[c] + r[:, CHUNK:]
        on = o * lax.rsqrt(jnp.mean(o * o, axis=-1, keepdims=True) + EPS) * onorm_g
        for b in range(nb):
            for h in range(N_HEADS):
                lo, hi = h * HEAD_D, (h + 1) * HEAD_D
                z = z_ref[b, pl.ds(base, CHUNK), lo:hi].astype(F32)
                o_ref[b, pl.ds(base, CHUNK), lo:hi] = (on[b * N_HEADS + h] * _silu(z)).astype(BF16)
        return carry

    lax.fori_loop(0, tb // (PRE_CHUNKS * CHUNK), precompute, 0)
    lax.fori_loop(0, tb // CHUNK, recur, 0)


def _gdn(proj, ba, conv_w, head_params, batch, seq, tb):
    kd = N_HEADS * HEAD_D
    nc = tb // CHUNK
    nbh = batch * N_HEADS
    proj3 = proj.reshape(batch, seq, proj.shape[1])
    ba3 = ba.reshape(batch, seq, LANES)

    def col(j):
        return pl.BlockSpec((batch, tb, kd), lambda t: (0, t, j))

    out = pl.pallas_call(
        _gdn_kernel,
        grid=(seq // tb,),
        in_specs=[
            col(0), col(1), col(2), col(3),
            pl.BlockSpec((batch, tb, LANES), lambda t: (0, t, 0)),
            pl.BlockSpec((QKV_CONV, SUBLANES, 3 * kd), lambda t: (0, 0, 0)),
            pl.BlockSpec((SUBLANES, LANES), lambda t: (0, 0)),
        ],
        out_specs=pl.BlockSpec((batch, tb, kd), lambda t: (0, t, 0)),
        out_shape=jax.ShapeDtypeStruct((batch, seq, kd), BF16),
        scratch_shapes=[
            pltpu.VMEM((nbh, HEAD_D, HEAD_D), F32),
            pltpu.VMEM((batch, SUBLANES, 3 * kd), F32),
            pltpu.VMEM((nc, nbh, 2 * CHUNK, HEAD_D), BF16),
            pltpu.VMEM((nc, nbh, CHUNK, HEAD_D), F32),
            pltpu.VMEM((nc, nbh, CHUNK + HEAD_D, CHUNK), BF16),
            pltpu.VMEM((nc, nbh, 1, HEAD_D), F32),
        ],
        compiler_params=_params("gdn"),
        name="gdn",
    )(proj3, proj3, proj3, proj3, ba3, conv_w, head_params)
    return out.reshape(batch * seq, kd)


def _post_kernel(x_ref, og_ref, sb_ref, sc_ref, sx_ref, ga_ref, gb_ref, mod_ref, n2g_ref, cw_ref,
                 wpa_ref, wpb_ref, wout_ref, wrh_ref, wrl_ref, br_ref,
                 x1_ref, h2_ref, route_ref, cnt_ref, win_ref, run_ref, *, per_batch):
    tm = x_ref.shape[0]
    i = pl.program_id(0)

    @pl.when(i == 0)
    def _():
        run_ref[...] = jnp.zeros(run_ref.shape, F32)

    @pl.when(i % per_batch == 0)
    def _():
        win_ref[...] = jnp.zeros(win_ref.shape, F32)

    assert SC_CONV == 3
    prod = sc_ref[...].astype(F32) * sx_ref[...].astype(F32)
    win = jnp.concatenate([win_ref[...], prod], axis=0)
    win_ref[...] = prod[tm - SUBLANES:, :]
    tiled = (win.shape[0] // SUBLANES, SUBLANES, win.shape[1])
    win1 = pltpu.roll(win, 1, 0)
    conv = (win.reshape(tiled) * cw_ref[2][None] + win1.reshape(tiled) * cw_ref[1][None]
            + pltpu.roll(win1, 1, 0).reshape(tiled) * cw_ref[0][None]).reshape(win.shape)[SUBLANES:, :]
    y_b = _dot((sb_ref[...].astype(F32) * conv).astype(BF16), wpb_ref[...])
    y_a = _dot(og_ref[...], wpa_ref[...])
    merged = _sigmoid(ga_ref[...].astype(F32)) * y_a + _sigmoid(gb_ref[...].astype(F32)) * y_b
    mix = _dot(merged.astype(BF16), wout_ref[...])
    x1 = x_ref[...] + mod_ref[0:1, :] * mix
    x1_ref[...] = x1

    y = x1 * lax.rsqrt(jnp.mean(x1 * x1, axis=-1, keepdims=True) + EPS)
    h2 = (y * n2g_ref[...]) * (1.0 + mod_ref[1:2, :]) + mod_ref[2:3, :]
    _write_rows(h2_ref, h2)

    lg = _dot_split(h2, wrh_ref[...], wrl_ref[...]) + br_ref[...]
    lane = lax.broadcasted_iota(jnp.int32, lg.shape, 1).astype(F32)
    neg = jnp.float32(-jnp.inf)
    big = jnp.float32(2 * LANES)

    def first_max(mask):
        vmax = jnp.max(jnp.where(mask, lg, neg), axis=-1, keepdims=True)
        idx = jnp.min(jnp.where(mask & (lg == vmax), lane, big), axis=-1, keepdims=True)
        return vmax, idx

    gmask = lane < N_GROUPS
    g_max, g_sel = first_max(gmask)
    p_group = 1.0 / jnp.sum(jnp.where(gmask, jnp.exp(lg - g_max), 0.0), axis=-1, keepdims=True)
    e_lo = N_GROUPS + EXPERTS_PER_GROUP * g_sel
    emask = (lane >= e_lo) & (lane < e_lo + EXPERTS_PER_GROUP)
    v1, i1 = first_max(emask)
    v2, i2 = first_max(emask & (lane != i1))
    ex = jnp.exp(v2 - v1)
    w1 = p_group * (1.0 / (1.0 + ex))
    w2 = p_group * (ex / (1.0 + ex))
    e1 = i1 - N_GROUPS
    e2 = i2 - N_GROUPS

    onehot = jnp.where((lane == e1) | (lane == e2), 1.0, 0.0).astype(F32)
    row = lax.broadcasted_iota(jnp.int32, (tm, tm), 0)
    col = lax.broadcasted_iota(jnp.int32, (tm, tm), 1)
    before = jnp.where(row > col, 1.0, 0.0).astype(BF16)
    seen = _dot(before, onehot.astype(BF16)) + run_ref[0:1, :]
    r1 = jnp.sum(jnp.where(lane == e1, seen, 0.0), axis=-1, keepdims=True)
    r2 = jnp.sum(jnp.where(lane == e2, seen, 0.0), axis=-1, keepdims=True)
    run_ref[0:1, :] = run_ref[0:1, :] + jnp.sum(onehot, axis=0, keepdims=True)
    cnt_ref[...] = jnp.broadcast_to(run_ref[0:1, :], cnt_ref.shape)

    out = jnp.where(lane == 0, e1, 0.0)
    out = jnp.where(lane == 1, e2, out)
    out = jnp.where(lane == 2, w1, out)
    out = jnp.where(lane == 3, w2, out)
    out = jnp.where(lane == 4, r1, out)
    out = jnp.where(lane == 5, r2, out)
    route_ref[...] = out


def _post(x2, og, proj, mod, n2g, conv_w, wpa, wpb, wout, w_route_hi, w_route_lo, b_route, seq, tm):
    m, d = x2.shape
    assert d == SUBLANES * LANES
    per_batch = seq // tm

    def rows(j):
        return pl.BlockSpec((tm, d), lambda i: (i, j))

    def whole(shape):
        return pl.BlockSpec(shape, lambda i: tuple(0 for _ in shape))

    return pl.pallas_call(
        functools.partial(_post_kernel, per_batch=per_batch),
        grid=(m // tm,),
        in_specs=[
            rows(0), rows(0), rows(4), rows(5), rows(6), rows(7), rows(8),
            pl.BlockSpec((None, SUBLANES, d), lambda i: (i // per_batch, 0, 0)),
            whole((1, d)), whole((SC_CONV, SUBLANES, d)),
            whole((d, d)), whole((d, d)), whole((d, d)),
            whole((d, LANES)), whole((d, LANES)), whole((1, LANES)),
        ],
        out_specs=[
            rows(0), pl.BlockSpec((tm * SUBLANES, LANES), lambda i: (i, 0)),
            pl.BlockSpec((tm, LANES), lambda i: (i, 0)),
            pl.BlockSpec((SUBLANES, LANES), lambda i: (0, 0)),
        ],
        out_shape=[
            jax.ShapeDtypeStruct((m, d), F32),
            jax.ShapeDtypeStruct((m * SUBLANES, LANES), F32),
            jax.ShapeDtypeStruct((m, LANES), F32),
            jax.ShapeDtypeStruct((SUBLANES, LANES), F32),
        ],
        scratch_shapes=[
            pltpu.VMEM((SUBLANES, d), F32),
            pltpu.VMEM((SUBLANES, LANES), F32),
        ],
        compiler_params=_params("post"),
        name="post",
    )(x2, og, proj, proj, proj, proj, proj, mod, n2g, conv_w, wpa, wpb, wout, w_route_hi, w_route_lo,
      b_route)


def _plan_kernel(route_ref, cnt_ref, d_ref):
    bm = EXPERT_BLOCK
    sizes = cnt_ref[...]
    padded = jnp.floor((sizes + (bm - 1.0)) * (1.0 / bm)) * bm
    lane_i = lax.broadcasted_iota(jnp.int32, sizes.shape, 1)
    incl = padded
    s = 1
    while s < LANES:
        incl = incl + jnp.where(lane_i >= s, pltpu.roll(incl, s, 1), 0.0)
        s *= 2
    start = (incl - padded)[0:1, :]
    r = route_ref[...]
    lane = lax.broadcasted_iota(jnp.int32, r.shape, 1).astype(F32)
    d1 = jnp.sum(jnp.where(lane == r[:, 0:1], start, 0.0), axis=-1, keepdims=True) + r[:, 4:5]
    d2 = jnp.sum(jnp.where(lane == r[:, 1:2], start, 0.0), axis=-1, keepdims=True) + r[:, 5:6]
    out = jnp.where(lane == 0.0, d1, jnp.where(lane == 1.0, d2, 0.0))
    d_ref[...] = out.T[0:SUBLANES, :].astype(jnp.int32)


def _plan(route, counts, tm):
    m = route.shape[0]
    return pl.pallas_call(
        _plan_kernel,
        grid=(m // tm,),
        in_specs=[
            pl.BlockSpec((tm, LANES), lambda i: (i, 0)),
            pl.BlockSpec((SUBLANES, LANES), lambda i: (0, 0)),
        ],
        out_specs=pl.BlockSpec((SUBLANES, tm), lambda i: (0, i)),
        out_shape=jax.ShapeDtypeStruct((SUBLANES, m), jnp.int32),
        compiler_params=_params("plan"),
        name="plan",
    )(route, counts)


def _row_copy(src_ref, src_row, dst_ref, dst_row, sem):
    return pltpu.make_async_copy(_row_tile(src_ref, src_row), _row_tile(dst_ref, dst_row), sem)


def _rows_copy(src_ref, dst_ref, dst_row, n, sem):
    return pltpu.make_async_copy(src_ref, dst_ref.at[pl.ds(dst_row * SUBLANES, n * SUBLANES), :], sem)


def _dispatch_kernel(d1_ref, d2_ref, tail_ref, h2_ref, xs_ref, zero_ref, sem):
    tm = h2_ref.shape[0] // SUBLANES
    bm = zero_ref.shape[0] // SUBLANES
    t0 = pl.program_id(0) * tm

    @pl.when(pl.program_id(0) == 0)
    def _():
        zero_ref[...] = jnp.zeros(zero_ref.shape, F32)
        n_blocks = xs_ref.shape[0] // (bm * SUBLANES)

        def zero_block(row):
            return _rows_copy(zero_ref, xs_ref, pl.multiple_of(row, bm), bm, sem.at[0])

        for e in range(N_EXPERTS):
            @pl.when(tail_ref[e] >= 0)
            def _():
                zero_block(tail_ref[e]).start()
        lax.fori_loop(tail_ref[N_EXPERTS], n_blocks, lambda j, c: (zero_block(j * bm).start(), c)[1], 0)
        for e in range(N_EXPERTS):
            @pl.when(tail_ref[e] >= 0)
            def _():
                zero_block(tail_ref[e]).wait()
        lax.fori_loop(tail_ref[N_EXPERTS], n_blocks, lambda j, c: (zero_block(j * bm).wait(), c)[1], 0)

    def start(r, carry):
        _row_copy(h2_ref, r, xs_ref, d1_ref[t0 + r], sem.at[0]).start(priority=0)
        _row_copy(h2_ref, r, xs_ref, d2_ref[t0 + r], sem.at[1]).start(priority=1)
        return carry

    lax.fori_loop(0, tm, start, 0, unroll=DMA_UNROLL)
    _rows_copy(h2_ref, xs_ref, 0, tm, sem.at[0]).wait()
    _rows_copy(h2_ref, xs_ref, 0, tm, sem.at[1]).wait()


def _dispatch(d1, d2, tail_start, h2, cap, tm):
    m = h2.shape[0] // SUBLANES
    return pl.pallas_call(
        _dispatch_kernel,
        grid_spec=pltpu.PrefetchScalarGridSpec(
            num_scalar_prefetch=3,
            grid=(m // tm,),
            in_specs=[pl.BlockSpec((tm * SUBLANES, LANES), lambda i, d1, d2, tl: (i, 0))],
            out_specs=pl.BlockSpec(memory_space=pl.ANY),
            scratch_shapes=[pltpu.VMEM((EXPERT_BLOCK * SUBLANES, LANES), F32),
                            pltpu.SemaphoreType.DMA((2,))],
        ),
        out_shape=jax.ShapeDtypeStruct((cap * SUBLANES, LANES), F32),
        compiler_params=_params("dispatch"),
        name="dispatch",
    )(d1, d2, tail_start, h2)


def _expert_kernel(be_ref, act_ref, x_ref, w1_ref, w3_ref, w2_ref, y_ref, w1b_ref, w3b_ref, w2b_ref):
    b = pl.program_id(0)

    @pl.when(act_ref[b] > 0)
    def _():
        @pl.when((b == 0) | (be_ref[b] != be_ref[jnp.maximum(b - 1, 0)]))
        def _():
            w1b_ref[...] = w1_ref[...].astype(BF16)
            w3b_ref[...] = w3_ref[...].astype(BF16)
            w2b_ref[...] = w2_ref[...].astype(BF16)

        xb = _read_rows(x_ref, x_ref.shape[0] // SUBLANES).astype(BF16)
        hid = _silu(_dot(xb, w1b_ref[...])) * _dot(xb, w3b_ref[...])
        _write_rows(y_ref, _dot(hid.astype(BF16), w2b_ref[...]))

    @pl.when(act_ref[b] == 0)
    def _():
        y_ref[...] = jnp.zeros(y_ref.shape, F32)


def _experts(block_expert, block_active, xs, w1, w3, w2):
    cap = xs.shape[0] // SUBLANES
    d, de = w1.shape[1], w1.shape[2]
    bm = EXPERT_BLOCK

    def x_block(b, be, act):
        return (jnp.minimum(b, jnp.maximum(act[cap // bm], 1) - 1), 0)

    return pl.pallas_call(
        _expert_kernel,
        grid_spec=pltpu.PrefetchScalarGridSpec(
            num_scalar_prefetch=2,
            grid=(cap // bm,),
            in_specs=[
                pl.BlockSpec((bm * SUBLANES, LANES), x_block),
                pl.BlockSpec((None, d, de), lambda b, be, act: (be[b], 0, 0)),
                pl.BlockSpec((None, d, de), lambda b, be, act: (be[b], 0, 0)),
                pl.BlockSpec((None, de, d), lambda b, be, act: (be[b], 0, 0)),
            ],
            out_specs=pl.BlockSpec((bm * SUBLANES, LANES), lambda b, be, act: (b, 0)),
            scratch_shapes=[pltpu.VMEM((d, de), BF16), pltpu.VMEM((d, de), BF16),
                            pltpu.VMEM((de, d), BF16)],
        ),
        out_shape=jax.ShapeDtypeStruct((cap * SUBLANES, LANES), F32),
        compiler_params=_params("experts"),
        name="experts",
    )(block_expert, block_active, xs, w1, w3, w2)


def _final_kernel(d1_ref, d2_ref, x1_ref, route_ref, gt_ref, nfg_ref, ys_ref, o_ref, buf_ref, sem):
    tm = x1_ref.shape[0]
    i = pl.program_id(0)

    def gather(tile, s):
        t0 = tile * tm

        def start(r, carry):
            _row_copy(ys_ref, d1_ref[t0 + r], buf_ref.at[s, 0], r, sem.at[s, 0]).start(priority=0)
            _row_copy(ys_ref, d2_ref[t0 + r], buf_ref.at[s, 1], r, sem.at[s, 1]).start(priority=1)
            return carry

        lax.fori_loop(0, tm, start, 0, unroll=DMA_UNROLL)

    def wait_rows(s):
        for k in range(2):
            pltpu.make_async_copy(ys_ref.at[pl.ds(0, tm * SUBLANES), :], buf_ref.at[s, k],
                                  sem.at[s, k]).wait()

    def combine(cur, nxt):
        wait_rows(cur)
        last = pl.num_programs(0) - 1
        t0 = jnp.minimum(i + 1, last) * tm
        for r in range(tm):
            _row_copy(ys_ref, d1_ref[t0 + r], buf_ref.at[nxt, 0], r, sem.at[nxt, 0]).start(priority=0)
            _row_copy(ys_ref, d2_ref[t0 + r], buf_ref.at[nxt, 1], r, sem.at[nxt, 1]).start(priority=1)
        route = route_ref[...]
        moe = (_read_rows(buf_ref.at[cur, 0], tm) * route[:, 2:3]
               + _read_rows(buf_ref.at[cur, 1], tm) * route[:, 3:4])
        x2 = x1_ref[...] + gt_ref[...] * moe
        y = x2 * lax.rsqrt(jnp.mean(x2 * x2, axis=-1, keepdims=True) + EPS)
        o_ref[...] = y * nfg_ref[...]

        @pl.when(i == last)
        def _():
            wait_rows(nxt)

    @pl.when(i == 0)
    def _():
        gather(0, 0)

    @pl.when(i % 2 == 0)
    def _():
        combine(0, 1)

    @pl.when(i % 2 == 1)
    def _():
        combine(1, 0)


def _final(d1, d2, x1, route, gt2, nfg, ys, seq, tm):
    m, d = x1.shape
    per_batch = seq // tm
    return pl.pallas_call(
        _final_kernel,
        grid_spec=pltpu.PrefetchScalarGridSpec(
            num_scalar_prefetch=2,
            grid=(m // tm,),
            in_specs=[
                pl.BlockSpec((tm, d), lambda i, d1, d2: (i, 0)),
                pl.BlockSpec((tm, LANES), lambda i, d1, d2: (i, 0)),
                pl.BlockSpec((None, 1, d), lambda i, d1, d2: (i // per_batch, 0, 0)),
                pl.BlockSpec((1, d), lambda i, d1, d2: (0, 0)),
                pl.BlockSpec(memory_space=pl.ANY),
            ],
            out_specs=pl.BlockSpec((tm, d), lambda i, d1, d2: (i, 0)),
            scratch_shapes=[pltpu.VMEM((2, 2, tm * SUBLANES, LANES), F32),
                            pltpu.SemaphoreType.DMA((2, 2))],
        ),
        out_shape=jax.ShapeDtypeStruct((m, d), F32),
        compiler_params=_params("final"),
        name="final",
    )(d1, d2, x1, route, gt2, nfg, ys)


def _tile(n, pref):
    t = min(n, pref)
    assert n % t == 0
    return t


def kernel(x, c, w_ada, b_ada, norm1_g, w_in, conv_qkv_w, a_log, dt_bias, onorm_g, w_proj_a,
           conv_sc_w, w_proj_b, w_out, norm2_g, w_group, b_group, w_expert, b_expert, w1, w3, w2,
           normf_g):
    batch, seq, d = x.shape
    depth = w_ada.shape[0]
    m = batch * seq
    kd = N_HEADS * HEAD_D
    assert d == kd and seq % CHUNK == 0 and batch <= SUBLANES
    assert depth == 1, "the last stage applies the final rmsnorm: one layer only"
    c_pad = jnp.zeros((SUBLANES, d), F32).at[:batch].set(c)
    x2 = x.reshape(m, d)

    for l in range(depth):
        mod = _ada(c_pad, w_ada[l], b_ada[l][None, :])[:batch]
        sh1, sc1, gt1, sh2, sc2, gt2 = [mod[:, None, j * d:(j + 1) * d] for j in range(6)]

        w = w_in[l]
        o_ba = 3 * kd + kd
        w_b = w[:, o_ba + 2 * N_HEADS:].astype(BF16)
        w_ba = jnp.zeros((d, LANES), F32).at[:, :2 * N_HEADS].set(w[:, o_ba:o_ba + 2 * N_HEADS]).astype(BF16)
        proj, ba = _inproj(x2, norm1_g[l][None, :], sc1, sh1, w, o_ba // d, w_b, w_ba, seq,
                           _tile(seq, INPROJ_ROWS))

        head_params = jnp.zeros((SUBLANES, LANES), F32)
        head_params = head_params.at[0, N_HEADS:2 * N_HEADS].set(a_log[l])
        head_params = head_params.at[1, N_HEADS:2 * N_HEADS].set(dt_bias[l])
        head_params = head_params.at[2, :].set(onorm_g[l])
        conv_taps = jnp.broadcast_to(conv_qkv_w[l][:, None, :], (QKV_CONV, SUBLANES, 3 * kd))
        og = _gdn(proj, ba, conv_taps, head_params, batch, seq, _tile(seq, GDN_ROWS))

        mod_post = jnp.zeros((batch, SUBLANES, d), F32)
        mod_post = mod_post.at[:, 0:1].set(gt1).at[:, 1:2].set(sc2).at[:, 2:3].set(sh2)
        w_route = jnp.zeros((d, LANES), F32)
        w_route = w_route.at[:, :N_GROUPS].set(w_group[l]).at[:, N_GROUPS:N_GROUPS + N_EXPERTS].set(w_expert[l])
        w_route_hi = w_route.astype(BF16)
        w_route_lo = (w_route - w_route_hi.astype(F32)).astype(BF16)
        b_route = jnp.zeros((1, LANES), F32)
        b_route = b_route.at[0, :N_GROUPS].set(b_group[l]).at[0, N_GROUPS:N_GROUPS + N_EXPERTS].set(b_expert[l])
        x1, h2, route, counts = _post(
            x2, og, proj, mod_post, norm2_g[l][None, :],
            jnp.broadcast_to(conv_sc_w[l][:, None, :], (SC_CONV, SUBLANES, d)),
            w_proj_a[l].astype(BF16), w_proj_b[l].astype(BF16), w_out[l].astype(BF16),
            w_route_hi, w_route_lo, b_route, seq, _tile(seq, POST_ROWS))

        bm = EXPERT_BLOCK
        n_blocks = (2 * m) // bm + N_EXPERTS
        sizes = counts[0, :N_EXPERTS].astype(jnp.int32)
        padded = ((sizes + bm - 1) // bm) * bm
        pad_end = jnp.cumsum(padded)
        tail_start = jnp.concatenate([jnp.where(padded > 0, pad_end - bm, -1),
                                      pad_end[-1:] // bm]).astype(jnp.int32)
        block_row = jnp.arange(n_blocks, dtype=jnp.int32) * bm
        block_expert = jnp.minimum(jnp.sum(block_row[:, None] >= pad_end[None, :], axis=1),
                                   N_EXPERTS - 1).astype(jnp.int32)
        block_active = jnp.concatenate([(block_row < pad_end[-1]).astype(jnp.int32),
                                        (pad_end[-1:] // bm).astype(jnp.int32)])
        dest = _plan(route, counts, _tile(seq, INPROJ_ROWS))
        d1, d2 = dest[0], dest[1]

        xs = _dispatch(d1, d2, tail_start, h2, n_blocks * bm, _tile(seq, DISPATCH_ROWS))
        ys = _experts(block_expert, block_active, xs, w1[l], w3[l], w2[l])
        nfg = normf_g[None, :]
        x2 = _final(d1, d2, x1, route, gt2, nfg, ys, seq, _tile(seq, FINAL_ROWS))
    return x2.reshape(batch, seq, d)
```

```python
import functools

import jax
import jax.numpy as jnp
from jax import lax
from jax.experimental import pallas as pl
from jax.experimental.pallas import tpu as pltpu

F32 = jnp.float32
BF16 = jnp.bfloat16
HIGHEST = lax.Precision.HIGHEST

N_HEADS = 8
HEAD_D = 128
CHUNK = 64
QKV_CONV = 4
SC_CONV = 3
N_GROUPS = 4
EXPERTS_PER_GROUP = 8
N_EXPERTS = N_GROUPS * EXPERTS_PER_GROUP
EPS = 1e-6

LANES = 128
SUBLANES = 8
BF16_ROWS = 16
EXPERT_BLOCK = 512
INPROJ_ROWS = 2048
GDN_ROWS = 256
PRE_CHUNKS = 2
POST_ROWS = 512
DISPATCH_ROWS = 1024
FINAL_ROWS = 512
DMA_UNROLL = 8
MIB = 1024 * 1024
VMEM_MIB = {"ada": 24, "inproj": 56, "gdn": 48, "post": 56, "plan": 16, "dispatch": 24, "experts": 40,
            "final": 32}


def _sigmoid(x):
    return 0.5 + 0.5 * jnp.tanh(0.5 * x)


def _silu(x):
    half = 0.5 * x
    return half + half * jnp.tanh(half)


def _softplus(x):
    return jnp.maximum(x, 0.0) + jnp.log(1.0 + jnp.exp(-jnp.abs(x)))


def _dot(a, b):
    return jnp.dot(a, b, preferred_element_type=F32)


def _dot_nt(a, b):
    return lax.dot_general(a, b, (((1,), (1,)), ((), ())), preferred_element_type=F32)


def _dot_hi(a, b):
    return jnp.dot(a, b, preferred_element_type=F32, precision=HIGHEST)


def _dot_split(a, b_hi, b_lo):
    a_hi = a.astype(BF16)
    a_lo = (a - a_hi.astype(F32)).astype(BF16)
    return _dot(a_hi, b_hi) + (_dot(a_lo, b_hi) + _dot(a_hi, b_lo))


def _read_rows(ref, rows):
    return jnp.concatenate([ref[pl.ds(j, rows, stride=SUBLANES), :] for j in range(SUBLANES)], axis=1)


def _write_rows(ref, value):
    rows = value.shape[0]
    for j in range(SUBLANES):
        ref[pl.ds(j, rows, stride=SUBLANES), :] = value[:, j * LANES:(j + 1) * LANES]


def _row_tile(ref, row):
    if not isinstance(row, int):
        row = pl.multiple_of(row * SUBLANES, SUBLANES)
    else:
        row = row * SUBLANES
    return ref.at[pl.ds(row, SUBLANES), :]


def _params(call, n_grid_axes=1):
    return pltpu.CompilerParams(dimension_semantics=("arbitrary",) * n_grid_axes,
                                vmem_limit_bytes=VMEM_MIB[call] * MIB)


def _ada_kernel(c_ref, w_ref, b_ref, o_ref):
    o_ref[...] = _dot_hi(_silu(c_ref[...]), w_ref[...]) + b_ref[...]


def _ada(c_pad, w_ada, b_ada):
    d = c_pad.shape[1]
    n = w_ada.shape[1]
    return pl.pallas_call(
        _ada_kernel,
        grid=(n // d,),
        in_specs=[
            pl.BlockSpec((SUBLANES, d), lambda j: (0, 0)),
            pl.BlockSpec((d, d), lambda j: (0, j)),
            pl.BlockSpec((1, d), lambda j: (0, j)),
        ],
        out_specs=pl.BlockSpec((SUBLANES, d), lambda j: (0, j)),
        out_shape=jax.ShapeDtypeStruct((SUBLANES, n), F32),
        compiler_params=_params("ada"),
        name="ada",
    )(c_pad, w_ada, b_ada)


def _inproj_kernel(x_hbm, g_ref, sc_ref, sh_ref, wt_hbm, o_ref, ba_ref, x_ref, h_ref, w_ref, wba_ref, stage_ref,
                   stage_ba_ref, sem_x, sem, sem_ba, *, n_lead, n_ba):
    i, j = pl.program_id(0), pl.program_id(1)
    n_tiles = pl.num_programs(0)
    n_blocks, tn = w_ref.shape[0], o_ref.shape[1]
    tm, half = x_ref.shape[0], stage_ref.shape[1]

    def x_copy(tile):
        return pltpu.make_async_copy(x_hbm.at[pl.ds(pl.multiple_of(tile * tm, tm), tm)], x_ref, sem_x)

    def block_copy(jb, part):
        row = jb * tn + jnp.where(jb >= n_lead, n_ba, 0) + part * half
        return pltpu.make_async_copy(wt_hbm.at[pl.ds(pl.multiple_of(row, SUBLANES), half)],
                                     stage_ref.at[part], sem.at[part])

    def ba_copy():
        return pltpu.make_async_copy(wt_hbm.at[pl.ds(n_lead * tn, n_ba)], stage_ba_ref, sem_ba)

    @pl.when(j == 0)
    def _():
        @pl.when(i == 0)
        def _():
            x_copy(i).start()
            ba_copy().start()
            for part in range(2):
                block_copy(j, part).start()

        x_copy(i).wait()
        x = x_ref[...]
        y = x * lax.rsqrt(jnp.mean(x * x, axis=-1, keepdims=True) + EPS)
        h = (y * g_ref[...]) * (1.0 + sc_ref[...]) + sh_ref[...]
        hb = h.astype(BF16)
        h_ref[...] = hb

        @pl.when(i == 0)
        def _():
            ba_copy().wait()
            wba_ref[...] = jnp.zeros(wba_ref.shape, BF16)
            wba_ref[0:n_ba, :] = stage_ba_ref[...].astype(BF16)

        ba_ref[...] = _dot_nt(hb, wba_ref[...])

    @pl.when(jnp.logical_and(j == 1, i + 1 < n_tiles))
    def _():
        x_copy(i + 1).start()

    @pl.when(i == 0)
    def _():
        for part in range(2):
            block_copy(j, part).wait()
            w_ref[j, pl.ds(part * half, half), :] = stage_ref[part].astype(BF16)

            @pl.when(j + 1 < n_blocks)
            def _():
                block_copy(j + 1, part).start()

    rows = o_ref.shape[0] // 2
    for r in range(2):
        o_ref[pl.ds(r * rows, rows), :] = _dot_nt(h_ref[pl.ds(r * rows, rows), :], w_ref[j]).astype(BF16)


def _inproj(x2, norm_g, sc, sh, wt, n_lead, n_ba, seq, tm):
    m, d = x2.shape
    tn = d
    n = wt.shape[0] - n_ba
    assert n % tn == 0 and n_ba % (2 * SUBLANES) == 0 and n_ba <= LANES
    per_batch = seq // tm
    return pl.pallas_call(
        functools.partial(_inproj_kernel, n_lead=n_lead, n_ba=n_ba),
        grid=(m // tm, n // tn),
        in_specs=[
            pl.BlockSpec(memory_space=pl.ANY),
            pl.BlockSpec((1, d), lambda i, j: (0, 0)),
            pl.BlockSpec((None, 1, d), lambda i, j: (i // per_batch, 0, 0)),
            pl.BlockSpec((None, 1, d), lambda i, j: (i // per_batch, 0, 0)),
            pl.BlockSpec(memory_space=pl.ANY),
        ],
        out_specs=[
            pl.BlockSpec((tm, tn), lambda i, j: (i, j)),
            pl.BlockSpec((tm, LANES), lambda i, j: (i, 0)),
        ],
        out_shape=[
            jax.ShapeDtypeStruct((m, n), BF16),
            jax.ShapeDtypeStruct((m, LANES), F32),
        ],
        scratch_shapes=[
            pltpu.VMEM((tm, d), F32),
            pltpu.VMEM((tm, d), BF16),
            pltpu.VMEM((n // tn, tn, d), BF16),
            pltpu.VMEM((LANES, d), BF16),
            pltpu.VMEM((2, tn // 2, d), F32),
            pltpu.VMEM((n_ba, d), F32),
            pltpu.SemaphoreType.DMA(()),
            pltpu.SemaphoreType.DMA((2,)),
            pltpu.SemaphoreType.DMA(()),
        ],
        compiler_params=_params("inproj", 2),
        name="inproj",
    )(x2, norm_g, sc, sh, wt)


def _bmm(a, b):
    return jnp.einsum("hmk,hkn->hmn", a.astype(BF16), b.astype(BF16), preferred_element_type=F32)


def _bmm_nt(a, b):
    return jnp.einsum("hmk,hnk->hmn", a.astype(BF16), b.astype(BF16), preferred_element_type=F32)


def _unit_lower_inverse(a):
    row = lax.broadcasted_iota(jnp.int32, a.shape[1:], 0)
    col = lax.broadcasted_iota(jnp.int32, a.shape[1:], 1)
    apart = row ^ col
    eye = jnp.where(row == col, 1.0, 0.0).astype(F32)
    t = jnp.where(apart < 2, eye - a, 0.0)
    s = 2
    while s < CHUNK:
        coupling = jnp.where((apart >= s) & (apart < 2 * s), a, 0.0)
        t = t - _bmm(t, _bmm(coupling, t))
        s *= 2
    return t


def _lane_sums(x):
    h, rows, width = x.shape
    ones = jnp.ones((width, width), BF16)
    return _dot(x.reshape(h * rows, width).astype(BF16), ones).reshape(h, rows, width)


def _causal_conv_silu(win, cw, k_w):
    assert k_w == 4
    tiled = (win.shape[0] // SUBLANES, SUBLANES, win.shape[1])

    def pair(x, x1, j):
        return (x.reshape(tiled) * cw[j][None] + x1.reshape(tiled) * cw[j - 1][None]).reshape(win.shape)

    win1 = pltpu.roll(win, 1, 0)
    acc = pair(win, win1, 3) + pltpu.roll(pair(win, win1, 1), 2, 0)
    return _silu(acc[SUBLANES:, :])


def _gdn_kernel(q_ref, k_ref, v_ref, z_ref, ba_ref, cw_ref, hp_ref, o_ref,
                s_ref, tail_ref, wq_ref, u_ref, ik_ref, dec_ref):
    nb, tb = q_ref.shape[0], q_ref.shape[1]
    kd = N_HEADS * HEAD_D
    nbh = nb * N_HEADS

    @pl.when(pl.program_id(0) == 0)
    def _():
        s_ref[...] = jnp.zeros(s_ref.shape, F32)
        tail_ref[...] = jnp.zeros(tail_ref.shape, F32)

    row = lax.broadcasted_iota(jnp.int32, (CHUNK, CHUNK), 0)
    col = lax.broadcasted_iota(jnp.int32, (CHUNK, CHUNK), 1)
    causal = row >= col
    strict = row > col
    tril = jnp.where(causal, 1.0, 0.0).astype(F32)
    a_log = hp_ref[0:1, :]
    dt_bias = hp_ref[1:2, :]
    onorm_g = hp_ref[2:3, :]
    zeros_half = jnp.zeros((CHUNK, HEAD_D), F32)

    def precompute(cp, carry):
        qs, ks, vs, betas, gcs, grs, gls = [], [], [], [], [], [], []
        for sub in range(PRE_CHUNKS):
            base = pl.multiple_of((cp * PRE_CHUNKS + sub) * CHUNK, CHUNK)
            prev = pl.multiple_of(base - BF16_ROWS, BF16_ROWS)
            for b in range(nb):
                ba = ba_ref[b, pl.ds(base, CHUNK), :]
                beta_all = _sigmoid(ba)
                g_all = -jnp.exp(a_log) * _softplus(ba + dt_bias)
                gcum = _dot_hi(tril, g_all)
                gcum_t = jnp.concatenate([gcum, gcum], axis=0).T
                for h in range(N_HEADS):
                    lo, hi = h * HEAD_D, (h + 1) * HEAD_D

                    def conv(ref, off):
                        cur = ref[b, pl.ds(base, CHUNK), lo:hi].astype(F32)
                        if sub == 0:
                            before = tail_ref[b, :, off + lo:off + hi]
                        else:
                            before = ref[b, pl.ds(prev, BF16_ROWS), lo:hi].astype(F32)[SUBLANES:]
                        win = jnp.concatenate([before, cur], axis=0)
                        return _causal_conv_silu(win, cw_ref[:, :, off + lo:off + hi], QKV_CONV)

                    qs.append(conv(q_ref, 0))
                    ks.append(conv(k_ref, kd))
                    vs.append(conv(v_ref, 2 * kd))
                    betas.append(beta_all[:, h:h + 1])
                    gcs.append(gcum[:, N_HEADS + h:N_HEADS + h + 1])
                    grs.append(gcum_t[N_HEADS + h:N_HEADS + h + 1, 0:CHUNK])
                    gls.append(gcum[CHUNK - 1:CHUNK, N_HEADS + h:N_HEADS + h + 1])
        q, k, v = jnp.stack(qs), jnp.stack(ks), jnp.stack(vs)
        beta, gc, gr, gl = jnp.stack(betas), jnp.stack(gcs), jnp.stack(grs), jnp.stack(gls)
        qn = q * (lax.rsqrt(_lane_sums(q * q) + EPS) * (HEAD_D ** -0.5))
        kn = k * lax.rsqrt(_lane_sums(k * k) + EPS)
        decay = jnp.where(causal, jnp.exp(jnp.where(causal, gc - gr, 0.0)), 0.0)
        kb = kn * beta
        e_gc = jnp.exp(gc)
        kq = _bmm_nt(jnp.concatenate([kb, qn], axis=1), kn)
        a = jnp.where(strict, kq[:, :CHUNK] * decay, 0.0)
        intra = kq[:, CHUNK:] * decay
        uw = _bmm(_unit_lower_inverse(a), jnp.concatenate([v * beta, kb * e_gc], axis=2))
        wq = jnp.concatenate([uw[:, :, HEAD_D:], qn * e_gc], axis=1).astype(BF16)
        k_dec = kn * jnp.exp(gl - gc)
        k_dec_t = jnp.stack([jnp.concatenate([k_dec[i], zeros_half], axis=0).T[:, :CHUNK]
                             for i in range(PRE_CHUNKS * nbh)])
        ik = jnp.concatenate([intra, k_dec_t], axis=1).astype(BF16)
        dec = jnp.broadcast_to(jnp.exp(gl), (PRE_CHUNKS * nbh, 1, HEAD_D))
        for sub in range(PRE_CHUNKS):
            c = cp * PRE_CHUNKS + sub
            rows = slice(sub * nbh, (sub + 1) * nbh)
            u_ref[c] = uw[rows, :, :HEAD_D]
            wq_ref[c] = wq[rows]
            ik_ref[c] = ik[rows]
            dec_ref[c] = dec[rows]

        last = pl.multiple_of((cp + 1) * PRE_CHUNKS * CHUNK - BF16_ROWS, BF16_ROWS)
        for b in range(nb):
            for j, ref in enumerate((q_ref, k_ref, v_ref)):
                rows = ref[b, pl.ds(last, BF16_ROWS), :].astype(F32)
                tail_ref[b, :, j * kd:(j + 1) * kd] = rows[BF16_ROWS - SUBLANES:, :]
        return carry

    def recur(c, carry):
        base = pl.multiple_of(c * CHUNK, CHUNK)
        state = s_ref[...]
        ws = _bmm(wq_ref[c], state)
        v_new = u_ref[c] - ws[:, :CHUNK]
        r = _bmm(ik_ref[c], v_new)
        o = ws[:, CHUNK:] + r[:, :CHUNK]
        s_ref[...] = state * dec_ref[c] + r[:, CHUNK:]
        on = o * lax.rsqrt(jnp.mean(o * o, axis=-1, keepdims=True) + EPS) * onorm_g
        for b in range(nb):
            for h in range(N_HEADS):
                lo, hi = h * HEAD_D, (h + 1) * HEAD_D
                z = z_ref[b, pl.ds(base, CHUNK), lo:hi].astype(F32)
                o_ref[b, pl.ds(base, CHUNK), lo:hi] = (on[b * N_HEADS + h] * _silu(z)).astype(BF16)
        return carry

    lax.fori_loop(0, tb // (PRE_CHUNKS * CHUNK), precompute, 0)
    lax.fori_loop(0, tb // CHUNK, recur, 0)


def _gdn(proj, ba, conv_w, head_params, batch, seq, tb):
    kd = N_HEADS * HEAD_D
    nc = tb // CHUNK
    nbh = batch * N_HEADS
    proj3 = proj.reshape(batch, seq, proj.shape[1])
    ba3 = ba.reshape(batch, seq, LANES)

    def col(j):
        return pl.BlockSpec((batch, tb, kd), lambda t: (0, t, j))

    out = pl.pallas_call(
        _gdn_kernel,
        grid=(seq // tb,),
        in_specs=[
            col(0), col(1), col(2), col(3),
            pl.BlockSpec((batch, tb, LANES), lambda t: (0, t, 0)),
            pl.BlockSpec((QKV_CONV, SUBLANES, 3 * kd), lambda t: (0, 0, 0)),
            pl.BlockSpec((SUBLANES, LANES), lambda t: (0, 0)),
        ],
        out_specs=pl.BlockSpec((batch, tb, kd), lambda t: (0, t, 0)),
        out_shape=jax.ShapeDtypeStruct((batch, seq, kd), BF16),
        scratch_shapes=[
            pltpu.VMEM((nbh, HEAD_D, HEAD_D), F32),
            pltpu.VMEM((batch, SUBLANES, 3 * kd), F32),
            pltpu.VMEM((nc, nbh, 2 * CHUNK, HEAD_D), BF16),
            pltpu.VMEM((nc, nbh, CHUNK, HEAD_D), F32),
            pltpu.VMEM((nc, nbh, CHUNK + HEAD_D, CHUNK), BF16),
            pltpu.VMEM((nc, nbh, 1, HEAD_D), F32),
        ],
        compiler_params=_params("gdn"),
        name="gdn",
    )(proj3, proj3, proj3, proj3, ba3, conv_w, head_params)
    return out.reshape(batch * seq, kd)


def _post_kernel(x_ref, og_ref, sb_ref, sc_ref, sx_ref, ga_ref, gb_ref, mod_ref, n2g_ref, cw_ref,
                 wpa_ref, wpb_ref, wout_ref, wrh_ref, wrl_ref, br_ref,
                 x1_ref, h2_ref, route_ref, cnt_ref, win_ref, run_ref, *, per_batch):
    tm = x_ref.shape[0]
    i = pl.program_id(0)

    @pl.when(i == 0)
    def _():
        run_ref[...] = jnp.zeros(run_ref.shape, F32)

    @pl.when(i % per_batch == 0)
    def _():
        win_ref[...] = jnp.zeros(win_ref.shape, F32)

    assert SC_CONV == 3
    prod = sc_ref[...].astype(F32) * sx_ref[...].astype(F32)
    win = jnp.concatenate([win_ref[...], prod], axis=0)
    win_ref[...] = prod[tm - SUBLANES:, :]
    tiled = (win.shape[0] // SUBLANES, SUBLANES, win.shape[1])
    win1 = pltpu.roll(win, 1, 0)
    conv = (win.reshape(tiled) * cw_ref[2][None] + win1.reshape(tiled) * cw_ref[1][None]
            + pltpu.roll(win1, 1, 0).reshape(tiled) * cw_ref[0][None]).reshape(win.shape)[SUBLANES:, :]
    y_b = _dot((sb_ref[...].astype(F32) * conv).astype(BF16), wpb_ref[...])
    y_a = _dot(og_ref[...], wpa_ref[...])
    merged = _sigmoid(ga_ref[...].astype(F32)) * y_a + _sigmoid(gb_ref[...].astype(F32)) * y_b
    mix = _dot(merged.astype(BF16), wout_ref[...])
    x1 = x_ref[...] + mod_ref[0:1, :] * mix
    x1_ref[...] = x1

    y = x1 * lax.rsqrt(jnp.mean(x1 * x1, axis=-1, keepdims=True) + EPS)
    h2 = (y * n2g_ref[...]) * (1.0 + mod_ref[1:2, :]) + mod_ref[2:3, :]
    _write_rows(h2_ref, h2)

    lg = _dot_split(h2, wrh_ref[...], wrl_ref[...]) + br_ref[...]
    lane = lax.broadcasted_iota(jnp.int32, lg.shape, 1).astype(F32)
    neg = jnp.float32(-jnp.inf)
    big = jnp.float32(2 * LANES)

    def first_max(mask):
        vmax = jnp.max(jnp.where(mask, lg, neg), axis=-1, keepdims=True)
        idx = jnp.min(jnp.where(mask & (lg == vmax), lane, big), axis=-1, keepdims=True)
        return vmax, idx

    gmask = lane < N_GROUPS
    g_max, g_sel = first_max(gmask)
    p_group = 1.0 / jnp.sum(jnp.where(gmask, jnp.exp(lg - g_max), 0.0), axis=-1, keepdims=True)
    e_lo = N_GROUPS + EXPERTS_PER_GROUP * g_sel
    emask = (lane >= e_lo) & (lane < e_lo + EXPERTS_PER_GROUP)
    v1, i1 = first_max(emask)
    v2, i2 = first_max(emask & (lane != i1))
    ex = jnp.exp(v2 - v1)
    w1 = p_group * (1.0 / (1.0 + ex))
    w2 = p_group * (ex / (1.0 + ex))
    e1 = i1 - N_GROUPS
    e2 = i2 - N_GROUPS

    onehot = jnp.where((lane == e1) | (lane == e2), 1.0, 0.0).astype(F32)
    row = lax.broadcasted_iota(jnp.int32, (tm, tm), 0)
    col = lax.broadcasted_iota(jnp.int32, (tm, tm), 1)
    before = jnp.where(row > col, 1.0, 0.0).astype(BF16)
    seen = _dot(before, onehot.astype(BF16)) + run_ref[0:1, :]
    r1 = jnp.sum(jnp.where(lane == e1, seen, 0.0), axis=-1, keepdims=True)
    r2 = jnp.sum(jnp.where(lane == e2, seen, 0.0), axis=-1, keepdims=True)
    run_ref[0:1, :] = run_ref[0:1, :] + jnp.sum(onehot, axis=0, keepdims=True)
    cnt_ref[...] = jnp.broadcast_to(run_ref[0:1, :], cnt_ref.shape)

    out = jnp.where(lane == 0, e1, 0.0)
    out = jnp.where(lane == 1, e2, out)
    out = jnp.where(lane == 2, w1, out)
    out = jnp.where(lane == 3, w2, out)
    out = jnp.where(lane == 4, r1, out)
    out = jnp.where(lane == 5, r2, out)
    route_ref[...] = out


def _post(x2, og, proj, mod, n2g, conv_w, wpa, wpb, wout, w_route_hi, w_route_lo, b_route, seq, tm):
    m, d = x2.shape
    assert d == SUBLANES * LANES
    per_batch = seq // tm

    def rows(j):
        return pl.BlockSpec((tm, d), lambda i: (i, j))

    def whole(shape):
        return pl.BlockSpec(shape, lambda i: tuple(0 for _ in shape))

    return pl.pallas_call(
        functools.partial(_post_kernel, per_batch=per_batch),
        grid=(m // tm,),
        in_specs=[
            rows(0), rows(0), rows(4), rows(5), rows(6), rows(7), rows(8),
            pl.BlockSpec((None, SUBLANES, d), lambda i: (i // per_batch, 0, 0)),
            whole((1, d)), whole((SC_CONV, SUBLANES, d)),
            whole((d, d)), whole((d, d)), whole((d, d)),
            whole((d, LANES)), whole((d, LANES)), whole((1, LANES)),
        ],
        out_specs=[
            rows(0), pl.BlockSpec((tm * SUBLANES, LANES), lambda i: (i, 0)),
            pl.BlockSpec((tm, LANES), lambda i: (i, 0)),
            pl.BlockSpec((SUBLANES, LANES), lambda i: (0, 0)),
        ],
        out_shape=[
            jax.ShapeDtypeStruct((m, d), F32),
            jax.ShapeDtypeStruct((m * SUBLANES, LANES), F32),
            jax.ShapeDtypeStruct((m, LANES), F32),
            jax.ShapeDtypeStruct((SUBLANES, LANES), F32),
        ],
        scratch_shapes=[
            pltpu.VMEM((SUBLANES, d), F32),
            pltpu.VMEM((SUBLANES, LANES), F32),
        ],
        compiler_params=_params("post"),
        name="post",
    )(x2, og, proj, proj, proj, proj, proj, mod, n2g, conv_w, wpa, wpb, wout, w_route_hi, w_route_lo,
      b_route)


def _plan_kernel(route_ref, cnt_ref, d_ref):
    bm = EXPERT_BLOCK
    sizes = cnt_ref[...]
    padded = jnp.floor((sizes + (bm - 1.0)) * (1.0 / bm)) * bm
    lane_i = lax.broadcasted_iota(jnp.int32, sizes.shape, 1)
    incl = padded
    s = 1
    while s < LANES:
        incl = incl + jnp.where(lane_i >= s, pltpu.roll(incl, s, 1), 0.0)
        s *= 2
    start = (incl - padded)[0:1, :]
    r = route_ref[...]
    lane = lax.broadcasted_iota(jnp.int32, r.shape, 1).astype(F32)
    d1 = jnp.sum(jnp.where(lane == r[:, 0:1], start, 0.0), axis=-1, keepdims=True) + r[:, 4:5]
    d2 = jnp.sum(jnp.where(lane == r[:, 1:2], start, 0.0), axis=-1, keepdims=True) + r[:, 5:6]
    out = jnp.where(lane == 0.0, d1, jnp.where(lane == 1.0, d2, 0.0))
    d_ref[...] = out.T[0:SUBLANES, :].astype(jnp.int32)


def _plan(route, counts, tm):
    m = route.shape[0]
    return pl.pallas_call(
        _plan_kernel,
        grid=(m // tm,),
        in_specs=[
            pl.BlockSpec((tm, LANES), lambda i: (i, 0)),
            pl.BlockSpec((SUBLANES, LANES), lambda i: (0, 0)),
        ],
        out_specs=pl.BlockSpec((SUBLANES, tm), lambda i: (0, i)),
        out_shape=jax.ShapeDtypeStruct((SUBLANES, m), jnp.int32),
        compiler_params=_params("plan"),
        name="plan",
    )(route, counts)


def _row_copy(src_ref, src_row, dst_ref, dst_row, sem):
    return pltpu.make_async_copy(_row_tile(src_ref, src_row), _row_tile(dst_ref, dst_row), sem)


def _rows_copy(src_ref, dst_ref, dst_row, n, sem):
    return pltpu.make_async_copy(src_ref, dst_ref.at[pl.ds(dst_row * SUBLANES, n * SUBLANES), :], sem)


def _dispatch_kernel(d1_ref, d2_ref, tail_ref, h2_ref, xs_ref, zero_ref, sem):
    tm = h2_ref.shape[0] // SUBLANES
    bm = zero_ref.shape[0] // SUBLANES
    t0 = pl.program_id(0) * tm

    @pl.when(pl.program_id(0) == 0)
    def _():
        zero_ref[...] = jnp.zeros(zero_ref.shape, F32)
        n_blocks = xs_ref.shape[0] // (bm * SUBLANES)

        def zero_block(row):
            return _rows_copy(zero_ref, xs_ref, pl.multiple_of(row, bm), bm, sem.at[0])

        for e in range(N_EXPERTS):
            @pl.when(tail_ref[e] >= 0)
            def _():
                zero_block(tail_ref[e]).start()
        lax.fori_loop(tail_ref[N_EXPERTS], n_blocks, lambda j, c: (zero_block(j * bm).start(), c)[1], 0)
        for e in range(N_EXPERTS):
            @pl.when(tail_ref[e] >= 0)
            def _():
                zero_block(tail_ref[e]).wait()
        lax.fori_loop(tail_ref[N_EXPERTS], n_blocks, lambda j, c: (zero_block(j * bm).wait(), c)[1], 0)

    def start(r, carry):
        _row_copy(h2_ref, r, xs_ref, d1_ref[t0 + r], sem.at[0]).start(priority=0)
        _row_copy(h2_ref, r, xs_ref, d2_ref[t0 + r], sem.at[1]).start(priority=1)
        return carry

    lax.fori_loop(0, tm, start, 0, unroll=DMA_UNROLL)
    _rows_copy(h2_ref, xs_ref, 0, tm, sem.at[0]).wait()
    _rows_copy(h2_ref, xs_ref, 0, tm, sem.at[1]).wait()


def _dispatch(d1, d2, tail_start, h2, cap, tm):
    m = h2.shape[0] // SUBLANES
    return pl.pallas_call(
        _dispatch_kernel,
        grid_spec=pltpu.PrefetchScalarGridSpec(
            num_scalar_prefetch=3,
            grid=(m // tm,),
            in_specs=[pl.BlockSpec((tm * SUBLANES, LANES), lambda i, d1, d2, tl: (i, 0))],
            out_specs=pl.BlockSpec(memory_space=pl.ANY),
            scratch_shapes=[pltpu.VMEM((EXPERT_BLOCK * SUBLANES, LANES), F32),
                            pltpu.SemaphoreType.DMA((2,))],
        ),
        out_shape=jax.ShapeDtypeStruct((cap * SUBLANES, LANES), F32),
        compiler_params=_params("dispatch"),
        name="dispatch",
    )(d1, d2, tail_start, h2)


def _expert_kernel(be_ref, act_ref, x_ref, w1_ref, w3_ref, w2_ref, y_ref, w1b_ref, w3b_ref, w2b_ref):
    b = pl.program_id(0)

    @pl.when(act_ref[b] > 0)
    def _():
        @pl.when((b == 0) | (be_ref[b] != be_ref[jnp.maximum(b - 1, 0)]))
        def _():
            w1b_ref[...] = w1_ref[...].astype(BF16)
            w3b_ref[...] = w3_ref[...].astype(BF16)
            w2b_ref[...] = w2_ref[...].astype(BF16)

        xb = _read_rows(x_ref, x_ref.shape[0] // SUBLANES).astype(BF16)
        hid = _silu(_dot(xb, w1b_ref[...])) * _dot(xb, w3b_ref[...])
        _write_rows(y_ref, _dot(hid.astype(BF16), w2b_ref[...]))

    @pl.when(act_ref[b] == 0)
    def _():
        y_ref[...] = jnp.zeros(y_ref.shape, F32)


def _experts(block_expert, block_active, xs, w1, w3, w2):
    cap = xs.shape[0] // SUBLANES
    d, de = w1.shape[1], w1.shape[2]
    bm = EXPERT_BLOCK

    def x_block(b, be, act):
        return (jnp.minimum(b, jnp.maximum(act[cap // bm], 1) - 1), 0)

    return pl.pallas_call(
        _expert_kernel,
        grid_spec=pltpu.PrefetchScalarGridSpec(
            num_scalar_prefetch=2,
            grid=(cap // bm,),
            in_specs=[
                pl.BlockSpec((bm * SUBLANES, LANES), x_block),
                pl.BlockSpec((None, d, de), lambda b, be, act: (be[b], 0, 0)),
                pl.BlockSpec((None, d, de), lambda b, be, act: (be[b], 0, 0)),
                pl.BlockSpec((None, de, d), lambda b, be, act: (be[b], 0, 0)),
            ],
            out_specs=pl.BlockSpec((bm * SUBLANES, LANES), lambda b, be, act: (b, 0)),
            scratch_shapes=[pltpu.VMEM((d, de), BF16), pltpu.VMEM((d, de), BF16),
                            pltpu.VMEM((de, d), BF16)],
        ),
        out_shape=jax.ShapeDtypeStruct((cap * SUBLANES, LANES), F32),
        compiler_params=_params("experts"),
        name="experts",
    )(block_expert, block_active, xs, w1, w3, w2)


def _final_kernel(d1_ref, d2_ref, x1_ref, route_ref, gt_ref, nfg_ref, ys_ref, o_ref, buf_ref, sem):
    tm = x1_ref.shape[0]
    i = pl.program_id(0)

    def gather(tile, s):
        t0 = tile * tm

        def start(r, carry):
            _row_copy(ys_ref, d1_ref[t0 + r], buf_ref.at[s, 0], r, sem.at[s, 0]).start(priority=0)
            _row_copy(ys_ref, d2_ref[t0 + r], buf_ref.at[s, 1], r, sem.at[s, 1]).start(priority=1)
            return carry

        lax.fori_loop(0, tm, start, 0, unroll=DMA_UNROLL)

    def wait_rows(s):
        for k in range(2):
            pltpu.make_async_copy(ys_ref.at[pl.ds(0, tm * SUBLANES), :], buf_ref.at[s, k],
                                  sem.at[s, k]).wait()

    def combine(cur, nxt):
        wait_rows(cur)
        last = pl.num_programs(0) - 1
        t0 = jnp.minimum(i + 1, last) * tm
        for r in range(tm):
            _row_copy(ys_ref, d1_ref[t0 + r], buf_ref.at[nxt, 0], r, sem.at[nxt, 0]).start(priority=0)
            _row_copy(ys_ref, d2_ref[t0 + r], buf_ref.at[nxt, 1], r, sem.at[nxt, 1]).start(priority=1)
        route = route_ref[...]
        moe = (_read_rows(buf_ref.at[cur, 0], tm) * route[:, 2:3]
               + _read_rows(buf_ref.at[cur, 1], tm) * route[:, 3:4])
        x2 = x1_ref[...] + gt_ref[...] * moe
        y = x2 * lax.rsqrt(jnp.mean(x2 * x2, axis=-1, keepdims=True) + EPS)
        o_ref[...] = y * nfg_ref[...]

        @pl.when(i == last)
        def _():
            wait_rows(nxt)

    @pl.when(i == 0)
    def _():
        gather(0, 0)

    @pl.when(i % 2 == 0)
    def _():
        combine(0, 1)

    @pl.when(i % 2 == 1)
    def _():
        combine(1, 0)


def _final(d1, d2, x1, route, gt2, nfg, ys, seq, tm):
    m, d = x1.shape
    per_batch = seq // tm
    return pl.pallas_call(
        _final_kernel,
        grid_spec=pltpu.PrefetchScalarGridSpec(
            num_scalar_prefetch=2,
            grid=(m // tm,),
            in_specs=[
                pl.BlockSpec((tm, d), lambda i, d1, d2: (i, 0)),
                pl.BlockSpec((tm, LANES), lambda i, d1, d2: (i, 0)),
                pl.BlockSpec((None, 1, d), lambda i, d1, d2: (i // per_batch, 0, 0)),
                pl.BlockSpec((1, d), lambda i, d1, d2: (0, 0)),
                pl.BlockSpec(memory_space=pl.ANY),
            ],
            out_specs=pl.BlockSpec((tm, d), lambda i, d1, d2: (i, 0)),
            scratch_shapes=[pltpu.VMEM((2, 2, tm * SUBLANES, LANES), F32),
                            pltpu.SemaphoreType.DMA((2, 2))],
        ),
        out_shape=jax.ShapeDtypeStruct((m, d), F32),
        compiler_params=_params("final"),
        name="final",
    )(d1, d2, x1, route, gt2, nfg, ys)


def _tile(n, pref):
    t = min(n, pref)
    assert n % t == 0
    return t


def kernel(x, c, w_ada, b_ada, norm1_g, w_in, conv_qkv_w, a_log, dt_bias, onorm_g, w_proj_a,
           conv_sc_w, w_proj_b, w_out, norm2_g, w_group, b_group, w_expert, b_expert, w1, w3, w2,
           normf_g):
    batch, seq, d = x.shape
    depth = w_ada.shape[0]
    m = batch * seq
    kd = N_HEADS * HEAD_D
    assert d == kd and seq % CHUNK == 0 and batch <= SUBLANES
    assert depth == 1, "the last stage applies the final rmsnorm: one layer only"
    c_pad = jnp.zeros((SUBLANES, d), F32).at[:batch].set(c)
    x2 = x.reshape(m, d)

    for l in range(depth):
        mod = _ada(c_pad, w_ada[l], b_ada[l][None, :])[:batch]
        sh1, sc1, gt1, sh2, sc2, gt2 = [mod[:, None, j * d:(j + 1) * d] for j in range(6)]

        w = w_in[l]
        o_ba = 3 * kd + kd
        proj, ba = _inproj(x2, norm1_g[l][None, :], sc1, sh1, jnp.swapaxes(w, 0, 1), o_ba // d, 2 * N_HEADS,
                           seq, _tile(seq, INPROJ_ROWS))

        head_params = jnp.zeros((SUBLANES, LANES), F32)
        head_params = head_params.at[0, N_HEADS:2 * N_HEADS].set(a_log[l])
        head_params = head_params.at[1, N_HEADS:2 * N_HEADS].set(dt_bias[l])
        head_params = head_params.at[2, :].set(onorm_g[l])
        conv_taps = jnp.broadcast_to(conv_qkv_w[l][:, None, :], (QKV_CONV, SUBLANES, 3 * kd))
        og = _gdn(proj, ba, conv_taps, head_params, batch, seq, _tile(seq, GDN_ROWS))

        mod_post = jnp.zeros((batch, SUBLANES, d), F32)
        mod_post = mod_post.at[:, 0:1].set(gt1).at[:, 1:2].set(sc2).at[:, 2:3].set(sh2)
        w_route = jnp.zeros((d, LANES), F32)
        w_route = w_route.at[:, :N_GROUPS].set(w_group[l]).at[:, N_GROUPS:N_GROUPS + N_EXPERTS].set(w_expert[l])
        w_route_hi = w_route.astype(BF16)
        w_route_lo = (w_route - w_route_hi.astype(F32)).astype(BF16)
        b_route = jnp.zeros((1, LANES), F32)
        b_route = b_route.at[0, :N_GROUPS].set(b_group[l]).at[0, N_GROUPS:N_GROUPS + N_EXPERTS].set(b_expert[l])
        x1, h2, route, counts = _post(
            x2, og, proj, mod_post, norm2_g[l][None, :],
            jnp.broadcast_to(conv_sc_w[l][:, None, :], (SC_CONV, SUBLANES, d)),
            w_proj_a[l].astype(BF16), w_proj_b[l].astype(BF16), w_out[l].astype(BF16),
            w_route_hi, w_route_lo, b_route, seq, _tile(seq, POST_ROWS))

        bm = EXPERT_BLOCK
        n_blocks = (2 * m) // bm + N_EXPERTS
        sizes = counts[0, :N_EXPERTS].astype(jnp.int32)
        padded = ((sizes + bm - 1) // bm) * bm
        pad_end = jnp.cumsum(padded)
        tail_start = jnp.concatenate([jnp.where(padded > 0, pad_end - bm, -1),
                                      pad_end[-1:] // bm]).astype(jnp.int32)
        block_row = jnp.arange(n_blocks, dtype=jnp.int32) * bm
        block_expert = jnp.minimum(jnp.sum(block_row[:, None] >= pad_end[None, :], axis=1),
                                   N_EXPERTS - 1).astype(jnp.int32)
        block_active = jnp.concatenate([(block_row < pad_end[-1]).astype(jnp.int32),
                                        (pad_end[-1:] // bm).astype(jnp.int32)])
        dest = _plan(route, counts, _tile(seq, INPROJ_ROWS))
        d1, d2 = dest[0], dest[1]

        xs = _dispatch(d1, d2, tail_start, h2, n_blocks * bm, _tile(seq, DISPATCH_ROWS))
        ys = _experts(block_expert, block_active, xs, w1[l], w3[l], w2[l])
        nfg = normf_g[None, :]
        x2 = _final(d1, d2, x1, route, gt2, nfg, ys, seq, _tile(seq, FINAL_ROWS))
    return x2.reshape(batch, seq, d)
```

```python
import functools

import jax
import jax.numpy as jnp
from jax import lax
from jax.experimental import pallas as pl
from jax.experimental.pallas import tpu as pltpu

F32 = jnp.float32
BF16 = jnp.bfloat16
HIGHEST = lax.Precision.HIGHEST

N_HEADS = 8
HEAD_D = 128
CHUNK = 64
QKV_CONV = 4
SC_CONV = 3
N_GROUPS = 4
EXPERTS_PER_GROUP = 8
N_EXPERTS = N_GROUPS * EXPERTS_PER_GROUP
EPS = 1e-6

LANES = 128
SUBLANES = 8
BF16_ROWS = 16
EXPERT_BLOCK = 512
INPROJ_ROWS = 2048
GDN_ROWS = 256
PRE_CHUNKS = 2
POST_ROWS = 512
DISPATCH_ROWS = 1024
FINAL_ROWS = 512
DMA_UNROLL = 8
MIB = 1024 * 1024
VMEM_MIB = {"ada": 24, "inproj": 56, "gdn": 48, "post": 56, "plan": 16, "dispatch": 24, "experts": 40,
            "final": 32}


def _sigmoid(x):
    return 0.5 + 0.5 * jnp.tanh(0.5 * x)


def _silu(x):
    half = 0.5 * x
    return half + half * jnp.tanh(half)


def _softplus(x):
    return jnp.maximum(x, 0.0) + jnp.log(1.0 + jnp.exp(-jnp.abs(x)))


def _dot(a, b):
    return jnp.dot(a, b, preferred_element_type=F32)


def _dot_nt(a, b):
    return lax.dot_general(a, b, (((1,), (1,)), ((), ())), preferred_element_type=F32)


def _dot_hi(a, b):
    return jnp.dot(a, b, preferred_element_type=F32, precision=HIGHEST)


def _dot_split(a, b_hi, b_lo):
    a_hi = a.astype(BF16)
    a_lo = (a - a_hi.astype(F32)).astype(BF16)
    return _dot(a_hi, b_hi) + (_dot(a_lo, b_hi) + _dot(a_hi, b_lo))


def _read_rows(ref, rows):
    return jnp.concatenate([ref[pl.ds(j, rows, stride=SUBLANES), :] for j in range(SUBLANES)], axis=1)


def _write_rows(ref, value):
    rows = value.shape[0]
    for j in range(SUBLANES):
        ref[pl.ds(j, rows, stride=SUBLANES), :] = value[:, j * LANES:(j + 1) * LANES]


def _row_tile(ref, row):
    if not isinstance(row, int):
        row = pl.multiple_of(row * SUBLANES, SUBLANES)
    else:
        row = row * SUBLANES
    return ref.at[pl.ds(row, SUBLANES), :]


def _params(call, n_grid_axes=1):
    return pltpu.CompilerParams(dimension_semantics=("arbitrary",) * n_grid_axes,
                                vmem_limit_bytes=VMEM_MIB[call] * MIB)


def _ada_kernel(c_ref, w_ref, b_ref, o_ref):
    o_ref[...] = _dot_hi(_silu(c_ref[...]), w_ref[...]) + b_ref[...]


def _ada(c_pad, w_ada, b_ada):
    d = c_pad.shape[1]
    n = w_ada.shape[1]
    return pl.pallas_call(
        _ada_kernel,
        grid=(n // d,),
        in_specs=[
            pl.BlockSpec((SUBLANES, d), lambda j: (0, 0)),
            pl.BlockSpec((d, d), lambda j: (0, j)),
            pl.BlockSpec((1, d), lambda j: (0, j)),
        ],
        out_specs=pl.BlockSpec((SUBLANES, d), lambda j: (0, j)),
        out_shape=jax.ShapeDtypeStruct((SUBLANES, n), F32),
        compiler_params=_params("ada"),
        name="ada",
    )(c_pad, w_ada, b_ada)


def _inproj_kernel(x_hbm, g_ref, sc_ref, sh_ref, wt_hbm, o_ref, ba_ref, x_ref, h_ref, w_ref, wba_ref, stage_ref,
                   stage_ba_ref, sem_x, sem, sem_ba, *, n_lead, n_ba):
    i, j = pl.program_id(0), pl.program_id(1)
    n_tiles = pl.num_programs(0)
    n_blocks, tn = w_ref.shape[0], o_ref.shape[1]
    tm, half = x_ref.shape[0], stage_ref.shape[1]

    def x_copy(tile):
        return pltpu.make_async_copy(x_hbm.at[pl.ds(pl.multiple_of(tile * tm, tm), tm)], x_ref, sem_x)

    def block_copy(jb, part):
        row = jb * tn + jnp.where(jb >= n_lead, n_ba, 0) + part * half
        return pltpu.make_async_copy(wt_hbm.at[pl.ds(pl.multiple_of(row, SUBLANES), half)],
                                     stage_ref.at[part], sem.at[part])

    def ba_copy():
        return pltpu.make_async_copy(wt_hbm.at[pl.ds(n_lead * tn, n_ba)], stage_ba_ref, sem_ba)

    @pl.when(j == 0)
    def _():
        @pl.when(i == 0)
        def _():
            x_copy(i).start()
            ba_copy().start()
            for part in range(2):
                block_copy(j, part).start()

        x_copy(i).wait()
        x = x_ref[...]
        y = x * lax.rsqrt(jnp.mean(x * x, axis=-1, keepdims=True) + EPS)
        h = (y * g_ref[...]) * (1.0 + sc_ref[...]) + sh_ref[...]
        hb = h.astype(BF16)
        h_ref[...] = hb

        @pl.when(i == 0)
        def _():
            ba_copy().wait()
            wba_ref[...] = jnp.zeros(wba_ref.shape, BF16)
            wba_ref[0:n_ba, :] = stage_ba_ref[...].astype(BF16)

        ba_ref[...] = _dot_nt(hb, wba_ref[...])

    @pl.when(jnp.logical_and(j == 1, i + 1 < n_tiles))
    def _():
        x_copy(i + 1).start()

    @pl.when(i == 0)
    def _():
        for part in range(2):
            block_copy(j, part).wait()
            w_ref[j, pl.ds(part * half, half), :] = stage_ref[part].astype(BF16)

            @pl.when(j + 1 < n_blocks)
            def _():
                block_copy(j + 1, part).start()

    rows = o_ref.shape[0] // 2
    for r in range(2):
        o_ref[pl.ds(r * rows, rows), :] = _dot_nt(h_ref[pl.ds(r * rows, rows), :], w_ref[j]).astype(BF16)


def _inproj(x2, norm_g, sc, sh, wt, n_lead, n_ba, seq, tm):
    m, d = x2.shape
    tn = d
    n = wt.shape[0] - n_ba
    assert n % tn == 0 and n_ba % (2 * SUBLANES) == 0 and n_ba <= LANES
    per_batch = seq // tm
    return pl.pallas_call(
        functools.partial(_inproj_kernel, n_lead=n_lead, n_ba=n_ba),
        grid=(m // tm, n // tn),
        in_specs=[
            pl.BlockSpec(memory_space=pl.ANY),
            pl.BlockSpec((1, d), lambda i, j: (0, 0)),
            pl.BlockSpec((None, 1, d), lambda i, j: (i // per_batch, 0, 0)),
            pl.BlockSpec((None, 1, d), lambda i, j: (i // per_batch, 0, 0)),
            pl.BlockSpec(memory_space=pl.ANY),
        ],
        out_specs=[
            pl.BlockSpec((tm, tn), lambda i, j: (i, j)),
            pl.BlockSpec((tm, LANES), lambda i, j: (i, 0)),
        ],
        out_shape=[
            jax.ShapeDtypeStruct((m, n), BF16),
            jax.ShapeDtypeStruct((m, LANES), F32),
        ],
        scratch_shapes=[
            pltpu.VMEM((tm, d), F32),
            pltpu.VMEM((tm, d), BF16),
            pltpu.VMEM((n // tn, tn, d), BF16),
            pltpu.VMEM((LANES, d), BF16),
            pltpu.VMEM((2, tn // 2, d), F32),
            pltpu.VMEM((n_ba, d), F32),
            pltpu.SemaphoreType.DMA(()),
            pltpu.SemaphoreType.DMA((2,)),
            pltpu.SemaphoreType.DMA(()),
        ],
        compiler_params=_params("inproj", 2),
        name="inproj",
    )(x2, norm_g, sc, sh, wt)


def _bmm(a, b):
    return jnp.einsum("hmk,hkn->hmn", a.astype(BF16), b.astype(BF16), preferred_element_type=F32)


def _bmm_nt(a, b):
    return jnp.einsum("hmk,hnk->hmn", a.astype(BF16), b.astype(BF16), preferred_element_type=F32)


def _unit_lower_inverse(a):
    row = lax.broadcasted_iota(jnp.int32, a.shape[1:], 0)
    col = lax.broadcasted_iota(jnp.int32, a.shape[1:], 1)
    apart = row ^ col
    eye = jnp.where(row == col, 1.0, 0.0).astype(F32)
    t = jnp.where(apart < 2, eye - a, 0.0)
    s = 2
    while s < CHUNK:
        coupling = jnp.where((apart >= s) & (apart < 2 * s), a, 0.0)
        t = t - _bmm(t, _bmm(coupling, t))
        s *= 2
    return t


def _lane_sums(x):
    h, rows, width = x.shape
    ones = jnp.ones((width, width), BF16)
    return _dot(x.reshape(h * rows, width).astype(BF16), ones).reshape(h, rows, width)


def _causal_conv_silu(win, cw, k_w):
    assert k_w == 4
    tiled = (win.shape[0] // SUBLANES, SUBLANES, win.shape[1])

    def pair(x, x1, j):
        return (x.reshape(tiled) * cw[j][None] + x1.reshape(tiled) * cw[j - 1][None]).reshape(win.shape)

    win1 = pltpu.roll(win, 1, 0)
    acc = pair(win, win1, 3) + pltpu.roll(pair(win, win1, 1), 2, 0)
    return _silu(acc[SUBLANES:, :])


def _gdn_kernel(q_ref, k_ref, v_ref, z_ref, ba_ref, cw_ref, hp_ref, o_ref,
                s_ref, tail_ref, wq_ref, u_ref, ik_ref, dec_ref):
    nb, tb = q_ref.shape[0], q_ref.shape[1]
    kd = N_HEADS * HEAD_D
    nbh = nb * N_HEADS

    @pl.when(pl.program_id(0) == 0)
    def _():
        s_ref[...] = jnp.zeros(s_ref.shape, F32)
        tail_ref[...] = jnp.zeros(tail_ref.shape, F32)

    row = lax.broadcasted_iota(jnp.int32, (CHUNK, CHUNK), 0)
    col = lax.broadcasted_iota(jnp.int32, (CHUNK, CHUNK), 1)
    causal = row >= col
    strict = row > col
    tril = jnp.where(causal, 1.0, 0.0).astype(F32)
    a_log = hp_ref[0:1, :]
    dt_bias = hp_ref[1:2, :]
    onorm_g = hp_ref[2:3, :]
    zeros_half = jnp.zeros((CHUNK, HEAD_D), F32)

    def precompute(cp, carry):
        qs, ks, vs, betas, gcs, grs, gls = [], [], [], [], [], [], []
        for sub in range(PRE_CHUNKS):
            base = pl.multiple_of((cp * PRE_CHUNKS + sub) * CHUNK, CHUNK)
            prev = pl.multiple_of(base - BF16_ROWS, BF16_ROWS)
            for b in range(nb):
                ba = ba_ref[b, pl.ds(base, CHUNK), :]
                beta_all = _sigmoid(ba)
                g_all = -jnp.exp(a_log) * _softplus(ba + dt_bias)
                gcum = _dot_hi(tril, g_all)
                gcum_t = jnp.concatenate([gcum, gcum], axis=0).T
                for h in range(N_HEADS):
                    lo, hi = h * HEAD_D, (h + 1) * HEAD_D

                    def conv(ref, off):
                        cur = ref[b, pl.ds(base, CHUNK), lo:hi].astype(F32)
                        if sub == 0:
                            before = tail_ref[b, :, off + lo:off + hi]
                        else:
                            before = ref[b, pl.ds(prev, BF16_ROWS), lo:hi].astype(F32)[SUBLANES:]
                        win = jnp.concatenate([before, cur], axis=0)
                        return _causal_conv_silu(win, cw_ref[:, :, off + lo:off + hi], QKV_CONV)

                    qs.append(conv(q_ref, 0))
                    ks.append(conv(k_ref, kd))
                    vs.append(conv(v_ref, 2 * kd))
                    betas.append(beta_all[:, h:h + 1])
                    gcs.append(gcum[:, N_HEADS + h:N_HEADS + h + 1])
                    grs.append(gcum_t[N_HEADS + h:N_HEADS + h + 1, 0:CHUNK])
                    gls.append(gcum[CHUNK - 1:CHUNK, N_HEADS + h:N_HEADS + h + 1])
        q, k, v = jnp.stack(qs), jnp.stack(ks), jnp.stack(vs)
        beta, gc, gr, gl = jnp.stack(betas), jnp.stack(gcs), jnp.stack(grs), jnp.stack(gls)
        qn = q * (lax.rsqrt(_lane_sums(q * q) + EPS) * (HEAD_D ** -0.5))
        kn = k * lax.rsqrt(_lane_sums(k * k) + EPS)
        decay = jnp.where(causal, jnp.exp(jnp.where(causal, gc - gr, 0.0)), 0.0)
        kb = kn * beta
        e_gc = jnp.exp(gc)
        kq = _bmm_nt(jnp.concatenate([kb, qn], axis=1), kn)
        a = jnp.where(strict, kq[:, :CHUNK] * decay, 0.0)
        intra = kq[:, CHUNK:] * decay
        uw = _bmm(_unit_lower_inverse(a), jnp.concatenate([v * beta, kb * e_gc], axis=2))
        wq = jnp.concatenate([uw[:, :, HEAD_D:], qn * e_gc], axis=1).astype(BF16)
        k_dec = kn * jnp.exp(gl - gc)
        k_dec_t = jnp.stack([jnp.concatenate([k_dec[i], zeros_half], axis=0).T[:, :CHUNK]
                             for i in range(PRE_CHUNKS * nbh)])
        ik = jnp.concatenate([intra, k_dec_t], axis=1).astype(BF16)
        dec = jnp.broadcast_to(jnp.exp(gl), (PRE_CHUNKS * nbh, 1, HEAD_D))
        for sub in range(PRE_CHUNKS):
            c = cp * PRE_CHUNKS + sub
            rows = slice(sub * nbh, (sub + 1) * nbh)
            u_ref[c] = uw[rows, :, :HEAD_D]
            wq_ref[c] = wq[rows]
            ik_ref[c] = ik[rows]
            dec_ref[c] = dec[rows]

        last = pl.multiple_of((cp + 1) * PRE_CHUNKS * CHUNK - BF16_ROWS, BF16_ROWS)
        for b in range(nb):
            for j, ref in enumerate((q_ref, k_ref, v_ref)):
                rows = ref[b, pl.ds(last, BF16_ROWS), :].astype(F32)
                tail_ref[b, :, j * kd:(j + 1) * kd] = rows[BF16_ROWS - SUBLANES:, :]
        return carry

    def recur(c, carry):
        base = pl.multiple_of(c * CHUNK, CHUNK)
        state = s_ref[...]
        ws = _bmm(wq_ref[c], state)
        v_new = u_ref[c] - ws[:, :CHUNK]
        r = _bmm(ik_ref[c], v_new)
        o = ws[:, CHUNK:] + r[:, :CHUNK]
        s_ref[...] = state * dec_ref[c] + r[:, CHUNK:]
        on = o * lax.rsqrt(jnp.mean(o * o, axis=-1, keepdims=True) + EPS) * onorm_g
        for b in range(nb):
            for h in range(N_HEADS):
                lo, hi = h * HEAD_D, (h + 1) * HEAD_D
                z = z_ref[b, pl.ds(base, CHUNK), lo:hi].astype(F32)
                o_ref[b, pl.ds(base, CHUNK), lo:hi] = (on[b * N_HEADS + h] * _silu(z)).astype(BF16)
        return carry

    lax.fori_loop(0, tb // (PRE_CHUNKS * CHUNK), precompute, 0)
    lax.fori_loop(0, tb // CHUNK, recur, 0)


def _gdn(proj, ba, conv_w, head_params, batch, seq, tb):
    kd = N_HEADS * HEAD_D
    nc = tb // CHUNK
    nbh = batch * N_HEADS
    proj3 = proj.reshape(batch, seq, proj.shape[1])
    ba3 = ba.reshape(batch, seq, LANES)

    def col(j):
        return pl.BlockSpec((batch, tb, kd), lambda t: (0, t, j))

    out = pl.pallas_call(
        _gdn_kernel,
        grid=(seq // tb,),
        in_specs=[
            col(0), col(1), col(2), col(3),
            pl.BlockSpec((batch, tb, LANES), lambda t: (0, t, 0)),
            pl.BlockSpec((QKV_CONV, SUBLANES, 3 * kd), lambda t: (0, 0, 0)),
            pl.BlockSpec((SUBLANES, LANES), lambda t: (0, 0)),
        ],
        out_specs=pl.BlockSpec((batch, tb, kd), lambda t: (0, t, 0)),
        out_shape=jax.ShapeDtypeStruct((batch, seq, kd), BF16),
        scratch_shapes=[
            pltpu.VMEM((nbh, HEAD_D, HEAD_D), F32),
            pltpu.VMEM((batch, SUBLANES, 3 * kd), F32),
            pltpu.VMEM((nc, nbh, 2 * CHUNK, HEAD_D), BF16),
            pltpu.VMEM((nc, nbh, CHUNK, HEAD_D), F32),
            pltpu.VMEM((nc, nbh, CHUNK + HEAD_D, CHUNK), BF16),
            pltpu.VMEM((nc, nbh, 1, HEAD_D), F32),
        ],
        compiler_params=_params("gdn"),
        name="gdn",
    )(proj3, proj3, proj3, proj3, ba3, conv_w, head_params)
    return out.reshape(batch * seq, kd)


def _post_kernel(x_ref, og_ref, sb_ref, sc_ref, sx_ref, ga_ref, gb_ref, mod_ref, n2g_ref, cw_ref,
                 wpa_ref, wpb_ref, wout_ref, wrh_ref, wrl_ref, br_ref,
                 x1_ref, h2_ref, route_ref, cnt_ref, win_ref, run_ref, *, per_batch):
    tm = x_ref.shape[0]
    i = pl.program_id(0)

    @pl.when(i == 0)
    def _():
        run_ref[...] = jnp.zeros(run_ref.shape, F32)

    @pl.when(i % per_batch == 0)
    def _():
        win_ref[...] = jnp.zeros(win_ref.shape, F32)

    assert SC_CONV == 3
    prod = sc_ref[...].astype(F32) * sx_ref[...].astype(F32)
    win = jnp.concatenate([win_ref[...], prod], axis=0)
    win_ref[...] = prod[tm - SUBLANES:, :]
    tiled = (win.shape[0] // SUBLANES, SUBLANES, win.shape[1])
    win1 = pltpu.roll(win, 1, 0)
    conv = (win.reshape(tiled) * cw_ref[2][None] + win1.reshape(tiled) * cw_ref[1][None]
            + pltpu.roll(win1, 1, 0).reshape(tiled) * cw_ref[0][None]).reshape(win.shape)[SUBLANES:, :]
    y_b = _dot((sb_ref[...].astype(F32) * conv).astype(BF16), wpb_ref[...])
    y_a = _dot(og_ref[...], wpa_ref[...])
    merged = _sigmoid(ga_ref[...].astype(F32)) * y_a + _sigmoid(gb_ref[...].astype(F32)) * y_b
    mix = _dot(merged.astype(BF16), wout_ref[...])
    x1 = x_ref[...] + mod_ref[0:1, :] * mix
    x1_ref[...] = x1

    y = x1 * lax.rsqrt(jnp.mean(x1 * x1, axis=-1, keepdims=True) + EPS)
    h2 = (y * n2g_ref[...]) * (1.0 + mod_ref[1:2, :]) + mod_ref[2:3, :]
    _write_rows(h2_ref, h2)

    lg = _dot_split(h2, wrh_ref[...], wrl_ref[...]) + br_ref[...]
    lane = lax.broadcasted_iota(jnp.int32, lg.shape, 1).astype(F32)
    neg = jnp.float32(-jnp.inf)
    big = jnp.float32(2 * LANES)

    def first_max(mask):
        vmax = jnp.max(jnp.where(mask, lg, neg), axis=-1, keepdims=True)
        idx = jnp.min(jnp.where(mask & (lg == vmax), lane, big), axis=-1, keepdims=True)
        return vmax, idx

    gmask = lane < N_GROUPS
    g_max, g_sel = first_max(gmask)
    p_group = 1.0 / jnp.sum(jnp.where(gmask, jnp.exp(lg - g_max), 0.0), axis=-1, keepdims=True)
    e_lo = N_GROUPS + EXPERTS_PER_GROUP * g_sel
    emask = (lane >= e_lo) & (lane < e_lo + EXPERTS_PER_GROUP)
    v1, i1 = first_max(emask)
    v2, i2 = first_max(emask & (lane != i1))
    ex = jnp.exp(v2 - v1)
    w1 = p_group * (1.0 / (1.0 + ex))
    w2 = p_group * (ex / (1.0 + ex))
    e1 = i1 - N_GROUPS
    e2 = i2 - N_GROUPS

    onehot = jnp.where((lane == e1) | (lane == e2), 1.0, 0.0).astype(F32)
    row = lax.broadcasted_iota(jnp.int32, (tm, tm), 0)
    col = lax.broadcasted_iota(jnp.int32, (tm, tm), 1)
    before = jnp.where(row > col, 1.0, 0.0).astype(BF16)
    seen = _dot(before, onehot.astype(BF16)) + run_ref[0:1, :]
    r1 = jnp.sum(jnp.where(lane == e1, seen, 0.0), axis=-1, keepdims=True)
    r2 = jnp.sum(jnp.where(lane == e2, seen, 0.0), axis=-1, keepdims=True)
    run_ref[0:1, :] = run_ref[0:1, :] + jnp.sum(onehot, axis=0, keepdims=True)
    cnt_ref[...] = jnp.broadcast_to(run_ref[0:1, :], cnt_ref.shape)

    out = jnp.where(lane == 0, e1, 0.0)
    out = jnp.where(lane == 1, e2, out)
    out = jnp.where(lane == 2, w1, out)
    out = jnp.where(lane == 3, w2, out)
    out = jnp.where(lane == 4, r1, out)
    out = jnp.where(lane == 5, r2, out)
    route_ref[...] = out


def _post(x2, og, proj, mod, n2g, conv_w, wpa, wpb, wout, w_route_hi, w_route_lo, b_route, seq, tm):
    m, d = x2.shape
    assert d == SUBLANES * LANES
    per_batch = seq // tm

    def rows(j):
        return pl.BlockSpec((tm, d), lambda i: (i, j))

    def whole(shape):
        return pl.BlockSpec(shape, lambda i: tuple(0 for _ in shape))

    return pl.pallas_call(
        functools.partial(_post_kernel, per_batch=per_batch),
        grid=(m // tm,),
        in_specs=[
            rows(0), rows(0), rows(4), rows(5), rows(6), rows(7), rows(8),
            pl.BlockSpec((None, SUBLANES, d), lambda i: (i // per_batch, 0, 0)),
            whole((1, d)), whole((SC_CONV, SUBLANES, d)),
            whole((d, d)), whole((d, d)), whole((d, d)),
            whole((d, LANES)), whole((d, LANES)), whole((1, LANES)),
        ],
        out_specs=[
            rows(0), pl.BlockSpec((tm * SUBLANES, LANES), lambda i: (i, 0)),
            pl.BlockSpec((tm, LANES), lambda i: (i, 0)),
            pl.BlockSpec((SUBLANES, LANES), lambda i: (0, 0)),
        ],
        out_shape=[
            jax.ShapeDtypeStruct((m, d), F32),
            jax.ShapeDtypeStruct((m * SUBLANES, LANES), F32),
            jax.ShapeDtypeStruct((m, LANES), F32),
            jax.ShapeDtypeStruct((SUBLANES, LANES), F32),
        ],
        scratch_shapes=[
            pltpu.VMEM((SUBLANES, d), F32),
            pltpu.VMEM((SUBLANES, LANES), F32),
        ],
        compiler_params=_params("post"),
        name="post",
    )(x2, og, proj, proj, proj, proj, proj, mod, n2g, conv_w, wpa, wpb, wout, w_route_hi, w_route_lo,
      b_route)


def _plan_kernel(route_ref, cnt_ref, d_ref):
    bm = EXPERT_BLOCK
    sizes = cnt_ref[...]
    padded = jnp.floor((sizes + (bm - 1.0)) * (1.0 / bm)) * bm
    lane_i = lax.broadcasted_iota(jnp.int32, sizes.shape, 1)
    incl = padded
    s = 1
    while s < LANES:
        incl = incl + jnp.where(lane_i >= s, pltpu.roll(incl, s, 1), 0.0)
        s *= 2
    start = (incl - padded)[0:1, :]
    r = route_ref[...]
    lane = lax.broadcasted_iota(jnp.int32, r.shape, 1).astype(F32)
    d1 = jnp.sum(jnp.where(lane == r[:, 0:1], start, 0.0), axis=-1, keepdims=True) + r[:, 4:5]
    d2 = jnp.sum(jnp.where(lane == r[:, 1:2], start, 0.0), axis=-1, keepdims=True) + r[:, 5:6]
    out = jnp.where(lane == 0.0, d1, jnp.where(lane == 1.0, d2, 0.0))
    d_ref[...] = out.T[0:SUBLANES, :].astype(jnp.int32)


def _plan(route, counts, tm):
    m = route.shape[0]
    return pl.pallas_call(
        _plan_kernel,
        grid=(m // tm,),
        in_specs=[
            pl.BlockSpec((tm, LANES), lambda i: (i, 0)),
            pl.BlockSpec((SUBLANES, LANES), lambda i: (0, 0)),
        ],
        out_specs=pl.BlockSpec((SUBLANES, tm), lambda i: (0, i)),
        out_shape=jax.ShapeDtypeStruct((SUBLANES, m), jnp.int32),
        compiler_params=_params("plan"),
        name="plan",
    )(route, counts)


def _row_copy(src_ref, src_row, dst_ref, dst_row, sem):
    return pltpu.make_async_copy(_row_tile(src_ref, src_row), _row_tile(dst_ref, dst_row), sem)


def _rows_copy(src_ref, dst_ref, dst_row, n, sem):
    return pltpu.make_async_copy(src_ref, dst_ref.at[pl.ds(dst_row * SUBLANES, n * SUBLANES), :], sem)


def _dispatch_kernel(d1_ref, d2_ref, tail_ref, h2_ref, xs_ref, zero_ref, sem):
    tm = h2_ref.shape[0] // SUBLANES
    bm = zero_ref.shape[0] // SUBLANES
    t0 = pl.program_id(0) * tm

    @pl.when(pl.program_id(0) == 0)
    def _():
        zero_ref[...] = jnp.zeros(zero_ref.shape, F32)
        n_blocks = xs_ref.shape[0] // (bm * SUBLANES)

        def zero_block(row):
            return _rows_copy(zero_ref, xs_ref, pl.multiple_of(row, bm), bm, sem.at[0])

        for e in range(N_EXPERTS):
            @pl.when(tail_ref[e] >= 0)
            def _():
                zero_block(tail_ref[e]).start()
        lax.fori_loop(tail_ref[N_EXPERTS], n_blocks, lambda j, c: (zero_block(j * bm).start(), c)[1], 0)
        for e in range(N_EXPERTS):
            @pl.when(tail_ref[e] >= 0)
            def _():
                zero_block(tail_ref[e]).wait()
        lax.fori_loop(tail_ref[N_EXPERTS], n_blocks, lambda j, c: (zero_block(j * bm).wait(), c)[1], 0)

    def start(r, carry):
        _row_copy(h2_ref, r, xs_ref, d1_ref[t0 + r], sem.at[0]).start(priority=0)
        _row_copy(h2_ref, r, xs_ref, d2_ref[t0 + r], sem.at[1]).start(priority=1)
        return carry

    lax.fori_loop(0, tm, start, 0, unroll=DMA_UNROLL)
    _rows_copy(h2_ref, xs_ref, 0, tm, sem.at[0]).wait()
    _rows_copy(h2_ref, xs_ref, 0, tm, sem.at[1]).wait()


def _dispatch(d1, d2, tail_start, h2, cap, tm):
    m = h2.shape[0] // SUBLANES
    return pl.pallas_call(
        _dispatch_kernel,
        grid_spec=pltpu.PrefetchScalarGridSpec(
            num_scalar_prefetch=3,
            grid=(m // tm,),
            in_specs=[pl.BlockSpec((tm * SUBLANES, LANES), lambda i, d1, d2, tl: (i, 0))],
            out_specs=pl.BlockSpec(memory_space=pl.ANY),
            scratch_shapes=[pltpu.VMEM((EXPERT_BLOCK * SUBLANES, LANES), F32),
                            pltpu.SemaphoreType.DMA((2,))],
        ),
        out_shape=jax.ShapeDtypeStruct((cap * SUBLANES, LANES), F32),
        compiler_params=_params("dispatch"),
        name="dispatch",
    )(d1, d2, tail_start, h2)


def _expert_kernel(be_ref, act_ref, nxt_ref, x_ref, w1_hbm, w3_hbm, w2_hbm, y_ref, w1b_ref, w3b_ref, w2b_ref,
                   w1s_ref, w3s_ref, w2s_ref, sem):
    b = pl.program_id(0)
    weights = ((w1_hbm, w1s_ref, w1b_ref), (w3_hbm, w3s_ref, w3b_ref), (w2_hbm, w2s_ref, w2b_ref))

    def weight_copy(k, e):
        w_hbm, ws_ref, _ = weights[k]
        return pltpu.make_async_copy(w_hbm.at[e], ws_ref, sem.at[k])

    @pl.when(act_ref[b] > 0)
    def _():
        @pl.when(b == 0)
        def _():
            for k in range(len(weights)):
                weight_copy(k, be_ref[b]).start()

        @pl.when((b == 0) | (be_ref[b] != be_ref[jnp.maximum(b - 1, 0)]))
        def _():
            for k, (_, ws_ref, wb_ref) in enumerate(weights):
                weight_copy(k, be_ref[b]).wait()
                wb_ref[...] = ws_ref[...].astype(BF16)

                @pl.when(nxt_ref[b] >= 0)
                def _():
                    weight_copy(k, nxt_ref[b]).start()

        xb = _read_rows(x_ref, x_ref.shape[0] // SUBLANES).astype(BF16)
        hid = _silu(_dot(xb, w1b_ref[...])) * _dot(xb, w3b_ref[...])
        _write_rows(y_ref, _dot(hid.astype(BF16), w2b_ref[...]))

    @pl.when(act_ref[b] == 0)
    def _():
        y_ref[...] = jnp.zeros(y_ref.shape, F32)


def _experts(block_expert, block_active, block_next, xs, w1, w3, w2):
    cap = xs.shape[0] // SUBLANES
    d, de = w1.shape[1], w1.shape[2]
    bm = EXPERT_BLOCK

    def x_block(b, be, act, nxt):
        return (jnp.minimum(b, jnp.maximum(act[cap // bm], 1) - 1), 0)

    return pl.pallas_call(
        _expert_kernel,
        grid_spec=pltpu.PrefetchScalarGridSpec(
            num_scalar_prefetch=3,
            grid=(cap // bm,),
            in_specs=[
                pl.BlockSpec((bm * SUBLANES, LANES), x_block),
                pl.BlockSpec(memory_space=pl.ANY),
                pl.BlockSpec(memory_space=pl.ANY),
                pl.BlockSpec(memory_space=pl.ANY),
            ],
            out_specs=pl.BlockSpec((bm * SUBLANES, LANES), lambda b, be, act, nxt: (b, 0)),
            scratch_shapes=[pltpu.VMEM((d, de), BF16), pltpu.VMEM((d, de), BF16), pltpu.VMEM((de, d), BF16),
                            pltpu.VMEM((d, de), F32), pltpu.VMEM((d, de), F32), pltpu.VMEM((de, d), F32),
                            pltpu.SemaphoreType.DMA((3,))],
        ),
        out_shape=jax.ShapeDtypeStruct((cap * SUBLANES, LANES), F32),
        compiler_params=_params("experts"),
        name="experts",
    )(block_expert, block_active, block_next, xs, w1, w3, w2)


def _final_kernel(d1_ref, d2_ref, x1_ref, route_ref, gt_ref, nfg_ref, ys_ref, o_ref, buf_ref, sem):
    tm = x1_ref.shape[0]
    i = pl.program_id(0)

    def gather(tile, s):
        t0 = tile * tm

        def start(r, carry):
            _row_copy(ys_ref, d1_ref[t0 + r], buf_ref.at[s, 0], r, sem.at[s, 0]).start(priority=0)
            _row_copy(ys_ref, d2_ref[t0 + r], buf_ref.at[s, 1], r, sem.at[s, 1]).start(priority=1)
            return carry

        lax.fori_loop(0, tm, start, 0, unroll=DMA_UNROLL)

    def wait_rows(s):
        for k in range(2):
            pltpu.make_async_copy(ys_ref.at[pl.ds(0, tm * SUBLANES), :], buf_ref.at[s, k],
                                  sem.at[s, k]).wait()

    def combine(cur, nxt):
        wait_rows(cur)
        last = pl.num_programs(0) - 1
        t0 = jnp.minimum(i + 1, last) * tm
        for r in range(tm):
            _row_copy(ys_ref, d1_ref[t0 + r], buf_ref.at[nxt, 0], r, sem.at[nxt, 0]).start(priority=0)
            _row_copy(ys_ref, d2_ref[t0 + r], buf_ref.at[nxt, 1], r, sem.at[nxt, 1]).start(priority=1)
        route = route_ref[...]
        moe = (_read_rows(buf_ref.at[cur, 0], tm) * route[:, 2:3]
               + _read_rows(buf_ref.at[cur, 1], tm) * route[:, 3:4])
        x2 = x1_ref[...] + gt_ref[...] * moe
        y = x2 * lax.rsqrt(jnp.mean(x2 * x2, axis=-1, keepdims=True) + EPS)
        o_ref[...] = y * nfg_ref[...]

        @pl.when(i == last)
        def _():
            wait_rows(nxt)

    @pl.when(i == 0)
    def _():
        gather(0, 0)

    @pl.when(i % 2 == 0)
    def _():
        combine(0, 1)

    @pl.when(i % 2 == 1)
    def _():
        combine(1, 0)


def _final(d1, d2, x1, route, gt2, nfg, ys, seq, tm):
    m, d = x1.shape
    per_batch = seq // tm
    return pl.pallas_call(
        _final_kernel,
        grid_spec=pltpu.PrefetchScalarGridSpec(
            num_scalar_prefetch=2,
            grid=(m // tm,),
            in_specs=[
                pl.BlockSpec((tm, d), lambda i, d1, d2: (i, 0)),
                pl.BlockSpec((tm, LANES), lambda i, d1, d2: (i, 0)),
                pl.BlockSpec((None, 1, d), lambda i, d1, d2: (i // per_batch, 0, 0)),
                pl.BlockSpec((1, d), lambda i, d1, d2: (0, 0)),
                pl.BlockSpec(memory_space=pl.ANY),
            ],
            out_specs=pl.BlockSpec((tm, d), lambda i, d1, d2: (i, 0)),
            scratch_shapes=[pltpu.VMEM((2, 2, tm * SUBLANES, LANES), F32),
                            pltpu.SemaphoreType.DMA((2, 2))],
        ),
        out_shape=jax.ShapeDtypeStruct((m, d), F32),
        compiler_params=_params("final"),
        name="final",
    )(d1, d2, x1, route, gt2, nfg, ys)


def _tile(n, pref):
    t = min(n, pref)
    assert n % t == 0
    return t


def kernel(x, c, w_ada, b_ada, norm1_g, w_in, conv_qkv_w, a_log, dt_bias, onorm_g, w_proj_a,
           conv_sc_w, w_proj_b, w_out, norm2_g, w_group, b_group, w_expert, b_expert, w1, w3, w2,
           normf_g):
    batch, seq, d = x.shape
    depth = w_ada.shape[0]
    m = batch * seq
    kd = N_HEADS * HEAD_D
    assert d == kd and seq % CHUNK == 0 and batch <= SUBLANES
    assert depth == 1, "the last stage applies the final rmsnorm: one layer only"
    c_pad = jnp.zeros((SUBLANES, d), F32).at[:batch].set(c)
    x2 = x.reshape(m, d)

    for l in range(depth):
        mod = _ada(c_pad, w_ada[l], b_ada[l][None, :])[:batch]
        sh1, sc1, gt1, sh2, sc2, gt2 = [mod[:, None, j * d:(j + 1) * d] for j in range(6)]

        w = w_in[l]
        o_ba = 3 * kd + kd
        proj, ba = _inproj(x2, norm1_g[l][None, :], sc1, sh1, jnp.swapaxes(w, 0, 1), o_ba // d, 2 * N_HEADS,
                           seq, _tile(seq, INPROJ_ROWS))

        head_params = jnp.zeros((SUBLANES, LANES), F32)
        head_params = head_params.at[0, N_HEADS:2 * N_HEADS].set(a_log[l])
        head_params = head_params.at[1, N_HEADS:2 * N_HEADS].set(dt_bias[l])
        head_params = head_params.at[2, :].set(onorm_g[l])
        conv_taps = jnp.broadcast_to(conv_qkv_w[l][:, None, :], (QKV_CONV, SUBLANES, 3 * kd))
        og = _gdn(proj, ba, conv_taps, head_params, batch, seq, _tile(seq, GDN_ROWS))

        mod_post = jnp.zeros((batch, SUBLANES, d), F32)
        mod_post = mod_post.at[:, 0:1].set(gt1).at[:, 1:2].set(sc2).at[:, 2:3].set(sh2)
        w_route = jnp.zeros((d, LANES), F32)
        w_route = w_route.at[:, :N_GROUPS].set(w_group[l]).at[:, N_GROUPS:N_GROUPS + N_EXPERTS].set(w_expert[l])
        w_route_hi = w_route.astype(BF16)
        w_route_lo = (w_route - w_route_hi.astype(F32)).astype(BF16)
        b_route = jnp.zeros((1, LANES), F32)
        b_route = b_route.at[0, :N_GROUPS].set(b_group[l]).at[0, N_GROUPS:N_GROUPS + N_EXPERTS].set(b_expert[l])
        x1, h2, route, counts = _post(
            x2, og, proj, mod_post, norm2_g[l][None, :],
            jnp.broadcast_to(conv_sc_w[l][:, None, :], (SC_CONV, SUBLANES, d)),
            w_proj_a[l].astype(BF16), w_proj_b[l].astype(BF16), w_out[l].astype(BF16),
            w_route_hi, w_route_lo, b_route, seq, _tile(seq, POST_ROWS))

        bm = EXPERT_BLOCK
        n_blocks = (2 * m) // bm + N_EXPERTS
        sizes = counts[0, :N_EXPERTS].astype(jnp.int32)
        padded = ((sizes + bm - 1) // bm) * bm
        pad_end = jnp.cumsum(padded)
        tail_start = jnp.concatenate([jnp.where(padded > 0, pad_end - bm, -1),
                                      pad_end[-1:] // bm]).astype(jnp.int32)
        block_row = jnp.arange(n_blocks, dtype=jnp.int32) * bm
        block_expert = jnp.minimum(jnp.sum(block_row[:, None] >= pad_end[None, :], axis=1),
                                   N_EXPERTS - 1).astype(jnp.int32)
        block_active = jnp.concatenate([(block_row < pad_end[-1]).astype(jnp.int32),
                                        (pad_end[-1:] // bm).astype(jnp.int32)])
        expert_id = jnp.arange(N_EXPERTS, dtype=jnp.int32)
        later = (expert_id[None, :] > expert_id[:, None]) & (padded[None, :] > 0)
        next_expert = jnp.min(jnp.where(later, expert_id[None, :], N_EXPERTS), axis=1)
        block_next = jnp.where(next_expert < N_EXPERTS, next_expert, -1)[block_expert].astype(jnp.int32)
        dest = _plan(route, counts, _tile(seq, INPROJ_ROWS))
        d1, d2 = dest[0], dest[1]

        xs = _dispatch(d1, d2, tail_start, h2, n_blocks * bm, _tile(seq, DISPATCH_ROWS))
        ys = _experts(block_expert, block_active, block_next, xs, w1[l], w3[l], w2[l])
        nfg = normf_g[None, :]
        x2 = _final(d1, d2, x1, route, gt2, nfg, ys, seq, _tile(seq, FINAL_ROWS))
    return x2.reshape(batch, seq, d)
```

```python
import functools

import jax
import jax.numpy as jnp
from jax import lax
from jax.experimental import pallas as pl
from jax.experimental.pallas import tpu as pltpu

F32 = jnp.float32
BF16 = jnp.bfloat16
HIGHEST = lax.Precision.HIGHEST

N_HEADS = 8
HEAD_D = 128
CHUNK = 64
QKV_CONV = 4
SC_CONV = 3
N_GROUPS = 4
EXPERTS_PER_GROUP = 8
N_EXPERTS = N_GROUPS * EXPERTS_PER_GROUP
EPS = 1e-6

LANES = 128
SUBLANES = 8
BF16_ROWS = 16
EXPERT_BLOCK = 512
INPROJ_ROWS = 2048
GDN_ROWS = 256
PRE_CHUNKS = 2
POST_ROWS = 512
DISPATCH_ROWS = 1024
FINAL_ROWS = 512
DMA_UNROLL = 8
MIB = 1024 * 1024
VMEM_MIB = {"ada": 24, "inproj": 56, "gdn": 48, "post": 56, "plan": 16, "dispatch": 24, "experts": 40,
            "final": 32}


def _sigmoid(x):
    return 0.5 + 0.5 * jnp.tanh(0.5 * x)


def _silu(x):
    half = 0.5 * x
    return half + half * jnp.tanh(half)


def _softplus(x):
    return jnp.maximum(x, 0.0) + jnp.log(1.0 + jnp.exp(-jnp.abs(x)))


def _dot(a, b):
    return jnp.dot(a, b, preferred_element_type=F32)


def _dot_nt(a, b):
    return lax.dot_general(a, b, (((1,), (1,)), ((), ())), preferred_element_type=F32)


def _dot_hi(a, b):
    return jnp.dot(a, b, preferred_element_type=F32, precision=HIGHEST)


def _dot_split(a, b_hi, b_lo):
    a_hi = a.astype(BF16)
    a_lo = (a - a_hi.astype(F32)).astype(BF16)
    return _dot(a_hi, b_hi) + (_dot(a_lo, b_hi) + _dot(a_hi, b_lo))


def _read_rows(ref, rows):
    return jnp.concatenate([ref[pl.ds(j, rows, stride=SUBLANES), :] for j in range(SUBLANES)], axis=1)


def _write_rows(ref, value):
    rows = value.shape[0]
    for j in range(SUBLANES):
        ref[pl.ds(j, rows, stride=SUBLANES), :] = value[:, j * LANES:(j + 1) * LANES]


def _row_tile(ref, row):
    if not isinstance(row, int):
        row = pl.multiple_of(row * SUBLANES, SUBLANES)
    else:
        row = row * SUBLANES
    return ref.at[pl.ds(row, SUBLANES), :]


def _params(call, n_grid_axes=1):
    return pltpu.CompilerParams(dimension_semantics=("arbitrary",) * n_grid_axes,
                                vmem_limit_bytes=VMEM_MIB[call] * MIB)


def _ada_kernel(cb_ref, w_ref, b_ref, o_ref, s_ref):
    batch, d, _ = cb_ref.shape
    groups = w_ref.shape[1] // LANES

    @pl.when(pl.program_id(0) == 0)
    def _():
        s_ref[...] = _silu(cb_ref[...])

    def body(kc, accs):
        k0 = pl.multiple_of(kc * SUBLANES, SUBLANES)
        w = w_ref[pl.ds(k0, SUBLANES), :]
        out = []
        for b in range(batch):
            s = s_ref[b, pl.ds(k0, SUBLANES), :]
            out += [accs[b * groups + g] + w[:, g * LANES:(g + 1) * LANES] * s for g in range(groups)]
        return tuple(out)

    zero = jnp.zeros((SUBLANES, LANES), F32)
    accs = lax.fori_loop(0, d // SUBLANES, body, (zero,) * (batch * groups), unroll=4)
    o_ref[...] = jnp.zeros(o_ref.shape, F32)
    for b in range(batch):
        for g in range(groups):
            cols = slice(g * LANES, (g + 1) * LANES)
            o_ref[b:b + 1, cols] = jnp.sum(accs[b * groups + g], axis=0, keepdims=True) + b_ref[:, cols]


def _ada(c, w_ada, b_ada):
    batch, d = c.shape
    n = w_ada.shape[1]
    cb = jnp.broadcast_to(c[:, :, None], (batch, d, LANES))
    return pl.pallas_call(
        _ada_kernel,
        grid=(n // d,),
        in_specs=[
            pl.BlockSpec((batch, d, LANES), lambda j: (0, 0, 0)),
            pl.BlockSpec((d, d), lambda j: (0, j)),
            pl.BlockSpec((1, d), lambda j: (0, j)),
        ],
        out_specs=pl.BlockSpec((SUBLANES, d), lambda j: (0, j)),
        out_shape=jax.ShapeDtypeStruct((SUBLANES, n), F32),
        scratch_shapes=[pltpu.VMEM((batch, d, LANES), F32)],
        compiler_params=_params("ada"),
        name="ada",
    )(cb, w_ada, b_ada)


def _inproj_kernel(x_hbm, g_ref, sc_ref, sh_ref, wt_hbm, o_ref, ba_ref, x_ref, h_ref, w_ref, wba_ref, stage_ref,
                   stage_ba_ref, sem_x, sem, sem_ba, *, n_lead, n_ba):
    i, j = pl.program_id(0), pl.program_id(1)
    n_tiles = pl.num_programs(0)
    n_blocks, tn = w_ref.shape[0], o_ref.shape[1]
    tm, half = x_ref.shape[0], stage_ref.shape[1]

    def x_copy(tile):
        return pltpu.make_async_copy(x_hbm.at[pl.ds(pl.multiple_of(tile * tm, tm), tm)], x_ref, sem_x)

    def block_copy(jb, part):
        row = jb * tn + jnp.where(jb >= n_lead, n_ba, 0) + part * half
        return pltpu.make_async_copy(wt_hbm.at[pl.ds(pl.multiple_of(row, SUBLANES), half)],
                                     stage_ref.at[part], sem.at[part])

    def ba_copy():
        return pltpu.make_async_copy(wt_hbm.at[pl.ds(n_lead * tn, n_ba)], stage_ba_ref, sem_ba)

    @pl.when(j == 0)
    def _():
        @pl.when(i == 0)
        def _():
            x_copy(i).start()
            ba_copy().start()
            for part in range(2):
                block_copy(j, part).start()

        x_copy(i).wait()
        x = x_ref[...]
        y = x * lax.rsqrt(jnp.mean(x * x, axis=-1, keepdims=True) + EPS)
        h = (y * g_ref[...]) * (1.0 + sc_ref[...]) + sh_ref[...]
        hb = h.astype(BF16)
        h_ref[...] = hb

        @pl.when(i == 0)
        def _():
            ba_copy().wait()
            wba_ref[...] = jnp.zeros(wba_ref.shape, BF16)
            wba_ref[0:n_ba, :] = stage_ba_ref[...].astype(BF16)

        ba_ref[...] = _dot_nt(hb, wba_ref[...])

    @pl.when(jnp.logical_and(j == 1, i + 1 < n_tiles))
    def _():
        x_copy(i + 1).start()

    @pl.when(i == 0)
    def _():
        for part in range(2):
            block_copy(j, part).wait()
            w_ref[j, pl.ds(part * half, half), :] = stage_ref[part].astype(BF16)

            @pl.when(j + 1 < n_blocks)
            def _():
                block_copy(j + 1, part).start()

    rows = o_ref.shape[0] // 2
    for r in range(2):
        o_ref[pl.ds(r * rows, rows), :] = _dot_nt(h_ref[pl.ds(r * rows, rows), :], w_ref[j]).astype(BF16)


def _inproj(x2, norm_g, sc, sh, wt, n_lead, n_ba, seq, tm):
    m, d = x2.shape
    tn = d
    n = wt.shape[0] - n_ba
    assert n % tn == 0 and n_ba % (2 * SUBLANES) == 0 and n_ba <= LANES
    per_batch = seq // tm
    return pl.pallas_call(
        functools.partial(_inproj_kernel, n_lead=n_lead, n_ba=n_ba),
        grid=(m // tm, n // tn),
        in_specs=[
            pl.BlockSpec(memory_space=pl.ANY),
            pl.BlockSpec((1, d), lambda i, j: (0, 0)),
            pl.BlockSpec((None, 1, d), lambda i, j: (i // per_batch, 0, 0)),
            pl.BlockSpec((None, 1, d), lambda i, j: (i // per_batch, 0, 0)),
            pl.BlockSpec(memory_space=pl.ANY),
        ],
        out_specs=[
            pl.BlockSpec((tm, tn), lambda i, j: (i, j)),
            pl.BlockSpec((tm, LANES), lambda i, j: (i, 0)),
        ],
        out_shape=[
            jax.ShapeDtypeStruct((m, n), BF16),
            jax.ShapeDtypeStruct((m, LANES), F32),
        ],
        scratch_shapes=[
            pltpu.VMEM((tm, d), F32),
            pltpu.VMEM((tm, d), BF16),
            pltpu.VMEM((n // tn, tn, d), BF16),
            pltpu.VMEM((LANES, d), BF16),
            pltpu.VMEM((2, tn // 2, d), F32),
            pltpu.VMEM((n_ba, d), F32),
            pltpu.SemaphoreType.DMA(()),
            pltpu.SemaphoreType.DMA((2,)),
            pltpu.SemaphoreType.DMA(()),
        ],
        compiler_params=_params("inproj", 2),
        name="inproj",
    )(x2, norm_g, sc, sh, wt)


def _bmm(a, b):
    return jnp.einsum("hmk,hkn->hmn", a.astype(BF16), b.astype(BF16), preferred_element_type=F32)


def _bmm_nt(a, b):
    return jnp.einsum("hmk,hnk->hmn", a.astype(BF16), b.astype(BF16), preferred_element_type=F32)


def _unit_lower_inverse(a):
    row = lax.broadcasted_iota(jnp.int32, a.shape[1:], 0)
    col = lax.broadcasted_iota(jnp.int32, a.shape[1:], 1)
    apart = row ^ col
    eye = jnp.where(row == col, 1.0, 0.0).astype(F32)
    t = jnp.where(apart < 2, eye - a, 0.0)
    s = 2
    while s < CHUNK:
        coupling = jnp.where((apart >= s) & (apart < 2 * s), a, 0.0)
        t = t - _bmm(t, _bmm(coupling, t))
        s *= 2
    return t


def _lane_sums(x):
    h, rows, width = x.shape
    ones = jnp.ones((width, width), BF16)
    return _dot(x.reshape(h * rows, width).astype(BF16), ones).reshape(h, rows, width)


def _causal_conv_silu(win, cw, k_w):
    assert k_w == 4
    tiled = (win.shape[0] // SUBLANES, SUBLANES, win.shape[1])

    def pair(x, x1, j):
        return (x.reshape(tiled) * cw[j][None] + x1.reshape(tiled) * cw[j - 1][None]).reshape(win.shape)

    win1 = pltpu.roll(win, 1, 0)
    acc = pair(win, win1, 3) + pltpu.roll(pair(win, win1, 1), 2, 0)
    return _silu(acc[SUBLANES:, :])


def _gdn_kernel(q_ref, k_ref, v_ref, z_ref, ba_ref, cw_ref, hp_ref, o_ref,
                s_ref, tail_ref, wq_ref, u_ref, ik_ref, dec_ref):
    nb, tb = q_ref.shape[0], q_ref.shape[1]
    kd = N_HEADS * HEAD_D
    nbh = nb * N_HEADS

    @pl.when(pl.program_id(0) == 0)
    def _():
        s_ref[...] = jnp.zeros(s_ref.shape, F32)
        tail_ref[...] = jnp.zeros(tail_ref.shape, F32)

    row = lax.broadcasted_iota(jnp.int32, (CHUNK, CHUNK), 0)
    col = lax.broadcasted_iota(jnp.int32, (CHUNK, CHUNK), 1)
    causal = row >= col
    strict = row > col
    tril = jnp.where(causal, 1.0, 0.0).astype(F32)
    a_log = hp_ref[0:1, :]
    dt_bias = hp_ref[1:2, :]
    onorm_g = hp_ref[2:3, :]
    zeros_half = jnp.zeros((CHUNK, HEAD_D), F32)

    def precompute(cp, carry):
        qs, ks, vs, betas, gcs, grs, gls = [], [], [], [], [], [], []
        for sub in range(PRE_CHUNKS):
            base = pl.multiple_of((cp * PRE_CHUNKS + sub) * CHUNK, CHUNK)
            prev = pl.multiple_of(base - BF16_ROWS, BF16_ROWS)
            for b in range(nb):
                ba = ba_ref[b, pl.ds(base, CHUNK), :]
                beta_all = _sigmoid(ba)
                g_all = -jnp.exp(a_log) * _softplus(ba + dt_bias)
                gcum = _dot_hi(tril, g_all)
                gcum_t = jnp.concatenate([gcum, gcum], axis=0).T
                for h in range(N_HEADS):
                    lo, hi = h * HEAD_D, (h + 1) * HEAD_D

                    def conv(ref, off):
                        cur = ref[b, pl.ds(base, CHUNK), lo:hi].astype(F32)
                        if sub == 0:
                            before = tail_ref[b, :, off + lo:off + hi]
                        else:
                            before = ref[b, pl.ds(prev, BF16_ROWS), lo:hi].astype(F32)[SUBLANES:]
                        win = jnp.concatenate([before, cur], axis=0)
                        return _causal_conv_silu(win, cw_ref[:, :, off + lo:off + hi], QKV_CONV)

                    qs.append(conv(q_ref, 0))
                    ks.append(conv(k_ref, kd))
                    vs.append(conv(v_ref, 2 * kd))
                    betas.append(beta_all[:, h:h + 1])
                    gcs.append(gcum[:, N_HEADS + h:N_HEADS + h + 1])
                    grs.append(gcum_t[N_HEADS + h:N_HEADS + h + 1, 0:CHUNK])
                    gls.append(gcum[CHUNK - 1:CHUNK, N_HEADS + h:N_HEADS + h + 1])
        q, k, v = jnp.stack(qs), jnp.stack(ks), jnp.stack(vs)
        beta, gc, gr, gl = jnp.stack(betas), jnp.stack(gcs), jnp.stack(grs), jnp.stack(gls)
        qn = q * (lax.rsqrt(_lane_sums(q * q) + EPS) * (HEAD_D ** -0.5))
        kn = k * lax.rsqrt(_lane_sums(k * k) + EPS)
        decay = jnp.where(causal, jnp.exp(jnp.where(causal, gc - gr, 0.0)), 0.0)
        kb = kn * beta
        e_gc = jnp.exp(gc)
        kq = _bmm_nt(jnp.concatenate([kb, qn], axis=1), kn)
        a = jnp.where(strict, kq[:, :CHUNK] * decay, 0.0)
        intra = kq[:, CHUNK:] * decay
        uw = _bmm(_unit_lower_inverse(a), jnp.concatenate([v * beta, kb * e_gc], axis=2))
        wq = jnp.concatenate([uw[:, :, HEAD_D:], qn * e_gc], axis=1).astype(BF16)
        k_dec = kn * jnp.exp(gl - gc)
        k_dec_t = jnp.stack([jnp.concatenate([k_dec[i], zeros_half], axis=0).T[:, :CHUNK]
                             for i in range(PRE_CHUNKS * nbh)])
        ik = jnp.concatenate([intra, k_dec_t], axis=1).astype(BF16)
        dec = jnp.broadcast_to(jnp.exp(gl), (PRE_CHUNKS * nbh, 1, HEAD_D))
        for sub in range(PRE_CHUNKS):
            c = cp * PRE_CHUNKS + sub
            rows = slice(sub * nbh, (sub + 1) * nbh)
            u_ref[c] = uw[rows, :, :HEAD_D]
            wq_ref[c] = wq[rows]
            ik_ref[c] = ik[rows]
            dec_ref[c] = dec[rows]

        last = pl.multiple_of((cp + 1) * PRE_CHUNKS * CHUNK - BF16_ROWS, BF16_ROWS)
        for b in range(nb):
            for j, ref in enumerate((q_ref, k_ref, v_ref)):
                rows = ref[b, pl.ds(last, BF16_ROWS), :].astype(F32)
                tail_ref[b, :, j * kd:(j + 1) * kd] = rows[BF16_ROWS - SUBLANES:, :]
        return carry

    def recur(c, carry):
        base = pl.multiple_of(c * CHUNK, CHUNK)
        state = s_ref[...]
        ws = _bmm(wq_ref[c], state)
        v_new = u_ref[c] - ws[:, :CHUNK]
        r = _bmm(ik_ref[c], v_new)
        o = ws[:, CHUNK:] + r[:, :CHUNK]
        s_ref[...] = state * dec_ref[c] + r[:, CHUNK:]
        on = o * lax.rsqrt(jnp.mean(o * o, axis=-1, keepdims=True) + EPS) * onorm_g
        for b in range(nb):
            for h in range(N_HEADS):
                lo, hi = h * HEAD_D, (h + 1) * HEAD_D
                z = z_ref[b, pl.ds(base, CHUNK), lo:hi].astype(F32)
                o_ref[b, pl.ds(base, CHUNK), lo:hi] = (on[b * N_HEADS + h] * _silu(z)).astype(BF16)
        return carry

    lax.fori_loop(0, tb // (PRE_CHUNKS * CHUNK), precompute, 0)
    lax.fori_loop(0, tb // CHUNK, recur, 0)


def _gdn(proj, ba, conv_w, head_params, batch, seq, tb):
    kd = N_HEADS * HEAD_D
    nc = tb // CHUNK
    nbh = batch * N_HEADS
    proj3 = proj.reshape(batch, seq, proj.shape[1])
    ba3 = ba.reshape(batch, seq, LANES)

    def col(j):
        return pl.BlockSpec((batch, tb, kd), lambda t: (0, t, j))

    out = pl.pallas_call(
        _gdn_kernel,
        grid=(seq // tb,),
        in_specs=[
            col(0), col(1), col(2), col(3),
            pl.BlockSpec((batch, tb, LANES), lambda t: (0, t, 0)),
            pl.BlockSpec((QKV_CONV, SUBLANES, 3 * kd), lambda t: (0, 0, 0)),
            pl.BlockSpec((SUBLANES, LANES), lambda t: (0, 0)),
        ],
        out_specs=pl.BlockSpec((batch, tb, kd), lambda t: (0, t, 0)),
        out_shape=jax.ShapeDtypeStruct((batch, seq, kd), BF16),
        scratch_shapes=[
            pltpu.VMEM((nbh, HEAD_D, HEAD_D), F32),
            pltpu.VMEM((batch, SUBLANES, 3 * kd), F32),
            pltpu.VMEM((nc, nbh, 2 * CHUNK, HEAD_D), BF16),
            pltpu.VMEM((nc, nbh, CHUNK, HEAD_D), F32),
            pltpu.VMEM((nc, nbh, CHUNK + HEAD_D, CHUNK), BF16),
            pltpu.VMEM((nc, nbh, 1, HEAD_D), F32),
        ],
        compiler_params=_params("gdn"),
        name="gdn",
    )(proj3, proj3, proj3, proj3, ba3, conv_w, head_params)
    return out.reshape(batch * seq, kd)


def _post_kernel(x_ref, og_ref, sb_ref, sc_ref, sx_ref, ga_ref, gb_ref, mod_ref, n2g_ref, cw_ref,
                 wpa_ref, wpb_ref, wout_ref, wrh_ref, wrl_ref, br_ref,
                 x1_ref, h2_ref, route_ref, cnt_ref, win_ref, run_ref, *, per_batch):
    tm = x_ref.shape[0]
    i = pl.program_id(0)

    @pl.when(i == 0)
    def _():
        run_ref[...] = jnp.zeros(run_ref.shape, F32)

    @pl.when(i % per_batch == 0)
    def _():
        win_ref[...] = jnp.zeros(win_ref.shape, F32)

    assert SC_CONV == 3
    prod = sc_ref[...].astype(F32) * sx_ref[...].astype(F32)
    win = jnp.concatenate([win_ref[...], prod], axis=0)
    win_ref[...] = prod[tm - SUBLANES:, :]
    tiled = (win.shape[0] // SUBLANES, SUBLANES, win.shape[1])
    win1 = pltpu.roll(win, 1, 0)
    conv = (win.reshape(tiled) * cw_ref[2][None] + win1.reshape(tiled) * cw_ref[1][None]
            + pltpu.roll(win1, 1, 0).reshape(tiled) * cw_ref[0][None]).reshape(win.shape)[SUBLANES:, :]
    y_b = _dot((sb_ref[...].astype(F32) * conv).astype(BF16), wpb_ref[...])
    y_a = _dot(og_ref[...], wpa_ref[...])
    merged = _sigmoid(ga_ref[...].astype(F32)) * y_a + _sigmoid(gb_ref[...].astype(F32)) * y_b
    mix = _dot(merged.astype(BF16), wout_ref[...])
    x1 = x_ref[...] + mod_ref[0:1, :] * mix
    x1_ref[...] = x1

    y = x1 * lax.rsqrt(jnp.mean(x1 * x1, axis=-1, keepdims=True) + EPS)
    h2 = (y * n2g_ref[...]) * (1.0 + mod_ref[1:2, :]) + mod_ref[2:3, :]
    _write_rows(h2_ref, h2)

    lg = _dot_split(h2, wrh_ref[...], wrl_ref[...]) + br_ref[...]
    lane = lax.broadcasted_iota(jnp.int32, lg.shape, 1).astype(F32)
    neg = jnp.float32(-jnp.inf)
    big = jnp.float32(2 * LANES)

    def first_max(mask):
        vmax = jnp.max(jnp.where(mask, lg, neg), axis=-1, keepdims=True)
        idx = jnp.min(jnp.where(mask & (lg == vmax), lane, big), axis=-1, keepdims=True)
        return vmax, idx

    gmask = lane < N_GROUPS
    g_max, g_sel = first_max(gmask)
    p_group = 1.0 / jnp.sum(jnp.where(gmask, jnp.exp(lg - g_max), 0.0), axis=-1, keepdims=True)
    e_lo = N_GROUPS + EXPERTS_PER_GROUP * g_sel
    emask = (lane >= e_lo) & (lane < e_lo + EXPERTS_PER_GROUP)
    v1, i1 = first_max(emask)
    v2, i2 = first_max(emask & (lane != i1))
    ex = jnp.exp(v2 - v1)
    w1 = p_group * (1.0 / (1.0 + ex))
    w2 = p_group * (ex / (1.0 + ex))
    e1 = i1 - N_GROUPS
    e2 = i2 - N_GROUPS

    onehot = jnp.where((lane == e1) | (lane == e2), 1.0, 0.0).astype(F32)
    row = lax.broadcasted_iota(jnp.int32, (tm, tm), 0)
    col = lax.broadcasted_iota(jnp.int32, (tm, tm), 1)
    before = jnp.where(row > col, 1.0, 0.0).astype(BF16)
    seen = _dot(before, onehot.astype(BF16)) + run_ref[0:1, :]
    r1 = jnp.sum(jnp.where(lane == e1, seen, 0.0), axis=-1, keepdims=True)
    r2 = jnp.sum(jnp.where(lane == e2, seen, 0.0), axis=-1, keepdims=True)
    run_ref[0:1, :] = run_ref[0:1, :] + jnp.sum(onehot, axis=0, keepdims=True)
    cnt_ref[...] = jnp.broadcast_to(run_ref[0:1, :], cnt_ref.shape)

    out = jnp.where(lane == 0, e1, 0.0)
    out = jnp.where(lane == 1, e2, out)
    out = jnp.where(lane == 2, w1, out)
    out = jnp.where(lane == 3, w2, out)
    out = jnp.where(lane == 4, r1, out)
    out = jnp.where(lane == 5, r2, out)
    route_ref[...] = out


def _post(x2, og, proj, mod, n2g, conv_w, wpa, wpb, wout, w_route_hi, w_route_lo, b_route, seq, tm):
    m, d = x2.shape
    assert d == SUBLANES * LANES
    per_batch = seq // tm

    def rows(j):
        return pl.BlockSpec((tm, d), lambda i: (i, j))

    def whole(shape):
        return pl.BlockSpec(shape, lambda i: tuple(0 for _ in shape))

    return pl.pallas_call(
        functools.partial(_post_kernel, per_batch=per_batch),
        grid=(m // tm,),
        in_specs=[
            rows(0), rows(0), rows(4), rows(5), rows(6), rows(7), rows(8),
            pl.BlockSpec((None, SUBLANES, d), lambda i: (i // per_batch, 0, 0)),
            whole((1, d)), whole((SC_CONV, SUBLANES, d)),
            whole((d, d)), whole((d, d)), whole((d, d)),
            whole((d, LANES)), whole((d, LANES)), whole((1, LANES)),
        ],
        out_specs=[
            rows(0), pl.BlockSpec((tm * SUBLANES, LANES), lambda i: (i, 0)),
            pl.BlockSpec((tm, LANES), lambda i: (i, 0)),
            pl.BlockSpec((SUBLANES, LANES), lambda i: (0, 0)),
        ],
        out_shape=[
            jax.ShapeDtypeStruct((m, d), F32),
            jax.ShapeDtypeStruct((m * SUBLANES, LANES), F32),
            jax.ShapeDtypeStruct((m, LANES), F32),
            jax.ShapeDtypeStruct((SUBLANES, LANES), F32),
        ],
        scratch_shapes=[
            pltpu.VMEM((SUBLANES, d), F32),
            pltpu.VMEM((SUBLANES, LANES), F32),
        ],
        compiler_params=_params("post"),
        name="post",
    )(x2, og, proj, proj, proj, proj, proj, mod, n2g, conv_w, wpa, wpb, wout, w_route_hi, w_route_lo,
      b_route)


def _plan_kernel(route_ref, cnt_ref, d_ref):
    bm = EXPERT_BLOCK
    sizes = cnt_ref[...]
    padded = jnp.floor((sizes + (bm - 1.0)) * (1.0 / bm)) * bm
    lane_i = lax.broadcasted_iota(jnp.int32, sizes.shape, 1)
    incl = padded
    s = 1
    while s < LANES:
        incl = incl + jnp.where(lane_i >= s, pltpu.roll(incl, s, 1), 0.0)
        s *= 2
    start = (incl - padded)[0:1, :]
    r = route_ref[...]
    lane = lax.broadcasted_iota(jnp.int32, r.shape, 1).astype(F32)
    d1 = jnp.sum(jnp.where(lane == r[:, 0:1], start, 0.0), axis=-1, keepdims=True) + r[:, 4:5]
    d2 = jnp.sum(jnp.where(lane == r[:, 1:2], start, 0.0), axis=-1, keepdims=True) + r[:, 5:6]
    out = jnp.where(lane == 0.0, d1, jnp.where(lane == 1.0, d2, 0.0))
    d_ref[...] = out.T[0:SUBLANES, :].astype(jnp.int32)


def _plan(route, counts, tm):
    m = route.shape[0]
    return pl.pallas_call(
        _plan_kernel,
        grid=(m // tm,),
        in_specs=[
            pl.BlockSpec((tm, LANES), lambda i: (i, 0)),
            pl.BlockSpec((SUBLANES, LANES), lambda i: (0, 0)),
        ],
        out_specs=pl.BlockSpec((SUBLANES, tm), lambda i: (0, i)),
        out_shape=jax.ShapeDtypeStruct((SUBLANES, m), jnp.int32),
        compiler_params=_params("plan"),
        name="plan",
    )(route, counts)


def _row_copy(src_ref, src_row, dst_ref, dst_row, sem):
    return pltpu.make_async_copy(_row_tile(src_ref, src_row), _row_tile(dst_ref, dst_row), sem)


def _rows_copy(src_ref, dst_ref, dst_row, n, sem):
    return pltpu.make_async_copy(src_ref, dst_ref.at[pl.ds(dst_row * SUBLANES, n * SUBLANES), :], sem)


def _dispatch_kernel(d1_ref, d2_ref, tail_ref, h2_ref, xs_ref, zero_ref, sem):
    tm = h2_ref.shape[0] // SUBLANES
    bm = zero_ref.shape[0] // SUBLANES
    t0 = pl.program_id(0) * tm

    @pl.when(pl.program_id(0) == 0)
    def _():
        zero_ref[...] = jnp.zeros(zero_ref.shape, F32)
        n_blocks = xs_ref.shape[0] // (bm * SUBLANES)

        def zero_block(row):
            return _rows_copy(zero_ref, xs_ref, pl.multiple_of(row, bm), bm, sem.at[0])

        for e in range(N_EXPERTS):
            @pl.when(tail_ref[e] >= 0)
            def _():
                zero_block(tail_ref[e]).start()
        lax.fori_loop(tail_ref[N_EXPERTS], n_blocks, lambda j, c: (zero_block(j * bm).start(), c)[1], 0)
        for e in range(N_EXPERTS):
            @pl.when(tail_ref[e] >= 0)
            def _():
                zero_block(tail_ref[e]).wait()
        lax.fori_loop(tail_ref[N_EXPERTS], n_blocks, lambda j, c: (zero_block(j * bm).wait(), c)[1], 0)

    def start(r, carry):
        _row_copy(h2_ref, r, xs_ref, d1_ref[t0 + r], sem.at[0]).start(priority=0)
        _row_copy(h2_ref, r, xs_ref, d2_ref[t0 + r], sem.at[1]).start(priority=1)
        return carry

    lax.fori_loop(0, tm, start, 0, unroll=DMA_UNROLL)
    _rows_copy(h2_ref, xs_ref, 0, tm, sem.at[0]).wait()
    _rows_copy(h2_ref, xs_ref, 0, tm, sem.at[1]).wait()


def _dispatch(d1, d2, tail_start, h2, cap, tm):
    m = h2.shape[0] // SUBLANES
    return pl.pallas_call(
        _dispatch_kernel,
        grid_spec=pltpu.PrefetchScalarGridSpec(
            num_scalar_prefetch=3,
            grid=(m // tm,),
            in_specs=[pl.BlockSpec((tm * SUBLANES, LANES), lambda i, d1, d2, tl: (i, 0))],
            out_specs=pl.BlockSpec(memory_space=pl.ANY),
            scratch_shapes=[pltpu.VMEM((EXPERT_BLOCK * SUBLANES, LANES), F32),
                            pltpu.SemaphoreType.DMA((2,))],
        ),
        out_shape=jax.ShapeDtypeStruct((cap * SUBLANES, LANES), F32),
        compiler_params=_params("dispatch"),
        name="dispatch",
    )(d1, d2, tail_start, h2)


def _expert_kernel(be_ref, act_ref, nxt_ref, x_ref, w1_hbm, w3_hbm, w2_hbm, y_ref, w1b_ref, w3b_ref, w2b_ref,
                   w1s_ref, w3s_ref, w2s_ref, sem):
    b = pl.program_id(0)
    weights = ((w1_hbm, w1s_ref, w1b_ref), (w3_hbm, w3s_ref, w3b_ref), (w2_hbm, w2s_ref, w2b_ref))

    def weight_copy(k, e):
        w_hbm, ws_ref, _ = weights[k]
        return pltpu.make_async_copy(w_hbm.at[e], ws_ref, sem.at[k])

    @pl.when(act_ref[b] > 0)
    def _():
        @pl.when(b == 0)
        def _():
            for k in range(len(weights)):
                weight_copy(k, be_ref[b]).start()

        @pl.when((b == 0) | (be_ref[b] != be_ref[jnp.maximum(b - 1, 0)]))
        def _():
            for k, (_, ws_ref, wb_ref) in enumerate(weights):
                weight_copy(k, be_ref[b]).wait()
                wb_ref[...] = ws_ref[...].astype(BF16)

                @pl.when(nxt_ref[b] >= 0)
                def _():
                    weight_copy(k, nxt_ref[b]).start()

        xb = _read_rows(x_ref, x_ref.shape[0] // SUBLANES).astype(BF16)
        hid = _silu(_dot(xb, w1b_ref[...])) * _dot(xb, w3b_ref[...])
        _write_rows(y_ref, _dot(hid.astype(BF16), w2b_ref[...]))

    @pl.when(act_ref[b] == 0)
    def _():
        y_ref[...] = jnp.zeros(y_ref.shape, F32)


def _experts(block_expert, block_active, block_next, xs, w1, w3, w2):
    cap = xs.shape[0] // SUBLANES
    d, de = w1.shape[1], w1.shape[2]
    bm = EXPERT_BLOCK

    def x_block(b, be, act, nxt):
        return (jnp.minimum(b, jnp.maximum(act[cap // bm], 1) - 1), 0)

    return pl.pallas_call(
        _expert_kernel,
        grid_spec=pltpu.PrefetchScalarGridSpec(
            num_scalar_prefetch=3,
            grid=(cap // bm,),
            in_specs=[
                pl.BlockSpec((bm * SUBLANES, LANES), x_block),
                pl.BlockSpec(memory_space=pl.ANY),
                pl.BlockSpec(memory_space=pl.ANY),
                pl.BlockSpec(memory_space=pl.ANY),
            ],
            out_specs=pl.BlockSpec((bm * SUBLANES, LANES), lambda b, be, act, nxt: (b, 0)),
            scratch_shapes=[pltpu.VMEM((d, de), BF16), pltpu.VMEM((d, de), BF16), pltpu.VMEM((de, d), BF16),
                            pltpu.VMEM((d, de), F32), pltpu.VMEM((d, de), F32), pltpu.VMEM((de, d), F32),
                            pltpu.SemaphoreType.DMA((3,))],
        ),
        out_shape=jax.ShapeDtypeStruct((cap * SUBLANES, LANES), F32),
        compiler_params=_params("experts"),
        name="experts",
    )(block_expert, block_active, block_next, xs, w1, w3, w2)


def _final_kernel(d1_ref, d2_ref, x1_ref, route_ref, gt_ref, nfg_ref, ys_ref, o_ref, buf_ref, sem):
    tm = x1_ref.shape[0]
    i = pl.program_id(0)

    def gather(tile, s):
        t0 = tile * tm

        def start(r, carry):
            _row_copy(ys_ref, d1_ref[t0 + r], buf_ref.at[s, 0], r, sem.at[s, 0]).start(priority=0)
            _row_copy(ys_ref, d2_ref[t0 + r], buf_ref.at[s, 1], r, sem.at[s, 1]).start(priority=1)
            return carry

        lax.fori_loop(0, tm, start, 0, unroll=DMA_UNROLL)

    def wait_rows(s):
        for k in range(2):
            pltpu.make_async_copy(ys_ref.at[pl.ds(0, tm * SUBLANES), :], buf_ref.at[s, k],
                                  sem.at[s, k]).wait()

    def combine(cur, nxt):
        wait_rows(cur)
        last = pl.num_programs(0) - 1
        t0 = jnp.minimum(i + 1, last) * tm
        for r in range(tm):
            _row_copy(ys_ref, d1_ref[t0 + r], buf_ref.at[nxt, 0], r, sem.at[nxt, 0]).start(priority=0)
            _row_copy(ys_ref, d2_ref[t0 + r], buf_ref.at[nxt, 1], r, sem.at[nxt, 1]).start(priority=1)
        route = route_ref[...]
        moe = (_read_rows(buf_ref.at[cur, 0], tm) * route[:, 2:3]
               + _read_rows(buf_ref.at[cur, 1], tm) * route[:, 3:4])
        x2 = x1_ref[...] + gt_ref[...] * moe
        y = x2 * lax.rsqrt(jnp.mean(x2 * x2, axis=-1, keepdims=True) + EPS)
        o_ref[...] = y * nfg_ref[...]

        @pl.when(i == last)
        def _():
            wait_rows(nxt)

    @pl.when(i == 0)
    def _():
        gather(0, 0)

    @pl.when(i % 2 == 0)
    def _():
        combine(0, 1)

    @pl.when(i % 2 == 1)
    def _():
        combine(1, 0)


def _final(d1, d2, x1, route, gt2, nfg, ys, seq, tm):
    m, d = x1.shape
    per_batch = seq // tm
    return pl.pallas_call(
        _final_kernel,
        grid_spec=pltpu.PrefetchScalarGridSpec(
            num_scalar_prefetch=2,
            grid=(m // tm,),
            in_specs=[
                pl.BlockSpec((tm, d), lambda i, d1, d2: (i, 0)),
                pl.BlockSpec((tm, LANES), lambda i, d1, d2: (i, 0)),
                pl.BlockSpec((None, 1, d), lambda i, d1, d2: (i // per_batch, 0, 0)),
                pl.BlockSpec((1, d), lambda i, d1, d2: (0, 0)),
                pl.BlockSpec(memory_space=pl.ANY),
            ],
            out_specs=pl.BlockSpec((tm, d), lambda i, d1, d2: (i, 0)),
            scratch_shapes=[pltpu.VMEM((2, 2, tm * SUBLANES, LANES), F32),
                            pltpu.SemaphoreType.DMA((2, 2))],
        ),
        out_shape=jax.ShapeDtypeStruct((m, d), F32),
        compiler_params=_params("final"),
        name="final",
    )(d1, d2, x1, route, gt2, nfg, ys)


def _tile(n, pref):
    t = min(n, pref)
    assert n % t == 0
    return t


def kernel(x, c, w_ada, b_ada, norm1_g, w_in, conv_qkv_w, a_log, dt_bias, onorm_g, w_proj_a,
           conv_sc_w, w_proj_b, w_out, norm2_g, w_group, b_group, w_expert, b_expert, w1, w3, w2,
           normf_g):
    batch, seq, d = x.shape
    depth = w_ada.shape[0]
    m = batch * seq
    kd = N_HEADS * HEAD_D
    assert d == kd and seq % CHUNK == 0 and batch <= SUBLANES
    assert depth == 1, "the last stage applies the final rmsnorm: one layer only"
    x2 = x.reshape(m, d)

    for l in range(depth):
        mod = _ada(c, w_ada[l], b_ada[l][None, :])[:batch]
        sh1, sc1, gt1, sh2, sc2, gt2 = [mod[:, None, j * d:(j + 1) * d] for j in range(6)]

        w = w_in[l]
        o_ba = 3 * kd + kd
        proj, ba = _inproj(x2, norm1_g[l][None, :], sc1, sh1, jnp.swapaxes(w, 0, 1), o_ba // d, 2 * N_HEADS,
                           seq, _tile(seq, INPROJ_ROWS))

        head_params = jnp.zeros((SUBLANES, LANES), F32)
        head_params = head_params.at[0, N_HEADS:2 * N_HEADS].set(a_log[l])
        head_params = head_params.at[1, N_HEADS:2 * N_HEADS].set(dt_bias[l])
        head_params = head_params.at[2, :].set(onorm_g[l])
        conv_taps = jnp.broadcast_to(conv_qkv_w[l][:, None, :], (QKV_CONV, SUBLANES, 3 * kd))
        og = _gdn(proj, ba, conv_taps, head_params, batch, seq, _tile(seq, GDN_ROWS))

        mod_post = jnp.zeros((batch, SUBLANES, d), F32)
        mod_post = mod_post.at[:, 0:1].set(gt1).at[:, 1:2].set(sc2).at[:, 2:3].set(sh2)
        w_route = jnp.zeros((d, LANES), F32)
        w_route = w_route.at[:, :N_GROUPS].set(w_group[l]).at[:, N_GROUPS:N_GROUPS + N_EXPERTS].set(w_expert[l])
        w_route_hi = w_route.astype(BF16)
        w_route_lo = (w_route - w_route_hi.astype(F32)).astype(BF16)
        b_route = jnp.zeros((1, LANES), F32)
        b_route = b_route.at[0, :N_GROUPS].set(b_group[l]).at[0, N_GROUPS:N_GROUPS + N_EXPERTS].set(b_expert[l])
        x1, h2, route, counts = _post(
            x2, og, proj, mod_post, norm2_g[l][None, :],
            jnp.broadcast_to(conv_sc_w[l][:, None, :], (SC_CONV, SUBLANES, d)),
            w_proj_a[l].astype(BF16), w_proj_b[l].astype(BF16), w_out[l].astype(BF16),
            w_route_hi, w_route_lo, b_route, seq, _tile(seq, POST_ROWS))

        bm = EXPERT_BLOCK
        n_blocks = (2 * m) // bm + N_EXPERTS
        sizes = counts[0, :N_EXPERTS].astype(jnp.int32)
        padded = ((sizes + bm - 1) // bm) * bm
        pad_end = jnp.cumsum(padded)
        tail_start = jnp.concatenate([jnp.where(padded > 0, pad_end - bm, -1),
                                      pad_end[-1:] // bm]).astype(jnp.int32)
        block_row = jnp.arange(n_blocks, dtype=jnp.int32) * bm
        block_expert = jnp.minimum(jnp.sum(block_row[:, None] >= pad_end[None, :], axis=1),
                                   N_EXPERTS - 1).astype(jnp.int32)
        block_active = jnp.concatenate([(block_row < pad_end[-1]).astype(jnp.int32),
                                        (pad_end[-1:] // bm).astype(jnp.int32)])
        expert_id = jnp.arange(N_EXPERTS, dtype=jnp.int32)
        later = (expert_id[None, :] > expert_id[:, None]) & (padded[None, :] > 0)
        next_expert = jnp.min(jnp.where(later, expert_id[None, :], N_EXPERTS), axis=1)
        block_next = jnp.where(next_expert < N_EXPERTS, next_expert, -1)[block_expert].astype(jnp.int32)
        dest = _plan(route, counts, _tile(seq, INPROJ_ROWS))
        d1, d2 = dest[0], dest[1]

        xs = _dispatch(d1, d2, tail_start, h2, n_blocks * bm, _tile(seq, DISPATCH_ROWS))
        ys = _experts(block_expert, block_active, block_next, xs, w1[l], w3[l], w2[l])
        nfg = normf_g[None, :]
        x2 = _final(d1, d2, x1, route, gt2, nfg, ys, seq, _tile(seq, FINAL_ROWS))
    return x2.reshape(batch, seq, d)
```

```python
import functools

import jax
import jax.numpy as jnp
from jax import lax
from jax.experimental import pallas as pl
from jax.experimental.pallas import tpu as pltpu

F32 = jnp.float32
BF16 = jnp.bfloat16
HIGHEST = lax.Precision.HIGHEST

N_HEADS = 8
HEAD_D = 128
CHUNK = 64
QKV_CONV = 4
SC_CONV = 3
N_GROUPS = 4
EXPERTS_PER_GROUP = 8
N_EXPERTS = N_GROUPS * EXPERTS_PER_GROUP
EPS = 1e-6

LANES = 128
SUBLANES = 8
BF16_ROWS = 16
EXPERT_BLOCK = 512
INPROJ_ROWS = 2048
GDN_ROWS = 256
PRE_CHUNKS = 2
POST_ROWS = 512
DISPATCH_ROWS = 1024
FINAL_ROWS = 512
DMA_UNROLL = 8
MIB = 1024 * 1024
VMEM_MIB = {"ada": 24, "inproj": 56, "gdn": 48, "post": 56, "plan": 16, "dispatch": 24, "experts": 40,
            "final": 32}


def _sigmoid(x):
    return 0.5 + 0.5 * jnp.tanh(0.5 * x)


def _silu(x):
    half = 0.5 * x
    return half + half * jnp.tanh(half)


def _softplus(x):
    return jnp.maximum(x, 0.0) + jnp.log(1.0 + jnp.exp(-jnp.abs(x)))


def _dot(a, b):
    return jnp.dot(a, b, preferred_element_type=F32)


def _dot_nt(a, b):
    return lax.dot_general(a, b, (((1,), (1,)), ((), ())), preferred_element_type=F32)


def _dot_hi(a, b):
    return jnp.dot(a, b, preferred_element_type=F32, precision=HIGHEST)


def _dot_split(a, b_hi, b_lo):
    a_hi = a.astype(BF16)
    a_lo = (a - a_hi.astype(F32)).astype(BF16)
    return _dot(a_hi, b_hi) + (_dot(a_lo, b_hi) + _dot(a_hi, b_lo))


def _read_rows(ref, rows):
    return jnp.concatenate([ref[pl.ds(j, rows, stride=SUBLANES), :] for j in range(SUBLANES)], axis=1)


def _write_rows(ref, value):
    rows = value.shape[0]
    for j in range(SUBLANES):
        ref[pl.ds(j, rows, stride=SUBLANES), :] = value[:, j * LANES:(j + 1) * LANES]


def _row_tile(ref, row):
    if not isinstance(row, int):
        row = pl.multiple_of(row * SUBLANES, SUBLANES)
    else:
        row = row * SUBLANES
    return ref.at[pl.ds(row, SUBLANES), :]


def _params(call, n_grid_axes=1):
    return pltpu.CompilerParams(dimension_semantics=("arbitrary",) * n_grid_axes,
                                vmem_limit_bytes=VMEM_MIB[call] * MIB)


def _ada_kernel(cb_ref, w_ref, b_ref, o_ref, s_ref):
    batch, d, _ = cb_ref.shape
    groups = w_ref.shape[1] // LANES

    @pl.when(pl.program_id(0) == 0)
    def _():
        s_ref[...] = _silu(cb_ref[...])

    def body(kc, accs):
        k0 = pl.multiple_of(kc * SUBLANES, SUBLANES)
        w = w_ref[pl.ds(k0, SUBLANES), :]
        out = []
        for b in range(batch):
            s = s_ref[b, pl.ds(k0, SUBLANES), :]
            out += [accs[b * groups + g] + w[:, g * LANES:(g + 1) * LANES] * s for g in range(groups)]
        return tuple(out)

    zero = jnp.zeros((SUBLANES, LANES), F32)
    accs = lax.fori_loop(0, d // SUBLANES, body, (zero,) * (batch * groups), unroll=4)
    o_ref[...] = jnp.zeros(o_ref.shape, F32)
    for b in range(batch):
        for g in range(groups):
            cols = slice(g * LANES, (g + 1) * LANES)
            o_ref[b:b + 1, cols] = jnp.sum(accs[b * groups + g], axis=0, keepdims=True) + b_ref[:, cols]


def _ada(c, w_ada, b_ada):
    batch, d = c.shape
    n = w_ada.shape[1]
    cb = jnp.broadcast_to(c[:, :, None], (batch, d, LANES))
    return pl.pallas_call(
        _ada_kernel,
        grid=(n // d,),
        in_specs=[
            pl.BlockSpec((batch, d, LANES), lambda j: (0, 0, 0)),
            pl.BlockSpec((d, d), lambda j: (0, j)),
            pl.BlockSpec((1, d), lambda j: (0, j)),
        ],
        out_specs=pl.BlockSpec((SUBLANES, d), lambda j: (0, j)),
        out_shape=jax.ShapeDtypeStruct((SUBLANES, n), F32),
        scratch_shapes=[pltpu.VMEM((batch, d, LANES), F32)],
        compiler_params=_params("ada"),
        name="ada",
    )(cb, w_ada, b_ada)


def _inproj_kernel(x_hbm, g_ref, sc_ref, sh_ref, wt_hbm, o_ref, ba_ref, x_ref, h_ref, w_ref, wba_ref, stage_ref,
                   stage_ba_ref, sem_x, sem, sem_ba, *, n_lead, n_ba):
    i, j = pl.program_id(0), pl.program_id(1)
    n_tiles = pl.num_programs(0)
    n_blocks, tn = w_ref.shape[0], o_ref.shape[1]
    tm, half = x_ref.shape[0], stage_ref.shape[1]

    def x_copy(tile):
        return pltpu.make_async_copy(x_hbm.at[pl.ds(pl.multiple_of(tile * tm, tm), tm)], x_ref, sem_x)

    def block_copy(jb, part):
        row = jb * tn + jnp.where(jb >= n_lead, n_ba, 0) + part * half
        return pltpu.make_async_copy(wt_hbm.at[pl.ds(pl.multiple_of(row, SUBLANES), half)],
                                     stage_ref.at[part], sem.at[part])

    def ba_copy():
        return pltpu.make_async_copy(wt_hbm.at[pl.ds(n_lead * tn, n_ba)], stage_ba_ref, sem_ba)

    @pl.when(j == 0)
    def _():
        @pl.when(i == 0)
        def _():
            x_copy(i).start()
            ba_copy().start()
            for part in range(2):
                block_copy(j, part).start()

        x_copy(i).wait()
        x = x_ref[...]
        y = x * lax.rsqrt(jnp.mean(x * x, axis=-1, keepdims=True) + EPS)
        h = (y * g_ref[...]) * (1.0 + sc_ref[...]) + sh_ref[...]
        hb = h.astype(BF16)
        h_ref[...] = hb

        @pl.when(i == 0)
        def _():
            ba_copy().wait()
            wba_ref[...] = jnp.zeros(wba_ref.shape, BF16)
            wba_ref[0:n_ba, :] = stage_ba_ref[...].astype(BF16)

        ba_ref[...] = _dot_nt(hb, wba_ref[...])

    @pl.when(jnp.logical_and(j == 1, i + 1 < n_tiles))
    def _():
        x_copy(i + 1).start()

    @pl.when(i == 0)
    def _():
        for part in range(2):
            block_copy(j, part).wait()
            w_ref[j, pl.ds(part * half, half), :] = stage_ref[part].astype(BF16)

            @pl.when(j + 1 < n_blocks)
            def _():
                block_copy(j + 1, part).start()

    rows = o_ref.shape[0] // 2
    for r in range(2):
        o_ref[pl.ds(r * rows, rows), :] = _dot_nt(h_ref[pl.ds(r * rows, rows), :], w_ref[j]).astype(BF16)


def _inproj(x2, norm_g, sc, sh, wt, n_lead, n_ba, seq, tm):
    m, d = x2.shape
    tn = d
    n = wt.shape[0] - n_ba
    assert n % tn == 0 and n_ba % (2 * SUBLANES) == 0 and n_ba <= LANES
    per_batch = seq // tm
    return pl.pallas_call(
        functools.partial(_inproj_kernel, n_lead=n_lead, n_ba=n_ba),
        grid=(m // tm, n // tn),
        in_specs=[
            pl.BlockSpec(memory_space=pl.ANY),
            pl.BlockSpec((1, d), lambda i, j: (0, 0)),
            pl.BlockSpec((None, 1, d), lambda i, j: (i // per_batch, 0, 0)),
            pl.BlockSpec((None, 1, d), lambda i, j: (i // per_batch, 0, 0)),
            pl.BlockSpec(memory_space=pl.ANY),
        ],
        out_specs=[
            pl.BlockSpec((tm, tn), lambda i, j: (i, j)),
            pl.BlockSpec((tm, LANES), lambda i, j: (i, 0)),
        ],
        out_shape=[
            jax.ShapeDtypeStruct((m, n), BF16),
            jax.ShapeDtypeStruct((m, LANES), F32),
        ],
        scratch_shapes=[
            pltpu.VMEM((tm, d), F32),
            pltpu.VMEM((tm, d), BF16),
            pltpu.VMEM((n // tn, tn, d), BF16),
            pltpu.VMEM((LANES, d), BF16),
            pltpu.VMEM((2, tn // 2, d), F32),
            pltpu.VMEM((n_ba, d), F32),
            pltpu.SemaphoreType.DMA(()),
            pltpu.SemaphoreType.DMA((2,)),
            pltpu.SemaphoreType.DMA(()),
        ],
        compiler_params=_params("inproj", 2),
        name="inproj",
    )(x2, norm_g, sc, sh, wt)


def _bmm(a, b):
    return jnp.einsum("hmk,hkn->hmn", a.astype(BF16), b.astype(BF16), preferred_element_type=F32)


def _bmm_nt(a, b):
    return jnp.einsum("hmk,hnk->hmn", a.astype(BF16), b.astype(BF16), preferred_element_type=F32)


def _unit_lower_inverse(a):
    row = lax.broadcasted_iota(jnp.int32, a.shape[1:], 0)
    col = lax.broadcasted_iota(jnp.int32, a.shape[1:], 1)
    apart = row ^ col
    eye = jnp.where(row == col, 1.0, 0.0).astype(F32)
    t = jnp.where(apart < 2, eye - a, 0.0)
    s = 2
    while s < CHUNK:
        coupling = jnp.where((apart >= s) & (apart < 2 * s), a, 0.0)
        t = t - _bmm(t, _bmm(coupling, t))
        s *= 2
    return t


def _lane_sums(x):
    h, rows, width = x.shape
    ones = jnp.ones((width, width), BF16)
    return _dot(x.reshape(h * rows, width).astype(BF16), ones).reshape(h, rows, width)


def _causal_conv_silu(win, cw, k_w):
    assert k_w == 4
    tiled = (win.shape[0] // SUBLANES, SUBLANES, win.shape[1])

    def pair(x, x1, j):
        return (x.reshape(tiled) * cw[j][None] + x1.reshape(tiled) * cw[j - 1][None]).reshape(win.shape)

    win1 = pltpu.roll(win, 1, 0)
    acc = pair(win, win1, 3) + pltpu.roll(pair(win, win1, 1), 2, 0)
    return _silu(acc[SUBLANES:, :])


def _gdn_kernel(q_ref, k_ref, v_ref, z_ref, ba_ref, cw_ref, hp_ref, o_ref,
                s_ref, tail_ref, wq_ref, u_ref, ik_ref, dec_ref):
    nb, tb = q_ref.shape[0], q_ref.shape[1]
    kd = N_HEADS * HEAD_D
    nbh = nb * N_HEADS

    @pl.when(pl.program_id(0) == 0)
    def _():
        s_ref[...] = jnp.zeros(s_ref.shape, F32)
        tail_ref[...] = jnp.zeros(tail_ref.shape, F32)

    row = lax.broadcasted_iota(jnp.int32, (CHUNK, CHUNK), 0)
    col = lax.broadcasted_iota(jnp.int32, (CHUNK, CHUNK), 1)
    causal = row >= col
    strict = row > col
    tril = jnp.where(causal, 1.0, 0.0).astype(F32)
    a_log = hp_ref[0:1, :]
    dt_bias = hp_ref[1:2, :]
    onorm_g = hp_ref[2:3, :]
    zeros_half = jnp.zeros((CHUNK, HEAD_D), F32)

    def precompute(cp, carry):
        qs, ks, vs, betas, gcs, grs, gls = [], [], [], [], [], [], []
        for sub in range(PRE_CHUNKS):
            base = pl.multiple_of((cp * PRE_CHUNKS + sub) * CHUNK, CHUNK)
            prev = pl.multiple_of(base - BF16_ROWS, BF16_ROWS)
            for b in range(nb):
                ba = ba_ref[b, pl.ds(base, CHUNK), :]
                beta_all = _sigmoid(ba)
                g_all = -jnp.exp(a_log) * _softplus(ba + dt_bias)
                gcum = _dot_hi(tril, g_all)
                gcum_t = jnp.concatenate([gcum, gcum], axis=0).T
                for h in range(N_HEADS):
                    lo, hi = h * HEAD_D, (h + 1) * HEAD_D

                    def conv(ref, off):
                        cur = ref[b, pl.ds(base, CHUNK), lo:hi].astype(F32)
                        if sub == 0:
                            before = tail_ref[b, :, off + lo:off + hi]
                        else:
                            before = ref[b, pl.ds(prev, BF16_ROWS), lo:hi].astype(F32)[SUBLANES:]
                        win = jnp.concatenate([before, cur], axis=0)
                        return _causal_conv_silu(win, cw_ref[:, :, off + lo:off + hi], QKV_CONV)

                    qs.append(conv(q_ref, 0))
                    ks.append(conv(k_ref, kd))
                    vs.append(conv(v_ref, 2 * kd))
                    betas.append(beta_all[:, h:h + 1])
                    gcs.append(gcum[:, N_HEADS + h:N_HEADS + h + 1])
                    grs.append(gcum_t[N_HEADS + h:N_HEADS + h + 1, 0:CHUNK])
                    gls.append(gcum[CHUNK - 1:CHUNK, N_HEADS + h:N_HEADS + h + 1])
        q, k, v = jnp.stack(qs), jnp.stack(ks), jnp.stack(vs)
        beta, gc, gr, gl = jnp.stack(betas), jnp.stack(gcs), jnp.stack(grs), jnp.stack(gls)
        qn = q * (lax.rsqrt(_lane_sums(q * q) + EPS) * (HEAD_D ** -0.5))
        kn = k * lax.rsqrt(_lane_sums(k * k) + EPS)
        decay = jnp.where(causal, jnp.exp(jnp.where(causal, gc - gr, 0.0)), 0.0)
        kb = kn * beta
        e_gc = jnp.exp(gc)
        kq = _bmm_nt(jnp.concatenate([kb, qn], axis=1), kn)
        a = jnp.where(strict, kq[:, :CHUNK] * decay, 0.0)
        intra = kq[:, CHUNK:] * decay
        uw = _bmm(_unit_lower_inverse(a), jnp.concatenate([v * beta, kb * e_gc], axis=2))
        wq = jnp.concatenate([uw[:, :, HEAD_D:], qn * e_gc], axis=1).astype(BF16)
        k_dec = kn * jnp.exp(gl - gc)
        k_dec_t = jnp.stack([jnp.concatenate([k_dec[i], zeros_half], axis=0).T[:, :CHUNK]
                             for i in range(PRE_CHUNKS * nbh)])
        ik = jnp.concatenate([intra, k_dec_t], axis=1).astype(BF16)
        dec = jnp.broadcast_to(jnp.exp(gl), (PRE_CHUNKS * nbh, 1, HEAD_D))
        for sub in range(PRE_CHUNKS):
            c = cp * PRE_CHUNKS + sub
            rows = slice(sub * nbh, (sub + 1) * nbh)
            u_ref[c] = uw[rows, :, :HEAD_D]
            wq_ref[c] = wq[rows]
            ik_ref[c] = ik[rows]
            dec_ref[c] = dec[rows]

        last = pl.multiple_of((cp + 1) * PRE_CHUNKS * CHUNK - BF16_ROWS, BF16_ROWS)
        for b in range(nb):
            for j, ref in enumerate((q_ref, k_ref, v_ref)):
                rows = ref[b, pl.ds(last, BF16_ROWS), :].astype(F32)
                tail_ref[b, :, j * kd:(j + 1) * kd] = rows[BF16_ROWS - SUBLANES:, :]
        return carry

    def recur(c, carry):
        base = pl.multiple_of(c * CHUNK, CHUNK)
        state = s_ref[...]
        ws = _bmm(wq_ref[c], state)
        v_new = u_ref[c] - ws[:, :CHUNK]
        r = _bmm(ik_ref[c], v_new)
        o = ws[:, CHUNK:] + r[:, :CHUNK]
        s_ref[...] = state * dec_ref[c] + r[:, CHUNK:]
        on = o * lax.rsqrt(jnp.mean(o * o, axis=-1, keepdims=True) + EPS) * onorm_g
        for b in range(nb):
            for h in range(N_HEADS):
                lo, hi = h * HEAD_D, (h + 1) * HEAD_D
                z = z_ref[b, pl.ds(base, CHUNK), lo:hi].astype(F32)
                o_ref[b, pl.ds(base, CHUNK), lo:hi] = (on[b * N_HEADS + h] * _silu(z)).astype(BF16)
        return carry

    lax.fori_loop(0, tb // (PRE_CHUNKS * CHUNK), precompute, 0)
    lax.fori_loop(0, tb // CHUNK, recur, 0)


def _gdn(proj, ba, conv_w, head_params, batch, seq, tb):
    kd = N_HEADS * HEAD_D
    nc = tb // CHUNK
    nbh = batch * N_HEADS
    proj3 = proj.reshape(batch, seq, proj.shape[1])
    ba3 = ba.reshape(batch, seq, LANES)

    def col(j):
        return pl.BlockSpec((batch, tb, kd), lambda t: (0, t, j))

    out = pl.pallas_call(
        _gdn_kernel,
        grid=(seq // tb,),
        in_specs=[
            col(0), col(1), col(2), col(3),
            pl.BlockSpec((batch, tb, LANES), lambda t: (0, t, 0)),
            pl.BlockSpec((QKV_CONV, SUBLANES, 3 * kd), lambda t: (0, 0, 0)),
            pl.BlockSpec((SUBLANES, LANES), lambda t: (0, 0)),
        ],
        out_specs=pl.BlockSpec((batch, tb, kd), lambda t: (0, t, 0)),
        out_shape=jax.ShapeDtypeStruct((batch, seq, kd), BF16),
        scratch_shapes=[
            pltpu.VMEM((nbh, HEAD_D, HEAD_D), F32),
            pltpu.VMEM((batch, SUBLANES, 3 * kd), F32),
            pltpu.VMEM((nc, nbh, 2 * CHUNK, HEAD_D), BF16),
            pltpu.VMEM((nc, nbh, CHUNK, HEAD_D), F32),
            pltpu.VMEM((nc, nbh, CHUNK + HEAD_D, CHUNK), BF16),
            pltpu.VMEM((nc, nbh, 1, HEAD_D), F32),
        ],
        compiler_params=_params("gdn"),
        name="gdn",
    )(proj3, proj3, proj3, proj3, ba3, conv_w, head_params)
    return out.reshape(batch * seq, kd)


def _post_kernel(x_ref, og_ref, sb_ref, sc_ref, sx_ref, ga_ref, gb_ref, mod_ref, n2g_ref, cw_ref,
                 wpa_ref, wpb_ref, wout_ref, wrh_ref, wrl_ref, br_ref,
                 x1_ref, h2_ref, route_ref, cnt_ref, win_ref, run_ref, *, per_batch):
    tm = x_ref.shape[0]
    i = pl.program_id(0)

    @pl.when(i == 0)
    def _():
        run_ref[...] = jnp.zeros(run_ref.shape, F32)

    @pl.when(i % per_batch == 0)
    def _():
        win_ref[...] = jnp.zeros(win_ref.shape, F32)

    assert SC_CONV == 3
    prod = sc_ref[...].astype(F32) * sx_ref[...].astype(F32)
    win = jnp.concatenate([win_ref[...], prod], axis=0)
    win_ref[...] = prod[tm - SUBLANES:, :]
    tiled = (win.shape[0] // SUBLANES, SUBLANES, win.shape[1])
    win1 = pltpu.roll(win, 1, 0)
    conv = (win.reshape(tiled) * cw_ref[2][None] + win1.reshape(tiled) * cw_ref[1][None]
            + pltpu.roll(win1, 1, 0).reshape(tiled) * cw_ref[0][None]).reshape(win.shape)[SUBLANES:, :]
    y_b = _dot((sb_ref[...].astype(F32) * conv).astype(BF16), wpb_ref[...])
    y_a = _dot(og_ref[...], wpa_ref[...])
    merged = _sigmoid(ga_ref[...].astype(F32)) * y_a + _sigmoid(gb_ref[...].astype(F32)) * y_b
    mix = _dot(merged.astype(BF16), wout_ref[...])
    x1 = x_ref[...] + mod_ref[0:1, :] * mix
    x1_ref[...] = x1

    y = x1 * lax.rsqrt(jnp.mean(x1 * x1, axis=-1, keepdims=True) + EPS)
    h2 = (y * n2g_ref[...]) * (1.0 + mod_ref[1:2, :]) + mod_ref[2:3, :]
    _write_rows(h2_ref, h2)

    lg = _dot_split(h2, wrh_ref[...], wrl_ref[...]) + br_ref[...]
    lane = lax.broadcasted_iota(jnp.int32, lg.shape, 1).astype(F32)
    neg = jnp.float32(-jnp.inf)
    big = jnp.float32(2 * LANES)

    def first_max(mask):
        vmax = jnp.max(jnp.where(mask, lg, neg), axis=-1, keepdims=True)
        idx = jnp.min(jnp.where(mask & (lg == vmax), lane, big), axis=-1, keepdims=True)
        return vmax, idx

    gmask = lane < N_GROUPS
    g_max, g_sel = first_max(gmask)
    p_group = 1.0 / jnp.sum(jnp.where(gmask, jnp.exp(lg - g_max), 0.0), axis=-1, keepdims=True)
    e_lo = N_GROUPS + EXPERTS_PER_GROUP * g_sel
    emask = (lane >= e_lo) & (lane < e_lo + EXPERTS_PER_GROUP)
    v1, i1 = first_max(emask)
    v2, i2 = first_max(emask & (lane != i1))
    ex = jnp.exp(v2 - v1)
    w1 = p_group * (1.0 / (1.0 + ex))
    w2 = p_group * (ex / (1.0 + ex))
    e1 = i1 - N_GROUPS
    e2 = i2 - N_GROUPS

    onehot = jnp.where((lane == e1) | (lane == e2), 1.0, 0.0).astype(F32)
    row = lax.broadcasted_iota(jnp.int32, (tm, tm), 0)
    col = lax.broadcasted_iota(jnp.int32, (tm, tm), 1)
    before = jnp.where(row > col, 1.0, 0.0).astype(BF16)
    seen = _dot(before, onehot.astype(BF16)) + run_ref[0:1, :]
    r1 = jnp.sum(jnp.where(lane == e1, seen, 0.0), axis=-1, keepdims=True)
    r2 = jnp.sum(jnp.where(lane == e2, seen, 0.0), axis=-1, keepdims=True)
    run_ref[0:1, :] = run_ref[0:1, :] + jnp.sum(onehot, axis=0, keepdims=True)
    cnt_ref[...] = jnp.broadcast_to(run_ref[0:1, :], cnt_ref.shape)

    out = jnp.where(lane == 0, e1, 0.0)
    out = jnp.where(lane == 1, e2, out)
    out = jnp.where(lane == 2, w1, out)
    out = jnp.where(lane == 3, w2, out)
    out = jnp.where(lane == 4, r1, out)
    out = jnp.where(lane == 5, r2, out)
    route_ref[...] = out


def _post(x2, og, proj, mod, n2g, conv_w, wpa, wpb, wout, w_route_hi, w_route_lo, b_route, seq, tm):
    m, d = x2.shape
    assert d == SUBLANES * LANES
    per_batch = seq // tm

    def rows(j):
        return pl.BlockSpec((tm, d), lambda i: (i, j))

    def whole(shape):
        return pl.BlockSpec(shape, lambda i: tuple(0 for _ in shape))

    return pl.pallas_call(
        functools.partial(_post_kernel, per_batch=per_batch),
        grid=(m // tm,),
        in_specs=[
            rows(0), rows(0), rows(4), rows(5), rows(6), rows(7), rows(8),
            pl.BlockSpec((None, SUBLANES, d), lambda i: (i // per_batch, 0, 0)),
            whole((1, d)), whole((SC_CONV, SUBLANES, d)),
            whole((d, d)), whole((d, d)), whole((d, d)),
            whole((d, LANES)), whole((d, LANES)), whole((1, LANES)),
        ],
        out_specs=[
            rows(0), pl.BlockSpec((tm * SUBLANES, LANES), lambda i: (i, 0)),
            pl.BlockSpec((tm, LANES), lambda i: (i, 0)),
            pl.BlockSpec((SUBLANES, LANES), lambda i: (0, 0)),
        ],
        out_shape=[
            jax.ShapeDtypeStruct((m, d), F32),
            jax.ShapeDtypeStruct((m * SUBLANES, LANES), F32),
            jax.ShapeDtypeStruct((m, LANES), F32),
            jax.ShapeDtypeStruct((SUBLANES, LANES), F32),
        ],
        scratch_shapes=[
            pltpu.VMEM((SUBLANES, d), F32),
            pltpu.VMEM((SUBLANES, LANES), F32),
        ],
        compiler_params=_params("post"),
        name="post",
    )(x2, og, proj, proj, proj, proj, proj, mod, n2g, conv_w, wpa, wpb, wout, w_route_hi, w_route_lo,
      b_route)


def _plan_kernel(route_ref, cnt_ref, d_ref):
    bm = EXPERT_BLOCK
    sizes = cnt_ref[...]
    padded = jnp.floor((sizes + (bm - 1.0)) * (1.0 / bm)) * bm
    lane_i = lax.broadcasted_iota(jnp.int32, sizes.shape, 1)
    incl = padded
    s = 1
    while s < LANES:
        incl = incl + jnp.where(lane_i >= s, pltpu.roll(incl, s, 1), 0.0)
        s *= 2
    start = (incl - padded)[0:1, :]
    r = route_ref[...]
    lane = lax.broadcasted_iota(jnp.int32, r.shape, 1).astype(F32)
    d1 = jnp.sum(jnp.where(lane == r[:, 0:1], start, 0.0), axis=-1, keepdims=True) + r[:, 4:5]
    d2 = jnp.sum(jnp.where(lane == r[:, 1:2], start, 0.0), axis=-1, keepdims=True) + r[:, 5:6]
    out = jnp.where(lane == 0.0, d1, jnp.where(lane == 1.0, d2, 0.0))
    d_ref[...] = out.T[0:SUBLANES, :].astype(jnp.int32)


def _plan(route, counts, tm):
    m = route.shape[0]
    return pl.pallas_call(
        _plan_kernel,
        grid=(m // tm,),
        in_specs=[
            pl.BlockSpec((tm, LANES), lambda i: (i, 0)),
            pl.BlockSpec((SUBLANES, LANES), lambda i: (0, 0)),
        ],
        out_specs=pl.BlockSpec((SUBLANES, tm), lambda i: (0, i)),
        out_shape=jax.ShapeDtypeStruct((SUBLANES, m), jnp.int32),
        compiler_params=_params("plan"),
        name="plan",
    )(route, counts)


def _row_copy(src_ref, src_row, dst_ref, dst_row, sem):
    return pltpu.make_async_copy(_row_tile(src_ref, src_row), _row_tile(dst_ref, dst_row), sem)


def _rows_copy(src_ref, dst_ref, dst_row, n, sem):
    return pltpu.make_async_copy(src_ref, dst_ref.at[pl.ds(dst_row * SUBLANES, n * SUBLANES), :], sem)


def _dispatch_kernel(d1_ref, d2_ref, pad_ref, h2_ref, xs_ref, zero_ref, sem):
    tm = h2_ref.shape[0] // SUBLANES
    bm = zero_ref.shape[0] // SUBLANES
    i = pl.program_id(0)
    t0 = i * tm
    n_blocks = xs_ref.shape[0] // (bm * SUBLANES)

    def zero_rows(row, n):
        dst = xs_ref.at[pl.ds(pl.multiple_of(row * SUBLANES, SUBLANES), n * SUBLANES), :]
        return pltpu.make_async_copy(zero_ref.at[pl.ds(0, n * SUBLANES), :], dst, sem.at[2])

    def zero_fill(act):
        def expert(e, carry):
            row, cnt = pad_ref[e], pad_ref[N_EXPERTS + e]
            n = bm // 2
            while n >= 1:
                @pl.when((cnt & n) != 0)
                def _():
                    act(zero_rows(row + (cnt & (-2 * n)), n))
                n //= 2
            return carry

        lax.fori_loop(0, N_EXPERTS, expert, 0)
        lax.fori_loop(pad_ref[2 * N_EXPERTS], n_blocks, lambda j, c: (act(zero_rows(j * bm, bm)), c)[1], 0)

    @pl.when(i == 0)
    def _():
        zero_ref[...] = jnp.zeros(zero_ref.shape, F32)
        zero_fill(lambda copy: copy.start())

    def start(r, carry):
        _row_copy(h2_ref, r, xs_ref, d1_ref[t0 + r], sem.at[0]).start(priority=0)
        _row_copy(h2_ref, r, xs_ref, d2_ref[t0 + r], sem.at[1]).start(priority=1)
        return carry

    lax.fori_loop(0, tm, start, 0, unroll=DMA_UNROLL)
    _rows_copy(h2_ref, xs_ref, 0, tm, sem.at[0]).wait()
    _rows_copy(h2_ref, xs_ref, 0, tm, sem.at[1]).wait()

    @pl.when(i == pl.num_programs(0) - 1)
    def _():
        zero_fill(lambda copy: copy.wait())


def _dispatch(d1, d2, pad_info, h2, cap, tm):
    m = h2.shape[0] // SUBLANES
    return pl.pallas_call(
        _dispatch_kernel,
        grid_spec=pltpu.PrefetchScalarGridSpec(
            num_scalar_prefetch=3,
            grid=(m // tm,),
            in_specs=[pl.BlockSpec((tm * SUBLANES, LANES), lambda i, d1, d2, tl: (i, 0))],
            out_specs=pl.BlockSpec(memory_space=pl.ANY),
            scratch_shapes=[pltpu.VMEM((EXPERT_BLOCK * SUBLANES, LANES), F32),
                            pltpu.SemaphoreType.DMA((3,))],
        ),
        out_shape=jax.ShapeDtypeStruct((cap * SUBLANES, LANES), F32),
        compiler_params=_params("dispatch"),
        name="dispatch",
    )(d1, d2, pad_info, h2)


def _expert_kernel(be_ref, act_ref, nxt_ref, x_ref, w1_hbm, w3_hbm, w2_hbm, y_ref, w1b_ref, w3b_ref, w2b_ref,
                   w1s_ref, w3s_ref, w2s_ref, sem):
    b = pl.program_id(0)
    weights = ((w1_hbm, w1s_ref, w1b_ref), (w3_hbm, w3s_ref, w3b_ref), (w2_hbm, w2s_ref, w2b_ref))

    def weight_copy(k, e):
        w_hbm, ws_ref, _ = weights[k]
        return pltpu.make_async_copy(w_hbm.at[e], ws_ref, sem.at[k])

    @pl.when(act_ref[b] > 0)
    def _():
        @pl.when(b == 0)
        def _():
            for k in range(len(weights)):
                weight_copy(k, be_ref[b]).start()

        @pl.when((b == 0) | (be_ref[b] != be_ref[jnp.maximum(b - 1, 0)]))
        def _():
            for k, (_, ws_ref, wb_ref) in enumerate(weights):
                weight_copy(k, be_ref[b]).wait()
                wb_ref[...] = ws_ref[...].astype(BF16)

                @pl.when(nxt_ref[b] >= 0)
                def _():
                    weight_copy(k, nxt_ref[b]).start()

        xb = _read_rows(x_ref, x_ref.shape[0] // SUBLANES).astype(BF16)
        hid = _silu(_dot(xb, w1b_ref[...])) * _dot(xb, w3b_ref[...])
        _write_rows(y_ref, _dot(hid.astype(BF16), w2b_ref[...]))

    @pl.when(act_ref[b] == 0)
    def _():
        y_ref[...] = jnp.zeros(y_ref.shape, F32)


def _experts(block_expert, block_active, block_next, xs, w1, w3, w2):
    cap = xs.shape[0] // SUBLANES
    d, de = w1.shape[1], w1.shape[2]
    bm = EXPERT_BLOCK

    def x_block(b, be, act, nxt):
        return (jnp.minimum(b, jnp.maximum(act[cap // bm], 1) - 1), 0)

    return pl.pallas_call(
        _expert_kernel,
        grid_spec=pltpu.PrefetchScalarGridSpec(
            num_scalar_prefetch=3,
            grid=(cap // bm,),
            in_specs=[
                pl.BlockSpec((bm * SUBLANES, LANES), x_block),
                pl.BlockSpec(memory_space=pl.ANY),
                pl.BlockSpec(memory_space=pl.ANY),
                pl.BlockSpec(memory_space=pl.ANY),
            ],
            out_specs=pl.BlockSpec((bm * SUBLANES, LANES), lambda b, be, act, nxt: (b, 0)),
            scratch_shapes=[pltpu.VMEM((d, de), BF16), pltpu.VMEM((d, de), BF16), pltpu.VMEM((de, d), BF16),
                            pltpu.VMEM((d, de), F32), pltpu.VMEM((d, de), F32), pltpu.VMEM((de, d), F32),
                            pltpu.SemaphoreType.DMA((3,))],
        ),
        out_shape=jax.ShapeDtypeStruct((cap * SUBLANES, LANES), F32),
        compiler_params=_params("experts"),
        name="experts",
    )(block_expert, block_active, block_next, xs, w1, w3, w2)


def _final_kernel(d1_ref, d2_ref, x1_ref, route_ref, gt_ref, nfg_ref, ys_ref, o_ref, buf_ref, sem):
    tm = x1_ref.shape[0]
    i = pl.program_id(0)

    def gather(tile, s):
        t0 = tile * tm

        def start(r, carry):
            _row_copy(ys_ref, d1_ref[t0 + r], buf_ref.at[s, 0], r, sem.at[s, 0]).start(priority=0)
            _row_copy(ys_ref, d2_ref[t0 + r], buf_ref.at[s, 1], r, sem.at[s, 1]).start(priority=1)
            return carry

        lax.fori_loop(0, tm, start, 0, unroll=DMA_UNROLL)

    def wait_rows(s):
        for k in range(2):
            pltpu.make_async_copy(ys_ref.at[pl.ds(0, tm * SUBLANES), :], buf_ref.at[s, k],
                                  sem.at[s, k]).wait()

    def combine(cur, nxt):
        wait_rows(cur)
        last = pl.num_programs(0) - 1
        t0 = jnp.minimum(i + 1, last) * tm
        for r in range(tm):
            _row_copy(ys_ref, d1_ref[t0 + r], buf_ref.at[nxt, 0], r, sem.at[nxt, 0]).start(priority=0)
            _row_copy(ys_ref, d2_ref[t0 + r], buf_ref.at[nxt, 1], r, sem.at[nxt, 1]).start(priority=1)
        route = route_ref[...]
        moe = (_read_rows(buf_ref.at[cur, 0], tm) * route[:, 2:3]
               + _read_rows(buf_ref.at[cur, 1], tm) * route[:, 3:4])
        x2 = x1_ref[...] + gt_ref[...] * moe
        y = x2 * lax.rsqrt(jnp.mean(x2 * x2, axis=-1, keepdims=True) + EPS)
        o_ref[...] = y * nfg_ref[...]

        @pl.when(i == last)
        def _():
            wait_rows(nxt)

    @pl.when(i == 0)
    def _():
        gather(0, 0)

    @pl.when(i % 2 == 0)
    def _():
        combine(0, 1)

    @pl.when(i % 2 == 1)
    def _():
        combine(1, 0)


def _final(d1, d2, x1, route, gt2, nfg, ys, seq, tm):
    m, d = x1.shape
    per_batch = seq // tm
    return pl.pallas_call(
        _final_kernel,
        grid_spec=pltpu.PrefetchScalarGridSpec(
            num_scalar_prefetch=2,
            grid=(m // tm,),
            in_specs=[
                pl.BlockSpec((tm, d), lambda i, d1, d2: (i, 0)),
                pl.BlockSpec((tm, LANES), lambda i, d1, d2: (i, 0)),
                pl.BlockSpec((None, 1, d), lambda i, d1, d2: (i // per_batch, 0, 0)),
                pl.BlockSpec((1, d), lambda i, d1, d2: (0, 0)),
                pl.BlockSpec(memory_space=pl.ANY),
            ],
            out_specs=pl.BlockSpec((tm, d), lambda i, d1, d2: (i, 0)),
            scratch_shapes=[pltpu.VMEM((2, 2, tm * SUBLANES, LANES), F32),
                            pltpu.SemaphoreType.DMA((2, 2))],
        ),
        out_shape=jax.ShapeDtypeStruct((m, d), F32),
        compiler_params=_params("final"),
        name="final",
    )(d1, d2, x1, route, gt2, nfg, ys)


def _tile(n, pref):
    t = min(n, pref)
    assert n % t == 0
    return t


def kernel(x, c, w_ada, b_ada, norm1_g, w_in, conv_qkv_w, a_log, dt_bias, onorm_g, w_proj_a,
           conv_sc_w, w_proj_b, w_out, norm2_g, w_group, b_group, w_expert, b_expert, w1, w3, w2,
           normf_g):
    batch, seq, d = x.shape
    depth = w_ada.shape[0]
    m = batch * seq
    kd = N_HEADS * HEAD_D
    assert d == kd and seq % CHUNK == 0 and batch <= SUBLANES
    assert depth == 1, "the last stage applies the final rmsnorm: one layer only"
    x2 = x.reshape(m, d)

    for l in range(depth):
        mod = _ada(c, w_ada[l], b_ada[l][None, :])[:batch]
        sh1, sc1, gt1, sh2, sc2, gt2 = [mod[:, None, j * d:(j + 1) * d] for j in range(6)]

        w = w_in[l]
        o_ba = 3 * kd + kd
        proj, ba = _inproj(x2, norm1_g[l][None, :], sc1, sh1, jnp.swapaxes(w, 0, 1), o_ba // d, 2 * N_HEADS,
                           seq, _tile(seq, INPROJ_ROWS))

        head_params = jnp.zeros((SUBLANES, LANES), F32)
        head_params = head_params.at[0, N_HEADS:2 * N_HEADS].set(a_log[l])
        head_params = head_params.at[1, N_HEADS:2 * N_HEADS].set(dt_bias[l])
        head_params = head_params.at[2, :].set(onorm_g[l])
        conv_taps = jnp.broadcast_to(conv_qkv_w[l][:, None, :], (QKV_CONV, SUBLANES, 3 * kd))
        og = _gdn(proj, ba, conv_taps, head_params, batch, seq, _tile(seq, GDN_ROWS))

        mod_post = jnp.zeros((batch, SUBLANES, d), F32)
        mod_post = mod_post.at[:, 0:1].set(gt1).at[:, 1:2].set(sc2).at[:, 2:3].set(sh2)
        w_route = jnp.zeros((d, LANES), F32)
        w_route = w_route.at[:, :N_GROUPS].set(w_group[l]).at[:, N_GROUPS:N_GROUPS + N_EXPERTS].set(w_expert[l])
        w_route_hi = w_route.astype(BF16)
        w_route_lo = (w_route - w_route_hi.astype(F32)).astype(BF16)
        b_route = jnp.zeros((1, LANES), F32)
        b_route = b_route.at[0, :N_GROUPS].set(b_group[l]).at[0, N_GROUPS:N_GROUPS + N_EXPERTS].set(b_expert[l])
        x1, h2, route, counts = _post(
            x2, og, proj, mod_post, norm2_g[l][None, :],
            jnp.broadcast_to(conv_sc_w[l][:, None, :], (SC_CONV, SUBLANES, d)),
            w_proj_a[l].astype(BF16), w_proj_b[l].astype(BF16), w_out[l].astype(BF16),
            w_route_hi, w_route_lo, b_route, seq, _tile(seq, POST_ROWS))

        bm = EXPERT_BLOCK
        n_blocks = (2 * m) // bm + N_EXPERTS
        sizes = counts[0, :N_EXPERTS].astype(jnp.int32)
        padded = ((sizes + bm - 1) // bm) * bm
        pad_end = jnp.cumsum(padded)
        pad_info = jnp.concatenate([pad_end - padded + sizes, padded - sizes,
                                    pad_end[-1:] // bm]).astype(jnp.int32)
        block_row = jnp.arange(n_blocks, dtype=jnp.int32) * bm
        block_expert = jnp.minimum(jnp.sum(block_row[:, None] >= pad_end[None, :], axis=1),
                                   N_EXPERTS - 1).astype(jnp.int32)
        block_active = jnp.concatenate([(block_row < pad_end[-1]).astype(jnp.int32),
                                        (pad_end[-1:] // bm).astype(jnp.int32)])
        expert_id = jnp.arange(N_EXPERTS, dtype=jnp.int32)
        later = (expert_id[None, :] > expert_id[:, None]) & (padded[None, :] > 0)
        next_expert = jnp.min(jnp.where(later, expert_id[None, :], N_EXPERTS), axis=1)
        block_next = jnp.where(next_expert < N_EXPERTS, next_expert, -1)[block_expert].astype(jnp.int32)
        dest = _plan(route, counts, _tile(seq, INPROJ_ROWS))
        d1, d2 = dest[0], dest[1]

        xs = _dispatch(d1, d2, pad_info, h2, n_blocks * bm, _tile(seq, DISPATCH_ROWS))
        ys = _experts(block_expert, block_active, block_next, xs, w1[l], w3[l], w2[l])
        nfg = normf_g[None, :]
        x2 = _final(d1, d2, x1, route, gt2, nfg, ys, seq, _tile(seq, FINAL_ROWS))
    return x2.reshape(batch, seq, d)
```

```python
import functools

import jax
import jax.numpy as jnp
from jax import lax
from jax.experimental import pallas as pl
from jax.experimental.pallas import tpu as pltpu

F32 = jnp.float32
BF16 = jnp.bfloat16
HIGHEST = lax.Precision.HIGHEST

N_HEADS = 8
HEAD_D = 128
CHUNK = 64
QKV_CONV = 4
SC_CONV = 3
N_GROUPS = 4
EXPERTS_PER_GROUP = 8
N_EXPERTS = N_GROUPS * EXPERTS_PER_GROUP
EPS = 1e-6

LANES = 128
SUBLANES = 8
BF16_ROWS = 16
EXPERT_BLOCK = 512
INPROJ_ROWS = 2048
GDN_ROWS = 256
PRE_CHUNKS = 2
POST_ROWS = 512
DISPATCH_ROWS = 1024
FINAL_ROWS = 512
DMA_UNROLL = 8
MIB = 1024 * 1024
VMEM_MIB = {"ada": 24, "inproj": 56, "gdn": 48, "post": 56, "plan": 16, "dispatch": 24, "experts": 40,
            "final": 32}


def _sigmoid(x):
    return 0.5 + 0.5 * jnp.tanh(0.5 * x)


def _silu(x):
    half = 0.5 * x
    return half + half * jnp.tanh(half)


def _softplus(x):
    return jnp.maximum(x, 0.0) + jnp.log(1.0 + jnp.exp(-jnp.abs(x)))


def _dot(a, b):
    return jnp.dot(a, b, preferred_element_type=F32)


def _dot_nt(a, b):
    return lax.dot_general(a, b, (((1,), (1,)), ((), ())), preferred_element_type=F32)


def _dot_hi(a, b):
    return jnp.dot(a, b, preferred_element_type=F32, precision=HIGHEST)


def _dot_split(a, b_hi, b_lo):
    a_hi = a.astype(BF16)
    a_lo = (a - a_hi.astype(F32)).astype(BF16)
    return _dot(a_hi, b_hi) + (_dot(a_lo, b_hi) + _dot(a_hi, b_lo))


def _read_rows(ref, rows):
    return jnp.concatenate([ref[pl.ds(j, rows, stride=SUBLANES), :] for j in range(SUBLANES)], axis=1)


def _write_rows(ref, value):
    rows = value.shape[0]
    for j in range(SUBLANES):
        ref[pl.ds(j, rows, stride=SUBLANES), :] = value[:, j * LANES:(j + 1) * LANES]


def _row_tile(ref, row):
    if not isinstance(row, int):
        row = pl.multiple_of(row * SUBLANES, SUBLANES)
    else:
        row = row * SUBLANES
    return ref.at[pl.ds(row, SUBLANES), :]


def _params(call, n_grid_axes=1):
    return pltpu.CompilerParams(dimension_semantics=("arbitrary",) * n_grid_axes,
                                vmem_limit_bytes=VMEM_MIB[call] * MIB)


def _ada_kernel(cb_ref, w_ref, b_ref, o_ref, s_ref):
    batch, d, _ = cb_ref.shape
    groups = w_ref.shape[1] // LANES

    @pl.when(pl.program_id(0) == 0)
    def _():
        s_ref[...] = _silu(cb_ref[...])

    def body(kc, accs):
        k0 = pl.multiple_of(kc * SUBLANES, SUBLANES)
        w = w_ref[pl.ds(k0, SUBLANES), :]
        out = []
        for b in range(batch):
            s = s_ref[b, pl.ds(k0, SUBLANES), :]
            out += [accs[b * groups + g] + w[:, g * LANES:(g + 1) * LANES] * s for g in range(groups)]
        return tuple(out)

    zero = jnp.zeros((SUBLANES, LANES), F32)
    accs = lax.fori_loop(0, d // SUBLANES, body, (zero,) * (batch * groups), unroll=4)
    o_ref[...] = jnp.zeros(o_ref.shape, F32)
    for b in range(batch):
        for g in range(groups):
            cols = slice(g * LANES, (g + 1) * LANES)
            o_ref[b:b + 1, cols] = jnp.sum(accs[b * groups + g], axis=0, keepdims=True) + b_ref[:, cols]


def _ada(c, w_ada, b_ada):
    batch, d = c.shape
    n = w_ada.shape[1]
    cb = jnp.broadcast_to(c[:, :, None], (batch, d, LANES))
    return pl.pallas_call(
        _ada_kernel,
        grid=(n // d,),
        in_specs=[
            pl.BlockSpec((batch, d, LANES), lambda j: (0, 0, 0)),
            pl.BlockSpec((d, d), lambda j: (0, j)),
            pl.BlockSpec((1, d), lambda j: (0, j)),
        ],
        out_specs=pl.BlockSpec((SUBLANES, d), lambda j: (0, j)),
        out_shape=jax.ShapeDtypeStruct((SUBLANES, n), F32),
        scratch_shapes=[pltpu.VMEM((batch, d, LANES), F32)],
        compiler_params=_params("ada"),
        name="ada",
    )(cb, w_ada, b_ada)


def _inproj_kernel(x_hbm, g_ref, sc_ref, sh_ref, wt_hbm, o_ref, ba_ref, x_ref, h_ref, w_ref, wba_ref, stage_ref,
                   stage_ba_ref, sem_x, sem, sem_ba, *, n_lead, n_ba):
    i, j = pl.program_id(0), pl.program_id(1)
    n_tiles = pl.num_programs(0)
    n_blocks, tn = w_ref.shape[0], o_ref.shape[1]
    tm, half = x_ref.shape[0], stage_ref.shape[1]

    def x_copy(tile):
        return pltpu.make_async_copy(x_hbm.at[pl.ds(pl.multiple_of(tile * tm, tm), tm)], x_ref, sem_x)

    def block_copy(jb, part):
        row = jb * tn + jnp.where(jb >= n_lead, n_ba, 0) + part * half
        return pltpu.make_async_copy(wt_hbm.at[pl.ds(pl.multiple_of(row, SUBLANES), half)],
                                     stage_ref.at[part], sem.at[part])

    def ba_copy():
        return pltpu.make_async_copy(wt_hbm.at[pl.ds(n_lead * tn, n_ba)], stage_ba_ref, sem_ba)

    @pl.when(j == 0)
    def _():
        @pl.when(i == 0)
        def _():
            x_copy(i).start()
            ba_copy().start()
            for part in range(2):
                block_copy(j, part).start()

        x_copy(i).wait()
        x = x_ref[...]
        y = x * lax.rsqrt(jnp.mean(x * x, axis=-1, keepdims=True) + EPS)
        h = (y * g_ref[...]) * (1.0 + sc_ref[...]) + sh_ref[...]
        hb = h.astype(BF16)
        h_ref[...] = hb

        @pl.when(i == 0)
        def _():
            ba_copy().wait()
            wba_ref[...] = jnp.zeros(wba_ref.shape, BF16)
            wba_ref[0:n_ba, :] = stage_ba_ref[...].astype(BF16)

        ba_ref[...] = _dot_nt(hb, wba_ref[...])

    @pl.when(jnp.logical_and(j == 1, i + 1 < n_tiles))
    def _():
        x_copy(i + 1).start()

    @pl.when(i == 0)
    def _():
        for part in range(2):
            block_copy(j, part).wait()
            w_ref[j, pl.ds(part * half, half), :] = stage_ref[part].astype(BF16)

            @pl.when(j + 1 < n_blocks)
            def _():
                block_copy(j + 1, part).start()

    rows = o_ref.shape[0] // 2
    for r in range(2):
        o_ref[pl.ds(r * rows, rows), :] = _dot_nt(h_ref[pl.ds(r * rows, rows), :], w_ref[j]).astype(BF16)


def _inproj(x2, norm_g, sc, sh, wt, n_lead, n_ba, seq, tm):
    m, d = x2.shape
    tn = d
    n = wt.shape[0] - n_ba
    assert n % tn == 0 and n_ba % (2 * SUBLANES) == 0 and n_ba <= LANES
    per_batch = seq // tm
    return pl.pallas_call(
        functools.partial(_inproj_kernel, n_lead=n_lead, n_ba=n_ba),
        grid=(m // tm, n // tn),
        in_specs=[
            pl.BlockSpec(memory_space=pl.ANY),
            pl.BlockSpec((1, d), lambda i, j: (0, 0)),
            pl.BlockSpec((None, 1, d), lambda i, j: (i // per_batch, 0, 0)),
            pl.BlockSpec((None, 1, d), lambda i, j: (i // per_batch, 0, 0)),
            pl.BlockSpec(memory_space=pl.ANY),
        ],
        out_specs=[
            pl.BlockSpec((tm, tn), lambda i, j: (i, j)),
            pl.BlockSpec((tm, LANES), lambda i, j: (i, 0)),
        ],
        out_shape=[
            jax.ShapeDtypeStruct((m, n), BF16),
            jax.ShapeDtypeStruct((m, LANES), F32),
        ],
        scratch_shapes=[
            pltpu.VMEM((tm, d), F32),
            pltpu.VMEM((tm, d), BF16),
            pltpu.VMEM((n // tn, tn, d), BF16),
            pltpu.VMEM((LANES, d), BF16),
            pltpu.VMEM((2, tn // 2, d), F32),
            pltpu.VMEM((n_ba, d), F32),
            pltpu.SemaphoreType.DMA(()),
            pltpu.SemaphoreType.DMA((2,)),
            pltpu.SemaphoreType.DMA(()),
        ],
        compiler_params=_params("inproj", 2),
        name="inproj",
    )(x2, norm_g, sc, sh, wt)


def _bmm(a, b):
    return jnp.einsum("hmk,hkn->hmn", a.astype(BF16), b.astype(BF16), preferred_element_type=F32)


def _bmm_nt(a, b):
    return jnp.einsum("hmk,hnk->hmn", a.astype(BF16), b.astype(BF16), preferred_element_type=F32)


def _unit_lower_inverse(a):
    row = lax.broadcasted_iota(jnp.int32, a.shape[1:], 0)
    col = lax.broadcasted_iota(jnp.int32, a.shape[1:], 1)
    apart = row ^ col
    eye = jnp.where(row == col, 1.0, 0.0).astype(F32)
    t = jnp.where(apart < 2, eye - a, 0.0)
    s = 2
    while s < CHUNK:
        coupling = jnp.where((apart >= s) & (apart < 2 * s), a, 0.0)
        t = t - _bmm(t, _bmm(coupling, t))
        s *= 2
    return t


def _lane_sums(x):
    h, rows, width = x.shape
    ones = jnp.ones((width, width), BF16)
    return _dot(x.reshape(h * rows, width).astype(BF16), ones).reshape(h, rows, width)


def _causal_conv_silu(win, cw, k_w):
    assert k_w == 4
    tiled = (win.shape[0] // SUBLANES, SUBLANES, win.shape[1])

    def pair(x, x1, j):
        return (x.reshape(tiled) * cw[j][None] + x1.reshape(tiled) * cw[j - 1][None]).reshape(win.shape)

    win1 = pltpu.roll(win, 1, 0)
    acc = pair(win, win1, 3) + pltpu.roll(pair(win, win1, 1), 2, 0)
    return _silu(acc[SUBLANES:, :])


def _gdn_kernel(q_ref, k_ref, v_ref, z_ref, ba_ref, cw_ref, hp_ref, o_ref,
                s_ref, tail_ref, wq_ref, u_ref, ik_ref, dec_ref):
    nb, tb = q_ref.shape[0], q_ref.shape[1]
    kd = N_HEADS * HEAD_D
    nbh = nb * N_HEADS

    @pl.when(pl.program_id(0) == 0)
    def _():
        s_ref[...] = jnp.zeros(s_ref.shape, F32)
        tail_ref[...] = jnp.zeros(tail_ref.shape, F32)

    row = lax.broadcasted_iota(jnp.int32, (CHUNK, CHUNK), 0)
    col = lax.broadcasted_iota(jnp.int32, (CHUNK, CHUNK), 1)
    causal = row >= col
    strict = row > col
    tril = jnp.where(causal, 1.0, 0.0).astype(F32)
    a_log = hp_ref[0:1, :]
    dt_bias = hp_ref[1:2, :]
    onorm_g = hp_ref[2:3, :]
    zeros_half = jnp.zeros((CHUNK, HEAD_D), F32)

    def precompute(cp, carry):
        qs, ks, vs, betas, gcs, grs, gls = [], [], [], [], [], [], []
        for sub in range(PRE_CHUNKS):
            base = pl.multiple_of((cp * PRE_CHUNKS + sub) * CHUNK, CHUNK)
            prev = pl.multiple_of(base - BF16_ROWS, BF16_ROWS)
            for b in range(nb):
                ba = ba_ref[b, pl.ds(base, CHUNK), :]
                beta_all = _sigmoid(ba)
                g_all = -jnp.exp(a_log) * _softplus(ba + dt_bias)
                gcum = _dot_hi(tril, g_all)
                gcum_t = jnp.concatenate([gcum, gcum], axis=0).T
                for h in range(N_HEADS):
                    lo, hi = h * HEAD_D, (h + 1) * HEAD_D

                    def conv(ref, off):
                        cur = ref[b, pl.ds(base, CHUNK), lo:hi].astype(F32)
                        if sub == 0:
                            before = tail_ref[b, :, off + lo:off + hi]
                        else:
                            before = ref[b, pl.ds(prev, BF16_ROWS), lo:hi].astype(F32)[SUBLANES:]
                        win = jnp.concatenate([before, cur], axis=0)
                        return _causal_conv_silu(win, cw_ref[:, :, off + lo:off + hi], QKV_CONV)

                    qs.append(conv(q_ref, 0))
                    ks.append(conv(k_ref, kd))
                    vs.append(conv(v_ref, 2 * kd))
                    betas.append(beta_all[:, h:h + 1])
                    gcs.append(gcum[:, N_HEADS + h:N_HEADS + h + 1])
                    grs.append(gcum_t[N_HEADS + h:N_HEADS + h + 1, 0:CHUNK])
                    gls.append(gcum[CHUNK - 1:CHUNK, N_HEADS + h:N_HEADS + h + 1])
        q, k, v = jnp.stack(qs), jnp.stack(ks), jnp.stack(vs)
        beta, gc, gr, gl = jnp.stack(betas), jnp.stack(gcs), jnp.stack(grs), jnp.stack(gls)
        qn = q * (lax.rsqrt(_lane_sums(q * q) + EPS) * (HEAD_D ** -0.5))
        kn = k * lax.rsqrt(_lane_sums(k * k) + EPS)
        decay = jnp.where(causal, jnp.exp(jnp.where(causal, gc - gr, 0.0)), 0.0)
        kb = kn * beta
        e_gc = jnp.exp(gc)
        kq = _bmm_nt(jnp.concatenate([kb, qn], axis=1), kn)
        a = jnp.where(strict, kq[:, :CHUNK] * decay, 0.0)
        intra = kq[:, CHUNK:] * decay
        uw = _bmm(_unit_lower_inverse(a), jnp.concatenate([v * beta, kb * e_gc], axis=2))
        wq = jnp.concatenate([uw[:, :, HEAD_D:], qn * e_gc], axis=1).astype(BF16)
        k_dec = kn * jnp.exp(gl - gc)
        k_dec_t = jnp.stack([jnp.concatenate([k_dec[i], zeros_half], axis=0).T[:, :CHUNK]
                             for i in range(PRE_CHUNKS * nbh)])
        ik = jnp.concatenate([intra, k_dec_t], axis=1).astype(BF16)
        dec = jnp.broadcast_to(jnp.exp(gl), (PRE_CHUNKS * nbh, 1, HEAD_D))
        for sub in range(PRE_CHUNKS):
            c = cp * PRE_CHUNKS + sub
            rows = slice(sub * nbh, (sub + 1) * nbh)
            u_ref[c] = uw[rows, :, :HEAD_D]
            wq_ref[c] = wq[rows]
            ik_ref[c] = ik[rows]
            dec_ref[c] = dec[rows]

        last = pl.multiple_of((cp + 1) * PRE_CHUNKS * CHUNK - BF16_ROWS, BF16_ROWS)
        for b in range(nb):
            for j, ref in enumerate((q_ref, k_ref, v_ref)):
                rows = ref[b, pl.ds(last, BF16_ROWS), :].astype(F32)
                tail_ref[b, :, j * kd:(j + 1) * kd] = rows[BF16_ROWS - SUBLANES:, :]
        return carry

    def recur(c, carry):
        base = pl.multiple_of(c * CHUNK, CHUNK)
        state = s_ref[...]
        ws = _bmm(wq_ref[c], state)
        v_new = u_ref[c] - ws[:, :CHUNK]
        r = _bmm(ik_ref[c], v_new)
        o = ws[:, CHUNK:] + r[:, :CHUNK]
        s_ref[...] = state * dec_ref[c] + r[:, CHUNK:]
        on = o * lax.rsqrt(jnp.mean(o * o, axis=-1, keepdims=True) + EPS) * onorm_g
        for b in range(nb):
            for h in range(N_HEADS):
                lo, hi = h * HEAD_D, (h + 1) * HEAD_D
                z = z_ref[b, pl.ds(base, CHUNK), lo:hi].astype(F32)
                o_ref[b, pl.ds(base, CHUNK), lo:hi] = (on[b * N_HEADS + h] * _silu(z)).astype(BF16)
        return carry

    lax.fori_loop(0, tb // (PRE_CHUNKS * CHUNK), precompute, 0)
    lax.fori_loop(0, tb // CHUNK, recur, 0)


def _gdn(proj, ba, conv_w, head_params, batch, seq, tb):
    kd = N_HEADS * HEAD_D
    nc = tb // CHUNK
    nbh = batch * N_HEADS
    proj3 = proj.reshape(batch, seq, proj.shape[1])
    ba3 = ba.reshape(batch, seq, LANES)

    def col(j):
        return pl.BlockSpec((batch, tb, kd), lambda t: (0, t, j))

    out = pl.pallas_call(
        _gdn_kernel,
        grid=(seq // tb,),
        in_specs=[
            col(0), col(1), col(2), col(3),
            pl.BlockSpec((batch, tb, LANES), lambda t: (0, t, 0)),
            pl.BlockSpec((QKV_CONV, SUBLANES, 3 * kd), lambda t: (0, 0, 0)),
            pl.BlockSpec((SUBLANES, LANES), lambda t: (0, 0)),
        ],
        out_specs=pl.BlockSpec((batch, tb, kd), lambda t: (0, t, 0)),
        out_shape=jax.ShapeDtypeStruct((batch, seq, kd), BF16),
        scratch_shapes=[
            pltpu.VMEM((nbh, HEAD_D, HEAD_D), F32),
            pltpu.VMEM((batch, SUBLANES, 3 * kd), F32),
            pltpu.VMEM((nc, nbh, 2 * CHUNK, HEAD_D), BF16),
            pltpu.VMEM((nc, nbh, CHUNK, HEAD_D), F32),
            pltpu.VMEM((nc, nbh, CHUNK + HEAD_D, CHUNK), BF16),
            pltpu.VMEM((nc, nbh, 1, HEAD_D), F32),
        ],
        compiler_params=_params("gdn"),
        name="gdn",
    )(proj3, proj3, proj3, proj3, ba3, conv_w, head_params)
    return out.reshape(batch * seq, kd)


def _post_kernel(x_ref, og_ref, sb_ref, sc_ref, sx_ref, ga_ref, gb_ref, mod_ref, n2g_ref, cw_ref,
                 wpa_ref, wpb_ref, wout_ref, wrh_ref, wrl_ref, br_ref,
                 x1_ref, h2_ref, route_ref, cnt_ref, win_ref, run_ref, *, per_batch):
    tm = x_ref.shape[0]
    i = pl.program_id(0)

    @pl.when(i == 0)
    def _():
        run_ref[...] = jnp.zeros(run_ref.shape, F32)

    @pl.when(i % per_batch == 0)
    def _():
        win_ref[...] = jnp.zeros(win_ref.shape, F32)

    assert SC_CONV == 3
    prod = sc_ref[...].astype(F32) * sx_ref[...].astype(F32)
    win = jnp.concatenate([win_ref[...], prod], axis=0)
    win_ref[...] = prod[tm - SUBLANES:, :]
    tiled = (win.shape[0] // SUBLANES, SUBLANES, win.shape[1])
    win1 = pltpu.roll(win, 1, 0)
    conv = (win.reshape(tiled) * cw_ref[2][None] + win1.reshape(tiled) * cw_ref[1][None]
            + pltpu.roll(win1, 1, 0).reshape(tiled) * cw_ref[0][None]).reshape(win.shape)[SUBLANES:, :]
    y_b = _dot((sb_ref[...].astype(F32) * conv).astype(BF16), wpb_ref[...])
    y_a = _dot(og_ref[...], wpa_ref[...])
    merged = _sigmoid(ga_ref[...].astype(F32)) * y_a + _sigmoid(gb_ref[...].astype(F32)) * y_b
    mix = _dot(merged.astype(BF16), wout_ref[...])
    x1 = x_ref[...] + mod_ref[0:1, :] * mix
    x1_ref[...] = x1

    y = x1 * lax.rsqrt(jnp.mean(x1 * x1, axis=-1, keepdims=True) + EPS)
    h2 = (y * n2g_ref[...]) * (1.0 + mod_ref[1:2, :]) + mod_ref[2:3, :]
    _write_rows(h2_ref, h2)

    lg = _dot_split(h2, wrh_ref[...], wrl_ref[...]) + br_ref[...]
    lane = lax.broadcasted_iota(jnp.int32, lg.shape, 1).astype(F32)
    neg = jnp.float32(-jnp.inf)
    big = jnp.float32(2 * LANES)

    def first_max(mask):
        vmax = jnp.max(jnp.where(mask, lg, neg), axis=-1, keepdims=True)
        idx = jnp.min(jnp.where(mask & (lg == vmax), lane, big), axis=-1, keepdims=True)
        return vmax, idx

    gmask = lane < N_GROUPS
    g_max, g_sel = first_max(gmask)
    p_group = 1.0 / jnp.sum(jnp.where(gmask, jnp.exp(lg - g_max), 0.0), axis=-1, keepdims=True)
    e_lo = N_GROUPS + EXPERTS_PER_GROUP * g_sel
    emask = (lane >= e_lo) & (lane < e_lo + EXPERTS_PER_GROUP)
    v1, i1 = first_max(emask)
    v2, i2 = first_max(emask & (lane != i1))
    ex = jnp.exp(v2 - v1)
    w1 = p_group * (1.0 / (1.0 + ex))
    w2 = p_group * (ex / (1.0 + ex))
    e1 = i1 - N_GROUPS
    e2 = i2 - N_GROUPS

    onehot = jnp.where((lane == e1) | (lane == e2), 1.0, 0.0).astype(F32)
    row = lax.broadcasted_iota(jnp.int32, (tm, tm), 0)
    col = lax.broadcasted_iota(jnp.int32, (tm, tm), 1)
    before = jnp.where(row > col, 1.0, 0.0).astype(BF16)
    seen = _dot(before, onehot.astype(BF16)) + run_ref[0:1, :]
    r1 = jnp.sum(jnp.where(lane == e1, seen, 0.0), axis=-1, keepdims=True)
    r2 = jnp.sum(jnp.where(lane == e2, seen, 0.0), axis=-1, keepdims=True)
    run_ref[0:1, :] = run_ref[0:1, :] + jnp.sum(onehot, axis=0, keepdims=True)
    cnt_ref[...] = jnp.broadcast_to(run_ref[0:1, :], cnt_ref.shape)

    out = jnp.where(lane == 0, e1, 0.0)
    out = jnp.where(lane == 1, e2, out)
    out = jnp.where(lane == 2, w1, out)
    out = jnp.where(lane == 3, w2, out)
    out = jnp.where(lane == 4, r1, out)
    out = jnp.where(lane == 5, r2, out)
    route_ref[...] = out


def _post(x2, og, proj, mod, n2g, conv_w, wpa, wpb, wout, w_route_hi, w_route_lo, b_route, seq, tm):
    m, d = x2.shape
    assert d == SUBLANES * LANES
    per_batch = seq // tm

    def rows(j):
        return pl.BlockSpec((tm, d), lambda i: (i, j))

    def whole(shape):
        return pl.BlockSpec(shape, lambda i: tuple(0 for _ in shape))

    return pl.pallas_call(
        functools.partial(_post_kernel, per_batch=per_batch),
        grid=(m // tm,),
        in_specs=[
            rows(0), rows(0), rows(4), rows(5), rows(6), rows(7), rows(8),
            pl.BlockSpec((None, SUBLANES, d), lambda i: (i // per_batch, 0, 0)),
            whole((1, d)), whole((SC_CONV, SUBLANES, d)),
            whole((d, d)), whole((d, d)), whole((d, d)),
            whole((d, LANES)), whole((d, LANES)), whole((1, LANES)),
        ],
        out_specs=[
            rows(0), pl.BlockSpec((tm * SUBLANES, LANES), lambda i: (i, 0)),
            pl.BlockSpec((tm, LANES), lambda i: (i, 0)),
            pl.BlockSpec((SUBLANES, LANES), lambda i: (0, 0)),
        ],
        out_shape=[
            jax.ShapeDtypeStruct((m, d), F32),
            jax.ShapeDtypeStruct((m * SUBLANES, LANES), F32),
            jax.ShapeDtypeStruct((m, LANES), F32),
            jax.ShapeDtypeStruct((SUBLANES, LANES), F32),
        ],
        scratch_shapes=[
            pltpu.VMEM((SUBLANES, d), F32),
            pltpu.VMEM((SUBLANES, LANES), F32),
        ],
        compiler_params=_params("post"),
        name="post",
    )(x2, og, proj, proj, proj, proj, proj, mod, n2g, conv_w, wpa, wpb, wout, w_route_hi, w_route_lo,
      b_route)


def _plan_kernel(route_ref, cnt_ref, d_ref):
    bm = EXPERT_BLOCK
    sizes = cnt_ref[...]
    padded = jnp.floor((sizes + (bm - 1.0)) * (1.0 / bm)) * bm
    lane_i = lax.broadcasted_iota(jnp.int32, sizes.shape, 1)
    incl = padded
    s = 1
    while s < LANES:
        incl = incl + jnp.where(lane_i >= s, pltpu.roll(incl, s, 1), 0.0)
        s *= 2
    start = (incl - padded)[0:1, :]
    r = route_ref[...]
    lane = lax.broadcasted_iota(jnp.int32, r.shape, 1).astype(F32)
    d1 = jnp.sum(jnp.where(lane == r[:, 0:1], start, 0.0), axis=-1, keepdims=True) + r[:, 4:5]
    d2 = jnp.sum(jnp.where(lane == r[:, 1:2], start, 0.0), axis=-1, keepdims=True) + r[:, 5:6]
    out = jnp.where(lane == 0.0, d1, jnp.where(lane == 1.0, d2, 0.0))
    d_ref[...] = out.T[0:SUBLANES, :].astype(jnp.int32)


def _plan(route, counts, tm):
    m = route.shape[0]
    return pl.pallas_call(
        _plan_kernel,
        grid=(m // tm,),
        in_specs=[
            pl.BlockSpec((tm, LANES), lambda i: (i, 0)),
            pl.BlockSpec((SUBLANES, LANES), lambda i: (0, 0)),
        ],
        out_specs=pl.BlockSpec((SUBLANES, tm), lambda i: (0, i)),
        out_shape=jax.ShapeDtypeStruct((SUBLANES, m), jnp.int32),
        compiler_params=_params("plan"),
        name="plan",
    )(route, counts)


def _row_copy(src_ref, src_row, dst_ref, dst_row, sem):
    return pltpu.make_async_copy(_row_tile(src_ref, src_row), _row_tile(dst_ref, dst_row), sem)


def _rows_copy(src_ref, dst_ref, dst_row, n, sem):
    return pltpu.make_async_copy(src_ref, dst_ref.at[pl.ds(dst_row * SUBLANES, n * SUBLANES), :], sem)


def _dispatch_kernel(d1_ref, d2_ref, pad_ref, h2_ref, xs_ref, zero_ref, sem):
    tm = h2_ref.shape[0] // SUBLANES
    bm = zero_ref.shape[0] // SUBLANES
    i = pl.program_id(0)
    t0 = i * tm
    n_blocks = xs_ref.shape[0] // (bm * SUBLANES)

    def zero_rows(row, n):
        dst = xs_ref.at[pl.ds(pl.multiple_of(row * SUBLANES, SUBLANES), n * SUBLANES), :]
        return pltpu.make_async_copy(zero_ref.at[pl.ds(0, n * SUBLANES), :], dst, sem.at[2])

    def zero_fill(act):
        def expert(e, carry):
            row, cnt = pad_ref[e], pad_ref[N_EXPERTS + e]
            n = bm // 2
            while n >= 1:
                @pl.when((cnt & n) != 0)
                def _():
                    act(zero_rows(row + (cnt & (-2 * n)), n), 0)
                n //= 2
            return carry

        lax.fori_loop(0, N_EXPERTS, expert, 0)
        lax.fori_loop(pad_ref[2 * N_EXPERTS], n_blocks, lambda j, c: (act(zero_rows(j * bm, bm), 1), c)[1], 0)

    @pl.when(i == 0)
    def _():
        zero_ref[...] = jnp.zeros(zero_ref.shape, F32)
        zero_fill(lambda copy, priority: copy.start(priority=priority))

    def start(r, carry):
        _row_copy(h2_ref, r, xs_ref, d1_ref[t0 + r], sem.at[0]).start(priority=0)
        _row_copy(h2_ref, r, xs_ref, d2_ref[t0 + r], sem.at[1]).start(priority=1)
        return carry

    lax.fori_loop(0, tm, start, 0, unroll=DMA_UNROLL)
    _rows_copy(h2_ref, xs_ref, 0, tm, sem.at[0]).wait()
    _rows_copy(h2_ref, xs_ref, 0, tm, sem.at[1]).wait()

    @pl.when(i == pl.num_programs(0) - 1)
    def _():
        zero_fill(lambda copy, priority: copy.wait())


def _dispatch(d1, d2, pad_info, h2, cap, tm):
    m = h2.shape[0] // SUBLANES
    return pl.pallas_call(
        _dispatch_kernel,
        grid_spec=pltpu.PrefetchScalarGridSpec(
            num_scalar_prefetch=3,
            grid=(m // tm,),
            in_specs=[pl.BlockSpec((tm * SUBLANES, LANES), lambda i, d1, d2, tl: (i, 0))],
            out_specs=pl.BlockSpec(memory_space=pl.ANY),
            scratch_shapes=[pltpu.VMEM((EXPERT_BLOCK * SUBLANES, LANES), F32),
                            pltpu.SemaphoreType.DMA((3,))],
        ),
        out_shape=jax.ShapeDtypeStruct((cap * SUBLANES, LANES), F32),
        compiler_params=_params("dispatch"),
        name="dispatch",
    )(d1, d2, pad_info, h2)


def _expert_kernel(be_ref, act_ref, nxt_ref, x_ref, w1_hbm, w3_hbm, w2_hbm, y_ref, w1b_ref, w3b_ref, w2b_ref,
                   w1s_ref, w3s_ref, w2s_ref, sem):
    b = pl.program_id(0)
    weights = ((w1_hbm, w1s_ref, w1b_ref), (w3_hbm, w3s_ref, w3b_ref), (w2_hbm, w2s_ref, w2b_ref))

    def weight_copy(k, e):
        w_hbm, ws_ref, _ = weights[k]
        return pltpu.make_async_copy(w_hbm.at[e], ws_ref, sem.at[k])

    @pl.when(act_ref[b] > 0)
    def _():
        @pl.when(b == 0)
        def _():
            for k in range(len(weights)):
                weight_copy(k, be_ref[b]).start()

        @pl.when((b == 0) | (be_ref[b] != be_ref[jnp.maximum(b - 1, 0)]))
        def _():
            for k, (_, ws_ref, wb_ref) in enumerate(weights):
                weight_copy(k, be_ref[b]).wait()
                wb_ref[...] = ws_ref[...].astype(BF16)

                @pl.when(nxt_ref[b] >= 0)
                def _():
                    weight_copy(k, nxt_ref[b]).start()

        xb = _read_rows(x_ref, x_ref.shape[0] // SUBLANES).astype(BF16)
        hid = _silu(_dot(xb, w1b_ref[...])) * _dot(xb, w3b_ref[...])
        _write_rows(y_ref, _dot(hid.astype(BF16), w2b_ref[...]))

    @pl.when(act_ref[b] == 0)
    def _():
        y_ref[...] = jnp.zeros(y_ref.shape, F32)


def _experts(block_expert, block_active, block_next, xs, w1, w3, w2):
    cap = xs.shape[0] // SUBLANES
    d, de = w1.shape[1], w1.shape[2]
    bm = EXPERT_BLOCK

    def x_block(b, be, act, nxt):
        return (jnp.minimum(b, jnp.maximum(act[cap // bm], 1) - 1), 0)

    return pl.pallas_call(
        _expert_kernel,
        grid_spec=pltpu.PrefetchScalarGridSpec(
            num_scalar_prefetch=3,
            grid=(cap // bm,),
            in_specs=[
                pl.BlockSpec((bm * SUBLANES, LANES), x_block),
                pl.BlockSpec(memory_space=pl.ANY),
                pl.BlockSpec(memory_space=pl.ANY),
                pl.BlockSpec(memory_space=pl.ANY),
            ],
            out_specs=pl.BlockSpec((bm * SUBLANES, LANES), lambda b, be, act, nxt: (b, 0)),
            scratch_shapes=[pltpu.VMEM((d, de), BF16), pltpu.VMEM((d, de), BF16), pltpu.VMEM((de, d), BF16),
                            pltpu.VMEM((d, de), F32), pltpu.VMEM((d, de), F32), pltpu.VMEM((de, d), F32),
                            pltpu.SemaphoreType.DMA((3,))],
        ),
        out_shape=jax.ShapeDtypeStruct((cap * SUBLANES, LANES), F32),
        compiler_params=_params("experts"),
        name="experts",
    )(block_expert, block_active, block_next, xs, w1, w3, w2)


def _final_kernel(d1_ref, d2_ref, x1_ref, route_ref, gt_ref, nfg_ref, ys_ref, o_ref, buf_ref, sem):
    tm = x1_ref.shape[0]
    i = pl.program_id(0)

    def gather(tile, s):
        t0 = tile * tm

        def start(r, carry):
            _row_copy(ys_ref, d1_ref[t0 + r], buf_ref.at[s, 0], r, sem.at[s, 0]).start(priority=0)
            _row_copy(ys_ref, d2_ref[t0 + r], buf_ref.at[s, 1], r, sem.at[s, 1]).start(priority=1)
            return carry

        lax.fori_loop(0, tm, start, 0, unroll=DMA_UNROLL)

    def wait_rows(s):
        for k in range(2):
            pltpu.make_async_copy(ys_ref.at[pl.ds(0, tm * SUBLANES), :], buf_ref.at[s, k],
                                  sem.at[s, k]).wait()

    def combine(cur, nxt):
        wait_rows(cur)
        last = pl.num_programs(0) - 1
        t0 = jnp.minimum(i + 1, last) * tm
        for r in range(tm):
            _row_copy(ys_ref, d1_ref[t0 + r], buf_ref.at[nxt, 0], r, sem.at[nxt, 0]).start(priority=0)
            _row_copy(ys_ref, d2_ref[t0 + r], buf_ref.at[nxt, 1], r, sem.at[nxt, 1]).start(priority=1)
        route = route_ref[...]
        moe = (_read_rows(buf_ref.at[cur, 0], tm) * route[:, 2:3]
               + _read_rows(buf_ref.at[cur, 1], tm) * route[:, 3:4])
        x2 = x1_ref[...] + gt_ref[...] * moe
        y = x2 * lax.rsqrt(jnp.mean(x2 * x2, axis=-1, keepdims=True) + EPS)
        o_ref[...] = y * nfg_ref[...]

        @pl.when(i == last)
        def _():
            wait_rows(nxt)

    @pl.when(i == 0)
    def _():
        gather(0, 0)

    @pl.when(i % 2 == 0)
    def _():
        combine(0, 1)

    @pl.when(i % 2 == 1)
    def _():
        combine(1, 0)


def _final(d1, d2, x1, route, gt2, nfg, ys, seq, tm):
    m, d = x1.shape
    per_batch = seq // tm
    return pl.pallas_call(
        _final_kernel,
        grid_spec=pltpu.PrefetchScalarGridSpec(
            num_scalar_prefetch=2,
            grid=(m // tm,),
            in_specs=[
                pl.BlockSpec((tm, d), lambda i, d1, d2: (i, 0)),
                pl.BlockSpec((tm, LANES), lambda i, d1, d2: (i, 0)),
                pl.BlockSpec((None, 1, d), lambda i, d1, d2: (i // per_batch, 0, 0)),
                pl.BlockSpec((1, d), lambda i, d1, d2: (0, 0)),
                pl.BlockSpec(memory_space=pl.ANY),
            ],
            out_specs=pl.BlockSpec((tm, d), lambda i, d1, d2: (i, 0)),
            scratch_shapes=[pltpu.VMEM((2, 2, tm * SUBLANES, LANES), F32),
                            pltpu.SemaphoreType.DMA((2, 2))],
        ),
        out_shape=jax.ShapeDtypeStruct((m, d), F32),
        compiler_params=_params("final"),
        name="final",
    )(d1, d2, x1, route, gt2, nfg, ys)


def _tile(n, pref):
    t = min(n, pref)
    assert n % t == 0
    return t


def kernel(x, c, w_ada, b_ada, norm1_g, w_in, conv_qkv_w, a_log, dt_bias, onorm_g, w_proj_a,
           conv_sc_w, w_proj_b, w_out, norm2_g, w_group, b_group, w_expert, b_expert, w1, w3, w2,
           normf_g):
    batch, seq, d = x.shape
    depth = w_ada.shape[0]
    m = batch * seq
    kd = N_HEADS * HEAD_D
    assert d == kd and seq % CHUNK == 0 and batch <= SUBLANES
    assert depth == 1, "the last stage applies the final rmsnorm: one layer only"
    x2 = x.reshape(m, d)

    for l in range(depth):
        mod = _ada(c, w_ada[l], b_ada[l][None, :])[:batch]
        sh1, sc1, gt1, sh2, sc2, gt2 = [mod[:, None, j * d:(j + 1) * d] for j in range(6)]

        w = w_in[l]
        o_ba = 3 * kd + kd
        proj, ba = _inproj(x2, norm1_g[l][None, :], sc1, sh1, jnp.swapaxes(w, 0, 1), o_ba // d, 2 * N_HEADS,
                           seq, _tile(seq, INPROJ_ROWS))

        head_params = jnp.zeros((SUBLANES, LANES), F32)
        head_params = head_params.at[0, N_HEADS:2 * N_HEADS].set(a_log[l])
        head_params = head_params.at[1, N_HEADS:2 * N_HEADS].set(dt_bias[l])
        head_params = head_params.at[2, :].set(onorm_g[l])
        conv_taps = jnp.broadcast_to(conv_qkv_w[l][:, None, :], (QKV_CONV, SUBLANES, 3 * kd))
        og = _gdn(proj, ba, conv_taps, head_params, batch, seq, _tile(seq, GDN_ROWS))

        mod_post = jnp.zeros((batch, SUBLANES, d), F32)
        mod_post = mod_post.at[:, 0:1].set(gt1).at[:, 1:2].set(sc2).at[:, 2:3].set(sh2)
        w_route = jnp.zeros((d, LANES), F32)
        w_route = w_route.at[:, :N_GROUPS].set(w_group[l]).at[:, N_GROUPS:N_GROUPS + N_EXPERTS].set(w_expert[l])
        w_route_hi = w_route.astype(BF16)
        w_route_lo = (w_route - w_route_hi.astype(F32)).astype(BF16)
        b_route = jnp.zeros((1, LANES), F32)
        b_route = b_route.at[0, :N_GROUPS].set(b_group[l]).at[0, N_GROUPS:N_GROUPS + N_EXPERTS].set(b_expert[l])
        x1, h2, route, counts = _post(
            x2, og, proj, mod_post, norm2_g[l][None, :],
            jnp.broadcast_to(conv_sc_w[l][:, None, :], (SC_CONV, SUBLANES, d)),
            w_proj_a[l].astype(BF16), w_proj_b[l].astype(BF16), w_out[l].astype(BF16),
            w_route_hi, w_route_lo, b_route, seq, _tile(seq, POST_ROWS))

        bm = EXPERT_BLOCK
        n_blocks = (2 * m) // bm + N_EXPERTS
        sizes = counts[0, :N_EXPERTS].astype(jnp.int32)
        padded = ((sizes + bm - 1) // bm) * bm
        pad_end = jnp.cumsum(padded)
        pad_info = jnp.concatenate([pad_end - padded + sizes, padded - sizes,
                                    pad_end[-1:] // bm]).astype(jnp.int32)
        block_row = jnp.arange(n_blocks, dtype=jnp.int32) * bm
        block_expert = jnp.minimum(jnp.sum(block_row[:, None] >= pad_end[None, :], axis=1),
                                   N_EXPERTS - 1).astype(jnp.int32)
        block_active = jnp.concatenate([(block_row < pad_end[-1]).astype(jnp.int32),
                                        (pad_end[-1:] // bm).astype(jnp.int32)])
        expert_id = jnp.arange(N_EXPERTS, dtype=jnp.int32)
        later = (expert_id[None, :] > expert_id[:, None]) & (padded[None, :] > 0)
        next_expert = jnp.min(jnp.where(later, expert_id[None, :], N_EXPERTS), axis=1)
        block_next = jnp.where(next_expert < N_EXPERTS, next_expert, -1)[block_expert].astype(jnp.int32)
        dest = _plan(route, counts, _tile(seq, INPROJ_ROWS))
        d1, d2 = dest[0], dest[1]

        xs = _dispatch(d1, d2, pad_info, h2, n_blocks * bm, _tile(seq, DISPATCH_ROWS))
        ys = _experts(block_expert, block_active, block_next, xs, w1[l], w3[l], w2[l])
        nfg = normf_g[None, :]
        x2 = _final(d1, d2, x1, route, gt2, nfg, ys, seq, _tile(seq, FINAL_ROWS))
    return x2.reshape(batch, seq, d)
```

```python
import functools

import jax
import jax.numpy as jnp
from jax import lax
from jax.experimental import pallas as pl
from jax.experimental.pallas import tpu as pltpu

F32 = jnp.float32
BF16 = jnp.bfloat16
HIGHEST = lax.Precision.HIGHEST

N_HEADS = 8
HEAD_D = 128
CHUNK = 64
QKV_CONV = 4
SC_CONV = 3
N_GROUPS = 4
EXPERTS_PER_GROUP = 8
N_EXPERTS = N_GROUPS * EXPERTS_PER_GROUP
EPS = 1e-6

LANES = 128
SUBLANES = 8
BF16_ROWS = 16
EXPERT_BLOCK = 512
INPROJ_ROWS = 2048
GDN_ROWS = 256
PRE_CHUNKS = 2
POST_ROWS = 512
DISPATCH_ROWS = 2048
FINAL_ROWS = 512
DMA_UNROLL = 8
MIB = 1024 * 1024
VMEM_MIB = {"ada": 24, "inproj": 56, "gdn": 48, "post": 56, "plan": 16, "dispatch": 24, "experts": 40,
            "final": 32}


def _sigmoid(x):
    return 0.5 + 0.5 * jnp.tanh(0.5 * x)


def _silu(x):
    half = 0.5 * x
    return half + half * jnp.tanh(half)


def _softplus(x):
    return jnp.maximum(x, 0.0) + jnp.log(1.0 + jnp.exp(-jnp.abs(x)))


def _dot(a, b):
    return jnp.dot(a, b, preferred_element_type=F32)


def _dot_nt(a, b):
    return lax.dot_general(a, b, (((1,), (1,)), ((), ())), preferred_element_type=F32)


def _dot_hi(a, b):
    return jnp.dot(a, b, preferred_element_type=F32, precision=HIGHEST)


def _dot_split(a, b_hi, b_lo):
    a_hi = a.astype(BF16)
    a_lo = (a - a_hi.astype(F32)).astype(BF16)
    return _dot(a_hi, b_hi) + (_dot(a_lo, b_hi) + _dot(a_hi, b_lo))


def _read_rows(ref, rows):
    return jnp.concatenate([ref[pl.ds(j, rows, stride=SUBLANES), :] for j in range(SUBLANES)], axis=1)


def _write_rows(ref, value):
    rows = value.shape[0]
    for j in range(SUBLANES):
        ref[pl.ds(j, rows, stride=SUBLANES), :] = value[:, j * LANES:(j + 1) * LANES]


def _row_tile(ref, row):
    if not isinstance(row, int):
        row = pl.multiple_of(row * SUBLANES, SUBLANES)
    else:
        row = row * SUBLANES
    return ref.at[pl.ds(row, SUBLANES), :]


def _params(call, n_grid_axes=1):
    return pltpu.CompilerParams(dimension_semantics=("arbitrary",) * n_grid_axes,
                                vmem_limit_bytes=VMEM_MIB[call] * MIB)


def _ada_kernel(cb_ref, w_ref, b_ref, o_ref, s_ref):
    batch, d, _ = cb_ref.shape
    groups = w_ref.shape[1] // LANES

    @pl.when(pl.program_id(0) == 0)
    def _():
        s_ref[...] = _silu(cb_ref[...])

    def body(kc, accs):
        k0 = pl.multiple_of(kc * SUBLANES, SUBLANES)
        w = w_ref[pl.ds(k0, SUBLANES), :]
        out = []
        for b in range(batch):
            s = s_ref[b, pl.ds(k0, SUBLANES), :]
            out += [accs[b * groups + g] + w[:, g * LANES:(g + 1) * LANES] * s for g in range(groups)]
        return tuple(out)

    zero = jnp.zeros((SUBLANES, LANES), F32)
    accs = lax.fori_loop(0, d // SUBLANES, body, (zero,) * (batch * groups), unroll=4)
    o_ref[...] = jnp.zeros(o_ref.shape, F32)
    for b in range(batch):
        for g in range(groups):
            cols = slice(g * LANES, (g + 1) * LANES)
            o_ref[b:b + 1, cols] = jnp.sum(accs[b * groups + g], axis=0, keepdims=True) + b_ref[:, cols]


def _ada(c, w_ada, b_ada):
    batch, d = c.shape
    n = w_ada.shape[1]
    cb = jnp.broadcast_to(c[:, :, None], (batch, d, LANES))
    return pl.pallas_call(
        _ada_kernel,
        grid=(n // d,),
        in_specs=[
            pl.BlockSpec((batch, d, LANES), lambda j: (0, 0, 0)),
            pl.BlockSpec((d, d), lambda j: (0, j)),
            pl.BlockSpec((1, d), lambda j: (0, j)),
        ],
        out_specs=pl.BlockSpec((SUBLANES, d), lambda j: (0, j)),
        out_shape=jax.ShapeDtypeStruct((SUBLANES, n), F32),
        scratch_shapes=[pltpu.VMEM((batch, d, LANES), F32)],
        compiler_params=_params("ada"),
        name="ada",
    )(cb, w_ada, b_ada)


def _inproj_kernel(x_hbm, g_ref, sc_ref, sh_ref, wt_hbm, o_ref, ba_ref, x_ref, h_ref, w_ref, wba_ref, stage_ref,
                   stage_ba_ref, sem_x, sem, sem_ba, *, n_lead, n_ba):
    i, j = pl.program_id(0), pl.program_id(1)
    n_tiles = pl.num_programs(0)
    n_blocks, tn = w_ref.shape[0], o_ref.shape[1]
    tm, half = x_ref.shape[0], stage_ref.shape[1]

    def x_copy(tile):
        return pltpu.make_async_copy(x_hbm.at[pl.ds(pl.multiple_of(tile * tm, tm), tm)], x_ref, sem_x)

    def block_copy(jb, part):
        row = jb * tn + jnp.where(jb >= n_lead, n_ba, 0) + part * half
        return pltpu.make_async_copy(wt_hbm.at[pl.ds(pl.multiple_of(row, SUBLANES), half)],
                                     stage_ref.at[part], sem.at[part])

    def ba_copy():
        return pltpu.make_async_copy(wt_hbm.at[pl.ds(n_lead * tn, n_ba)], stage_ba_ref, sem_ba)

    @pl.when(j == 0)
    def _():
        @pl.when(i == 0)
        def _():
            x_copy(i).start()
            ba_copy().start()
            for part in range(2):
                block_copy(j, part).start()

        x_copy(i).wait()
        x = x_ref[...]
        y = x * lax.rsqrt(jnp.mean(x * x, axis=-1, keepdims=True) + EPS)
        h = (y * g_ref[...]) * (1.0 + sc_ref[...]) + sh_ref[...]
        hb = h.astype(BF16)
        h_ref[...] = hb

        @pl.when(i == 0)
        def _():
            ba_copy().wait()
            wba_ref[...] = jnp.zeros(wba_ref.shape, BF16)
            wba_ref[0:n_ba, :] = stage_ba_ref[...].astype(BF16)

        ba_ref[...] = _dot_nt(hb, wba_ref[...])

    @pl.when(jnp.logical_and(j == 1, i + 1 < n_tiles))
    def _():
        x_copy(i + 1).start()

    @pl.when(i == 0)
    def _():
        for part in range(2):
            block_copy(j, part).wait()
            w_ref[j, pl.ds(part * half, half), :] = stage_ref[part].astype(BF16)

            @pl.when(j + 1 < n_blocks)
            def _():
                block_copy(j + 1, part).start()

    rows = o_ref.shape[0] // 2
    for r in range(2):
        o_ref[pl.ds(r * rows, rows), :] = _dot_nt(h_ref[pl.ds(r * rows, rows), :], w_ref[j]).astype(BF16)


def _inproj(x2, norm_g, sc, sh, wt, n_lead, n_ba, seq, tm):
    m, d = x2.shape
    tn = d
    n = wt.shape[0] - n_ba
    assert n % tn == 0 and n_ba % (2 * SUBLANES) == 0 and n_ba <= LANES
    per_batch = seq // tm
    return pl.pallas_call(
        functools.partial(_inproj_kernel, n_lead=n_lead, n_ba=n_ba),
        grid=(m // tm, n // tn),
        in_specs=[
            pl.BlockSpec(memory_space=pl.ANY),
            pl.BlockSpec((1, d), lambda i, j: (0, 0)),
            pl.BlockSpec((None, 1, d), lambda i, j: (i // per_batch, 0, 0)),
            pl.BlockSpec((None, 1, d), lambda i, j: (i // per_batch, 0, 0)),
            pl.BlockSpec(memory_space=pl.ANY),
        ],
        out_specs=[
            pl.BlockSpec((tm, tn), lambda i, j: (i, j)),
            pl.BlockSpec((tm, LANES), lambda i, j: (i, 0)),
        ],
        out_shape=[
            jax.ShapeDtypeStruct((m, n), BF16),
            jax.ShapeDtypeStruct((m, LANES), F32),
        ],
        scratch_shapes=[
            pltpu.VMEM((tm, d), F32),
            pltpu.VMEM((tm, d), BF16),
            pltpu.VMEM((n // tn, tn, d), BF16),
            pltpu.VMEM((LANES, d), BF16),
            pltpu.VMEM((2, tn // 2, d), F32),
            pltpu.VMEM((n_ba, d), F32),
            pltpu.SemaphoreType.DMA(()),
            pltpu.SemaphoreType.DMA((2,)),
            pltpu.SemaphoreType.DMA(()),
        ],
        compiler_params=_params("inproj", 2),
        name="inproj",
    )(x2, norm_g, sc, sh, wt)


def _bmm(a, b):
    return jnp.einsum("hmk,hkn->hmn", a.astype(BF16), b.astype(BF16), preferred_element_type=F32)


def _bmm_nt(a, b):
    return jnp.einsum("hmk,hnk->hmn", a.astype(BF16), b.astype(BF16), preferred_element_type=F32)


def _unit_lower_inverse(a):
    row = lax.broadcasted_iota(jnp.int32, a.shape[1:], 0)
    col = lax.broadcasted_iota(jnp.int32, a.shape[1:], 1)
    apart = row ^ col
    eye = jnp.where(row == col, 1.0, 0.0).astype(F32)
    t = jnp.where(apart < 2, eye - a, 0.0)
    s = 2
    while s < CHUNK:
        coupling = jnp.where((apart >= s) & (apart < 2 * s), a, 0.0)
        t = t - _bmm(t, _bmm(coupling, t))
        s *= 2
    return t


def _lane_sums(x):
    h, rows, width = x.shape
    ones = jnp.ones((width, width), BF16)
    return _dot(x.reshape(h * rows, width).astype(BF16), ones).reshape(h, rows, width)


def _causal_conv_silu(win, cw, k_w):
    assert k_w == 4
    tiled = (win.shape[0] // SUBLANES, SUBLANES, win.shape[1])

    def pair(x, x1, j):
        return (x.reshape(tiled) * cw[j][None] + x1.reshape(tiled) * cw[j - 1][None]).reshape(win.shape)

    win1 = pltpu.roll(win, 1, 0)
    acc = pair(win, win1, 3) + pltpu.roll(pair(win, win1, 1), 2, 0)
    return _silu(acc[SUBLANES:, :])


def _gdn_kernel(q_ref, k_ref, v_ref, z_ref, ba_ref, cw_ref, hp_ref, o_ref,
                s_ref, tail_ref, wq_ref, u_ref, ik_ref, dec_ref):
    nb, tb = q_ref.shape[0], q_ref.shape[1]
    kd = N_HEADS * HEAD_D
    nbh = nb * N_HEADS

    @pl.when(pl.program_id(0) == 0)
    def _():
        s_ref[...] = jnp.zeros(s_ref.shape, F32)
        tail_ref[...] = jnp.zeros(tail_ref.shape, F32)

    row = lax.broadcasted_iota(jnp.int32, (CHUNK, CHUNK), 0)
    col = lax.broadcasted_iota(jnp.int32, (CHUNK, CHUNK), 1)
    causal = row >= col
    strict = row > col
    tril = jnp.where(causal, 1.0, 0.0).astype(F32)
    a_log = hp_ref[0:1, :]
    dt_bias = hp_ref[1:2, :]
    onorm_g = hp_ref[2:3, :]
    zeros_half = jnp.zeros((CHUNK, HEAD_D), F32)

    def precompute(cp, carry):
        qs, ks, vs, betas, gcs, grs, gls = [], [], [], [], [], [], []
        for sub in range(PRE_CHUNKS):
            base = pl.multiple_of((cp * PRE_CHUNKS + sub) * CHUNK, CHUNK)
            prev = pl.multiple_of(base - BF16_ROWS, BF16_ROWS)
            for b in range(nb):
                ba = ba_ref[b, pl.ds(base, CHUNK), :]
                beta_all = _sigmoid(ba)
                g_all = -jnp.exp(a_log) * _softplus(ba + dt_bias)
                gcum = _dot_hi(tril, g_all)
                gcum_t = jnp.concatenate([gcum, gcum], axis=0).T
                for h in range(N_HEADS):
                    lo, hi = h * HEAD_D, (h + 1) * HEAD_D

                    def conv(ref, off):
                        cur = ref[b, pl.ds(base, CHUNK), lo:hi].astype(F32)
                        if sub == 0:
                            before = tail_ref[b, :, off + lo:off + hi]
                        else:
                            before = ref[b, pl.ds(prev, BF16_ROWS), lo:hi].astype(F32)[SUBLANES:]
                        win = jnp.concatenate([before, cur], axis=0)
                        return _causal_conv_silu(win, cw_ref[:, :, off + lo:off + hi], QKV_CONV)

                    qs.append(conv(q_ref, 0))
                    ks.append(conv(k_ref, kd))
                    vs.append(conv(v_ref, 2 * kd))
                    betas.append(beta_all[:, h:h + 1])
                    gcs.append(gcum[:, N_HEADS + h:N_HEADS + h + 1])
                    grs.append(gcum_t[N_HEADS + h:N_HEADS + h + 1, 0:CHUNK])
                    gls.append(gcum[CHUNK - 1:CHUNK, N_HEADS + h:N_HEADS + h + 1])
        q, k, v = jnp.stack(qs), jnp.stack(ks), jnp.stack(vs)
        beta, gc, gr, gl = jnp.stack(betas), jnp.stack(gcs), jnp.stack(grs), jnp.stack(gls)
        qn = q * (lax.rsqrt(_lane_sums(q * q) + EPS) * (HEAD_D ** -0.5))
        kn = k * lax.rsqrt(_lane_sums(k * k) + EPS)
        decay = jnp.where(causal, jnp.exp(jnp.where(causal, gc - gr, 0.0)), 0.0)
        kb = kn * beta
        e_gc = jnp.exp(gc)
        kq = _bmm_nt(jnp.concatenate([kb, qn], axis=1), kn)
        a = jnp.where(strict, kq[:, :CHUNK] * decay, 0.0)
        intra = kq[:, CHUNK:] * decay
        uw = _bmm(_unit_lower_inverse(a), jnp.concatenate([v * beta, kb * e_gc], axis=2))
        wq = jnp.concatenate([uw[:, :, HEAD_D:], qn * e_gc], axis=1).astype(BF16)
        k_dec = kn * jnp.exp(gl - gc)
        k_dec_t = jnp.stack([jnp.concatenate([k_dec[i], zeros_half], axis=0).T[:, :CHUNK]
                             for i in range(PRE_CHUNKS * nbh)])
        ik = jnp.concatenate([intra, k_dec_t], axis=1).astype(BF16)
        dec = jnp.broadcast_to(jnp.exp(gl), (PRE_CHUNKS * nbh, 1, HEAD_D))
        for sub in range(PRE_CHUNKS):
            c = cp * PRE_CHUNKS + sub
            rows = slice(sub * nbh, (sub + 1) * nbh)
            u_ref[c] = uw[rows, :, :HEAD_D]
            wq_ref[c] = wq[rows]
            ik_ref[c] = ik[rows]
            dec_ref[c] = dec[rows]

        last = pl.multiple_of((cp + 1) * PRE_CHUNKS * CHUNK - BF16_ROWS, BF16_ROWS)
        for b in range(nb):
            for j, ref in enumerate((q_ref, k_ref, v_ref)):
                rows = ref[b, pl.ds(last, BF16_ROWS), :].astype(F32)
                tail_ref[b, :, j * kd:(j + 1) * kd] = rows[BF16_ROWS - SUBLANES:, :]
        return carry

    def recur(c, carry):
        base = pl.multiple_of(c * CHUNK, CHUNK)
        state = s_ref[...]
        ws = _bmm(wq_ref[c], state)
        v_new = u_ref[c] - ws[:, :CHUNK]
        r = _bmm(ik_ref[c], v_new)
        o = ws[:, CHUNK:] + r[:, :CHUNK]
        s_ref[...] = state * dec_ref[c] + r[:, CHUNK:]
        on = o * lax.rsqrt(jnp.mean(o * o, axis=-1, keepdims=True) + EPS) * onorm_g
        for b in range(nb):
            for h in range(N_HEADS):
                lo, hi = h * HEAD_D, (h + 1) * HEAD_D
                z = z_ref[b, pl.ds(base, CHUNK), lo:hi].astype(F32)
                o_ref[b, pl.ds(base, CHUNK), lo:hi] = (on[b * N_HEADS + h] * _silu(z)).astype(BF16)
        return carry

    lax.fori_loop(0, tb // (PRE_CHUNKS * CHUNK), precompute, 0)
    lax.fori_loop(0, tb // CHUNK, recur, 0)


def _gdn(proj, ba, conv_w, head_params, batch, seq, tb):
    kd = N_HEADS * HEAD_D
    nc = tb // CHUNK
    nbh = batch * N_HEADS
    proj3 = proj.reshape(batch, seq, proj.shape[1])
    ba3 = ba.reshape(batch, seq, LANES)

    def col(j):
        return pl.BlockSpec((batch, tb, kd), lambda t: (0, t, j))

    out = pl.pallas_call(
        _gdn_kernel,
        grid=(seq // tb,),
        in_specs=[
            col(0), col(1), col(2), col(3),
            pl.BlockSpec((batch, tb, LANES), lambda t: (0, t, 0)),
            pl.BlockSpec((QKV_CONV, SUBLANES, 3 * kd), lambda t: (0, 0, 0)),
            pl.BlockSpec((SUBLANES, LANES), lambda t: (0, 0)),
        ],
        out_specs=pl.BlockSpec((batch, tb, kd), lambda t: (0, t, 0)),
        out_shape=jax.ShapeDtypeStruct((batch, seq, kd), BF16),
        scratch_shapes=[
            pltpu.VMEM((nbh, HEAD_D, HEAD_D), F32),
            pltpu.VMEM((batch, SUBLANES, 3 * kd), F32),
            pltpu.VMEM((nc, nbh, 2 * CHUNK, HEAD_D), BF16),
            pltpu.VMEM((nc, nbh, CHUNK, HEAD_D), F32),
            pltpu.VMEM((nc, nbh, CHUNK + HEAD_D, CHUNK), BF16),
            pltpu.VMEM((nc, nbh, 1, HEAD_D), F32),
        ],
        compiler_params=_params("gdn"),
        name="gdn",
    )(proj3, proj3, proj3, proj3, ba3, conv_w, head_params)
    return out.reshape(batch * seq, kd)


def _post_kernel(x_ref, og_ref, sb_ref, sc_ref, sx_ref, ga_ref, gb_ref, mod_ref, n2g_ref, cw_ref,
                 wpa_ref, wpb_ref, wout_ref, wrh_ref, wrl_ref, br_ref,
                 x1_ref, h2_ref, route_ref, cnt_ref, win_ref, run_ref, *, per_batch):
    tm = x_ref.shape[0]
    i = pl.program_id(0)

    @pl.when(i == 0)
    def _():
        run_ref[...] = jnp.zeros(run_ref.shape, F32)

    @pl.when(i % per_batch == 0)
    def _():
        win_ref[...] = jnp.zeros(win_ref.shape, F32)

    assert SC_CONV == 3
    prod = sc_ref[...].astype(F32) * sx_ref[...].astype(F32)
    win = jnp.concatenate([win_ref[...], prod], axis=0)
    win_ref[...] = prod[tm - SUBLANES:, :]
    tiled = (win.shape[0] // SUBLANES, SUBLANES, win.shape[1])
    win1 = pltpu.roll(win, 1, 0)
    conv = (win.reshape(tiled) * cw_ref[2][None] + win1.reshape(tiled) * cw_ref[1][None]
            + pltpu.roll(win1, 1, 0).reshape(tiled) * cw_ref[0][None]).reshape(win.shape)[SUBLANES:, :]
    y_b = _dot((sb_ref[...].astype(F32) * conv).astype(BF16), wpb_ref[...])
    y_a = _dot(og_ref[...], wpa_ref[...])
    merged = _sigmoid(ga_ref[...].astype(F32)) * y_a + _sigmoid(gb_ref[...].astype(F32)) * y_b
    mix = _dot(merged.astype(BF16), wout_ref[...])
    x1 = x_ref[...] + mod_ref[0:1, :] * mix
    x1_ref[...] = x1

    y = x1 * lax.rsqrt(jnp.mean(x1 * x1, axis=-1, keepdims=True) + EPS)
    h2 = (y * n2g_ref[...]) * (1.0 + mod_ref[1:2, :]) + mod_ref[2:3, :]
    _write_rows(h2_ref, h2)

    lg = _dot_split(h2, wrh_ref[...], wrl_ref[...]) + br_ref[...]
    lane = lax.broadcasted_iota(jnp.int32, lg.shape, 1).astype(F32)
    neg = jnp.float32(-jnp.inf)
    big = jnp.float32(2 * LANES)

    def first_max(mask):
        vmax = jnp.max(jnp.where(mask, lg, neg), axis=-1, keepdims=True)
        idx = jnp.min(jnp.where(mask & (lg == vmax), lane, big), axis=-1, keepdims=True)
        return vmax, idx

    gmask = lane < N_GROUPS
    g_max, g_sel = first_max(gmask)
    p_group = 1.0 / jnp.sum(jnp.where(gmask, jnp.exp(lg - g_max), 0.0), axis=-1, keepdims=True)
    e_lo = N_GROUPS + EXPERTS_PER_GROUP * g_sel
    emask = (lane >= e_lo) & (lane < e_lo + EXPERTS_PER_GROUP)
    v1, i1 = first_max(emask)
    v2, i2 = first_max(emask & (lane != i1))
    ex = jnp.exp(v2 - v1)
    w1 = p_group * (1.0 / (1.0 + ex))
    w2 = p_group * (ex / (1.0 + ex))
    e1 = i1 - N_GROUPS
    e2 = i2 - N_GROUPS

    onehot = jnp.where((lane == e1) | (lane == e2), 1.0, 0.0).astype(F32)
    row = lax.broadcasted_iota(jnp.int32, (tm, tm), 0)
    col = lax.broadcasted_iota(jnp.int32, (tm, tm), 1)
    before = jnp.where(row > col, 1.0, 0.0).astype(BF16)
    seen = _dot(before, onehot.astype(BF16)) + run_ref[0:1, :]
    r1 = jnp.sum(jnp.where(lane == e1, seen, 0.0), axis=-1, keepdims=True)
    r2 = jnp.sum(jnp.where(lane == e2, seen, 0.0), axis=-1, keepdims=True)
    run_ref[0:1, :] = run_ref[0:1, :] + jnp.sum(onehot, axis=0, keepdims=True)
    cnt_ref[...] = jnp.broadcast_to(run_ref[0:1, :], cnt_ref.shape)

    out = jnp.where(lane == 0, e1, 0.0)
    out = jnp.where(lane == 1, e2, out)
    out = jnp.where(lane == 2, w1, out)
    out = jnp.where(lane == 3, w2, out)
    out = jnp.where(lane == 4, r1, out)
    out = jnp.where(lane == 5, r2, out)
    route_ref[...] = out


def _post(x2, og, proj, mod, n2g, conv_w, wpa, wpb, wout, w_route_hi, w_route_lo, b_route, seq, tm):
    m, d = x2.shape
    assert d == SUBLANES * LANES
    per_batch = seq // tm

    def rows(j):
        return pl.BlockSpec((tm, d), lambda i: (i, j))

    def whole(shape):
        return pl.BlockSpec(shape, lambda i: tuple(0 for _ in shape))

    return pl.pallas_call(
        functools.partial(_post_kernel, per_batch=per_batch),
        grid=(m // tm,),
        in_specs=[
            rows(0), rows(0), rows(4), rows(5), rows(6), rows(7), rows(8),
            pl.BlockSpec((None, SUBLANES, d), lambda i: (i // per_batch, 0, 0)),
            whole((1, d)), whole((SC_CONV, SUBLANES, d)),
            whole((d, d)), whole((d, d)), whole((d, d)),
            whole((d, LANES)), whole((d, LANES)), whole((1, LANES)),
        ],
        out_specs=[
            rows(0), pl.BlockSpec((tm * SUBLANES, LANES), lambda i: (i, 0)),
            pl.BlockSpec((tm, LANES), lambda i: (i, 0)),
            pl.BlockSpec((SUBLANES, LANES), lambda i: (0, 0)),
        ],
        out_shape=[
            jax.ShapeDtypeStruct((m, d), F32),
            jax.ShapeDtypeStruct((m * SUBLANES, LANES), F32),
            jax.ShapeDtypeStruct((m, LANES), F32),
            jax.ShapeDtypeStruct((SUBLANES, LANES), F32),
        ],
        scratch_shapes=[
            pltpu.VMEM((SUBLANES, d), F32),
            pltpu.VMEM((SUBLANES, LANES), F32),
        ],
        compiler_params=_params("post"),
        name="post",
    )(x2, og, proj, proj, proj, proj, proj, mod, n2g, conv_w, wpa, wpb, wout, w_route_hi, w_route_lo,
      b_route)


def _plan_kernel(route_ref, cnt_ref, d_ref):
    bm = EXPERT_BLOCK
    sizes = cnt_ref[...]
    padded = jnp.floor((sizes + (bm - 1.0)) * (1.0 / bm)) * bm
    lane_i = lax.broadcasted_iota(jnp.int32, sizes.shape, 1)
    incl = padded
    s = 1
    while s < LANES:
        incl = incl + jnp.where(lane_i >= s, pltpu.roll(incl, s, 1), 0.0)
        s *= 2
    start = (incl - padded)[0:1, :]
    r = route_ref[...]
    lane = lax.broadcasted_iota(jnp.int32, r.shape, 1).astype(F32)
    d1 = jnp.sum(jnp.where(lane == r[:, 0:1], start, 0.0), axis=-1, keepdims=True) + r[:, 4:5]
    d2 = jnp.sum(jnp.where(lane == r[:, 1:2], start, 0.0), axis=-1, keepdims=True) + r[:, 5:6]
    out = jnp.where(lane == 0.0, d1, jnp.where(lane == 1.0, d2, 0.0))
    d_ref[...] = out.T[0:SUBLANES, :].astype(jnp.int32)


def _plan(route, counts, tm):
    m = route.shape[0]
    return pl.pallas_call(
        _plan_kernel,
        grid=(m // tm,),
        in_specs=[
            pl.BlockSpec((tm, LANES), lambda i: (i, 0)),
            pl.BlockSpec((SUBLANES, LANES), lambda i: (0, 0)),
        ],
        out_specs=pl.BlockSpec((SUBLANES, tm), lambda i: (0, i)),
        out_shape=jax.ShapeDtypeStruct((SUBLANES, m), jnp.int32),
        compiler_params=_params("plan"),
        name="plan",
    )(route, counts)


def _row_copy(src_ref, src_row, dst_ref, dst_row, sem):
    return pltpu.make_async_copy(_row_tile(src_ref, src_row), _row_tile(dst_ref, dst_row), sem)


def _rows_copy(src_ref, dst_ref, dst_row, n, sem):
    return pltpu.make_async_copy(src_ref, dst_ref.at[pl.ds(dst_row * SUBLANES, n * SUBLANES), :], sem)


def _dispatch_kernel(d1_ref, d2_ref, pad_ref, h2_ref, xs_ref, zero_ref, sem):
    tm = h2_ref.shape[0] // SUBLANES
    bm = zero_ref.shape[0] // SUBLANES
    i = pl.program_id(0)
    t0 = i * tm
    n_blocks = xs_ref.shape[0] // (bm * SUBLANES)

    def zero_rows(row, n):
        dst = xs_ref.at[pl.ds(pl.multiple_of(row * SUBLANES, SUBLANES), n * SUBLANES), :]
        return pltpu.make_async_copy(zero_ref.at[pl.ds(0, n * SUBLANES), :], dst, sem.at[2])

    def zero_fill(act):
        def expert(e, carry):
            row, cnt = pad_ref[e], pad_ref[N_EXPERTS + e]
            n = bm // 2
            while n >= 1:
                @pl.when((cnt & n) != 0)
                def _():
                    act(zero_rows(row + (cnt & (-2 * n)), n))
                n //= 2
            return carry

        lax.fori_loop(0, N_EXPERTS, expert, 0)
        lax.fori_loop(pad_ref[2 * N_EXPERTS], n_blocks, lambda j, c: (act(zero_rows(j * bm, bm)), c)[1], 0)

    @pl.when(i == 0)
    def _():
        zero_ref[...] = jnp.zeros(zero_ref.shape, F32)
        zero_fill(lambda copy: copy.start())

    def start(r, carry):
        _row_copy(h2_ref, r, xs_ref, d1_ref[t0 + r], sem.at[0]).start(priority=0)
        _row_copy(h2_ref, r, xs_ref, d2_ref[t0 + r], sem.at[1]).start(priority=1)
        return carry

    lax.fori_loop(0, tm, start, 0, unroll=DMA_UNROLL)
    _rows_copy(h2_ref, xs_ref, 0, tm, sem.at[0]).wait()
    _rows_copy(h2_ref, xs_ref, 0, tm, sem.at[1]).wait()

    @pl.when(i == pl.num_programs(0) - 1)
    def _():
        zero_fill(lambda copy: copy.wait())


def _dispatch(d1, d2, pad_info, h2, cap, tm):
    m = h2.shape[0] // SUBLANES
    return pl.pallas_call(
        _dispatch_kernel,
        grid_spec=pltpu.PrefetchScalarGridSpec(
            num_scalar_prefetch=3,
            grid=(m // tm,),
            in_specs=[pl.BlockSpec((tm * SUBLANES, LANES), lambda i, d1, d2, tl: (i, 0))],
            out_specs=pl.BlockSpec(memory_space=pl.ANY),
            scratch_shapes=[pltpu.VMEM((EXPERT_BLOCK * SUBLANES, LANES), F32),
                            pltpu.SemaphoreType.DMA((3,))],
        ),
        out_shape=jax.ShapeDtypeStruct((cap * SUBLANES, LANES), F32),
        compiler_params=_params("dispatch"),
        name="dispatch",
    )(d1, d2, pad_info, h2)


def _expert_kernel(be_ref, act_ref, nxt_ref, x_ref, w1_hbm, w3_hbm, w2_hbm, y_ref, w1b_ref, w3b_ref, w2b_ref,
                   w1s_ref, w3s_ref, w2s_ref, sem):
    b = pl.program_id(0)
    weights = ((w1_hbm, w1s_ref, w1b_ref), (w3_hbm, w3s_ref, w3b_ref), (w2_hbm, w2s_ref, w2b_ref))

    def weight_copy(k, e):
        w_hbm, ws_ref, _ = weights[k]
        return pltpu.make_async_copy(w_hbm.at[e], ws_ref, sem.at[k])

    @pl.when(act_ref[b] > 0)
    def _():
        @pl.when(b == 0)
        def _():
            for k in range(len(weights)):
                weight_copy(k, be_ref[b]).start()

        @pl.when((b == 0) | (be_ref[b] != be_ref[jnp.maximum(b - 1, 0)]))
        def _():
            for k, (_, ws_ref, wb_ref) in enumerate(weights):
                weight_copy(k, be_ref[b]).wait()
                wb_ref[...] = ws_ref[...].astype(BF16)

                @pl.when(nxt_ref[b] >= 0)
                def _():
                    weight_copy(k, nxt_ref[b]).start()

        xb = _read_rows(x_ref, x_ref.shape[0] // SUBLANES).astype(BF16)
        hid = _silu(_dot(xb, w1b_ref[...])) * _dot(xb, w3b_ref[...])
        _write_rows(y_ref, _dot(hid.astype(BF16), w2b_ref[...]))

    @pl.when(act_ref[b] == 0)
    def _():
        y_ref[...] = jnp.zeros(y_ref.shape, F32)


def _experts(block_expert, block_active, block_next, xs, w1, w3, w2):
    cap = xs.shape[0] // SUBLANES
    d, de = w1.shape[1], w1.shape[2]
    bm = EXPERT_BLOCK

    def x_block(b, be, act, nxt):
        return (jnp.minimum(b, jnp.maximum(act[cap // bm], 1) - 1), 0)

    return pl.pallas_call(
        _expert_kernel,
        grid_spec=pltpu.PrefetchScalarGridSpec(
            num_scalar_prefetch=3,
            grid=(cap // bm,),
            in_specs=[
                pl.BlockSpec((bm * SUBLANES, LANES), x_block),
                pl.BlockSpec(memory_space=pl.ANY),
                pl.BlockSpec(memory_space=pl.ANY),
                pl.BlockSpec(memory_space=pl.ANY),
            ],
            out_specs=pl.BlockSpec((bm * SUBLANES, LANES), lambda b, be, act, nxt: (b, 0)),
            scratch_shapes=[pltpu.VMEM((d, de), BF16), pltpu.VMEM((d, de), BF16), pltpu.VMEM((de, d), BF16),
                            pltpu.VMEM((d, de), F32), pltpu.VMEM((d, de), F32), pltpu.VMEM((de, d), F32),
                            pltpu.SemaphoreType.DMA((3,))],
        ),
        out_shape=jax.ShapeDtypeStruct((cap * SUBLANES, LANES), F32),
        compiler_params=_params("experts"),
        name="experts",
    )(block_expert, block_active, block_next, xs, w1, w3, w2)


def _final_kernel(d1_ref, d2_ref, x1_ref, route_ref, gt_ref, nfg_ref, ys_ref, o_ref, buf_ref, sem):
    tm = x1_ref.shape[0]
    i = pl.program_id(0)

    def gather(tile, s):
        t0 = tile * tm

        def start(r, carry):
            _row_copy(ys_ref, d1_ref[t0 + r], buf_ref.at[s, 0], r, sem.at[s, 0]).start(priority=0)
            _row_copy(ys_ref, d2_ref[t0 + r], buf_ref.at[s, 1], r, sem.at[s, 1]).start(priority=1)
            return carry

        lax.fori_loop(0, tm, start, 0, unroll=DMA_UNROLL)

    def wait_rows(s):
        for k in range(2):
            pltpu.make_async_copy(ys_ref.at[pl.ds(0, tm * SUBLANES), :], buf_ref.at[s, k],
                                  sem.at[s, k]).wait()

    def combine(cur, nxt):
        wait_rows(cur)
        last = pl.num_programs(0) - 1
        t0 = jnp.minimum(i + 1, last) * tm
        for r in range(tm):
            _row_copy(ys_ref, d1_ref[t0 + r], buf_ref.at[nxt, 0], r, sem.at[nxt, 0]).start(priority=0)
            _row_copy(ys_ref, d2_ref[t0 + r], buf_ref.at[nxt, 1], r, sem.at[nxt, 1]).start(priority=1)
        route = route_ref[...]
        moe = (_read_rows(buf_ref.at[cur, 0], tm) * route[:, 2:3]
               + _read_rows(buf_ref.at[cur, 1], tm) * route[:, 3:4])
        x2 = x1_ref[...] + gt_ref[...] * moe
        y = x2 * lax.rsqrt(jnp.mean(x2 * x2, axis=-1, keepdims=True) + EPS)
        o_ref[...] = y * nfg_ref[...]

        @pl.when(i == last)
        def _():
            wait_rows(nxt)

    @pl.when(i == 0)
    def _():
        gather(0, 0)

    @pl.when(i % 2 == 0)
    def _():
        combine(0, 1)

    @pl.when(i % 2 == 1)
    def _():
        combine(1, 0)


def _final(d1, d2, x1, route, gt2, nfg, ys, seq, tm):
    m, d = x1.shape
    per_batch = seq // tm
    return pl.pallas_call(
        _final_kernel,
        grid_spec=pltpu.PrefetchScalarGridSpec(
            num_scalar_prefetch=2,
            grid=(m // tm,),
            in_specs=[
                pl.BlockSpec((tm, d), lambda i, d1, d2: (i, 0)),
                pl.BlockSpec((tm, LANES), lambda i, d1, d2: (i, 0)),
                pl.BlockSpec((None, 1, d), lambda i, d1, d2: (i // per_batch, 0, 0)),
                pl.BlockSpec((1, d), lambda i, d1, d2: (0, 0)),
                pl.BlockSpec(memory_space=pl.ANY),
            ],
            out_specs=pl.BlockSpec((tm, d), lambda i, d1, d2: (i, 0)),
            scratch_shapes=[pltpu.VMEM((2, 2, tm * SUBLANES, LANES), F32),
                            pltpu.SemaphoreType.DMA((2, 2))],
        ),
        out_shape=jax.ShapeDtypeStruct((m, d), F32),
        compiler_params=_params("final"),
        name="final",
    )(d1, d2, x1, route, gt2, nfg, ys)


def _tile(n, pref):
    t = min(n, pref)
    assert n % t == 0
    return t


def kernel(x, c, w_ada, b_ada, norm1_g, w_in, conv_qkv_w, a_log, dt_bias, onorm_g, w_proj_a,
           conv_sc_w, w_proj_b, w_out, norm2_g, w_group, b_group, w_expert, b_expert, w1, w3, w2,
           normf_g):
    batch, seq, d = x.shape
    depth = w_ada.shape[0]
    m = batch * seq
    kd = N_HEADS * HEAD_D
    assert d == kd and seq % CHUNK == 0 and batch <= SUBLANES
    assert depth == 1, "the last stage applies the final rmsnorm: one layer only"
    x2 = x.reshape(m, d)

    for l in range(depth):
        mod = _ada(c, w_ada[l], b_ada[l][None, :])[:batch]
        sh1, sc1, gt1, sh2, sc2, gt2 = [mod[:, None, j * d:(j + 1) * d] for j in range(6)]

        w = w_in[l]
        o_ba = 3 * kd + kd
        proj, ba = _inproj(x2, norm1_g[l][None, :], sc1, sh1, jnp.swapaxes(w, 0, 1), o_ba // d, 2 * N_HEADS,
                           seq, _tile(seq, INPROJ_ROWS))

        head_params = jnp.zeros((SUBLANES, LANES), F32)
        head_params = head_params.at[0, N_HEADS:2 * N_HEADS].set(a_log[l])
        head_params = head_params.at[1, N_HEADS:2 * N_HEADS].set(dt_bias[l])
        head_params = head_params.at[2, :].set(onorm_g[l])
        conv_taps = jnp.broadcast_to(conv_qkv_w[l][:, None, :], (QKV_CONV, SUBLANES, 3 * kd))
        og = _gdn(proj, ba, conv_taps, head_params, batch, seq, _tile(seq, GDN_ROWS))

        mod_post = jnp.zeros((batch, SUBLANES, d), F32)
        mod_post = mod_post.at[:, 0:1].set(gt1).at[:, 1:2].set(sc2).at[:, 2:3].set(sh2)
        w_route = jnp.zeros((d, LANES), F32)
        w_route = w_route.at[:, :N_GROUPS].set(w_group[l]).at[:, N_GROUPS:N_GROUPS + N_EXPERTS].set(w_expert[l])
        w_route_hi = w_route.astype(BF16)
        w_route_lo = (w_route - w_route_hi.astype(F32)).astype(BF16)
        b_route = jnp.zeros((1, LANES), F32)
        b_route = b_route.at[0, :N_GROUPS].set(b_group[l]).at[0, N_GROUPS:N_GROUPS + N_EXPERTS].set(b_expert[l])
        x1, h2, route, counts = _post(
            x2, og, proj, mod_post, norm2_g[l][None, :],
            jnp.broadcast_to(conv_sc_w[l][:, None, :], (SC_CONV, SUBLANES, d)),
            w_proj_a[l].astype(BF16), w_proj_b[l].astype(BF16), w_out[l].astype(BF16),
            w_route_hi, w_route_lo, b_route, seq, _tile(seq, POST_ROWS))

        bm = EXPERT_BLOCK
        n_blocks = (2 * m) // bm + N_EXPERTS
        sizes = counts[0, :N_EXPERTS].astype(jnp.int32)
        padded = ((sizes + bm - 1) // bm) * bm
        pad_end = jnp.cumsum(padded)
        pad_info = jnp.concatenate([pad_end - padded + sizes, padded - sizes,
                                    pad_end[-1:] // bm]).astype(jnp.int32)
        block_row = jnp.arange(n_blocks, dtype=jnp.int32) * bm
        block_expert = jnp.minimum(jnp.sum(block_row[:, None] >= pad_end[None, :], axis=1),
                                   N_EXPERTS - 1).astype(jnp.int32)
        block_active = jnp.concatenate([(block_row < pad_end[-1]).astype(jnp.int32),
                                        (pad_end[-1:] // bm).astype(jnp.int32)])
        expert_id = jnp.arange(N_EXPERTS, dtype=jnp.int32)
        later = (expert_id[None, :] > expert_id[:, None]) & (padded[None, :] > 0)
        next_expert = jnp.min(jnp.where(later, expert_id[None, :], N_EXPERTS), axis=1)
        block_next = jnp.where(next_expert < N_EXPERTS, next_expert, -1)[block_expert].astype(jnp.int32)
        dest = _plan(route, counts, _tile(seq, INPROJ_ROWS))
        d1, d2 = dest[0], dest[1]

        xs = _dispatch(d1, d2, pad_info, h2, n_blocks * bm, _tile(seq, DISPATCH_ROWS))
        ys = _experts(block_expert, block_active, block_next, xs, w1[l], w3[l], w2[l])
        nfg = normf_g[None, :]
        x2 = _final(d1, d2, x1, route, gt2, nfg, ys, seq, _tile(seq, FINAL_ROWS))
    return x2.reshape(batch, seq, d)
```

```python
import functools

import jax
import jax.numpy as jnp
from jax import lax
from jax.experimental import pallas as pl
from jax.experimental.pallas import tpu as pltpu

F32 = jnp.float32
BF16 = jnp.bfloat16
HIGHEST = lax.Precision.HIGHEST

N_HEADS = 8
HEAD_D = 128
CHUNK = 64
QKV_CONV = 4
SC_CONV = 3
N_GROUPS = 4
EXPERTS_PER_GROUP = 8
N_EXPERTS = N_GROUPS * EXPERTS_PER_GROUP
EPS = 1e-6

LANES = 128
SUBLANES = 8
BF16_ROWS = 16
EXPERT_BLOCK = 512
INPROJ_ROWS = 2048
GDN_ROWS = 256
PRE_CHUNKS = 2
POST_ROWS = 512
EXPERT_X_BUFFERS = 3
DISPATCH_ROWS = 2048
FINAL_ROWS = 512
DMA_UNROLL = 8
MIB = 1024 * 1024
VMEM_MIB = {"ada": 24, "inproj": 56, "gdn": 48, "post": 56, "plan": 16, "dispatch": 24, "experts": 40,
            "final": 32}


def _sigmoid(x):
    return 0.5 + 0.5 * jnp.tanh(0.5 * x)


def _silu(x):
    half = 0.5 * x
    return half + half * jnp.tanh(half)


def _softplus(x):
    return jnp.maximum(x, 0.0) + jnp.log(1.0 + jnp.exp(-jnp.abs(x)))


def _dot(a, b):
    return jnp.dot(a, b, preferred_element_type=F32)


def _dot_nt(a, b):
    return lax.dot_general(a, b, (((1,), (1,)), ((), ())), preferred_element_type=F32)


def _dot_hi(a, b):
    return jnp.dot(a, b, preferred_element_type=F32, precision=HIGHEST)


def _dot_split(a, b_hi, b_lo):
    a_hi = a.astype(BF16)
    a_lo = (a - a_hi.astype(F32)).astype(BF16)
    return _dot(a_hi, b_hi) + (_dot(a_lo, b_hi) + _dot(a_hi, b_lo))


def _read_rows(ref, rows):
    return jnp.concatenate([ref[pl.ds(j, rows, stride=SUBLANES), :] for j in range(SUBLANES)], axis=1)


def _write_rows(ref, value):
    rows = value.shape[0]
    for j in range(SUBLANES):
        ref[pl.ds(j, rows, stride=SUBLANES), :] = value[:, j * LANES:(j + 1) * LANES]


def _row_tile(ref, row):
    if not isinstance(row, int):
        row = pl.multiple_of(row * SUBLANES, SUBLANES)
    else:
        row = row * SUBLANES
    return ref.at[pl.ds(row, SUBLANES), :]


def _params(call, n_grid_axes=1):
    return pltpu.CompilerParams(dimension_semantics=("arbitrary",) * n_grid_axes,
                                vmem_limit_bytes=VMEM_MIB[call] * MIB)


def _ada_kernel(cb_ref, w_ref, b_ref, o_ref, s_ref):
    batch, d, _ = cb_ref.shape
    groups = w_ref.shape[1] // LANES

    @pl.when(pl.program_id(0) == 0)
    def _():
        s_ref[...] = _silu(cb_ref[...])

    def body(kc, accs):
        k0 = pl.multiple_of(kc * SUBLANES, SUBLANES)
        w = w_ref[pl.ds(k0, SUBLANES), :]
        out = []
        for b in range(batch):
            s = s_ref[b, pl.ds(k0, SUBLANES), :]
            out += [accs[b * groups + g] + w[:, g * LANES:(g + 1) * LANES] * s for g in range(groups)]
        return tuple(out)

    zero = jnp.zeros((SUBLANES, LANES), F32)
    accs = lax.fori_loop(0, d // SUBLANES, body, (zero,) * (batch * groups), unroll=4)
    o_ref[...] = jnp.zeros(o_ref.shape, F32)
    for b in range(batch):
        for g in range(groups):
            cols = slice(g * LANES, (g + 1) * LANES)
            o_ref[b:b + 1, cols] = jnp.sum(accs[b * groups + g], axis=0, keepdims=True) + b_ref[:, cols]


def _ada(c, w_ada, b_ada):
    batch, d = c.shape
    n = w_ada.shape[1]
    cb = jnp.broadcast_to(c[:, :, None], (batch, d, LANES))
    return pl.pallas_call(
        _ada_kernel,
        grid=(n // d,),
        in_specs=[
            pl.BlockSpec((batch, d, LANES), lambda j: (0, 0, 0)),
            pl.BlockSpec((d, d), lambda j: (0, j)),
            pl.BlockSpec((1, d), lambda j: (0, j)),
        ],
        out_specs=pl.BlockSpec((SUBLANES, d), lambda j: (0, j)),
        out_shape=jax.ShapeDtypeStruct((SUBLANES, n), F32),
        scratch_shapes=[pltpu.VMEM((batch, d, LANES), F32)],
        compiler_params=_params("ada"),
        name="ada",
    )(cb, w_ada, b_ada)


def _inproj_kernel(x_hbm, g_ref, sc_ref, sh_ref, wt_hbm, o_ref, ba_ref, x_ref, h_ref, w_ref, wba_ref, stage_ref,
                   stage_ba_ref, sem_x, sem, sem_ba, *, n_lead, n_ba):
    i, j = pl.program_id(0), pl.program_id(1)
    n_tiles = pl.num_programs(0)
    n_blocks, tn = w_ref.shape[0], o_ref.shape[1]
    tm, half = x_ref.shape[0], stage_ref.shape[1]

    def x_copy(tile):
        return pltpu.make_async_copy(x_hbm.at[pl.ds(pl.multiple_of(tile * tm, tm), tm)], x_ref, sem_x)

    def block_copy(jb, part):
        row = jb * tn + jnp.where(jb >= n_lead, n_ba, 0) + part * half
        return pltpu.make_async_copy(wt_hbm.at[pl.ds(pl.multiple_of(row, SUBLANES), half)],
                                     stage_ref.at[part], sem.at[part])

    def ba_copy():
        return pltpu.make_async_copy(wt_hbm.at[pl.ds(n_lead * tn, n_ba)], stage_ba_ref, sem_ba)

    @pl.when(j == 0)
    def _():
        @pl.when(i == 0)
        def _():
            x_copy(i).start()
            ba_copy().start()
            for part in range(2):
                block_copy(j, part).start()

        x_copy(i).wait()
        x = x_ref[...]
        y = x * lax.rsqrt(jnp.mean(x * x, axis=-1, keepdims=True) + EPS)
        h = (y * g_ref[...]) * (1.0 + sc_ref[...]) + sh_ref[...]
        hb = h.astype(BF16)
        h_ref[...] = hb

        @pl.when(i == 0)
        def _():
            ba_copy().wait()
            wba_ref[...] = jnp.zeros(wba_ref.shape, BF16)
            wba_ref[0:n_ba, :] = stage_ba_ref[...].astype(BF16)

        ba_ref[...] = _dot_nt(hb, wba_ref[...])

    @pl.when(jnp.logical_and(j == 1, i + 1 < n_tiles))
    def _():
        x_copy(i + 1).start()

    @pl.when(i == 0)
    def _():
        for part in range(2):
            block_copy(j, part).wait()
            w_ref[j, pl.ds(part * half, half), :] = stage_ref[part].astype(BF16)

            @pl.when(j + 1 < n_blocks)
            def _():
                block_copy(j + 1, part).start()

    rows = o_ref.shape[0] // 2
    for r in range(2):
        o_ref[pl.ds(r * rows, rows), :] = _dot_nt(h_ref[pl.ds(r * rows, rows), :], w_ref[j]).astype(BF16)


def _inproj(x2, norm_g, sc, sh, wt, n_lead, n_ba, seq, tm):
    m, d = x2.shape
    tn = d
    n = wt.shape[0] - n_ba
    assert n % tn == 0 and n_ba % (2 * SUBLANES) == 0 and n_ba <= LANES
    per_batch = seq // tm
    return pl.pallas_call(
        functools.partial(_inproj_kernel, n_lead=n_lead, n_ba=n_ba),
        grid=(m // tm, n // tn),
        in_specs=[
            pl.BlockSpec(memory_space=pl.ANY),
            pl.BlockSpec((1, d), lambda i, j: (0, 0)),
            pl.BlockSpec((None, 1, d), lambda i, j: (i // per_batch, 0, 0)),
            pl.BlockSpec((None, 1, d), lambda i, j: (i // per_batch, 0, 0)),
            pl.BlockSpec(memory_space=pl.ANY),
        ],
        out_specs=[
            pl.BlockSpec((tm, tn), lambda i, j: (i, j)),
            pl.BlockSpec((tm, LANES), lambda i, j: (i, 0)),
        ],
        out_shape=[
            jax.ShapeDtypeStruct((m, n), BF16),
            jax.ShapeDtypeStruct((m, LANES), F32),
        ],
        scratch_shapes=[
            pltpu.VMEM((tm, d), F32),
            pltpu.VMEM((tm, d), BF16),
            pltpu.VMEM((n // tn, tn, d), BF16),
            pltpu.VMEM((LANES, d), BF16),
            pltpu.VMEM((2, tn // 2, d), F32),
            pltpu.VMEM((n_ba, d), F32),
            pltpu.SemaphoreType.DMA(()),
            pltpu.SemaphoreType.DMA((2,)),
            pltpu.SemaphoreType.DMA(()),
        ],
        compiler_params=_params("inproj", 2),
        name="inproj",
    )(x2, norm_g, sc, sh, wt)


def _bmm(a, b):
    return jnp.einsum("hmk,hkn->hmn", a.astype(BF16), b.astype(BF16), preferred_element_type=F32)


def _bmm_nt(a, b):
    return jnp.einsum("hmk,hnk->hmn", a.astype(BF16), b.astype(BF16), preferred_element_type=F32)


def _unit_lower_inverse(a):
    row = lax.broadcasted_iota(jnp.int32, a.shape[1:], 0)
    col = lax.broadcasted_iota(jnp.int32, a.shape[1:], 1)
    apart = row ^ col
    eye = jnp.where(row == col, 1.0, 0.0).astype(F32)
    t = jnp.where(apart < 2, eye - a, 0.0)
    s = 2
    while s < CHUNK:
        coupling = jnp.where((apart >= s) & (apart < 2 * s), a, 0.0)
        t = t - _bmm(t, _bmm(coupling, t))
        s *= 2
    return t


def _lane_sums(x):
    h, rows, width = x.shape
    ones = jnp.ones((width, width), BF16)
    return _dot(x.reshape(h * rows, width).astype(BF16), ones).reshape(h, rows, width)


def _causal_conv_silu(win, cw, k_w):
    assert k_w == 4
    tiled = (win.shape[0] // SUBLANES, SUBLANES, win.shape[1])

    def pair(x, x1, j):
        return (x.reshape(tiled) * cw[j][None] + x1.reshape(tiled) * cw[j - 1][None]).reshape(win.shape)

    win1 = pltpu.roll(win, 1, 0)
    acc = pair(win, win1, 3) + pltpu.roll(pair(win, win1, 1), 2, 0)
    return _silu(acc[SUBLANES:, :])


def _gdn_kernel(q_ref, k_ref, v_ref, z_ref, ba_ref, cw_ref, hp_ref, o_ref,
                s_ref, tail_ref, wq_ref, u_ref, ik_ref, dec_ref):
    nb, tb = q_ref.shape[0], q_ref.shape[1]
    kd = N_HEADS * HEAD_D
    nbh = nb * N_HEADS

    @pl.when(pl.program_id(0) == 0)
    def _():
        s_ref[...] = jnp.zeros(s_ref.shape, F32)
        tail_ref[...] = jnp.zeros(tail_ref.shape, F32)

    row = lax.broadcasted_iota(jnp.int32, (CHUNK, CHUNK), 0)
    col = lax.broadcasted_iota(jnp.int32, (CHUNK, CHUNK), 1)
    causal = row >= col
    strict = row > col
    tril = jnp.where(causal, 1.0, 0.0).astype(F32)
    a_log = hp_ref[0:1, :]
    dt_bias = hp_ref[1:2, :]
    onorm_g = hp_ref[2:3, :]
    zeros_half = jnp.zeros((CHUNK, HEAD_D), F32)

    def precompute(cp, carry):
        qs, ks, vs, betas, gcs, grs, gls = [], [], [], [], [], [], []
        for sub in range(PRE_CHUNKS):
            base = pl.multiple_of((cp * PRE_CHUNKS + sub) * CHUNK, CHUNK)
            prev = pl.multiple_of(base - BF16_ROWS, BF16_ROWS)
            for b in range(nb):
                ba = ba_ref[b, pl.ds(base, CHUNK), :]
                beta_all = _sigmoid(ba)
                g_all = -jnp.exp(a_log) * _softplus(ba + dt_bias)
                gcum = _dot_hi(tril, g_all)
                gcum_t = jnp.concatenate([gcum, gcum], axis=0).T
                for h in range(N_HEADS):
                    lo, hi = h * HEAD_D, (h + 1) * HEAD_D

                    def conv(ref, off):
                        cur = ref[b, pl.ds(base, CHUNK), lo:hi].astype(F32)
                        if sub == 0:
                            before = tail_ref[b, :, off + lo:off + hi]
                        else:
                            before = ref[b, pl.ds(prev, BF16_ROWS), lo:hi].astype(F32)[SUBLANES:]
                        win = jnp.concatenate([before, cur], axis=0)
                        return _causal_conv_silu(win, cw_ref[:, :, off + lo:off + hi], QKV_CONV)

                    qs.append(conv(q_ref, 0))
                    ks.append(conv(k_ref, kd))
                    vs.append(conv(v_ref, 2 * kd))
                    betas.append(beta_all[:, h:h + 1])
                    gcs.append(gcum[:, N_HEADS + h:N_HEADS + h + 1])
                    grs.append(gcum_t[N_HEADS + h:N_HEADS + h + 1, 0:CHUNK])
                    gls.append(gcum[CHUNK - 1:CHUNK, N_HEADS + h:N_HEADS + h + 1])
        q, k, v = jnp.stack(qs), jnp.stack(ks), jnp.stack(vs)
        beta, gc, gr, gl = jnp.stack(betas), jnp.stack(gcs), jnp.stack(grs), jnp.stack(gls)
        qn = q * (lax.rsqrt(_lane_sums(q * q) + EPS) * (HEAD_D ** -0.5))
        kn = k * lax.rsqrt(_lane_sums(k * k) + EPS)
        decay = jnp.where(causal, jnp.exp(jnp.where(causal, gc - gr, 0.0)), 0.0)
        kb = kn * beta
        e_gc = jnp.exp(gc)
        kq = _bmm_nt(jnp.concatenate([kb, qn], axis=1), kn)
        a = jnp.where(strict, kq[:, :CHUNK] * decay, 0.0)
        intra = kq[:, CHUNK:] * decay
        uw = _bmm(_unit_lower_inverse(a), jnp.concatenate([v * beta, kb * e_gc], axis=2))
        wq = jnp.concatenate([uw[:, :, HEAD_D:], qn * e_gc], axis=1).astype(BF16)
        k_dec = kn * jnp.exp(gl - gc)
        k_dec_t = jnp.stack([jnp.concatenate([k_dec[i], zeros_half], axis=0).T[:, :CHUNK]
                             for i in range(PRE_CHUNKS * nbh)])
        ik = jnp.concatenate([intra, k_dec_t], axis=1).astype(BF16)
        dec = jnp.broadcast_to(jnp.exp(gl), (PRE_CHUNKS * nbh, 1, HEAD_D))
        for sub in range(PRE_CHUNKS):
            c = cp * PRE_CHUNKS + sub
            rows = slice(sub * nbh, (sub + 1) * nbh)
            u_ref[c] = uw[rows, :, :HEAD_D]
            wq_ref[c] = wq[rows]
            ik_ref[c] = ik[rows]
            dec_ref[c] = dec[rows]

        last = pl.multiple_of((cp + 1) * PRE_CHUNKS * CHUNK - BF16_ROWS, BF16_ROWS)
        for b in range(nb):
            for j, ref in enumerate((q_ref, k_ref, v_ref)):
                rows = ref[b, pl.ds(last, BF16_ROWS), :].astype(F32)
                tail_ref[b, :, j * kd:(j + 1) * kd] = rows[BF16_ROWS - SUBLANES:, :]
        return carry

    def recur(c, carry):
        base = pl.multiple_of(c * CHUNK, CHUNK)
        state = s_ref[...]
        ws = _bmm(wq_ref[c], state)
        v_new = u_ref[c] - ws[:, :CHUNK]
        r = _bmm(ik_ref[c], v_new)
        o = ws[:, CHUNK:] + r[:, :CHUNK]
        s_ref[...] = state * dec_ref[c] + r[:, CHUNK:]
        on = o * lax.rsqrt(jnp.mean(o * o, axis=-1, keepdims=True) + EPS) * onorm_g
        for b in range(nb):
            for h in range(N_HEADS):
                lo, hi = h * HEAD_D, (h + 1) * HEAD_D
                z = z_ref[b, pl.ds(base, CHUNK), lo:hi].astype(F32)
                o_ref[b, pl.ds(base, CHUNK), lo:hi] = (on[b * N_HEADS + h] * _silu(z)).astype(BF16)
        return carry

    lax.fori_loop(0, tb // (PRE_CHUNKS * CHUNK), precompute, 0)
    lax.fori_loop(0, tb // CHUNK, recur, 0)


def _gdn(proj, ba, conv_w, head_params, batch, seq, tb):
    kd = N_HEADS * HEAD_D
    nc = tb // CHUNK
    nbh = batch * N_HEADS
    proj3 = proj.reshape(batch, seq, proj.shape[1])
    ba3 = ba.reshape(batch, seq, LANES)

    def col(j):
        return pl.BlockSpec((batch, tb, kd), lambda t: (0, t, j))

    out = pl.pallas_call(
        _gdn_kernel,
        grid=(seq // tb,),
        in_specs=[
            col(0), col(1), col(2), col(3),
            pl.BlockSpec((batch, tb, LANES), lambda t: (0, t, 0)),
            pl.BlockSpec((QKV_CONV, SUBLANES, 3 * kd), lambda t: (0, 0, 0)),
            pl.BlockSpec((SUBLANES, LANES), lambda t: (0, 0)),
        ],
        out_specs=pl.BlockSpec((batch, tb, kd), lambda t: (0, t, 0)),
        out_shape=jax.ShapeDtypeStruct((batch, seq, kd), BF16),
        scratch_shapes=[
            pltpu.VMEM((nbh, HEAD_D, HEAD_D), F32),
            pltpu.VMEM((batch, SUBLANES, 3 * kd), F32),
            pltpu.VMEM((nc, nbh, 2 * CHUNK, HEAD_D), BF16),
            pltpu.VMEM((nc, nbh, CHUNK, HEAD_D), F32),
            pltpu.VMEM((nc, nbh, CHUNK + HEAD_D, CHUNK), BF16),
            pltpu.VMEM((nc, nbh, 1, HEAD_D), F32),
        ],
        compiler_params=_params("gdn"),
        name="gdn",
    )(proj3, proj3, proj3, proj3, ba3, conv_w, head_params)
    return out.reshape(batch * seq, kd)


def _post_kernel(x_ref, og_ref, sb_ref, sc_ref, sx_ref, ga_ref, gb_ref, mod_ref, n2g_ref, cw_ref,
                 wpa_ref, wpb_ref, wout_ref, wrh_ref, wrl_ref, br_ref,
                 x1_ref, h2_ref, route_ref, cnt_ref, win_ref, run_ref, *, per_batch):
    tm = x_ref.shape[0]
    i = pl.program_id(0)

    @pl.when(i == 0)
    def _():
        run_ref[...] = jnp.zeros(run_ref.shape, F32)

    @pl.when(i % per_batch == 0)
    def _():
        win_ref[...] = jnp.zeros(win_ref.shape, F32)

    assert SC_CONV == 3
    prod = sc_ref[...].astype(F32) * sx_ref[...].astype(F32)
    win = jnp.concatenate([win_ref[...], prod], axis=0)
    win_ref[...] = prod[tm - SUBLANES:, :]
    tiled = (win.shape[0] // SUBLANES, SUBLANES, win.shape[1])
    win1 = pltpu.roll(win, 1, 0)
    conv = (win.reshape(tiled) * cw_ref[2][None] + win1.reshape(tiled) * cw_ref[1][None]
            + pltpu.roll(win1, 1, 0).reshape(tiled) * cw_ref[0][None]).reshape(win.shape)[SUBLANES:, :]
    y_b = _dot((sb_ref[...].astype(F32) * conv).astype(BF16), wpb_ref[...])
    y_a = _dot(og_ref[...], wpa_ref[...])
    merged = _sigmoid(ga_ref[...].astype(F32)) * y_a + _sigmoid(gb_ref[...].astype(F32)) * y_b
    mix = _dot(merged.astype(BF16), wout_ref[...])
    x1 = x_ref[...] + mod_ref[0:1, :] * mix
    x1_ref[...] = x1

    y = x1 * lax.rsqrt(jnp.mean(x1 * x1, axis=-1, keepdims=True) + EPS)
    h2 = (y * n2g_ref[...]) * (1.0 + mod_ref[1:2, :]) + mod_ref[2:3, :]
    _write_rows(h2_ref, h2)

    lg = _dot_split(h2, wrh_ref[...], wrl_ref[...]) + br_ref[...]
    lane = lax.broadcasted_iota(jnp.int32, lg.shape, 1).astype(F32)
    neg = jnp.float32(-jnp.inf)
    big = jnp.float32(2 * LANES)

    def first_max(mask):
        vmax = jnp.max(jnp.where(mask, lg, neg), axis=-1, keepdims=True)
        idx = jnp.min(jnp.where(mask & (lg == vmax), lane, big), axis=-1, keepdims=True)
        return vmax, idx

    gmask = lane < N_GROUPS
    g_max, g_sel = first_max(gmask)
    p_group = 1.0 / jnp.sum(jnp.where(gmask, jnp.exp(lg - g_max), 0.0), axis=-1, keepdims=True)
    e_lo = N_GROUPS + EXPERTS_PER_GROUP * g_sel
    emask = (lane >= e_lo) & (lane < e_lo + EXPERTS_PER_GROUP)
    v1, i1 = first_max(emask)
    v2, i2 = first_max(emask & (lane != i1))
    ex = jnp.exp(v2 - v1)
    w1 = p_group * (1.0 / (1.0 + ex))
    w2 = p_group * (ex / (1.0 + ex))
    e1 = i1 - N_GROUPS
    e2 = i2 - N_GROUPS

    onehot = jnp.where((lane == e1) | (lane == e2), 1.0, 0.0).astype(F32)
    row = lax.broadcasted_iota(jnp.int32, (tm, tm), 0)
    col = lax.broadcasted_iota(jnp.int32, (tm, tm), 1)
    before = jnp.where(row > col, 1.0, 0.0).astype(BF16)
    seen = _dot(before, onehot.astype(BF16)) + run_ref[0:1, :]
    r1 = jnp.sum(jnp.where(lane == e1, seen, 0.0), axis=-1, keepdims=True)
    r2 = jnp.sum(jnp.where(lane == e2, seen, 0.0), axis=-1, keepdims=True)
    run_ref[0:1, :] = run_ref[0:1, :] + jnp.sum(onehot, axis=0, keepdims=True)
    cnt_ref[...] = jnp.broadcast_to(run_ref[0:1, :], cnt_ref.shape)

    out = jnp.where(lane == 0, e1, 0.0)
    out = jnp.where(lane == 1, e2, out)
    out = jnp.where(lane == 2, w1, out)
    out = jnp.where(lane == 3, w2, out)
    out = jnp.where(lane == 4, r1, out)
    out = jnp.where(lane == 5, r2, out)
    route_ref[...] = out


def _post(x2, og, proj, mod, n2g, conv_w, wpa, wpb, wout, w_route_hi, w_route_lo, b_route, seq, tm):
    m, d = x2.shape
    assert d == SUBLANES * LANES
    per_batch = seq // tm

    def rows(j):
        return pl.BlockSpec((tm, d), lambda i: (i, j))

    def whole(shape):
        return pl.BlockSpec(shape, lambda i: tuple(0 for _ in shape))

    return pl.pallas_call(
        functools.partial(_post_kernel, per_batch=per_batch),
        grid=(m // tm,),
        in_specs=[
            rows(0), rows(0), rows(4), rows(5), rows(6), rows(7), rows(8),
            pl.BlockSpec((None, SUBLANES, d), lambda i: (i // per_batch, 0, 0)),
            whole((1, d)), whole((SC_CONV, SUBLANES, d)),
            whole((d, d)), whole((d, d)), whole((d, d)),
            whole((d, LANES)), whole((d, LANES)), whole((1, LANES)),
        ],
        out_specs=[
            rows(0), pl.BlockSpec((tm * SUBLANES, LANES), lambda i: (i, 0)),
            pl.BlockSpec((tm, LANES), lambda i: (i, 0)),
            pl.BlockSpec((SUBLANES, LANES), lambda i: (0, 0)),
        ],
        out_shape=[
            jax.ShapeDtypeStruct((m, d), F32),
            jax.ShapeDtypeStruct((m * SUBLANES, LANES), F32),
            jax.ShapeDtypeStruct((m, LANES), F32),
            jax.ShapeDtypeStruct((SUBLANES, LANES), F32),
        ],
        scratch_shapes=[
            pltpu.VMEM((SUBLANES, d), F32),
            pltpu.VMEM((SUBLANES, LANES), F32),
        ],
        compiler_params=_params("post"),
        name="post",
    )(x2, og, proj, proj, proj, proj, proj, mod, n2g, conv_w, wpa, wpb, wout, w_route_hi, w_route_lo,
      b_route)


def _plan_kernel(route_ref, cnt_ref, d_ref):
    bm = EXPERT_BLOCK
    sizes = cnt_ref[...]
    padded = jnp.floor((sizes + (bm - 1.0)) * (1.0 / bm)) * bm
    lane_i = lax.broadcasted_iota(jnp.int32, sizes.shape, 1)
    incl = padded
    s = 1
    while s < LANES:
        incl = incl + jnp.where(lane_i >= s, pltpu.roll(incl, s, 1), 0.0)
        s *= 2
    start = (incl - padded)[0:1, :]
    r = route_ref[...]
    lane = lax.broadcasted_iota(jnp.int32, r.shape, 1).astype(F32)
    d1 = jnp.sum(jnp.where(lane == r[:, 0:1], start, 0.0), axis=-1, keepdims=True) + r[:, 4:5]
    d2 = jnp.sum(jnp.where(lane == r[:, 1:2], start, 0.0), axis=-1, keepdims=True) + r[:, 5:6]
    out = jnp.where(lane == 0.0, d1, jnp.where(lane == 1.0, d2, 0.0))
    d_ref[...] = out.T[0:SUBLANES, :].astype(jnp.int32)


def _plan(route, counts, tm):
    m = route.shape[0]
    return pl.pallas_call(
        _plan_kernel,
        grid=(m // tm,),
        in_specs=[
            pl.BlockSpec((tm, LANES), lambda i: (i, 0)),
            pl.BlockSpec((SUBLANES, LANES), lambda i: (0, 0)),
        ],
        out_specs=pl.BlockSpec((SUBLANES, tm), lambda i: (0, i)),
        out_shape=jax.ShapeDtypeStruct((SUBLANES, m), jnp.int32),
        compiler_params=_params("plan"),
        name="plan",
    )(route, counts)


def _row_copy(src_ref, src_row, dst_ref, dst_row, sem):
    return pltpu.make_async_copy(_row_tile(src_ref, src_row), _row_tile(dst_ref, dst_row), sem)


def _rows_copy(src_ref, dst_ref, dst_row, n, sem):
    return pltpu.make_async_copy(src_ref, dst_ref.at[pl.ds(dst_row * SUBLANES, n * SUBLANES), :], sem)


def _dispatch_kernel(d1_ref, d2_ref, pad_ref, h2_ref, xs_ref, zero_ref, sem):
    tm = h2_ref.shape[0] // SUBLANES
    bm = zero_ref.shape[0] // SUBLANES
    i = pl.program_id(0)
    t0 = i * tm
    n_blocks = xs_ref.shape[0] // (bm * SUBLANES)

    def zero_rows(row, n):
        dst = xs_ref.at[pl.ds(pl.multiple_of(row * SUBLANES, SUBLANES), n * SUBLANES), :]
        return pltpu.make_async_copy(zero_ref.at[pl.ds(0, n * SUBLANES), :], dst, sem.at[2])

    def zero_fill(act):
        def expert(e, carry):
            row, cnt = pad_ref[e], pad_ref[N_EXPERTS + e]
            n = bm // 2
            while n >= 1:
                @pl.when((cnt & n) != 0)
                def _():
                    act(zero_rows(row + (cnt & (-2 * n)), n))
                n //= 2
            return carry

        lax.fori_loop(0, N_EXPERTS, expert, 0)
        lax.fori_loop(pad_ref[2 * N_EXPERTS], n_blocks, lambda j, c: (act(zero_rows(j * bm, bm)), c)[1], 0)

    @pl.when(i == 0)
    def _():
        zero_ref[...] = jnp.zeros(zero_ref.shape, F32)
        zero_fill(lambda copy: copy.start())

    def start(r, carry):
        _row_copy(h2_ref, r, xs_ref, d1_ref[t0 + r], sem.at[0]).start(priority=0)
        _row_copy(h2_ref, r, xs_ref, d2_ref[t0 + r], sem.at[1]).start(priority=1)
        return carry

    lax.fori_loop(0, tm, start, 0, unroll=DMA_UNROLL)
    _rows_copy(h2_ref, xs_ref, 0, tm, sem.at[0]).wait()
    _rows_copy(h2_ref, xs_ref, 0, tm, sem.at[1]).wait()

    @pl.when(i == pl.num_programs(0) - 1)
    def _():
        zero_fill(lambda copy: copy.wait())


def _dispatch(d1, d2, pad_info, h2, cap, tm):
    m = h2.shape[0] // SUBLANES
    return pl.pallas_call(
        _dispatch_kernel,
        grid_spec=pltpu.PrefetchScalarGridSpec(
            num_scalar_prefetch=3,
            grid=(m // tm,),
            in_specs=[pl.BlockSpec((tm * SUBLANES, LANES), lambda i, d1, d2, tl: (i, 0))],
            out_specs=pl.BlockSpec(memory_space=pl.ANY),
            scratch_shapes=[pltpu.VMEM((EXPERT_BLOCK * SUBLANES, LANES), F32),
                            pltpu.SemaphoreType.DMA((3,))],
        ),
        out_shape=jax.ShapeDtypeStruct((cap * SUBLANES, LANES), F32),
        compiler_params=_params("dispatch"),
        name="dispatch",
    )(d1, d2, pad_info, h2)


def _expert_kernel(be_ref, act_ref, nxt_ref, xs_hbm, w1_hbm, w3_hbm, w2_hbm, y_ref, w1b_ref, w3b_ref, w2b_ref,
                   w1s_ref, w3s_ref, w2s_ref, xbuf_ref, sem, sem_x):
    b = pl.program_id(0)
    n_active = act_ref[pl.num_programs(0)]
    weights = ((w1_hbm, w1s_ref, w1b_ref), (w3_hbm, w3s_ref, w3b_ref), (w2_hbm, w2s_ref, w2b_ref))
    n_buf, rows = xbuf_ref.shape[0], xbuf_ref.shape[1]
    ahead = n_buf - 1

    def weight_copy(k, e):
        w_hbm, ws_ref, _ = weights[k]
        return pltpu.make_async_copy(w_hbm.at[e], ws_ref, sem.at[k])

    def x_copy(blk):
        slot = lax.rem(blk, n_buf)
        src = xs_hbm.at[pl.ds(pl.multiple_of(blk * rows, rows), rows), :]
        return pltpu.make_async_copy(src, xbuf_ref.at[slot], sem_x.at[slot])

    @pl.when(act_ref[b] > 0)
    def _():
        @pl.when(b == 0)
        def _():
            for k in range(len(weights)):
                weight_copy(k, be_ref[b]).start()
            for a in range(ahead):
                @pl.when(a < n_active)
                def _():
                    x_copy(jnp.int32(a)).start()

        @pl.when(b + ahead < n_active)
        def _():
            x_copy(b + ahead).start()

        @pl.when((b == 0) | (be_ref[b] != be_ref[jnp.maximum(b - 1, 0)]))
        def _():
            for k, (_, ws_ref, wb_ref) in enumerate(weights):
                weight_copy(k, be_ref[b]).wait()
                wb_ref[...] = ws_ref[...].astype(BF16)

                @pl.when(nxt_ref[b] >= 0)
                def _():
                    weight_copy(k, nxt_ref[b]).start()

        x_copy(b).wait()
        xb = _read_rows(xbuf_ref.at[lax.rem(b, n_buf)], rows // SUBLANES).astype(BF16)
        hid = _silu(_dot(xb, w1b_ref[...])) * _dot(xb, w3b_ref[...])
        _write_rows(y_ref, _dot(hid.astype(BF16), w2b_ref[...]))

    @pl.when(act_ref[b] == 0)
    def _():
        y_ref[...] = jnp.zeros(y_ref.shape, F32)


def _experts(block_expert, block_active, block_next, xs, w1, w3, w2):
    cap = xs.shape[0] // SUBLANES
    d, de = w1.shape[1], w1.shape[2]
    bm = EXPERT_BLOCK

    return pl.pallas_call(
        _expert_kernel,
        grid_spec=pltpu.PrefetchScalarGridSpec(
            num_scalar_prefetch=3,
            grid=(cap // bm,),
            in_specs=[
                pl.BlockSpec(memory_space=pl.ANY),
                pl.BlockSpec(memory_space=pl.ANY),
                pl.BlockSpec(memory_space=pl.ANY),
                pl.BlockSpec(memory_space=pl.ANY),
            ],
            out_specs=pl.BlockSpec((bm * SUBLANES, LANES), lambda b, be, act, nxt: (b, 0)),
            scratch_shapes=[pltpu.VMEM((d, de), BF16), pltpu.VMEM((d, de), BF16), pltpu.VMEM((de, d), BF16),
                            pltpu.VMEM((d, de), F32), pltpu.VMEM((d, de), F32), pltpu.VMEM((de, d), F32),
                            pltpu.VMEM((EXPERT_X_BUFFERS, bm * SUBLANES, LANES), F32),
                            pltpu.SemaphoreType.DMA((3,)), pltpu.SemaphoreType.DMA((EXPERT_X_BUFFERS,))],
        ),
        out_shape=jax.ShapeDtypeStruct((cap * SUBLANES, LANES), F32),
        compiler_params=_params("experts"),
        name="experts",
    )(block_expert, block_active, block_next, xs, w1, w3, w2)


def _final_kernel(d1_ref, d2_ref, x1_ref, route_ref, gt_ref, nfg_ref, ys_ref, o_ref, buf_ref, sem):
    tm = x1_ref.shape[0]
    i = pl.program_id(0)

    def gather(tile, s):
        t0 = tile * tm

        def start(r, carry):
            _row_copy(ys_ref, d1_ref[t0 + r], buf_ref.at[s, 0], r, sem.at[s, 0]).start(priority=0)
            _row_copy(ys_ref, d2_ref[t0 + r], buf_ref.at[s, 1], r, sem.at[s, 1]).start(priority=1)
            return carry

        lax.fori_loop(0, tm, start, 0, unroll=DMA_UNROLL)

    def wait_rows(s):
        for k in range(2):
            pltpu.make_async_copy(ys_ref.at[pl.ds(0, tm * SUBLANES), :], buf_ref.at[s, k],
                                  sem.at[s, k]).wait()

    def combine(cur, nxt):
        wait_rows(cur)
        last = pl.num_programs(0) - 1
        t0 = jnp.minimum(i + 1, last) * tm
        for r in range(tm):
            _row_copy(ys_ref, d1_ref[t0 + r], buf_ref.at[nxt, 0], r, sem.at[nxt, 0]).start(priority=0)
            _row_copy(ys_ref, d2_ref[t0 + r], buf_ref.at[nxt, 1], r, sem.at[nxt, 1]).start(priority=1)
        route = route_ref[...]
        moe = (_read_rows(buf_ref.at[cur, 0], tm) * route[:, 2:3]
               + _read_rows(buf_ref.at[cur, 1], tm) * route[:, 3:4])
        x2 = x1_ref[...] + gt_ref[...] * moe
        y = x2 * lax.rsqrt(jnp.mean(x2 * x2, axis=-1, keepdims=True) + EPS)
        o_ref[...] = y * nfg_ref[...]

        @pl.when(i == last)
        def _():
            wait_rows(nxt)

    @pl.when(i == 0)
    def _():
        gather(0, 0)

    @pl.when(i % 2 == 0)
    def _():
        combine(0, 1)

    @pl.when(i % 2 == 1)
    def _():
        combine(1, 0)


def _final(d1, d2, x1, route, gt2, nfg, ys, seq, tm):
    m, d = x1.shape
    per_batch = seq // tm
    return pl.pallas_call(
        _final_kernel,
        grid_spec=pltpu.PrefetchScalarGridSpec(
            num_scalar_prefetch=2,
            grid=(m // tm,),
            in_specs=[
                pl.BlockSpec((tm, d), lambda i, d1, d2: (i, 0)),
                pl.BlockSpec((tm, LANES), lambda i, d1, d2: (i, 0)),
                pl.BlockSpec((None, 1, d), lambda i, d1, d2: (i // per_batch, 0, 0)),
                pl.BlockSpec((1, d), lambda i, d1, d2: (0, 0)),
                pl.BlockSpec(memory_space=pl.ANY),
            ],
            out_specs=pl.BlockSpec((tm, d), lambda i, d1, d2: (i, 0)),
            scratch_shapes=[pltpu.VMEM((2, 2, tm * SUBLANES, LANES), F32),
                            pltpu.SemaphoreType.DMA((2, 2))],
        ),
        out_shape=jax.ShapeDtypeStruct((m, d), F32),
        compiler_params=_params("final"),
        name="final",
    )(d1, d2, x1, route, gt2, nfg, ys)


def _tile(n, pref):
    t = min(n, pref)
    assert n % t == 0
    return t


def kernel(x, c, w_ada, b_ada, norm1_g, w_in, conv_qkv_w, a_log, dt_bias, onorm_g, w_proj_a,
           conv_sc_w, w_proj_b, w_out, norm2_g, w_group, b_group, w_expert, b_expert, w1, w3, w2,
           normf_g):
    batch, seq, d = x.shape
    depth = w_ada.shape[0]
    m = batch * seq
    kd = N_HEADS * HEAD_D
    assert d == kd and seq % CHUNK == 0 and batch <= SUBLANES
    assert depth == 1, "the last stage applies the final rmsnorm: one layer only"
    x2 = x.reshape(m, d)

    for l in range(depth):
        mod = _ada(c, w_ada[l], b_ada[l][None, :])[:batch]
        sh1, sc1, gt1, sh2, sc2, gt2 = [mod[:, None, j * d:(j + 1) * d] for j in range(6)]

        w = w_in[l]
        o_ba = 3 * kd + kd
        proj, ba = _inproj(x2, norm1_g[l][None, :], sc1, sh1, jnp.swapaxes(w, 0, 1), o_ba // d, 2 * N_HEADS,
                           seq, _tile(seq, INPROJ_ROWS))

        head_params = jnp.zeros((SUBLANES, LANES), F32)
        head_params = head_params.at[0, N_HEADS:2 * N_HEADS].set(a_log[l])
        head_params = head_params.at[1, N_HEADS:2 * N_HEADS].set(dt_bias[l])
        head_params = head_params.at[2, :].set(onorm_g[l])
        conv_taps = jnp.broadcast_to(conv_qkv_w[l][:, None, :], (QKV_CONV, SUBLANES, 3 * kd))
        og = _gdn(proj, ba, conv_taps, head_params, batch, seq, _tile(seq, GDN_ROWS))

        mod_post = jnp.zeros((batch, SUBLANES, d), F32)
        mod_post = mod_post.at[:, 0:1].set(gt1).at[:, 1:2].set(sc2).at[:, 2:3].set(sh2)
        w_route = jnp.zeros((d, LANES), F32)
        w_route = w_route.at[:, :N_GROUPS].set(w_group[l]).at[:, N_GROUPS:N_GROUPS + N_EXPERTS].set(w_expert[l])
        w_route_hi = w_route.astype(BF16)
        w_route_lo = (w_route - w_route_hi.astype(F32)).astype(BF16)
        b_route = jnp.zeros((1, LANES), F32)
        b_route = b_route.at[0, :N_GROUPS].set(b_group[l]).at[0, N_GROUPS:N_GROUPS + N_EXPERTS].set(b_expert[l])
        x1, h2, route, counts = _post(
            x2, og, proj, mod_post, norm2_g[l][None, :],
            jnp.broadcast_to(conv_sc_w[l][:, None, :], (SC_CONV, SUBLANES, d)),
            w_proj_a[l].astype(BF16), w_proj_b[l].astype(BF16), w_out[l].astype(BF16),
            w_route_hi, w_route_lo, b_route, seq, _tile(seq, POST_ROWS))

        bm = EXPERT_BLOCK
        n_blocks = (2 * m) // bm + N_EXPERTS
        sizes = counts[0, :N_EXPERTS].astype(jnp.int32)
        padded = ((sizes + bm - 1) // bm) * bm
        pad_end = jnp.cumsum(padded)
        pad_info = jnp.concatenate([pad_end - padded + sizes, padded - sizes,
                                    pad_end[-1:] // bm]).astype(jnp.int32)
        block_row = jnp.arange(n_blocks, dtype=jnp.int32) * bm
        block_expert = jnp.minimum(jnp.sum(block_row[:, None] >= pad_end[None, :], axis=1),
                                   N_EXPERTS - 1).astype(jnp.int32)
        block_active = jnp.concatenate([(block_row < pad_end[-1]).astype(jnp.int32),
                                        (pad_end[-1:] // bm).astype(jnp.int32)])
        expert_id = jnp.arange(N_EXPERTS, dtype=jnp.int32)
        later = (expert_id[None, :] > expert_id[:, None]) & (padded[None, :] > 0)
        next_expert = jnp.min(jnp.where(later, expert_id[None, :], N_EXPERTS), axis=1)
        block_next = jnp.where(next_expert < N_EXPERTS, next_expert, -1)[block_expert].astype(jnp.int32)
        dest = _plan(route, counts, _tile(seq, INPROJ_ROWS))
        d1, d2 = dest[0], dest[1]

        xs = _dispatch(d1, d2, pad_info, h2, n_blocks * bm, _tile(seq, DISPATCH_ROWS))
        ys = _experts(block_expert, block_active, block_next, xs, w1[l], w3[l], w2[l])
        nfg = normf_g[None, :]
        x2 = _final(d1, d2, x1, route, gt2, nfg, ys, seq, _tile(seq, FINAL_ROWS))
    return x2.reshape(batch, seq, d)
```

```python
import functools

import jax
import jax.numpy as jnp
from jax import lax
from jax.experimental import pallas as pl
from jax.experimental.pallas import tpu as pltpu

F32 = jnp.float32
BF16 = jnp.bfloat16
HIGHEST = lax.Precision.HIGHEST

N_HEADS = 8
HEAD_D = 128
CHUNK = 64
QKV_CONV = 4
SC_CONV = 3
N_GROUPS = 4
EXPERTS_PER_GROUP = 8
N_EXPERTS = N_GROUPS * EXPERTS_PER_GROUP
EPS = 1e-6

LANES = 128
SUBLANES = 8
BF16_ROWS = 16
EXPERT_BLOCK = 512
INPROJ_ROWS = 2048
GDN_ROWS = 256
PRE_CHUNKS = 2
POST_ROWS = 512
EXPERT_X_BUFFERS = 4
DISPATCH_ROWS = 2048
FINAL_ROWS = 512
DMA_UNROLL = 8
MIB = 1024 * 1024
VMEM_MIB = {"ada": 24, "inproj": 56, "gdn": 48, "post": 56, "plan": 16, "dispatch": 24, "experts": 40,
            "final": 32}


def _sigmoid(x):
    return 0.5 + 0.5 * jnp.tanh(0.5 * x)


def _silu(x):
    half = 0.5 * x
    return half + half * jnp.tanh(half)


def _softplus(x):
    return jnp.maximum(x, 0.0) + jnp.log(1.0 + jnp.exp(-jnp.abs(x)))


def _dot(a, b):
    return jnp.dot(a, b, preferred_element_type=F32)


def _dot_nt(a, b):
    return lax.dot_general(a, b, (((1,), (1,)), ((), ())), preferred_element_type=F32)


def _dot_hi(a, b):
    return jnp.dot(a, b, preferred_element_type=F32, precision=HIGHEST)


def _dot_split(a, b_hi, b_lo):
    a_hi = a.astype(BF16)
    a_lo = (a - a_hi.astype(F32)).astype(BF16)
    return _dot(a_hi, b_hi) + (_dot(a_lo, b_hi) + _dot(a_hi, b_lo))


def _read_rows(ref, rows):
    return jnp.concatenate([ref[pl.ds(j, rows, stride=SUBLANES), :] for j in range(SUBLANES)], axis=1)


def _write_rows(ref, value):
    rows = value.shape[0]
    for j in range(SUBLANES):
        ref[pl.ds(j, rows, stride=SUBLANES), :] = value[:, j * LANES:(j + 1) * LANES]


def _row_tile(ref, row):
    if not isinstance(row, int):
        row = pl.multiple_of(row * SUBLANES, SUBLANES)
    else:
        row = row * SUBLANES
    return ref.at[pl.ds(row, SUBLANES), :]


def _params(call, n_grid_axes=1):
    return pltpu.CompilerParams(dimension_semantics=("arbitrary",) * n_grid_axes,
                                vmem_limit_bytes=VMEM_MIB[call] * MIB)


def _ada_kernel(cb_ref, w_ref, b_ref, o_ref, s_ref):
    batch, d, _ = cb_ref.shape
    groups = w_ref.shape[1] // LANES

    @pl.when(pl.program_id(0) == 0)
    def _():
        s_ref[...] = _silu(cb_ref[...])

    def body(kc, accs):
        k0 = pl.multiple_of(kc * SUBLANES, SUBLANES)
        w = w_ref[pl.ds(k0, SUBLANES), :]
        out = []
        for b in range(batch):
            s = s_ref[b, pl.ds(k0, SUBLANES), :]
            out += [accs[b * groups + g] + w[:, g * LANES:(g + 1) * LANES] * s for g in range(groups)]
        return tuple(out)

    zero = jnp.zeros((SUBLANES, LANES), F32)
    accs = lax.fori_loop(0, d // SUBLANES, body, (zero,) * (batch * groups), unroll=4)
    o_ref[...] = jnp.zeros(o_ref.shape, F32)
    for b in range(batch):
        for g in range(groups):
            cols = slice(g * LANES, (g + 1) * LANES)
            o_ref[b:b + 1, cols] = jnp.sum(accs[b * groups + g], axis=0, keepdims=True) + b_ref[:, cols]


def _ada(c, w_ada, b_ada):
    batch, d = c.shape
    n = w_ada.shape[1]
    cb = jnp.broadcast_to(c[:, :, None], (batch, d, LANES))
    return pl.pallas_call(
        _ada_kernel,
        grid=(n // d,),
        in_specs=[
            pl.BlockSpec((batch, d, LANES), lambda j: (0, 0, 0)),
            pl.BlockSpec((d, d), lambda j: (0, j)),
            pl.BlockSpec((1, d), lambda j: (0, j)),
        ],
        out_specs=pl.BlockSpec((SUBLANES, d), lambda j: (0, j)),
        out_shape=jax.ShapeDtypeStruct((SUBLANES, n), F32),
        scratch_shapes=[pltpu.VMEM((batch, d, LANES), F32)],
        compiler_params=_params("ada"),
        name="ada",
    )(cb, w_ada, b_ada)


def _inproj_kernel(x_hbm, g_ref, sc_ref, sh_ref, wt_hbm, o_ref, ba_ref, x_ref, h_ref, w_ref, wba_ref, stage_ref,
                   stage_ba_ref, sem_x, sem, sem_ba, *, n_lead, n_ba):
    i, j = pl.program_id(0), pl.program_id(1)
    n_tiles = pl.num_programs(0)
    n_blocks, tn = w_ref.shape[0], o_ref.shape[1]
    tm, half = x_ref.shape[0], stage_ref.shape[1]

    def x_copy(tile):
        return pltpu.make_async_copy(x_hbm.at[pl.ds(pl.multiple_of(tile * tm, tm), tm)], x_ref, sem_x)

    def block_copy(jb, part):
        row = jb * tn + jnp.where(jb >= n_lead, n_ba, 0) + part * half
        return pltpu.make_async_copy(wt_hbm.at[pl.ds(pl.multiple_of(row, SUBLANES), half)],
                                     stage_ref.at[part], sem.at[part])

    def ba_copy():
        return pltpu.make_async_copy(wt_hbm.at[pl.ds(n_lead * tn, n_ba)], stage_ba_ref, sem_ba)

    @pl.when(j == 0)
    def _():
        @pl.when(i == 0)
        def _():
            x_copy(i).start()
            ba_copy().start()
            for part in range(2):
                block_copy(j, part).start()

        x_copy(i).wait()
        x = x_ref[...]
        y = x * lax.rsqrt(jnp.mean(x * x, axis=-1, keepdims=True) + EPS)
        h = (y * g_ref[...]) * (1.0 + sc_ref[...]) + sh_ref[...]
        hb = h.astype(BF16)
        h_ref[...] = hb

        @pl.when(i == 0)
        def _():
            ba_copy().wait()
            wba_ref[...] = jnp.zeros(wba_ref.shape, BF16)
            wba_ref[0:n_ba, :] = stage_ba_ref[...].astype(BF16)

        ba_ref[...] = _dot_nt(hb, wba_ref[...])

    @pl.when(jnp.logical_and(j == 1, i + 1 < n_tiles))
    def _():
        x_copy(i + 1).start()

    @pl.when(i == 0)
    def _():
        for part in range(2):
            block_copy(j, part).wait()
            w_ref[j, pl.ds(part * half, half), :] = stage_ref[part].astype(BF16)

            @pl.when(j + 1 < n_blocks)
            def _():
                block_copy(j + 1, part).start()

    rows = o_ref.shape[0] // 2
    for r in range(2):
        o_ref[pl.ds(r * rows, rows), :] = _dot_nt(h_ref[pl.ds(r * rows, rows), :], w_ref[j]).astype(BF16)


def _inproj(x2, norm_g, sc, sh, wt, n_lead, n_ba, seq, tm):
    m, d = x2.shape
    tn = d
    n = wt.shape[0] - n_ba
    assert n % tn == 0 and n_ba % (2 * SUBLANES) == 0 and n_ba <= LANES
    per_batch = seq // tm
    return pl.pallas_call(
        functools.partial(_inproj_kernel, n_lead=n_lead, n_ba=n_ba),
        grid=(m // tm, n // tn),
        in_specs=[
            pl.BlockSpec(memory_space=pl.ANY),
            pl.BlockSpec((1, d), lambda i, j: (0, 0)),
            pl.BlockSpec((None, 1, d), lambda i, j: (i // per_batch, 0, 0)),
            pl.BlockSpec((None, 1, d), lambda i, j: (i // per_batch, 0, 0)),
            pl.BlockSpec(memory_space=pl.ANY),
        ],
        out_specs=[
            pl.BlockSpec((tm, tn), lambda i, j: (i, j)),
            pl.BlockSpec((tm, LANES), lambda i, j: (i, 0)),
        ],
        out_shape=[
            jax.ShapeDtypeStruct((m, n), BF16),
            jax.ShapeDtypeStruct((m, LANES), F32),
        ],
        scratch_shapes=[
            pltpu.VMEM((tm, d), F32),
            pltpu.VMEM((tm, d), BF16),
            pltpu.VMEM((n // tn, tn, d), BF16),
            pltpu.VMEM((LANES, d), BF16),
            pltpu.VMEM((2, tn // 2, d), F32),
            pltpu.VMEM((n_ba, d), F32),
            pltpu.SemaphoreType.DMA(()),
            pltpu.SemaphoreType.DMA((2,)),
            pltpu.SemaphoreType.DMA(()),
        ],
        compiler_params=_params("inproj", 2),
        name="inproj",
    )(x2, norm_g, sc, sh, wt)


def _bmm(a, b):
    return jnp.einsum("hmk,hkn->hmn", a.astype(BF16), b.astype(BF16), preferred_element_type=F32)


def _bmm_nt(a, b):
    return jnp.einsum("hmk,hnk->hmn", a.astype(BF16), b.astype(BF16), preferred_element_type=F32)


def _unit_lower_inverse(a):
    row = lax.broadcasted_iota(jnp.int32, a.shape[1:], 0)
    col = lax.broadcasted_iota(jnp.int32, a.shape[1:], 1)
    apart = row ^ col
    eye = jnp.where(row == col, 1.0, 0.0).astype(F32)
    t = jnp.where(apart < 2, eye - a, 0.0)
    s = 2
    while s < CHUNK:
        coupling = jnp.where((apart >= s) & (apart < 2 * s), a, 0.0)
        t = t - _bmm(t, _bmm(coupling, t))
        s *= 2
    return t


def _lane_sums(x):
    h, rows, width = x.shape
    ones = jnp.ones((width, width), BF16)
    return _dot(x.reshape(h * rows, width).astype(BF16), ones).reshape(h, rows, width)


def _causal_conv_silu(win, cw, k_w):
    assert k_w == 4
    tiled = (win.shape[0] // SUBLANES, SUBLANES, win.shape[1])

    def pair(x, x1, j):
        return (x.reshape(tiled) * cw[j][None] + x1.reshape(tiled) * cw[j - 1][None]).reshape(win.shape)

    win1 = pltpu.roll(win, 1, 0)
    acc = pair(win, win1, 3) + pltpu.roll(pair(win, win1, 1), 2, 0)
    return _silu(acc[SUBLANES:, :])


def _gdn_kernel(q_ref, k_ref, v_ref, z_ref, ba_ref, cw_ref, hp_ref, o_ref,
                s_ref, tail_ref, wq_ref, u_ref, ik_ref, dec_ref):
    nb, tb = q_ref.shape[0], q_ref.shape[1]
    kd = N_HEADS * HEAD_D
    nbh = nb * N_HEADS

    @pl.when(pl.program_id(0) == 0)
    def _():
        s_ref[...] = jnp.zeros(s_ref.shape, F32)
        tail_ref[...] = jnp.zeros(tail_ref.shape, F32)

    row = lax.broadcasted_iota(jnp.int32, (CHUNK, CHUNK), 0)
    col = lax.broadcasted_iota(jnp.int32, (CHUNK, CHUNK), 1)
    causal = row >= col
    strict = row > col
    tril = jnp.where(causal, 1.0, 0.0).astype(F32)
    a_log = hp_ref[0:1, :]
    dt_bias = hp_ref[1:2, :]
    onorm_g = hp_ref[2:3, :]
    zeros_half = jnp.zeros((CHUNK, HEAD_D), F32)

    def precompute(cp, carry):
        qs, ks, vs, betas, gcs, grs, gls = [], [], [], [], [], [], []
        for sub in range(PRE_CHUNKS):
            base = pl.multiple_of((cp * PRE_CHUNKS + sub) * CHUNK, CHUNK)
            prev = pl.multiple_of(base - BF16_ROWS, BF16_ROWS)
            for b in range(nb):
                ba = ba_ref[b, pl.ds(base, CHUNK), :]
                beta_all = _sigmoid(ba)
                g_all = -jnp.exp(a_log) * _softplus(ba + dt_bias)
                gcum = _dot_hi(tril, g_all)
                gcum_t = jnp.concatenate([gcum, gcum], axis=0).T
                for h in range(N_HEADS):
                    lo, hi = h * HEAD_D, (h + 1) * HEAD_D

                    def conv(ref, off):
                        cur = ref[b, pl.ds(base, CHUNK), lo:hi].astype(F32)
                        if sub == 0:
                            before = tail_ref[b, :, off + lo:off + hi]
                        else:
                            before = ref[b, pl.ds(prev, BF16_ROWS), lo:hi].astype(F32)[SUBLANES:]
                        win = jnp.concatenate([before, cur], axis=0)
                        return _causal_conv_silu(win, cw_ref[:, :, off + lo:off + hi], QKV_CONV)

                    qs.append(conv(q_ref, 0))
                    ks.append(conv(k_ref, kd))
                    vs.append(conv(v_ref, 2 * kd))
                    betas.append(beta_all[:, h:h + 1])
                    gcs.append(gcum[:, N_HEADS + h:N_HEADS + h + 1])
                    grs.append(gcum_t[N_HEADS + h:N_HEADS + h + 1, 0:CHUNK])
                    gls.append(gcum[CHUNK - 1:CHUNK, N_HEADS + h:N_HEADS + h + 1])
        q, k, v = jnp.stack(qs), jnp.stack(ks), jnp.stack(vs)
        beta, gc, gr, gl = jnp.stack(betas), jnp.stack(gcs), jnp.stack(grs), jnp.stack(gls)
        qn = q * (lax.rsqrt(_lane_sums(q * q) + EPS) * (HEAD_D ** -0.5))
        kn = k * lax.rsqrt(_lane_sums(k * k) + EPS)
        decay = jnp.where(causal, jnp.exp(jnp.where(causal, gc - gr, 0.0)), 0.0)
        kb = kn * beta
        e_gc = jnp.exp(gc)
        kq = _bmm_nt(jnp.concatenate([kb, qn], axis=1), kn)
        a = jnp.where(strict, kq[:, :CHUNK] * decay, 0.0)
        intra = kq[:, CHUNK:] * decay
        uw = _bmm(_unit_lower_inverse(a), jnp.concatenate([v * beta, kb * e_gc], axis=2))
        wq = jnp.concatenate([uw[:, :, HEAD_D:], qn * e_gc], axis=1).astype(BF16)
        k_dec = kn * jnp.exp(gl - gc)
        k_dec_t = jnp.stack([jnp.concatenate([k_dec[i], zeros_half], axis=0).T[:, :CHUNK]
                             for i in range(PRE_CHUNKS * nbh)])
        ik = jnp.concatenate([intra, k_dec_t], axis=1).astype(BF16)
        dec = jnp.broadcast_to(jnp.exp(gl), (PRE_CHUNKS * nbh, 1, HEAD_D))
        for sub in range(PRE_CHUNKS):
            c = cp * PRE_CHUNKS + sub
            rows = slice(sub * nbh, (sub + 1) * nbh)
            u_ref[c] = uw[rows, :, :HEAD_D]
            wq_ref[c] = wq[rows]
            ik_ref[c] = ik[rows]
            dec_ref[c] = dec[rows]

        last = pl.multiple_of((cp + 1) * PRE_CHUNKS * CHUNK - BF16_ROWS, BF16_ROWS)
        for b in range(nb):
            for j, ref in enumerate((q_ref, k_ref, v_ref)):
                rows = ref[b, pl.ds(last, BF16_ROWS), :].astype(F32)
                tail_ref[b, :, j * kd:(j + 1) * kd] = rows[BF16_ROWS - SUBLANES:, :]
        return carry

    def recur(c, carry):
        base = pl.multiple_of(c * CHUNK, CHUNK)
        state = s_ref[...]
        ws = _bmm(wq_ref[c], state)
        v_new = u_ref[c] - ws[:, :CHUNK]
        r = _bmm(ik_ref[c], v_new)
        o = ws[:, CHUNK:] + r[:, :CHUNK]
        s_ref[...] = state * dec_ref[c] + r[:, CHUNK:]
        on = o * lax.rsqrt(jnp.mean(o * o, axis=-1, keepdims=True) + EPS) * onorm_g
        for b in range(nb):
            for h in range(N_HEADS):
                lo, hi = h * HEAD_D, (h + 1) * HEAD_D
                z = z_ref[b, pl.ds(base, CHUNK), lo:hi].astype(F32)
                o_ref[b, pl.ds(base, CHUNK), lo:hi] = (on[b * N_HEADS + h] * _silu(z)).astype(BF16)
        return carry

    lax.fori_loop(0, tb // (PRE_CHUNKS * CHUNK), precompute, 0)
    lax.fori_loop(0, tb // CHUNK, recur, 0)


def _gdn(proj, ba, conv_w, head_params, batch, seq, tb):
    kd = N_HEADS * HEAD_D
    nc = tb // CHUNK
    nbh = batch * N_HEADS
    proj3 = proj.reshape(batch, seq, proj.shape[1])
    ba3 = ba.reshape(batch, seq, LANES)

    def col(j):
        return pl.BlockSpec((batch, tb, kd), lambda t: (0, t, j))

    out = pl.pallas_call(
        _gdn_kernel,
        grid=(seq // tb,),
        in_specs=[
            col(0), col(1), col(2), col(3),
            pl.BlockSpec((batch, tb, LANES), lambda t: (0, t, 0)),
            pl.BlockSpec((QKV_CONV, SUBLANES, 3 * kd), lambda t: (0, 0, 0)),
            pl.BlockSpec((SUBLANES, LANES), lambda t: (0, 0)),
        ],
        out_specs=pl.BlockSpec((batch, tb, kd), lambda t: (0, t, 0)),
        out_shape=jax.ShapeDtypeStruct((batch, seq, kd), BF16),
        scratch_shapes=[
            pltpu.VMEM((nbh, HEAD_D, HEAD_D), F32),
            pltpu.VMEM((batch, SUBLANES, 3 * kd), F32),
            pltpu.VMEM((nc, nbh, 2 * CHUNK, HEAD_D), BF16),
            pltpu.VMEM((nc, nbh, CHUNK, HEAD_D), F32),
            pltpu.VMEM((nc, nbh, CHUNK + HEAD_D, CHUNK), BF16),
            pltpu.VMEM((nc, nbh, 1, HEAD_D), F32),
        ],
        compiler_params=_params("gdn"),
        name="gdn",
    )(proj3, proj3, proj3, proj3, ba3, conv_w, head_params)
    return out.reshape(batch * seq, kd)


def _post_kernel(x_ref, og_ref, sb_ref, sc_ref, sx_ref, ga_ref, gb_ref, mod_ref, n2g_ref, cw_ref,
                 wpa_ref, wpb_ref, wout_ref, wrh_ref, wrl_ref, br_ref,
                 x1_ref, h2_ref, route_ref, cnt_ref, win_ref, run_ref, *, per_batch):
    tm = x_ref.shape[0]
    i = pl.program_id(0)

    @pl.when(i == 0)
    def _():
        run_ref[...] = jnp.zeros(run_ref.shape, F32)

    @pl.when(i % per_batch == 0)
    def _():
        win_ref[...] = jnp.zeros(win_ref.shape, F32)

    assert SC_CONV == 3
    prod = sc_ref[...].astype(F32) * sx_ref[...].astype(F32)
    win = jnp.concatenate([win_ref[...], prod], axis=0)
    win_ref[...] = prod[tm - SUBLANES:, :]
    tiled = (win.shape[0] // SUBLANES, SUBLANES, win.shape[1])
    win1 = pltpu.roll(win, 1, 0)
    conv = (win.reshape(tiled) * cw_ref[2][None] + win1.reshape(tiled) * cw_ref[1][None]
            + pltpu.roll(win1, 1, 0).reshape(tiled) * cw_ref[0][None]).reshape(win.shape)[SUBLANES:, :]
    y_b = _dot((sb_ref[...].astype(F32) * conv).astype(BF16), wpb_ref[...])
    y_a = _dot(og_ref[...], wpa_ref[...])
    merged = _sigmoid(ga_ref[...].astype(F32)) * y_a + _sigmoid(gb_ref[...].astype(F32)) * y_b
    mix = _dot(merged.astype(BF16), wout_ref[...])
    x1 = x_ref[...] + mod_ref[0:1, :] * mix
    x1_ref[...] = x1

    y = x1 * lax.rsqrt(jnp.mean(x1 * x1, axis=-1, keepdims=True) + EPS)
    h2 = (y * n2g_ref[...]) * (1.0 + mod_ref[1:2, :]) + mod_ref[2:3, :]
    _write_rows(h2_ref, h2)

    lg = _dot_split(h2, wrh_ref[...], wrl_ref[...]) + br_ref[...]
    lane = lax.broadcasted_iota(jnp.int32, lg.shape, 1).astype(F32)
    neg = jnp.float32(-jnp.inf)
    big = jnp.float32(2 * LANES)

    def first_max(mask):
        vmax = jnp.max(jnp.where(mask, lg, neg), axis=-1, keepdims=True)
        idx = jnp.min(jnp.where(mask & (lg == vmax), lane, big), axis=-1, keepdims=True)
        return vmax, idx

    gmask = lane < N_GROUPS
    g_max, g_sel = first_max(gmask)
    p_group = 1.0 / jnp.sum(jnp.where(gmask, jnp.exp(lg - g_max), 0.0), axis=-1, keepdims=True)
    e_lo = N_GROUPS + EXPERTS_PER_GROUP * g_sel
    emask = (lane >= e_lo) & (lane < e_lo + EXPERTS_PER_GROUP)
    v1, i1 = first_max(emask)
    v2, i2 = first_max(emask & (lane != i1))
    ex = jnp.exp(v2 - v1)
    w1 = p_group * (1.0 / (1.0 + ex))
    w2 = p_group * (ex / (1.0 + ex))
    e1 = i1 - N_GROUPS
    e2 = i2 - N_GROUPS

    onehot = jnp.where((lane == e1) | (lane == e2), 1.0, 0.0).astype(F32)
    row = lax.broadcasted_iota(jnp.int32, (tm, tm), 0)
    col = lax.broadcasted_iota(jnp.int32, (tm, tm), 1)
    before = jnp.where(row > col, 1.0, 0.0).astype(BF16)
    seen = _dot(before, onehot.astype(BF16)) + run_ref[0:1, :]
    r1 = jnp.sum(jnp.where(lane == e1, seen, 0.0), axis=-1, keepdims=True)
    r2 = jnp.sum(jnp.where(lane == e2, seen, 0.0), axis=-1, keepdims=True)
    run_ref[0:1, :] = run_ref[0:1, :] + jnp.sum(onehot, axis=0, keepdims=True)
    cnt_ref[...] = jnp.broadcast_to(run_ref[0:1, :], cnt_ref.shape)

    out = jnp.where(lane == 0, e1, 0.0)
    out = jnp.where(lane == 1, e2, out)
    out = jnp.where(lane == 2, w1, out)
    out = jnp.where(lane == 3, w2, out)
    out = jnp.where(lane == 4, r1, out)
    out = jnp.where(lane == 5, r2, out)
    route_ref[...] = out


def _post(x2, og, proj, mod, n2g, conv_w, wpa, wpb, wout, w_route_hi, w_route_lo, b_route, seq, tm):
    m, d = x2.shape
    assert d == SUBLANES * LANES
    per_batch = seq // tm

    def rows(j):
        return pl.BlockSpec((tm, d), lambda i: (i, j))

    def whole(shape):
        return pl.BlockSpec(shape, lambda i: tuple(0 for _ in shape))

    return pl.pallas_call(
        functools.partial(_post_kernel, per_batch=per_batch),
        grid=(m // tm,),
        in_specs=[
            rows(0), rows(0), rows(4), rows(5), rows(6), rows(7), rows(8),
            pl.BlockSpec((None, SUBLANES, d), lambda i: (i // per_batch, 0, 0)),
            whole((1, d)), whole((SC_CONV, SUBLANES, d)),
            whole((d, d)), whole((d, d)), whole((d, d)),
            whole((d, LANES)), whole((d, LANES)), whole((1, LANES)),
        ],
        out_specs=[
            rows(0), pl.BlockSpec((tm * SUBLANES, LANES), lambda i: (i, 0)),
            pl.BlockSpec((tm, LANES), lambda i: (i, 0)),
            pl.BlockSpec((SUBLANES, LANES), lambda i: (0, 0)),
        ],
        out_shape=[
            jax.ShapeDtypeStruct((m, d), F32),
            jax.ShapeDtypeStruct((m * SUBLANES, LANES), F32),
            jax.ShapeDtypeStruct((m, LANES), F32),
            jax.ShapeDtypeStruct((SUBLANES, LANES), F32),
        ],
        scratch_shapes=[
            pltpu.VMEM((SUBLANES, d), F32),
            pltpu.VMEM((SUBLANES, LANES), F32),
        ],
        compiler_params=_params("post"),
        name="post",
    )(x2, og, proj, proj, proj, proj, proj, mod, n2g, conv_w, wpa, wpb, wout, w_route_hi, w_route_lo,
      b_route)


def _plan_kernel(route_ref, cnt_ref, d_ref):
    bm = EXPERT_BLOCK
    sizes = cnt_ref[...]
    padded = jnp.floor((sizes + (bm - 1.0)) * (1.0 / bm)) * bm
    lane_i = lax.broadcasted_iota(jnp.int32, sizes.shape, 1)
    incl = padded
    s = 1
    while s < LANES:
        incl = incl + jnp.where(lane_i >= s, pltpu.roll(incl, s, 1), 0.0)
        s *= 2
    start = (incl - padded)[0:1, :]
    r = route_ref[...]
    lane = lax.broadcasted_iota(jnp.int32, r.shape, 1).astype(F32)
    d1 = jnp.sum(jnp.where(lane == r[:, 0:1], start, 0.0), axis=-1, keepdims=True) + r[:, 4:5]
    d2 = jnp.sum(jnp.where(lane == r[:, 1:2], start, 0.0), axis=-1, keepdims=True) + r[:, 5:6]
    out = jnp.where(lane == 0.0, d1, jnp.where(lane == 1.0, d2, 0.0))
    d_ref[...] = out.T[0:SUBLANES, :].astype(jnp.int32)


def _plan(route, counts, tm):
    m = route.shape[0]
    return pl.pallas_call(
        _plan_kernel,
        grid=(m // tm,),
        in_specs=[
            pl.BlockSpec((tm, LANES), lambda i: (i, 0)),
            pl.BlockSpec((SUBLANES, LANES), lambda i: (0, 0)),
        ],
        out_specs=pl.BlockSpec((SUBLANES, tm), lambda i: (0, i)),
        out_shape=jax.ShapeDtypeStruct((SUBLANES, m), jnp.int32),
        compiler_params=_params("plan"),
        name="plan",
    )(route, counts)


def _row_copy(src_ref, src_row, dst_ref, dst_row, sem):
    return pltpu.make_async_copy(_row_tile(src_ref, src_row), _row_tile(dst_ref, dst_row), sem)


def _rows_copy(src_ref, dst_ref, dst_row, n, sem):
    return pltpu.make_async_copy(src_ref, dst_ref.at[pl.ds(dst_row * SUBLANES, n * SUBLANES), :], sem)


def _dispatch_kernel(d1_ref, d2_ref, pad_ref, h2_ref, xs_ref, zero_ref, sem):
    tm = h2_ref.shape[0] // SUBLANES
    bm = zero_ref.shape[0] // SUBLANES
    i = pl.program_id(0)
    t0 = i * tm
    n_blocks = xs_ref.shape[0] // (bm * SUBLANES)

    def zero_rows(row, n):
        dst = xs_ref.at[pl.ds(pl.multiple_of(row * SUBLANES, SUBLANES), n * SUBLANES), :]
        return pltpu.make_async_copy(zero_ref.at[pl.ds(0, n * SUBLANES), :], dst, sem.at[2])

    def zero_fill(act):
        def expert(e, carry):
            row, cnt = pad_ref[e], pad_ref[N_EXPERTS + e]
            n = bm // 2
            while n >= 1:
                @pl.when((cnt & n) != 0)
                def _():
                    act(zero_rows(row + (cnt & (-2 * n)), n))
                n //= 2
            return carry

        lax.fori_loop(0, N_EXPERTS, expert, 0)
        lax.fori_loop(pad_ref[2 * N_EXPERTS], n_blocks, lambda j, c: (act(zero_rows(j * bm, bm)), c)[1], 0)

    @pl.when(i == 0)
    def _():
        zero_ref[...] = jnp.zeros(zero_ref.shape, F32)
        zero_fill(lambda copy: copy.start())

    def start(r, carry):
        _row_copy(h2_ref, r, xs_ref, d1_ref[t0 + r], sem.at[0]).start(priority=0)
        _row_copy(h2_ref, r, xs_ref, d2_ref[t0 + r], sem.at[1]).start(priority=1)
        return carry

    lax.fori_loop(0, tm, start, 0, unroll=DMA_UNROLL)
    _rows_copy(h2_ref, xs_ref, 0, tm, sem.at[0]).wait()
    _rows_copy(h2_ref, xs_ref, 0, tm, sem.at[1]).wait()

    @pl.when(i == pl.num_programs(0) - 1)
    def _():
        zero_fill(lambda copy: copy.wait())


def _dispatch(d1, d2, pad_info, h2, cap, tm):
    m = h2.shape[0] // SUBLANES
    return pl.pallas_call(
        _dispatch_kernel,
        grid_spec=pltpu.PrefetchScalarGridSpec(
            num_scalar_prefetch=3,
            grid=(m // tm,),
            in_specs=[pl.BlockSpec((tm * SUBLANES, LANES), lambda i, d1, d2, tl: (i, 0))],
            out_specs=pl.BlockSpec(memory_space=pl.ANY),
            scratch_shapes=[pltpu.VMEM((EXPERT_BLOCK * SUBLANES, LANES), F32),
                            pltpu.SemaphoreType.DMA((3,))],
        ),
        out_shape=jax.ShapeDtypeStruct((cap * SUBLANES, LANES), F32),
        compiler_params=_params("dispatch"),
        name="dispatch",
    )(d1, d2, pad_info, h2)


def _expert_kernel(be_ref, act_ref, nxt_ref, xs_hbm, w1_hbm, w3_hbm, w2_hbm, y_ref, w1b_ref, w3b_ref, w2b_ref,
                   w1s_ref, w3s_ref, w2s_ref, xbuf_ref, sem, sem_x):
    b = pl.program_id(0)
    n_active = act_ref[pl.num_programs(0)]
    weights = ((w1_hbm, w1s_ref, w1b_ref), (w3_hbm, w3s_ref, w3b_ref), (w2_hbm, w2s_ref, w2b_ref))
    n_buf, rows = xbuf_ref.shape[0], xbuf_ref.shape[1]
    ahead = n_buf - 1

    def weight_copy(k, e):
        w_hbm, ws_ref, _ = weights[k]
        return pltpu.make_async_copy(w_hbm.at[e], ws_ref, sem.at[k])

    def x_copy(blk):
        slot = lax.rem(blk, n_buf)
        src = xs_hbm.at[pl.ds(pl.multiple_of(blk * rows, rows), rows), :]
        return pltpu.make_async_copy(src, xbuf_ref.at[slot], sem_x.at[slot])

    @pl.when(act_ref[b] > 0)
    def _():
        @pl.when(b == 0)
        def _():
            for k in range(len(weights)):
                weight_copy(k, be_ref[b]).start()
            for a in range(ahead):
                @pl.when(a < n_active)
                def _():
                    x_copy(jnp.int32(a)).start()

        @pl.when(b + ahead < n_active)
        def _():
            x_copy(b + ahead).start()

        @pl.when((b == 0) | (be_ref[b] != be_ref[jnp.maximum(b - 1, 0)]))
        def _():
            for k, (_, ws_ref, wb_ref) in enumerate(weights):
                weight_copy(k, be_ref[b]).wait()
                wb_ref[...] = ws_ref[...].astype(BF16)

                @pl.when(nxt_ref[b] >= 0)
                def _():
                    weight_copy(k, nxt_ref[b]).start()

        x_copy(b).wait()
        xb = _read_rows(xbuf_ref.at[lax.rem(b, n_buf)], rows // SUBLANES).astype(BF16)
        hid = _silu(_dot(xb, w1b_ref[...])) * _dot(xb, w3b_ref[...])
        _write_rows(y_ref, _dot(hid.astype(BF16), w2b_ref[...]))

    @pl.when(act_ref[b] == 0)
    def _():
        y_ref[...] = jnp.zeros(y_ref.shape, F32)


def _experts(block_expert, block_active, block_next, xs, w1, w3, w2):
    cap = xs.shape[0] // SUBLANES
    d, de = w1.shape[1], w1.shape[2]
    bm = EXPERT_BLOCK

    return pl.pallas_call(
        _expert_kernel,
        grid_spec=pltpu.PrefetchScalarGridSpec(
            num_scalar_prefetch=3,
            grid=(cap // bm,),
            in_specs=[
                pl.BlockSpec(memory_space=pl.ANY),
                pl.BlockSpec(memory_space=pl.ANY),
                pl.BlockSpec(memory_space=pl.ANY),
                pl.BlockSpec(memory_space=pl.ANY),
            ],
            out_specs=pl.BlockSpec((bm * SUBLANES, LANES), lambda b, be, act, nxt: (b, 0)),
            scratch_shapes=[pltpu.VMEM((d, de), BF16), pltpu.VMEM((d, de), BF16), pltpu.VMEM((de, d), BF16),
                            pltpu.VMEM((d, de), F32), pltpu.VMEM((d, de), F32), pltpu.VMEM((de, d), F32),
                            pltpu.VMEM((EXPERT_X_BUFFERS, bm * SUBLANES, LANES), F32),
                            pltpu.SemaphoreType.DMA((3,)), pltpu.SemaphoreType.DMA((EXPERT_X_BUFFERS,))],
        ),
        out_shape=jax.ShapeDtypeStruct((cap * SUBLANES, LANES), F32),
        compiler_params=_params("experts"),
        name="experts",
    )(block_expert, block_active, block_next, xs, w1, w3, w2)


def _final_kernel(d1_ref, d2_ref, x1_ref, route_ref, gt_ref, nfg_ref, ys_ref, o_ref, buf_ref, sem):
    tm = x1_ref.shape[0]
    i = pl.program_id(0)

    def gather(tile, s):
        t0 = tile * tm

        def start(r, carry):
            _row_copy(ys_ref, d1_ref[t0 + r], buf_ref.at[s, 0], r, sem.at[s, 0]).start(priority=0)
            _row_copy(ys_ref, d2_ref[t0 + r], buf_ref.at[s, 1], r, sem.at[s, 1]).start(priority=1)
            return carry

        lax.fori_loop(0, tm, start, 0, unroll=DMA_UNROLL)

    def wait_rows(s):
        for k in range(2):
            pltpu.make_async_copy(ys_ref.at[pl.ds(0, tm * SUBLANES), :], buf_ref.at[s, k],
                                  sem.at[s, k]).wait()

    def combine(cur, nxt):
        wait_rows(cur)
        last = pl.num_programs(0) - 1
        t0 = jnp.minimum(i + 1, last) * tm
        for r in range(tm):
            _row_copy(ys_ref, d1_ref[t0 + r], buf_ref.at[nxt, 0], r, sem.at[nxt, 0]).start(priority=0)
            _row_copy(ys_ref, d2_ref[t0 + r], buf_ref.at[nxt, 1], r, sem.at[nxt, 1]).start(priority=1)
        route = route_ref[...]
        moe = (_read_rows(buf_ref.at[cur, 0], tm) * route[:, 2:3]
               + _read_rows(buf_ref.at[cur, 1], tm) * route[:, 3:4])
        x2 = x1_ref[...] + gt_ref[...] * moe
        y = x2 * lax.rsqrt(jnp.mean(x2 * x2, axis=-1, keepdims=True) + EPS)
        o_ref[...] = y * nfg_ref[...]

        @pl.when(i == last)
        def _():
            wait_rows(nxt)

    @pl.when(i == 0)
    def _():
        gather(0, 0)

    @pl.when(i % 2 == 0)
    def _():
        combine(0, 1)

    @pl.when(i % 2 == 1)
    def _():
        combine(1, 0)


def _final(d1, d2, x1, route, gt2, nfg, ys, seq, tm):
    m, d = x1.shape
    per_batch = seq // tm
    return pl.pallas_call(
        _final_kernel,
        grid_spec=pltpu.PrefetchScalarGridSpec(
            num_scalar_prefetch=2,
            grid=(m // tm,),
            in_specs=[
                pl.BlockSpec((tm, d), lambda i, d1, d2: (i, 0)),
                pl.BlockSpec((tm, LANES), lambda i, d1, d2: (i, 0)),
                pl.BlockSpec((None, 1, d), lambda i, d1, d2: (i // per_batch, 0, 0)),
                pl.BlockSpec((1, d), lambda i, d1, d2: (0, 0)),
                pl.BlockSpec(memory_space=pl.ANY),
            ],
            out_specs=pl.BlockSpec((tm, d), lambda i, d1, d2: (i, 0)),
            scratch_shapes=[pltpu.VMEM((2, 2, tm * SUBLANES, LANES), F32),
                            pltpu.SemaphoreType.DMA((2, 2))],
        ),
        out_shape=jax.ShapeDtypeStruct((m, d), F32),
        compiler_params=_params("final"),
        name="final",
    )(d1, d2, x1, route, gt2, nfg, ys)


def _tile(n, pref):
    t = min(n, pref)
    assert n % t == 0
    return t


def kernel(x, c, w_ada, b_ada, norm1_g, w_in, conv_qkv_w, a_log, dt_bias, onorm_g, w_proj_a,
           conv_sc_w, w_proj_b, w_out, norm2_g, w_group, b_group, w_expert, b_expert, w1, w3, w2,
           normf_g):
    batch, seq, d = x.shape
    depth = w_ada.shape[0]
    m = batch * seq
    kd = N_HEADS * HEAD_D
    assert d == kd and seq % CHUNK == 0 and batch <= SUBLANES
    assert depth == 1, "the last stage applies the final rmsnorm: one layer only"
    x2 = x.reshape(m, d)

    for l in range(depth):
        mod = _ada(c, w_ada[l], b_ada[l][None, :])[:batch]
        sh1, sc1, gt1, sh2, sc2, gt2 = [mod[:, None, j * d:(j + 1) * d] for j in range(6)]

        w = w_in[l]
        o_ba = 3 * kd + kd
        proj, ba = _inproj(x2, norm1_g[l][None, :], sc1, sh1, jnp.swapaxes(w, 0, 1), o_ba // d, 2 * N_HEADS,
                           seq, _tile(seq, INPROJ_ROWS))

        head_params = jnp.zeros((SUBLANES, LANES), F32)
        head_params = head_params.at[0, N_HEADS:2 * N_HEADS].set(a_log[l])
        head_params = head_params.at[1, N_HEADS:2 * N_HEADS].set(dt_bias[l])
        head_params = head_params.at[2, :].set(onorm_g[l])
        conv_taps = jnp.broadcast_to(conv_qkv_w[l][:, None, :], (QKV_CONV, SUBLANES, 3 * kd))
        og = _gdn(proj, ba, conv_taps, head_params, batch, seq, _tile(seq, GDN_ROWS))

        mod_post = jnp.zeros((batch, SUBLANES, d), F32)
        mod_post = mod_post.at[:, 0:1].set(gt1).at[:, 1:2].set(sc2).at[:, 2:3].set(sh2)
        w_route = jnp.zeros((d, LANES), F32)
        w_route = w_route.at[:, :N_GROUPS].set(w_group[l]).at[:, N_GROUPS:N_GROUPS + N_EXPERTS].set(w_expert[l])
        w_route_hi = w_route.astype(BF16)
        w_route_lo = (w_route - w_route_hi.astype(F32)).astype(BF16)
        b_route = jnp.zeros((1, LANES), F32)
        b_route = b_route.at[0, :N_GROUPS].set(b_group[l]).at[0, N_GROUPS:N_GROUPS + N_EXPERTS].set(b_expert[l])
        x1, h2, route, counts = _post(
            x2, og, proj, mod_post, norm2_g[l][None, :],
            jnp.broadcast_to(conv_sc_w[l][:, None, :], (SC_CONV, SUBLANES, d)),
            w_proj_a[l].astype(BF16), w_proj_b[l].astype(BF16), w_out[l].astype(BF16),
            w_route_hi, w_route_lo, b_route, seq, _tile(seq, POST_ROWS))

        bm = EXPERT_BLOCK
        n_blocks = (2 * m) // bm + N_EXPERTS
        sizes = counts[0, :N_EXPERTS].astype(jnp.int32)
        padded = ((sizes + bm - 1) // bm) * bm
        pad_end = jnp.cumsum(padded)
        pad_info = jnp.concatenate([pad_end - padded + sizes, padded - sizes,
                                    pad_end[-1:] // bm]).astype(jnp.int32)
        block_row = jnp.arange(n_blocks, dtype=jnp.int32) * bm
        block_expert = jnp.minimum(jnp.sum(block_row[:, None] >= pad_end[None, :], axis=1),
                                   N_EXPERTS - 1).astype(jnp.int32)
        block_active = jnp.concatenate([(block_row < pad_end[-1]).astype(jnp.int32),
                                        (pad_end[-1:] // bm).astype(jnp.int32)])
        expert_id = jnp.arange(N_EXPERTS, dtype=jnp.int32)
        later = (expert_id[None, :] > expert_id[:, None]) & (padded[None, :] > 0)
        next_expert = jnp.min(jnp.where(later, expert_id[None, :], N_EXPERTS), axis=1)
        block_next = jnp.where(next_expert < N_EXPERTS, next_expert, -1)[block_expert].astype(jnp.int32)
        dest = _plan(route, counts, _tile(seq, INPROJ_ROWS))
        d1, d2 = dest[0], dest[1]

        xs = _dispatch(d1, d2, pad_info, h2, n_blocks * bm, _tile(seq, DISPATCH_ROWS))
        ys = _experts(block_expert, block_active, block_next, xs, w1[l], w3[l], w2[l])
        nfg = normf_g[None, :]
        x2 = _final(d1, d2, x1, route, gt2, nfg, ys, seq, _tile(seq, FINAL_ROWS))
    return x2.reshape(batch, seq, d)
```

```python
import functools

import jax
import jax.numpy as jnp
from jax import lax
from jax.experimental import pallas as pl
from jax.experimental.pallas import tpu as pltpu

F32 = jnp.float32
BF16 = jnp.bfloat16
HIGHEST = lax.Precision.HIGHEST

N_HEADS = 8
HEAD_D = 128
CHUNK = 64
QKV_CONV = 4
SC_CONV = 3
N_GROUPS = 4
EXPERTS_PER_GROUP = 8
N_EXPERTS = N_GROUPS * EXPERTS_PER_GROUP
EPS = 1e-6

LANES = 128
SUBLANES = 8
BF16_ROWS = 16
EXPERT_BLOCK = 512
INPROJ_ROWS = 2048
GDN_ROWS = 256
PRE_CHUNKS = 2
POST_ROWS = 512
EXPERT_X_BUFFERS = 3
DISPATCH_ROWS = 2048
FINAL_ROWS = 512
DMA_UNROLL = 8
MIB = 1024 * 1024
VMEM_MIB = {"ada": 24, "inproj": 56, "gdn": 48, "post": 56, "plan": 16, "dispatch": 24, "experts": 40,
            "final": 32}


def _sigmoid(x):
    return 0.5 + 0.5 * jnp.tanh(0.5 * x)


def _silu(x):
    half = 0.5 * x
    return half + half * jnp.tanh(half)


def _softplus(x):
    return jnp.maximum(x, 0.0) + jnp.log(1.0 + jnp.exp(-jnp.abs(x)))


def _dot(a, b):
    return jnp.dot(a, b, preferred_element_type=F32)


def _dot_nt(a, b):
    return lax.dot_general(a, b, (((1,), (1,)), ((), ())), preferred_element_type=F32)


def _dot_hi(a, b):
    return jnp.dot(a, b, preferred_element_type=F32, precision=HIGHEST)


def _dot_split(a, b_hi, b_lo):
    a_hi = a.astype(BF16)
    a_lo = (a - a_hi.astype(F32)).astype(BF16)
    return _dot(a_hi, b_hi) + (_dot(a_lo, b_hi) + _dot(a_hi, b_lo))


def _read_rows(ref, rows):
    return jnp.concatenate([ref[pl.ds(j, rows, stride=SUBLANES), :] for j in range(SUBLANES)], axis=1)


def _write_rows(ref, value):
    rows = value.shape[0]
    for j in range(SUBLANES):
        ref[pl.ds(j, rows, stride=SUBLANES), :] = value[:, j * LANES:(j + 1) * LANES]


def _row_tile(ref, row):
    if not isinstance(row, int):
        row = pl.multiple_of(row * SUBLANES, SUBLANES)
    else:
        row = row * SUBLANES
    return ref.at[pl.ds(row, SUBLANES), :]


def _params(call, n_grid_axes=1):
    return pltpu.CompilerParams(dimension_semantics=("arbitrary",) * n_grid_axes,
                                vmem_limit_bytes=VMEM_MIB[call] * MIB)


def _ada_kernel(cb_ref, w_ref, b_ref, o_ref, s_ref):
    batch, d, _ = cb_ref.shape
    groups = w_ref.shape[1] // LANES

    @pl.when(pl.program_id(0) == 0)
    def _():
        s_ref[...] = _silu(cb_ref[...])

    def body(kc, accs):
        k0 = pl.multiple_of(kc * SUBLANES, SUBLANES)
        w = w_ref[pl.ds(k0, SUBLANES), :]
        out = []
        for b in range(batch):
            s = s_ref[b, pl.ds(k0, SUBLANES), :]
            out += [accs[b * groups + g] + w[:, g * LANES:(g + 1) * LANES] * s for g in range(groups)]
        return tuple(out)

    zero = jnp.zeros((SUBLANES, LANES), F32)
    accs = lax.fori_loop(0, d // SUBLANES, body, (zero,) * (batch * groups), unroll=4)
    o_ref[...] = jnp.zeros(o_ref.shape, F32)
    for b in range(batch):
        for g in range(groups):
            cols = slice(g * LANES, (g + 1) * LANES)
            o_ref[b:b + 1, cols] = jnp.sum(accs[b * groups + g], axis=0, keepdims=True) + b_ref[:, cols]


def _ada(c, w_ada, b_ada):
    batch, d = c.shape
    n = w_ada.shape[1]
    cb = jnp.broadcast_to(c[:, :, None], (batch, d, LANES))
    return pl.pallas_call(
        _ada_kernel,
        grid=(n // d,),
        in_specs=[
            pl.BlockSpec((batch, d, LANES), lambda j: (0, 0, 0)),
            pl.BlockSpec((d, d), lambda j: (0, j)),
            pl.BlockSpec((1, d), lambda j: (0, j)),
        ],
        out_specs=pl.BlockSpec((SUBLANES, d), lambda j: (0, j)),
        out_shape=jax.ShapeDtypeStruct((SUBLANES, n), F32),
        scratch_shapes=[pltpu.VMEM((batch, d, LANES), F32)],
        compiler_params=_params("ada"),
        name="ada",
    )(cb, w_ada, b_ada)


def _inproj_kernel(x_hbm, g_ref, sc_ref, sh_ref, wt_hbm, o_ref, ba_ref, x_ref, h_ref, w_ref, wba_ref, stage_ref,
                   stage_ba_ref, sem_x, sem, sem_ba, *, n_lead, n_ba):
    i, j = pl.program_id(0), pl.program_id(1)
    n_tiles = pl.num_programs(0)
    n_blocks, tn = w_ref.shape[0], o_ref.shape[1]
    tm, half = x_ref.shape[0], stage_ref.shape[1]

    def x_copy(tile):
        return pltpu.make_async_copy(x_hbm.at[pl.ds(pl.multiple_of(tile * tm, tm), tm)], x_ref, sem_x)

    def block_copy(jb, part):
        row = jb * tn + jnp.where(jb >= n_lead, n_ba, 0) + part * half
        return pltpu.make_async_copy(wt_hbm.at[pl.ds(pl.multiple_of(row, SUBLANES), half)],
                                     stage_ref.at[part], sem.at[part])

    def ba_copy():
        return pltpu.make_async_copy(wt_hbm.at[pl.ds(n_lead * tn, n_ba)], stage_ba_ref, sem_ba)

    @pl.when(j == 0)
    def _():
        @pl.when(i == 0)
        def _():
            x_copy(i).start()
            ba_copy().start()
            for part in range(2):
                block_copy(j, part).start()

        x_copy(i).wait()
        x = x_ref[...]
        y = x * lax.rsqrt(jnp.mean(x * x, axis=-1, keepdims=True) + EPS)
        h = (y * g_ref[...]) * (1.0 + sc_ref[...]) + sh_ref[...]
        hb = h.astype(BF16)
        h_ref[...] = hb

        @pl.when(i == 0)
        def _():
            ba_copy().wait()
            wba_ref[...] = jnp.zeros(wba_ref.shape, BF16)
            wba_ref[0:n_ba, :] = stage_ba_ref[...].astype(BF16)

        ba_ref[...] = _dot_nt(hb, wba_ref[...])

    @pl.when(jnp.logical_and(j == 1, i + 1 < n_tiles))
    def _():
        x_copy(i + 1).start()

    @pl.when(i == 0)
    def _():
        for part in range(2):
            block_copy(j, part).wait()
            w_ref[j, pl.ds(part * half, half), :] = stage_ref[part].astype(BF16)

            @pl.when(j + 1 < n_blocks)
            def _():
                block_copy(j + 1, part).start()

    rows = o_ref.shape[0] // 2
    for r in range(2):
        o_ref[pl.ds(r * rows, rows), :] = _dot_nt(h_ref[pl.ds(r * rows, rows), :], w_ref[j]).astype(BF16)


def _inproj(x2, norm_g, sc, sh, wt, n_lead, n_ba, seq, tm):
    m, d = x2.shape
    tn = d
    n = wt.shape[0] - n_ba
    assert n % tn == 0 and n_ba % (2 * SUBLANES) == 0 and n_ba <= LANES
    per_batch = seq // tm
    return pl.pallas_call(
        functools.partial(_inproj_kernel, n_lead=n_lead, n_ba=n_ba),
        grid=(m // tm, n // tn),
        in_specs=[
            pl.BlockSpec(memory_space=pl.ANY),
            pl.BlockSpec((1, d), lambda i, j: (0, 0)),
            pl.BlockSpec((None, 1, d), lambda i, j: (i // per_batch, 0, 0)),
            pl.BlockSpec((None, 1, d), lambda i, j: (i // per_batch, 0, 0)),
            pl.BlockSpec(memory_space=pl.ANY),
        ],
        out_specs=[
            pl.BlockSpec((tm, tn), lambda i, j: (i, j)),
            pl.BlockSpec((tm, LANES), lambda i, j: (i, 0)),
        ],
        out_shape=[
            jax.ShapeDtypeStruct((m, n), BF16),
            jax.ShapeDtypeStruct((m, LANES), F32),
        ],
        scratch_shapes=[
            pltpu.VMEM((tm, d), F32),
            pltpu.VMEM((tm, d), BF16),
            pltpu.VMEM((n // tn, tn, d), BF16),
            pltpu.VMEM((LANES, d), BF16),
            pltpu.VMEM((2, tn // 2, d), F32),
            pltpu.VMEM((n_ba, d), F32),
            pltpu.SemaphoreType.DMA(()),
            pltpu.SemaphoreType.DMA((2,)),
            pltpu.SemaphoreType.DMA(()),
        ],
        compiler_params=_params("inproj", 2),
        name="inproj",
    )(x2, norm_g, sc, sh, wt)


def _bmm(a, b):
    return jnp.einsum("hmk,hkn->hmn", a.astype(BF16), b.astype(BF16), preferred_element_type=F32)


def _bmm_nt(a, b):
    return jnp.einsum("hmk,hnk->hmn", a.astype(BF16), b.astype(BF16), preferred_element_type=F32)


def _unit_lower_inverse(a):
    row = lax.broadcasted_iota(jnp.int32, a.shape[1:], 0)
    col = lax.broadcasted_iota(jnp.int32, a.shape[1:], 1)
    apart = row ^ col
    eye = jnp.where(row == col, 1.0, 0.0).astype(F32)
    t = jnp.where(apart < 2, eye - a, 0.0)
    s = 2
    while s < CHUNK:
        coupling = jnp.where((apart >= s) & (apart < 2 * s), a, 0.0)
        t = t - _bmm(t, _bmm(coupling, t))
        s *= 2
    return t


def _lane_sums(x):
    h, rows, width = x.shape
    ones = jnp.ones((width, width), BF16)
    return _dot(x.reshape(h * rows, width).astype(BF16), ones).reshape(h, rows, width)


def _causal_conv_silu(win, cw, k_w):
    assert k_w == 4
    tiled = (win.shape[0] // SUBLANES, SUBLANES, win.shape[1])

    def pair(x, x1, j):
        return (x.reshape(tiled) * cw[j][None] + x1.reshape(tiled) * cw[j - 1][None]).reshape(win.shape)

    win1 = pltpu.roll(win, 1, 0)
    acc = pair(win, win1, 3) + pltpu.roll(pair(win, win1, 1), 2, 0)
    return _silu(acc[SUBLANES:, :])


def _gdn_kernel(q_ref, k_ref, v_ref, z_ref, ba_ref, cw_ref, hp_ref, o_ref,
                s_ref, tail_ref, wq_ref, u_ref, ik_ref, dec_ref):
    nb, tb = q_ref.shape[0], q_ref.shape[1]
    kd = N_HEADS * HEAD_D
    nbh = nb * N_HEADS

    @pl.when(pl.program_id(0) == 0)
    def _():
        s_ref[...] = jnp.zeros(s_ref.shape, F32)
        tail_ref[...] = jnp.zeros(tail_ref.shape, F32)

    row = lax.broadcasted_iota(jnp.int32, (CHUNK, CHUNK), 0)
    col = lax.broadcasted_iota(jnp.int32, (CHUNK, CHUNK), 1)
    causal = row >= col
    strict = row > col
    tril = jnp.where(causal, 1.0, 0.0).astype(F32)
    a_log = hp_ref[0:1, :]
    dt_bias = hp_ref[1:2, :]
    onorm_g = hp_ref[2:3, :]
    zeros_half = jnp.zeros((CHUNK, HEAD_D), F32)

    def precompute(cp, carry):
        qs, ks, vs, betas, gcs, grs, gls = [], [], [], [], [], [], []
        for sub in range(PRE_CHUNKS):
            base = pl.multiple_of((cp * PRE_CHUNKS + sub) * CHUNK, CHUNK)
            prev = pl.multiple_of(base - BF16_ROWS, BF16_ROWS)
            for b in range(nb):
                ba = ba_ref[b, pl.ds(base, CHUNK), :]
                beta_all = _sigmoid(ba)
                g_all = -jnp.exp(a_log) * _softplus(ba + dt_bias)
                gcum = _dot_hi(tril, g_all)
                gcum_t = jnp.concatenate([gcum, gcum], axis=0).T
                for h in range(N_HEADS):
                    lo, hi = h * HEAD_D, (h + 1) * HEAD_D

                    def conv(ref, off):
                        cur = ref[b, pl.ds(base, CHUNK), lo:hi].astype(F32)
                        if sub == 0:
                            before = tail_ref[b, :, off + lo:off + hi]
                        else:
                            before = ref[b, pl.ds(prev, BF16_ROWS), lo:hi].astype(F32)[SUBLANES:]
                        win = jnp.concatenate([before, cur], axis=0)
                        return _causal_conv_silu(win, cw_ref[:, :, off + lo:off + hi], QKV_CONV)

                    qs.append(conv(q_ref, 0))
                    ks.append(conv(k_ref, kd))
                    vs.append(conv(v_ref, 2 * kd))
                    betas.append(beta_all[:, h:h + 1])
                    gcs.append(gcum[:, N_HEADS + h:N_HEADS + h + 1])
                    grs.append(gcum_t[N_HEADS + h:N_HEADS + h + 1, 0:CHUNK])
                    gls.append(gcum[CHUNK - 1:CHUNK, N_HEADS + h:N_HEADS + h + 1])
        q, k, v = jnp.stack(qs), jnp.stack(ks), jnp.stack(vs)
        beta, gc, gr, gl = jnp.stack(betas), jnp.stack(gcs), jnp.stack(grs), jnp.stack(gls)
        qn = q * (lax.rsqrt(_lane_sums(q * q) + EPS) * (HEAD_D ** -0.5))
        kn = k * lax.rsqrt(_lane_sums(k * k) + EPS)
        decay = jnp.where(causal, jnp.exp(jnp.where(causal, gc - gr, 0.0)), 0.0)
        kb = kn * beta
        e_gc = jnp.exp(gc)
        kq = _bmm_nt(jnp.concatenate([kb, qn], axis=1), kn)
        a = jnp.where(strict, kq[:, :CHUNK] * decay, 0.0)
        intra = kq[:, CHUNK:] * decay
        uw = _bmm(_unit_lower_inverse(a), jnp.concatenate([v * beta, kb * e_gc], axis=2))
        wq = jnp.concatenate([uw[:, :, HEAD_D:], qn * e_gc], axis=1).astype(BF16)
        k_dec = kn * jnp.exp(gl - gc)
        k_dec_t = jnp.stack([jnp.concatenate([k_dec[i], zeros_half], axis=0).T[:, :CHUNK]
                             for i in range(PRE_CHUNKS * nbh)])
        ik = jnp.concatenate([intra, k_dec_t], axis=1).astype(BF16)
        dec = jnp.broadcast_to(jnp.exp(gl), (PRE_CHUNKS * nbh, 1, HEAD_D))
        for sub in range(PRE_CHUNKS):
            c = cp * PRE_CHUNKS + sub
            rows = slice(sub * nbh, (sub + 1) * nbh)
            u_ref[c] = uw[rows, :, :HEAD_D]
            wq_ref[c] = wq[rows]
            ik_ref[c] = ik[rows]
            dec_ref[c] = dec[rows]

        last = pl.multiple_of((cp + 1) * PRE_CHUNKS * CHUNK - BF16_ROWS, BF16_ROWS)
        for b in range(nb):
            for j, ref in enumerate((q_ref, k_ref, v_ref)):
                rows = ref[b, pl.ds(last, BF16_ROWS), :].astype(F32)
                tail_ref[b, :, j * kd:(j + 1) * kd] = rows[BF16_ROWS - SUBLANES:, :]
        return carry

    def recur(c, carry):
        base = pl.multiple_of(c * CHUNK, CHUNK)
        state = s_ref[...]
        ws = _bmm(wq_ref[c], state)
        v_new = u_ref[c] - ws[:, :CHUNK]
        r = _bmm(ik_ref[c], v_new)
        o = ws[:, CHUNK:] + r[:, :CHUNK]
        s_ref[...] = state * dec_ref[c] + r[:, CHUNK:]
        on = o * lax.rsqrt(jnp.mean(o * o, axis=-1, keepdims=True) + EPS) * onorm_g
        for b in range(nb):
            for h in range(N_HEADS):
                lo, hi = h * HEAD_D, (h + 1) * HEAD_D
                z = z_ref[b, pl.ds(base, CHUNK), lo:hi].astype(F32)
                o_ref[b, pl.ds(base, CHUNK), lo:hi] = (on[b * N_HEADS + h] * _silu(z)).astype(BF16)
        return carry

    lax.fori_loop(0, tb // (PRE_CHUNKS * CHUNK), precompute, 0)
    lax.fori_loop(0, tb // CHUNK, recur, 0)


def _gdn(proj, ba, conv_w, head_params, batch, seq, tb):
    kd = N_HEADS * HEAD_D
    nc = tb // CHUNK
    nbh = batch * N_HEADS
    proj3 = proj.reshape(batch, seq, proj.shape[1])
    ba3 = ba.reshape(batch, seq, LANES)

    def col(j):
        return pl.BlockSpec((batch, tb, kd), lambda t: (0, t, j))

    out = pl.pallas_call(
        _gdn_kernel,
        grid=(seq // tb,),
        in_specs=[
            col(0), col(1), col(2), col(3),
            pl.BlockSpec((batch, tb, LANES), lambda t: (0, t, 0)),
            pl.BlockSpec((QKV_CONV, SUBLANES, 3 * kd), lambda t: (0, 0, 0)),
            pl.BlockSpec((SUBLANES, LANES), lambda t: (0, 0)),
        ],
        out_specs=pl.BlockSpec((batch, tb, kd), lambda t: (0, t, 0)),
        out_shape=jax.ShapeDtypeStruct((batch, seq, kd), BF16),
        scratch_shapes=[
            pltpu.VMEM((nbh, HEAD_D, HEAD_D), F32),
            pltpu.VMEM((batch, SUBLANES, 3 * kd), F32),
            pltpu.VMEM((nc, nbh, 2 * CHUNK, HEAD_D), BF16),
            pltpu.VMEM((nc, nbh, CHUNK, HEAD_D), F32),
            pltpu.VMEM((nc, nbh, CHUNK + HEAD_D, CHUNK), BF16),
            pltpu.VMEM((nc, nbh, 1, HEAD_D), F32),
        ],
        compiler_params=_params("gdn"),
        name="gdn",
    )(proj3, proj3, proj3, proj3, ba3, conv_w, head_params)
    return out.reshape(batch * seq, kd)


def _post_kernel(x_ref, og_ref, sb_ref, sc_ref, sx_ref, ga_ref, gb_ref, mod_ref, n2g_ref, cw_ref,
                 wpa_ref, wpb_ref, wout_ref, wrh_ref, wrl_ref, br_ref,
                 x1_ref, h2_ref, route_ref, cnt_ref, win_ref, run_ref, *, per_batch):
    tm = x_ref.shape[0]
    i = pl.program_id(0)

    @pl.when(i == 0)
    def _():
        run_ref[...] = jnp.zeros(run_ref.shape, F32)

    @pl.when(i % per_batch == 0)
    def _():
        win_ref[...] = jnp.zeros(win_ref.shape, F32)

    assert SC_CONV == 3
    prod = sc_ref[...].astype(F32) * sx_ref[...].astype(F32)
    win = jnp.concatenate([win_ref[...], prod], axis=0)
    win_ref[...] = prod[tm - SUBLANES:, :]
    tiled = (win.shape[0] // SUBLANES, SUBLANES, win.shape[1])
    win1 = pltpu.roll(win, 1, 0)
    conv = (win.reshape(tiled) * cw_ref[2][None] + win1.reshape(tiled) * cw_ref[1][None]
            + pltpu.roll(win1, 1, 0).reshape(tiled) * cw_ref[0][None]).reshape(win.shape)[SUBLANES:, :]
    y_b = _dot((sb_ref[...].astype(F32) * conv).astype(BF16), wpb_ref[...])
    y_a = _dot(og_ref[...], wpa_ref[...])
    merged = _sigmoid(ga_ref[...].astype(F32)) * y_a + _sigmoid(gb_ref[...].astype(F32)) * y_b
    mix = _dot(merged.astype(BF16), wout_ref[...])
    x1 = x_ref[...] + mod_ref[0:1, :] * mix
    x1_ref[...] = x1

    y = x1 * lax.rsqrt(jnp.mean(x1 * x1, axis=-1, keepdims=True) + EPS)
    h2 = (y * n2g_ref[...]) * (1.0 + mod_ref[1:2, :]) + mod_ref[2:3, :]
    _write_rows(h2_ref, h2)

    lg = _dot_split(h2, wrh_ref[...], wrl_ref[...]) + br_ref[...]
    lane = lax.broadcasted_iota(jnp.int32, lg.shape, 1).astype(F32)
    neg = jnp.float32(-jnp.inf)
    big = jnp.float32(2 * LANES)

    def first_max(mask):
        vmax = jnp.max(jnp.where(mask, lg, neg), axis=-1, keepdims=True)
        idx = jnp.min(jnp.where(mask & (lg == vmax), lane, big), axis=-1, keepdims=True)
        return vmax, idx

    gmask = lane < N_GROUPS
    g_max, g_sel = first_max(gmask)
    p_group = 1.0 / jnp.sum(jnp.where(gmask, jnp.exp(lg - g_max), 0.0), axis=-1, keepdims=True)
    e_lo = N_GROUPS + EXPERTS_PER_GROUP * g_sel
    emask = (lane >= e_lo) & (lane < e_lo + EXPERTS_PER_GROUP)
    v1, i1 = first_max(emask)
    v2, i2 = first_max(emask & (lane != i1))
    ex = jnp.exp(v2 - v1)
    w1 = p_group * (1.0 / (1.0 + ex))
    w2 = p_group * (ex / (1.0 + ex))
    e1 = i1 - N_GROUPS
    e2 = i2 - N_GROUPS

    onehot = jnp.where((lane == e1) | (lane == e2), 1.0, 0.0).astype(F32)
    row = lax.broadcasted_iota(jnp.int32, (tm, tm), 0)
    col = lax.broadcasted_iota(jnp.int32, (tm, tm), 1)
    before = jnp.where(row > col, 1.0, 0.0).astype(BF16)
    seen = _dot(before, onehot.astype(BF16)) + run_ref[0:1, :]
    r1 = jnp.sum(jnp.where(lane == e1, seen, 0.0), axis=-1, keepdims=True)
    r2 = jnp.sum(jnp.where(lane == e2, seen, 0.0), axis=-1, keepdims=True)
    run_ref[0:1, :] = run_ref[0:1, :] + jnp.sum(onehot, axis=0, keepdims=True)
    cnt_ref[...] = jnp.broadcast_to(run_ref[0:1, :], cnt_ref.shape)

    out = jnp.where(lane == 0, e1, 0.0)
    out = jnp.where(lane == 1, e2, out)
    out = jnp.where(lane == 2, w1, out)
    out = jnp.where(lane == 3, w2, out)
    out = jnp.where(lane == 4, r1, out)
    out = jnp.where(lane == 5, r2, out)
    route_ref[...] = out


def _post(x2, og, proj, mod, n2g, conv_w, wpa, wpb, wout, w_route_hi, w_route_lo, b_route, seq, tm):
    m, d = x2.shape
    assert d == SUBLANES * LANES
    per_batch = seq // tm

    def rows(j):
        return pl.BlockSpec((tm, d), lambda i: (i, j))

    def whole(shape):
        return pl.BlockSpec(shape, lambda i: tuple(0 for _ in shape))

    return pl.pallas_call(
        functools.partial(_post_kernel, per_batch=per_batch),
        grid=(m // tm,),
        in_specs=[
            rows(0), rows(0), rows(4), rows(5), rows(6), rows(7), rows(8),
            pl.BlockSpec((None, SUBLANES, d), lambda i: (i // per_batch, 0, 0)),
            whole((1, d)), whole((SC_CONV, SUBLANES, d)),
            whole((d, d)), whole((d, d)), whole((d, d)),
            whole((d, LANES)), whole((d, LANES)), whole((1, LANES)),
        ],
        out_specs=[
            rows(0), pl.BlockSpec((tm * SUBLANES, LANES), lambda i: (i, 0)),
            pl.BlockSpec((tm, LANES), lambda i: (i, 0)),
            pl.BlockSpec((SUBLANES, LANES), lambda i: (0, 0)),
        ],
        out_shape=[
            jax.ShapeDtypeStruct((m, d), F32),
            jax.ShapeDtypeStruct((m * SUBLANES, LANES), F32),
            jax.ShapeDtypeStruct((m, LANES), F32),
            jax.ShapeDtypeStruct((SUBLANES, LANES), F32),
        ],
        scratch_shapes=[
            pltpu.VMEM((SUBLANES, d), F32),
            pltpu.VMEM((SUBLANES, LANES), F32),
        ],
        compiler_params=_params("post"),
        name="post",
    )(x2, og, proj, proj, proj, proj, proj, mod, n2g, conv_w, wpa, wpb, wout, w_route_hi, w_route_lo,
      b_route)


def _plan_kernel(route_ref, cnt_ref, d_ref):
    bm = EXPERT_BLOCK
    sizes = cnt_ref[...]
    padded = jnp.floor((sizes + (bm - 1.0)) * (1.0 / bm)) * bm
    lane_i = lax.broadcasted_iota(jnp.int32, sizes.shape, 1)
    incl = padded
    s = 1
    while s < LANES:
        incl = incl + jnp.where(lane_i >= s, pltpu.roll(incl, s, 1), 0.0)
        s *= 2
    start = (incl - padded)[0:1, :]
    r = route_ref[...]
    lane = lax.broadcasted_iota(jnp.int32, r.shape, 1).astype(F32)
    d1 = jnp.sum(jnp.where(lane == r[:, 0:1], start, 0.0), axis=-1, keepdims=True) + r[:, 4:5]
    d2 = jnp.sum(jnp.where(lane == r[:, 1:2], start, 0.0), axis=-1, keepdims=True) + r[:, 5:6]
    out = jnp.where(lane == 0.0, d1, jnp.where(lane == 1.0, d2, 0.0))
    d_ref[...] = out.T[0:SUBLANES, :].astype(jnp.int32)


def _plan(route, counts, tm):
    m = route.shape[0]
    return pl.pallas_call(
        _plan_kernel,
        grid=(m // tm,),
        in_specs=[
            pl.BlockSpec((tm, LANES), lambda i: (i, 0)),
            pl.BlockSpec((SUBLANES, LANES), lambda i: (0, 0)),
        ],
        out_specs=pl.BlockSpec((SUBLANES, tm), lambda i: (0, i)),
        out_shape=jax.ShapeDtypeStruct((SUBLANES, m), jnp.int32),
        compiler_params=_params("plan"),
        name="plan",
    )(route, counts)


def _row_copy(src_ref, src_row, dst_ref, dst_row, sem):
    return pltpu.make_async_copy(_row_tile(src_ref, src_row), _row_tile(dst_ref, dst_row), sem)


def _rows_copy(src_ref, dst_ref, dst_row, n, sem):
    return pltpu.make_async_copy(src_ref, dst_ref.at[pl.ds(dst_row * SUBLANES, n * SUBLANES), :], sem)


def _dispatch_kernel(d1_ref, d2_ref, pad_ref, h2_ref, xs_ref, zero_ref, sem):
    tm = h2_ref.shape[0] // SUBLANES
    bm = zero_ref.shape[0] // SUBLANES
    i = pl.program_id(0)
    t0 = i * tm
    n_blocks = xs_ref.shape[0] // (bm * SUBLANES)

    def zero_rows(row, n):
        dst = xs_ref.at[pl.ds(pl.multiple_of(row * SUBLANES, SUBLANES), n * SUBLANES), :]
        return pltpu.make_async_copy(zero_ref.at[pl.ds(0, n * SUBLANES), :], dst, sem.at[2])

    def zero_fill(act):
        def expert(e, carry):
            row, cnt = pad_ref[e], pad_ref[N_EXPERTS + e]
            n = bm // 2
            while n >= 1:
                @pl.when((cnt & n) != 0)
                def _():
                    act(zero_rows(row + (cnt & (-2 * n)), n))
                n //= 2
            return carry

        lax.fori_loop(0, N_EXPERTS, expert, 0)
        lax.fori_loop(pad_ref[2 * N_EXPERTS], n_blocks, lambda j, c: (act(zero_rows(j * bm, bm)), c)[1], 0)

    @pl.when(i == 0)
    def _():
        zero_ref[...] = jnp.zeros(zero_ref.shape, F32)
        zero_fill(lambda copy: copy.start())

    def start(r, carry):
        _row_copy(h2_ref, r, xs_ref, d1_ref[t0 + r], sem.at[0]).start(priority=0)
        _row_copy(h2_ref, r, xs_ref, d2_ref[t0 + r], sem.at[1]).start(priority=1)
        return carry

    lax.fori_loop(0, tm, start, 0, unroll=DMA_UNROLL)
    _rows_copy(h2_ref, xs_ref, 0, tm, sem.at[0]).wait()
    _rows_copy(h2_ref, xs_ref, 0, tm, sem.at[1]).wait()

    @pl.when(i == pl.num_programs(0) - 1)
    def _():
        zero_fill(lambda copy: copy.wait())


def _dispatch(d1, d2, pad_info, h2, cap, tm):
    m = h2.shape[0] // SUBLANES
    return pl.pallas_call(
        _dispatch_kernel,
        grid_spec=pltpu.PrefetchScalarGridSpec(
            num_scalar_prefetch=3,
            grid=(m // tm,),
            in_specs=[pl.BlockSpec((tm * SUBLANES, LANES), lambda i, d1, d2, tl: (i, 0))],
            out_specs=pl.BlockSpec(memory_space=pl.ANY),
            scratch_shapes=[pltpu.VMEM((EXPERT_BLOCK * SUBLANES, LANES), F32),
                            pltpu.SemaphoreType.DMA((3,))],
        ),
        out_shape=jax.ShapeDtypeStruct((cap * SUBLANES, LANES), F32),
        compiler_params=_params("dispatch"),
        name="dispatch",
    )(d1, d2, pad_info, h2)


def _expert_kernel(be_ref, act_ref, nxt_ref, xs_hbm, w1_hbm, w3_hbm, w2_hbm, y_ref, w1b_ref, w3b_ref, w2b_ref,
                   w1s_ref, w3s_ref, w2s_ref, xbuf_ref, sem, sem_x):
    b = pl.program_id(0)
    n_active = act_ref[pl.num_programs(0)]
    weights = ((w1_hbm, w1s_ref, w1b_ref), (w3_hbm, w3s_ref, w3b_ref), (w2_hbm, w2s_ref, w2b_ref))
    n_buf, rows = xbuf_ref.shape[0], xbuf_ref.shape[1]
    ahead = n_buf - 1

    def weight_copy(k, e):
        w_hbm, ws_ref, _ = weights[k]
        return pltpu.make_async_copy(w_hbm.at[e], ws_ref, sem.at[k])

    def x_copy(blk):
        slot = lax.rem(blk, n_buf)
        src = xs_hbm.at[pl.ds(pl.multiple_of(blk * rows, rows), rows), :]
        return pltpu.make_async_copy(src, xbuf_ref.at[slot], sem_x.at[slot])

    @pl.when(act_ref[b] > 0)
    def _():
        @pl.when(b == 0)
        def _():
            for k in range(len(weights)):
                weight_copy(k, be_ref[b]).start()
            for a in range(ahead):
                @pl.when(a < n_active)
                def _():
                    x_copy(jnp.int32(a)).start(priority=1)

        @pl.when(b + ahead < n_active)
        def _():
            x_copy(b + ahead).start(priority=1)

        @pl.when((b == 0) | (be_ref[b] != be_ref[jnp.maximum(b - 1, 0)]))
        def _():
            for k, (_, ws_ref, wb_ref) in enumerate(weights):
                weight_copy(k, be_ref[b]).wait()
                wb_ref[...] = ws_ref[...].astype(BF16)

                @pl.when(nxt_ref[b] >= 0)
                def _():
                    weight_copy(k, nxt_ref[b]).start()

        x_copy(b).wait()
        xb = _read_rows(xbuf_ref.at[lax.rem(b, n_buf)], rows // SUBLANES).astype(BF16)
        hid = _silu(_dot(xb, w1b_ref[...])) * _dot(xb, w3b_ref[...])
        _write_rows(y_ref, _dot(hid.astype(BF16), w2b_ref[...]))

    @pl.when(act_ref[b] == 0)
    def _():
        y_ref[...] = jnp.zeros(y_ref.shape, F32)


def _experts(block_expert, block_active, block_next, xs, w1, w3, w2):
    cap = xs.shape[0] // SUBLANES
    d, de = w1.shape[1], w1.shape[2]
    bm = EXPERT_BLOCK

    return pl.pallas_call(
        _expert_kernel,
        grid_spec=pltpu.PrefetchScalarGridSpec(
            num_scalar_prefetch=3,
            grid=(cap // bm,),
            in_specs=[
                pl.BlockSpec(memory_space=pl.ANY),
                pl.BlockSpec(memory_space=pl.ANY),
                pl.BlockSpec(memory_space=pl.ANY),
                pl.BlockSpec(memory_space=pl.ANY),
            ],
            out_specs=pl.BlockSpec((bm * SUBLANES, LANES), lambda b, be, act, nxt: (b, 0)),
            scratch_shapes=[pltpu.VMEM((d, de), BF16), pltpu.VMEM((d, de), BF16), pltpu.VMEM((de, d), BF16),
                            pltpu.VMEM((d, de), F32), pltpu.VMEM((d, de), F32), pltpu.VMEM((de, d), F32),
                            pltpu.VMEM((EXPERT_X_BUFFERS, bm * SUBLANES, LANES), F32),
                            pltpu.SemaphoreType.DMA((3,)), pltpu.SemaphoreType.DMA((EXPERT_X_BUFFERS,))],
        ),
        out_shape=jax.ShapeDtypeStruct((cap * SUBLANES, LANES), F32),
        compiler_params=_params("experts"),
        name="experts",
    )(block_expert, block_active, block_next, xs, w1, w3, w2)


def _final_kernel(d1_ref, d2_ref, x1_ref, route_ref, gt_ref, nfg_ref, ys_ref, o_ref, buf_ref, sem):
    tm = x1_ref.shape[0]
    i = pl.program_id(0)

    def gather(tile, s):
        t0 = tile * tm

        def start(r, carry):
            _row_copy(ys_ref, d1_ref[t0 + r], buf_ref.at[s, 0], r, sem.at[s, 0]).start(priority=0)
            _row_copy(ys_ref, d2_ref[t0 + r], buf_ref.at[s, 1], r, sem.at[s, 1]).start(priority=1)
            return carry

        lax.fori_loop(0, tm, start, 0, unroll=DMA_UNROLL)

    def wait_rows(s):
        for k in range(2):
            pltpu.make_async_copy(ys_ref.at[pl.ds(0, tm * SUBLANES), :], buf_ref.at[s, k],
                                  sem.at[s, k]).wait()

    def combine(cur, nxt):
        wait_rows(cur)
        last = pl.num_programs(0) - 1
        t0 = jnp.minimum(i + 1, last) * tm
        for r in range(tm):
            _row_copy(ys_ref, d1_ref[t0 + r], buf_ref.at[nxt, 0], r, sem.at[nxt, 0]).start(priority=0)
            _row_copy(ys_ref, d2_ref[t0 + r], buf_ref.at[nxt, 1], r, sem.at[nxt, 1]).start(priority=1)
        route = route_ref[...]
        moe = (_read_rows(buf_ref.at[cur, 0], tm) * route[:, 2:3]
               + _read_rows(buf_ref.at[cur, 1], tm) * route[:, 3:4])
        x2 = x1_ref[...] + gt_ref[...] * moe
        y = x2 * lax.rsqrt(jnp.mean(x2 * x2, axis=-1, keepdims=True) + EPS)
        o_ref[...] = y * nfg_ref[...]

        @pl.when(i == last)
        def _():
            wait_rows(nxt)

    @pl.when(i == 0)
    def _():
        gather(0, 0)

    @pl.when(i % 2 == 0)
    def _():
        combine(0, 1)

    @pl.when(i % 2 == 1)
    def _():
        combine(1, 0)


def _final(d1, d2, x1, route, gt2, nfg, ys, seq, tm):
    m, d = x1.shape
    per_batch = seq // tm
    return pl.pallas_call(
        _final_kernel,
        grid_spec=pltpu.PrefetchScalarGridSpec(
            num_scalar_prefetch=2,
            grid=(m // tm,),
            in_specs=[
                pl.BlockSpec((tm, d), lambda i, d1, d2: (i, 0)),
                pl.BlockSpec((tm, LANES), lambda i, d1, d2: (i, 0)),
                pl.BlockSpec((None, 1, d), lambda i, d1, d2: (i // per_batch, 0, 0)),
                pl.BlockSpec((1, d), lambda i, d1, d2: (0, 0)),
                pl.BlockSpec(memory_space=pl.ANY),
            ],
            out_specs=pl.BlockSpec((tm, d), lambda i, d1, d2: (i, 0)),
            scratch_shapes=[pltpu.VMEM((2, 2, tm * SUBLANES, LANES), F32),
                            pltpu.SemaphoreType.DMA((2, 2))],
        ),
        out_shape=jax.ShapeDtypeStruct((m, d), F32),
        compiler_params=_params("final"),
        name="final",
    )(d1, d2, x1, route, gt2, nfg, ys)


def _tile(n, pref):
    t = min(n, pref)
    assert n % t == 0
    return t


def kernel(x, c, w_ada, b_ada, norm1_g, w_in, conv_qkv_w, a_log, dt_bias, onorm_g, w_proj_a,
           conv_sc_w, w_proj_b, w_out, norm2_g, w_group, b_group, w_expert, b_expert, w1, w3, w2,
           normf_g):
    batch, seq, d = x.shape
    depth = w_ada.shape[0]
    m = batch * seq
    kd = N_HEADS * HEAD_D
    assert d == kd and seq % CHUNK == 0 and batch <= SUBLANES
    assert depth == 1, "the last stage applies the final rmsnorm: one layer only"
    x2 = x.reshape(m, d)

    for l in range(depth):
        mod = _ada(c, w_ada[l], b_ada[l][None, :])[:batch]
        sh1, sc1, gt1, sh2, sc2, gt2 = [mod[:, None, j * d:(j + 1) * d] for j in range(6)]

        w = w_in[l]
        o_ba = 3 * kd + kd
        proj, ba = _inproj(x2, norm1_g[l][None, :], sc1, sh1, jnp.swapaxes(w, 0, 1), o_ba // d, 2 * N_HEADS,
                           seq, _tile(seq, INPROJ_ROWS))

        head_params = jnp.zeros((SUBLANES, LANES), F32)
        head_params = head_params.at[0, N_HEADS:2 * N_HEADS].set(a_log[l])
        head_params = head_params.at[1, N_HEADS:2 * N_HEADS].set(dt_bias[l])
        head_params = head_params.at[2, :].set(onorm_g[l])
        conv_taps = jnp.broadcast_to(conv_qkv_w[l][:, None, :], (QKV_CONV, SUBLANES, 3 * kd))
        og = _gdn(proj, ba, conv_taps, head_params, batch, seq, _tile(seq, GDN_ROWS))

        mod_post = jnp.zeros((batch, SUBLANES, d), F32)
        mod_post = mod_post.at[:, 0:1].set(gt1).at[:, 1:2].set(sc2).at[:, 2:3].set(sh2)
        w_route = jnp.zeros((d, LANES), F32)
        w_route = w_route.at[:, :N_GROUPS].set(w_group[l]).at[:, N_GROUPS:N_GROUPS + N_EXPERTS].set(w_expert[l])
        w_route_hi = w_route.astype(BF16)
        w_route_lo = (w_route - w_route_hi.astype(F32)).astype(BF16)
        b_route = jnp.zeros((1, LANES), F32)
        b_route = b_route.at[0, :N_GROUPS].set(b_group[l]).at[0, N_GROUPS:N_GROUPS + N_EXPERTS].set(b_expert[l])
        x1, h2, route, counts = _post(
            x2, og, proj, mod_post, norm2_g[l][None, :],
            jnp.broadcast_to(conv_sc_w[l][:, None, :], (SC_CONV, SUBLANES, d)),
            w_proj_a[l].astype(BF16), w_proj_b[l].astype(BF16), w_out[l].astype(BF16),
            w_route_hi, w_route_lo, b_route, seq, _tile(seq, POST_ROWS))

        bm = EXPERT_BLOCK
        n_blocks = (2 * m) // bm + N_EXPERTS
        sizes = counts[0, :N_EXPERTS].astype(jnp.int32)
        padded = ((sizes + bm - 1) // bm) * bm
        pad_end = jnp.cumsum(padded)
        pad_info = jnp.concatenate([pad_end - padded + sizes, padded - sizes,
                                    pad_end[-1:] // bm]).astype(jnp.int32)
        block_row = jnp.arange(n_blocks, dtype=jnp.int32) * bm
        block_expert = jnp.minimum(jnp.sum(block_row[:, None] >= pad_end[None, :], axis=1),
                                   N_EXPERTS - 1).astype(jnp.int32)
        block_active = jnp.concatenate([(block_row < pad_end[-1]).astype(jnp.int32),
                                        (pad_end[-1:] // bm).astype(jnp.int32)])
        expert_id = jnp.arange(N_EXPERTS, dtype=jnp.int32)
        later = (expert_id[None, :] > expert_id[:, None]) & (padded[None, :] > 0)
        next_expert = jnp.min(jnp.where(later, expert_id[None, :], N_EXPERTS), axis=1)
        block_next = jnp.where(next_expert < N_EXPERTS, next_expert, -1)[block_expert].astype(jnp.int32)
        dest = _plan(route, counts, _tile(seq, INPROJ_ROWS))
        d1, d2 = dest[0], dest[1]

        xs = _dispatch(d1, d2, pad_info, h2, n_blocks * bm, _tile(seq, DISPATCH_ROWS))
        ys = _experts(block_expert, block_active, block_next, xs, w1[l], w3[l], w2[l])
        nfg = normf_g[None, :]
        x2 = _final(d1, d2, x1, route, gt2, nfg, ys, seq, _tile(seq, FINAL_ROWS))
    return x2.reshape(batch, seq, d)
```
